```python
import jax, jax.numpy as jnp
from jax import lax
import numpy as np

D_MODEL = 1024
BATCH = 8
SEQ = 2048
DEPTH = 1
DEC_BATCH = 128
DEC_SEQ = 8
PAST_LEN = 16384
PAGE_SIZE = 128

D_MIX = D_MODEL
H_A = 4
DK_A = D_MIX // 2 // H_A
DV_A = D_MIX // 2 // H_A
H_B = 4
DV_B = D_MIX // 2 // H_B
DK_B = DV_B // 2
CONV_W = 4
CONV_CH = H_A * (2 * DK_A + DV_A)
CHUNK = 64
N_EXPERTS = 32
TOP_K = 4
D_FF = D_MODEL
SWIGLU_LIMIT = 7.0
SWIGLU_ALPHA = 1.702
MOE_BLOCK = 128
PLE_DIM = 256
EPS = 1e-6
NEG = -1e30
SPLITS = (CONV_CH, H_A * DV_A, H_A, H_A,
          H_B * DK_B, H_B * DK_B, H_B * DV_B, H_B * DV_B, H_B, H_B)
IN_W = sum(SPLITS)

kernel_name = "hymba_gdn_mlstm_moe_ple_step"


def _split(a, sizes):
    return jnp.split(a, [int(s) for s in np.cumsum(sizes)[:-1]], axis=-1)


def _rms(x, g):
    x32 = x.astype(jnp.float32)
    return x32 * lax.rsqrt(jnp.mean(x32 * x32, axis=-1, keepdims=True) + EPS) * g.astype(jnp.float32)


def _l2norm(x):
    return x * lax.rsqrt(jnp.sum(x * x, axis=-1, keepdims=True) + EPS)


def _heads(a, h):
    b, t, _ = a.shape
    return a.reshape(b, t, h, -1).transpose(0, 2, 1, 3)


def _merge(a):
    b, h, t, d = a.shape
    return a.transpose(0, 2, 1, 3).reshape(b, t, h * d)


def _pad_chunks(a, n_c, L, value):
    pad = n_c * L - a.shape[2]
    a = jnp.pad(a, [(0, 0), (0, 0), (0, pad)] + [(0, 0)] * (a.ndim - 3), constant_values=value)
    return a.reshape(a.shape[0], a.shape[1], n_c, L, *a.shape[3:])


def _gated_delta_rule(q, k, v, g, beta, S0):
    B, H, T, _ = q.shape
    L = min(T, CHUNK)
    n_c = -(-T // L)
    q, k, v, g, beta = (_pad_chunks(a, n_c, L, 0.0) for a in (q, k, v, g, beta))
    tril = jnp.tril(jnp.ones((L, L), bool))
    strict = jnp.tril(jnp.ones((L, L), bool), -1)
    gc = jnp.cumsum(g, axis=-1)
    diff = gc[..., :, None] - gc[..., None, :]
    decay = jnp.where(tril, jnp.exp(jnp.where(tril, diff, 0.0)), 0.0)
    kb = k * beta[..., None]
    a_mat = jnp.where(strict, jnp.einsum('bhcid,bhcjd->bhcij', kb, k) * decay, 0.0)
    lhs = a_mat + jnp.eye(L, dtype=a_mat.dtype)
    rhs = jnp.concatenate([v * beta[..., None], kb * jnp.exp(gc)[..., None]], axis=-1)
    sol = lax.linalg.triangular_solve(lhs, rhs, left_side=True, lower=True, unit_diagonal=True)
    dv = v.shape[-1]
    u, w = sol[..., :dv], sol[..., dv:]
    qk = jnp.einsum('bhcid,bhcjd->bhcij', q, k) * decay
    qg = q * jnp.exp(gc)[..., None]
    kd = k * jnp.exp(gc[..., -1:] - gc)[..., None]
    g_last = jnp.exp(gc[..., -1])

    def step(S, xs):
        u_c, w_c, qk_c, qg_c, kd_c, gl_c = xs
        v_new = u_c - jnp.einsum('bhld,bhdv->bhlv', w_c, S)
        o_c = jnp.einsum('bhld,bhdv->bhlv', qg_c, S) + jnp.einsum('bhij,bhjv->bhiv', qk_c, v_new)
        S = S * gl_c[..., None, None] + jnp.einsum('bhld,bhlv->bhdv', kd_c, v_new)
        return S, o_c

    xs = tuple(jnp.moveaxis(a, 2, 0) for a in (u, w, qk, qg, kd, g_last))
    S, o = lax.scan(step, S0, xs)
    o = jnp.moveaxis(o, 0, 2).reshape(B, H, n_c * L, dv)[:, :, :T]
    return o, S


def _mlstm_chunked(q, k, v, ig, lf, C0, n0, m0):
    B, H, T, _ = q.shape
    L = min(T, CHUNK)
    n_c = -(-T // L)
    q, k, v, lf = (_pad_chunks(a, n_c, L, 0.0) for a in (q, k, v, lf))
    ig = _pad_chunks(ig, n_c, L, NEG)
    tril = jnp.tril(jnp.ones((L, L), bool))

    def step(carry, xs):
        C, n, m = carry
        q_c, k_c, v_c, i_c, f_c = xs
        b = jnp.cumsum(f_c, axis=-1)
        d_log = jnp.where(tril, b[..., :, None] - b[..., None, :] + i_c[..., None, :], NEG)
        inter = b + m[..., None]
        m_t = jnp.maximum(inter, d_log.max(axis=-1))
        s = jnp.einsum('bhid,bhjd->bhij', q_c, k_c) * jnp.exp(d_log - m_t[..., None])
        e_inter = jnp.exp(inter - m_t)
        num = e_inter[..., None] * jnp.einsum('bhld,bhdv->bhlv', q_c, C) + jnp.einsum('bhij,bhjv->bhiv', s, v_c)
        den = e_inter * jnp.einsum('bhld,bhd->bhl', q_c, n) + s.sum(axis=-1)
        h = num / jnp.maximum(jnp.abs(den), jnp.exp(-m_t))[..., None]
        m_new = m_t[..., -1]
        w_last = jnp.exp(b[..., -1:] - b + i_c - m_new[..., None])
        f_tot = jnp.exp(b[..., -1] + m - m_new)
        C = f_tot[..., None, None] * C + jnp.einsum('bhl,bhld,bhlv->bhdv', w_last, k_c, v_c)
        n = f_tot[..., None] * n + jnp.einsum('bhl,bhld->bhd', w_last, k_c)
        return (C, n, m_new), h

    xs = tuple(jnp.moveaxis(a, 2, 0) for a in (q, k, v, ig, lf))
    (C, n, m), h = lax.scan(step, (C0, n0, m0), xs)
    h = jnp.moveaxis(h, 0, 2).reshape(B, H, n_c * L, -1)[:, :, :T]
    return h, C, n, m


def _mixers(h, conv_buf, S0, C0, n0, m0, w_in, conv_w, gdn_a_log, gdn_dt_bias, gdn_norm_g,
            mlstm_i_bias, mlstm_f_bias, mlstm_norm_g):
    f32 = jnp.float32
    B, T, _ = h.shape
    proj = (h @ w_in).astype(f32)
    conv_in, z_a, a_a, b_a, q_b, k_b, v_b, o_b, i_b, f_b = _split(proj, SPLITS)
    xc = jnp.concatenate([conv_buf.astype(f32), conv_in], axis=1)
    cw = conv_w.astype(f32)
    conv = sum(xc[:, j:j + T] * cw[j] for j in range(CONV_W))
    new_conv = xc[:, T:]
    q_a, k_a, v_a = _split(jax.nn.silu(conv), (H_A * DK_A, H_A * DK_A, H_A * DV_A))
    q_a = _l2norm(_heads(q_a, H_A)) * DK_A ** -0.5
    k_a = _l2norm(_heads(k_a, H_A))
    v_a = _heads(v_a, H_A)
    g_a = -jnp.exp(gdn_a_log.astype(f32)) * jax.nn.softplus(a_a + gdn_dt_bias.astype(f32))
    beta_a = jax.nn.sigmoid(b_a)
    o_a, S = _gated_delta_rule(q_a, k_a, v_a, g_a.transpose(0, 2, 1), beta_a.transpose(0, 2, 1),
                               S0.astype(f32))
    o_a = _rms(o_a, gdn_norm_g) * jax.nn.silu(_heads(z_a, H_A))
    q_b = _heads(q_b, H_B) * DK_B ** -0.5
    k_b = _heads(k_b, H_B)
    v_b = _heads(v_b, H_B)
    ig = (i_b + mlstm_i_bias.astype(f32)).transpose(0, 2, 1)
    lf = jax.nn.log_sigmoid(f_b + mlstm_f_bias.astype(f32)).transpose(0, 2, 1)
    h_b, C, n, m = _mlstm_chunked(q_b, k_b, v_b, ig, lf, C0.astype(f32), n0.astype(f32), m0.astype(f32))
    h_b = _rms(h_b, mlstm_norm_g.reshape(H_B, 1, DV_B)) * jax.nn.sigmoid(_heads(o_b, H_B))
    mix = jnp.concatenate([_merge(o_a), _merge(h_b)], axis=-1).astype(h.dtype)
    return mix, new_conv, S, C, n, m


def _moe(x, router_w, router_b, w_gu, b_gu, w_down, b_down):
    B, T, D = x.shape
    n_tok = B * T
    xt = x.reshape(n_tok, D)
    logits = (xt @ router_w + router_b).astype(jnp.float32)
    top_val, top_idx = lax.top_k(logits, TOP_K)
    gates = jax.nn.softmax(top_val, axis=-1)
    n_slot = n_tok * TOP_K
    flat_e = top_idx.reshape(n_slot)
    flat_tok = jnp.repeat(jnp.arange(n_tok, dtype=jnp.int32), TOP_K)
    flat_g = gates.reshape(n_slot)
    counts = jnp.bincount(flat_e, length=N_EXPERTS)
    padded = (counts + MOE_BLOCK - 1) // MOE_BLOCK * MOE_BLOCK
    pad_end = jnp.cumsum(padded)
    pad_start = pad_end - padded
    start = jnp.cumsum(counts) - counts
    order = jnp.argsort(flat_e)
    e_sorted = flat_e[order]
    dest = pad_start[e_sorted] + jnp.arange(n_slot, dtype=jnp.int32) - start[e_sorted]
    n_blocks = -(-n_slot // MOE_BLOCK) + N_EXPERTS
    n_rows = n_blocks * MOE_BLOCK
    row_tok = jnp.full((n_rows,), n_tok, jnp.int32).at[dest].set(flat_tok[order])
    row_gate = jnp.zeros((n_rows,), jnp.float32).at[dest].set(flat_g[order])
    block_e = jnp.minimum(jnp.searchsorted(pad_end, jnp.arange(n_blocks, dtype=pad_end.dtype) * MOE_BLOCK,
                                           side='right'), N_EXPERTS - 1)
    x_rows = jnp.concatenate([xt, jnp.zeros((1, D), xt.dtype)], axis=0)[row_tok].reshape(n_blocks, MOE_BLOCK, D)

    def expert_block(args):
        xb, e = args
        hgu = xb @ w_gu[e] + b_gu[e]
        gate = jnp.minimum(hgu[:, :D_FF], SWIGLU_LIMIT)
        up = jnp.clip(hgu[:, D_FF:], -SWIGLU_LIMIT, SWIGLU_LIMIT)
        act = (up + 1.0) * gate * jax.nn.sigmoid(SWIGLU_ALPHA * gate)
        return act @ w_down[e] + b_down[e]

    y_rows = lax.map(expert_block, (x_rows, block_e)).reshape(n_rows, D)
    y = jax.ops.segment_sum(y_rows * row_gate[:, None].astype(y_rows.dtype), row_tok,
                            num_segments=n_tok + 1)[:n_tok]
    return y.reshape(B, T, D).astype(x.dtype)


def _trunk(x, p, conv_st, gdn_st, c_st, n_st, m_st, params):
    (norm_attn_g, w_in, conv_w, gdn_a_log, gdn_dt_bias, gdn_norm_g, mlstm_i_bias, mlstm_f_bias,
     mlstm_norm_g, w_out, norm_moe_g, router_w, router_b, expert_w_gu, expert_b_gu, expert_w_down,
     expert_b_down, norm_ple_g, ple_gate_w, ple_w, final_norm_g) = params
    new_conv, new_gdn, new_c, new_n, new_m = [], [], [], [], []
    for i in range(DEPTH):
        hn = _rms(x, norm_attn_g[i]).astype(x.dtype)
        mix, cv, S, C, n, m = _mixers(hn, conv_st[i], gdn_st[i], c_st[i], n_st[i], m_st[i], w_in[i], conv_w[i],
                                      gdn_a_log[i], gdn_dt_bias[i], gdn_norm_g[i], mlstm_i_bias[i],
                                      mlstm_f_bias[i], mlstm_norm_g[i])
        x = x + mix @ w_out[i]
        x = x + _moe(_rms(x, norm_moe_g[i]).astype(x.dtype), router_w[i], router_b[i], expert_w_gu[i],
                     expert_b_gu[i], expert_w_down[i], expert_b_down[i])
        gate = jax.nn.sigmoid(_rms(x, norm_ple_g[i]).astype(x.dtype) @ ple_gate_w[i])
        x = x + gate * (p[i] @ ple_w[i])
        new_conv.append(cv); new_gdn.append(S); new_c.append(C); new_n.append(n); new_m.append(m)
    y = _rms(x, final_norm_g).astype(x.dtype)
    return (y, jnp.stack(new_conv), jnp.stack(new_gdn), jnp.stack(new_c), jnp.stack(new_n), jnp.stack(new_m))


def setup_inputs(seed: int = 0) -> dict:
    key = jax.random.key(seed)
    ks = jax.random.split(key, 40)
    f32 = jnp.float32

    def nrm(k, shape, scale):
        return jax.random.normal(k, shape, f32) * scale

    def gain(k, shape):
        return 1.0 + 0.02 * jax.random.normal(k, shape, f32)

    dt = jnp.exp(jax.random.uniform(ks[9], (DEPTH, H_A), f32, np.log(0.001), np.log(0.1)))
    return {
        "x_prompt": nrm(ks[0], (BATCH, SEQ, D_MODEL), 1.0),
        "x_sample": nrm(ks[1], (DEC_BATCH, DEC_SEQ, D_MODEL), 1.0),
        "p_prompt": nrm(ks[2], (DEPTH, BATCH, SEQ, PLE_DIM), 1.0),
        "p_sample": nrm(ks[3], (DEPTH, DEC_BATCH, DEC_SEQ, PLE_DIM), 1.0),
        "state_conv": nrm(ks[4], (DEPTH, DEC_BATCH, CONV_W - 1, CONV_CH), 1.0),
        "state_gdn": nrm(ks[5], (DEPTH, DEC_BATCH, H_A, DK_A, DV_A), DK_A ** -0.5),
        "state_mlstm_c": nrm(ks[6], (DEPTH, DEC_BATCH, H_B, DK_B, DV_B), 0.1),
        "state_mlstm_n": nrm(ks[7], (DEPTH, DEC_BATCH, H_B, DK_B), 0.5),
        "state_mlstm_m": nrm(ks[8], (DEPTH, DEC_BATCH, H_B), 0.5),
        "norm_attn_g": gain(ks[10], (DEPTH, D_MODEL)),
        "w_in": nrm(ks[11], (DEPTH, D_MODEL, IN_W), D_MODEL ** -0.5),
        "conv_w": nrm(ks[12], (DEPTH, CONV_W, CONV_CH), CONV_W ** -0.5),
        "gdn_a_log": jnp.log(jax.random.uniform(ks[13], (DEPTH, H_A), f32, 1.0, 16.0)),
        "gdn_dt_bias": jnp.log(jnp.expm1(dt)),
        "gdn_norm_g": gain(ks[14], (DEPTH, DV_A)),
        "mlstm_i_bias": nrm(ks[15], (DEPTH, H_B), 0.1),
        "mlstm_f_bias": jax.random.uniform(ks[16], (DEPTH, H_B), f32, 3.0, 6.0),
        "mlstm_norm_g": gain(ks[17], (DEPTH, H_B * DV_B)),
        "w_out": nrm(ks[18], (DEPTH, D_MIX, D_MODEL), D_MIX ** -0.5),
        "norm_moe_g": gain(ks[19], (DEPTH, D_MODEL)),
        "router_w": nrm(ks[20], (DEPTH, D_MODEL, N_EXPERTS), D_MODEL ** -0.5),
        "router_b": nrm(ks[21], (DEPTH, N_EXPERTS), 0.01),
        "expert_w_gu": nrm(ks[22], (DEPTH, N_EXPERTS, D_MODEL, 2 * D_FF), D_MODEL ** -0.5),
        "expert_b_gu": nrm(ks[23], (DEPTH, N_EXPERTS, 2 * D_FF), 0.01),
        "expert_w_down": nrm(ks[24], (DEPTH, N_EXPERTS, D_FF, D_MODEL), D_FF ** -0.5),
        "expert_b_down": nrm(ks[25], (DEPTH, N_EXPERTS, D_MODEL), 0.01),
        "norm_ple_g": gain(ks[26], (DEPTH, D_MODEL)),
        "ple_gate_w": nrm(ks[27], (DEPTH, D_MODEL, D_MODEL), D_MODEL ** -0.5),
        "ple_w": nrm(ks[28], (DEPTH, PLE_DIM, D_MODEL), PLE_DIM ** -0.5),
        "final_norm_g": gain(ks[29], (D_MODEL,)),
    }


def reference(x_prompt, x_sample, p_prompt, p_sample, state_conv, state_gdn, state_mlstm_c, state_mlstm_n,
              state_mlstm_m, norm_attn_g, w_in, conv_w, gdn_a_log, gdn_dt_bias, gdn_norm_g, mlstm_i_bias,
              mlstm_f_bias, mlstm_norm_g, w_out, norm_moe_g, router_w, router_b, expert_w_gu, expert_b_gu,
              expert_w_down, expert_b_down, norm_ple_g, ple_gate_w, ple_w, final_norm_g):
    params = (norm_attn_g, w_in, conv_w, gdn_a_log, gdn_dt_bias, gdn_norm_g, mlstm_i_bias, mlstm_f_bias,
              mlstm_norm_g, w_out, norm_moe_g, router_w, router_b, expert_w_gu, expert_b_gu, expert_w_down,
              expert_b_down, norm_ple_g, ple_gate_w, ple_w, final_norm_g)
    B = x_prompt.shape[0]
    f32 = jnp.float32
    conv0 = jnp.zeros((DEPTH, B, CONV_W - 1, CONV_CH), f32)
    gdn0 = jnp.zeros((DEPTH, B, H_A, DK_A, DV_A), f32)
    c0 = jnp.zeros((DEPTH, B, H_B, DK_B, DV_B), f32)
    n0 = jnp.zeros((DEPTH, B, H_B, DK_B), f32)
    m0 = jnp.zeros((DEPTH, B, H_B), f32)
    y_prompt, conv_p, gdn_p, c_p, n_p, m_p = _trunk(x_prompt, p_prompt, conv0, gdn0, c0, n0, m0, params)
    y_sample, conv_s, gdn_s, c_s, n_s, m_s = _trunk(x_sample, p_sample, state_conv, state_gdn, state_mlstm_c,
                                                    state_mlstm_n, state_mlstm_m, params)
    return (y_prompt, y_sample, conv_p, gdn_p, c_p, n_p, m_p, conv_s, gdn_s, c_s, n_s, m_s)
```

```python
import functools

import numpy as np
import jax
import jax.numpy as jnp
from jax import lax
from jax.experimental import pallas as pl
from jax.experimental.pallas import tpu as pltpu

F32 = jnp.float32
BF16 = jnp.bfloat16

D_MODEL = 1024
H_A, DK_A, DV_A = 4, 128, 128
H_B, DK_B, DV_B = 4, 64, 128
CONV_W = 4
CONV_CH = H_A * (2 * DK_A + DV_A)
N_EXPERTS = 32
TOP_K = 4
D_FF = 1024
SWIGLU_LIMIT = 7.0
SWIGLU_ALPHA = 1.702
PLE_DIM = 256
EPS = 1e-6
NEG = -1e30
CHUNK = 64

LANE = 128
GDN_W = CONV_CH + H_A * DV_A
MLP_W = 2 * H_B * LANE + 2 * H_B * DV_B
N_GATE = 16
W_ALL = GDN_W + MLP_W + LANE

VMEM_LIMIT = 48 * 1024 * 1024

HI = lax.Precision.HIGHEST


def _dot(a, b, dims=(((1,), (0,)), ((), ()))):
    return lax.dot_general(a.astype(BF16), b.astype(BF16), dims, preferred_element_type=F32)


def _dot_hi(a, b, dims=(((1,), (0,)), ((), ()))):
    return lax.dot_general(a, b, dims, precision=HI, preferred_element_type=F32)


_NT = (((1,), (1,)), ((), ()))
_TN = (((0,), (0,)), ((), ()))


def _rms(x, g):
    return x * lax.rsqrt(jnp.mean(x * x, axis=-1, keepdims=True) + EPS) * g


def _softplus(t):
    return jnp.maximum(t, 0.0) + jnp.log1p(jnp.exp(-jnp.abs(t)))


def _sigmoid(t):
    return 1.0 / (1.0 + jnp.exp(-t))


def _silu(t):
    return t * _sigmoid(t)


def _activate_gates(raw, idx, alog, bias):
    t = raw + bias
    g = -jnp.exp(alog) * _softplus(t)
    beta = _sigmoid(t)
    lf = -_softplus(-t)
    return jnp.where(idx < 4, g, jnp.where(idx < 8, beta, jnp.where(idx < 12, t, lf)))


def _inproj_kernel(x_ref, g_ref, w_ref, wst_ref, pc_ref, pr_ref, gdn_ref, ml_ref, gate_ref, gatet_ref):
    tm = x_ref.shape[0]
    hn = _rms(x_ref[...], g_ref[...]).astype(BF16)
    gdn_ref[...] = jnp.dot(hn, w_ref[:, :GDN_W], preferred_element_type=F32)
    ml_ref[...] = jnp.dot(hn, w_ref[:, GDN_W:GDN_W + MLP_W], preferred_element_type=F32)
    raw = jnp.dot(hn, w_ref[:, GDN_W + MLP_W:], preferred_element_type=F32)
    lane = lax.broadcasted_iota(jnp.int32, (tm, LANE), 1)
    gate_ref[...] = _activate_gates(raw, lane, pc_ref[0:1, :], pc_ref[1:2, :])
    raw_t = lax.dot_general(wst_ref[...], hn, _NT, preferred_element_type=F32)
    row = lax.broadcasted_iota(jnp.int32, (N_GATE, tm), 0)
    gatet_ref[...] = _activate_gates(raw_t, row, pr_ref[:, 0:1], pr_ref[:, 1:2])


def _inproj(x, g, w_all, ws_t, pcol, prow, tm):
    n = x.shape[0]
    return pl.pallas_call(
        _inproj_kernel,
        grid=(n // tm,),
        in_specs=[
            pl.BlockSpec((tm, D_MODEL), lambda i: (i, 0)),
            pl.BlockSpec((1, D_MODEL), lambda i: (0, 0)),
            pl.BlockSpec((D_MODEL, W_ALL), lambda i: (0, 0)),
            pl.BlockSpec((N_GATE, D_MODEL), lambda i: (0, 0)),
            pl.BlockSpec((8, LANE), lambda i: (0, 0)),
            pl.BlockSpec((N_GATE, LANE), lambda i: (0, 0)),
        ],
        out_specs=[
            pl.BlockSpec((tm, GDN_W), lambda i: (i, 0)),
            pl.BlockSpec((tm, MLP_W), lambda i: (i, 0)),
            pl.BlockSpec((tm, LANE), lambda i: (i, 0)),
            pl.BlockSpec((N_GATE, tm), lambda i: (0, i)),
        ],
        out_shape=[
            jax.ShapeDtypeStruct((n, GDN_W), F32),
            jax.ShapeDtypeStruct((n, MLP_W), F32),
            jax.ShapeDtypeStruct((n, LANE), F32),
            jax.ShapeDtypeStruct((N_GATE, n), F32),
        ],
        compiler_params=pltpu.CompilerParams(dimension_semantics=("parallel",), vmem_limit_bytes=VMEM_LIMIT),
        name="inproj",
    )(x, g, w_all, ws_t, pcol, prow)


def _unit_lower_inverse(a_strict, L):
    eye = (lax.broadcasted_iota(jnp.int32, (L, L), 0) == lax.broadcasted_iota(jnp.int32, (L, L), 1)).astype(F32)
    x = -a_strict
    t = eye + x
    p = x
    n = 2
    while n < L:
        p = _dot_hi(p, p)
        t = t + _dot_hi(t, p)
        n *= 2
    return t


def _gdn_kernel(*refs, L, has_state):
    if has_state:
        (xin_ref, gate_ref, gatet_ref, cw_ref, ng_ref, cst_ref, s0_ref,
         mix_ref, cnew_ref, snew_ref, xc_ref, s_ref) = refs
    else:
        (xin_ref, gate_ref, gatet_ref, cw_ref, ng_ref,
         mix_ref, cnew_ref, snew_ref, xc_ref, s_ref) = refs
    c = pl.program_id(1)

    @pl.when(c == 0)
    def _():
        if has_state:
            xc_ref[0:8, :] = jnp.zeros((8, CONV_CH), F32)
            xc_ref[8 - (CONV_W - 1):8, :] = cst_ref[0]
            s_ref[...] = s0_ref[0]
        else:
            xc_ref[0:8, :] = jnp.zeros((8, CONV_CH), F32)
            s_ref[...] = jnp.zeros_like(s_ref)

    @pl.when(c > 0)
    def _():
        xc_ref[0:8, :] = xc_ref[L:L + 8, :]

    xc_ref[8:8 + L, :] = xin_ref[:, :CONV_CH]
    conv = xc_ref[5:5 + L, :] * cw_ref[0:1, :]
    for j in range(1, CONV_W):
        conv = conv + xc_ref[5 + j:5 + j + L, :] * cw_ref[j:j + 1, :]
    cnew_ref[0] = xc_ref[8 + L - (CONV_W - 1):8 + L, :]
    act = _silu(conv)

    gact = gate_ref[...]
    gact_t = gatet_ref[0]
    ri = lax.broadcasted_iota(jnp.int32, (L, L), 0)
    ci = lax.broadcasted_iota(jnp.int32, (L, L), 1)
    tril = ri >= ci
    strict = ri > ci
    cum_c = _dot_hi(tril.astype(F32), gact)
    cum_r = _dot_hi(gact_t, (ri <= ci).astype(F32))

    for h in range(H_A):
        q = act[:, h * DK_A:(h + 1) * DK_A]
        k = act[:, H_A * DK_A + h * DK_A:H_A * DK_A + (h + 1) * DK_A]
        v = act[:, 2 * H_A * DK_A + h * DV_A:2 * H_A * DK_A + (h + 1) * DV_A]
        q = q * lax.rsqrt(jnp.sum(q * q, axis=-1, keepdims=True) + EPS) * (DK_A ** -0.5)
        k = k * lax.rsqrt(jnp.sum(k * k, axis=-1, keepdims=True) + EPS)
        beta = gact[:, 4 + h:5 + h]
        gc = cum_c[:, h:h + 1]
        gr = cum_r[h:h + 1, :]
        gl = cum_c[L - 1:L, h:h + 1]
        decay = jnp.where(tril, jnp.exp(jnp.where(tril, gc - gr, 0.0)), 0.0)
        kb = k * beta
        egc = jnp.exp(gc)
        a_mat = jnp.where(strict, _dot_hi(kb, k, _NT) * decay, 0.0)
        t_inv = _unit_lower_inverse(a_mat, L)
        rhs = jnp.concatenate([v * beta, kb * egc], axis=-1)
        sol = _dot_hi(t_inv, rhs)
        u, w = sol[:, :DV_A], sol[:, DV_A:]
        s = s_ref[h]
        v_new = u - _dot_hi(w, s)
        qk = jnp.where(tril, _dot_hi(q, k, _NT) * decay, 0.0)
        o = _dot_hi(q * egc, s) + _dot_hi(qk, v_new)
        kd = k * jnp.exp(gl - gc)
        s_ref[h] = s * jnp.exp(gl) + _dot_hi(kd, v_new, _TN)
        z = xin_ref[:, CONV_CH + h * DV_A:CONV_CH + (h + 1) * DV_A]
        mix_ref[:, h * DV_A:(h + 1) * DV_A] = (_rms(o, ng_ref[...]) * _silu(z)).astype(mix_ref.dtype)

    snew_ref[0] = s_ref[...]


def _gdn(gdn_in, gates, gates_t3, cw, ng, *, n_seq, T, L, row0, state=None, mix_init=None):
    n_c = T // L
    blk0 = row0 // L
    n = gdn_in.shape[0]
    has_state = state is not None
    in_specs = [
        pl.BlockSpec((L, GDN_W), lambda b, c: (blk0 + b * n_c + c, 0)),
        pl.BlockSpec((L, LANE), lambda b, c: (blk0 + b * n_c + c, 0)),
        pl.BlockSpec((1, N_GATE, L), lambda b, c: (b * n_c + c, 0, 0)),
        pl.BlockSpec((8, CONV_CH), lambda b, c: (0, 0)),
        pl.BlockSpec((1, DV_A), lambda b, c: (0, 0)),
    ]
    args = [gdn_in, gates, gates_t3, cw, ng]
    if has_state:
        conv_st, s0 = state
        in_specs += [
            pl.BlockSpec((1, CONV_W - 1, CONV_CH), lambda b, c: (b, 0, 0)),
            pl.BlockSpec((1, H_A, DK_A, DV_A), lambda b, c: (b, 0, 0, 0)),
        ]
        args += [conv_st, s0]
    aliases = {}
    if mix_init is not None:
        in_specs.append(pl.BlockSpec(memory_space=pl.ANY))
        args.append(mix_init)
        aliases = {len(args) - 1: 0}
    kern = functools.partial(_gdn_kernel, L=L, has_state=has_state)
    if mix_init is not None:
        kern = _drop_alias_ref(kern, len(args) - 1)
    return pl.pallas_call(
        kern,
        grid=(n_seq, n_c),
        in_specs=in_specs,
        out_specs=[
            pl.BlockSpec((L, H_A * DV_A), lambda b, c: (blk0 + b * n_c + c, 0)),
            pl.BlockSpec((1, CONV_W - 1, CONV_CH), lambda b, c: (b, 0, 0)),
            pl.BlockSpec((1, H_A, DK_A, DV_A), lambda b, c: (b, 0, 0, 0)),
        ],
        out_shape=[
            jax.ShapeDtypeStruct((n, H_A * DV_A), F32),
            jax.ShapeDtypeStruct((n_seq, CONV_W - 1, CONV_CH), F32),
            jax.ShapeDtypeStruct((n_seq, H_A, DK_A, DV_A), F32),
        ],
        scratch_shapes=[pltpu.VMEM((L + 8, CONV_CH), F32), pltpu.VMEM((H_A, DK_A, DV_A), F32)],
        input_output_aliases=aliases,
        compiler_params=pltpu.CompilerParams(dimension_semantics=("parallel", "arbitrary"),
                                             vmem_limit_bytes=VMEM_LIMIT),
        name=f"gdn_L{L}",
    )(*args)


def _drop_alias_ref(kern, pos):
    def wrapped(*refs):
        return kern(*(refs[:pos] + refs[pos + 1:]))
    return wrapped


def _mlstm_kernel(*refs, L, has_state):
    if has_state:
        (xin_ref, gate_ref, gatet_ref, ng_ref, c0_ref, n0_ref, m0_ref,
         mix_ref, cnew_ref, nnew_ref, mnew_ref, c_ref, n_ref, m_ref) = refs
    else:
        (xin_ref, gate_ref, gatet_ref, ng_ref,
         mix_ref, cnew_ref, nnew_ref, mnew_ref, c_ref, n_ref, m_ref) = refs
    c = pl.program_id(1)

    @pl.when(c == 0)
    def _():
        c_ref[...] = jnp.zeros_like(c_ref)
        n_ref[...] = jnp.zeros_like(n_ref)
        m_ref[...] = jnp.zeros_like(m_ref)
        if has_state:
            c_ref[:, 0:DK_B, :] = c0_ref[0]
            n_ref[0:H_B, 0:DK_B] = n0_ref[0]
            m_ref[0:1, 0:H_B] = m0_ref[0]

    gact = gate_ref[...]
    gact_t = gatet_ref[0]
    ri = lax.broadcasted_iota(jnp.int32, (L, L), 0)
    ci = lax.broadcasted_iota(jnp.int32, (L, L), 1)
    tril = ri >= ci
    cum_c = _dot_hi(tril.astype(F32), gact)
    cum_r = _dot_hi(gact_t, (ri <= ci).astype(F32))

    for h in range(H_B):
        q = xin_ref[:, h * LANE:(h + 1) * LANE] * (DK_B ** -0.5)
        k = xin_ref[:, (H_B + h) * LANE:(H_B + h + 1) * LANE]
        v = xin_ref[:, 2 * H_B * LANE + h * DV_B:2 * H_B * LANE + (h + 1) * DV_B]
        og = xin_ref[:, 2 * H_B * LANE + H_B * DV_B + h * DV_B:2 * H_B * LANE + H_B * DV_B + (h + 1) * DV_B]
        ig_c = gact[:, 8 + h:9 + h]
        ig_r = gact_t[8 + h:9 + h, :]
        b_c = cum_c[:, 12 + h:13 + h]
        b_r = cum_r[12 + h:13 + h, :]
        b_last = cum_c[L - 1:L, 12 + h:13 + h]
        m_prev = m_ref[0:1, h:h + 1]
        cm = c_ref[h]
        nv = n_ref[h:h + 1, :]
        d_log = jnp.where(tril, b_c - b_r + ig_r, NEG)
        inter = b_c + m_prev
        m_t = jnp.maximum(inter, jnp.max(d_log, axis=-1, keepdims=True))
        s = _dot_hi(q, k, _NT) * jnp.exp(d_log - m_t)
        e_inter = jnp.exp(inter - m_t)
        num = e_inter * _dot_hi(q, cm) + _dot_hi(s, v)
        den = e_inter * jnp.sum(q * nv, axis=-1, keepdims=True) + jnp.sum(s, axis=-1, keepdims=True)
        hh = num / jnp.maximum(jnp.abs(den), jnp.exp(-m_t))
        m_new = m_t[L - 1:L, :]
        w_last = jnp.exp(b_last - b_c + ig_c - m_new)
        f_tot = jnp.exp(b_last + m_prev - m_new)
        kw = k * w_last
        c_ref[h] = f_tot * cm + _dot_hi(kw, v, _TN)
        n_ref[h:h + 1, :] = f_tot * nv + jnp.sum(kw, axis=0, keepdims=True)
        m_ref[0:1, h:h + 1] = m_new
        out = _rms(hh, ng_ref[h:h + 1, :]) * _sigmoid(og)
        mix_ref[:, h * DV_B:(h + 1) * DV_B] = out.astype(mix_ref.dtype)

    cnew_ref[0] = c_ref[:, 0:DK_B, :]
    nnew_ref[0] = n_ref[0:H_B, 0:DK_B]
    mnew_ref[0] = m_ref[0:1, 0:H_B]


def _mlstm(ml_in, gates, gates_t3, ng, *, n_seq, T, L, row0, state=None, mix_init=None):
    n_c = T // L
    blk0 = row0 // L
    n = ml_in.shape[0]
    has_state = state is not None
    in_specs = [
        pl.BlockSpec((L, MLP_W), lambda b, c: (blk0 + b * n_c + c, 0)),
        pl.BlockSpec((L, LANE), lambda b, c: (blk0 + b * n_c + c, 0)),
        pl.BlockSpec((1, N_GATE, L), lambda b, c: (b * n_c + c, 0, 0)),
        pl.BlockSpec((H_B, DV_B), lambda b, c: (0, 0)),
    ]
    args = [ml_in, gates, gates_t3, ng]
    if has_state:
        c0, n0, m0 = state
        in_specs += [
            pl.BlockSpec((1, H_B, DK_B, DV_B), lambda b, c: (b, 0, 0, 0)),
            pl.BlockSpec((1, H_B, DK_B), lambda b, c: (b, 0, 0)),
            pl.BlockSpec((1, 1, H_B), lambda b, c: (b, 0, 0)),
        ]
        args += [c0, n0, m0]
    aliases = {}
    if mix_init is not None:
        in_specs.append(pl.BlockSpec(memory_space=pl.ANY))
        args.append(mix_init)
        aliases = {len(args) - 1: 0}
    kern = functools.partial(_mlstm_kernel, L=L, has_state=has_state)
    if mix_init is not None:
        kern = _drop_alias_ref(kern, len(args) - 1)
    return pl.pallas_call(
        kern,
        grid=(n_seq, n_c),
        in_specs=in_specs,
        out_specs=[
            pl.BlockSpec((L, H_B * DV_B), lambda b, c: (blk0 + b * n_c + c, 0)),
            pl.BlockSpec((1, H_B, DK_B, DV_B), lambda b, c: (b, 0, 0, 0)),
            pl.BlockSpec((1, H_B, DK_B), lambda b, c: (b, 0, 0)),
            pl.BlockSpec((1, 1, H_B), lambda b, c: (b, 0, 0)),
        ],
        out_shape=[
            jax.ShapeDtypeStruct((n, H_B * DV_B), F32),
            jax.ShapeDtypeStruct((n_seq, H_B, DK_B, DV_B), F32),
            jax.ShapeDtypeStruct((n_seq, H_B, DK_B), F32),
            jax.ShapeDtypeStruct((n_seq, 1, H_B), F32),
        ],
        scratch_shapes=[pltpu.VMEM((H_B, LANE, DV_B), F32), pltpu.VMEM((8, LANE), F32), pltpu.VMEM((8, LANE), F32)],
        input_output_aliases=aliases,
        compiler_params=pltpu.CompilerParams(dimension_semantics=("parallel", "arbitrary"),
                                             vmem_limit_bytes=VMEM_LIMIT),
        name=f"mlstm_L{L}",
    )(*args)


def _outproj_kernel(x_ref, ma_ref, mb_ref, wo_ref, g_ref, rw_ref, rb_ref, x1_ref, hn_ref, lg_ref):
    half = H_A * DV_A
    x1 = (x_ref[...] + jnp.dot(ma_ref[...].astype(BF16), wo_ref[:half, :], preferred_element_type=F32)
          + jnp.dot(mb_ref[...].astype(BF16), wo_ref[half:, :], preferred_element_type=F32))
    x1_ref[...] = x1
    hn = _rms(x1, g_ref[...])
    hn_ref[...] = hn
    lg_ref[...] = _dot_hi(hn, rw_ref[...]) + rb_ref[...]


def _outproj(x, mix_a, mix_b, w_out, g, rw, rb, tm):
    n = x.shape[0]
    half = H_A * DV_A
    return pl.pallas_call(
        _outproj_kernel,
        grid=(n // tm,),
        in_specs=[
            pl.BlockSpec((tm, D_MODEL), lambda i: (i, 0)),
            pl.BlockSpec((tm, half), lambda i: (i, 0)),
            pl.BlockSpec((tm, half), lambda i: (i, 0)),
            pl.BlockSpec((D_MODEL, D_MODEL), lambda i: (0, 0)),
            pl.BlockSpec((1, D_MODEL), lambda i: (0, 0)),
            pl.BlockSpec((D_MODEL, LANE), lambda i: (0, 0)),
            pl.BlockSpec((1, LANE), lambda i: (0, 0)),
        ],
        out_specs=[
            pl.BlockSpec((tm, D_MODEL), lambda i: (i, 0)),
            pl.BlockSpec((tm, D_MODEL), lambda i: (i, 0)),
            pl.BlockSpec((tm, LANE), lambda i: (i, 0)),
        ],
        out_shape=[
            jax.ShapeDtypeStruct((n, D_MODEL), F32),
            jax.ShapeDtypeStruct((n, D_MODEL), F32),
            jax.ShapeDtypeStruct((n, LANE), F32),
        ],
        compiler_params=pltpu.CompilerParams(dimension_semantics=("parallel",), vmem_limit_bytes=VMEM_LIMIT),
        name="outproj",
    )(x, mix_a, mix_b, w_out, g, rw, rb)


def _expert_kernel(be_ref, nu_ref, rt_ref, rtn_ref, x_hbm, wgu_ref, bgu_ref, wd_ref, bd_ref,
                   y_ref, xbuf, sem, wgu_bf, wd_bf, *, tb):
    b = pl.program_id(0)
    n_used = nu_ref[0]
    slot = b % 2

    def start_gather(idx_ref, s):
        def body(r, carry):
            tok = idx_ref[0, 0, r]
            pltpu.make_async_copy(x_hbm.at[pl.ds(tok, 1)], xbuf.at[s, pl.ds(r, 1)], sem.at[s]).start()
            return carry
        lax.fori_loop(0, tb, body, 0)

    @pl.when(b == 0)
    def _():
        start_gather(rt_ref, 0)

    @pl.when(b + 1 < n_used)
    def _():
        start_gather(rtn_ref, 1 - slot)

    @pl.when(b < n_used)
    def _():
        pltpu.make_async_copy(x_hbm.at[pl.ds(0, tb)], xbuf.at[slot], sem.at[slot]).wait()
        e_prev = be_ref[jnp.maximum(b - 1, 0)]

        @pl.when(jnp.logical_or(b == 0, be_ref[b] != e_prev))
        def _():
            wgu_bf[...] = wgu_ref[0].astype(BF16)
            wd_bf[...] = wd_ref[0].astype(BF16)

        xb = xbuf[slot].astype(BF16)
        hgu = jnp.dot(xb, wgu_bf[...], preferred_element_type=F32) + bgu_ref[0]
        gate = jnp.minimum(hgu[:, :D_FF], SWIGLU_LIMIT)
        up = jnp.clip(hgu[:, D_FF:], -SWIGLU_LIMIT, SWIGLU_LIMIT)
        act = (up + 1.0) * gate * _sigmoid(SWIGLU_ALPHA * gate)
        y_ref[...] = jnp.dot(act.astype(BF16), wd_bf[...], preferred_element_type=F32) + bd_ref[0]


def _experts(block_e, n_used, row_tok3, x_src, w_gu, b_gu, w_down, b_down, tb):
    nb = row_tok3.shape[0]

    def cur(b, be, nu):
        return jnp.minimum(b, nu[0] - 1)

    grid_spec = pltpu.PrefetchScalarGridSpec(
        num_scalar_prefetch=2,
        grid=(nb,),
        in_specs=[
            pl.BlockSpec((1, 1, tb), lambda b, be, nu: (cur(b, be, nu), 0, 0), memory_space=pltpu.SMEM),
            pl.BlockSpec((1, 1, tb), lambda b, be, nu: (jnp.minimum(b + 1, nu[0] - 1), 0, 0),
                         memory_space=pltpu.SMEM),
            pl.BlockSpec(memory_space=pl.ANY),
            pl.BlockSpec((1, D_MODEL, 2 * D_FF), lambda b, be, nu: (be[cur(b, be, nu)], 0, 0)),
            pl.BlockSpec((1, 1, 2 * D_FF), lambda b, be, nu: (be[cur(b, be, nu)], 0, 0)),
            pl.BlockSpec((1, D_FF, D_MODEL), lambda b, be, nu: (be[cur(b, be, nu)], 0, 0)),
            pl.BlockSpec((1, 1, D_MODEL), lambda b, be, nu: (be[cur(b, be, nu)], 0, 0)),
        ],
        out_specs=pl.BlockSpec((tb, D_MODEL), lambda b, be, nu: (cur(b, be, nu), 0)),
        scratch_shapes=[
            pltpu.VMEM((2, tb, D_MODEL), F32),
            pltpu.SemaphoreType.DMA((2,)),
            pltpu.VMEM((D_MODEL, 2 * D_FF), BF16),
            pltpu.VMEM((D_FF, D_MODEL), BF16),
        ],
    )
    return pl.pallas_call(
        functools.partial(_expert_kernel, tb=tb),
        grid_spec=grid_spec,
        out_shape=jax.ShapeDtypeStruct((nb * tb, D_MODEL), F32),
        compiler_params=pltpu.CompilerParams(dimension_semantics=("arbitrary",), vmem_limit_bytes=VMEM_LIMIT),
        name="experts",
    )(block_e, n_used, row_tok3, row_tok3, x_src, w_gu, b_gu, w_down, b_down)


def _combine_kernel(pos_ref, posn_ref, y_hbm, x1_ref, gt_ref, p_ref, gple_ref, wg_ref, wp_ref, gfin_ref,
                    out_ref, ybuf, sem, *, tm):
    i = pl.program_id(0)
    n_i = pl.num_programs(0)
    slot = i % 2

    def start_gather(idx_ref, s):
        def body(r, carry):
            for kk in range(TOP_K):
                row = idx_ref[0, 0, kk * tm + r]
                pltpu.make_async_copy(y_hbm.at[pl.ds(row, 1)], ybuf.at[s, kk, pl.ds(r, 1)], sem.at[s]).start()
            return carry
        lax.fori_loop(0, tm, body, 0)

    @pl.when(i == 0)
    def _():
        start_gather(pos_ref, 0)

    @pl.when(i + 1 < n_i)
    def _():
        start_gather(posn_ref, 1 - slot)

    for kk in range(TOP_K):
        pltpu.make_async_copy(y_hbm.at[pl.ds(0, tm)], ybuf.at[slot, kk], sem.at[slot]).wait()
    gt = gt_ref[...]
    x2 = x1_ref[...]
    for kk in range(TOP_K):
        x2 = x2 + gt[:, kk:kk + 1] * ybuf[slot, kk]
    hn = _rms(x2, gple_ref[...]).astype(BF16)
    gate = _sigmoid(jnp.dot(hn, wg_ref[...], preferred_element_type=F32))
    pe = jnp.dot(p_ref[...].astype(BF16), wp_ref[...], preferred_element_type=F32)
    x3 = x2 + gate * pe
    out_ref[...] = _rms(x3, gfin_ref[...])


def _combine(pos3, y_rows, x1, gates_pad, p, g_ple, w_gate, w_p, g_fin, tm):
    n = x1.shape[0]
    nt = n // tm
    return pl.pallas_call(
        functools.partial(_combine_kernel, tm=tm),
        grid=(nt,),
        in_specs=[
            pl.BlockSpec((1, 1, TOP_K * tm), lambda i: (i, 0, 0), memory_space=pltpu.SMEM),
            pl.BlockSpec((1, 1, TOP_K * tm), lambda i: (jnp.minimum(i + 1, nt - 1), 0, 0), memory_space=pltpu.SMEM),
            pl.BlockSpec(memory_space=pl.ANY),
            pl.BlockSpec((tm, D_MODEL), lambda i: (i, 0)),
            pl.BlockSpec((tm, LANE), lambda i: (i, 0)),
            pl.BlockSpec((tm, PLE_DIM), lambda i: (i, 0)),
            pl.BlockSpec((1, D_MODEL), lambda i: (0, 0)),
            pl.BlockSpec((D_MODEL, D_MODEL), lambda i: (0, 0)),
            pl.BlockSpec((PLE_DIM, D_MODEL), lambda i: (0, 0)),
            pl.BlockSpec((1, D_MODEL), lambda i: (0, 0)),
        ],
        out_specs=pl.BlockSpec((tm, D_MODEL), lambda i: (i, 0)),
        out_shape=jax.ShapeDtypeStruct((n, D_MODEL), F32),
        scratch_shapes=[pltpu.VMEM((2, TOP_K, tm, D_MODEL), F32), pltpu.SemaphoreType.DMA((2,))],
        compiler_params=pltpu.CompilerParams(dimension_semantics=("arbitrary",), vmem_limit_bytes=VMEM_LIMIT),
        name="combine",
    )(pos3, pos3, y_rows, x1, gates_pad, p, g_ple, w_gate, w_p, g_fin)


def _routing(logits, tb, nb):
    n = logits.shape[0]
    top_val, top_idx = lax.top_k(logits, TOP_K)
    gates = jax.nn.softmax(top_val, axis=-1)
    onehot = (top_idx[:, :, None] == jnp.arange(N_EXPERTS, dtype=jnp.int32)[None, None, :])
    mask = jnp.sum(onehot.astype(jnp.int32), axis=1)
    csum = jnp.cumsum(mask, axis=0)
    counts = csum[-1]
    rank = csum - mask
    padded = (counts + tb - 1) // tb * tb
    pad_end = jnp.cumsum(padded)
    pad_start = pad_end - padded
    pos = jnp.take_along_axis(pad_start[None, :] + rank, top_idx, axis=1).astype(jnp.int32)
    tok = jnp.broadcast_to(jnp.arange(n, dtype=jnp.int32)[:, None], (n, TOP_K))
    row_tok = jnp.zeros((nb * tb,), jnp.int32).at[pos.reshape(-1)].set(tok.reshape(-1))
    block_e = jnp.minimum(jnp.searchsorted(pad_end, jnp.arange(nb, dtype=pad_end.dtype) * tb, side='right'),
                          N_EXPERTS - 1).astype(jnp.int32)
    n_used = (pad_end[-1] // tb).astype(jnp.int32).reshape(1)
    return gates, pos, row_tok, block_e, n_used


def _rearranged_in_weights(w_in):
    o = np.cumsum([0, CONV_CH, H_A * DV_A, H_A, H_A, H_B * DK_B, H_B * DK_B, H_B * DV_B, H_B * DV_B, H_B, H_B])
    conv_in, z_a, a_a, b_a, q_b, k_b, v_b, o_b, i_b, f_b = (w_in[:, int(o[j]):int(o[j + 1])] for j in range(10))
    zpad = jnp.zeros((D_MODEL, LANE - DK_B), w_in.dtype)

    def pad_heads(w):
        return jnp.concatenate([jnp.concatenate([w[:, h * DK_B:(h + 1) * DK_B], zpad], axis=1) for h in range(H_B)],
                               axis=1)

    small = jnp.concatenate([a_a, b_a, i_b, f_b], axis=1)
    w_all = jnp.concatenate([conv_in, z_a, pad_heads(q_b), pad_heads(k_b), v_b, o_b,
                             small, jnp.zeros((D_MODEL, LANE - N_GATE), w_in.dtype)], axis=1)
    return w_all.astype(BF16), small.T.astype(BF16)


def _gate_params(a_log, dt_bias, i_bias, f_bias):
    z4 = jnp.zeros((4,), F32)
    alog = jnp.concatenate([a_log.astype(F32), z4, z4, z4])
    bias = jnp.concatenate([dt_bias.astype(F32), z4, i_bias.astype(F32), f_bias.astype(F32)])
    pad = jnp.zeros((LANE - N_GATE,), F32)
    pcol = jnp.zeros((8, LANE), F32).at[0].set(jnp.concatenate([alog, pad])).at[1].set(jnp.concatenate([bias, pad]))
    prow = jnp.zeros((N_GATE, LANE), F32).at[:, 0].set(alog).at[:, 1].set(bias)
    return pcol, prow


def kernel(x_prompt, x_sample, p_prompt, p_sample, state_conv, state_gdn, state_mlstm_c, state_mlstm_n, state_mlstm_m, norm_attn_g, w_in, conv_w, gdn_a_log, gdn_dt_bias, gdn_norm_g, mlstm_i_bias, mlstm_f_bias, mlstm_norm_g, w_out, norm_moe_g, router_w, router_b, expert_w_gu, expert_b_gu, expert_w_down, expert_b_down, norm_ple_g, ple_gate_w, ple_w, final_norm_g):
    bp, tp, _ = x_prompt.shape
    bs, ts, _ = x_sample.shape
    n_p, n_s = bp * tp, bs * ts
    n = n_p + n_s
    lp, ls = min(tp, CHUNK), min(ts, CHUNK)
    assert tp % lp == 0 and ts % ls == 0 and n_p % 512 == 0 and n_s % 512 == 0

    x = jnp.concatenate([x_prompt.reshape(n_p, D_MODEL), x_sample.reshape(n_s, D_MODEL)], axis=0)
    p = jnp.concatenate([p_prompt[0].reshape(n_p, PLE_DIM), p_sample[0].reshape(n_s, PLE_DIM)], axis=0)

    w_all, ws_t = _rearranged_in_weights(w_in[0])
    pcol, prow = _gate_params(gdn_a_log[0], gdn_dt_bias[0], mlstm_i_bias[0], mlstm_f_bias[0])
    gdn_in, ml_in, gates, gates_t = _inproj(x, norm_attn_g[0].reshape(1, D_MODEL), w_all, ws_t, pcol, prow, tm=512)
    gt_p = gates_t[:, :n_p].reshape(N_GATE, n_p // lp, lp).transpose(1, 0, 2)
    gt_s = gates_t[:, n_p:].reshape(N_GATE, n_s // ls, ls).transpose(1, 0, 2)

    cw = jnp.zeros((8, CONV_CH), F32).at[:CONV_W].set(conv_w[0].astype(F32))
    ng_a = gdn_norm_g[0].reshape(1, DV_A).astype(F32)
    ng_b = mlstm_norm_g[0].reshape(H_B, DV_B).astype(F32)
    mix_a, conv_p, gdn_p = _gdn(gdn_in, gates, gt_p, cw, ng_a, n_seq=bp, T=tp, L=lp, row0=0)
    mix_a, conv_s, gdn_s = _gdn(gdn_in, gates, gt_s, cw, ng_a, n_seq=bs, T=ts, L=ls, row0=n_p,
                                state=(state_conv[0], state_gdn[0]), mix_init=mix_a)
    mix_b, c_p, nn_p, m_p = _mlstm(ml_in, gates, gt_p, ng_b, n_seq=bp, T=tp, L=lp, row0=0)
    mix_b, c_s, nn_s, m_s = _mlstm(ml_in, gates, gt_s, ng_b, n_seq=bs, T=ts, L=ls, row0=n_p,
                                   state=(state_mlstm_c[0], state_mlstm_n[0], state_mlstm_m[0].reshape(bs, 1, H_B)),
                                   mix_init=mix_b)

    rw = jnp.zeros((D_MODEL, LANE), F32).at[:, :N_EXPERTS].set(router_w[0])
    rb = jnp.full((1, LANE), NEG, F32).at[0, :N_EXPERTS].set(router_b[0])
    x1, hn2, logits = _outproj(x, mix_a, mix_b, w_out[0].astype(BF16), norm_moe_g[0].reshape(1, D_MODEL), rw, rb,
                               tm=512)

    tb = 256
    nb = n * TOP_K // tb + N_EXPERTS
    gates_k, pos, row_tok, block_e, n_used = _routing(logits[:, :N_EXPERTS], tb, nb)
    y_rows = _experts(block_e, n_used, row_tok.reshape(nb, 1, tb), hn2,
                      expert_w_gu[0], expert_b_gu[0].reshape(N_EXPERTS, 1, 2 * D_FF),
                      expert_w_down[0], expert_b_down[0].reshape(N_EXPERTS, 1, D_MODEL), tb)
    tm5 = 256
    pos3 = pos.reshape(n // tm5, tm5, TOP_K).transpose(0, 2, 1).reshape(n // tm5, 1, TOP_K * tm5)
    gates_pad = jnp.zeros((n, LANE), F32).at[:, :TOP_K].set(gates_k)
    y = _combine(pos3, y_rows, x1, gates_pad, p, norm_ple_g[0].reshape(1, D_MODEL),
                 ple_gate_w[0].astype(BF16), ple_w[0].astype(BF16), final_norm_g.reshape(1, D_MODEL), tm5)

    y_prompt = y[:n_p].reshape(bp, tp, D_MODEL)
    y_sample = y[n_p:].reshape(bs, ts, D_MODEL)
    return (y_prompt, y_sample,
            conv_p[None], gdn_p[None], c_p[None], nn_p[None], m_p.reshape(1, bp, H_B),
            conv_s[None], gdn_s[None], c_s[None], nn_s[None], m_s.reshape(1, bs, H_B))
```

```python
import functools

import numpy as np
import jax
import jax.numpy as jnp
from jax import lax
from jax.experimental import pallas as pl
from jax.experimental.pallas import tpu as pltpu

F32 = jnp.float32
BF16 = jnp.bfloat16

D_MODEL = 1024
H_A, DK_A, DV_A = 4, 128, 128
H_B, DK_B, DV_B = 4, 64, 128
CONV_W = 4
CONV_CH = H_A * (2 * DK_A + DV_A)
N_EXPERTS = 32
TOP_K = 4
D_FF = 1024
SWIGLU_LIMIT = 7.0
SWIGLU_ALPHA = 1.702
PLE_DIM = 256
EPS = 1e-6
NEG = -1e30
CHUNK = 64

LANE = 128
SUBLANE = 8
GDN_W = CONV_CH + H_A * DV_A
MLP_W = 2 * H_B * LANE + 2 * H_B * DV_B
N_GATE = 16
W_ALL = GDN_W + MLP_W + LANE

VMEM_LIMIT = 48 * 1024 * 1024

HI = lax.Precision.HIGHEST

_NN = (((1,), (0,)), ((), ()))
_NT = (((1,), (1,)), ((), ()))
_TN = (((0,), (0,)), ((), ()))


def _dot(a, b, dims=_NN):
    return lax.dot_general(a.astype(BF16), b.astype(BF16), dims, preferred_element_type=F32)


def _dot_hi(a, b, dims=_NN):
    return lax.dot_general(a, b, dims, precision=HI, preferred_element_type=F32)


def _rms(x, g):
    return x * lax.rsqrt(jnp.mean(x * x, axis=-1, keepdims=True) + EPS) * g


def _softplus(t):
    return jnp.maximum(t, 0.0) + jnp.log1p(jnp.exp(-jnp.abs(t)))


def _sigmoid(t):
    return 1.0 / (1.0 + jnp.exp(-t))


def _silu(t):
    return t * _sigmoid(t)


def _activate_gates(raw, idx, alog, bias):
    t = raw + bias
    g = -jnp.exp(alog) * _softplus(t)
    beta = _sigmoid(t)
    lf = -_softplus(-t)
    return jnp.where(idx < 4, g, jnp.where(idx < 8, beta, jnp.where(idx < 12, t, lf)))


def _two_segment_specs(tm, width, n_p_tiles):
    return [pl.BlockSpec((tm, width), lambda i: (jnp.minimum(i, n_p_tiles - 1), 0)),
            pl.BlockSpec((tm, width), lambda i: (jnp.maximum(i - n_p_tiles, 0), 0))]


def _for_segment(n_p_tiles, body):
    i = pl.program_id(0)

    @pl.when(i < n_p_tiles)
    def _():
        body(0)

    @pl.when(i >= n_p_tiles)
    def _():
        body(1)


def _inproj_kernel(xp_ref, xs_ref, g_ref, w_ref, wst_ref, pc_ref, pr_ref,
                   gdnp_ref, gdns_ref, mlp_ref, mls_ref, gatep_ref, gates_ref, gatetp_ref, gatets_ref,
                   *, n_p_tiles):
    tm = xp_ref.shape[0]

    def body(seg):
        x_ref = (xp_ref, xs_ref)[seg]
        gdn_ref, ml_ref = (gdnp_ref, gdns_ref)[seg], (mlp_ref, mls_ref)[seg]
        gate_ref, gatet_ref = (gatep_ref, gates_ref)[seg], (gatetp_ref, gatets_ref)[seg]
        hn = _rms(x_ref[...], g_ref[...]).astype(BF16)
        gdn_ref[...] = jnp.dot(hn, w_ref[:, :GDN_W], preferred_element_type=F32)
        ml_ref[...] = jnp.dot(hn, w_ref[:, GDN_W:GDN_W + MLP_W], preferred_element_type=F32)
        raw = jnp.dot(hn, w_ref[:, GDN_W + MLP_W:], preferred_element_type=F32)
        lane = lax.broadcasted_iota(jnp.int32, (tm, LANE), 1)
        gate_ref[...] = _activate_gates(raw, lane, pc_ref[0:1, :], pc_ref[1:2, :])
        raw_t = lax.dot_general(wst_ref[...], hn, _NT, preferred_element_type=F32)
        row = lax.broadcasted_iota(jnp.int32, (N_GATE, tm), 0)
        gatet_ref[...] = _activate_gates(raw_t, row, pr_ref[:, 0:1], pr_ref[:, 1:2])

    _for_segment(n_p_tiles, body)


def _inproj(xp, xs, g, w_all, ws_t, pcol, prow, tm):
    n_p, n_s = xp.shape[0], xs.shape[0]
    npt = n_p // tm

    def out2(width):
        return _two_segment_specs(tm, width, npt)

    def shp2(width):
        return [jax.ShapeDtypeStruct((n_p, width), F32), jax.ShapeDtypeStruct((n_s, width), F32)]

    return pl.pallas_call(
        functools.partial(_inproj_kernel, n_p_tiles=npt),
        grid=((n_p + n_s) // tm,),
        in_specs=_two_segment_specs(tm, D_MODEL, npt) + [
            pl.BlockSpec((1, D_MODEL), lambda i: (0, 0)),
            pl.BlockSpec((D_MODEL, W_ALL), lambda i: (0, 0)),
            pl.BlockSpec((N_GATE, D_MODEL), lambda i: (0, 0)),
            pl.BlockSpec((SUBLANE, LANE), lambda i: (0, 0)),
            pl.BlockSpec((N_GATE, LANE), lambda i: (0, 0)),
        ],
        out_specs=out2(GDN_W) + out2(MLP_W) + out2(LANE) + [
            pl.BlockSpec((N_GATE, tm), lambda i: (0, jnp.minimum(i, npt - 1))),
            pl.BlockSpec((N_GATE, tm), lambda i: (0, jnp.maximum(i - npt, 0))),
        ],
        out_shape=shp2(GDN_W) + shp2(MLP_W) + shp2(LANE) + [
            jax.ShapeDtypeStruct((N_GATE, n_p), F32), jax.ShapeDtypeStruct((N_GATE, n_s), F32)],
        compiler_params=pltpu.CompilerParams(dimension_semantics=("arbitrary",), vmem_limit_bytes=VMEM_LIMIT),
        name="inproj",
    )(xp, xs, g, w_all, ws_t, pcol, prow)


def _unit_lower_inverse(a_strict, L):
    eye = (lax.broadcasted_iota(jnp.int32, (L, L), 0) == lax.broadcasted_iota(jnp.int32, (L, L), 1)).astype(F32)
    x = -a_strict
    t = eye + x
    p = x
    n = 2
    while n < L:
        p = _dot(p, p)
        t = t + _dot(t, p)
        n *= 2
    return t


def _chunk_masks(L):
    ri = lax.broadcasted_iota(jnp.int32, (L, L), 0)
    ci = lax.broadcasted_iota(jnp.int32, (L, L), 1)
    return ri >= ci, ri > ci, ri <= ci


def _gdn_kernel(*refs, L, G, has_state):
    if has_state:
        (xin_ref, gate_ref, gatet_ref, cw_ref, ng_ref, cst_ref, s0_ref,
         mix_ref, cnew_ref, snew_ref, xc_ref, s_ref) = refs
    else:
        (xin_ref, gate_ref, gatet_ref, cw_ref, ng_ref,
         mix_ref, cnew_ref, snew_ref, xc_ref, s_ref) = refs
    c = pl.program_id(1)

    @pl.when(c == 0)
    def _():
        xc_ref[:, 0:SUBLANE, :] = jnp.zeros((G, SUBLANE, CONV_CH), F32)
        if has_state:
            xc_ref[:, SUBLANE - (CONV_W - 1):SUBLANE, :] = cst_ref[...]
            s_ref[...] = s0_ref[...]
        else:
            s_ref[...] = jnp.zeros_like(s_ref)

    @pl.when(c > 0)
    def _():
        xc_ref[:, 0:SUBLANE, :] = xc_ref[:, L:L + SUBLANE, :]

    tril, strict, triu = _chunk_masks(L)
    tril_f, triu_f = tril.astype(F32), triu.astype(F32)
    base = SUBLANE - (CONV_W - 1)

    for g in range(G):
        xc_ref[g, SUBLANE:SUBLANE + L, :] = xin_ref[g, :, :CONV_CH]
        conv = xc_ref[g, base:base + L, :] * cw_ref[0:1, :]
        for j in range(1, CONV_W):
            conv = conv + xc_ref[g, base + j:base + j + L, :] * cw_ref[j:j + 1, :]
        cnew_ref[g] = xc_ref[g, SUBLANE + L - (CONV_W - 1):SUBLANE + L, :]
        act = _silu(conv)

        gact = gate_ref[g]
        gact_t = gatet_ref[g, 0]
        cum_c = _dot_hi(tril_f, gact)
        cum_r = _dot_hi(gact_t, triu_f)

        for h in range(H_A):
            q = act[:, h * DK_A:(h + 1) * DK_A]
            k = act[:, H_A * DK_A + h * DK_A:H_A * DK_A + (h + 1) * DK_A]
            v = act[:, 2 * H_A * DK_A + h * DV_A:2 * H_A * DK_A + (h + 1) * DV_A]
            q = q * lax.rsqrt(jnp.sum(q * q, axis=-1, keepdims=True) + EPS) * (DK_A ** -0.5)
            k = k * lax.rsqrt(jnp.sum(k * k, axis=-1, keepdims=True) + EPS)
            beta = gact[:, 4 + h:5 + h]
            gc = cum_c[:, h:h + 1]
            gr = cum_r[h:h + 1, :]
            gl = cum_c[L - 1:L, h:h + 1]
            decay = jnp.where(tril, jnp.exp(jnp.where(tril, gc - gr, 0.0)), 0.0)
            kb = k * beta
            egc = jnp.exp(gc)
            a_mat = jnp.where(strict, _dot(kb, k, _NT) * decay, 0.0)
            t_inv = _unit_lower_inverse(a_mat, L)
            rhs = jnp.concatenate([v * beta, kb * egc], axis=-1)
            sol = _dot(t_inv, rhs)
            u, w = sol[:, :DV_A], sol[:, DV_A:]
            s = s_ref[g, h]
            v_new = u - _dot(w, s)
            qk = jnp.where(tril, _dot(q, k, _NT) * decay, 0.0)
            o = _dot(q * egc, s) + _dot(qk, v_new)
            kd = k * jnp.exp(gl - gc)
            s_ref[g, h] = s * jnp.exp(gl) + _dot(kd, v_new, _TN)
            z = xin_ref[g, :, CONV_CH + h * DV_A:CONV_CH + (h + 1) * DV_A]
            mix_ref[g, :, h * DV_A:(h + 1) * DV_A] = _rms(o, ng_ref[...]) * _silu(z)

    snew_ref[...] = s_ref[...]


def _gdn(gdn_in, gates, gates_t, cw, ng, *, L, G, state=None):
    n_seq, T, _ = gdn_in.shape
    n_c = T // L
    has_state = state is not None
    in_specs = [
        pl.BlockSpec((G, L, GDN_W), lambda b, c: (b, c, 0)),
        pl.BlockSpec((G, L, LANE), lambda b, c: (b, c, 0)),
        pl.BlockSpec((G, 1, N_GATE, L), lambda b, c: (b, c, 0, 0)),
        pl.BlockSpec((SUBLANE, CONV_CH), lambda b, c: (0, 0)),
        pl.BlockSpec((1, DV_A), lambda b, c: (0, 0)),
    ]
    args = [gdn_in, gates, gates_t, cw, ng]
    if has_state:
        conv_st, s0 = state
        in_specs += [
            pl.BlockSpec((G, CONV_W - 1, CONV_CH), lambda b, c: (b, 0, 0)),
            pl.BlockSpec((G, H_A, DK_A, DV_A), lambda b, c: (b, 0, 0, 0)),
        ]
        args += [conv_st, s0]
    return pl.pallas_call(
        functools.partial(_gdn_kernel, L=L, G=G, has_state=has_state),
        grid=(n_seq // G, n_c),
        in_specs=in_specs,
        out_specs=[
            pl.BlockSpec((G, L, H_A * DV_A), lambda b, c: (b, c, 0)),
            pl.BlockSpec((G, CONV_W - 1, CONV_CH), lambda b, c: (b, 0, 0)),
            pl.BlockSpec((G, H_A, DK_A, DV_A), lambda b, c: (b, 0, 0, 0)),
        ],
        out_shape=[
            jax.ShapeDtypeStruct((n_seq, T, H_A * DV_A), F32),
            jax.ShapeDtypeStruct((n_seq, CONV_W - 1, CONV_CH), F32),
            jax.ShapeDtypeStruct((n_seq, H_A, DK_A, DV_A), F32),
        ],
        scratch_shapes=[pltpu.VMEM((G, L + SUBLANE, CONV_CH), F32), pltpu.VMEM((G, H_A, DK_A, DV_A), F32)],
        compiler_params=pltpu.CompilerParams(dimension_semantics=("parallel", "arbitrary"),
                                             vmem_limit_bytes=VMEM_LIMIT),
        name=f"gdn_L{L}",
    )(*args)


def _mlstm_kernel(*refs, L, G, has_state):
    if has_state:
        (xin_ref, gate_ref, gatet_ref, ng_ref, c0_ref, n0_ref, m0_ref,
         mix_ref, cnew_ref, nnew_ref, mnew_ref, c_ref, n_ref, m_ref) = refs
    else:
        (xin_ref, gate_ref, gatet_ref, ng_ref,
         mix_ref, cnew_ref, nnew_ref, mnew_ref, c_ref, n_ref, m_ref) = refs
    c = pl.program_id(1)

    @pl.when(c == 0)
    def _():
        c_ref[...] = jnp.zeros_like(c_ref)
        n_ref[...] = jnp.zeros_like(n_ref)
        m_ref[...] = jnp.zeros_like(m_ref)
        if has_state:
            c_ref[:, :, 0:DK_B, :] = c0_ref[...]
            n_ref[:, 0:H_B, 0:DK_B] = n0_ref[...]
            m_ref[:, 0:1, 0:H_B] = m0_ref[...]

    tril, _, triu = _chunk_masks(L)
    tril_f, triu_f = tril.astype(F32), triu.astype(F32)

    for g in range(G):
        gact = gate_ref[g]
        gact_t = gatet_ref[g, 0]
        cum_c = _dot_hi(tril_f, gact)
        cum_r = _dot_hi(gact_t, triu_f)

        for h in range(H_B):
            q = xin_ref[g, :, h * LANE:(h + 1) * LANE] * (DK_B ** -0.5)
            k = xin_ref[g, :, (H_B + h) * LANE:(H_B + h + 1) * LANE]
            v0 = 2 * H_B * LANE
            v = xin_ref[g, :, v0 + h * DV_B:v0 + (h + 1) * DV_B]
            og = xin_ref[g, :, v0 + H_B * DV_B + h * DV_B:v0 + H_B * DV_B + (h + 1) * DV_B]
            ig_c = gact[:, 8 + h:9 + h]
            ig_r = gact_t[8 + h:9 + h, :]
            b_c = cum_c[:, 12 + h:13 + h]
            b_r = cum_r[12 + h:13 + h, :]
            b_last = cum_c[L - 1:L, 12 + h:13 + h]
            m_prev = m_ref[g, 0:1, h:h + 1]
            cm = c_ref[g, h]
            nv = n_ref[g, h:h + 1, :]
            d_log = jnp.where(tril, b_c - b_r + ig_r, NEG)
            inter = b_c + m_prev
            m_t = jnp.maximum(inter, jnp.max(d_log, axis=-1, keepdims=True))
            s = _dot(q, k, _NT) * jnp.exp(d_log - m_t)
            e_inter = jnp.exp(inter - m_t)
            num = e_inter * _dot(q, cm) + _dot(s, v)
            den = e_inter * jnp.sum(q * nv, axis=-1, keepdims=True) + jnp.sum(s, axis=-1, keepdims=True)
            hh = num / jnp.maximum(jnp.abs(den), jnp.exp(-m_t))
            m_new = m_t[L - 1:L, :]
            w_last = jnp.exp(b_last - b_c + ig_c - m_new)
            f_tot = jnp.exp(b_last + m_prev - m_new)
            kw = k * w_last
            c_ref[g, h] = f_tot * cm + _dot(kw, v, _TN)
            n_ref[g, h:h + 1, :] = f_tot * nv + jnp.sum(kw, axis=0, keepdims=True)
            m_ref[g, 0:1, h:h + 1] = m_new
            mix_ref[g, :, h * DV_B:(h + 1) * DV_B] = _rms(hh, ng_ref[h:h + 1, :]) * _sigmoid(og)

    cnew_ref[...] = c_ref[:, :, 0:DK_B, :]
    nnew_ref[...] = n_ref[:, 0:H_B, 0:DK_B]
    mnew_ref[...] = m_ref[:, 0:1, 0:H_B]


def _mlstm(ml_in, gates, gates_t, ng, *, L, G, state=None):
    n_seq, T, _ = ml_in.shape
    n_c = T // L
    has_state = state is not None
    in_specs = [
        pl.BlockSpec((G, L, MLP_W), lambda b, c: (b, c, 0)),
        pl.BlockSpec((G, L, LANE), lambda b, c: (b, c, 0)),
        pl.BlockSpec((G, 1, N_GATE, L), lambda b, c: (b, c, 0, 0)),
        pl.BlockSpec((H_B, DV_B), lambda b, c: (0, 0)),
    ]
    args = [ml_in, gates, gates_t, ng]
    if has_state:
        c0, n0, m0 = state
        in_specs += [
            pl.BlockSpec((G, H_B, DK_B, DV_B), lambda b, c: (b, 0, 0, 0)),
            pl.BlockSpec((G, H_B, DK_B), lambda b, c: (b, 0, 0)),
            pl.BlockSpec((G, 1, H_B), lambda b, c: (b, 0, 0)),
        ]
        args += [c0, n0, m0]
    return pl.pallas_call(
        functools.partial(_mlstm_kernel, L=L, G=G, has_state=has_state),
        grid=(n_seq // G, n_c),
        in_specs=in_specs,
        out_specs=[
            pl.BlockSpec((G, L, H_B * DV_B), lambda b, c: (b, c, 0)),
            pl.BlockSpec((G, H_B, DK_B, DV_B), lambda b, c: (b, 0, 0, 0)),
            pl.BlockSpec((G, H_B, DK_B), lambda b, c: (b, 0, 0)),
            pl.BlockSpec((G, 1, H_B), lambda b, c: (b, 0, 0)),
        ],
        out_shape=[
            jax.ShapeDtypeStruct((n_seq, T, H_B * DV_B), F32),
            jax.ShapeDtypeStruct((n_seq, H_B, DK_B, DV_B), F32),
            jax.ShapeDtypeStruct((n_seq, H_B, DK_B), F32),
            jax.ShapeDtypeStruct((n_seq, 1, H_B), F32),
        ],
        scratch_shapes=[pltpu.VMEM((G, H_B, LANE, DV_B), F32), pltpu.VMEM((G, SUBLANE, LANE), F32),
                        pltpu.VMEM((G, SUBLANE, LANE), F32)],
        compiler_params=pltpu.CompilerParams(dimension_semantics=("parallel", "arbitrary"),
                                             vmem_limit_bytes=VMEM_LIMIT),
        name=f"mlstm_L{L}",
    )(*args)


def _outproj_kernel(xp_ref, xs_ref, map_ref, mas_ref, mbp_ref, mbs_ref, wo_ref, g_ref, rw_ref, rb_ref,
                    x1_ref, hn_ref, lg_ref, *, n_p_tiles):
    half = H_A * DV_A

    def body(seg):
        x_ref, ma_ref, mb_ref = (xp_ref, xs_ref)[seg], (map_ref, mas_ref)[seg], (mbp_ref, mbs_ref)[seg]
        x1 = (x_ref[...] + jnp.dot(ma_ref[...].astype(BF16), wo_ref[:half, :], preferred_element_type=F32)
              + jnp.dot(mb_ref[...].astype(BF16), wo_ref[half:, :], preferred_element_type=F32))
        x1_ref[...] = x1
        hn = _rms(x1, g_ref[...])
        hn_ref[...] = hn
        lg_ref[...] = _dot_hi(hn, rw_ref[...]) + rb_ref[...]

    _for_segment(n_p_tiles, body)


def _outproj(xp, xs, ma_p, ma_s, mb_p, mb_s, w_out, g, rw, rb, tm):
    n_p, n_s = xp.shape[0], xs.shape[0]
    n = n_p + n_s
    npt = n_p // tm
    half = H_A * DV_A
    return pl.pallas_call(
        functools.partial(_outproj_kernel, n_p_tiles=npt),
        grid=(n // tm,),
        in_specs=_two_segment_specs(tm, D_MODEL, npt) + _two_segment_specs(tm, half, npt)
        + _two_segment_specs(tm, half, npt) + [
            pl.BlockSpec((D_MODEL, D_MODEL), lambda i: (0, 0)),
            pl.BlockSpec((1, D_MODEL), lambda i: (0, 0)),
            pl.BlockSpec((D_MODEL, LANE), lambda i: (0, 0)),
            pl.BlockSpec((1, LANE), lambda i: (0, 0)),
        ],
        out_specs=[
            pl.BlockSpec((tm, D_MODEL), lambda i: (i, 0)),
            pl.BlockSpec((tm, D_MODEL), lambda i: (i, 0)),
            pl.BlockSpec((tm, LANE), lambda i: (i, 0)),
        ],
        out_shape=[
            jax.ShapeDtypeStruct((n, D_MODEL), F32),
            jax.ShapeDtypeStruct((n, D_MODEL), F32),
            jax.ShapeDtypeStruct((n, LANE), F32),
        ],
        compiler_params=pltpu.CompilerParams(dimension_semantics=("arbitrary",), vmem_limit_bytes=VMEM_LIMIT),
        name="outproj",
    )(xp, xs, ma_p, ma_s, mb_p, mb_s, w_out, g, rw, rb)


def _expert_kernel(be_ref, nu_ref, rt_ref, rtn_ref, x_hbm, wgu_ref, bgu_ref, wd_ref, bd_ref,
                   y_ref, xbuf, sem, wgu_bf, wd_bf, *, tb):
    b = pl.program_id(0)
    n_used = nu_ref[0]
    slot = b % 2

    def start_gather(idx_ref, s):
        def body(r, carry):
            tok = idx_ref[0, 0, r]
            pltpu.make_async_copy(x_hbm.at[pl.ds(tok, 1)], xbuf.at[s, pl.ds(r, 1)], sem.at[s]).start()
            return carry
        lax.fori_loop(0, tb, body, 0)

    @pl.when(b == 0)
    def _():
        start_gather(rt_ref, 0)

    @pl.when(b + 1 < n_used)
    def _():
        start_gather(rtn_ref, 1 - slot)

    @pl.when(b >= n_used)
    def _():
        y_ref[...] = jnp.zeros_like(y_ref)

    @pl.when(b < n_used)
    def _():
        pltpu.make_async_copy(x_hbm.at[pl.ds(0, tb)], xbuf.at[slot], sem.at[slot]).wait()
        e_prev = be_ref[jnp.maximum(b - 1, 0)]

        @pl.when(jnp.logical_or(b == 0, be_ref[b] != e_prev))
        def _():
            wgu_bf[...] = wgu_ref[0].astype(BF16)
            wd_bf[...] = wd_ref[0].astype(BF16)

        xb = xbuf[slot].astype(BF16)
        hgu = jnp.dot(xb, wgu_bf[...], preferred_element_type=F32) + bgu_ref[0]
        gate = jnp.minimum(hgu[:, :D_FF], SWIGLU_LIMIT)
        up = jnp.clip(hgu[:, D_FF:], -SWIGLU_LIMIT, SWIGLU_LIMIT)
        act = (up + 1.0) * gate * _sigmoid(SWIGLU_ALPHA * gate)
        y_ref[...] = jnp.dot(act.astype(BF16), wd_bf[...], preferred_element_type=F32) + bd_ref[0]


def _experts(block_e, n_used, row_tok3, x_src, w_gu, b_gu, w_down, b_down, tb):
    nb = row_tok3.shape[0]

    def cur(b, nu):
        return jnp.minimum(b, nu[0] - 1)

    grid_spec = pltpu.PrefetchScalarGridSpec(
        num_scalar_prefetch=2,
        grid=(nb,),
        in_specs=[
            pl.BlockSpec((1, 1, tb), lambda b, be, nu: (cur(b, nu), 0, 0), memory_space=pltpu.SMEM),
            pl.BlockSpec((1, 1, tb), lambda b, be, nu: (cur(b + 1, nu), 0, 0), memory_space=pltpu.SMEM),
            pl.BlockSpec(memory_space=pl.ANY),
            pl.BlockSpec((1, D_MODEL, 2 * D_FF), lambda b, be, nu: (be[cur(b, nu)], 0, 0)),
            pl.BlockSpec((1, 1, 2 * D_FF), lambda b, be, nu: (be[cur(b, nu)], 0, 0)),
            pl.BlockSpec((1, D_FF, D_MODEL), lambda b, be, nu: (be[cur(b, nu)], 0, 0)),
            pl.BlockSpec((1, 1, D_MODEL), lambda b, be, nu: (be[cur(b, nu)], 0, 0)),
        ],
        out_specs=pl.BlockSpec((tb, D_MODEL), lambda b, be, nu: (b, 0)),
        scratch_shapes=[
            pltpu.VMEM((2, tb, D_MODEL), F32),
            pltpu.SemaphoreType.DMA((2,)),
            pltpu.VMEM((D_MODEL, 2 * D_FF), BF16),
            pltpu.VMEM((D_FF, D_MODEL), BF16),
        ],
    )
    return pl.pallas_call(
        functools.partial(_expert_kernel, tb=tb),
        grid_spec=grid_spec,
        out_shape=jax.ShapeDtypeStruct((nb * tb, D_MODEL), F32),
        compiler_params=pltpu.CompilerParams(dimension_semantics=("arbitrary",), vmem_limit_bytes=VMEM_LIMIT),
        name="experts",
    )(block_e, n_used, row_tok3, row_tok3, x_src, w_gu, b_gu, w_down, b_down)


def _combine_kernel(pos_ref, posn_ref, y_hbm, x1_ref, gt_ref, pp_ref, ps_ref, gple_ref, wg_ref, wp_ref, gfin_ref,
                    outp_ref, outs_ref, ybuf, sem, *, tm, n_p_tiles):
    i = pl.program_id(0)
    n_i = pl.num_programs(0)
    slot = i % 2

    def start_gather(idx_ref, s):
        def body(r, carry):
            for kk in range(TOP_K):
                row = idx_ref[0, 0, kk * tm + r]
                pltpu.make_async_copy(y_hbm.at[pl.ds(row, 1)], ybuf.at[s, kk, pl.ds(r, 1)], sem.at[s]).start()
            return carry
        lax.fori_loop(0, tm, body, 0)

    @pl.when(i == 0)
    def _():
        start_gather(pos_ref, 0)

    @pl.when(i + 1 < n_i)
    def _():
        start_gather(posn_ref, 1 - slot)

    for kk in range(TOP_K):
        pltpu.make_async_copy(y_hbm.at[pl.ds(0, tm)], ybuf.at[slot, kk], sem.at[slot]).wait()

    def body(seg):
        p_ref, out_ref = (pp_ref, ps_ref)[seg], (outp_ref, outs_ref)[seg]
        gt = gt_ref[...]
        x2 = x1_ref[...]
        for kk in range(TOP_K):
            x2 = x2 + gt[:, kk:kk + 1] * ybuf[slot, kk]
        hn = _rms(x2, gple_ref[...]).astype(BF16)
        gate = _sigmoid(jnp.dot(hn, wg_ref[...], preferred_element_type=F32))
        pe = jnp.dot(p_ref[...].astype(BF16), wp_ref[...], preferred_element_type=F32)
        x3 = x2 + gate * pe
        out_ref[...] = _rms(x3, gfin_ref[...])

    _for_segment(n_p_tiles, body)


def _combine(pos3, y_rows, x1, gates_pad, pp, ps, g_ple, w_gate, w_p, g_fin, tm):
    n_p, n_s = pp.shape[0], ps.shape[0]
    n = n_p + n_s
    nt = n // tm
    npt = n_p // tm
    return pl.pallas_call(
        functools.partial(_combine_kernel, tm=tm, n_p_tiles=npt),
        grid=(nt,),
        in_specs=[
            pl.BlockSpec((1, 1, TOP_K * tm), lambda i: (i, 0, 0), memory_space=pltpu.SMEM),
            pl.BlockSpec((1, 1, TOP_K * tm), lambda i: (jnp.minimum(i + 1, nt - 1), 0, 0), memory_space=pltpu.SMEM),
            pl.BlockSpec(memory_space=pl.ANY),
            pl.BlockSpec((tm, D_MODEL), lambda i: (i, 0)),
            pl.BlockSpec((tm, LANE), lambda i: (i, 0)),
        ] + _two_segment_specs(tm, PLE_DIM, npt) + [
            pl.BlockSpec((1, D_MODEL), lambda i: (0, 0)),
            pl.BlockSpec((D_MODEL, D_MODEL), lambda i: (0, 0)),
            pl.BlockSpec((PLE_DIM, D_MODEL), lambda i: (0, 0)),
            pl.BlockSpec((1, D_MODEL), lambda i: (0, 0)),
        ],
        out_specs=_two_segment_specs(tm, D_MODEL, npt),
        out_shape=[jax.ShapeDtypeStruct((n_p, D_MODEL), F32), jax.ShapeDtypeStruct((n_s, D_MODEL), F32)],
        scratch_shapes=[pltpu.VMEM((2, TOP_K, tm, D_MODEL), F32), pltpu.SemaphoreType.DMA((2,))],
        compiler_params=pltpu.CompilerParams(dimension_semantics=("arbitrary",), vmem_limit_bytes=VMEM_LIMIT),
        name="combine",
    )(pos3, pos3, y_rows, x1, gates_pad, pp, ps, g_ple, w_gate, w_p, g_fin)


def _routing(logits, tb, nb):
    n = logits.shape[0]
    top_val, top_idx = lax.top_k(logits, TOP_K)
    gates = jax.nn.softmax(top_val, axis=-1)
    onehot = (top_idx[:, :, None] == jnp.arange(N_EXPERTS, dtype=jnp.int32)[None, None, :])
    mask = jnp.sum(onehot.astype(jnp.int32), axis=1)
    csum = jnp.cumsum(mask, axis=0)
    counts = csum[-1]
    rank = csum - mask
    padded = (counts + tb - 1) // tb * tb
    pad_end = jnp.cumsum(padded)
    pad_start = pad_end - padded
    pos = jnp.take_along_axis(pad_start[None, :] + rank, top_idx, axis=1).astype(jnp.int32)
    tok = jnp.broadcast_to(jnp.arange(n, dtype=jnp.int32)[:, None], (n, TOP_K))
    row_tok = jnp.zeros((nb * tb,), jnp.int32).at[pos.reshape(-1)].set(tok.reshape(-1))
    block_e = jnp.minimum(jnp.searchsorted(pad_end, jnp.arange(nb, dtype=pad_end.dtype) * tb, side='right'),
                          N_EXPERTS - 1).astype(jnp.int32)
    n_used = (pad_end[-1] // tb).astype(jnp.int32).reshape(1)
    return gates, pos, row_tok, block_e, n_used


def _rearranged_in_weights(w_in):
    o = np.cumsum([0, CONV_CH, H_A * DV_A, H_A, H_A, H_B * DK_B, H_B * DK_B, H_B * DV_B, H_B * DV_B, H_B, H_B])
    conv_in, z_a, a_a, b_a, q_b, k_b, v_b, o_b, i_b, f_b = (w_in[:, int(o[j]):int(o[j + 1])] for j in range(10))
    zpad = jnp.zeros((D_MODEL, LANE - DK_B), w_in.dtype)

    def pad_heads(w):
        return jnp.concatenate([jnp.concatenate([w[:, h * DK_B:(h + 1) * DK_B], zpad], axis=1) for h in range(H_B)],
                               axis=1)

    small = jnp.concatenate([a_a, b_a, i_b, f_b], axis=1)
    w_all = jnp.concatenate([conv_in, z_a, pad_heads(q_b), pad_heads(k_b), v_b, o_b,
                             small, jnp.zeros((D_MODEL, LANE - N_GATE), w_in.dtype)], axis=1)
    return w_all.astype(BF16), small.T.astype(BF16)


def _gate_params(a_log, dt_bias, i_bias, f_bias):
    z4 = jnp.zeros((4,), F32)
    alog = jnp.concatenate([a_log.astype(F32), z4, z4, z4])
    bias = jnp.concatenate([dt_bias.astype(F32), z4, i_bias.astype(F32), f_bias.astype(F32)])
    pad = jnp.zeros((LANE - N_GATE,), F32)
    pcol = jnp.zeros((SUBLANE, LANE), F32).at[0].set(jnp.concatenate([alog, pad])).at[1].set(
        jnp.concatenate([bias, pad]))
    prow = jnp.zeros((N_GATE, LANE), F32).at[:, 0].set(alog).at[:, 1].set(bias)
    return pcol, prow


def kernel(x_prompt, x_sample, p_prompt, p_sample, state_conv, state_gdn, state_mlstm_c, state_mlstm_n, state_mlstm_m, norm_attn_g, w_in, conv_w, gdn_a_log, gdn_dt_bias, gdn_norm_g, mlstm_i_bias, mlstm_f_bias, mlstm_norm_g, w_out, norm_moe_g, router_w, router_b, expert_w_gu, expert_b_gu, expert_w_down, expert_b_down, norm_ple_g, ple_gate_w, ple_w, final_norm_g):
    bp, tp, _ = x_prompt.shape
    bs, ts, _ = x_sample.shape
    n_p, n_s = bp * tp, bs * ts
    n = n_p + n_s
    lp, ls = min(tp, CHUNK), min(ts, CHUNK)
    tm = 512
    gp = 2 if bp % 2 == 0 else 1
    gs = 8 if bs % 8 == 0 else 1
    assert tp % lp == 0 and ts % ls == 0 and n_p % tm == 0 and n_s % tm == 0 and ls % SUBLANE == 0

    xp = x_prompt.reshape(n_p, D_MODEL)
    xs = x_sample.reshape(n_s, D_MODEL)

    w_all, ws_t = _rearranged_in_weights(w_in[0])
    pcol, prow = _gate_params(gdn_a_log[0], gdn_dt_bias[0], mlstm_i_bias[0], mlstm_f_bias[0])
    gdn_p, gdn_s, ml_p, ml_s, gate_p, gate_s, gatet_p, gatet_s = _inproj(
        xp, xs, norm_attn_g[0].reshape(1, D_MODEL), w_all, ws_t, pcol, prow, tm)
    gt_p = gatet_p.reshape(N_GATE, bp, tp // lp, lp).transpose(1, 2, 0, 3)
    gt_s = gatet_s.reshape(N_GATE, bs, ts // ls, ls).transpose(1, 2, 0, 3)

    cw = jnp.zeros((SUBLANE, CONV_CH), F32).at[:CONV_W].set(conv_w[0].astype(F32))
    ng_a = gdn_norm_g[0].reshape(1, DV_A).astype(F32)
    ng_b = mlstm_norm_g[0].reshape(H_B, DV_B).astype(F32)
    ma_p, conv_p, gdn_st_p = _gdn(gdn_p.reshape(bp, tp, GDN_W), gate_p.reshape(bp, tp, LANE), gt_p, cw, ng_a,
                                  L=lp, G=gp)
    ma_s, conv_s, gdn_st_s = _gdn(gdn_s.reshape(bs, ts, GDN_W), gate_s.reshape(bs, ts, LANE), gt_s, cw, ng_a,
                                  L=ls, G=gs, state=(state_conv[0], state_gdn[0]))
    mb_p, c_p, nn_p, m_p = _mlstm(ml_p.reshape(bp, tp, MLP_W), gate_p.reshape(bp, tp, LANE), gt_p, ng_b,
                                  L=lp, G=gp)
    mb_s, c_s, nn_s, m_s = _mlstm(ml_s.reshape(bs, ts, MLP_W), gate_s.reshape(bs, ts, LANE), gt_s, ng_b,
                                  L=ls, G=gs,
                                  state=(state_mlstm_c[0], state_mlstm_n[0], state_mlstm_m[0].reshape(bs, 1, H_B)))
    half = H_A * DV_A

    rw = jnp.zeros((D_MODEL, LANE), F32).at[:, :N_EXPERTS].set(router_w[0])
    rb = jnp.full((1, LANE), NEG, F32).at[0, :N_EXPERTS].set(router_b[0])
    x1, hn2, logits = _outproj(xp, xs, ma_p.reshape(n_p, half), ma_s.reshape(n_s, half),
                               mb_p.reshape(n_p, half), mb_s.reshape(n_s, half),
                               w_out[0].astype(BF16), norm_moe_g[0].reshape(1, D_MODEL), rw, rb, tm)

    tb = 256
    nb = n * TOP_K // tb + N_EXPERTS
    gates_k, pos, row_tok, block_e, n_used = _routing(logits[:, :N_EXPERTS], tb, nb)
    y_rows = _experts(block_e, n_used, row_tok.reshape(nb, 1, tb), hn2,
                      expert_w_gu[0], expert_b_gu[0].reshape(N_EXPERTS, 1, 2 * D_FF),
                      expert_w_down[0], expert_b_down[0].reshape(N_EXPERTS, 1, D_MODEL), tb)
    tm5 = 256
    pos3 = pos.reshape(n // tm5, tm5, TOP_K).transpose(0, 2, 1).reshape(n // tm5, 1, TOP_K * tm5)
    gates_pad = jnp.zeros((n, LANE), F32).at[:, :TOP_K].set(gates_k)
    y_p, y_s = _combine(pos3, y_rows, x1, gates_pad, p_prompt[0].reshape(n_p, PLE_DIM),
                        p_sample[0].reshape(n_s, PLE_DIM), norm_ple_g[0].reshape(1, D_MODEL),
                        ple_gate_w[0].astype(BF16), ple_w[0].astype(BF16), final_norm_g.reshape(1, D_MODEL), tm5)

    return (y_p.reshape(bp, tp, D_MODEL), y_s.reshape(bs, ts, D_MODEL),
            conv_p[None], gdn_st_p[None], c_p[None], nn_p[None], m_p.reshape(1, bp, H_B),
            conv_s[None], gdn_st_s[None], c_s[None], nn_s[None], m_s.reshape(1, bs, H_B))
```

```python
import functools

import numpy as np
import jax
import jax.numpy as jnp
from jax import lax
from jax.experimental import pallas as pl
from jax.experimental.pallas import tpu as pltpu

F32 = jnp.float32
BF16 = jnp.bfloat16

D_MODEL = 1024
H_A, DK_A, DV_A = 4, 128, 128
H_B, DK_B, DV_B = 4, 64, 128
CONV_W = 4
CONV_CH = H_A * (2 * DK_A + DV_A)
N_EXPERTS = 32
TOP_K = 4
D_FF = 1024
SWIGLU_LIMIT = 7.0
SWIGLU_ALPHA = 1.702
PLE_DIM = 256
EPS = 1e-6
NEG = -1e30
CHUNK = 64

LANE = 128
SUBLANE = 8
GDN_W = CONV_CH + H_A * DV_A
MLP_W = 2 * H_B * LANE + 2 * H_B * DV_B
N_GATE = 16
W_ALL = GDN_W + MLP_W + LANE

VMEM_LIMIT = 48 * 1024 * 1024

HI = lax.Precision.HIGHEST

_NN = (((1,), (0,)), ((), ()))
_NT = (((1,), (1,)), ((), ()))
_TN = (((0,), (0,)), ((), ()))


def _dot(a, b, dims=_NN):
    return lax.dot_general(a.astype(BF16), b.astype(BF16), dims, preferred_element_type=F32)


def _dot_hi(a, b, dims=_NN):
    return lax.dot_general(a, b, dims, precision=HI, preferred_element_type=F32)


def _rms(x, g):
    return x * lax.rsqrt(jnp.mean(x * x, axis=-1, keepdims=True) + EPS) * g


def _softplus(t):
    return jnp.maximum(t, 0.0) + jnp.log1p(jnp.exp(-jnp.abs(t)))


def _sigmoid(t):
    return 1.0 / (1.0 + jnp.exp(-t))


def _silu(t):
    return t * _sigmoid(t)


def _activate_gates(raw, idx, alog, bias):
    t = raw + bias
    g = -jnp.exp(alog) * _softplus(t)
    beta = _sigmoid(t)
    lf = -_softplus(-t)
    return jnp.where(idx < 4, g, jnp.where(idx < 8, beta, jnp.where(idx < 12, t, lf)))


def _two_segment_specs(tm, width, n_p_tiles):
    return [pl.BlockSpec((tm, width), lambda i: (jnp.minimum(i, n_p_tiles - 1), 0)),
            pl.BlockSpec((tm, width), lambda i: (jnp.maximum(i - n_p_tiles, 0), 0))]


def _for_segment(n_p_tiles, body):
    i = pl.program_id(0)

    @pl.when(i < n_p_tiles)
    def _():
        body(0)

    @pl.when(i >= n_p_tiles)
    def _():
        body(1)


def _inproj_kernel(xp_ref, xs_ref, g_ref, w_ref, wst_ref, pc_ref, pr_ref,
                   gdnp_ref, gdns_ref, mlp_ref, mls_ref, gatep_ref, gates_ref, gatetp_ref, gatets_ref,
                   *, n_p_tiles):
    tm = xp_ref.shape[0]

    def body(seg):
        x_ref = (xp_ref, xs_ref)[seg]
        gdn_ref, ml_ref = (gdnp_ref, gdns_ref)[seg], (mlp_ref, mls_ref)[seg]
        gate_ref, gatet_ref = (gatep_ref, gates_ref)[seg], (gatetp_ref, gatets_ref)[seg]
        hn = _rms(x_ref[...], g_ref[...]).astype(BF16)
        gdn_ref[...] = jnp.dot(hn, w_ref[:, :GDN_W], preferred_element_type=F32)
        ml_ref[...] = jnp.dot(hn, w_ref[:, GDN_W:GDN_W + MLP_W], preferred_element_type=F32)
        raw = jnp.dot(hn, w_ref[:, GDN_W + MLP_W:], preferred_element_type=F32)
        lane = lax.broadcasted_iota(jnp.int32, (tm, LANE), 1)
        gate_ref[...] = _activate_gates(raw, lane, pc_ref[0:1, :], pc_ref[1:2, :])
        raw_t = lax.dot_general(wst_ref[...], hn, _NT, preferred_element_type=F32)
        row = lax.broadcasted_iota(jnp.int32, (N_GATE, tm), 0)
        gatet_ref[...] = _activate_gates(raw_t, row, pr_ref[:, 0:1], pr_ref[:, 1:2])

    _for_segment(n_p_tiles, body)


def _inproj(xp, xs, g, w_all, ws_t, pcol, prow, tm):
    n_p, n_s = xp.shape[0], xs.shape[0]
    npt = n_p // tm

    def out2(width):
        return _two_segment_specs(tm, width, npt)

    def shp2(width):
        return [jax.ShapeDtypeStruct((n_p, width), F32), jax.ShapeDtypeStruct((n_s, width), F32)]

    return pl.pallas_call(
        functools.partial(_inproj_kernel, n_p_tiles=npt),
        grid=((n_p + n_s) // tm,),
        in_specs=_two_segment_specs(tm, D_MODEL, npt) + [
            pl.BlockSpec((1, D_MODEL), lambda i: (0, 0)),
            pl.BlockSpec((D_MODEL, W_ALL), lambda i: (0, 0)),
            pl.BlockSpec((N_GATE, D_MODEL), lambda i: (0, 0)),
            pl.BlockSpec((SUBLANE, LANE), lambda i: (0, 0)),
            pl.BlockSpec((N_GATE, LANE), lambda i: (0, 0)),
        ],
        out_specs=out2(GDN_W) + out2(MLP_W) + out2(LANE) + [
            pl.BlockSpec((N_GATE, tm), lambda i: (0, jnp.minimum(i, npt - 1))),
            pl.BlockSpec((N_GATE, tm), lambda i: (0, jnp.maximum(i - npt, 0))),
        ],
        out_shape=shp2(GDN_W) + shp2(MLP_W) + shp2(LANE) + [
            jax.ShapeDtypeStruct((N_GATE, n_p), F32), jax.ShapeDtypeStruct((N_GATE, n_s), F32)],
        compiler_params=pltpu.CompilerParams(dimension_semantics=("arbitrary",), vmem_limit_bytes=VMEM_LIMIT),
        name="inproj",
    )(xp, xs, g, w_all, ws_t, pcol, prow)


def _chunk_masks(L):
    ri = lax.broadcasted_iota(jnp.int32, (L, L), 0)
    ci = lax.broadcasted_iota(jnp.int32, (L, L), 1)
    return ri >= ci, ri > ci, ri <= ci


def _gdn_kernel(*refs, L, G, has_state):
    if has_state:
        (xin_ref, gate_ref, gatet_ref, cw_ref, ng_ref, cst_ref, s0_ref,
         mix_ref, cnew_ref, snew_ref, xc_ref, s_ref) = refs
    else:
        (xin_ref, gate_ref, gatet_ref, cw_ref, ng_ref,
         mix_ref, cnew_ref, snew_ref, xc_ref, s_ref) = refs
    c = pl.program_id(1)

    @pl.when(c == 0)
    def _():
        xc_ref[:, 0:SUBLANE, :] = jnp.zeros((G, SUBLANE, CONV_CH), F32)
        if has_state:
            xc_ref[:, SUBLANE - (CONV_W - 1):SUBLANE, :] = cst_ref[...]
            s_ref[...] = s0_ref[...]
        else:
            s_ref[...] = jnp.zeros_like(s_ref)

    @pl.when(c > 0)
    def _():
        xc_ref[:, 0:SUBLANE, :] = xc_ref[:, L:L + SUBLANE, :]

    tril, strict, triu = _chunk_masks(L)
    tril_f, triu_f = tril.astype(F32), triu.astype(F32)
    base = SUBLANE - (CONV_W - 1)

    chains = [(g, h) for g in range(G) for h in range(H_A)]
    s_old = [s_ref[g, h] for g, h in chains]
    for g in range(G):
        xc_ref[g, SUBLANE:SUBLANE + L, :] = xin_ref[g, :, :CONV_CH]

    q, k, v, beta, gc, gl, decay = [], [], [], [], [], [], []
    for g in range(G):
        conv = xc_ref[g, base:base + L, :] * cw_ref[0:1, :]
        for j in range(1, CONV_W):
            conv = conv + xc_ref[g, base + j:base + j + L, :] * cw_ref[j:j + 1, :]
        cnew_ref[g] = xc_ref[g, SUBLANE + L - (CONV_W - 1):SUBLANE + L, :]
        act = _silu(conv)
        gact = gate_ref[g]
        cum_c = _dot_hi(tril_f, gact)
        cum_r = _dot_hi(gatet_ref[g, 0], triu_f)
        for h in range(H_A):
            q.append(act[:, h * DK_A:(h + 1) * DK_A])
            k.append(act[:, H_A * DK_A + h * DK_A:H_A * DK_A + (h + 1) * DK_A])
            v.append(act[:, 2 * H_A * DK_A + h * DV_A:2 * H_A * DK_A + (h + 1) * DV_A])
            beta.append(gact[:, 4 + h:5 + h])
            gc.append(cum_c[:, h:h + 1])
            gl.append(cum_c[L - 1:L, h:h + 1])
            gr = cum_r[h:h + 1, :]
            decay.append(jnp.where(tril, jnp.exp(jnp.where(tril, cum_c[:, h:h + 1] - gr, 0.0)), 0.0))

    nc = range(len(chains))
    qss = [jnp.sum(q[i] * q[i], axis=-1, keepdims=True) for i in nc]
    kss = [jnp.sum(k[i] * k[i], axis=-1, keepdims=True) for i in nc]
    q = [q[i] * (lax.rsqrt(qss[i] + EPS) * (DK_A ** -0.5)) for i in nc]
    k = [k[i] * lax.rsqrt(kss[i] + EPS) for i in nc]
    kb = [k[i] * beta[i] for i in nc]
    egc = [jnp.exp(gc[i]) for i in nc]
    kk = [_dot(kb[i], k[i], _NT) for i in nc]
    qk = [_dot(q[i], k[i], _NT) for i in nc]
    eye = (lax.broadcasted_iota(jnp.int32, (L, L), 0) == lax.broadcasted_iota(jnp.int32, (L, L), 1)).astype(F32)
    pw = [-jnp.where(strict, kk[i] * decay[i], 0.0) for i in nc]
    t_inv = [eye + pw[i] for i in nc]
    span = 2
    while span < L:
        pw = [_dot(pw[i], pw[i]) for i in nc]
        t_inv = [t_inv[i] + _dot(t_inv[i], pw[i]) for i in nc]
        span *= 2
    sol = [_dot(t_inv[i], jnp.concatenate([v[i] * beta[i], kb[i] * egc[i]], axis=-1)) for i in nc]
    qs = [_dot(q[i] * egc[i], s_old[i]) for i in nc]
    ws = [_dot(sol[i][:, DV_A:], s_old[i]) for i in nc]
    v_new = [sol[i][:, :DV_A] - ws[i] for i in nc]
    o = [qs[i] + _dot(jnp.where(tril, qk[i] * decay[i], 0.0), v_new[i]) for i in nc]
    s_new = [s_old[i] * jnp.exp(gl[i]) + _dot(k[i] * jnp.exp(gl[i] - gc[i]), v_new[i], _TN) for i in nc]
    ms = [jnp.mean(o[i] * o[i], axis=-1, keepdims=True) for i in nc]
    on = [o[i] * lax.rsqrt(ms[i] + EPS) for i in nc]
    for i, (g, h) in enumerate(chains):
        z = xin_ref[g, :, CONV_CH + h * DV_A:CONV_CH + (h + 1) * DV_A]
        mix_ref[g, :, h * DV_A:(h + 1) * DV_A] = on[i] * ng_ref[...] * _silu(z)
    for i, (g, h) in enumerate(chains):
        s_ref[g, h] = s_new[i]
        snew_ref[g, h] = s_new[i]


def _gdn(gdn_in, gates, gates_t, cw, ng, *, L, G, state=None):
    n_seq, T, _ = gdn_in.shape
    n_c = T // L
    has_state = state is not None
    in_specs = [
        pl.BlockSpec((G, L, GDN_W), lambda b, c: (b, c, 0)),
        pl.BlockSpec((G, L, LANE), lambda b, c: (b, c, 0)),
        pl.BlockSpec((G, 1, N_GATE, L), lambda b, c: (b, c, 0, 0)),
        pl.BlockSpec((SUBLANE, CONV_CH), lambda b, c: (0, 0)),
        pl.BlockSpec((1, DV_A), lambda b, c: (0, 0)),
    ]
    args = [gdn_in, gates, gates_t, cw, ng]
    if has_state:
        conv_st, s0 = state
        in_specs += [
            pl.BlockSpec((G, CONV_W - 1, CONV_CH), lambda b, c: (b, 0, 0)),
            pl.BlockSpec((G, H_A, DK_A, DV_A), lambda b, c: (b, 0, 0, 0)),
        ]
        args += [conv_st, s0]
    return pl.pallas_call(
        functools.partial(_gdn_kernel, L=L, G=G, has_state=has_state),
        grid=(n_seq // G, n_c),
        in_specs=in_specs,
        out_specs=[
            pl.BlockSpec((G, L, H_A * DV_A), lambda b, c: (b, c, 0)),
            pl.BlockSpec((G, CONV_W - 1, CONV_CH), lambda b, c: (b, 0, 0)),
            pl.BlockSpec((G, H_A, DK_A, DV_A), lambda b, c: (b, 0, 0, 0)),
        ],
        out_shape=[
            jax.ShapeDtypeStruct((n_seq, T, H_A * DV_A), F32),
            jax.ShapeDtypeStruct((n_seq, CONV_W - 1, CONV_CH), F32),
            jax.ShapeDtypeStruct((n_seq, H_A, DK_A, DV_A), F32),
        ],
        scratch_shapes=[pltpu.VMEM((G, L + SUBLANE, CONV_CH), F32), pltpu.VMEM((G, H_A, DK_A, DV_A), F32)],
        compiler_params=pltpu.CompilerParams(dimension_semantics=("parallel", "arbitrary"),
                                             vmem_limit_bytes=VMEM_LIMIT),
        name=f"gdn_L{L}",
    )(*args)


def _mlstm_kernel(*refs, L, G, has_state):
    if has_state:
        (xin_ref, gate_ref, gatet_ref, ng_ref, c0_ref, n0_ref, m0_ref,
         mix_ref, cnew_ref, nnew_ref, mnew_ref, c_ref, n_ref, m_ref) = refs
    else:
        (xin_ref, gate_ref, gatet_ref, ng_ref,
         mix_ref, cnew_ref, nnew_ref, mnew_ref, c_ref, n_ref, m_ref) = refs
    c = pl.program_id(1)

    @pl.when(c == 0)
    def _():
        c_ref[...] = jnp.zeros_like(c_ref)
        n_ref[...] = jnp.zeros_like(n_ref)
        m_ref[...] = jnp.zeros_like(m_ref)
        if has_state:
            c_ref[:, :, 0:DK_B, :] = c0_ref[...]
            n_ref[:, 0:H_B, 0:DK_B] = n0_ref[...]
            m_ref[:, 0:1, 0:H_B] = m0_ref[...]

    tril, _, triu = _chunk_masks(L)
    tril_f, triu_f = tril.astype(F32), triu.astype(F32)

    chains = [(g, h) for g in range(G) for h in range(H_B)]
    nc = range(len(chains))
    c_old = [c_ref[g, h] for g, h in chains]
    n_old = [n_ref[g, h:h + 1, :] for g, h in chains]
    m_old = [m_ref[g, 0:1, h:h + 1] for g, h in chains]

    v0 = 2 * H_B * LANE
    q = [xin_ref[g, :, h * LANE:(h + 1) * LANE] * (DK_B ** -0.5) for g, h in chains]
    k = [xin_ref[g, :, (H_B + h) * LANE:(H_B + h + 1) * LANE] for g, h in chains]
    v = [xin_ref[g, :, v0 + h * DV_B:v0 + (h + 1) * DV_B] for g, h in chains]
    ig_c, b_c, b_last, d_log = [], [], [], []
    for g in range(G):
        gact = gate_ref[g]
        gact_t = gatet_ref[g, 0]
        cum_c = _dot_hi(tril_f, gact)
        cum_r = _dot_hi(gact_t, triu_f)
        for h in range(H_B):
            ig_c.append(gact[:, 8 + h:9 + h])
            b_c.append(cum_c[:, 12 + h:13 + h])
            b_last.append(cum_c[L - 1:L, 12 + h:13 + h])
            d_log.append(jnp.where(tril, cum_c[:, 12 + h:13 + h] - cum_r[12 + h:13 + h, :]
                                   + gact_t[8 + h:9 + h, :], NEG))
    qk = [_dot(q[i], k[i], _NT) for i in nc]
    qc = [_dot(q[i], c_old[i]) for i in nc]
    inter = [b_c[i] + m_old[i] for i in nc]
    m_t = [jnp.maximum(inter[i], jnp.max(d_log[i], axis=-1, keepdims=True)) for i in nc]
    s = [qk[i] * jnp.exp(d_log[i] - m_t[i]) for i in nc]
    e_inter = [jnp.exp(inter[i] - m_t[i]) for i in nc]
    sv = [_dot(s[i], v[i]) for i in nc]
    m_new = [m_t[i][L - 1:L, :] for i in nc]
    kw = [k[i] * jnp.exp(b_last[i] - b_c[i] + ig_c[i] - m_new[i]) for i in nc]
    f_tot = [jnp.exp(b_last[i] + m_old[i] - m_new[i]) for i in nc]
    c_new = [f_tot[i] * c_old[i] + _dot(kw[i], v[i], _TN) for i in nc]
    n_new = [f_tot[i] * n_old[i] + jnp.sum(kw[i], axis=0, keepdims=True) for i in nc]
    qn = [jnp.sum(q[i] * n_old[i], axis=-1, keepdims=True) for i in nc]
    ssum = [jnp.sum(s[i], axis=-1, keepdims=True) for i in nc]
    den = [jnp.maximum(jnp.abs(e_inter[i] * qn[i] + ssum[i]), jnp.exp(-m_t[i])) for i in nc]
    hh = [(e_inter[i] * qc[i] + sv[i]) / den[i] for i in nc]
    ms = [jnp.mean(hh[i] * hh[i], axis=-1, keepdims=True) for i in nc]
    hn = [hh[i] * lax.rsqrt(ms[i] + EPS) for i in nc]
    for i, (g, h) in enumerate(chains):
        og = xin_ref[g, :, v0 + H_B * DV_B + h * DV_B:v0 + H_B * DV_B + (h + 1) * DV_B]
        mix_ref[g, :, h * DV_B:(h + 1) * DV_B] = hn[i] * ng_ref[h:h + 1, :] * _sigmoid(og)
    for i, (g, h) in enumerate(chains):
        c_ref[g, h] = c_new[i]
        n_ref[g, h:h + 1, :] = n_new[i]
        m_ref[g, 0:1, h:h + 1] = m_new[i]
        cnew_ref[g, h] = c_new[i][0:DK_B, :]
        nnew_ref[g, h:h + 1, :] = n_new[i][:, 0:DK_B]
        mnew_ref[g, 0:1, h:h + 1] = m_new[i]


def _mlstm(ml_in, gates, gates_t, ng, *, L, G, state=None):
    n_seq, T, _ = ml_in.shape
    n_c = T // L
    has_state = state is not None
    in_specs = [
        pl.BlockSpec((G, L, MLP_W), lambda b, c: (b, c, 0)),
        pl.BlockSpec((G, L, LANE), lambda b, c: (b, c, 0)),
        pl.BlockSpec((G, 1, N_GATE, L), lambda b, c: (b, c, 0, 0)),
        pl.BlockSpec((H_B, DV_B), lambda b, c: (0, 0)),
    ]
    args = [ml_in, gates, gates_t, ng]
    if has_state:
        c0, n0, m0 = state
        in_specs += [
            pl.BlockSpec((G, H_B, DK_B, DV_B), lambda b, c: (b, 0, 0, 0)),
            pl.BlockSpec((G, H_B, DK_B), lambda b, c: (b, 0, 0)),
            pl.BlockSpec((G, 1, H_B), lambda b, c: (b, 0, 0)),
        ]
        args += [c0, n0, m0]
    return pl.pallas_call(
        functools.partial(_mlstm_kernel, L=L, G=G, has_state=has_state),
        grid=(n_seq // G, n_c),
        in_specs=in_specs,
        out_specs=[
            pl.BlockSpec((G, L, H_B * DV_B), lambda b, c: (b, c, 0)),
            pl.BlockSpec((G, H_B, DK_B, DV_B), lambda b, c: (b, 0, 0, 0)),
            pl.BlockSpec((G, H_B, DK_B), lambda b, c: (b, 0, 0)),
            pl.BlockSpec((G, 1, H_B), lambda b, c: (b, 0, 0)),
        ],
        out_shape=[
            jax.ShapeDtypeStruct((n_seq, T, H_B * DV_B), F32),
            jax.ShapeDtypeStruct((n_seq, H_B, DK_B, DV_B), F32),
            jax.ShapeDtypeStruct((n_seq, H_B, DK_B), F32),
            jax.ShapeDtypeStruct((n_seq, 1, H_B), F32),
        ],
        scratch_shapes=[pltpu.VMEM((G, H_B, LANE, DV_B), F32), pltpu.VMEM((G, SUBLANE, LANE), F32),
                        pltpu.VMEM((G, SUBLANE, LANE), F32)],
        compiler_params=pltpu.CompilerParams(dimension_semantics=("parallel", "arbitrary"),
                                             vmem_limit_bytes=VMEM_LIMIT),
        name=f"mlstm_L{L}",
    )(*args)


def _outproj_kernel(xp_ref, xs_ref, map_ref, mas_ref, mbp_ref, mbs_ref, wo_ref, g_ref, rw_ref, rb_ref,
                    x1_ref, hn_ref, lg_ref, *, n_p_tiles):
    half = H_A * DV_A

    def body(seg):
        x_ref, ma_ref, mb_ref = (xp_ref, xs_ref)[seg], (map_ref, mas_ref)[seg], (mbp_ref, mbs_ref)[seg]
        x1 = (x_ref[...] + jnp.dot(ma_ref[...].astype(BF16), wo_ref[:half, :], preferred_element_type=F32)
              + jnp.dot(mb_ref[...].astype(BF16), wo_ref[half:, :], preferred_element_type=F32))
        x1_ref[...] = x1
        hn = _rms(x1, g_ref[...])
        hn_ref[...] = hn
        lg_ref[...] = _dot_hi(hn, rw_ref[...]) + rb_ref[...]

    _for_segment(n_p_tiles, body)


def _outproj(xp, xs, ma_p, ma_s, mb_p, mb_s, w_out, g, rw, rb, tm):
    n_p, n_s = xp.shape[0], xs.shape[0]
    n = n_p + n_s
    npt = n_p // tm
    half = H_A * DV_A
    return pl.pallas_call(
        functools.partial(_outproj_kernel, n_p_tiles=npt),
        grid=(n // tm,),
        in_specs=_two_segment_specs(tm, D_MODEL, npt) + _two_segment_specs(tm, half, npt)
        + _two_segment_specs(tm, half, npt) + [
            pl.BlockSpec((D_MODEL, D_MODEL), lambda i: (0, 0)),
            pl.BlockSpec((1, D_MODEL), lambda i: (0, 0)),
            pl.BlockSpec((D_MODEL, LANE), lambda i: (0, 0)),
            pl.BlockSpec((1, LANE), lambda i: (0, 0)),
        ],
        out_specs=[
            pl.BlockSpec((tm, D_MODEL), lambda i: (i, 0)),
            pl.BlockSpec((tm, D_MODEL), lambda i: (i, 0)),
            pl.BlockSpec((tm, LANE), lambda i: (i, 0)),
        ],
        out_shape=[
            jax.ShapeDtypeStruct((n, D_MODEL), F32),
            jax.ShapeDtypeStruct((n, D_MODEL), F32),
            jax.ShapeDtypeStruct((n, LANE), F32),
        ],
        compiler_params=pltpu.CompilerParams(dimension_semantics=("arbitrary",), vmem_limit_bytes=VMEM_LIMIT),
        name="outproj",
    )(xp, xs, ma_p, ma_s, mb_p, mb_s, w_out, g, rw, rb)


def _expert_kernel(be_ref, nu_ref, rt_ref, rtn_ref, x_hbm, wgu_ref, bgu_ref, wd_ref, bd_ref,
                   y_ref, xbuf, sem, wgu_bf, wd_bf, *, tb):
    b = pl.program_id(0)
    n_used = nu_ref[0]
    slot = b % 2

    def start_gather(idx_ref, s):
        def body(r, carry):
            tok = idx_ref[0, 0, r]
            pltpu.make_async_copy(x_hbm.at[pl.ds(tok, 1)], xbuf.at[s, pl.ds(r, 1)], sem.at[s]).start()
            return carry
        lax.fori_loop(0, tb, body, 0)

    @pl.when(b == 0)
    def _():
        start_gather(rt_ref, 0)

    @pl.when(b + 1 < n_used)
    def _():
        start_gather(rtn_ref, 1 - slot)

    @pl.when(b >= n_used)
    def _():
        y_ref[...] = jnp.zeros_like(y_ref)

    @pl.when(b < n_used)
    def _():
        pltpu.make_async_copy(x_hbm.at[pl.ds(0, tb)], xbuf.at[slot], sem.at[slot]).wait()
        e_prev = be_ref[jnp.maximum(b - 1, 0)]

        @pl.when(jnp.logical_or(b == 0, be_ref[b] != e_prev))
        def _():
            wgu_bf[...] = wgu_ref[0].astype(BF16)
            wd_bf[...] = wd_ref[0].astype(BF16)

        xb = xbuf[slot].astype(BF16)
        hgu = jnp.dot(xb, wgu_bf[...], preferred_element_type=F32) + bgu_ref[0]
        gate = jnp.minimum(hgu[:, :D_FF], SWIGLU_LIMIT)
        up = jnp.clip(hgu[:, D_FF:], -SWIGLU_LIMIT, SWIGLU_LIMIT)
        act = (up + 1.0) * gate * _sigmoid(SWIGLU_ALPHA * gate)
        y_ref[...] = jnp.dot(act.astype(BF16), wd_bf[...], preferred_element_type=F32) + bd_ref[0]


def _experts(block_e, n_used, row_tok3, x_src, w_gu, b_gu, w_down, b_down, tb):
    nb = row_tok3.shape[0]

    def cur(b, nu):
        return jnp.minimum(b, nu[0] - 1)

    grid_spec = pltpu.PrefetchScalarGridSpec(
        num_scalar_prefetch=2,
        grid=(nb,),
        in_specs=[
            pl.BlockSpec((1, 1, tb), lambda b, be, nu: (cur(b, nu), 0, 0), memory_space=pltpu.SMEM),
            pl.BlockSpec((1, 1, tb), lambda b, be, nu: (cur(b + 1, nu), 0, 0), memory_space=pltpu.SMEM),
            pl.BlockSpec(memory_space=pl.ANY),
            pl.BlockSpec((1, D_MODEL, 2 * D_FF), lambda b, be, nu: (be[cur(b, nu)], 0, 0)),
            pl.BlockSpec((1, 1, 2 * D_FF), lambda b, be, nu: (be[cur(b, nu)], 0, 0)),
            pl.BlockSpec((1, D_FF, D_MODEL), lambda b, be, nu: (be[cur(b, nu)], 0, 0)),
            pl.BlockSpec((1, 1, D_MODEL), lambda b, be, nu: (be[cur(b, nu)], 0, 0)),
        ],
        out_specs=pl.BlockSpec((tb, D_MODEL), lambda b, be, nu: (b, 0)),
        scratch_shapes=[
            pltpu.VMEM((2, tb, D_MODEL), F32),
            pltpu.SemaphoreType.DMA((2,)),
            pltpu.VMEM((D_MODEL, 2 * D_FF), BF16),
            pltpu.VMEM((D_FF, D_MODEL), BF16),
        ],
    )
    return pl.pallas_call(
        functools.partial(_expert_kernel, tb=tb),
        grid_spec=grid_spec,
        out_shape=jax.ShapeDtypeStruct((nb * tb, D_MODEL), F32),
        compiler_params=pltpu.CompilerParams(dimension_semantics=("arbitrary",), vmem_limit_bytes=VMEM_LIMIT),
        name="experts",
    )(block_e, n_used, row_tok3, row_tok3, x_src, w_gu, b_gu, w_down, b_down)


def _combine_kernel(pos_ref, posn_ref, y_hbm, x1_ref, gt_ref, pp_ref, ps_ref, gple_ref, wg_ref, wp_ref, gfin_ref,
                    outp_ref, outs_ref, ybuf, sem, *, tm, n_p_tiles):
    i = pl.program_id(0)
    n_i = pl.num_programs(0)
    slot = i % 2

    def start_gather(idx_ref, s):
        def body(r, carry):
            for kk in range(TOP_K):
                row = idx_ref[0, 0, kk * tm + r]
                pltpu.make_async_copy(y_hbm.at[pl.ds(row, 1)], ybuf.at[s, kk, pl.ds(r, 1)], sem.at[s]).start()
            return carry
        lax.fori_loop(0, tm, body, 0)

    @pl.when(i == 0)
    def _():
        start_gather(pos_ref, 0)

    @pl.when(i + 1 < n_i)
    def _():
        start_gather(posn_ref, 1 - slot)

    for kk in range(TOP_K):
        pltpu.make_async_copy(y_hbm.at[pl.ds(0, tm)], ybuf.at[slot, kk], sem.at[slot]).wait()

    def body(seg):
        p_ref, out_ref = (pp_ref, ps_ref)[seg], (outp_ref, outs_ref)[seg]
        gt = gt_ref[...]
        x2 = x1_ref[...]
        for kk in range(TOP_K):
            x2 = x2 + gt[:, kk:kk + 1] * ybuf[slot, kk]
        hn = _rms(x2, gple_ref[...]).astype(BF16)
        gate = _sigmoid(jnp.dot(hn, wg_ref[...], preferred_element_type=F32))
        pe = jnp.dot(p_ref[...].astype(BF16), wp_ref[...], preferred_element_type=F32)
        x3 = x2 + gate * pe
        out_ref[...] = _rms(x3, gfin_ref[...])

    _for_segment(n_p_tiles, body)


def _combine(pos3, y_rows, x1, gates_pad, pp, ps, g_ple, w_gate, w_p, g_fin, tm):
    n_p, n_s = pp.shape[0], ps.shape[0]
    n = n_p + n_s
    nt = n // tm
    npt = n_p // tm
    return pl.pallas_call(
        functools.partial(_combine_kernel, tm=tm, n_p_tiles=npt),
        grid=(nt,),
        in_specs=[
            pl.BlockSpec((1, 1, TOP_K * tm), lambda i: (i, 0, 0), memory_space=pltpu.SMEM),
            pl.BlockSpec((1, 1, TOP_K * tm), lambda i: (jnp.minimum(i + 1, nt - 1), 0, 0), memory_space=pltpu.SMEM),
            pl.BlockSpec(memory_space=pl.ANY),
            pl.BlockSpec((tm, D_MODEL), lambda i: (i, 0)),
            pl.BlockSpec((tm, LANE), lambda i: (i, 0)),
        ] + _two_segment_specs(tm, PLE_DIM, npt) + [
            pl.BlockSpec((1, D_MODEL), lambda i: (0, 0)),
            pl.BlockSpec((D_MODEL, D_MODEL), lambda i: (0, 0)),
            pl.BlockSpec((PLE_DIM, D_MODEL), lambda i: (0, 0)),
            pl.BlockSpec((1, D_MODEL), lambda i: (0, 0)),
        ],
        out_specs=_two_segment_specs(tm, D_MODEL, npt),
        out_shape=[jax.ShapeDtypeStruct((n_p, D_MODEL), F32), jax.ShapeDtypeStruct((n_s, D_MODEL), F32)],
        scratch_shapes=[pltpu.VMEM((2, TOP_K, tm, D_MODEL), F32), pltpu.SemaphoreType.DMA((2,))],
        compiler_params=pltpu.CompilerParams(dimension_semantics=("arbitrary",), vmem_limit_bytes=VMEM_LIMIT),
        name="combine",
    )(pos3, pos3, y_rows, x1, gates_pad, pp, ps, g_ple, w_gate, w_p, g_fin)


def _routing(logits, tb, nb):
    n = logits.shape[0]
    top_val, top_idx = lax.top_k(logits, TOP_K)
    gates = jax.nn.softmax(top_val, axis=-1)
    onehot = (top_idx[:, :, None] == jnp.arange(N_EXPERTS, dtype=jnp.int32)[None, None, :])
    mask = jnp.sum(onehot.astype(jnp.int32), axis=1)
    csum = jnp.cumsum(mask, axis=0)
    counts = csum[-1]
    rank = csum - mask
    padded = (counts + tb - 1) // tb * tb
    pad_end = jnp.cumsum(padded)
    pad_start = pad_end - padded
    pos = jnp.take_along_axis(pad_start[None, :] + rank, top_idx, axis=1).astype(jnp.int32)
    tok = jnp.broadcast_to(jnp.arange(n, dtype=jnp.int32)[:, None], (n, TOP_K))
    row_tok = jnp.zeros((nb * tb,), jnp.int32).at[pos.reshape(-1)].set(tok.reshape(-1))
    block_e = jnp.minimum(jnp.searchsorted(pad_end, jnp.arange(nb, dtype=pad_end.dtype) * tb, side='right'),
                          N_EXPERTS - 1).astype(jnp.int32)
    n_used = (pad_end[-1] // tb).astype(jnp.int32).reshape(1)
    return gates, pos, row_tok, block_e, n_used


def _rearranged_in_weights(w_in):
    o = np.cumsum([0, CONV_CH, H_A * DV_A, H_A, H_A, H_B * DK_B, H_B * DK_B, H_B * DV_B, H_B * DV_B, H_B, H_B])
    conv_in, z_a, a_a, b_a, q_b, k_b, v_b, o_b, i_b, f_b = (w_in[:, int(o[j]):int(o[j + 1])] for j in range(10))
    zpad = jnp.zeros((D_MODEL, LANE - DK_B), w_in.dtype)

    def pad_heads(w):
        return jnp.concatenate([jnp.concatenate([w[:, h * DK_B:(h + 1) * DK_B], zpad], axis=1) for h in range(H_B)],
                               axis=1)

    small = jnp.concatenate([a_a, b_a, i_b, f_b], axis=1)
    w_all = jnp.concatenate([conv_in, z_a, pad_heads(q_b), pad_heads(k_b), v_b, o_b,
                             small, jnp.zeros((D_MODEL, LANE - N_GATE), w_in.dtype)], axis=1)
    return w_all.astype(BF16), small.T.astype(BF16)


def _gate_params(a_log, dt_bias, i_bias, f_bias):
    z4 = jnp.zeros((4,), F32)
    alog = jnp.concatenate([a_log.astype(F32), z4, z4, z4])
    bias = jnp.concatenate([dt_bias.astype(F32), z4, i_bias.astype(F32), f_bias.astype(F32)])
    pad = jnp.zeros((LANE - N_GATE,), F32)
    pcol = jnp.zeros((SUBLANE, LANE), F32).at[0].set(jnp.concatenate([alog, pad])).at[1].set(
        jnp.concatenate([bias, pad]))
    prow = jnp.zeros((N_GATE, LANE), F32).at[:, 0].set(alog).at[:, 1].set(bias)
    return pcol, prow


def kernel(x_prompt, x_sample, p_prompt, p_sample, state_conv, state_gdn, state_mlstm_c, state_mlstm_n, state_mlstm_m, norm_attn_g, w_in, conv_w, gdn_a_log, gdn_dt_bias, gdn_norm_g, mlstm_i_bias, mlstm_f_bias, mlstm_norm_g, w_out, norm_moe_g, router_w, router_b, expert_w_gu, expert_b_gu, expert_w_down, expert_b_down, norm_ple_g, ple_gate_w, ple_w, final_norm_g):
    bp, tp, _ = x_prompt.shape
    bs, ts, _ = x_sample.shape
    n_p, n_s = bp * tp, bs * ts
    n = n_p + n_s
    lp, ls = min(tp, CHUNK), min(ts, CHUNK)
    tm = 512
    gp_a = 4 if bp % 4 == 0 else 1
    gp_b = 2 if bp % 2 == 0 else 1
    gs = 8 if bs % 8 == 0 else 1
    assert tp % lp == 0 and ts % ls == 0 and n_p % tm == 0 and n_s % tm == 0 and ls % SUBLANE == 0

    xp = x_prompt.reshape(n_p, D_MODEL)
    xs = x_sample.reshape(n_s, D_MODEL)

    w_all, ws_t = _rearranged_in_weights(w_in[0])
    pcol, prow = _gate_params(gdn_a_log[0], gdn_dt_bias[0], mlstm_i_bias[0], mlstm_f_bias[0])
    gdn_p, gdn_s, ml_p, ml_s, gate_p, gate_s, gatet_p, gatet_s = _inproj(
        xp, xs, norm_attn_g[0].reshape(1, D_MODEL), w_all, ws_t, pcol, prow, tm)
    gt_p = gatet_p.reshape(N_GATE, bp, tp // lp, lp).transpose(1, 2, 0, 3)
    gt_s = gatet_s.reshape(N_GATE, bs, ts // ls, ls).transpose(1, 2, 0, 3)

    cw = jnp.zeros((SUBLANE, CONV_CH), F32).at[:CONV_W].set(conv_w[0].astype(F32))
    ng_a = gdn_norm_g[0].reshape(1, DV_A).astype(F32)
    ng_b = mlstm_norm_g[0].reshape(H_B, DV_B).astype(F32)
    ma_p, conv_p, gdn_st_p = _gdn(gdn_p.reshape(bp, tp, GDN_W), gate_p.reshape(bp, tp, LANE), gt_p, cw, ng_a,
                                  L=lp, G=gp_a)
    ma_s, conv_s, gdn_st_s = _gdn(gdn_s.reshape(bs, ts, GDN_W), gate_s.reshape(bs, ts, LANE), gt_s, cw, ng_a,
                                  L=ls, G=gs, state=(state_conv[0], state_gdn[0]))
    mb_p, c_p, nn_p, m_p = _mlstm(ml_p.reshape(bp, tp, MLP_W), gate_p.reshape(bp, tp, LANE), gt_p, ng_b,
                                  L=lp, G=gp_b)
    mb_s, c_s, nn_s, m_s = _mlstm(ml_s.reshape(bs, ts, MLP_W), gate_s.reshape(bs, ts, LANE), gt_s, ng_b,
                                  L=ls, G=gs,
                                  state=(state_mlstm_c[0], state_mlstm_n[0], state_mlstm_m[0].reshape(bs, 1, H_B)))
    half = H_A * DV_A

    rw = jnp.zeros((D_MODEL, LANE), F32).at[:, :N_EXPERTS].set(router_w[0])
    rb = jnp.full((1, LANE), NEG, F32).at[0, :N_EXPERTS].set(router_b[0])
    x1, hn2, logits = _outproj(xp, xs, ma_p.reshape(n_p, half), ma_s.reshape(n_s, half),
                               mb_p.reshape(n_p, half), mb_s.reshape(n_s, half),
                               w_out[0].astype(BF16), norm_moe_g[0].reshape(1, D_MODEL), rw, rb, tm)

    tb = 256
    nb = n * TOP_K // tb + N_EXPERTS
    gates_k, pos, row_tok, block_e, n_used = _routing(logits[:, :N_EXPERTS], tb, nb)
    y_rows = _experts(block_e, n_used, row_tok.reshape(nb, 1, tb), hn2,
                      expert_w_gu[0], expert_b_gu[0].reshape(N_EXPERTS, 1, 2 * D_FF),
                      expert_w_down[0], expert_b_down[0].reshape(N_EXPERTS, 1, D_MODEL), tb)
    tm5 = 256
    pos3 = pos.reshape(n // tm5, tm5, TOP_K).transpose(0, 2, 1).reshape(n // tm5, 1, TOP_K * tm5)
    gates_pad = jnp.zeros((n, LANE), F32).at[:, :TOP_K].set(gates_k)
    y_p, y_s = _combine(pos3, y_rows, x1, gates_pad, p_prompt[0].reshape(n_p, PLE_DIM),
                        p_sample[0].reshape(n_s, PLE_DIM), norm_ple_g[0].reshape(1, D_MODEL),
                        ple_gate_w[0].astype(BF16), ple_w[0].astype(BF16), final_norm_g.reshape(1, D_MODEL), tm5)

    return (y_p.reshape(bp, tp, D_MODEL), y_s.reshape(bs, ts, D_MODEL),
            conv_p[None], gdn_st_p[None], c_p[None], nn_p[None], m_p.reshape(1, bp, H_B),
            conv_s[None], gdn_st_s[None], c_s[None], nn_s[None], m_s.reshape(1, bs, H_B))
```

```python
import functools

import numpy as np
import jax
import jax.numpy as jnp
from jax import lax
from jax.experimental import pallas as pl
from jax.experimental.pallas import tpu as pltpu

F32 = jnp.float32
BF16 = jnp.bfloat16

D_MODEL = 1024
H_A, DK_A, DV_A = 4, 128, 128
H_B, DK_B, DV_B = 4, 64, 128
CONV_W = 4
CONV_CH = H_A * (2 * DK_A + DV_A)
N_EXPERTS = 32
TOP_K = 4
D_FF = 1024
SWIGLU_LIMIT = 7.0
SWIGLU_ALPHA = 1.702
PLE_DIM = 256
EPS = 1e-6
NEG = -1e30
CHUNK = 64

LANE = 128
SUBLANE = 8
GDN_W = CONV_CH + H_A * DV_A
MLP_W = 2 * H_B * LANE + 2 * H_B * DV_B
N_GATE = 16
W_ALL = GDN_W + MLP_W + LANE

VMEM_LIMIT = 48 * 1024 * 1024

MOE_TM = 256
MOE_BLK = 256
SEG_ALIGN = 16
MOE_CAP = -(-(MOE_TM * TOP_K + N_EXPERTS * (SEG_ALIGN - 1)) // LANE) * LANE
SEG_SIZES = (256, 128, 64, 32, 16)

HI = lax.Precision.HIGHEST

_NN = (((1,), (0,)), ((), ()))
_NT = (((1,), (1,)), ((), ()))
_TN = (((0,), (0,)), ((), ()))


def _dot(a, b, dims=_NN):
    return lax.dot_general(a.astype(BF16), b.astype(BF16), dims, preferred_element_type=F32)


def _dot_hi(a, b, dims=_NN):
    return lax.dot_general(a, b, dims, precision=HI, preferred_element_type=F32)


def _rms(x, g):
    return x * lax.rsqrt(jnp.mean(x * x, axis=-1, keepdims=True) + EPS) * g


def _softplus(t):
    return jnp.maximum(t, 0.0) + jnp.log1p(jnp.exp(-jnp.abs(t)))


def _sigmoid(t):
    return 1.0 / (1.0 + jnp.exp(-t))


def _silu(t):
    return t * _sigmoid(t)


def _activate_gates(raw, idx, alog, bias):
    t = raw + bias
    g = -jnp.exp(alog) * _softplus(t)
    beta = _sigmoid(t)
    lf = -_softplus(-t)
    return jnp.where(idx < 4, g, jnp.where(idx < 8, beta, jnp.where(idx < 12, t, lf)))


def _two_segment_specs(tm, width, n_p_tiles):
    return [pl.BlockSpec((tm, width), lambda i: (jnp.minimum(i, n_p_tiles - 1), 0)),
            pl.BlockSpec((tm, width), lambda i: (jnp.maximum(i - n_p_tiles, 0), 0))]


def _for_segment(n_p_tiles, body):
    i = pl.program_id(0)

    @pl.when(i < n_p_tiles)
    def _():
        body(0)

    @pl.when(i >= n_p_tiles)
    def _():
        body(1)


def _inproj_kernel(xp_ref, xs_ref, g_ref, w_ref, wst_ref, pc_ref, pr_ref,
                   gdnp_ref, gdns_ref, mlp_ref, mls_ref, gatep_ref, gates_ref, gatetp_ref, gatets_ref,
                   *, n_p_tiles):
    tm = xp_ref.shape[0]

    def body(seg):
        x_ref = (xp_ref, xs_ref)[seg]
        gdn_ref, ml_ref = (gdnp_ref, gdns_ref)[seg], (mlp_ref, mls_ref)[seg]
        gate_ref, gatet_ref = (gatep_ref, gates_ref)[seg], (gatetp_ref, gatets_ref)[seg]
        hn = _rms(x_ref[...], g_ref[...]).astype(BF16)
        gdn_ref[...] = jnp.dot(hn, w_ref[:, :GDN_W], preferred_element_type=F32)
        ml_ref[...] = jnp.dot(hn, w_ref[:, GDN_W:GDN_W + MLP_W], preferred_element_type=F32)
        raw = jnp.dot(hn, w_ref[:, GDN_W + MLP_W:], preferred_element_type=F32)
        lane = lax.broadcasted_iota(jnp.int32, (tm, LANE), 1)
        gate_ref[...] = _activate_gates(raw, lane, pc_ref[0:1, :], pc_ref[1:2, :])
        raw_t = lax.dot_general(wst_ref[...], hn, _NT, preferred_element_type=F32)
        row = lax.broadcasted_iota(jnp.int32, (N_GATE, tm), 0)
        gatet_ref[...] = _activate_gates(raw_t, row, pr_ref[:, 0:1], pr_ref[:, 1:2])

    _for_segment(n_p_tiles, body)


def _inproj(xp, xs, g, w_all, ws_t, pcol, prow, tm):
    n_p, n_s = xp.shape[0], xs.shape[0]
    npt = n_p // tm

    def out2(width):
        return _two_segment_specs(tm, width, npt)

    def shp2(width):
        return [jax.ShapeDtypeStruct((n_p, width), F32), jax.ShapeDtypeStruct((n_s, width), F32)]

    return pl.pallas_call(
        functools.partial(_inproj_kernel, n_p_tiles=npt),
        grid=((n_p + n_s) // tm,),
        in_specs=_two_segment_specs(tm, D_MODEL, npt) + [
            pl.BlockSpec((1, D_MODEL), lambda i: (0, 0)),
            pl.BlockSpec((D_MODEL, W_ALL), lambda i: (0, 0)),
            pl.BlockSpec((N_GATE, D_MODEL), lambda i: (0, 0)),
            pl.BlockSpec((SUBLANE, LANE), lambda i: (0, 0)),
            pl.BlockSpec((N_GATE, LANE), lambda i: (0, 0)),
        ],
        out_specs=out2(GDN_W) + out2(MLP_W) + out2(LANE) + [
            pl.BlockSpec((N_GATE, tm), lambda i: (0, jnp.minimum(i, npt - 1))),
            pl.BlockSpec((N_GATE, tm), lambda i: (0, jnp.maximum(i - npt, 0))),
        ],
        out_shape=shp2(GDN_W) + shp2(MLP_W) + shp2(LANE) + [
            jax.ShapeDtypeStruct((N_GATE, n_p), F32), jax.ShapeDtypeStruct((N_GATE, n_s), F32)],
        compiler_params=pltpu.CompilerParams(dimension_semantics=("arbitrary",), vmem_limit_bytes=VMEM_LIMIT),
        name="inproj",
    )(xp, xs, g, w_all, ws_t, pcol, prow)


def _chunk_masks(L):
    ri = lax.broadcasted_iota(jnp.int32, (L, L), 0)
    ci = lax.broadcasted_iota(jnp.int32, (L, L), 1)
    return ri >= ci, ri > ci, ri <= ci


def _gdn_kernel(*refs, L, G, has_state):
    if has_state:
        (xin_ref, gate_ref, gatet_ref, cw_ref, ng_ref, cst_ref, s0_ref,
         mix_ref, cnew_ref, snew_ref, xc_ref, s_ref) = refs
    else:
        (xin_ref, gate_ref, gatet_ref, cw_ref, ng_ref,
         mix_ref, cnew_ref, snew_ref, xc_ref, s_ref) = refs
    c = pl.program_id(1)

    @pl.when(c == 0)
    def _():
        xc_ref[:, 0:SUBLANE, :] = jnp.zeros((G, SUBLANE, CONV_CH), F32)
        if has_state:
            xc_ref[:, SUBLANE - (CONV_W - 1):SUBLANE, :] = cst_ref[...]
            s_ref[...] = s0_ref[...]
        else:
            s_ref[...] = jnp.zeros_like(s_ref)

    @pl.when(c > 0)
    def _():
        xc_ref[:, 0:SUBLANE, :] = xc_ref[:, L:L + SUBLANE, :]

    tril, strict, triu = _chunk_masks(L)
    tril_f, triu_f = tril.astype(F32), triu.astype(F32)
    base = SUBLANE - (CONV_W - 1)

    chains = [(g, h) for g in range(G) for h in range(H_A)]
    s_old = [s_ref[g, h] for g, h in chains]
    for g in range(G):
        xc_ref[g, SUBLANE:SUBLANE + L, :] = xin_ref[g, :, :CONV_CH]

    q, k, v, beta, gc, gl, decay = [], [], [], [], [], [], []
    for g in range(G):
        conv = xc_ref[g, base:base + L, :] * cw_ref[0:1, :]
        for j in range(1, CONV_W):
            conv = conv + xc_ref[g, base + j:base + j + L, :] * cw_ref[j:j + 1, :]
        cnew_ref[g] = xc_ref[g, SUBLANE + L - (CONV_W - 1):SUBLANE + L, :]
        act = _silu(conv)
        gact = gate_ref[g]
        cum_c = _dot_hi(tril_f, gact)
        cum_r = _dot_hi(gatet_ref[g, 0], triu_f)
        for h in range(H_A):
            q.append(act[:, h * DK_A:(h + 1) * DK_A])
            k.append(act[:, H_A * DK_A + h * DK_A:H_A * DK_A + (h + 1) * DK_A])
            v.append(act[:, 2 * H_A * DK_A + h * DV_A:2 * H_A * DK_A + (h + 1) * DV_A])
            beta.append(gact[:, 4 + h:5 + h])
            gc.append(cum_c[:, h:h + 1])
            gl.append(cum_c[L - 1:L, h:h + 1])
            gr = cum_r[h:h + 1, :]
            decay.append(jnp.where(tril, jnp.exp(jnp.where(tril, cum_c[:, h:h + 1] - gr, 0.0)), 0.0))

    nc = range(len(chains))
    qss = [jnp.sum(q[i] * q[i], axis=-1, keepdims=True) for i in nc]
    kss = [jnp.sum(k[i] * k[i], axis=-1, keepdims=True) for i in nc]
    q = [q[i] * (lax.rsqrt(qss[i] + EPS) * (DK_A ** -0.5)) for i in nc]
    k = [k[i] * lax.rsqrt(kss[i] + EPS) for i in nc]
    kb = [k[i] * beta[i] for i in nc]
    egc = [jnp.exp(gc[i]) for i in nc]
    kk = [_dot(kb[i], k[i], _NT) for i in nc]
    qk = [_dot(q[i], k[i], _NT) for i in nc]
    eye = (lax.broadcasted_iota(jnp.int32, (L, L), 0) == lax.broadcasted_iota(jnp.int32, (L, L), 1)).astype(F32)
    pw = [-jnp.where(strict, kk[i] * decay[i], 0.0) for i in nc]
    t_inv = [eye + pw[i] for i in nc]
    span = 2
    while span < L:
        pw = [_dot(pw[i], pw[i]) for i in nc]
        t_inv = [t_inv[i] + _dot(t_inv[i], pw[i]) for i in nc]
        span *= 2
    sol = [_dot(t_inv[i], jnp.concatenate([v[i] * beta[i], kb[i] * egc[i]], axis=-1)) for i in nc]
    qs = [_dot(q[i] * egc[i], s_old[i]) for i in nc]
    ws = [_dot(sol[i][:, DV_A:], s_old[i]) for i in nc]
    v_new = [sol[i][:, :DV_A] - ws[i] for i in nc]
    o = [qs[i] + _dot(jnp.where(tril, qk[i] * decay[i], 0.0), v_new[i]) for i in nc]
    s_new = [s_old[i] * jnp.exp(gl[i]) + _dot(k[i] * jnp.exp(gl[i] - gc[i]), v_new[i], _TN) for i in nc]
    ms = [jnp.mean(o[i] * o[i], axis=-1, keepdims=True) for i in nc]
    on = [o[i] * lax.rsqrt(ms[i] + EPS) for i in nc]
    for i, (g, h) in enumerate(chains):
        z = xin_ref[g, :, CONV_CH + h * DV_A:CONV_CH + (h + 1) * DV_A]
        mix_ref[g, :, h * DV_A:(h + 1) * DV_A] = on[i] * ng_ref[...] * _silu(z)
    for i, (g, h) in enumerate(chains):
        s_ref[g, h] = s_new[i]
        snew_ref[g, h] = s_new[i]


def _gdn(gdn_in, gates, gates_t, cw, ng, *, L, G, state=None):
    n_seq, T, _ = gdn_in.shape
    n_c = T // L
    has_state = state is not None
    in_specs = [
        pl.BlockSpec((G, L, GDN_W), lambda b, c: (b, c, 0)),
        pl.BlockSpec((G, L, LANE), lambda b, c: (b, c, 0)),
        pl.BlockSpec((G, 1, N_GATE, L), lambda b, c: (b, c, 0, 0)),
        pl.BlockSpec((SUBLANE, CONV_CH), lambda b, c: (0, 0)),
        pl.BlockSpec((1, DV_A), lambda b, c: (0, 0)),
    ]
    args = [gdn_in, gates, gates_t, cw, ng]
    if has_state:
        conv_st, s0 = state
        in_specs += [
            pl.BlockSpec((G, CONV_W - 1, CONV_CH), lambda b, c: (b, 0, 0)),
            pl.BlockSpec((G, H_A, DK_A, DV_A), lambda b, c: (b, 0, 0, 0)),
        ]
        args += [conv_st, s0]
    return pl.pallas_call(
        functools.partial(_gdn_kernel, L=L, G=G, has_state=has_state),
        grid=(n_seq // G, n_c),
        in_specs=in_specs,
        out_specs=[
            pl.BlockSpec((G, L, H_A * DV_A), lambda b, c: (b, c, 0)),
            pl.BlockSpec((G, CONV_W - 1, CONV_CH), lambda b, c: (b, 0, 0)),
            pl.BlockSpec((G, H_A, DK_A, DV_A), lambda b, c: (b, 0, 0, 0)),
        ],
        out_shape=[
            jax.ShapeDtypeStruct((n_seq, T, H_A * DV_A), F32),
            jax.ShapeDtypeStruct((n_seq, CONV_W - 1, CONV_CH), F32),
            jax.ShapeDtypeStruct((n_seq, H_A, DK_A, DV_A), F32),
        ],
        scratch_shapes=[pltpu.VMEM((G, L + SUBLANE, CONV_CH), F32), pltpu.VMEM((G, H_A, DK_A, DV_A), F32)],
        compiler_params=pltpu.CompilerParams(dimension_semantics=("parallel", "arbitrary"),
                                             vmem_limit_bytes=VMEM_LIMIT),
        name=f"gdn_L{L}",
    )(*args)


def _mlstm_kernel(*refs, L, G, has_state):
    if has_state:
        (xin_ref, gate_ref, gatet_ref, ng_ref, c0_ref, n0_ref, m0_ref,
         mix_ref, cnew_ref, nnew_ref, mnew_ref, c_ref, n_ref, m_ref) = refs
    else:
        (xin_ref, gate_ref, gatet_ref, ng_ref,
         mix_ref, cnew_ref, nnew_ref, mnew_ref, c_ref, n_ref, m_ref) = refs
    c = pl.program_id(1)

    @pl.when(c == 0)
    def _():
        c_ref[...] = jnp.zeros_like(c_ref)
        n_ref[...] = jnp.zeros_like(n_ref)
        m_ref[...] = jnp.zeros_like(m_ref)
        if has_state:
            c_ref[:, :, 0:DK_B, :] = c0_ref[...]
            n_ref[:, 0:H_B, 0:DK_B] = n0_ref[...]
            m_ref[:, 0:1, 0:H_B] = m0_ref[...]

    tril, _, triu = _chunk_masks(L)
    tril_f, triu_f = tril.astype(F32), triu.astype(F32)

    chains = [(g, h) for g in range(G) for h in range(H_B)]
    nc = range(len(chains))
    c_old = [c_ref[g, h] for g, h in chains]
    n_old = [n_ref[g, h:h + 1, :] for g, h in chains]
    m_old = [m_ref[g, 0:1, h:h + 1] for g, h in chains]

    v0 = 2 * H_B * LANE
    q = [xin_ref[g, :, h * LANE:(h + 1) * LANE] * (DK_B ** -0.5) for g, h in chains]
    k = [xin_ref[g, :, (H_B + h) * LANE:(H_B + h + 1) * LANE] for g, h in chains]
    v = [xin_ref[g, :, v0 + h * DV_B:v0 + (h + 1) * DV_B] for g, h in chains]
    ig_c, b_c, b_last, d_log = [], [], [], []
    for g in range(G):
        gact = gate_ref[g]
        gact_t = gatet_ref[g, 0]
        cum_c = _dot_hi(tril_f, gact)
        cum_r = _dot_hi(gact_t, triu_f)
        for h in range(H_B):
            ig_c.append(gact[:, 8 + h:9 + h])
            b_c.append(cum_c[:, 12 + h:13 + h])
            b_last.append(cum_c[L - 1:L, 12 + h:13 + h])
            d_log.append(jnp.where(tril, cum_c[:, 12 + h:13 + h] - cum_r[12 + h:13 + h, :]
                                   + gact_t[8 + h:9 + h, :], NEG))
    qk = [_dot(q[i], k[i], _NT) for i in nc]
    qc = [_dot(q[i], c_old[i]) for i in nc]
    inter = [b_c[i] + m_old[i] for i in nc]
    m_t = [jnp.maximum(inter[i], jnp.max(d_log[i], axis=-1, keepdims=True)) for i in nc]
    s = [qk[i] * jnp.exp(d_log[i] - m_t[i]) for i in nc]
    e_inter = [jnp.exp(inter[i] - m_t[i]) for i in nc]
    sv = [_dot(s[i], v[i]) for i in nc]
    m_new = [m_t[i][L - 1:L, :] for i in nc]
    kw = [k[i] * jnp.exp(b_last[i] - b_c[i] + ig_c[i] - m_new[i]) for i in nc]
    f_tot = [jnp.exp(b_last[i] + m_old[i] - m_new[i]) for i in nc]
    c_new = [f_tot[i] * c_old[i] + _dot(kw[i], v[i], _TN) for i in nc]
    n_new = [f_tot[i] * n_old[i] + jnp.sum(kw[i], axis=0, keepdims=True) for i in nc]
    qn = [jnp.sum(q[i] * n_old[i], axis=-1, keepdims=True) for i in nc]
    ssum = [jnp.sum(s[i], axis=-1, keepdims=True) for i in nc]
    den = [jnp.maximum(jnp.abs(e_inter[i] * qn[i] + ssum[i]), jnp.exp(-m_t[i])) for i in nc]
    hh = [(e_inter[i] * qc[i] + sv[i]) / den[i] for i in nc]
    ms = [jnp.mean(hh[i] * hh[i], axis=-1, keepdims=True) for i in nc]
    hn = [hh[i] * lax.rsqrt(ms[i] + EPS) for i in nc]
    for i, (g, h) in enumerate(chains):
        og = xin_ref[g, :, v0 + H_B * DV_B + h * DV_B:v0 + H_B * DV_B + (h + 1) * DV_B]
        mix_ref[g, :, h * DV_B:(h + 1) * DV_B] = hn[i] * ng_ref[h:h + 1, :] * _sigmoid(og)
    for i, (g, h) in enumerate(chains):
        c_ref[g, h] = c_new[i]
        n_ref[g, h:h + 1, :] = n_new[i]
        m_ref[g, 0:1, h:h + 1] = m_new[i]
        cnew_ref[g, h] = c_new[i][0:DK_B, :]
        nnew_ref[g, h:h + 1, :] = n_new[i][:, 0:DK_B]
        mnew_ref[g, 0:1, h:h + 1] = m_new[i]


def _mlstm(ml_in, gates, gates_t, ng, *, L, G, state=None):
    n_seq, T, _ = ml_in.shape
    n_c = T // L
    has_state = state is not None
    in_specs = [
        pl.BlockSpec((G, L, MLP_W), lambda b, c: (b, c, 0)),
        pl.BlockSpec((G, L, LANE), lambda b, c: (b, c, 0)),
        pl.BlockSpec((G, 1, N_GATE, L), lambda b, c: (b, c, 0, 0)),
        pl.BlockSpec((H_B, DV_B), lambda b, c: (0, 0)),
    ]
    args = [ml_in, gates, gates_t, ng]
    if has_state:
        c0, n0, m0 = state
        in_specs += [
            pl.BlockSpec((G, H_B, DK_B, DV_B), lambda b, c: (b, 0, 0, 0)),
            pl.BlockSpec((G, H_B, DK_B), lambda b, c: (b, 0, 0)),
            pl.BlockSpec((G, 1, H_B), lambda b, c: (b, 0, 0)),
        ]
        args += [c0, n0, m0]
    return pl.pallas_call(
        functools.partial(_mlstm_kernel, L=L, G=G, has_state=has_state),
        grid=(n_seq // G, n_c),
        in_specs=in_specs,
        out_specs=[
            pl.BlockSpec((G, L, H_B * DV_B), lambda b, c: (b, c, 0)),
            pl.BlockSpec((G, H_B, DK_B, DV_B), lambda b, c: (b, 0, 0, 0)),
            pl.BlockSpec((G, H_B, DK_B), lambda b, c: (b, 0, 0)),
            pl.BlockSpec((G, 1, H_B), lambda b, c: (b, 0, 0)),
        ],
        out_shape=[
            jax.ShapeDtypeStruct((n_seq, T, H_B * DV_B), F32),
            jax.ShapeDtypeStruct((n_seq, H_B, DK_B, DV_B), F32),
            jax.ShapeDtypeStruct((n_seq, H_B, DK_B), F32),
            jax.ShapeDtypeStruct((n_seq, 1, H_B), F32),
        ],
        scratch_shapes=[pltpu.VMEM((G, H_B, LANE, DV_B), F32), pltpu.VMEM((G, SUBLANE, LANE), F32),
                        pltpu.VMEM((G, SUBLANE, LANE), F32)],
        compiler_params=pltpu.CompilerParams(dimension_semantics=("parallel", "arbitrary"),
                                             vmem_limit_bytes=VMEM_LIMIT),
        name=f"mlstm_L{L}",
    )(*args)


def _outproj_kernel(xp_ref, xs_ref, map_ref, mas_ref, mbp_ref, mbs_ref, wo_ref, g_ref, rw_ref, rb_ref,
                    x1_ref, xsort_ref, info_ref, cpad_ref, *, n_p_tiles):
    half = H_A * DV_A
    tm = xp_ref.shape[0]

    def body(seg):
        x_ref, ma_ref, mb_ref = (xp_ref, xs_ref)[seg], (map_ref, mas_ref)[seg], (mbp_ref, mbs_ref)[seg]
        x1 = (x_ref[...] + jnp.dot(ma_ref[...].astype(BF16), wo_ref[:half, :], preferred_element_type=F32)
              + jnp.dot(mb_ref[...].astype(BF16), wo_ref[half:, :], preferred_element_type=F32))
        x1_ref[...] = x1
        hn = _rms(x1, g_ref[...])
        logits = _dot_hi(hn, rw_ref[...]) + rb_ref[...]

        vals = logits.T[:N_EXPERTS, :]
        e_iota = lax.broadcasted_iota(jnp.int32, (N_EXPERTS, tm), 0)
        sels, tops = [], []
        for _ in range(TOP_K):
            m = jnp.max(vals, axis=0, keepdims=True)
            first = jnp.min(jnp.where(vals == m, e_iota, N_EXPERTS), axis=0, keepdims=True)
            sel = e_iota == first
            vals = jnp.where(sel, -jnp.inf, vals)
            sels.append(sel)
            tops.append(m)
        ex = [jnp.exp(t - tops[0]) for t in tops]
        den = ex[0] + ex[1] + ex[2] + ex[3]
        gates = [e / den for e in ex]
        mask = sels[0].astype(F32) + sels[1].astype(F32) + sels[2].astype(F32) + sels[3].astype(F32)
        ri = lax.broadcasted_iota(jnp.int32, (tm, tm), 0)
        ci = lax.broadcasted_iota(jnp.int32, (tm, tm), 1)
        rank = _dot(mask, (ri < ci).astype(F32))
        cnt = jnp.sum(mask, axis=1, keepdims=True)
        cpad = jnp.ceil(cnt * (1.0 / SEG_ALIGN)) * SEG_ALIGN
        cpad_b = jnp.broadcast_to(cpad, (N_EXPERTS, tm))
        er = lax.broadcasted_iota(jnp.int32, (N_EXPERTS, N_EXPERTS), 0)
        ec = lax.broadcasted_iota(jnp.int32, (N_EXPERTS, N_EXPERTS), 1)
        seg_off = _dot((er > ec).astype(F32), cpad_b)
        pos = seg_off + rank
        q = [jnp.sum(jnp.where(s, pos, 0.0), axis=0, keepdims=True) for s in sels]

        j_iota = lax.broadcasted_iota(jnp.int32, (MOE_CAP, tm), 0).astype(F32)
        perm = (j_iota == q[0]) | (j_iota == q[1]) | (j_iota == q[2]) | (j_iota == q[3])
        xsorted = _dot(jnp.where(perm, 1.0, 0.0), hn)
        xsort_ref[...] = xsorted.astype(BF16)

        r_iota = lax.broadcasted_iota(jnp.int32, (LANE, tm), 0)
        info = jnp.zeros((LANE, tm), F32)
        for kk in range(TOP_K):
            info = jnp.where(r_iota == kk, q[kk], info)
            info = jnp.where(r_iota == TOP_K + kk, gates[kk], info)
        info_ref[...] = info.T
        cpad_ref[0] = cpad_b[:, :LANE]

    _for_segment(n_p_tiles, body)


def _outproj(xp, xs, ma_p, ma_s, mb_p, mb_s, w_out, g, rw, rb, tm):
    n_p, n_s = xp.shape[0], xs.shape[0]
    n = n_p + n_s
    nt = n // tm
    npt = n_p // tm
    half = H_A * DV_A
    return pl.pallas_call(
        functools.partial(_outproj_kernel, n_p_tiles=npt),
        grid=(nt,),
        in_specs=_two_segment_specs(tm, D_MODEL, npt) + _two_segment_specs(tm, half, npt)
        + _two_segment_specs(tm, half, npt) + [
            pl.BlockSpec((D_MODEL, D_MODEL), lambda i: (0, 0)),
            pl.BlockSpec((1, D_MODEL), lambda i: (0, 0)),
            pl.BlockSpec((D_MODEL, LANE), lambda i: (0, 0)),
            pl.BlockSpec((1, LANE), lambda i: (0, 0)),
        ],
        out_specs=[
            pl.BlockSpec((tm, D_MODEL), lambda i: (i, 0)),
            pl.BlockSpec((MOE_CAP, D_MODEL), lambda i: (i, 0)),
            pl.BlockSpec((tm, LANE), lambda i: (i, 0)),
            pl.BlockSpec((1, N_EXPERTS, LANE), lambda i: (i, 0, 0)),
        ],
        out_shape=[
            jax.ShapeDtypeStruct((n, D_MODEL), F32),
            jax.ShapeDtypeStruct((nt * MOE_CAP, D_MODEL), BF16),
            jax.ShapeDtypeStruct((n, LANE), F32),
            jax.ShapeDtypeStruct((nt, N_EXPERTS, LANE), F32),
        ],
        compiler_params=pltpu.CompilerParams(dimension_semantics=("arbitrary",), vmem_limit_bytes=VMEM_LIMIT),
        name="outproj",
    )(xp, xs, ma_p, ma_s, mb_p, mb_s, w_out, g, rw, rb)


def _expert_kernel(be_ref, bj_ref, tf_ref, tl_ref, cov_ref, nu_ref, vt_ref, ct_ref, lt_ref,
                   xs_hbm, wgu_ref, bgu_ref, wd_ref, bd_ref, ys_hbm,
                   xbuf, ybuf, gsem, ssem, wgu_bf, wd_bf, *, nt):
    b = pl.program_id(0)
    n_used = nu_ref[0]
    slot = b % 2

    def for_segments(bb, fn):
        e = be_ref[bb]
        base = bj_ref[bb] * MOE_BLK

        def body(t, carry):
            v0 = vt_ref[e * nt + t]
            lo = jnp.maximum(v0, base)
            hi = jnp.minimum(v0 + ct_ref[t * N_EXPERTS + e], base + MOE_BLK)
            ln = jnp.maximum(hi - lo, 0)
            src = t * MOE_CAP + lt_ref[t * N_EXPERTS + e] + (lo - v0)
            dst = lo - base
            for size in SEG_SIZES:
                off = (ln // (2 * size)) * (2 * size)

                @pl.when((ln & size) != 0)
                def _():
                    fn(pl.multiple_of(src + off, SEG_ALIGN), pl.multiple_of(dst + off, SEG_ALIGN), size)
            return carry

        lax.fori_loop(tf_ref[bb], tl_ref[bb] + 1, body, 0)

    def gather_copy(s, src, dst, size):
        return pltpu.make_async_copy(xs_hbm.at[pl.ds(src, size)], xbuf.at[s, pl.ds(dst, size)], gsem.at[s])

    def scatter_copy(s, src, dst, size):
        return pltpu.make_async_copy(ybuf.at[s, pl.ds(dst, size)], ys_hbm.at[pl.ds(src, size)], ssem.at[s])

    def gather(bb, s):
        for_segments(bb, lambda src, dst, size: gather_copy(s, src, dst, size).start())

    def scatter(bb, s):
        for_segments(bb, lambda src, dst, size: scatter_copy(s, src, dst, size).start())

    def wait_rows(count, copy, s):
        for size in SEG_SIZES:
            @pl.when((count & size) != 0)
            def _():
                copy(s, 0, 0, size).wait()

    @pl.when(b == 0)
    def _():
        xbuf[...] = jnp.zeros_like(xbuf)
        gather(0, 0)

    @pl.when(b + 1 < n_used)
    def _():
        gather(b + 1, 1 - slot)

    @pl.when(b < n_used)
    def _():
        wait_rows(cov_ref[b], gather_copy, slot)
        e_prev = be_ref[jnp.maximum(b - 1, 0)]

        @pl.when(jnp.logical_or(b == 0, be_ref[b] != e_prev))
        def _():
            wgu_bf[...] = wgu_ref[0].astype(BF16)
            wd_bf[...] = wd_ref[0].astype(BF16)

        hgu = jnp.dot(xbuf[slot], wgu_bf[...], preferred_element_type=F32) + bgu_ref[0]
        gate = jnp.minimum(hgu[:, :D_FF], SWIGLU_LIMIT)
        up = jnp.clip(hgu[:, D_FF:], -SWIGLU_LIMIT, SWIGLU_LIMIT)
        act = (up + 1.0) * gate * _sigmoid(SWIGLU_ALPHA * gate)
        y = jnp.dot(act.astype(BF16), wd_bf[...], preferred_element_type=F32) + bd_ref[0]

        @pl.when(b >= 2)
        def _():
            wait_rows(cov_ref[jnp.maximum(b - 2, 0)], scatter_copy, slot)

        ybuf[slot] = y.astype(BF16)
        scatter(b, slot)

        @pl.when(b == n_used - 1)
        def _():
            wait_rows(cov_ref[b], scatter_copy, slot)

            @pl.when(b >= 1)
            def _():
                wait_rows(cov_ref[jnp.maximum(b - 1, 0)], scatter_copy, 1 - slot)


def _experts(tables, xs, w_gu, b_gu, w_down, b_down, nt):
    nb = tables[0].shape[0]

    def wblk(b, be, bj, tf, tl, cov, nu, vt, ct, lt):
        return (be[jnp.minimum(b, nu[0] - 1)], 0, 0)

    grid_spec = pltpu.PrefetchScalarGridSpec(
        num_scalar_prefetch=len(tables),
        grid=(nb,),
        in_specs=[
            pl.BlockSpec(memory_space=pl.ANY),
            pl.BlockSpec((1, D_MODEL, 2 * D_FF), wblk),
            pl.BlockSpec((1, 1, 2 * D_FF), wblk),
            pl.BlockSpec((1, D_FF, D_MODEL), wblk),
            pl.BlockSpec((1, 1, D_MODEL), wblk),
        ],
        out_specs=pl.BlockSpec(memory_space=pl.ANY),
        scratch_shapes=[
            pltpu.VMEM((2, MOE_BLK, D_MODEL), BF16),
            pltpu.VMEM((2, MOE_BLK, D_MODEL), BF16),
            pltpu.SemaphoreType.DMA((2,)),
            pltpu.SemaphoreType.DMA((2,)),
            pltpu.VMEM((D_MODEL, 2 * D_FF), BF16),
            pltpu.VMEM((D_FF, D_MODEL), BF16),
        ],
    )
    return pl.pallas_call(
        functools.partial(_expert_kernel, nt=nt),
        grid_spec=grid_spec,
        out_shape=jax.ShapeDtypeStruct(xs.shape, xs.dtype),
        input_output_aliases={len(tables): 0},
        compiler_params=pltpu.CompilerParams(dimension_semantics=("arbitrary",), vmem_limit_bytes=VMEM_LIMIT),
        name="experts",
    )(*tables, xs, w_gu, b_gu, w_down, b_down)


def _combine_kernel(ys_ref, info_ref, x1_ref, pp_ref, ps_ref, gple_ref, wg_ref, wp_ref, gfin_ref,
                    outp_ref, outs_ref, *, n_p_tiles):
    tm = x1_ref.shape[0]

    def body(seg):
        p_ref, out_ref = (pp_ref, ps_ref)[seg], (outp_ref, outs_ref)[seg]
        info = info_ref[...]
        j_iota = lax.broadcasted_iota(jnp.int32, (tm, MOE_CAP), 1).astype(F32)
        gmat = jnp.where(j_iota == info[:, 0:1], info[:, TOP_K:TOP_K + 1], 0.0)
        for kk in range(1, TOP_K):
            gmat = gmat + jnp.where(j_iota == info[:, kk:kk + 1], info[:, TOP_K + kk:TOP_K + kk + 1], 0.0)
        x2 = x1_ref[...] + jnp.dot(gmat.astype(BF16), ys_ref[...], preferred_element_type=F32)
        hn = _rms(x2, gple_ref[...]).astype(BF16)
        gate = _sigmoid(jnp.dot(hn, wg_ref[...], preferred_element_type=F32))
        pe = jnp.dot(p_ref[...].astype(BF16), wp_ref[...], preferred_element_type=F32)
        x3 = x2 + gate * pe
        out_ref[...] = _rms(x3, gfin_ref[...])

    _for_segment(n_p_tiles, body)


def _combine(ys, info, x1, pp, ps, g_ple, w_gate, w_p, g_fin, tm):
    n_p, n_s = pp.shape[0], ps.shape[0]
    n = n_p + n_s
    nt = n // tm
    npt = n_p // tm
    return pl.pallas_call(
        functools.partial(_combine_kernel, n_p_tiles=npt),
        grid=(nt,),
        in_specs=[
            pl.BlockSpec((MOE_CAP, D_MODEL), lambda i: (i, 0)),
            pl.BlockSpec((tm, LANE), lambda i: (i, 0)),
            pl.BlockSpec((tm, D_MODEL), lambda i: (i, 0)),
        ] + _two_segment_specs(tm, PLE_DIM, npt) + [
            pl.BlockSpec((1, D_MODEL), lambda i: (0, 0)),
            pl.BlockSpec((D_MODEL, D_MODEL), lambda i: (0, 0)),
            pl.BlockSpec((PLE_DIM, D_MODEL), lambda i: (0, 0)),
            pl.BlockSpec((1, D_MODEL), lambda i: (0, 0)),
        ],
        out_specs=_two_segment_specs(tm, D_MODEL, npt),
        out_shape=[jax.ShapeDtypeStruct((n_p, D_MODEL), F32), jax.ShapeDtypeStruct((n_s, D_MODEL), F32)],
        compiler_params=pltpu.CompilerParams(dimension_semantics=("arbitrary",), vmem_limit_bytes=VMEM_LIMIT),
        name="combine",
    )(ys, info, x1, pp, ps, g_ple, w_gate, w_p, g_fin)


def _block_tables(seg_len, nb):
    seg_off = jnp.cumsum(seg_len, axis=1) - seg_len
    seg_end = jnp.cumsum(seg_len, axis=0).T
    seg_start = seg_end - seg_len.T
    n_rows = seg_end[:, -1]
    n_blk = (n_rows + MOE_BLK - 1) // MOE_BLK
    blk_end = jnp.cumsum(n_blk)
    b = jnp.arange(nb, dtype=jnp.int32)
    block_e = jnp.minimum(jnp.sum((blk_end[None, :] <= b[:, None]).astype(jnp.int32), axis=1), N_EXPERTS - 1)
    block_j = b - (blk_end - n_blk)[block_e]
    base = block_j * MOE_BLK
    t_first = jnp.sum((seg_end[block_e] <= base[:, None]).astype(jnp.int32), axis=1)
    t_last = jnp.sum((seg_start[block_e] < (base + MOE_BLK)[:, None]).astype(jnp.int32), axis=1) - 1
    cover = jnp.clip(n_rows[block_e] - base, 0, MOE_BLK)
    tables = (block_e, block_j, t_first, t_last, cover, blk_end[-1:], seg_start.reshape(-1),
              seg_len.reshape(-1), seg_off.reshape(-1))
    return tuple(t.astype(jnp.int32) for t in tables)


def _rearranged_in_weights(w_in):
    o = np.cumsum([0, CONV_CH, H_A * DV_A, H_A, H_A, H_B * DK_B, H_B * DK_B, H_B * DV_B, H_B * DV_B, H_B, H_B])
    conv_in, z_a, a_a, b_a, q_b, k_b, v_b, o_b, i_b, f_b = (w_in[:, int(o[j]):int(o[j + 1])] for j in range(10))
    zpad = jnp.zeros((D_MODEL, LANE - DK_B), w_in.dtype)

    def pad_heads(w):
        return jnp.concatenate([jnp.concatenate([w[:, h * DK_B:(h + 1) * DK_B], zpad], axis=1) for h in range(H_B)],
                               axis=1)

    small = jnp.concatenate([a_a, b_a, i_b, f_b], axis=1)
    w_all = jnp.concatenate([conv_in, z_a, pad_heads(q_b), pad_heads(k_b), v_b, o_b,
                             small, jnp.zeros((D_MODEL, LANE - N_GATE), w_in.dtype)], axis=1)
    return w_all.astype(BF16), small.T.astype(BF16)


def _gate_params(a_log, dt_bias, i_bias, f_bias):
    z4 = jnp.zeros((4,), F32)
    alog = jnp.concatenate([a_log.astype(F32), z4, z4, z4])
    bias = jnp.concatenate([dt_bias.astype(F32), z4, i_bias.astype(F32), f_bias.astype(F32)])
    pad = jnp.zeros((LANE - N_GATE,), F32)
    pcol = jnp.zeros((SUBLANE, LANE), F32).at[0].set(jnp.concatenate([alog, pad])).at[1].set(
        jnp.concatenate([bias, pad]))
    prow = jnp.zeros((N_GATE, LANE), F32).at[:, 0].set(alog).at[:, 1].set(bias)
    return pcol, prow


def kernel(x_prompt, x_sample, p_prompt, p_sample, state_conv, state_gdn, state_mlstm_c, state_mlstm_n, state_mlstm_m, norm_attn_g, w_in, conv_w, gdn_a_log, gdn_dt_bias, gdn_norm_g, mlstm_i_bias, mlstm_f_bias, mlstm_norm_g, w_out, norm_moe_g, router_w, router_b, expert_w_gu, expert_b_gu, expert_w_down, expert_b_down, norm_ple_g, ple_gate_w, ple_w, final_norm_g):
    bp, tp, _ = x_prompt.shape
    bs, ts, _ = x_sample.shape
    n_p, n_s = bp * tp, bs * ts
    n = n_p + n_s
    lp, ls = min(tp, CHUNK), min(ts, CHUNK)
    tm = 512
    gp_a = 4 if bp % 4 == 0 else 1
    gp_b = 2 if bp % 2 == 0 else 1
    gs = 8 if bs % 8 == 0 else 1
    assert tp % lp == 0 and ts % ls == 0 and n_p % tm == 0 and n_s % tm == 0 and ls % SUBLANE == 0

    xp = x_prompt.reshape(n_p, D_MODEL)
    xs = x_sample.reshape(n_s, D_MODEL)

    w_all, ws_t = _rearranged_in_weights(w_in[0])
    pcol, prow = _gate_params(gdn_a_log[0], gdn_dt_bias[0], mlstm_i_bias[0], mlstm_f_bias[0])
    gdn_p, gdn_s, ml_p, ml_s, gate_p, gate_s, gatet_p, gatet_s = _inproj(
        xp, xs, norm_attn_g[0].reshape(1, D_MODEL), w_all, ws_t, pcol, prow, tm)
    gt_p = gatet_p.reshape(N_GATE, bp, tp // lp, lp).transpose(1, 2, 0, 3)
    gt_s = gatet_s.reshape(N_GATE, bs, ts // ls, ls).transpose(1, 2, 0, 3)

    cw = jnp.zeros((SUBLANE, CONV_CH), F32).at[:CONV_W].set(conv_w[0].astype(F32))
    ng_a = gdn_norm_g[0].reshape(1, DV_A).astype(F32)
    ng_b = mlstm_norm_g[0].reshape(H_B, DV_B).astype(F32)
    ma_p, conv_p, gdn_st_p = _gdn(gdn_p.reshape(bp, tp, GDN_W), gate_p.reshape(bp, tp, LANE), gt_p, cw, ng_a,
                                  L=lp, G=gp_a)
    ma_s, conv_s, gdn_st_s = _gdn(gdn_s.reshape(bs, ts, GDN_W), gate_s.reshape(bs, ts, LANE), gt_s, cw, ng_a,
                                  L=ls, G=gs, state=(state_conv[0], state_gdn[0]))
    mb_p, c_p, nn_p, m_p = _mlstm(ml_p.reshape(bp, tp, MLP_W), gate_p.reshape(bp, tp, LANE), gt_p, ng_b,
                                  L=lp, G=gp_b)
    mb_s, c_s, nn_s, m_s = _mlstm(ml_s.reshape(bs, ts, MLP_W), gate_s.reshape(bs, ts, LANE), gt_s, ng_b,
                                  L=ls, G=gs,
                                  state=(state_mlstm_c[0], state_mlstm_n[0], state_mlstm_m[0].reshape(bs, 1, H_B)))
    half = H_A * DV_A

    rw = jnp.zeros((D_MODEL, LANE), F32).at[:, :N_EXPERTS].set(router_w[0])
    rb = jnp.full((1, LANE), NEG, F32).at[0, :N_EXPERTS].set(router_b[0])
    x1, x_sorted, info, seg_len = _outproj(xp, xs, ma_p.reshape(n_p, half), ma_s.reshape(n_s, half),
                                           mb_p.reshape(n_p, half), mb_s.reshape(n_s, half),
                                           w_out[0].astype(BF16), norm_moe_g[0].reshape(1, D_MODEL), rw, rb, MOE_TM)

    nt = n // MOE_TM
    nb = -(-(n * TOP_K + nt * N_EXPERTS * (SEG_ALIGN - 1)) // MOE_BLK) + N_EXPERTS
    tables = _block_tables(seg_len[:, :, 0].astype(jnp.int32), nb)
    y_sorted = _experts(tables, x_sorted, expert_w_gu[0], expert_b_gu[0].reshape(N_EXPERTS, 1, 2 * D_FF),
                        expert_w_down[0], expert_b_down[0].reshape(N_EXPERTS, 1, D_MODEL), nt)
    y_p, y_s = _combine(y_sorted, info, x1, p_prompt[0].reshape(n_p, PLE_DIM),
                        p_sample[0].reshape(n_s, PLE_DIM), norm_ple_g[0].reshape(1, D_MODEL),
                        ple_gate_w[0].astype(BF16), ple_w[0].astype(BF16), final_norm_g.reshape(1, D_MODEL), MOE_TM)

    return (y_p.reshape(bp, tp, D_MODEL), y_s.reshape(bs, ts, D_MODEL),
            conv_p[None], gdn_st_p[None], c_p[None], nn_p[None], m_p.reshape(1, bp, H_B),
            conv_s[None], gdn_st_s[None], c_s[None], nn_s[None], m_s.reshape(1, bs, H_B))
```

```python
import functools

import numpy as np
import jax
import jax.numpy as jnp
from jax import lax
from jax.experimental import pallas as pl
from jax.experimental.pallas import tpu as pltpu

F32 = jnp.float32
BF16 = jnp.bfloat16

D_MODEL = 1024
H_A, DK_A, DV_A = 4, 128, 128
H_B, DK_B, DV_B = 4, 64, 128
CONV_W = 4
CONV_CH = H_A * (2 * DK_A + DV_A)
N_EXPERTS = 32
TOP_K = 4
D_FF = 1024
SWIGLU_LIMIT = 7.0
SWIGLU_ALPHA = 1.702
PLE_DIM = 256
EPS = 1e-6
NEG = -1e30
CHUNK = 64

LANE = 128
SUBLANE = 8
GDN_W = CONV_CH + H_A * DV_A
MLP_W = 2 * H_B * LANE + 2 * H_B * DV_B
N_GATE = 16
W_ALL = GDN_W + MLP_W + LANE

VMEM_LIMIT = 48 * 1024 * 1024

MOE_TM = 256
MOE_BLK = 256
SEG_ALIGN = 16
MOE_CAP = -(-(MOE_TM * TOP_K + N_EXPERTS * (SEG_ALIGN - 1)) // LANE) * LANE

HI = lax.Precision.HIGHEST

_NN = (((1,), (0,)), ((), ()))
_NT = (((1,), (1,)), ((), ()))
_TN = (((0,), (0,)), ((), ()))


def _dot(a, b, dims=_NN):
    return lax.dot_general(a.astype(BF16), b.astype(BF16), dims, preferred_element_type=F32)


def _dot_hi(a, b, dims=_NN):
    return lax.dot_general(a, b, dims, precision=HI, preferred_element_type=F32)


def _rms(x, g):
    return x * lax.rsqrt(jnp.mean(x * x, axis=-1, keepdims=True) + EPS) * g


def _softplus(t):
    return jnp.maximum(t, 0.0) + jnp.log1p(jnp.exp(-jnp.abs(t)))


def _sigmoid(t):
    return 1.0 / (1.0 + jnp.exp(-t))


def _silu(t):
    return t * _sigmoid(t)


def _activate_gates(raw, idx, alog, bias):
    t = raw + bias
    g = -jnp.exp(alog) * _softplus(t)
    beta = _sigmoid(t)
    lf = -_softplus(-t)
    return jnp.where(idx < 4, g, jnp.where(idx < 8, beta, jnp.where(idx < 12, t, lf)))


def _two_segment_specs(tm, width, n_p_tiles):
    return [pl.BlockSpec((tm, width), lambda i: (jnp.minimum(i, n_p_tiles - 1), 0)),
            pl.BlockSpec((tm, width), lambda i: (jnp.maximum(i - n_p_tiles, 0), 0))]


def _for_segment(n_p_tiles, body):
    i = pl.program_id(0)

    @pl.when(i < n_p_tiles)
    def _():
        body(0)

    @pl.when(i >= n_p_tiles)
    def _():
        body(1)


def _inproj_kernel(xp_ref, xs_ref, g_ref, w_ref, wst_ref, pc_ref, pr_ref,
                   gdnp_ref, gdns_ref, mlp_ref, mls_ref, gatep_ref, gates_ref, gatetp_ref, gatets_ref,
                   *, n_p_tiles):
    tm = xp_ref.shape[0]

    def body(seg):
        x_ref = (xp_ref, xs_ref)[seg]
        gdn_ref, ml_ref = (gdnp_ref, gdns_ref)[seg], (mlp_ref, mls_ref)[seg]
        gate_ref, gatet_ref = (gatep_ref, gates_ref)[seg], (gatetp_ref, gatets_ref)[seg]
        hn = _rms(x_ref[...], g_ref[...]).astype(BF16)
        gdn_ref[...] = jnp.dot(hn, w_ref[:, :GDN_W], preferred_element_type=F32)
        ml_ref[...] = jnp.dot(hn, w_ref[:, GDN_W:GDN_W + MLP_W], preferred_element_type=F32)
        raw = jnp.dot(hn, w_ref[:, GDN_W + MLP_W:], preferred_element_type=F32)
        lane = lax.broadcasted_iota(jnp.int32, (tm, LANE), 1)
        gate_ref[...] = _activate_gates(raw, lane, pc_ref[0:1, :], pc_ref[1:2, :])
        raw_t = lax.dot_general(wst_ref[...], hn, _NT, preferred_element_type=F32)
        row = lax.broadcasted_iota(jnp.int32, (N_GATE, tm), 0)
        gatet_ref[...] = _activate_gates(raw_t, row, pr_ref[:, 0:1], pr_ref[:, 1:2])

    _for_segment(n_p_tiles, body)


def _inproj(xp, xs, g, w_all, ws_t, pcol, prow, tm):
    n_p, n_s = xp.shape[0], xs.shape[0]
    npt = n_p // tm

    def out2(width):
        return _two_segment_specs(tm, width, npt)

    def shp2(width):
        return [jax.ShapeDtypeStruct((n_p, width), F32), jax.ShapeDtypeStruct((n_s, width), F32)]

    return pl.pallas_call(
        functools.partial(_inproj_kernel, n_p_tiles=npt),
        grid=((n_p + n_s) // tm,),
        in_specs=_two_segment_specs(tm, D_MODEL, npt) + [
            pl.BlockSpec((1, D_MODEL), lambda i: (0, 0)),
            pl.BlockSpec((D_MODEL, W_ALL), lambda i: (0, 0)),
            pl.BlockSpec((N_GATE, D_MODEL), lambda i: (0, 0)),
            pl.BlockSpec((SUBLANE, LANE), lambda i: (0, 0)),
            pl.BlockSpec((N_GATE, LANE), lambda i: (0, 0)),
        ],
        out_specs=out2(GDN_W) + out2(MLP_W) + out2(LANE) + [
            pl.BlockSpec((N_GATE, tm), lambda i: (0, jnp.minimum(i, npt - 1))),
            pl.BlockSpec((N_GATE, tm), lambda i: (0, jnp.maximum(i - npt, 0))),
        ],
        out_shape=shp2(GDN_W) + shp2(MLP_W) + shp2(LANE) + [
            jax.ShapeDtypeStruct((N_GATE, n_p), F32), jax.ShapeDtypeStruct((N_GATE, n_s), F32)],
        compiler_params=pltpu.CompilerParams(dimension_semantics=("arbitrary",), vmem_limit_bytes=VMEM_LIMIT),
        name="inproj",
    )(xp, xs, g, w_all, ws_t, pcol, prow)


def _chunk_masks(L):
    ri = lax.broadcasted_iota(jnp.int32, (L, L), 0)
    ci = lax.broadcasted_iota(jnp.int32, (L, L), 1)
    return ri >= ci, ri > ci, ri <= ci


def _gdn_kernel(*refs, L, G, has_state):
    if has_state:
        (xin_ref, gate_ref, gatet_ref, cw_ref, ng_ref, cst_ref, s0_ref,
         mix_ref, cnew_ref, snew_ref, xc_ref, s_ref) = refs
    else:
        (xin_ref, gate_ref, gatet_ref, cw_ref, ng_ref,
         mix_ref, cnew_ref, snew_ref, xc_ref, s_ref) = refs
    c = pl.program_id(1)

    @pl.when(c == 0)
    def _():
        xc_ref[:, 0:SUBLANE, :] = jnp.zeros((G, SUBLANE, CONV_CH), F32)
        if has_state:
            xc_ref[:, SUBLANE - (CONV_W - 1):SUBLANE, :] = cst_ref[...]
            s_ref[...] = s0_ref[...]
        else:
            s_ref[...] = jnp.zeros_like(s_ref)

    @pl.when(c > 0)
    def _():
        xc_ref[:, 0:SUBLANE, :] = xc_ref[:, L:L + SUBLANE, :]

    tril, strict, triu = _chunk_masks(L)
    tril_f, triu_f = tril.astype(F32), triu.astype(F32)
    base = SUBLANE - (CONV_W - 1)

    chains = [(g, h) for g in range(G) for h in range(H_A)]
    s_old = [s_ref[g, h] for g, h in chains]
    for g in range(G):
        xc_ref[g, SUBLANE:SUBLANE + L, :] = xin_ref[g, :, :CONV_CH]

    q, k, v, beta, gc, gl, decay = [], [], [], [], [], [], []
    for g in range(G):
        conv = xc_ref[g, base:base + L, :] * cw_ref[0:1, :]
        for j in range(1, CONV_W):
            conv = conv + xc_ref[g, base + j:base + j + L, :] * cw_ref[j:j + 1, :]
        cnew_ref[g] = xc_ref[g, SUBLANE + L - (CONV_W - 1):SUBLANE + L, :]
        act = _silu(conv)
        gact = gate_ref[g]
        cum_c = _dot_hi(tril_f, gact)
        cum_r = _dot_hi(gatet_ref[g, 0], triu_f)
        for h in range(H_A):
            q.append(act[:, h * DK_A:(h + 1) * DK_A])
            k.append(act[:, H_A * DK_A + h * DK_A:H_A * DK_A + (h + 1) * DK_A])
            v.append(act[:, 2 * H_A * DK_A + h * DV_A:2 * H_A * DK_A + (h + 1) * DV_A])
            beta.append(gact[:, 4 + h:5 + h])
            gc.append(cum_c[:, h:h + 1])
            gl.append(cum_c[L - 1:L, h:h + 1])
            gr = cum_r[h:h + 1, :]
            decay.append(jnp.where(tril, jnp.exp(jnp.where(tril, cum_c[:, h:h + 1] - gr, 0.0)), 0.0))

    nc = range(len(chains))
    qss = [jnp.sum(q[i] * q[i], axis=-1, keepdims=True) for i in nc]
    kss = [jnp.sum(k[i] * k[i], axis=-1, keepdims=True) for i in nc]
    q = [q[i] * (lax.rsqrt(qss[i] + EPS) * (DK_A ** -0.5)) for i in nc]
    k = [k[i] * lax.rsqrt(kss[i] + EPS) for i in nc]
    kb = [k[i] * beta[i] for i in nc]
    egc = [jnp.exp(gc[i]) for i in nc]
    kk = [_dot(kb[i], k[i], _NT) for i in nc]
    qk = [_dot(q[i], k[i], _NT) for i in nc]
    eye = (lax.broadcasted_iota(jnp.int32, (L, L), 0) == lax.broadcasted_iota(jnp.int32, (L, L), 1)).astype(F32)
    pw = [-jnp.where(strict, kk[i] * decay[i], 0.0) for i in nc]
    t_inv = [eye + pw[i] for i in nc]
    span = 2
    while span < L:
        pw = [_dot(pw[i], pw[i]) for i in nc]
        t_inv = [t_inv[i] + _dot(t_inv[i], pw[i]) for i in nc]
        span *= 2
    sol = [_dot(t_inv[i], jnp.concatenate([v[i] * beta[i], kb[i] * egc[i]], axis=-1)) for i in nc]
    qs = [_dot(q[i] * egc[i], s_old[i]) for i in nc]
    ws = [_dot(sol[i][:, DV_A:], s_old[i]) for i in nc]
    v_new = [sol[i][:, :DV_A] - ws[i] for i in nc]
    o = [qs[i] + _dot(jnp.where(tril, qk[i] * decay[i], 0.0), v_new[i]) for i in nc]
    s_new = [s_old[i] * jnp.exp(gl[i]) + _dot(k[i] * jnp.exp(gl[i] - gc[i]), v_new[i], _TN) for i in nc]
    ms = [jnp.mean(o[i] * o[i], axis=-1, keepdims=True) for i in nc]
    on = [o[i] * lax.rsqrt(ms[i] + EPS) for i in nc]
    for i, (g, h) in enumerate(chains):
        z = xin_ref[g, :, CONV_CH + h * DV_A:CONV_CH + (h + 1) * DV_A]
        mix_ref[g, :, h * DV_A:(h + 1) * DV_A] = on[i] * ng_ref[...] * _silu(z)
    for i, (g, h) in enumerate(chains):
        s_ref[g, h] = s_new[i]
        snew_ref[g, h] = s_new[i]


def _gdn(gdn_in, gates, gates_t, cw, ng, *, L, G, state=None):
    n_seq, T, _ = gdn_in.shape
    n_c = T // L
    has_state = state is not None
    in_specs = [
        pl.BlockSpec((G, L, GDN_W), lambda b, c: (b, c, 0)),
        pl.BlockSpec((G, L, LANE), lambda b, c: (b, c, 0)),
        pl.BlockSpec((G, 1, N_GATE, L), lambda b, c: (b, c, 0, 0)),
        pl.BlockSpec((SUBLANE, CONV_CH), lambda b, c: (0, 0)),
        pl.BlockSpec((1, DV_A), lambda b, c: (0, 0)),
    ]
    args = [gdn_in, gates, gates_t, cw, ng]
    if has_state:
        conv_st, s0 = state
        in_specs += [
            pl.BlockSpec((G, CONV_W - 1, CONV_CH), lambda b, c: (b, 0, 0)),
            pl.BlockSpec((G, H_A, DK_A, DV_A), lambda b, c: (b, 0, 0, 0)),
        ]
        args += [conv_st, s0]
    return pl.pallas_call(
        functools.partial(_gdn_kernel, L=L, G=G, has_state=has_state),
        grid=(n_seq // G, n_c),
        in_specs=in_specs,
        out_specs=[
            pl.BlockSpec((G, L, H_A * DV_A), lambda b, c: (b, c, 0)),
            pl.BlockSpec((G, CONV_W - 1, CONV_CH), lambda b, c: (b, 0, 0)),
            pl.BlockSpec((G, H_A, DK_A, DV_A), lambda b, c: (b, 0, 0, 0)),
        ],
        out_shape=[
            jax.ShapeDtypeStruct((n_seq, T, H_A * DV_A), F32),
            jax.ShapeDtypeStruct((n_seq, CONV_W - 1, CONV_CH), F32),
            jax.ShapeDtypeStruct((n_seq, H_A, DK_A, DV_A), F32),
        ],
        scratch_shapes=[pltpu.VMEM((G, L + SUBLANE, CONV_CH), F32), pltpu.VMEM((G, H_A, DK_A, DV_A), F32)],
        compiler_params=pltpu.CompilerParams(dimension_semantics=("parallel", "arbitrary"),
                                             vmem_limit_bytes=VMEM_LIMIT),
        name=f"gdn_L{L}",
    )(*args)


def _mlstm_kernel(*refs, L, G, has_state):
    if has_state:
        (xin_ref, gate_ref, gatet_ref, ng_ref, c0_ref, n0_ref, m0_ref,
         mix_ref, cnew_ref, nnew_ref, mnew_ref, c_ref, n_ref, m_ref) = refs
    else:
        (xin_ref, gate_ref, gatet_ref, ng_ref,
         mix_ref, cnew_ref, nnew_ref, mnew_ref, c_ref, n_ref, m_ref) = refs
    c = pl.program_id(1)

    @pl.when(c == 0)
    def _():
        c_ref[...] = jnp.zeros_like(c_ref)
        n_ref[...] = jnp.zeros_like(n_ref)
        m_ref[...] = jnp.zeros_like(m_ref)
        if has_state:
            c_ref[:, :, 0:DK_B, :] = c0_ref[...]
            n_ref[:, 0:H_B, 0:DK_B] = n0_ref[...]
            m_ref[:, 0:1, 0:H_B] = m0_ref[...]

    tril, _, triu = _chunk_masks(L)
    tril_f, triu_f = tril.astype(F32), triu.astype(F32)

    chains = [(g, h) for g in range(G) for h in range(H_B)]
    nc = range(len(chains))
    c_old = [c_ref[g, h] for g, h in chains]
    n_old = [n_ref[g, h:h + 1, :] for g, h in chains]
    m_old = [m_ref[g, 0:1, h:h + 1] for g, h in chains]

    v0 = 2 * H_B * LANE
    q = [xin_ref[g, :, h * LANE:(h + 1) * LANE] * (DK_B ** -0.5) for g, h in chains]
    k = [xin_ref[g, :, (H_B + h) * LANE:(H_B + h + 1) * LANE] for g, h in chains]
    v = [xin_ref[g, :, v0 + h * DV_B:v0 + (h + 1) * DV_B] for g, h in chains]
    ig_c, b_c, b_last, d_log = [], [], [], []
    for g in range(G):
        gact = gate_ref[g]
        gact_t = gatet_ref[g, 0]
        cum_c = _dot_hi(tril_f, gact)
        cum_r = _dot_hi(gact_t, triu_f)
        for h in range(H_B):
            ig_c.append(gact[:, 8 + h:9 + h])
            b_c.append(cum_c[:, 12 + h:13 + h])
            b_last.append(cum_c[L - 1:L, 12 + h:13 + h])
            d_log.append(jnp.where(tril, cum_c[:, 12 + h:13 + h] - cum_r[12 + h:13 + h, :]
                                   + gact_t[8 + h:9 + h, :], NEG))
    qk = [_dot(q[i], k[i], _NT) for i in nc]
    qc = [_dot(q[i], c_old[i]) for i in nc]
    inter = [b_c[i] + m_old[i] for i in nc]
    m_t = [jnp.maximum(inter[i], jnp.max(d_log[i], axis=-1, keepdims=True)) for i in nc]
    s = [qk[i] * jnp.exp(d_log[i] - m_t[i]) for i in nc]
    e_inter = [jnp.exp(inter[i] - m_t[i]) for i in nc]
    sv = [_dot(s[i], v[i]) for i in nc]
    m_new = [m_t[i][L - 1:L, :] for i in nc]
    kw = [k[i] * jnp.exp(b_last[i] - b_c[i] + ig_c[i] - m_new[i]) for i in nc]
    f_tot = [jnp.exp(b_last[i] + m_old[i] - m_new[i]) for i in nc]
    c_new = [f_tot[i] * c_old[i] + _dot(kw[i], v[i], _TN) for i in nc]
    n_new = [f_tot[i] * n_old[i] + jnp.sum(kw[i], axis=0, keepdims=True) for i in nc]
    qn = [jnp.sum(q[i] * n_old[i], axis=-1, keepdims=True) for i in nc]
    ssum = [jnp.sum(s[i], axis=-1, keepdims=True) for i in nc]
    den = [jnp.maximum(jnp.abs(e_inter[i] * qn[i] + ssum[i]), jnp.exp(-m_t[i])) for i in nc]
    hh = [(e_inter[i] * qc[i] + sv[i]) / den[i] for i in nc]
    ms = [jnp.mean(hh[i] * hh[i], axis=-1, keepdims=True) for i in nc]
    hn = [hh[i] * lax.rsqrt(ms[i] + EPS) for i in nc]
    for i, (g, h) in enumerate(chains):
        og = xin_ref[g, :, v0 + H_B * DV_B + h * DV_B:v0 + H_B * DV_B + (h + 1) * DV_B]
        mix_ref[g, :, h * DV_B:(h + 1) * DV_B] = hn[i] * ng_ref[h:h + 1, :] * _sigmoid(og)
    for i, (g, h) in enumerate(chains):
        c_ref[g, h] = c_new[i]
        n_ref[g, h:h + 1, :] = n_new[i]
        m_ref[g, 0:1, h:h + 1] = m_new[i]
        cnew_ref[g, h] = c_new[i][0:DK_B, :]
        nnew_ref[g, h:h + 1, :] = n_new[i][:, 0:DK_B]
        mnew_ref[g, 0:1, h:h + 1] = m_new[i]


def _mlstm(ml_in, gates, gates_t, ng, *, L, G, state=None):
    n_seq, T, _ = ml_in.shape
    n_c = T // L
    has_state = state is not None
    in_specs = [
        pl.BlockSpec((G, L, MLP_W), lambda b, c: (b, c, 0)),
        pl.BlockSpec((G, L, LANE), lambda b, c: (b, c, 0)),
        pl.BlockSpec((G, 1, N_GATE, L), lambda b, c: (b, c, 0, 0)),
        pl.BlockSpec((H_B, DV_B), lambda b, c: (0, 0)),
    ]
    args = [ml_in, gates, gates_t, ng]
    if has_state:
        c0, n0, m0 = state
        in_specs += [
            pl.BlockSpec((G, H_B, DK_B, DV_B), lambda b, c: (b, 0, 0, 0)),
            pl.BlockSpec((G, H_B, DK_B), lambda b, c: (b, 0, 0)),
            pl.BlockSpec((G, 1, H_B), lambda b, c: (b, 0, 0)),
        ]
        args += [c0, n0, m0]
    return pl.pallas_call(
        functools.partial(_mlstm_kernel, L=L, G=G, has_state=has_state),
        grid=(n_seq // G, n_c),
        in_specs=in_specs,
        out_specs=[
            pl.BlockSpec((G, L, H_B * DV_B), lambda b, c: (b, c, 0)),
            pl.BlockSpec((G, H_B, DK_B, DV_B), lambda b, c: (b, 0, 0, 0)),
            pl.BlockSpec((G, H_B, DK_B), lambda b, c: (b, 0, 0)),
            pl.BlockSpec((G, 1, H_B), lambda b, c: (b, 0, 0)),
        ],
        out_shape=[
            jax.ShapeDtypeStruct((n_seq, T, H_B * DV_B), F32),
            jax.ShapeDtypeStruct((n_seq, H_B, DK_B, DV_B), F32),
            jax.ShapeDtypeStruct((n_seq, H_B, DK_B), F32),
            jax.ShapeDtypeStruct((n_seq, 1, H_B), F32),
        ],
        scratch_shapes=[pltpu.VMEM((G, H_B, LANE, DV_B), F32), pltpu.VMEM((G, SUBLANE, LANE), F32),
                        pltpu.VMEM((G, SUBLANE, LANE), F32)],
        compiler_params=pltpu.CompilerParams(dimension_semantics=("parallel", "arbitrary"),
                                             vmem_limit_bytes=VMEM_LIMIT),
        name=f"mlstm_L{L}",
    )(*args)


def _outproj_kernel(xp_ref, xs_ref, map_ref, mas_ref, mbp_ref, mbs_ref, wo_ref, g_ref, rw_ref, rb_ref,
                    x1_ref, xsort_ref, info_ref, cpad_ref, *, n_p_tiles):
    half = H_A * DV_A
    tm = xp_ref.shape[0]

    def body(seg):
        x_ref, ma_ref, mb_ref = (xp_ref, xs_ref)[seg], (map_ref, mas_ref)[seg], (mbp_ref, mbs_ref)[seg]
        x1 = (x_ref[...] + jnp.dot(ma_ref[...].astype(BF16), wo_ref[:half, :], preferred_element_type=F32)
              + jnp.dot(mb_ref[...].astype(BF16), wo_ref[half:, :], preferred_element_type=F32))
        x1_ref[...] = x1
        hn = _rms(x1, g_ref[...])
        hn_hi = hn.astype(BF16)
        hn_lo = (hn - hn_hi.astype(F32)).astype(BF16)
        logits = (jnp.dot(hn_hi, rw_ref[0], preferred_element_type=F32)
                  + jnp.dot(hn_hi, rw_ref[1], preferred_element_type=F32)
                  + jnp.dot(hn_lo, rw_ref[0], preferred_element_type=F32)) + rb_ref[...]

        vals = logits.T[:N_EXPERTS, :]
        e_iota = lax.broadcasted_iota(jnp.int32, (N_EXPERTS, tm), 0)
        sels, tops = [], []
        for _ in range(TOP_K):
            m = jnp.max(vals, axis=0, keepdims=True)
            first = jnp.min(jnp.where(vals == m, e_iota, N_EXPERTS), axis=0, keepdims=True)
            sel = e_iota == first
            vals = jnp.where(sel, -jnp.inf, vals)
            sels.append(sel)
            tops.append(m)
        ex = [jnp.exp(t - tops[0]) for t in tops]
        den = ex[0] + ex[1] + ex[2] + ex[3]
        gates = [e / den for e in ex]
        mask = sels[0].astype(F32) + sels[1].astype(F32) + sels[2].astype(F32) + sels[3].astype(F32)
        ri = lax.broadcasted_iota(jnp.int32, (tm, tm), 0)
        ci = lax.broadcasted_iota(jnp.int32, (tm, tm), 1)
        rank = _dot(mask, (ri < ci).astype(F32))
        cnt = jnp.sum(mask, axis=1, keepdims=True)
        cpad = jnp.ceil(cnt * (1.0 / SEG_ALIGN)) * SEG_ALIGN
        cpad_b = jnp.broadcast_to(cpad, (N_EXPERTS, tm))
        er = lax.broadcasted_iota(jnp.int32, (N_EXPERTS, N_EXPERTS), 0)
        ec = lax.broadcasted_iota(jnp.int32, (N_EXPERTS, N_EXPERTS), 1)
        seg_off = _dot((er > ec).astype(F32), cpad_b)
        pos = seg_off + rank
        q = [jnp.sum(jnp.where(s, pos, 0.0), axis=0, keepdims=True) for s in sels]

        j_iota = lax.broadcasted_iota(jnp.int32, (MOE_CAP, tm), 0).astype(F32)
        perm = jnp.zeros((MOE_CAP, tm), F32)
        for kk in range(TOP_K):
            perm = jnp.where(j_iota == q[kk], 1.0, perm)
        xsorted = _dot(perm, hn)
        xsort_ref[...] = xsorted.astype(BF16)

        r_iota = lax.broadcasted_iota(jnp.int32, (LANE, tm), 0)
        info = jnp.zeros((LANE, tm), F32)
        for kk in range(TOP_K):
            info = jnp.where(r_iota == kk, q[kk], info)
            info = jnp.where(r_iota == TOP_K + kk, gates[kk], info)
        info_ref[...] = info.T
        cpad_ref[0] = cpad_b[:, :LANE]

    _for_segment(n_p_tiles, body)


def _outproj(xp, xs, ma_p, ma_s, mb_p, mb_s, w_out, g, rw, rb, tm):
    n_p, n_s = xp.shape[0], xs.shape[0]
    n = n_p + n_s
    nt = n // tm
    npt = n_p // tm
    half = H_A * DV_A
    return pl.pallas_call(
        functools.partial(_outproj_kernel, n_p_tiles=npt),
        grid=(nt,),
        in_specs=_two_segment_specs(tm, D_MODEL, npt) + _two_segment_specs(tm, half, npt)
        + _two_segment_specs(tm, half, npt) + [
            pl.BlockSpec((D_MODEL, D_MODEL), lambda i: (0, 0)),
            pl.BlockSpec((1, D_MODEL), lambda i: (0, 0)),
            pl.BlockSpec((2, D_MODEL, LANE), lambda i: (0, 0, 0)),
            pl.BlockSpec((1, LANE), lambda i: (0, 0)),
        ],
        out_specs=[
            pl.BlockSpec((tm, D_MODEL), lambda i: (i, 0)),
            pl.BlockSpec((MOE_CAP, D_MODEL), lambda i: (i, 0)),
            pl.BlockSpec((tm, LANE), lambda i: (i, 0)),
            pl.BlockSpec((1, N_EXPERTS, LANE), lambda i: (i, 0, 0)),
        ],
        out_shape=[
            jax.ShapeDtypeStruct((n, D_MODEL), F32),
            jax.ShapeDtypeStruct((nt * MOE_CAP, D_MODEL), BF16),
            jax.ShapeDtypeStruct((n, LANE), F32),
            jax.ShapeDtypeStruct((nt, N_EXPERTS, LANE), F32),
        ],
        compiler_params=pltpu.CompilerParams(dimension_semantics=("arbitrary",), vmem_limit_bytes=VMEM_LIMIT),
        name="outproj",
    )(xp, xs, ma_p, ma_s, mb_p, mb_s, w_out, g, rw, rb)


def _expert_kernel(be_ref, bj_ref, tf_ref, tl_ref, cov_ref, nu_ref, vt_ref, ct_ref, lt_ref,
                   xs_hbm, wgu_ref, bgu_ref, wd_ref, bd_ref, ys_hbm,
                   xbuf, ybuf, gsem, ssem, wgu_bf, wd_bf, *, nt):
    b = pl.program_id(0)
    n_used = nu_ref[0]
    slot = b % 2

    def for_segments(bb, fn):
        e = be_ref[bb]
        base = bj_ref[bb] * MOE_BLK

        def body(t, carry):
            lo = jnp.maximum(vt_ref[e * nt + t], base)
            hi = jnp.minimum(ct_ref[e * nt + t], base + MOE_BLK)
            ln = jnp.maximum(hi - lo, 0)
            src = lt_ref[e * nt + t] + lo
            dst = lo - base

            @pl.when(ln > 0)
            def _():
                fn(pl.multiple_of(src, SEG_ALIGN), pl.multiple_of(dst, SEG_ALIGN), pl.multiple_of(ln, SEG_ALIGN))
            return carry

        lax.fori_loop(tf_ref[bb], tl_ref[bb] + 1, body, 0)

    def gather_copy(s, src, dst, size):
        return pltpu.make_async_copy(xs_hbm.at[pl.ds(src, size)], xbuf.at[s, pl.ds(dst, size)], gsem.at[s])

    def scatter_copy(s, src, dst, size):
        return pltpu.make_async_copy(ybuf.at[s, pl.ds(dst, size)], ys_hbm.at[pl.ds(src, size)], ssem.at[s])

    def gather(bb, s):
        for_segments(bb, lambda src, dst, size: gather_copy(s, src, dst, size).start())

    def scatter(bb, s):
        for_segments(bb, lambda src, dst, size: scatter_copy(s, src, dst, size).start())

    def wait_rows(count, copy, s):
        @pl.when(count > 0)
        def _():
            copy(s, 0, 0, pl.multiple_of(count, SEG_ALIGN)).wait()

    @pl.when(b == 0)
    def _():
        xbuf[...] = jnp.zeros_like(xbuf)
        gather(0, 0)

    @pl.when(b + 1 < n_used)
    def _():
        gather(b + 1, 1 - slot)

    @pl.when(b < n_used)
    def _():
        wait_rows(cov_ref[b], gather_copy, slot)
        e_prev = be_ref[jnp.maximum(b - 1, 0)]

        @pl.when(jnp.logical_or(b == 0, be_ref[b] != e_prev))
        def _():
            wgu_bf[...] = wgu_ref[0].astype(BF16)
            wd_bf[...] = wd_ref[0].astype(BF16)

        hgu = jnp.dot(xbuf[slot], wgu_bf[...], preferred_element_type=F32) + bgu_ref[0]
        gate = jnp.minimum(hgu[:, :D_FF], SWIGLU_LIMIT)
        up = jnp.clip(hgu[:, D_FF:], -SWIGLU_LIMIT, SWIGLU_LIMIT)
        act = (up + 1.0) * gate * _sigmoid(SWIGLU_ALPHA * gate)
        y = jnp.dot(act.astype(BF16), wd_bf[...], preferred_element_type=F32) + bd_ref[0]

        @pl.when(b >= 2)
        def _():
            wait_rows(cov_ref[jnp.maximum(b - 2, 0)], scatter_copy, slot)

        ybuf[slot] = y.astype(BF16)
        scatter(b, slot)

        @pl.when(b == n_used - 1)
        def _():
            wait_rows(cov_ref[b], scatter_copy, slot)

            @pl.when(b >= 1)
            def _():
                wait_rows(cov_ref[jnp.maximum(b - 1, 0)], scatter_copy, 1 - slot)


def _experts(tables, xs, w_gu, b_gu, w_down, b_down, nt):
    nb = tables[0].shape[0]

    def wblk(b, be, bj, tf, tl, cov, nu, vt, ct, lt):
        return (be[jnp.minimum(b, nu[0] - 1)], 0, 0)

    grid_spec = pltpu.PrefetchScalarGridSpec(
        num_scalar_prefetch=len(tables),
        grid=(nb,),
        in_specs=[
            pl.BlockSpec(memory_space=pl.ANY),
            pl.BlockSpec((1, D_MODEL, 2 * D_FF), wblk),
            pl.BlockSpec((1, 1, 2 * D_FF), wblk),
            pl.BlockSpec((1, D_FF, D_MODEL), wblk),
            pl.BlockSpec((1, 1, D_MODEL), wblk),
        ],
        out_specs=pl.BlockSpec(memory_space=pl.ANY),
        scratch_shapes=[
            pltpu.VMEM((2, MOE_BLK, D_MODEL), BF16),
            pltpu.VMEM((2, MOE_BLK, D_MODEL), BF16),
            pltpu.SemaphoreType.DMA((2,)),
            pltpu.SemaphoreType.DMA((2,)),
            pltpu.VMEM((D_MODEL, 2 * D_FF), BF16),
            pltpu.VMEM((D_FF, D_MODEL), BF16),
        ],
    )
    return pl.pallas_call(
        functools.partial(_expert_kernel, nt=nt),
        grid_spec=grid_spec,
        out_shape=jax.ShapeDtypeStruct(xs.shape, xs.dtype),
        input_output_aliases={len(tables): 0},
        compiler_params=pltpu.CompilerParams(dimension_semantics=("arbitrary",), vmem_limit_bytes=VMEM_LIMIT),
        name="experts",
    )(*tables, xs, w_gu, b_gu, w_down, b_down)


def _combine_kernel(ys_ref, info_ref, x1_ref, pp_ref, ps_ref, gple_ref, wg_ref, wp_ref, gfin_ref,
                    outp_ref, outs_ref, *, n_p_tiles):
    tm = x1_ref.shape[0]

    def body(seg):
        p_ref, out_ref = (pp_ref, ps_ref)[seg], (outp_ref, outs_ref)[seg]
        info = info_ref[...]
        j_iota = lax.broadcasted_iota(jnp.int32, (tm, MOE_CAP), 1).astype(F32)
        gmat = jnp.zeros((tm, MOE_CAP), F32)
        for kk in range(TOP_K):
            gmat = jnp.where(j_iota == info[:, kk:kk + 1], info[:, TOP_K + kk:TOP_K + kk + 1], gmat)
        x2 = x1_ref[...] + jnp.dot(gmat.astype(BF16), ys_ref[...], preferred_element_type=F32)
        hn = _rms(x2, gple_ref[...]).astype(BF16)
        gate = _sigmoid(jnp.dot(hn, wg_ref[...], preferred_element_type=F32))
        pe = jnp.dot(p_ref[...].astype(BF16), wp_ref[...], preferred_element_type=F32)
        x3 = x2 + gate * pe
        out_ref[...] = _rms(x3, gfin_ref[...])

    _for_segment(n_p_tiles, body)


def _combine(ys, info, x1, pp, ps, g_ple, w_gate, w_p, g_fin, tm):
    n_p, n_s = pp.shape[0], ps.shape[0]
    n = n_p + n_s
    nt = n // tm
    npt = n_p // tm
    return pl.pallas_call(
        functools.partial(_combine_kernel, n_p_tiles=npt),
        grid=(nt,),
        in_specs=[
            pl.BlockSpec((MOE_CAP, D_MODEL), lambda i: (i, 0)),
            pl.BlockSpec((tm, LANE), lambda i: (i, 0)),
            pl.BlockSpec((tm, D_MODEL), lambda i: (i, 0)),
        ] + _two_segment_specs(tm, PLE_DIM, npt) + [
            pl.BlockSpec((1, D_MODEL), lambda i: (0, 0)),
            pl.BlockSpec((D_MODEL, D_MODEL), lambda i: (0, 0)),
            pl.BlockSpec((PLE_DIM, D_MODEL), lambda i: (0, 0)),
            pl.BlockSpec((1, D_MODEL), lambda i: (0, 0)),
        ],
        out_specs=_two_segment_specs(tm, D_MODEL, npt),
        out_shape=[jax.ShapeDtypeStruct((n_p, D_MODEL), F32), jax.ShapeDtypeStruct((n_s, D_MODEL), F32)],
        compiler_params=pltpu.CompilerParams(dimension_semantics=("arbitrary",), vmem_limit_bytes=VMEM_LIMIT),
        name="combine",
    )(ys, info, x1, pp, ps, g_ple, w_gate, w_p, g_fin)


def _block_tables(seg_len, nb):
    nt = seg_len.shape[0]
    seg_off = jnp.cumsum(seg_len, axis=1) - seg_len
    seg_end = jnp.cumsum(seg_len, axis=0).T
    seg_start = seg_end - seg_len.T
    n_rows = seg_end[:, -1]
    n_blk = (n_rows + MOE_BLK - 1) // MOE_BLK
    blk_end = jnp.cumsum(n_blk)
    b = jnp.arange(nb, dtype=jnp.int32)
    block_e = jnp.minimum(jnp.sum((blk_end[None, :] <= b[:, None]).astype(jnp.int32), axis=1), N_EXPERTS - 1)
    block_j = b - (blk_end - n_blk)[block_e]
    base = block_j * MOE_BLK
    t_first = jnp.sum((seg_end[block_e] <= base[:, None]).astype(jnp.int32), axis=1)
    t_last = jnp.sum((seg_start[block_e] < (base + MOE_BLK)[:, None]).astype(jnp.int32), axis=1) - 1
    cover = jnp.clip(n_rows[block_e] - base, 0, MOE_BLK)
    seg_shift = (jnp.arange(nt, dtype=jnp.int32)[:, None] * MOE_CAP + seg_off).T - seg_start
    tables = (block_e, block_j, t_first, t_last, cover, blk_end[-1:], seg_start.reshape(-1),
              seg_end.reshape(-1), seg_shift.reshape(-1))
    return tuple(t.astype(jnp.int32) for t in tables)


def _rearranged_in_weights(w_in):
    o = np.cumsum([0, CONV_CH, H_A * DV_A, H_A, H_A, H_B * DK_B, H_B * DK_B, H_B * DV_B, H_B * DV_B, H_B, H_B])
    conv_in, z_a, a_a, b_a, q_b, k_b, v_b, o_b, i_b, f_b = (w_in[:, int(o[j]):int(o[j + 1])] for j in range(10))
    zpad = jnp.zeros((D_MODEL, LANE - DK_B), w_in.dtype)

    def pad_heads(w):
        return jnp.concatenate([jnp.concatenate([w[:, h * DK_B:(h + 1) * DK_B], zpad], axis=1) for h in range(H_B)],
                               axis=1)

    small = jnp.concatenate([a_a, b_a, i_b, f_b], axis=1)
    w_all = jnp.concatenate([conv_in, z_a, pad_heads(q_b), pad_heads(k_b), v_b, o_b,
                             small, jnp.zeros((D_MODEL, LANE - N_GATE), w_in.dtype)], axis=1)
    return w_all.astype(BF16), small.T.astype(BF16)


def _gate_params(a_log, dt_bias, i_bias, f_bias):
    z4 = jnp.zeros((4,), F32)
    alog = jnp.concatenate([a_log.astype(F32), z4, z4, z4])
    bias = jnp.concatenate([dt_bias.astype(F32), z4, i_bias.astype(F32), f_bias.astype(F32)])
    pad = jnp.zeros((LANE - N_GATE,), F32)
    pcol = jnp.zeros((SUBLANE, LANE), F32).at[0].set(jnp.concatenate([alog, pad])).at[1].set(
        jnp.concatenate([bias, pad]))
    prow = jnp.zeros((N_GATE, LANE), F32).at[:, 0].set(alog).at[:, 1].set(bias)
    return pcol, prow


def kernel(x_prompt, x_sample, p_prompt, p_sample, state_conv, state_gdn, state_mlstm_c, state_mlstm_n, state_mlstm_m, norm_attn_g, w_in, conv_w, gdn_a_log, gdn_dt_bias, gdn_norm_g, mlstm_i_bias, mlstm_f_bias, mlstm_norm_g, w_out, norm_moe_g, router_w, router_b, expert_w_gu, expert_b_gu, expert_w_down, expert_b_down, norm_ple_g, ple_gate_w, ple_w, final_norm_g):
    bp, tp, _ = x_prompt.shape
    bs, ts, _ = x_sample.shape
    n_p, n_s = bp * tp, bs * ts
    n = n_p + n_s
    lp, ls = min(tp, CHUNK), min(ts, CHUNK)
    tm = 512
    gp_a = 4 if bp % 4 == 0 else 1
    gp_b = 2 if bp % 2 == 0 else 1
    gs = 8 if bs % 8 == 0 else 1
    assert tp % lp == 0 and ts % ls == 0 and n_p % tm == 0 and n_s % tm == 0 and ls % SUBLANE == 0

    xp = x_prompt.reshape(n_p, D_MODEL)
    xs = x_sample.reshape(n_s, D_MODEL)

    w_all, ws_t = _rearranged_in_weights(w_in[0])
    pcol, prow = _gate_params(gdn_a_log[0], gdn_dt_bias[0], mlstm_i_bias[0], mlstm_f_bias[0])
    gdn_p, gdn_s, ml_p, ml_s, gate_p, gate_s, gatet_p, gatet_s = _inproj(
        xp, xs, norm_attn_g[0].reshape(1, D_MODEL), w_all, ws_t, pcol, prow, tm)
    gt_p = gatet_p.reshape(N_GATE, bp, tp // lp, lp).transpose(1, 2, 0, 3)
    gt_s = gatet_s.reshape(N_GATE, bs, ts // ls, ls).transpose(1, 2, 0, 3)

    cw = jnp.zeros((SUBLANE, CONV_CH), F32).at[:CONV_W].set(conv_w[0].astype(F32))
    ng_a = gdn_norm_g[0].reshape(1, DV_A).astype(F32)
    ng_b = mlstm_norm_g[0].reshape(H_B, DV_B).astype(F32)
    ma_p, conv_p, gdn_st_p = _gdn(gdn_p.reshape(bp, tp, GDN_W), gate_p.reshape(bp, tp, LANE), gt_p, cw, ng_a,
                                  L=lp, G=gp_a)
    ma_s, conv_s, gdn_st_s = _gdn(gdn_s.reshape(bs, ts, GDN_W), gate_s.reshape(bs, ts, LANE), gt_s, cw, ng_a,
                                  L=ls, G=gs, state=(state_conv[0], state_gdn[0]))
    mb_p, c_p, nn_p, m_p = _mlstm(ml_p.reshape(bp, tp, MLP_W), gate_p.reshape(bp, tp, LANE), gt_p, ng_b,
                                  L=lp, G=gp_b)
    mb_s, c_s, nn_s, m_s = _mlstm(ml_s.reshape(bs, ts, MLP_W), gate_s.reshape(bs, ts, LANE), gt_s, ng_b,
                                  L=ls, G=gs,
                                  state=(state_mlstm_c[0], state_mlstm_n[0], state_mlstm_m[0].reshape(bs, 1, H_B)))
    half = H_A * DV_A

    rw = jnp.zeros((D_MODEL, LANE), F32).at[:, :N_EXPERTS].set(router_w[0])
    rw_hi = rw.astype(BF16)
    rw = jnp.stack([rw_hi, (rw - rw_hi.astype(F32)).astype(BF16)])
    rb = jnp.full((1, LANE), NEG, F32).at[0, :N_EXPERTS].set(router_b[0])
    x1, x_sorted, info, seg_len = _outproj(xp, xs, ma_p.reshape(n_p, half), ma_s.reshape(n_s, half),
                                           mb_p.reshape(n_p, half), mb_s.reshape(n_s, half),
                                           w_out[0].astype(BF16), norm_moe_g[0].reshape(1, D_MODEL), rw, rb, MOE_TM)

    nt = n // MOE_TM
    nb = -(-(n * TOP_K + nt * N_EXPERTS * (SEG_ALIGN - 1)) // MOE_BLK) + N_EXPERTS
    tables = _block_tables(seg_len[:, :, 0].astype(jnp.int32), nb)
    y_sorted = _experts(tables, x_sorted, expert_w_gu[0], expert_b_gu[0].reshape(N_EXPERTS, 1, 2 * D_FF),
                        expert_w_down[0], expert_b_down[0].reshape(N_EXPERTS, 1, D_MODEL), nt)
    y_p, y_s = _combine(y_sorted, info, x1, p_prompt[0].reshape(n_p, PLE_DIM),
                        p_sample[0].reshape(n_s, PLE_DIM), norm_ple_g[0].reshape(1, D_MODEL),
                        ple_gate_w[0].astype(BF16), ple_w[0].astype(BF16), final_norm_g.reshape(1, D_MODEL), MOE_TM)

    return (y_p.reshape(bp, tp, D_MODEL), y_s.reshape(bs, ts, D_MODEL),
            conv_p[None], gdn_st_p[None], c_p[None], nn_p[None], m_p.reshape(1, bp, H_B),
            conv_s[None], gdn_st_s[None], c_s[None], nn_s[None], m_s.reshape(1, bs, H_B))
```

```python
import functools

import numpy as np
import jax
import jax.numpy as jnp
from jax import lax
from jax.experimental import pallas as pl
from jax.experimental.pallas import tpu as pltpu

F32 = jnp.float32
BF16 = jnp.bfloat16

D_MODEL = 1024
H_A, DK_A, DV_A = 4, 128, 128
H_B, DK_B, DV_B = 4, 64, 128
CONV_W = 4
CONV_CH = H_A * (2 * DK_A + DV_A)
N_EXPERTS = 32
TOP_K = 4
D_FF = 1024
SWIGLU_LIMIT = 7.0
SWIGLU_ALPHA = 1.702
PLE_DIM = 256
EPS = 1e-6
NEG = -1e30
CHUNK = 64

LANE = 128
SUBLANE = 8
GDN_W = CONV_CH + H_A * DV_A
MLP_W = 2 * H_B * LANE + 2 * H_B * DV_B
N_GATE = 16
W_ALL = GDN_W + MLP_W + LANE

VMEM_LIMIT = 48 * 1024 * 1024

MOE_TM = 256
MOE_BLK = 256
SEG_ALIGN = 16
MOE_CAP = -(-(MOE_TM * TOP_K + N_EXPERTS * (SEG_ALIGN - 1)) // LANE) * LANE
SEG_UNROLL = 4

HI = lax.Precision.HIGHEST

_NN = (((1,), (0,)), ((), ()))
_NT = (((1,), (1,)), ((), ()))
_TN = (((0,), (0,)), ((), ()))


def _dot(a, b, dims=_NN):
    return lax.dot_general(a.astype(BF16), b.astype(BF16), dims, preferred_element_type=F32)


def _dot_hi(a, b, dims=_NN):
    return lax.dot_general(a, b, dims, precision=HI, preferred_element_type=F32)


def _rms(x, g):
    return x * lax.rsqrt(jnp.mean(x * x, axis=-1, keepdims=True) + EPS) * g


def _softplus(t):
    return jnp.maximum(t, 0.0) + jnp.log1p(jnp.exp(-jnp.abs(t)))


def _sigmoid(t):
    return 1.0 / (1.0 + jnp.exp(-t))


def _silu(t):
    return t * _sigmoid(t)


def _activate_gates(raw, idx, alog, bias):
    t = raw + bias
    g = -jnp.exp(alog) * _softplus(t)
    beta = _sigmoid(t)
    lf = -_softplus(-t)
    return jnp.where(idx < 4, g, jnp.where(idx < 8, beta, jnp.where(idx < 12, t, lf)))


def _two_segment_specs(tm, width, n_p_tiles):
    return [pl.BlockSpec((tm, width), lambda i: (jnp.minimum(i, n_p_tiles - 1), 0)),
            pl.BlockSpec((tm, width), lambda i: (jnp.maximum(i - n_p_tiles, 0), 0))]


def _for_segment(n_p_tiles, body):
    i = pl.program_id(0)

    @pl.when(i < n_p_tiles)
    def _():
        body(0)

    @pl.when(i >= n_p_tiles)
    def _():
        body(1)


def _inproj_kernel(xp_ref, xs_ref, g_ref, w_ref, wst_ref, pc_ref, pr_ref,
                   gdnp_ref, gdns_ref, mlp_ref, mls_ref, gatep_ref, gates_ref, gatetp_ref, gatets_ref,
                   *, n_p_tiles):
    tm = xp_ref.shape[0]

    def body(seg):
        x_ref = (xp_ref, xs_ref)[seg]
        gdn_ref, ml_ref = (gdnp_ref, gdns_ref)[seg], (mlp_ref, mls_ref)[seg]
        gate_ref, gatet_ref = (gatep_ref, gates_ref)[seg], (gatetp_ref, gatets_ref)[seg]
        hn = _rms(x_ref[...], g_ref[...]).astype(BF16)
        gdn_ref[...] = jnp.dot(hn, w_ref[:, :GDN_W], preferred_element_type=F32)
        ml_ref[...] = jnp.dot(hn, w_ref[:, GDN_W:GDN_W + MLP_W], preferred_element_type=F32)
        raw = jnp.dot(hn, w_ref[:, GDN_W + MLP_W:], preferred_element_type=F32)
        lane = lax.broadcasted_iota(jnp.int32, (tm, LANE), 1)
        gate_ref[...] = _activate_gates(raw, lane, pc_ref[0:1, :], pc_ref[1:2, :])
        raw_t = lax.dot_general(wst_ref[...], hn, _NT, preferred_element_type=F32)
        row = lax.broadcasted_iota(jnp.int32, (N_GATE, tm), 0)
        gatet_ref[...] = _activate_gates(raw_t, row, pr_ref[:, 0:1], pr_ref[:, 1:2])

    _for_segment(n_p_tiles, body)


def _inproj(xp, xs, g, w_all, ws_t, pcol, prow, tm):
    n_p, n_s = xp.shape[0], xs.shape[0]
    npt = n_p // tm

    def out2(width):
        return _two_segment_specs(tm, width, npt)

    def shp2(width):
        return [jax.ShapeDtypeStruct((n_p, width), F32), jax.ShapeDtypeStruct((n_s, width), F32)]

    return pl.pallas_call(
        functools.partial(_inproj_kernel, n_p_tiles=npt),
        grid=((n_p + n_s) // tm,),
        in_specs=_two_segment_specs(tm, D_MODEL, npt) + [
            pl.BlockSpec((1, D_MODEL), lambda i: (0, 0)),
            pl.BlockSpec((D_MODEL, W_ALL), lambda i: (0, 0)),
            pl.BlockSpec((N_GATE, D_MODEL), lambda i: (0, 0)),
            pl.BlockSpec((SUBLANE, LANE), lambda i: (0, 0)),
            pl.BlockSpec((N_GATE, LANE), lambda i: (0, 0)),
        ],
        out_specs=out2(GDN_W) + out2(MLP_W) + out2(LANE) + [
            pl.BlockSpec((N_GATE, tm), lambda i: (0, jnp.minimum(i, npt - 1))),
            pl.BlockSpec((N_GATE, tm), lambda i: (0, jnp.maximum(i - npt, 0))),
        ],
        out_shape=shp2(GDN_W) + shp2(MLP_W) + shp2(LANE) + [
            jax.ShapeDtypeStruct((N_GATE, n_p), F32), jax.ShapeDtypeStruct((N_GATE, n_s), F32)],
        compiler_params=pltpu.CompilerParams(dimension_semantics=("arbitrary",), vmem_limit_bytes=VMEM_LIMIT),
        name="inproj",
    )(xp, xs, g, w_all, ws_t, pcol, prow)


def _chunk_masks(L):
    ri = lax.broadcasted_iota(jnp.int32, (L, L), 0)
    ci = lax.broadcasted_iota(jnp.int32, (L, L), 1)
    return ri >= ci, ri > ci, ri <= ci


def _gdn_kernel(*refs, L, G, has_state):
    if has_state:
        (xin_ref, gate_ref, gatet_ref, cw_ref, ng_ref, cst_ref, s0_ref,
         mix_ref, cnew_ref, snew_ref, xc_ref, s_ref) = refs
    else:
        (xin_ref, gate_ref, gatet_ref, cw_ref, ng_ref,
         mix_ref, cnew_ref, snew_ref, xc_ref, s_ref) = refs
    c = pl.program_id(1)

    @pl.when(c == 0)
    def _():
        xc_ref[:, 0:SUBLANE, :] = jnp.zeros((G, SUBLANE, CONV_CH), F32)
        if has_state:
            xc_ref[:, SUBLANE - (CONV_W - 1):SUBLANE, :] = cst_ref[...]
            s_ref[...] = s0_ref[...]
        else:
            s_ref[...] = jnp.zeros_like(s_ref)

    @pl.when(c > 0)
    def _():
        xc_ref[:, 0:SUBLANE, :] = xc_ref[:, L:L + SUBLANE, :]

    tril, strict, triu = _chunk_masks(L)
    tril_f, triu_f = tril.astype(F32), triu.astype(F32)
    base = SUBLANE - (CONV_W - 1)

    chains = [(g, h) for g in range(G) for h in range(H_A)]
    s_old = [s_ref[g, h] for g, h in chains]
    for g in range(G):
        xc_ref[g, SUBLANE:SUBLANE + L, :] = xin_ref[g, :, :CONV_CH]

    q, k, v, beta, gc, gl, decay = [], [], [], [], [], [], []
    for g in range(G):
        conv = xc_ref[g, base:base + L, :] * cw_ref[0:1, :]
        for j in range(1, CONV_W):
            conv = conv + xc_ref[g, base + j:base + j + L, :] * cw_ref[j:j + 1, :]
        cnew_ref[g] = xc_ref[g, SUBLANE + L - (CONV_W - 1):SUBLANE + L, :]
        act = _silu(conv)
        gact = gate_ref[g]
        cum_c = _dot_hi(tril_f, gact)
        cum_r = _dot_hi(gatet_ref[g, 0], triu_f)
        for h in range(H_A):
            q.append(act[:, h * DK_A:(h + 1) * DK_A])
            k.append(act[:, H_A * DK_A + h * DK_A:H_A * DK_A + (h + 1) * DK_A])
            v.append(act[:, 2 * H_A * DK_A + h * DV_A:2 * H_A * DK_A + (h + 1) * DV_A])
            beta.append(gact[:, 4 + h:5 + h])
            gc.append(cum_c[:, h:h + 1])
            gl.append(cum_c[L - 1:L, h:h + 1])
            gr = cum_r[h:h + 1, :]
            decay.append(jnp.where(tril, jnp.exp(jnp.where(tril, cum_c[:, h:h + 1] - gr, 0.0)), 0.0))

    nc = range(len(chains))
    qss = [jnp.sum(q[i] * q[i], axis=-1, keepdims=True) for i in nc]
    kss = [jnp.sum(k[i] * k[i], axis=-1, keepdims=True) for i in nc]
    q = [q[i] * (lax.rsqrt(qss[i] + EPS) * (DK_A ** -0.5)) for i in nc]
    k = [k[i] * lax.rsqrt(kss[i] + EPS) for i in nc]
    kb = [k[i] * beta[i] for i in nc]
    egc = [jnp.exp(gc[i]) for i in nc]
    kk = [_dot(kb[i], k[i], _NT) for i in nc]
    qk = [_dot(q[i], k[i], _NT) for i in nc]
    eye = (lax.broadcasted_iota(jnp.int32, (L, L), 0) == lax.broadcasted_iota(jnp.int32, (L, L), 1)).astype(F32)
    pw = [-jnp.where(strict, kk[i] * decay[i], 0.0) for i in nc]
    t_inv = [eye + pw[i] for i in nc]
    span = 2
    while span < L:
        pw = [_dot(pw[i], pw[i]) for i in nc]
        t_inv = [t_inv[i] + _dot(t_inv[i], pw[i]) for i in nc]
        span *= 2
    sol = [_dot(t_inv[i], jnp.concatenate([v[i] * beta[i], kb[i] * egc[i]], axis=-1)) for i in nc]
    qs = [_dot(q[i] * egc[i], s_old[i]) for i in nc]
    ws = [_dot(sol[i][:, DV_A:], s_old[i]) for i in nc]
    v_new = [sol[i][:, :DV_A] - ws[i] for i in nc]
    o = [qs[i] + _dot(jnp.where(tril, qk[i] * decay[i], 0.0), v_new[i]) for i in nc]
    s_new = [s_old[i] * jnp.exp(gl[i]) + _dot(k[i] * jnp.exp(gl[i] - gc[i]), v_new[i], _TN) for i in nc]
    ms = [jnp.mean(o[i] * o[i], axis=-1, keepdims=True) for i in nc]
    on = [o[i] * lax.rsqrt(ms[i] + EPS) for i in nc]
    for i, (g, h) in enumerate(chains):
        z = xin_ref[g, :, CONV_CH + h * DV_A:CONV_CH + (h + 1) * DV_A]
        mix_ref[g, :, h * DV_A:(h + 1) * DV_A] = on[i] * ng_ref[...] * _silu(z)
    for i, (g, h) in enumerate(chains):
        s_ref[g, h] = s_new[i]
        snew_ref[g, h] = s_new[i]


def _gdn(gdn_in, gates, gates_t, cw, ng, *, L, G, state=None):
    n_seq, T, _ = gdn_in.shape
    n_c = T // L
    has_state = state is not None
    in_specs = [
        pl.BlockSpec((G, L, GDN_W), lambda b, c: (b, c, 0)),
        pl.BlockSpec((G, L, LANE), lambda b, c: (b, c, 0)),
        pl.BlockSpec((G, 1, N_GATE, L), lambda b, c: (b, c, 0, 0)),
        pl.BlockSpec((SUBLANE, CONV_CH), lambda b, c: (0, 0)),
        pl.BlockSpec((1, DV_A), lambda b, c: (0, 0)),
    ]
    args = [gdn_in, gates, gates_t, cw, ng]
    if has_state:
        conv_st, s0 = state
        in_specs += [
            pl.BlockSpec((G, CONV_W - 1, CONV_CH), lambda b, c: (b, 0, 0)),
            pl.BlockSpec((G, H_A, DK_A, DV_A), lambda b, c: (b, 0, 0, 0)),
        ]
        args += [conv_st, s0]
    return pl.pallas_call(
        functools.partial(_gdn_kernel, L=L, G=G, has_state=has_state),
        grid=(n_seq // G, n_c),
        in_specs=in_specs,
        out_specs=[
            pl.BlockSpec((G, L, H_A * DV_A), lambda b, c: (b, c, 0)),
            pl.BlockSpec((G, CONV_W - 1, CONV_CH), lambda b, c: (b, 0, 0)),
            pl.BlockSpec((G, H_A, DK_A, DV_A), lambda b, c: (b, 0, 0, 0)),
        ],
        out_shape=[
            jax.ShapeDtypeStruct((n_seq, T, H_A * DV_A), F32),
            jax.ShapeDtypeStruct((n_seq, CONV_W - 1, CONV_CH), F32),
            jax.ShapeDtypeStruct((n_seq, H_A, DK_A, DV_A), F32),
        ],
        scratch_shapes=[pltpu.VMEM((G, L + SUBLANE, CONV_CH), F32), pltpu.VMEM((G, H_A, DK_A, DV_A), F32)],
        compiler_params=pltpu.CompilerParams(dimension_semantics=("parallel", "arbitrary"),
                                             vmem_limit_bytes=VMEM_LIMIT),
        name=f"gdn_L{L}",
    )(*args)


def _mlstm_kernel(*refs, L, G, has_state):
    if has_state:
        (xin_ref, gate_ref, gatet_ref, ng_ref, c0_ref, n0_ref, m0_ref,
         mix_ref, cnew_ref, nnew_ref, mnew_ref, c_ref, n_ref, m_ref) = refs
    else:
        (xin_ref, gate_ref, gatet_ref, ng_ref,
         mix_ref, cnew_ref, nnew_ref, mnew_ref, c_ref, n_ref, m_ref) = refs
    c = pl.program_id(1)

    @pl.when(c == 0)
    def _():
        c_ref[...] = jnp.zeros_like(c_ref)
        n_ref[...] = jnp.zeros_like(n_ref)
        m_ref[...] = jnp.zeros_like(m_ref)
        if has_state:
            c_ref[:, :, 0:DK_B, :] = c0_ref[...]
            n_ref[:, 0:H_B, 0:DK_B] = n0_ref[...]
            m_ref[:, 0:1, 0:H_B] = m0_ref[...]

    tril, _, triu = _chunk_masks(L)
    tril_f, triu_f = tril.astype(F32), triu.astype(F32)

    chains = [(g, h) for g in range(G) for h in range(H_B)]
    nc = range(len(chains))
    c_old = [c_ref[g, h] for g, h in chains]
    n_old = [n_ref[g, h:h + 1, :] for g, h in chains]
    m_old = [m_ref[g, 0:1, h:h + 1] for g, h in chains]

    v0 = 2 * H_B * LANE
    q = [xin_ref[g, :, h * LANE:(h + 1) * LANE] * (DK_B ** -0.5) for g, h in chains]
    k = [xin_ref[g, :, (H_B + h) * LANE:(H_B + h + 1) * LANE] for g, h in chains]
    v = [xin_ref[g, :, v0 + h * DV_B:v0 + (h + 1) * DV_B] for g, h in chains]
    ig_c, b_c, b_last, d_log = [], [], [], []
    for g in range(G):
        gact = gate_ref[g]
        gact_t = gatet_ref[g, 0]
        cum_c = _dot_hi(tril_f, gact)
        cum_r = _dot_hi(gact_t, triu_f)
        for h in range(H_B):
            ig_c.append(gact[:, 8 + h:9 + h])
            b_c.append(cum_c[:, 12 + h:13 + h])
            b_last.append(cum_c[L - 1:L, 12 + h:13 + h])
            d_log.append(jnp.where(tril, cum_c[:, 12 + h:13 + h] - cum_r[12 + h:13 + h, :]
                                   + gact_t[8 + h:9 + h, :], NEG))
    qk = [_dot(q[i], k[i], _NT) for i in nc]
    qc = [_dot(q[i], c_old[i]) for i in nc]
    inter = [b_c[i] + m_old[i] for i in nc]
    m_t = [jnp.maximum(inter[i], jnp.max(d_log[i], axis=-1, keepdims=True)) for i in nc]
    s = [qk[i] * jnp.exp(d_log[i] - m_t[i]) for i in nc]
    e_inter = [jnp.exp(inter[i] - m_t[i]) for i in nc]
    sv = [_dot(s[i], v[i]) for i in nc]
    m_new = [m_t[i][L - 1:L, :] for i in nc]
    kw = [k[i] * jnp.exp(b_last[i] - b_c[i] + ig_c[i] - m_new[i]) for i in nc]
    f_tot = [jnp.exp(b_last[i] + m_old[i] - m_new[i]) for i in nc]
    c_new = [f_tot[i] * c_old[i] + _dot(kw[i], v[i], _TN) for i in nc]
    n_new = [f_tot[i] * n_old[i] + jnp.sum(kw[i], axis=0, keepdims=True) for i in nc]
    qn = [jnp.sum(q[i] * n_old[i], axis=-1, keepdims=True) for i in nc]
    ssum = [jnp.sum(s[i], axis=-1, keepdims=True) for i in nc]
    den = [jnp.maximum(jnp.abs(e_inter[i] * qn[i] + ssum[i]), jnp.exp(-m_t[i])) for i in nc]
    hh = [(e_inter[i] * qc[i] + sv[i]) / den[i] for i in nc]
    ms = [jnp.mean(hh[i] * hh[i], axis=-1, keepdims=True) for i in nc]
    hn = [hh[i] * lax.rsqrt(ms[i] + EPS) for i in nc]
    for i, (g, h) in enumerate(chains):
        og = xin_ref[g, :, v0 + H_B * DV_B + h * DV_B:v0 + H_B * DV_B + (h + 1) * DV_B]
        mix_ref[g, :, h * DV_B:(h + 1) * DV_B] = hn[i] * ng_ref[h:h + 1, :] * _sigmoid(og)
    for i, (g, h) in enumerate(chains):
        c_ref[g, h] = c_new[i]
        n_ref[g, h:h + 1, :] = n_new[i]
        m_ref[g, 0:1, h:h + 1] = m_new[i]
        cnew_ref[g, h] = c_new[i][0:DK_B, :]
        nnew_ref[g, h:h + 1, :] = n_new[i][:, 0:DK_B]
        mnew_ref[g, 0:1, h:h + 1] = m_new[i]


def _mlstm(ml_in, gates, gates_t, ng, *, L, G, state=None):
    n_seq, T, _ = ml_in.shape
    n_c = T // L
    has_state = state is not None
    in_specs = [
        pl.BlockSpec((G, L, MLP_W), lambda b, c: (b, c, 0)),
        pl.BlockSpec((G, L, LANE), lambda b, c: (b, c, 0)),
        pl.BlockSpec((G, 1, N_GATE, L), lambda b, c: (b, c, 0, 0)),
        pl.BlockSpec((H_B, DV_B), lambda b, c: (0, 0)),
    ]
    args = [ml_in, gates, gates_t, ng]
    if has_state:
        c0, n0, m0 = state
        in_specs += [
            pl.BlockSpec((G, H_B, DK_B, DV_B), lambda b, c: (b, 0, 0, 0)),
            pl.BlockSpec((G, H_B, DK_B), lambda b, c: (b, 0, 0)),
            pl.BlockSpec((G, 1, H_B), lambda b, c: (b, 0, 0)),
        ]
        args += [c0, n0, m0]
    return pl.pallas_call(
        functools.partial(_mlstm_kernel, L=L, G=G, has_state=has_state),
        grid=(n_seq // G, n_c),
        in_specs=in_specs,
        out_specs=[
            pl.BlockSpec((G, L, H_B * DV_B), lambda b, c: (b, c, 0)),
            pl.BlockSpec((G, H_B, DK_B, DV_B), lambda b, c: (b, 0, 0, 0)),
            pl.BlockSpec((G, H_B, DK_B), lambda b, c: (b, 0, 0)),
            pl.BlockSpec((G, 1, H_B), lambda b, c: (b, 0, 0)),
        ],
        out_shape=[
            jax.ShapeDtypeStruct((n_seq, T, H_B * DV_B), F32),
            jax.ShapeDtypeStruct((n_seq, H_B, DK_B, DV_B), F32),
            jax.ShapeDtypeStruct((n_seq, H_B, DK_B), F32),
            jax.ShapeDtypeStruct((n_seq, 1, H_B), F32),
        ],
        scratch_shapes=[pltpu.VMEM((G, H_B, LANE, DV_B), F32), pltpu.VMEM((G, SUBLANE, LANE), F32),
                        pltpu.VMEM((G, SUBLANE, LANE), F32)],
        compiler_params=pltpu.CompilerParams(dimension_semantics=("parallel", "arbitrary"),
                                             vmem_limit_bytes=VMEM_LIMIT),
        name=f"mlstm_L{L}",
    )(*args)


def _outproj_kernel(xp_ref, xs_ref, map_ref, mas_ref, mbp_ref, mbs_ref, wo_ref, g_ref, rw_ref, rb_ref,
                    x1_ref, xsort_ref, info_ref, cpad_ref, *, n_p_tiles):
    half = H_A * DV_A
    tm = xp_ref.shape[0]

    def body(seg):
        x_ref, ma_ref, mb_ref = (xp_ref, xs_ref)[seg], (map_ref, mas_ref)[seg], (mbp_ref, mbs_ref)[seg]
        x1 = (x_ref[...] + jnp.dot(ma_ref[...].astype(BF16), wo_ref[:half, :], preferred_element_type=F32)
              + jnp.dot(mb_ref[...].astype(BF16), wo_ref[half:, :], preferred_element_type=F32))
        x1_ref[...] = x1
        hn = _rms(x1, g_ref[...])
        hn_hi = hn.astype(BF16)
        hn_lo = (hn - hn_hi.astype(F32)).astype(BF16)
        logits = (jnp.dot(hn_hi, rw_ref[0], preferred_element_type=F32)
                  + jnp.dot(hn_hi, rw_ref[1], preferred_element_type=F32)
                  + jnp.dot(hn_lo, rw_ref[0], preferred_element_type=F32)) + rb_ref[...]

        vals = logits.T[:N_EXPERTS, :]
        e_iota = lax.broadcasted_iota(jnp.int32, (N_EXPERTS, tm), 0)
        sels, tops = [], []
        for _ in range(TOP_K):
            m = jnp.max(vals, axis=0, keepdims=True)
            first = jnp.min(jnp.where(vals == m, e_iota, N_EXPERTS), axis=0, keepdims=True)
            sel = e_iota == first
            vals = jnp.where(sel, -jnp.inf, vals)
            sels.append(sel)
            tops.append(m)
        ex = [jnp.exp(t - tops[0]) for t in tops]
        den = ex[0] + ex[1] + ex[2] + ex[3]
        gates = [e / den for e in ex]
        mask = sels[0].astype(F32) + sels[1].astype(F32) + sels[2].astype(F32) + sels[3].astype(F32)
        ri = lax.broadcasted_iota(jnp.int32, (tm, tm), 0)
        ci = lax.broadcasted_iota(jnp.int32, (tm, tm), 1)
        rank = _dot(mask, (ri < ci).astype(F32))
        cnt = jnp.sum(mask, axis=1, keepdims=True)
        cpad = jnp.ceil(cnt * (1.0 / SEG_ALIGN)) * SEG_ALIGN
        cpad_b = jnp.broadcast_to(cpad, (N_EXPERTS, tm))
        er = lax.broadcasted_iota(jnp.int32, (N_EXPERTS, N_EXPERTS), 0)
        ec = lax.broadcasted_iota(jnp.int32, (N_EXPERTS, N_EXPERTS), 1)
        seg_off = _dot((er > ec).astype(F32), cpad_b)
        pos = seg_off + rank
        q = [jnp.sum(jnp.where(s, pos, 0.0), axis=0, keepdims=True) for s in sels]

        j_iota = lax.broadcasted_iota(jnp.int32, (MOE_CAP, tm), 0).astype(F32)
        perm = jnp.zeros((MOE_CAP, tm), F32)
        for kk in range(TOP_K):
            perm = jnp.where(j_iota == q[kk], 1.0, perm)
        xsorted = _dot(perm, hn)
        xsort_ref[...] = xsorted.astype(BF16)

        r_iota = lax.broadcasted_iota(jnp.int32, (LANE, tm), 0)
        info = jnp.zeros((LANE, tm), F32)
        for kk in range(TOP_K):
            info = jnp.where(r_iota == kk, q[kk], info)
            info = jnp.where(r_iota == TOP_K + kk, gates[kk], info)
        info_ref[...] = info.T
        cpad_ref[0] = cpad_b[:, :LANE]

    _for_segment(n_p_tiles, body)


def _outproj(xp, xs, ma_p, ma_s, mb_p, mb_s, w_out, g, rw, rb, tm):
    n_p, n_s = xp.shape[0], xs.shape[0]
    n = n_p + n_s
    nt = n // tm
    npt = n_p // tm
    half = H_A * DV_A
    return pl.pallas_call(
        functools.partial(_outproj_kernel, n_p_tiles=npt),
        grid=(nt,),
        in_specs=_two_segment_specs(tm, D_MODEL, npt) + _two_segment_specs(tm, half, npt)
        + _two_segment_specs(tm, half, npt) + [
            pl.BlockSpec((D_MODEL, D_MODEL), lambda i: (0, 0)),
            pl.BlockSpec((1, D_MODEL), lambda i: (0, 0)),
            pl.BlockSpec((2, D_MODEL, LANE), lambda i: (0, 0, 0)),
            pl.BlockSpec((1, LANE), lambda i: (0, 0)),
        ],
        out_specs=[
            pl.BlockSpec((tm, D_MODEL), lambda i: (i, 0)),
            pl.BlockSpec((MOE_CAP, D_MODEL), lambda i: (i, 0)),
            pl.BlockSpec((tm, LANE), lambda i: (i, 0)),
            pl.BlockSpec((1, N_EXPERTS, LANE), lambda i: (i, 0, 0)),
        ],
        out_shape=[
            jax.ShapeDtypeStruct((n, D_MODEL), F32),
            jax.ShapeDtypeStruct((nt * MOE_CAP, D_MODEL), BF16),
            jax.ShapeDtypeStruct((n, LANE), F32),
            jax.ShapeDtypeStruct((nt, N_EXPERTS, LANE), F32),
        ],
        compiler_params=pltpu.CompilerParams(dimension_semantics=("arbitrary",), vmem_limit_bytes=VMEM_LIMIT),
        name="outproj",
    )(xp, xs, ma_p, ma_s, mb_p, mb_s, w_out, g, rw, rb)


def _expert_kernel(be_ref, bj_ref, tf_ref, tl_ref, cov_ref, nu_ref, vt_ref, ct_ref, lt_ref, nx_ref, ws_ref,
                   xs_hbm, wgu_hbm, bgu_ref, wd_hbm, bd_ref, ys_hbm,
                   xbuf, ybuf, gsem, ssem, wgu_st, wd_st, wsem, wgu_bf, wd_bf, *, nt):
    b = pl.program_id(0)
    n_used = nu_ref[0]
    slot = b % 2

    def for_segments(bb, fn):
        e = be_ref[bb]
        base = bj_ref[bb] * MOE_BLK
        t0, t1 = tf_ref[bb], tl_ref[bb]

        def body(i, carry):
            for u in range(SEG_UNROLL):
                t = t0 + i * SEG_UNROLL + u
                k = e * nt + jnp.minimum(t, nt - 1)
                lo = jnp.maximum(vt_ref[k], base)
                hi = jnp.minimum(ct_ref[k], base + MOE_BLK)
                ln = jnp.where(t <= t1, jnp.maximum(hi - lo, 0), 0)
                src = lt_ref[k] + lo
                dst = lo - base

                @pl.when(ln > 0)
                def _():
                    fn(pl.multiple_of(src, SEG_ALIGN), pl.multiple_of(dst, SEG_ALIGN),
                       pl.multiple_of(ln, SEG_ALIGN))
            return carry

        lax.fori_loop(0, (t1 - t0 + SEG_UNROLL) // SEG_UNROLL, body, 0)

    def weight_copies(e):
        return (pltpu.make_async_copy(wgu_hbm.at[e], wgu_st, wsem.at[0]),
                pltpu.make_async_copy(wd_hbm.at[e], wd_st, wsem.at[1]))

    def cast_weights(p):
        wgu_bf[p] = wgu_st[...].astype(BF16)
        wd_bf[p] = wd_st[...].astype(BF16)

    def gather_copy(s, src, dst, size):
        return pltpu.make_async_copy(xs_hbm.at[pl.ds(src, size)], xbuf.at[s, pl.ds(dst, size)], gsem.at[s])

    def scatter_copy(s, src, dst, size):
        return pltpu.make_async_copy(ybuf.at[s, pl.ds(dst, size)], ys_hbm.at[pl.ds(src, size)], ssem.at[s])

    def gather(bb, s):
        for_segments(bb, lambda src, dst, size: gather_copy(s, src, dst, size).start())

    def scatter(bb, s):
        for_segments(bb, lambda src, dst, size: scatter_copy(s, src, dst, size).start())

    def wait_rows(count, copy, s):
        @pl.when(count > 0)
        def _():
            copy(s, 0, 0, pl.multiple_of(count, SEG_ALIGN)).wait()

    @pl.when(b == 0)
    def _():
        xbuf[...] = jnp.zeros_like(xbuf)
        gather(0, 0)

    @pl.when(b + 1 < n_used)
    def _():
        gather(b + 1, 1 - slot)

    @pl.when(b < n_used)
    def _():
        e = be_ref[b]
        first = jnp.logical_or(b == 0, be_ref[jnp.maximum(b - 1, 0)] != e)
        last = jnp.logical_or(b == n_used - 1, be_ref[jnp.minimum(b + 1, n_used - 1)] != e)
        has_next = nx_ref[b] < N_EXPERTS
        p = ws_ref[b]

        @pl.when(b == 0)
        def _():
            for cp in weight_copies(e):
                cp.start()
            for cp in weight_copies(e):
                cp.wait()
            cast_weights(p)

        @pl.when(jnp.logical_and(first, has_next))
        def _():
            for cp in weight_copies(nx_ref[b]):
                cp.start()

        wait_rows(cov_ref[b], gather_copy, slot)
        hgu = jnp.dot(xbuf[slot], wgu_bf[p], preferred_element_type=F32) + bgu_ref[0]
        gate = jnp.minimum(hgu[:, :D_FF], SWIGLU_LIMIT)
        up = jnp.clip(hgu[:, D_FF:], -SWIGLU_LIMIT, SWIGLU_LIMIT)
        act = (up + 1.0) * gate * _sigmoid(SWIGLU_ALPHA * gate)
        y = jnp.dot(act.astype(BF16), wd_bf[p], preferred_element_type=F32) + bd_ref[0]

        @pl.when(jnp.logical_and(last, has_next))
        def _():
            for cp in weight_copies(nx_ref[b]):
                cp.wait()
            cast_weights(1 - p)

        @pl.when(b >= 2)
        def _():
            wait_rows(cov_ref[jnp.maximum(b - 2, 0)], scatter_copy, slot)

        ybuf[slot] = y.astype(BF16)
        scatter(b, slot)

        @pl.when(b == n_used - 1)
        def _():
            wait_rows(cov_ref[b], scatter_copy, slot)

            @pl.when(b >= 1)
            def _():
                wait_rows(cov_ref[jnp.maximum(b - 1, 0)], scatter_copy, 1 - slot)


def _experts(tables, xs, w_gu, b_gu, w_down, b_down, nt):
    nb = tables[0].shape[0]

    def bias_blk(b, *t):
        return (t[0][jnp.minimum(b, t[5][0] - 1)], 0, 0)

    grid_spec = pltpu.PrefetchScalarGridSpec(
        num_scalar_prefetch=len(tables),
        grid=(nb,),
        in_specs=[
            pl.BlockSpec(memory_space=pl.ANY),
            pl.BlockSpec(memory_space=pl.ANY),
            pl.BlockSpec((1, 1, 2 * D_FF), bias_blk),
            pl.BlockSpec(memory_space=pl.ANY),
            pl.BlockSpec((1, 1, D_MODEL), bias_blk),
        ],
        out_specs=pl.BlockSpec(memory_space=pl.ANY),
        scratch_shapes=[
            pltpu.VMEM((2, MOE_BLK, D_MODEL), BF16),
            pltpu.VMEM((2, MOE_BLK, D_MODEL), BF16),
            pltpu.SemaphoreType.DMA((2,)),
            pltpu.SemaphoreType.DMA((2,)),
            pltpu.VMEM((D_MODEL, 2 * D_FF), F32),
            pltpu.VMEM((D_FF, D_MODEL), F32),
            pltpu.SemaphoreType.DMA((2,)),
            pltpu.VMEM((2, D_MODEL, 2 * D_FF), BF16),
            pltpu.VMEM((2, D_FF, D_MODEL), BF16),
        ],
    )
    return pl.pallas_call(
        functools.partial(_expert_kernel, nt=nt),
        grid_spec=grid_spec,
        out_shape=jax.ShapeDtypeStruct(xs.shape, xs.dtype),
        input_output_aliases={len(tables): 0},
        compiler_params=pltpu.CompilerParams(dimension_semantics=("arbitrary",), vmem_limit_bytes=VMEM_LIMIT),
        name="experts",
    )(*tables, xs, w_gu, b_gu, w_down, b_down)


def _combine_kernel(ys_ref, info_ref, x1_ref, pp_ref, ps_ref, gple_ref, wg_ref, wp_ref, gfin_ref,
                    outp_ref, outs_ref, *, n_p_tiles):
    tm = x1_ref.shape[0]

    def body(seg):
        p_ref, out_ref = (pp_ref, ps_ref)[seg], (outp_ref, outs_ref)[seg]
        info = info_ref[...]
        j_iota = lax.broadcasted_iota(jnp.int32, (tm, MOE_CAP), 1).astype(F32)
        gmat = jnp.zeros((tm, MOE_CAP), F32)
        for kk in range(TOP_K):
            gmat = jnp.where(j_iota == info[:, kk:kk + 1], info[:, TOP_K + kk:TOP_K + kk + 1], gmat)
        x2 = x1_ref[...] + jnp.dot(gmat.astype(BF16), ys_ref[...], preferred_element_type=F32)
        hn = _rms(x2, gple_ref[...]).astype(BF16)
        gate = _sigmoid(jnp.dot(hn, wg_ref[...], preferred_element_type=F32))
        pe = jnp.dot(p_ref[...].astype(BF16), wp_ref[...], preferred_element_type=F32)
        x3 = x2 + gate * pe
        out_ref[...] = _rms(x3, gfin_ref[...])

    _for_segment(n_p_tiles, body)


def _combine(ys, info, x1, pp, ps, g_ple, w_gate, w_p, g_fin, tm):
    n_p, n_s = pp.shape[0], ps.shape[0]
    n = n_p + n_s
    nt = n // tm
    npt = n_p // tm
    return pl.pallas_call(
        functools.partial(_combine_kernel, n_p_tiles=npt),
        grid=(nt,),
        in_specs=[
            pl.BlockSpec((MOE_CAP, D_MODEL), lambda i: (i, 0)),
            pl.BlockSpec((tm, LANE), lambda i: (i, 0)),
            pl.BlockSpec((tm, D_MODEL), lambda i: (i, 0)),
        ] + _two_segment_specs(tm, PLE_DIM, npt) + [
            pl.BlockSpec((1, D_MODEL), lambda i: (0, 0)),
            pl.BlockSpec((D_MODEL, D_MODEL), lambda i: (0, 0)),
            pl.BlockSpec((PLE_DIM, D_MODEL), lambda i: (0, 0)),
            pl.BlockSpec((1, D_MODEL), lambda i: (0, 0)),
        ],
        out_specs=_two_segment_specs(tm, D_MODEL, npt),
        out_shape=[jax.ShapeDtypeStruct((n_p, D_MODEL), F32), jax.ShapeDtypeStruct((n_s, D_MODEL), F32)],
        compiler_params=pltpu.CompilerParams(dimension_semantics=("arbitrary",), vmem_limit_bytes=VMEM_LIMIT),
        name="combine",
    )(ys, info, x1, pp, ps, g_ple, w_gate, w_p, g_fin)


def _block_tables(seg_len, nb):
    nt = seg_len.shape[0]
    seg_off = jnp.cumsum(seg_len, axis=1) - seg_len
    seg_end = jnp.cumsum(seg_len, axis=0).T
    seg_start = seg_end - seg_len.T
    n_rows = seg_end[:, -1]
    n_blk = (n_rows + MOE_BLK - 1) // MOE_BLK
    blk_end = jnp.cumsum(n_blk)
    b = jnp.arange(nb, dtype=jnp.int32)
    block_e = jnp.minimum(jnp.sum((blk_end[None, :] <= b[:, None]).astype(jnp.int32), axis=1), N_EXPERTS - 1)
    block_j = b - (blk_end - n_blk)[block_e]
    base = block_j * MOE_BLK
    t_first = jnp.sum((seg_end[block_e] <= base[:, None]).astype(jnp.int32), axis=1)
    t_last = jnp.sum((seg_start[block_e] < (base + MOE_BLK)[:, None]).astype(jnp.int32), axis=1) - 1
    cover = jnp.clip(n_rows[block_e] - base, 0, MOE_BLK)
    seg_shift = (jnp.arange(nt, dtype=jnp.int32)[:, None] * MOE_CAP + seg_off).T - seg_start
    idx = jnp.where(n_blk > 0, jnp.arange(N_EXPERTS, dtype=jnp.int32), N_EXPERTS)
    nxt = jnp.concatenate([lax.cummin(idx, axis=0, reverse=True)[1:], jnp.full((1,), N_EXPERTS, jnp.int32)])
    parity = (jnp.cumsum((n_blk > 0).astype(jnp.int32)) - 1) % 2
    tables = (block_e, block_j, t_first, t_last, cover, blk_end[-1:], seg_start.reshape(-1),
              seg_end.reshape(-1), seg_shift.reshape(-1), nxt[block_e], parity[block_e])
    return tuple(t.astype(jnp.int32) for t in tables)


def _rearranged_in_weights(w_in):
    o = np.cumsum([0, CONV_CH, H_A * DV_A, H_A, H_A, H_B * DK_B, H_B * DK_B, H_B * DV_B, H_B * DV_B, H_B, H_B])
    conv_in, z_a, a_a, b_a, q_b, k_b, v_b, o_b, i_b, f_b = (w_in[:, int(o[j]):int(o[j + 1])] for j in range(10))
    zpad = jnp.zeros((D_MODEL, LANE - DK_B), w_in.dtype)

    def pad_heads(w):
        return jnp.concatenate([jnp.concatenate([w[:, h * DK_B:(h + 1) * DK_B], zpad], axis=1) for h in range(H_B)],
                               axis=1)

    small = jnp.concatenate([a_a, b_a, i_b, f_b], axis=1)
    w_all = jnp.concatenate([conv_in, z_a, pad_heads(q_b), pad_heads(k_b), v_b, o_b,
                             small, jnp.zeros((D_MODEL, LANE - N_GATE), w_in.dtype)], axis=1)
    return w_all.astype(BF16), small.T.astype(BF16)


def _gate_params(a_log, dt_bias, i_bias, f_bias):
    z4 = jnp.zeros((4,), F32)
    alog = jnp.concatenate([a_log.astype(F32), z4, z4, z4])
    bias = jnp.concatenate([dt_bias.astype(F32), z4, i_bias.astype(F32), f_bias.astype(F32)])
    pad = jnp.zeros((LANE - N_GATE,), F32)
    pcol = jnp.zeros((SUBLANE, LANE), F32).at[0].set(jnp.concatenate([alog, pad])).at[1].set(
        jnp.concatenate([bias, pad]))
    prow = jnp.zeros((N_GATE, LANE), F32).at[:, 0].set(alog).at[:, 1].set(bias)
    return pcol, prow


def kernel(x_prompt, x_sample, p_prompt, p_sample, state_conv, state_gdn, state_mlstm_c, state_mlstm_n, state_mlstm_m, norm_attn_g, w_in, conv_w, gdn_a_log, gdn_dt_bias, gdn_norm_g, mlstm_i_bias, mlstm_f_bias, mlstm_norm_g, w_out, norm_moe_g, router_w, router_b, expert_w_gu, expert_b_gu, expert_w_down, expert_b_down, norm_ple_g, ple_gate_w, ple_w, final_norm_g):
    bp, tp, _ = x_prompt.shape
    bs, ts, _ = x_sample.shape
    n_p, n_s = bp * tp, bs * ts
    n = n_p + n_s
    lp, ls = min(tp, CHUNK), min(ts, CHUNK)
    tm = 512
    gp_a = 4 if bp % 4 == 0 else 1
    gp_b = 2 if bp % 2 == 0 else 1
    gs = 8 if bs % 8 == 0 else 1
    assert tp % lp == 0 and ts % ls == 0 and n_p % tm == 0 and n_s % tm == 0 and ls % SUBLANE == 0

    xp = x_prompt.reshape(n_p, D_MODEL)
    xs = x_sample.reshape(n_s, D_MODEL)

    w_all, ws_t = _rearranged_in_weights(w_in[0])
    pcol, prow = _gate_params(gdn_a_log[0], gdn_dt_bias[0], mlstm_i_bias[0], mlstm_f_bias[0])
    gdn_p, gdn_s, ml_p, ml_s, gate_p, gate_s, gatet_p, gatet_s = _inproj(
        xp, xs, norm_attn_g[0].reshape(1, D_MODEL), w_all, ws_t, pcol, prow, tm)
    gt_p = gatet_p.reshape(N_GATE, bp, tp // lp, lp).transpose(1, 2, 0, 3)
    gt_s = gatet_s.reshape(N_GATE, bs, ts // ls, ls).transpose(1, 2, 0, 3)

    cw = jnp.zeros((SUBLANE, CONV_CH), F32).at[:CONV_W].set(conv_w[0].astype(F32))
    ng_a = gdn_norm_g[0].reshape(1, DV_A).astype(F32)
    ng_b = mlstm_norm_g[0].reshape(H_B, DV_B).astype(F32)
    ma_p, conv_p, gdn_st_p = _gdn(gdn_p.reshape(bp, tp, GDN_W), gate_p.reshape(bp, tp, LANE), gt_p, cw, ng_a,
                                  L=lp, G=gp_a)
    ma_s, conv_s, gdn_st_s = _gdn(gdn_s.reshape(bs, ts, GDN_W), gate_s.reshape(bs, ts, LANE), gt_s, cw, ng_a,
                                  L=ls, G=gs, state=(state_conv[0], state_gdn[0]))
    mb_p, c_p, nn_p, m_p = _mlstm(ml_p.reshape(bp, tp, MLP_W), gate_p.reshape(bp, tp, LANE), gt_p, ng_b,
                                  L=lp, G=gp_b)
    mb_s, c_s, nn_s, m_s = _mlstm(ml_s.reshape(bs, ts, MLP_W), gate_s.reshape(bs, ts, LANE), gt_s, ng_b,
                                  L=ls, G=gs,
                                  state=(state_mlstm_c[0], state_mlstm_n[0], state_mlstm_m[0].reshape(bs, 1, H_B)))
    half = H_A * DV_A

    rw = jnp.zeros((D_MODEL, LANE), F32).at[:, :N_EXPERTS].set(router_w[0])
    rw_hi = rw.astype(BF16)
    rw = jnp.stack([rw_hi, (rw - rw_hi.astype(F32)).astype(BF16)])
    rb = jnp.full((1, LANE), NEG, F32).at[0, :N_EXPERTS].set(router_b[0])
    x1, x_sorted, info, seg_len = _outproj(xp, xs, ma_p.reshape(n_p, half), ma_s.reshape(n_s, half),
                                           mb_p.reshape(n_p, half), mb_s.reshape(n_s, half),
                                           w_out[0].astype(BF16), norm_moe_g[0].reshape(1, D_MODEL), rw, rb, MOE_TM)

    nt = n // MOE_TM
    nb = -(-(n * TOP_K + nt * N_EXPERTS * (SEG_ALIGN - 1)) // MOE_BLK) + N_EXPERTS
    tables = _block_tables(seg_len[:, :, 0].astype(jnp.int32), nb)
    y_sorted = _experts(tables, x_sorted, expert_w_gu[0], expert_b_gu[0].reshape(N_EXPERTS, 1, 2 * D_FF),
                        expert_w_down[0], expert_b_down[0].reshape(N_EXPERTS, 1, D_MODEL), nt)
    y_p, y_s = _combine(y_sorted, info, x1, p_prompt[0].reshape(n_p, PLE_DIM),
                        p_sample[0].reshape(n_s, PLE_DIM), norm_ple_g[0].reshape(1, D_MODEL),
                        ple_gate_w[0].astype(BF16), ple_w[0].astype(BF16), final_norm_g.reshape(1, D_MODEL), MOE_TM)

    return (y_p.reshape(bp, tp, D_MODEL), y_s.reshape(bs, ts, D_MODEL),
            conv_p[None], gdn_st_p[None], c_p[None], nn_p[None], m_p.reshape(1, bp, H_B),
            conv_s[None], gdn_st_s[None], c_s[None], nn_s[None], m_s.reshape(1, bs, H_B))
```

```python
import functools

import numpy as np
import jax
import jax.numpy as jnp
from jax import lax
from jax.experimental import pallas as pl
from jax.experimental.pallas import tpu as pltpu

F32 = jnp.float32
BF16 = jnp.bfloat16

D_MODEL = 1024
H_A, DK_A, DV_A = 4, 128, 128
H_B, DK_B, DV_B = 4, 64, 128
CONV_W = 4
CONV_CH = H_A * (2 * DK_A + DV_A)
N_EXPERTS = 32
TOP_K = 4
D_FF = 1024
SWIGLU_LIMIT = 7.0
SWIGLU_ALPHA = 1.702
PLE_DIM = 256
EPS = 1e-6
NEG = -1e30
CHUNK = 64

LANE = 128
SUBLANE = 8
GDN_W = CONV_CH + H_A * DV_A
MLP_W = 2 * H_B * LANE + 2 * H_B * DV_B
N_GATE = 16
W_ALL = GDN_W + MLP_W + LANE

VMEM_LIMIT = 48 * 1024 * 1024

MOE_TM = 256
MOE_BLK = 256
SEG_ALIGN = SUBLANE
MOE_CAP = -(-(MOE_TM * TOP_K + N_EXPERTS * (SEG_ALIGN - 1)) // LANE) * LANE

HI = lax.Precision.HIGHEST

_NN = (((1,), (0,)), ((), ()))
_NT = (((1,), (1,)), ((), ()))
_TN = (((0,), (0,)), ((), ()))


def _dot(a, b, dims=_NN):
    return lax.dot_general(a.astype(BF16), b.astype(BF16), dims, preferred_element_type=F32)


def _dot_hi(a, b, dims=_NN):
    return lax.dot_general(a, b, dims, precision=HI, preferred_element_type=F32)


def _rms(x, g):
    return x * lax.rsqrt(jnp.mean(x * x, axis=-1, keepdims=True) + EPS) * g


def _softplus(t):
    return jnp.maximum(t, 0.0) + jnp.log1p(jnp.exp(-jnp.abs(t)))


def _sigmoid(t):
    return 1.0 / (1.0 + jnp.exp(-t))


def _silu(t):
    return t * _sigmoid(t)


def _activate_gates(raw, idx, alog, bias):
    t = raw + bias
    g = -jnp.exp(alog) * _softplus(t)
    beta = _sigmoid(t)
    lf = -_softplus(-t)
    return jnp.where(idx < 4, g, jnp.where(idx < 8, beta, jnp.where(idx < 12, t, lf)))


def _two_segment_specs(tm, width, n_p_tiles):
    return [pl.BlockSpec((tm, width), lambda i: (jnp.minimum(i, n_p_tiles - 1), 0)),
            pl.BlockSpec((tm, width), lambda i: (jnp.maximum(i - n_p_tiles, 0), 0))]


def _for_segment(n_p_tiles, body):
    i = pl.program_id(0)

    @pl.when(i < n_p_tiles)
    def _():
        body(0)

    @pl.when(i >= n_p_tiles)
    def _():
        body(1)


def _inproj_kernel(xp_ref, xs_ref, g_ref, w_ref, wst_ref, pc_ref, pr_ref,
                   gdnp_ref, gdns_ref, mlp_ref, mls_ref, gatep_ref, gates_ref, gatetp_ref, gatets_ref,
                   *, n_p_tiles):
    tm = xp_ref.shape[0]

    def body(seg):
        x_ref = (xp_ref, xs_ref)[seg]
        gdn_ref, ml_ref = (gdnp_ref, gdns_ref)[seg], (mlp_ref, mls_ref)[seg]
        gate_ref, gatet_ref = (gatep_ref, gates_ref)[seg], (gatetp_ref, gatets_ref)[seg]
        hn = _rms(x_ref[...], g_ref[...]).astype(BF16)
        gdn_ref[...] = jnp.dot(hn, w_ref[:, :GDN_W], preferred_element_type=F32)
        ml_ref[...] = jnp.dot(hn, w_ref[:, GDN_W:GDN_W + MLP_W], preferred_element_type=F32)
        raw = jnp.dot(hn, w_ref[:, GDN_W + MLP_W:], preferred_element_type=F32)
        lane = lax.broadcasted_iota(jnp.int32, (tm, LANE), 1)
        gate_ref[...] = _activate_gates(raw, lane, pc_ref[0:1, :], pc_ref[1:2, :])
        raw_t = lax.dot_general(wst_ref[...], hn, _NT, preferred_element_type=F32)
        row = lax.broadcasted_iota(jnp.int32, (N_GATE, tm), 0)
        gatet_ref[...] = _activate_gates(raw_t, row, pr_ref[:, 0:1], pr_ref[:, 1:2])

    _for_segment(n_p_tiles, body)


def _inproj(xp, xs, g, w_all, ws_t, pcol, prow, tm):
    n_p, n_s = xp.shape[0], xs.shape[0]
    npt = n_p // tm

    def out2(width):
        return _two_segment_specs(tm, width, npt)

    def shp2(width):
        return [jax.ShapeDtypeStruct((n_p, width), F32), jax.ShapeDtypeStruct((n_s, width), F32)]

    return pl.pallas_call(
        functools.partial(_inproj_kernel, n_p_tiles=npt),
        grid=((n_p + n_s) // tm,),
        in_specs=_two_segment_specs(tm, D_MODEL, npt) + [
            pl.BlockSpec((1, D_MODEL), lambda i: (0, 0)),
            pl.BlockSpec((D_MODEL, W_ALL), lambda i: (0, 0)),
            pl.BlockSpec((N_GATE, D_MODEL), lambda i: (0, 0)),
            pl.BlockSpec((SUBLANE, LANE), lambda i: (0, 0)),
            pl.BlockSpec((N_GATE, LANE), lambda i: (0, 0)),
        ],
        out_specs=out2(GDN_W) + out2(MLP_W) + out2(LANE) + [
            pl.BlockSpec((N_GATE, tm), lambda i: (0, jnp.minimum(i, npt - 1))),
            pl.BlockSpec((N_GATE, tm), lambda i: (0, jnp.maximum(i - npt, 0))),
        ],
        out_shape=shp2(GDN_W) + shp2(MLP_W) + shp2(LANE) + [
            jax.ShapeDtypeStruct((N_GATE, n_p), F32), jax.ShapeDtypeStruct((N_GATE, n_s), F32)],
        compiler_params=pltpu.CompilerParams(dimension_semantics=("arbitrary",), vmem_limit_bytes=VMEM_LIMIT),
        name="inproj",
    )(xp, xs, g, w_all, ws_t, pcol, prow)


def _chunk_masks(L):
    ri = lax.broadcasted_iota(jnp.int32, (L, L), 0)
    ci = lax.broadcasted_iota(jnp.int32, (L, L), 1)
    return ri >= ci, ri > ci, ri <= ci


def _gdn_kernel(*refs, L, G, has_state):
    if has_state:
        (xin_ref, gate_ref, gatet_ref, cw_ref, ng_ref, cst_ref, s0_ref,
         mix_ref, cnew_ref, snew_ref, xc_ref, s_ref) = refs
    else:
        (xin_ref, gate_ref, gatet_ref, cw_ref, ng_ref,
         mix_ref, cnew_ref, snew_ref, xc_ref, s_ref) = refs
    c = pl.program_id(1)

    @pl.when(c == 0)
    def _():
        xc_ref[:, 0:SUBLANE, :] = jnp.zeros((G, SUBLANE, CONV_CH), F32)
        if has_state:
            xc_ref[:, SUBLANE - (CONV_W - 1):SUBLANE, :] = cst_ref[...]
            s_ref[...] = s0_ref[...]
        else:
            s_ref[...] = jnp.zeros_like(s_ref)

    @pl.when(c > 0)
    def _():
        xc_ref[:, 0:SUBLANE, :] = xc_ref[:, L:L + SUBLANE, :]

    tril, strict, triu = _chunk_masks(L)
    tril_f, triu_f = tril.astype(F32), triu.astype(F32)
    base = SUBLANE - (CONV_W - 1)

    chains = [(g, h) for g in range(G) for h in range(H_A)]
    s_old = [s_ref[g, h] for g, h in chains]
    for g in range(G):
        xc_ref[g, SUBLANE:SUBLANE + L, :] = xin_ref[g, :, :CONV_CH]

    q, k, v, beta, gc, gl, decay = [], [], [], [], [], [], []
    for g in range(G):
        conv = xc_ref[g, base:base + L, :] * cw_ref[0:1, :]
        for j in range(1, CONV_W):
            conv = conv + xc_ref[g, base + j:base + j + L, :] * cw_ref[j:j + 1, :]
        cnew_ref[g] = xc_ref[g, SUBLANE + L - (CONV_W - 1):SUBLANE + L, :]
        act = _silu(conv)
        gact = gate_ref[g]
        cum_c = _dot_hi(tril_f, gact)
        cum_r = _dot_hi(gatet_ref[g, 0], triu_f)
        for h in range(H_A):
            q.append(act[:, h * DK_A:(h + 1) * DK_A])
            k.append(act[:, H_A * DK_A + h * DK_A:H_A * DK_A + (h + 1) * DK_A])
            v.append(act[:, 2 * H_A * DK_A + h * DV_A:2 * H_A * DK_A + (h + 1) * DV_A])
            beta.append(gact[:, 4 + h:5 + h])
            gc.append(cum_c[:, h:h + 1])
            gl.append(cum_c[L - 1:L, h:h + 1])
            gr = cum_r[h:h + 1, :]
            decay.append(jnp.where(tril, jnp.exp(jnp.where(tril, cum_c[:, h:h + 1] - gr, 0.0)), 0.0))

    nc = range(len(chains))
    qss = [jnp.sum(q[i] * q[i], axis=-1, keepdims=True) for i in nc]
    kss = [jnp.sum(k[i] * k[i], axis=-1, keepdims=True) for i in nc]
    q = [q[i] * (lax.rsqrt(qss[i] + EPS) * (DK_A ** -0.5)) for i in nc]
    k = [k[i] * lax.rsqrt(kss[i] + EPS) for i in nc]
    kb = [k[i] * beta[i] for i in nc]
    egc = [jnp.exp(gc[i]) for i in nc]
    kk = [_dot(kb[i], k[i], _NT) for i in nc]
    qk = [_dot(q[i], k[i], _NT) for i in nc]
    eye = (lax.broadcasted_iota(jnp.int32, (L, L), 0) == lax.broadcasted_iota(jnp.int32, (L, L), 1)).astype(F32)
    pw = [-jnp.where(strict, kk[i] * decay[i], 0.0) for i in nc]
    t_inv = [eye + pw[i] for i in nc]
    span = 2
    while span < L:
        pw = [_dot(pw[i], pw[i]) for i in nc]
        t_inv = [t_inv[i] + _dot(t_inv[i], pw[i]) for i in nc]
        span *= 2
    sol = [_dot(t_inv[i], jnp.concatenate([v[i] * beta[i], kb[i] * egc[i]], axis=-1)) for i in nc]
    qs = [_dot(q[i] * egc[i], s_old[i]) for i in nc]
    ws = [_dot(sol[i][:, DV_A:], s_old[i]) for i in nc]
    v_new = [sol[i][:, :DV_A] - ws[i] for i in nc]
    o = [qs[i] + _dot(jnp.where(tril, qk[i] * decay[i], 0.0), v_new[i]) for i in nc]
    s_new = [s_old[i] * jnp.exp(gl[i]) + _dot(k[i] * jnp.exp(gl[i] - gc[i]), v_new[i], _TN) for i in nc]
    ms = [jnp.mean(o[i] * o[i], axis=-1, keepdims=True) for i in nc]
    on = [o[i] * lax.rsqrt(ms[i] + EPS) for i in nc]
    for i, (g, h) in enumerate(chains):
        z = xin_ref[g, :, CONV_CH + h * DV_A:CONV_CH + (h + 1) * DV_A]
        mix_ref[g, :, h * DV_A:(h + 1) * DV_A] = on[i] * ng_ref[...] * _silu(z)
    for i, (g, h) in enumerate(chains):
        s_ref[g, h] = s_new[i]
        snew_ref[g, h] = s_new[i]


def _gdn(gdn_in, gates, gates_t, cw, ng, *, L, G, state=None):
    n_seq, T, _ = gdn_in.shape
    n_c = T // L
    has_state = state is not None
    in_specs = [
        pl.BlockSpec((G, L, GDN_W), lambda b, c: (b, c, 0)),
        pl.BlockSpec((G, L, LANE), lambda b, c: (b, c, 0)),
        pl.BlockSpec((G, 1, N_GATE, L), lambda b, c: (b, c, 0, 0)),
        pl.BlockSpec((SUBLANE, CONV_CH), lambda b, c: (0, 0)),
        pl.BlockSpec((1, DV_A), lambda b, c: (0, 0)),
    ]
    args = [gdn_in, gates, gates_t, cw, ng]
    if has_state:
        conv_st, s0 = state
        in_specs += [
            pl.BlockSpec((G, CONV_W - 1, CONV_CH), lambda b, c: (b, 0, 0)),
            pl.BlockSpec((G, H_A, DK_A, DV_A), lambda b, c: (b, 0, 0, 0)),
        ]
        args += [conv_st, s0]
    return pl.pallas_call(
        functools.partial(_gdn_kernel, L=L, G=G, has_state=has_state),
        grid=(n_seq // G, n_c),
        in_specs=in_specs,
        out_specs=[
            pl.BlockSpec((G, L, H_A * DV_A), lambda b, c: (b, c, 0)),
            pl.BlockSpec((G, CONV_W - 1, CONV_CH), lambda b, c: (b, 0, 0)),
            pl.BlockSpec((G, H_A, DK_A, DV_A), lambda b, c: (b, 0, 0, 0)),
        ],
        out_shape=[
            jax.ShapeDtypeStruct((n_seq, T, H_A * DV_A), F32),
            jax.ShapeDtypeStruct((n_seq, CONV_W - 1, CONV_CH), F32),
            jax.ShapeDtypeStruct((n_seq, H_A, DK_A, DV_A), F32),
        ],
        scratch_shapes=[pltpu.VMEM((G, L + SUBLANE, CONV_CH), F32), pltpu.VMEM((G, H_A, DK_A, DV_A), F32)],
        compiler_params=pltpu.CompilerParams(dimension_semantics=("parallel", "arbitrary"),
                                             vmem_limit_bytes=VMEM_LIMIT),
        name=f"gdn_L{L}",
    )(*args)


def _mlstm_kernel(*refs, L, G, has_state):
    if has_state:
        (xin_ref, gate_ref, gatet_ref, ng_ref, c0_ref, n0_ref, m0_ref,
         mix_ref, cnew_ref, nnew_ref, mnew_ref, c_ref, n_ref, m_ref) = refs
    else:
        (xin_ref, gate_ref, gatet_ref, ng_ref,
         mix_ref, cnew_ref, nnew_ref, mnew_ref, c_ref, n_ref, m_ref) = refs
    c = pl.program_id(1)

    @pl.when(c == 0)
    def _():
        c_ref[...] = jnp.zeros_like(c_ref)
        n_ref[...] = jnp.zeros_like(n_ref)
        m_ref[...] = jnp.zeros_like(m_ref)
        if has_state:
            c_ref[:, :, 0:DK_B, :] = c0_ref[...]
            n_ref[:, 0:H_B, 0:DK_B] = n0_ref[...]
            m_ref[:, 0:1, 0:H_B] = m0_ref[...]

    tril, _, triu = _chunk_masks(L)
    tril_f, triu_f = tril.astype(F32), triu.astype(F32)

    chains = [(g, h) for g in range(G) for h in range(H_B)]
    nc = range(len(chains))
    c_old = [c_ref[g, h] for g, h in chains]
    n_old = [n_ref[g, h:h + 1, :] for g, h in chains]
    m_old = [m_ref[g, 0:1, h:h + 1] for g, h in chains]

    v0 = 2 * H_B * LANE
    q = [xin_ref[g, :, h * LANE:(h + 1) * LANE] * (DK_B ** -0.5) for g, h in chains]
    k = [xin_ref[g, :, (H_B + h) * LANE:(H_B + h + 1) * LANE] for g, h in chains]
    v = [xin_ref[g, :, v0 + h * DV_B:v0 + (h + 1) * DV_B] for g, h in chains]
    ig_c, b_c, b_last, d_log = [], [], [], []
    for g in range(G):
        gact = gate_ref[g]
        gact_t = gatet_ref[g, 0]
        cum_c = _dot_hi(tril_f, gact)
        cum_r = _dot_hi(gact_t, triu_f)
        for h in range(H_B):
            ig_c.append(gact[:, 8 + h:9 + h])
            b_c.append(cum_c[:, 12 + h:13 + h])
            b_last.append(cum_c[L - 1:L, 12 + h:13 + h])
            d_log.append(jnp.where(tril, cum_c[:, 12 + h:13 + h] - cum_r[12 + h:13 + h, :]
                                   + gact_t[8 + h:9 + h, :], NEG))
    qk = [_dot(q[i], k[i], _NT) for i in nc]
    qc = [_dot(q[i], c_old[i]) for i in nc]
    inter = [b_c[i] + m_old[i] for i in nc]
    m_t = [jnp.maximum(inter[i], jnp.max(d_log[i], axis=-1, keepdims=True)) for i in nc]
    s = [qk[i] * jnp.exp(d_log[i] - m_t[i]) for i in nc]
    e_inter = [jnp.exp(inter[i] - m_t[i]) for i in nc]
    sv = [_dot(s[i], v[i]) for i in nc]
    m_new = [m_t[i][L - 1:L, :] for i in nc]
    kw = [k[i] * jnp.exp(b_last[i] - b_c[i] + ig_c[i] - m_new[i]) for i in nc]
    f_tot = [jnp.exp(b_last[i] + m_old[i] - m_new[i]) for i in nc]
    c_new = [f_tot[i] * c_old[i] + _dot(kw[i], v[i], _TN) for i in nc]
    n_new = [f_tot[i] * n_old[i] + jnp.sum(kw[i], axis=0, keepdims=True) for i in nc]
    qn = [jnp.sum(q[i] * n_old[i], axis=-1, keepdims=True) for i in nc]
    ssum = [jnp.sum(s[i], axis=-1, keepdims=True) for i in nc]
    den = [jnp.maximum(jnp.abs(e_inter[i] * qn[i] + ssum[i]), jnp.exp(-m_t[i])) for i in nc]
    hh = [(e_inter[i] * qc[i] + sv[i]) / den[i] for i in nc]
    ms = [jnp.mean(hh[i] * hh[i], axis=-1, keepdims=True) for i in nc]
    hn = [hh[i] * lax.rsqrt(ms[i] + EPS) for i in nc]
    for i, (g, h) in enumerate(chains):
        og = xin_ref[g, :, v0 + H_B * DV_B + h * DV_B:v0 + H_B * DV_B + (h + 1) * DV_B]
        mix_ref[g, :, h * DV_B:(h + 1) * DV_B] = hn[i] * ng_ref[h:h + 1, :] * _sigmoid(og)
    for i, (g, h) in enumerate(chains):
        c_ref[g, h] = c_new[i]
        n_ref[g, h:h + 1, :] = n_new[i]
        m_ref[g, 0:1, h:h + 1] = m_new[i]
        cnew_ref[g, h] = c_new[i][0:DK_B, :]
        nnew_ref[g, h:h + 1, :] = n_new[i][:, 0:DK_B]
        mnew_ref[g, 0:1, h:h + 1] = m_new[i]


def _mlstm(ml_in, gates, gates_t, ng, *, L, G, state=None):
    n_seq, T, _ = ml_in.shape
    n_c = T // L
    has_state = state is not None
    in_specs = [
        pl.BlockSpec((G, L, MLP_W), lambda b, c: (b, c, 0)),
        pl.BlockSpec((G, L, LANE), lambda b, c: (b, c, 0)),
        pl.BlockSpec((G, 1, N_GATE, L), lambda b, c: (b, c, 0, 0)),
        pl.BlockSpec((H_B, DV_B), lambda b, c: (0, 0)),
    ]
    args = [ml_in, gates, gates_t, ng]
    if has_state:
        c0, n0, m0 = state
        in_specs += [
            pl.BlockSpec((G, H_B, DK_B, DV_B), lambda b, c: (b, 0, 0, 0)),
            pl.BlockSpec((G, H_B, DK_B), lambda b, c: (b, 0, 0)),
            pl.BlockSpec((G, 1, H_B), lambda b, c: (b, 0, 0)),
        ]
        args += [c0, n0, m0]
    return pl.pallas_call(
        functools.partial(_mlstm_kernel, L=L, G=G, has_state=has_state),
        grid=(n_seq // G, n_c),
        in_specs=in_specs,
        out_specs=[
            pl.BlockSpec((G, L, H_B * DV_B), lambda b, c: (b, c, 0)),
            pl.BlockSpec((G, H_B, DK_B, DV_B), lambda b, c: (b, 0, 0, 0)),
            pl.BlockSpec((G, H_B, DK_B), lambda b, c: (b, 0, 0)),
            pl.BlockSpec((G, 1, H_B), lambda b, c: (b, 0, 0)),
        ],
        out_shape=[
            jax.ShapeDtypeStruct((n_seq, T, H_B * DV_B), F32),
            jax.ShapeDtypeStruct((n_seq, H_B, DK_B, DV_B), F32),
            jax.ShapeDtypeStruct((n_seq, H_B, DK_B), F32),
            jax.ShapeDtypeStruct((n_seq, 1, H_B), F32),
        ],
        scratch_shapes=[pltpu.VMEM((G, H_B, LANE, DV_B), F32), pltpu.VMEM((G, SUBLANE, LANE), F32),
                        pltpu.VMEM((G, SUBLANE, LANE), F32)],
        compiler_params=pltpu.CompilerParams(dimension_semantics=("parallel", "arbitrary"),
                                             vmem_limit_bytes=VMEM_LIMIT),
        name=f"mlstm_L{L}",
    )(*args)


def _outproj_kernel(xp_ref, xs_ref, map_ref, mas_ref, mbp_ref, mbs_ref, wo_ref, g_ref, rw_ref, rb_ref,
                    x1_ref, xsort_ref, info_ref, cpad_ref, *, n_p_tiles):
    half = H_A * DV_A
    tm = xp_ref.shape[0]

    def body(seg):
        x_ref, ma_ref, mb_ref = (xp_ref, xs_ref)[seg], (map_ref, mas_ref)[seg], (mbp_ref, mbs_ref)[seg]
        x1 = (x_ref[...] + jnp.dot(ma_ref[...].astype(BF16), wo_ref[:half, :], preferred_element_type=F32)
              + jnp.dot(mb_ref[...].astype(BF16), wo_ref[half:, :], preferred_element_type=F32))
        x1_ref[...] = x1
        hn = _rms(x1, g_ref[...])
        hn_hi = hn.astype(BF16)
        hn_lo = (hn - hn_hi.astype(F32)).astype(BF16)
        logits = (jnp.dot(hn_hi, rw_ref[0], preferred_element_type=F32)
                  + jnp.dot(hn_hi, rw_ref[1], preferred_element_type=F32)
                  + jnp.dot(hn_lo, rw_ref[0], preferred_element_type=F32)) + rb_ref[...]

        vals = logits.T[:N_EXPERTS, :]
        e_iota = lax.broadcasted_iota(jnp.int32, (N_EXPERTS, tm), 0)
        sels, tops = [], []
        for _ in range(TOP_K):
            m = jnp.max(vals, axis=0, keepdims=True)
            first = jnp.min(jnp.where(vals == m, e_iota, N_EXPERTS), axis=0, keepdims=True)
            sel = e_iota == first
            vals = jnp.where(sel, -jnp.inf, vals)
            sels.append(sel)
            tops.append(m)
        ex = [jnp.exp(t - tops[0]) for t in tops]
        den = ex[0] + ex[1] + ex[2] + ex[3]
        gates = [e / den for e in ex]
        mask = sels[0].astype(F32) + sels[1].astype(F32) + sels[2].astype(F32) + sels[3].astype(F32)
        ri = lax.broadcasted_iota(jnp.int32, (tm, tm), 0)
        ci = lax.broadcasted_iota(jnp.int32, (tm, tm), 1)
        rank = _dot(mask, (ri < ci).astype(F32))
        cnt = jnp.sum(mask, axis=1, keepdims=True)
        cpad = jnp.ceil(cnt * (1.0 / SEG_ALIGN)) * SEG_ALIGN
        cpad_b = jnp.broadcast_to(cpad, (N_EXPERTS, tm))
        er = lax.broadcasted_iota(jnp.int32, (N_EXPERTS, N_EXPERTS), 0)
        ec = lax.broadcasted_iota(jnp.int32, (N_EXPERTS, N_EXPERTS), 1)
        seg_off = _dot((er > ec).astype(F32), cpad_b)
        pos = seg_off + rank
        q = [jnp.sum(jnp.where(s, pos, 0.0), axis=0, keepdims=True) for s in sels]

        j_iota = lax.broadcasted_iota(jnp.int32, (MOE_CAP, tm), 0).astype(F32)
        perm = jnp.zeros((MOE_CAP, tm), F32)
        for kk in range(TOP_K):
            perm = jnp.where(j_iota == q[kk], 1.0, perm)
        xsorted = _dot(perm, hn)
        xsort_ref[...] = xsorted

        r_iota = lax.broadcasted_iota(jnp.int32, (LANE, tm), 0)
        info = jnp.zeros((LANE, tm), F32)
        for kk in range(TOP_K):
            info = jnp.where(r_iota == kk, q[kk], info)
            info = jnp.where(r_iota == TOP_K + kk, gates[kk], info)
        info_ref[...] = info.T
        cpad_ref[0] = cpad_b[:, :LANE]

    _for_segment(n_p_tiles, body)


def _outproj(xp, xs, ma_p, ma_s, mb_p, mb_s, w_out, g, rw, rb, tm):
    n_p, n_s = xp.shape[0], xs.shape[0]
    n = n_p + n_s
    nt = n // tm
    npt = n_p // tm
    half = H_A * DV_A
    return pl.pallas_call(
        functools.partial(_outproj_kernel, n_p_tiles=npt),
        grid=(nt,),
        in_specs=_two_segment_specs(tm, D_MODEL, npt) + _two_segment_specs(tm, half, npt)
        + _two_segment_specs(tm, half, npt) + [
            pl.BlockSpec((D_MODEL, D_MODEL), lambda i: (0, 0)),
            pl.BlockSpec((1, D_MODEL), lambda i: (0, 0)),
            pl.BlockSpec((2, D_MODEL, LANE), lambda i: (0, 0, 0)),
            pl.BlockSpec((1, LANE), lambda i: (0, 0)),
        ],
        out_specs=[
            pl.BlockSpec((tm, D_MODEL), lambda i: (i, 0)),
            pl.BlockSpec((MOE_CAP, D_MODEL), lambda i: (i, 0)),
            pl.BlockSpec((tm, LANE), lambda i: (i, 0)),
            pl.BlockSpec((1, N_EXPERTS, LANE), lambda i: (i, 0, 0)),
        ],
        out_shape=[
            jax.ShapeDtypeStruct((n, D_MODEL), F32),
            jax.ShapeDtypeStruct((nt * MOE_CAP, D_MODEL), F32),
            jax.ShapeDtypeStruct((n, LANE), F32),
            jax.ShapeDtypeStruct((nt, N_EXPERTS, LANE), F32),
        ],
        compiler_params=pltpu.CompilerParams(dimension_semantics=("arbitrary",), vmem_limit_bytes=VMEM_LIMIT),
        name="outproj",
    )(xp, xs, ma_p, ma_s, mb_p, mb_s, w_out, g, rw, rb)


def _expert_kernel(be_ref, bj_ref, tf_ref, tl_ref, cov_ref, nu_ref, vt_ref, ct_ref, lt_ref, nx_ref, ws_ref,
                   xs_hbm, wgu_hbm, bgu_ref, wd_hbm, bd_ref, ys_hbm,
                   xbuf, ybuf, gsem, ssem, wgu_st, wd_st, wsem, wgu_bf, wd_bf, *, nt):
    b = pl.program_id(0)
    n_used = nu_ref[0]
    slot = b % 2

    def start_pieces(bb, copy, s):
        e = be_ref[bb]
        base = bj_ref[bb] * MOE_BLK

        def body(t, carry):
            k = e * nt + t
            lo = jnp.maximum(vt_ref[k], base)
            ln = jnp.minimum(ct_ref[k], base + MOE_BLK) - lo

            @pl.when(ln > 0)
            def _():
                copy(s, pl.multiple_of(lt_ref[k] + lo, SEG_ALIGN), pl.multiple_of(lo - base, SEG_ALIGN),
                     pl.multiple_of(ln, SEG_ALIGN)).start()
            return carry

        lax.fori_loop(tf_ref[bb], tl_ref[bb] + 1, body, 0)

    def weight_copies(e):
        return (pltpu.make_async_copy(wgu_hbm.at[e], wgu_st, wsem.at[0]),
                pltpu.make_async_copy(wd_hbm.at[e], wd_st, wsem.at[1]))

    def cast_weights(p):
        wgu_bf[p] = wgu_st[...].astype(BF16)
        wd_bf[p] = wd_st[...].astype(BF16)

    def gather_copy(s, src, dst, size):
        return pltpu.make_async_copy(xs_hbm.at[pl.ds(src, size)], xbuf.at[s, pl.ds(dst, size)], gsem.at[s])

    def scatter_copy(s, src, dst, size):
        return pltpu.make_async_copy(ybuf.at[s, pl.ds(dst, size)], ys_hbm.at[pl.ds(src, size)], ssem.at[s])

    def wait_rows(count, copy, s):
        @pl.when(count > 0)
        def _():
            copy(s, 0, 0, pl.multiple_of(count, SEG_ALIGN)).wait()

    @pl.when(b == 0)
    def _():
        xbuf[...] = jnp.zeros_like(xbuf)
        start_pieces(0, gather_copy, 0)

    @pl.when(b + 1 < n_used)
    def _():
        start_pieces(b + 1, gather_copy, 1 - slot)

    @pl.when(b < n_used)
    def _():
        e = be_ref[b]
        first = jnp.logical_or(b == 0, be_ref[jnp.maximum(b - 1, 0)] != e)
        last = jnp.logical_or(b == n_used - 1, be_ref[jnp.minimum(b + 1, n_used - 1)] != e)
        has_next = nx_ref[b] < N_EXPERTS
        p = ws_ref[b]

        @pl.when(b == 0)
        def _():
            for cp in weight_copies(e):
                cp.start()
            for cp in weight_copies(e):
                cp.wait()
            cast_weights(p)

        @pl.when(jnp.logical_and(first, has_next))
        def _():
            for cp in weight_copies(nx_ref[b]):
                cp.start()

        wait_rows(cov_ref[b], gather_copy, slot)
        hgu = jnp.dot(xbuf[slot].astype(BF16), wgu_bf[p], preferred_element_type=F32) + bgu_ref[0]
        gate = jnp.minimum(hgu[:, :D_FF], SWIGLU_LIMIT)
        up = jnp.clip(hgu[:, D_FF:], -SWIGLU_LIMIT, SWIGLU_LIMIT)
        act = (up + 1.0) * gate * _sigmoid(SWIGLU_ALPHA * gate)
        y = jnp.dot(act.astype(BF16), wd_bf[p], preferred_element_type=F32) + bd_ref[0]

        @pl.when(jnp.logical_and(last, has_next))
        def _():
            for cp in weight_copies(nx_ref[b]):
                cp.wait()
            cast_weights(1 - p)

        @pl.when(b >= 2)
        def _():
            wait_rows(cov_ref[jnp.maximum(b - 2, 0)], scatter_copy, slot)

        ybuf[slot] = y
        start_pieces(b, scatter_copy, slot)

        @pl.when(b == n_used - 1)
        def _():
            wait_rows(cov_ref[b], scatter_copy, slot)
            wait_rows(jnp.where(b >= 1, cov_ref[jnp.maximum(b - 1, 0)], 0), scatter_copy, 1 - slot)


def _experts(tables, xs, w_gu, b_gu, w_down, b_down, nt):
    nb = tables[0].shape[0]

    def bias_blk(b, *t):
        return (t[0][jnp.minimum(b, t[5][0] - 1)], 0, 0)

    grid_spec = pltpu.PrefetchScalarGridSpec(
        num_scalar_prefetch=len(tables),
        grid=(nb,),
        in_specs=[
            pl.BlockSpec(memory_space=pl.ANY),
            pl.BlockSpec(memory_space=pl.ANY),
            pl.BlockSpec((1, 1, 2 * D_FF), bias_blk),
            pl.BlockSpec(memory_space=pl.ANY),
            pl.BlockSpec((1, 1, D_MODEL), bias_blk),
        ],
        out_specs=pl.BlockSpec(memory_space=pl.ANY),
        scratch_shapes=[
            pltpu.VMEM((2, MOE_BLK, D_MODEL), F32),
            pltpu.VMEM((2, MOE_BLK, D_MODEL), F32),
            pltpu.SemaphoreType.DMA((2,)),
            pltpu.SemaphoreType.DMA((2,)),
            pltpu.VMEM((D_MODEL, 2 * D_FF), F32),
            pltpu.VMEM((D_FF, D_MODEL), F32),
            pltpu.SemaphoreType.DMA((2,)),
            pltpu.VMEM((2, D_MODEL, 2 * D_FF), BF16),
            pltpu.VMEM((2, D_FF, D_MODEL), BF16),
        ],
    )
    return pl.pallas_call(
        functools.partial(_expert_kernel, nt=nt),
        grid_spec=grid_spec,
        out_shape=jax.ShapeDtypeStruct(xs.shape, xs.dtype),
        input_output_aliases={len(tables): 0},
        compiler_params=pltpu.CompilerParams(dimension_semantics=("arbitrary",), vmem_limit_bytes=VMEM_LIMIT),
        name="experts",
    )(*tables, xs, w_gu, b_gu, w_down, b_down)


def _combine_kernel(ys_ref, info_ref, x1_ref, pp_ref, ps_ref, gple_ref, wg_ref, wp_ref, gfin_ref,
                    outp_ref, outs_ref, *, n_p_tiles):
    tm = x1_ref.shape[0]

    def body(seg):
        p_ref, out_ref = (pp_ref, ps_ref)[seg], (outp_ref, outs_ref)[seg]
        info = info_ref[...]
        j_iota = lax.broadcasted_iota(jnp.int32, (tm, MOE_CAP), 1).astype(F32)
        gmat = jnp.zeros((tm, MOE_CAP), F32)
        for kk in range(TOP_K):
            gmat = jnp.where(j_iota == info[:, kk:kk + 1], info[:, TOP_K + kk:TOP_K + kk + 1], gmat)
        x2 = x1_ref[...] + jnp.dot(gmat.astype(BF16), ys_ref[...].astype(BF16), preferred_element_type=F32)
        hn = _rms(x2, gple_ref[...]).astype(BF16)
        gate = _sigmoid(jnp.dot(hn, wg_ref[...], preferred_element_type=F32))
        pe = jnp.dot(p_ref[...].astype(BF16), wp_ref[...], preferred_element_type=F32)
        x3 = x2 + gate * pe
        out_ref[...] = _rms(x3, gfin_ref[...])

    _for_segment(n_p_tiles, body)


def _combine(ys, info, x1, pp, ps, g_ple, w_gate, w_p, g_fin, tm):
    n_p, n_s = pp.shape[0], ps.shape[0]
    n = n_p + n_s
    nt = n // tm
    npt = n_p // tm
    return pl.pallas_call(
        functools.partial(_combine_kernel, n_p_tiles=npt),
        grid=(nt,),
        in_specs=[
            pl.BlockSpec((MOE_CAP, D_MODEL), lambda i: (i, 0)),
            pl.BlockSpec((tm, LANE), lambda i: (i, 0)),
            pl.BlockSpec((tm, D_MODEL), lambda i: (i, 0)),
        ] + _two_segment_specs(tm, PLE_DIM, npt) + [
            pl.BlockSpec((1, D_MODEL), lambda i: (0, 0)),
            pl.BlockSpec((D_MODEL, D_MODEL), lambda i: (0, 0)),
            pl.BlockSpec((PLE_DIM, D_MODEL), lambda i: (0, 0)),
            pl.BlockSpec((1, D_MODEL), lambda i: (0, 0)),
        ],
        out_specs=_two_segment_specs(tm, D_MODEL, npt),
        out_shape=[jax.ShapeDtypeStruct((n_p, D_MODEL), F32), jax.ShapeDtypeStruct((n_s, D_MODEL), F32)],
        compiler_params=pltpu.CompilerParams(dimension_semantics=("arbitrary",), vmem_limit_bytes=VMEM_LIMIT),
        name="combine",
    )(ys, info, x1, pp, ps, g_ple, w_gate, w_p, g_fin)


def _block_tables(seg_len, nb):
    nt = seg_len.shape[0]
    seg_off = jnp.cumsum(seg_len, axis=1) - seg_len
    seg_end = jnp.cumsum(seg_len, axis=0).T
    seg_start = seg_end - seg_len.T
    n_rows = seg_end[:, -1]
    n_blk = (n_rows + MOE_BLK - 1) // MOE_BLK
    blk_end = jnp.cumsum(n_blk)
    b = jnp.arange(nb, dtype=jnp.int32)
    block_e = jnp.minimum(jnp.sum((blk_end[None, :] <= b[:, None]).astype(jnp.int32), axis=1), N_EXPERTS - 1)
    block_j = b - (blk_end - n_blk)[block_e]
    base = block_j * MOE_BLK
    t_first = jnp.sum((seg_end[block_e] <= base[:, None]).astype(jnp.int32), axis=1)
    t_last = jnp.sum((seg_start[block_e] < (base + MOE_BLK)[:, None]).astype(jnp.int32), axis=1) - 1
    cover = jnp.clip(n_rows[block_e] - base, 0, MOE_BLK)
    seg_shift = (jnp.arange(nt, dtype=jnp.int32)[:, None] * MOE_CAP + seg_off).T - seg_start
    idx = jnp.where(n_blk > 0, jnp.arange(N_EXPERTS, dtype=jnp.int32), N_EXPERTS)
    nxt = jnp.concatenate([lax.cummin(idx, axis=0, reverse=True)[1:], jnp.full((1,), N_EXPERTS, jnp.int32)])
    parity = (jnp.cumsum((n_blk > 0).astype(jnp.int32)) - 1) % 2
    tables = (block_e, block_j, t_first, t_last, cover, blk_end[-1:], seg_start.reshape(-1),
              seg_end.reshape(-1), seg_shift.reshape(-1), nxt[block_e], parity[block_e])
    return tuple(t.astype(jnp.int32) for t in tables)


def _rearranged_in_weights(w_in):
    o = np.cumsum([0, CONV_CH, H_A * DV_A, H_A, H_A, H_B * DK_B, H_B * DK_B, H_B * DV_B, H_B * DV_B, H_B, H_B])
    conv_in, z_a, a_a, b_a, q_b, k_b, v_b, o_b, i_b, f_b = (w_in[:, int(o[j]):int(o[j + 1])] for j in range(10))
    zpad = jnp.zeros((D_MODEL, LANE - DK_B), w_in.dtype)

    def pad_heads(w):
        return jnp.concatenate([jnp.concatenate([w[:, h * DK_B:(h + 1) * DK_B], zpad], axis=1) for h in range(H_B)],
                               axis=1)

    small = jnp.concatenate([a_a, b_a, i_b, f_b], axis=1)
    w_all = jnp.concatenate([conv_in, z_a, pad_heads(q_b), pad_heads(k_b), v_b, o_b,
                             small, jnp.zeros((D_MODEL, LANE - N_GATE), w_in.dtype)], axis=1)
    return w_all.astype(BF16), small.T.astype(BF16)


def _gate_params(a_log, dt_bias, i_bias, f_bias):
    z4 = jnp.zeros((4,), F32)
    alog = jnp.concatenate([a_log.astype(F32), z4, z4, z4])
    bias = jnp.concatenate([dt_bias.astype(F32), z4, i_bias.astype(F32), f_bias.astype(F32)])
    pad = jnp.zeros((LANE - N_GATE,), F32)
    pcol = jnp.zeros((SUBLANE, LANE), F32).at[0].set(jnp.concatenate([alog, pad])).at[1].set(
        jnp.concatenate([bias, pad]))
    prow = jnp.zeros((N_GATE, LANE), F32).at[:, 0].set(alog).at[:, 1].set(bias)
    return pcol, prow


def kernel(x_prompt, x_sample, p_prompt, p_sample, state_conv, state_gdn, state_mlstm_c, state_mlstm_n, state_mlstm_m, norm_attn_g, w_in, conv_w, gdn_a_log, gdn_dt_bias, gdn_norm_g, mlstm_i_bias, mlstm_f_bias, mlstm_norm_g, w_out, norm_moe_g, router_w, router_b, expert_w_gu, expert_b_gu, expert_w_down, expert_b_down, norm_ple_g, ple_gate_w, ple_w, final_norm_g):
    bp, tp, _ = x_prompt.shape
    bs, ts, _ = x_sample.shape
    n_p, n_s = bp * tp, bs * ts
    n = n_p + n_s
    lp, ls = min(tp, CHUNK), min(ts, CHUNK)
    tm = 512
    gp_a = 4 if bp % 4 == 0 else 1
    gp_b = 2 if bp % 2 == 0 else 1
    gs = 8 if bs % 8 == 0 else 1
    assert tp % lp == 0 and ts % ls == 0 and n_p % tm == 0 and n_s % tm == 0 and ls % SUBLANE == 0

    xp = x_prompt.reshape(n_p, D_MODEL)
    xs = x_sample.reshape(n_s, D_MODEL)

    w_all, ws_t = _rearranged_in_weights(w_in[0])
    pcol, prow = _gate_params(gdn_a_log[0], gdn_dt_bias[0], mlstm_i_bias[0], mlstm_f_bias[0])
    gdn_p, gdn_s, ml_p, ml_s, gate_p, gate_s, gatet_p, gatet_s = _inproj(
        xp, xs, norm_attn_g[0].reshape(1, D_MODEL), w_all, ws_t, pcol, prow, tm)
    gt_p = gatet_p.reshape(N_GATE, bp, tp // lp, lp).transpose(1, 2, 0, 3)
    gt_s = gatet_s.reshape(N_GATE, bs, ts // ls, ls).transpose(1, 2, 0, 3)

    cw = jnp.zeros((SUBLANE, CONV_CH), F32).at[:CONV_W].set(conv_w[0].astype(F32))
    ng_a = gdn_norm_g[0].reshape(1, DV_A).astype(F32)
    ng_b = mlstm_norm_g[0].reshape(H_B, DV_B).astype(F32)
    ma_p, conv_p, gdn_st_p = _gdn(gdn_p.reshape(bp, tp, GDN_W), gate_p.reshape(bp, tp, LANE), gt_p, cw, ng_a,
                                  L=lp, G=gp_a)
    ma_s, conv_s, gdn_st_s = _gdn(gdn_s.reshape(bs, ts, GDN_W), gate_s.reshape(bs, ts, LANE), gt_s, cw, ng_a,
                                  L=ls, G=gs, state=(state_conv[0], state_gdn[0]))
    mb_p, c_p, nn_p, m_p = _mlstm(ml_p.reshape(bp, tp, MLP_W), gate_p.reshape(bp, tp, LANE), gt_p, ng_b,
                                  L=lp, G=gp_b)
    mb_s, c_s, nn_s, m_s = _mlstm(ml_s.reshape(bs, ts, MLP_W), gate_s.reshape(bs, ts, LANE), gt_s, ng_b,
                                  L=ls, G=gs,
                                  state=(state_mlstm_c[0], state_mlstm_n[0], state_mlstm_m[0].reshape(bs, 1, H_B)))
    half = H_A * DV_A

    rw = jnp.zeros((D_MODEL, LANE), F32).at[:, :N_EXPERTS].set(router_w[0])
    rw_hi = rw.astype(BF16)
    rw = jnp.stack([rw_hi, (rw - rw_hi.astype(F32)).astype(BF16)])
    rb = jnp.full((1, LANE), NEG, F32).at[0, :N_EXPERTS].set(router_b[0])
    x1, x_sorted, info, seg_len = _outproj(xp, xs, ma_p.reshape(n_p, half), ma_s.reshape(n_s, half),
                                           mb_p.reshape(n_p, half), mb_s.reshape(n_s, half),
                                           w_out[0].astype(BF16), norm_moe_g[0].reshape(1, D_MODEL), rw, rb, MOE_TM)

    nt = n // MOE_TM
    nb = -(-(n * TOP_K + nt * N_EXPERTS * (SEG_ALIGN - 1)) // MOE_BLK) + N_EXPERTS
    tables = _block_tables(seg_len[:, :, 0].astype(jnp.int32), nb)
    y_sorted = _experts(tables, x_sorted, expert_w_gu[0], expert_b_gu[0].reshape(N_EXPERTS, 1, 2 * D_FF),
                        expert_w_down[0], expert_b_down[0].reshape(N_EXPERTS, 1, D_MODEL), nt)
    y_p, y_s = _combine(y_sorted, info, x1, p_prompt[0].reshape(n_p, PLE_DIM),
                        p_sample[0].reshape(n_s, PLE_DIM), norm_ple_g[0].reshape(1, D_MODEL),
                        ple_gate_w[0].astype(BF16), ple_w[0].astype(BF16), final_norm_g.reshape(1, D_MODEL), MOE_TM)

    return (y_p.reshape(bp, tp, D_MODEL), y_s.reshape(bs, ts, D_MODEL),
            conv_p[None], gdn_st_p[None], c_p[None], nn_p[None], m_p.reshape(1, bp, H_B),
            conv_s[None], gdn_st_s[None], c_s[None], nn_s[None], m_s.reshape(1, bs, H_B))
```

```python
import functools

import numpy as np
import jax
import jax.numpy as jnp
from jax import lax
from jax.experimental import pallas as pl
from jax.experimental.pallas import tpu as pltpu

F32 = jnp.float32
BF16 = jnp.bfloat16

D_MODEL = 1024
H_A, DK_A, DV_A = 4, 128, 128
H_B, DK_B, DV_B = 4, 64, 128
CONV_W = 4
CONV_CH = H_A * (2 * DK_A + DV_A)
N_EXPERTS = 32
TOP_K = 4
D_FF = 1024
SWIGLU_LIMIT = 7.0
SWIGLU_ALPHA = 1.702
PLE_DIM = 256
EPS = 1e-6
NEG = -1e30
CHUNK = 64

LANE = 128
SUBLANE = 8
GDN_W = CONV_CH + H_A * DV_A
MLP_W = 2 * H_B * LANE + 2 * H_B * DV_B
N_GATE = 16

VMEM_LIMIT = 48 * 1024 * 1024

MOE_TM = 256
MOE_BLK = 256
SEG_ALIGN = SUBLANE
MOE_CAP = -(-(MOE_TM * TOP_K + N_EXPERTS * (SEG_ALIGN - 1)) // LANE) * LANE

HI = lax.Precision.HIGHEST

_NN = (((1,), (0,)), ((), ()))
_NT = (((1,), (1,)), ((), ()))
_TN = (((0,), (0,)), ((), ()))


def _dot(a, b, dims=_NN):
    return lax.dot_general(a.astype(BF16), b.astype(BF16), dims, preferred_element_type=F32)


def _dot_hi(a, b, dims=_NN):
    return lax.dot_general(a, b, dims, precision=HI, preferred_element_type=F32)


def _rms(x, g):
    return x * lax.rsqrt(jnp.mean(x * x, axis=-1, keepdims=True) + EPS) * g


def _softplus(t):
    return jnp.maximum(t, 0.0) + jnp.log1p(jnp.exp(-jnp.abs(t)))


def _sigmoid(t):
    return 1.0 / (1.0 + jnp.exp(-t))


def _silu(t):
    return t * _sigmoid(t)


def _activate_gates(raw, idx, alog, bias):
    t = raw + bias
    g = -jnp.exp(alog) * _softplus(t)
    beta = _sigmoid(t)
    lf = -_softplus(-t)
    return jnp.where(idx < 4, g, jnp.where(idx < 8, beta, jnp.where(idx < 12, t, lf)))


def _two_segment_specs(tm, width, n_p_tiles):
    return [pl.BlockSpec((tm, width), lambda i: (jnp.minimum(i, n_p_tiles - 1), 0)),
            pl.BlockSpec((tm, width), lambda i: (jnp.maximum(i - n_p_tiles, 0), 0))]


def _for_segment(n_p_tiles, body):
    i = pl.program_id(0)

    @pl.when(i < n_p_tiles)
    def _():
        body(0)

    @pl.when(i >= n_p_tiles)
    def _():
        body(1)


def _gdn_preactivate(raw, xc_ref, cw_ref, first_of_seq):
    tm = raw.shape[0]
    xc_ref[0:SUBLANE, :] = jnp.where(first_of_seq, 0.0, xc_ref[tm:tm + SUBLANE, :])
    xc_ref[SUBLANE:SUBLANE + tm, :] = raw[:, :CONV_CH]
    base = SUBLANE - (CONV_W - 1)
    conv = xc_ref[base:base + tm, :] * cw_ref[0:1, :]
    for j in range(1, CONV_W):
        conv = conv + xc_ref[base + j:base + j + tm, :] * cw_ref[j:j + 1, :]
    act = _silu(conv)
    parts = []
    for h in range(H_A):
        qh = act[:, h * DK_A:(h + 1) * DK_A]
        parts.append(qh * (lax.rsqrt(jnp.sum(qh * qh, axis=-1, keepdims=True) + EPS) * (DK_A ** -0.5)))
    for h in range(H_A):
        kh = act[:, H_A * DK_A + h * DK_A:H_A * DK_A + (h + 1) * DK_A]
        parts.append(kh * lax.rsqrt(jnp.sum(kh * kh, axis=-1, keepdims=True) + EPS))
    parts.append(act[:, 2 * H_A * DK_A:])
    parts.append(_silu(raw[:, CONV_CH:]))
    new_conv = xc_ref[SUBLANE + tm - (CONV_W - 1):SUBLANE + tm, :]
    return jnp.concatenate(parts, axis=-1), new_conv


def _inproj_kernel(xp_ref, xs_ref, g_ref, wa_ref, wb_ref, wst_ref, pc_ref, pr_ref, cw_ref,
                   gdnp_ref, gdns_ref, mlp_ref, mls_ref, gatep_ref, gates_ref, gatetp_ref, gatets_ref, cnew_ref,
                   xc_ref, raw_ref, *, n_p_tiles, tiles_per_seq):
    tm = xp_ref.shape[0]
    i = pl.program_id(0)

    @pl.when(i == 0)
    def _():
        xc_ref[...] = jnp.zeros_like(xc_ref)
        raw_ref[...] = jnp.zeros_like(raw_ref)

    def preactivate_previous_tile():
        pre, new_conv = _gdn_preactivate(raw_ref[...], xc_ref, cw_ref, (i - 1) % tiles_per_seq == 0)
        gdnp_ref[...] = pre
        cnew_ref[0] = new_conv

    def body(seg):
        x_ref = (xp_ref, xs_ref)[seg]
        ml_ref = (mlp_ref, mls_ref)[seg]
        gate_ref, gatet_ref = (gatep_ref, gates_ref)[seg], (gatetp_ref, gatets_ref)[seg]
        if seg == 0:
            preactivate_previous_tile()
        else:
            pl.when(i == n_p_tiles)(preactivate_previous_tile)
        hn = _rms(x_ref[...], g_ref[...]).astype(BF16)
        raw_gdn = jnp.dot(hn, wa_ref[...], preferred_element_type=F32)
        if seg == 0:
            raw_ref[...] = raw_gdn
        else:
            gdns_ref[...] = raw_gdn
        ml_ref[...] = jnp.dot(hn, wb_ref[:, :MLP_W], preferred_element_type=F32)
        raw = jnp.dot(hn, wb_ref[:, MLP_W:], preferred_element_type=F32)
        lane = lax.broadcasted_iota(jnp.int32, (tm, LANE), 1)
        gate_ref[...] = _activate_gates(raw, lane, pc_ref[0:1, :], pc_ref[1:2, :])
        raw_t = lax.dot_general(wst_ref[...], hn, _NT, preferred_element_type=F32)
        row = lax.broadcasted_iota(jnp.int32, (N_GATE, tm), 0)
        gatet_ref[...] = _activate_gates(raw_t, row, pr_ref[:, 0:1], pr_ref[:, 1:2])

    _for_segment(n_p_tiles, body)


def _inproj(xp, xs, g, w_gdn, w_rest, ws_t, pcol, prow, cw, tm, n_seq_p):
    n_p, n_s = xp.shape[0], xs.shape[0]
    npt = n_p // tm
    tiles_per_seq = npt // n_seq_p

    def out2(width):
        return _two_segment_specs(tm, width, npt)

    def shp2(width):
        return [jax.ShapeDtypeStruct((n_p, width), F32), jax.ShapeDtypeStruct((n_s, width), F32)]

    def prev_tile(i):
        return jnp.clip(i - 1, 0, npt - 1)

    return pl.pallas_call(
        functools.partial(_inproj_kernel, n_p_tiles=npt, tiles_per_seq=tiles_per_seq),
        grid=((n_p + n_s) // tm,),
        in_specs=_two_segment_specs(tm, D_MODEL, npt) + [
            pl.BlockSpec((1, D_MODEL), lambda i: (0, 0)),
            pl.BlockSpec((D_MODEL, GDN_W), lambda i: (0, 0)),
            pl.BlockSpec((D_MODEL, MLP_W + LANE), lambda i: (0, 0)),
            pl.BlockSpec((N_GATE, D_MODEL), lambda i: (0, 0)),
            pl.BlockSpec((SUBLANE, LANE), lambda i: (0, 0)),
            pl.BlockSpec((N_GATE, LANE), lambda i: (0, 0)),
            pl.BlockSpec((SUBLANE, CONV_CH), lambda i: (0, 0)),
        ],
        out_specs=[
            pl.BlockSpec((tm, GDN_W), lambda i: (prev_tile(i), 0)),
            pl.BlockSpec((tm, GDN_W), lambda i: (jnp.maximum(i - npt, 0), 0)),
        ] + out2(MLP_W) + out2(LANE) + [
            pl.BlockSpec((N_GATE, tm), lambda i: (0, jnp.minimum(i, npt - 1))),
            pl.BlockSpec((N_GATE, tm), lambda i: (0, jnp.maximum(i - npt, 0))),
            pl.BlockSpec((1, CONV_W - 1, CONV_CH), lambda i: (prev_tile(i) // tiles_per_seq, 0, 0)),
        ],
        out_shape=shp2(GDN_W) + shp2(MLP_W) + shp2(LANE) + [
            jax.ShapeDtypeStruct((N_GATE, n_p), F32), jax.ShapeDtypeStruct((N_GATE, n_s), F32),
            jax.ShapeDtypeStruct((n_seq_p, CONV_W - 1, CONV_CH), F32)],
        scratch_shapes=[pltpu.VMEM((tm + SUBLANE, CONV_CH), F32), pltpu.VMEM((tm, GDN_W), F32)],
        compiler_params=pltpu.CompilerParams(dimension_semantics=("arbitrary",), vmem_limit_bytes=VMEM_LIMIT),
        name="inproj",
    )(xp, xs, g, w_gdn, w_rest, ws_t, pcol, prow, cw)


def _chunk_masks(L):
    ri = lax.broadcasted_iota(jnp.int32, (L, L), 0)
    ci = lax.broadcasted_iota(jnp.int32, (L, L), 1)
    return ri >= ci, ri > ci, ri <= ci


def _gdn_kernel(*refs, L, G, has_state):
    if has_state:
        (xin_ref, gate_ref, gatet_ref, cw_ref, ng_ref, cst_ref, s0_ref,
         mix_ref, cnew_ref, snew_ref, xc_ref, s_ref) = refs
    else:
        xin_ref, gate_ref, gatet_ref, ng_ref, mix_ref, snew_ref, s_ref = refs
    c = pl.program_id(1)

    @pl.when(c == 0)
    def _():
        if has_state:
            xc_ref[:, 0:SUBLANE, :] = jnp.zeros((G, SUBLANE, CONV_CH), F32)
            xc_ref[:, SUBLANE - (CONV_W - 1):SUBLANE, :] = cst_ref[...]
            s_ref[...] = s0_ref[...]
        else:
            s_ref[...] = jnp.zeros_like(s_ref)

    if has_state:
        @pl.when(c > 0)
        def _():
            xc_ref[:, 0:SUBLANE, :] = xc_ref[:, L:L + SUBLANE, :]

    tril, strict, triu = _chunk_masks(L)
    tril_f, triu_f = tril.astype(F32), triu.astype(F32)
    base = SUBLANE - (CONV_W - 1)

    chains = [(g, h) for g in range(G) for h in range(H_A)]
    s_old = [s_ref[g, h] for g, h in chains]
    if has_state:
        for g in range(G):
            xc_ref[g, SUBLANE:SUBLANE + L, :] = xin_ref[g, :, :CONV_CH]

    q, k, v, beta, gc, gl, decay = [], [], [], [], [], [], []
    for g in range(G):
        if has_state:
            conv = xc_ref[g, base:base + L, :] * cw_ref[0:1, :]
            for j in range(1, CONV_W):
                conv = conv + xc_ref[g, base + j:base + j + L, :] * cw_ref[j:j + 1, :]
            cnew_ref[g] = xc_ref[g, SUBLANE + L - (CONV_W - 1):SUBLANE + L, :]
            act = _silu(conv)
        else:
            act = xin_ref[g, :, :CONV_CH]
        gact = gate_ref[g]
        cum_c = _dot_hi(tril_f, gact)
        cum_r = _dot_hi(gatet_ref[g, 0], triu_f)
        for h in range(H_A):
            q.append(act[:, h * DK_A:(h + 1) * DK_A])
            k.append(act[:, H_A * DK_A + h * DK_A:H_A * DK_A + (h + 1) * DK_A])
            v.append(act[:, 2 * H_A * DK_A + h * DV_A:2 * H_A * DK_A + (h + 1) * DV_A])
            beta.append(gact[:, 4 + h:5 + h])
            gc.append(cum_c[:, h:h + 1])
            gl.append(cum_c[L - 1:L, h:h + 1])
            gr = cum_r[h:h + 1, :]
            decay.append(jnp.where(tril, jnp.exp(jnp.where(tril, cum_c[:, h:h + 1] - gr, 0.0)), 0.0))

    nc = range(len(chains))
    if has_state:
        qss = [jnp.sum(q[i] * q[i], axis=-1, keepdims=True) for i in nc]
        kss = [jnp.sum(k[i] * k[i], axis=-1, keepdims=True) for i in nc]
        q = [q[i] * (lax.rsqrt(qss[i] + EPS) * (DK_A ** -0.5)) for i in nc]
        k = [k[i] * lax.rsqrt(kss[i] + EPS) for i in nc]
    kb = [k[i] * beta[i] for i in nc]
    egc = [jnp.exp(gc[i]) for i in nc]
    kk = [_dot(kb[i], k[i], _NT) for i in nc]
    qk = [_dot(q[i], k[i], _NT) for i in nc]
    eye = (lax.broadcasted_iota(jnp.int32, (L, L), 0) == lax.broadcasted_iota(jnp.int32, (L, L), 1)).astype(F32)
    pw = [-jnp.where(strict, kk[i] * decay[i], 0.0) for i in nc]
    t_inv = [eye + pw[i] for i in nc]
    span = 2
    while span < L:
        pw = [_dot(pw[i], pw[i]) for i in nc]
        t_inv = [t_inv[i] + _dot(t_inv[i], pw[i]) for i in nc]
        span *= 2
    sol = [_dot(t_inv[i], jnp.concatenate([v[i] * beta[i], kb[i] * egc[i]], axis=-1)) for i in nc]
    qs = [_dot(q[i] * egc[i], s_old[i]) for i in nc]
    ws = [_dot(sol[i][:, DV_A:], s_old[i]) for i in nc]
    v_new = [sol[i][:, :DV_A] - ws[i] for i in nc]
    o = [qs[i] + _dot(jnp.where(tril, qk[i] * decay[i], 0.0), v_new[i]) for i in nc]
    s_new = [s_old[i] * jnp.exp(gl[i]) + _dot(k[i] * jnp.exp(gl[i] - gc[i]), v_new[i], _TN) for i in nc]
    ms = [jnp.mean(o[i] * o[i], axis=-1, keepdims=True) for i in nc]
    on = [o[i] * lax.rsqrt(ms[i] + EPS) for i in nc]
    for i, (g, h) in enumerate(chains):
        z = xin_ref[g, :, CONV_CH + h * DV_A:CONV_CH + (h + 1) * DV_A]
        mix_ref[g, :, h * DV_A:(h + 1) * DV_A] = on[i] * ng_ref[...] * (_silu(z) if has_state else z)
    for i, (g, h) in enumerate(chains):
        s_ref[g, h] = s_new[i]
        snew_ref[g, h] = s_new[i]


def _gdn(gdn_in, gates, gates_t, cw, ng, *, L, G, state=None):
    n_seq, T, _ = gdn_in.shape
    n_c = T // L
    has_state = state is not None
    data_specs = [
        pl.BlockSpec((G, L, GDN_W), lambda b, c: (b, c, 0)),
        pl.BlockSpec((G, L, LANE), lambda b, c: (b, c, 0)),
        pl.BlockSpec((G, 1, N_GATE, L), lambda b, c: (b, c, 0, 0)),
    ]
    ng_spec = pl.BlockSpec((1, DV_A), lambda b, c: (0, 0))
    mix_spec = pl.BlockSpec((G, L, H_A * DV_A), lambda b, c: (b, c, 0))
    conv_spec = pl.BlockSpec((G, CONV_W - 1, CONV_CH), lambda b, c: (b, 0, 0))
    s_spec = pl.BlockSpec((G, H_A, DK_A, DV_A), lambda b, c: (b, 0, 0, 0))
    mix_shape = jax.ShapeDtypeStruct((n_seq, T, H_A * DV_A), F32)
    conv_shape = jax.ShapeDtypeStruct((n_seq, CONV_W - 1, CONV_CH), F32)
    s_shape = jax.ShapeDtypeStruct((n_seq, H_A, DK_A, DV_A), F32)
    s_scratch = pltpu.VMEM((G, H_A, DK_A, DV_A), F32)
    if has_state:
        conv_st, s0 = state
        in_specs = data_specs + [pl.BlockSpec((SUBLANE, CONV_CH), lambda b, c: (0, 0)), ng_spec, conv_spec, s_spec]
        args = [gdn_in, gates, gates_t, cw, ng, conv_st, s0]
        out_specs, out_shape = [mix_spec, conv_spec, s_spec], [mix_shape, conv_shape, s_shape]
        scratch = [pltpu.VMEM((G, L + SUBLANE, CONV_CH), F32), s_scratch]
    else:
        in_specs = data_specs + [ng_spec]
        args = [gdn_in, gates, gates_t, ng]
        out_specs, out_shape = [mix_spec, s_spec], [mix_shape, s_shape]
        scratch = [s_scratch]
    return pl.pallas_call(
        functools.partial(_gdn_kernel, L=L, G=G, has_state=has_state),
        grid=(n_seq // G, n_c),
        in_specs=in_specs,
        out_specs=out_specs,
        out_shape=out_shape,
        scratch_shapes=scratch,
        compiler_params=pltpu.CompilerParams(dimension_semantics=("parallel", "arbitrary"),
                                             vmem_limit_bytes=VMEM_LIMIT),
        name=f"gdn_L{L}",
    )(*args)


def _mlstm_kernel(*refs, L, G, has_state):
    if has_state:
        (xin_ref, gate_ref, gatet_ref, ng_ref, c0_ref, n0_ref, m0_ref,
         mix_ref, cnew_ref, nnew_ref, mnew_ref, c_ref, n_ref, m_ref) = refs
    else:
        (xin_ref, gate_ref, gatet_ref, ng_ref,
         mix_ref, cnew_ref, nnew_ref, mnew_ref, c_ref, n_ref, m_ref) = refs
    c = pl.program_id(1)

    @pl.when(c == 0)
    def _():
        c_ref[...] = jnp.zeros_like(c_ref)
        n_ref[...] = jnp.zeros_like(n_ref)
        m_ref[...] = jnp.zeros_like(m_ref)
        if has_state:
            c_ref[:, :, 0:DK_B, :] = c0_ref[...]
            n_ref[:, 0:H_B, 0:DK_B] = n0_ref[...]
            m_ref[:, 0:1, 0:H_B] = m0_ref[...]

    tril, _, triu = _chunk_masks(L)
    tril_f, triu_f = tril.astype(F32), triu.astype(F32)

    chains = [(g, h) for g in range(G) for h in range(H_B)]
    nc = range(len(chains))
    c_old = [c_ref[g, h] for g, h in chains]
    n_old = [n_ref[g, h:h + 1, :] for g, h in chains]
    m_old = [m_ref[g, 0:1, h:h + 1] for g, h in chains]

    v0 = 2 * H_B * LANE
    q = [xin_ref[g, :, h * LANE:(h + 1) * LANE] * (DK_B ** -0.5) for g, h in chains]
    k = [xin_ref[g, :, (H_B + h) * LANE:(H_B + h + 1) * LANE] for g, h in chains]
    v = [xin_ref[g, :, v0 + h * DV_B:v0 + (h + 1) * DV_B] for g, h in chains]
    ig_c, b_c, b_last, d_log = [], [], [], []
    for g in range(G):
        gact = gate_ref[g]
        gact_t = gatet_ref[g, 0]
        cum_c = _dot_hi(tril_f, gact)
        cum_r = _dot_hi(gact_t, triu_f)
        for h in range(H_B):
            ig_c.append(gact[:, 8 + h:9 + h])
            b_c.append(cum_c[:, 12 + h:13 + h])
            b_last.append(cum_c[L - 1:L, 12 + h:13 + h])
            d_log.append(jnp.where(tril, cum_c[:, 12 + h:13 + h] - cum_r[12 + h:13 + h, :]
                                   + gact_t[8 + h:9 + h, :], NEG))
    qk = [_dot(q[i], k[i], _NT) for i in nc]
    qc = [_dot(q[i], c_old[i]) for i in nc]
    inter = [b_c[i] + m_old[i] for i in nc]
    m_t = [jnp.maximum(inter[i], jnp.max(d_log[i], axis=-1, keepdims=True)) for i in nc]
    s = [qk[i] * jnp.exp(d_log[i] - m_t[i]) for i in nc]
    e_inter = [jnp.exp(inter[i] - m_t[i]) for i in nc]
    sv = [_dot(s[i], v[i]) for i in nc]
    m_new = [m_t[i][L - 1:L, :] for i in nc]
    kw = [k[i] * jnp.exp(b_last[i] - b_c[i] + ig_c[i] - m_new[i]) for i in nc]
    f_tot = [jnp.exp(b_last[i] + m_old[i] - m_new[i]) for i in nc]
    c_new = [f_tot[i] * c_old[i] + _dot(kw[i], v[i], _TN) for i in nc]
    n_new = [f_tot[i] * n_old[i] + jnp.sum(kw[i], axis=0, keepdims=True) for i in nc]
    qn = [jnp.sum(q[i] * n_old[i], axis=-1, keepdims=True) for i in nc]
    ssum = [jnp.sum(s[i], axis=-1, keepdims=True) for i in nc]
    den = [jnp.maximum(jnp.abs(e_inter[i] * qn[i] + ssum[i]), jnp.exp(-m_t[i])) for i in nc]
    hh = [(e_inter[i] * qc[i] + sv[i]) / den[i] for i in nc]
    ms = [jnp.mean(hh[i] * hh[i], axis=-1, keepdims=True) for i in nc]
    hn = [hh[i] * lax.rsqrt(ms[i] + EPS) for i in nc]
    for i, (g, h) in enumerate(chains):
        og = xin_ref[g, :, v0 + H_B * DV_B + h * DV_B:v0 + H_B * DV_B + (h + 1) * DV_B]
        mix_ref[g, :, h * DV_B:(h + 1) * DV_B] = hn[i] * ng_ref[h:h + 1, :] * _sigmoid(og)
    for i, (g, h) in enumerate(chains):
        c_ref[g, h] = c_new[i]
        n_ref[g, h:h + 1, :] = n_new[i]
        m_ref[g, 0:1, h:h + 1] = m_new[i]
        cnew_ref[g, h] = c_new[i][0:DK_B, :]
        nnew_ref[g, h:h + 1, :] = n_new[i][:, 0:DK_B]
        mnew_ref[g, 0:1, h:h + 1] = m_new[i]


def _mlstm(ml_in, gates, gates_t, ng, *, L, G, state=None):
    n_seq, T, _ = ml_in.shape
    n_c = T // L
    has_state = state is not None
    in_specs = [
        pl.BlockSpec((G, L, MLP_W), lambda b, c: (b, c, 0)),
        pl.BlockSpec((G, L, LANE), lambda b, c: (b, c, 0)),
        pl.BlockSpec((G, 1, N_GATE, L), lambda b, c: (b, c, 0, 0)),
        pl.BlockSpec((H_B, DV_B), lambda b, c: (0, 0)),
    ]
    args = [ml_in, gates, gates_t, ng]
    if has_state:
        c0, n0, m0 = state
        in_specs += [
            pl.BlockSpec((G, H_B, DK_B, DV_B), lambda b, c: (b, 0, 0, 0)),
            pl.BlockSpec((G, H_B, DK_B), lambda b, c: (b, 0, 0)),
            pl.BlockSpec((G, 1, H_B), lambda b, c: (b, 0, 0)),
        ]
        args += [c0, n0, m0]
    return pl.pallas_call(
        functools.partial(_mlstm_kernel, L=L, G=G, has_state=has_state),
        grid=(n_seq // G, n_c),
        in_specs=in_specs,
        out_specs=[
            pl.BlockSpec((G, L, H_B * DV_B), lambda b, c: (b, c, 0)),
            pl.BlockSpec((G, H_B, DK_B, DV_B), lambda b, c: (b, 0, 0, 0)),
            pl.BlockSpec((G, H_B, DK_B), lambda b, c: (b, 0, 0)),
            pl.BlockSpec((G, 1, H_B), lambda b, c: (b, 0, 0)),
        ],
        out_shape=[
            jax.ShapeDtypeStruct((n_seq, T, H_B * DV_B), F32),
            jax.ShapeDtypeStruct((n_seq, H_B, DK_B, DV_B), F32),
            jax.ShapeDtypeStruct((n_seq, H_B, DK_B), F32),
            jax.ShapeDtypeStruct((n_seq, 1, H_B), F32),
        ],
        scratch_shapes=[pltpu.VMEM((G, H_B, LANE, DV_B), F32), pltpu.VMEM((G, SUBLANE, LANE), F32),
                        pltpu.VMEM((G, SUBLANE, LANE), F32)],
        compiler_params=pltpu.CompilerParams(dimension_semantics=("parallel", "arbitrary"),
                                             vmem_limit_bytes=VMEM_LIMIT),
        name=f"mlstm_L{L}",
    )(*args)


def _outproj_kernel(xp_ref, xs_ref, map_ref, mas_ref, mbp_ref, mbs_ref, wo_ref, g_ref, rw_ref, rb_ref,
                    x1_ref, xsort_ref, info_ref, cpad_ref, *, n_p_tiles):
    half = H_A * DV_A
    tm = xp_ref.shape[0]

    def body(seg):
        x_ref, ma_ref, mb_ref = (xp_ref, xs_ref)[seg], (map_ref, mas_ref)[seg], (mbp_ref, mbs_ref)[seg]
        x1 = (x_ref[...] + jnp.dot(ma_ref[...].astype(BF16), wo_ref[:half, :], preferred_element_type=F32)
              + jnp.dot(mb_ref[...].astype(BF16), wo_ref[half:, :], preferred_element_type=F32))
        x1_ref[...] = x1
        hn = _rms(x1, g_ref[...])
        hn_hi = hn.astype(BF16)
        hn_lo = (hn - hn_hi.astype(F32)).astype(BF16)
        logits = (jnp.dot(hn_hi, rw_ref[0], preferred_element_type=F32)
                  + jnp.dot(hn_hi, rw_ref[1], preferred_element_type=F32)
                  + jnp.dot(hn_lo, rw_ref[0], preferred_element_type=F32)) + rb_ref[...]

        vals = logits.T[:N_EXPERTS, :]
        e_iota = lax.broadcasted_iota(jnp.int32, (N_EXPERTS, tm), 0)
        sels, tops = [], []
        for _ in range(TOP_K):
            m = jnp.max(vals, axis=0, keepdims=True)
            first = jnp.min(jnp.where(vals == m, e_iota, N_EXPERTS), axis=0, keepdims=True)
            sel = e_iota == first
            vals = jnp.where(sel, -jnp.inf, vals)
            sels.append(sel)
            tops.append(m)
        ex = [jnp.exp(t - tops[0]) for t in tops]
        den = ex[0] + ex[1] + ex[2] + ex[3]
        gates = [e / den for e in ex]
        mask = sels[0].astype(F32) + sels[1].astype(F32) + sels[2].astype(F32) + sels[3].astype(F32)
        ri = lax.broadcasted_iota(jnp.int32, (tm, tm), 0)
        ci = lax.broadcasted_iota(jnp.int32, (tm, tm), 1)
        rank = _dot(mask, (ri < ci).astype(F32))
        cnt = jnp.sum(mask, axis=1, keepdims=True)
        cpad = jnp.ceil(cnt * (1.0 / SEG_ALIGN)) * SEG_ALIGN
        cpad_b = jnp.broadcast_to(cpad, (N_EXPERTS, tm))
        er = lax.broadcasted_iota(jnp.int32, (N_EXPERTS, N_EXPERTS), 0)
        ec = lax.broadcasted_iota(jnp.int32, (N_EXPERTS, N_EXPERTS), 1)
        seg_off = _dot((er > ec).astype(F32), cpad_b)
        pos = seg_off + rank
        q = [jnp.sum(jnp.where(s, pos, 0.0), axis=0, keepdims=True) for s in sels]

        j_iota = lax.broadcasted_iota(jnp.int32, (MOE_CAP, tm), 0).astype(F32)
        perm = jnp.zeros((MOE_CAP, tm), F32)
        for kk in range(TOP_K):
            perm = jnp.where(j_iota == q[kk], 1.0, perm)
        xsorted = _dot(perm, hn)
        xsort_ref[...] = xsorted

        r_iota = lax.broadcasted_iota(jnp.int32, (LANE, tm), 0)
        info = jnp.zeros((LANE, tm), F32)
        for kk in range(TOP_K):
            info = jnp.where(r_iota == kk, q[kk], info)
            info = jnp.where(r_iota == TOP_K + kk, gates[kk], info)
        info_ref[...] = info.T
        cpad_ref[0] = cpad_b[:, :LANE]

    _for_segment(n_p_tiles, body)


def _outproj(xp, xs, ma_p, ma_s, mb_p, mb_s, w_out, g, rw, rb, tm):
    n_p, n_s = xp.shape[0], xs.shape[0]
    n = n_p + n_s
    nt = n // tm
    npt = n_p // tm
    half = H_A * DV_A
    return pl.pallas_call(
        functools.partial(_outproj_kernel, n_p_tiles=npt),
        grid=(nt,),
        in_specs=_two_segment_specs(tm, D_MODEL, npt) + _two_segment_specs(tm, half, npt)
        + _two_segment_specs(tm, half, npt) + [
            pl.BlockSpec((D_MODEL, D_MODEL), lambda i: (0, 0)),
            pl.BlockSpec((1, D_MODEL), lambda i: (0, 0)),
            pl.BlockSpec((2, D_MODEL, LANE), lambda i: (0, 0, 0)),
            pl.BlockSpec((1, LANE), lambda i: (0, 0)),
        ],
        out_specs=[
            pl.BlockSpec((tm, D_MODEL), lambda i: (i, 0)),
            pl.BlockSpec((MOE_CAP, D_MODEL), lambda i: (i, 0)),
            pl.BlockSpec((tm, LANE), lambda i: (i, 0)),
            pl.BlockSpec((1, N_EXPERTS, LANE), lambda i: (i, 0, 0)),
        ],
        out_shape=[
            jax.ShapeDtypeStruct((n, D_MODEL), F32),
            jax.ShapeDtypeStruct((nt * MOE_CAP, D_MODEL), F32),
            jax.ShapeDtypeStruct((n, LANE), F32),
            jax.ShapeDtypeStruct((nt, N_EXPERTS, LANE), F32),
        ],
        compiler_params=pltpu.CompilerParams(dimension_semantics=("arbitrary",), vmem_limit_bytes=VMEM_LIMIT),
        name="outproj",
    )(xp, xs, ma_p, ma_s, mb_p, mb_s, w_out, g, rw, rb)


def _expert_kernel(be_ref, bj_ref, tf_ref, tl_ref, cov_ref, nu_ref, vt_ref, ct_ref, lt_ref, nx_ref, ws_ref,
                   xs_hbm, wgu_hbm, bgu_ref, wd_hbm, bd_ref, ys_hbm,
                   xbuf, ybuf, gsem, ssem, wgu_st, wd_st, wsem, wgu_bf, wd_bf, *, nt):
    b = pl.program_id(0)
    n_used = nu_ref[0]
    slot = b % 2

    def start_pieces(bb, copy, s):
        e = be_ref[bb]
        base = bj_ref[bb] * MOE_BLK

        def body(t, carry):
            k = e * nt + t
            lo = jnp.maximum(vt_ref[k], base)
            ln = jnp.minimum(ct_ref[k], base + MOE_BLK) - lo

            @pl.when(ln > 0)
            def _():
                copy(s, pl.multiple_of(lt_ref[k] + lo, SEG_ALIGN), pl.multiple_of(lo - base, SEG_ALIGN),
                     pl.multiple_of(ln, SEG_ALIGN)).start()
            return carry

        lax.fori_loop(tf_ref[bb], tl_ref[bb] + 1, body, 0)

    def weight_copies(e):
        return (pltpu.make_async_copy(wgu_hbm.at[e], wgu_st, wsem.at[0]),
                pltpu.make_async_copy(wd_hbm.at[e], wd_st, wsem.at[1]))

    def cast_weights(p):
        wgu_bf[p] = wgu_st[...].astype(BF16)
        wd_bf[p] = wd_st[...].astype(BF16)

    def gather_copy(s, src, dst, size):
        return pltpu.make_async_copy(xs_hbm.at[pl.ds(src, size)], xbuf.at[s, pl.ds(dst, size)], gsem.at[s])

    def scatter_copy(s, src, dst, size):
        return pltpu.make_async_copy(ybuf.at[s, pl.ds(dst, size)], ys_hbm.at[pl.ds(src, size)], ssem.at[s])

    def wait_rows(count, copy, s):
        @pl.when(count > 0)
        def _():
            copy(s, 0, 0, pl.multiple_of(count, SEG_ALIGN)).wait()

    @pl.when(b == 0)
    def _():
        xbuf[...] = jnp.zeros_like(xbuf)
        start_pieces(0, gather_copy, 0)

    @pl.when(b + 1 < n_used)
    def _():
        start_pieces(b + 1, gather_copy, 1 - slot)

    @pl.when(b < n_used)
    def _():
        e = be_ref[b]
        first = jnp.logical_or(b == 0, be_ref[jnp.maximum(b - 1, 0)] != e)
        last = jnp.logical_or(b == n_used - 1, be_ref[jnp.minimum(b + 1, n_used - 1)] != e)
        has_next = nx_ref[b] < N_EXPERTS
        p = ws_ref[b]

        @pl.when(b == 0)
        def _():
            for cp in weight_copies(e):
                cp.start()
            for cp in weight_copies(e):
                cp.wait()
            cast_weights(p)

        @pl.when(jnp.logical_and(first, has_next))
        def _():
            for cp in weight_copies(nx_ref[b]):
                cp.start()

        wait_rows(cov_ref[b], gather_copy, slot)
        hgu = jnp.dot(xbuf[slot].astype(BF16), wgu_bf[p], preferred_element_type=F32) + bgu_ref[0]
        gate = jnp.minimum(hgu[:, :D_FF], SWIGLU_LIMIT)
        up = jnp.clip(hgu[:, D_FF:], -SWIGLU_LIMIT, SWIGLU_LIMIT)
        act = (up + 1.0) * gate * _sigmoid(SWIGLU_ALPHA * gate)
        y = jnp.dot(act.astype(BF16), wd_bf[p], preferred_element_type=F32) + bd_ref[0]

        @pl.when(jnp.logical_and(last, has_next))
        def _():
            for cp in weight_copies(nx_ref[b]):
                cp.wait()
            cast_weights(1 - p)

        @pl.when(b >= 2)
        def _():
            wait_rows(cov_ref[jnp.maximum(b - 2, 0)], scatter_copy, slot)

        ybuf[slot] = y
        start_pieces(b, scatter_copy, slot)

        @pl.when(b == n_used - 1)
        def _():
            wait_rows(cov_ref[b], scatter_copy, slot)
            wait_rows(jnp.where(b >= 1, cov_ref[jnp.maximum(b - 1, 0)], 0), scatter_copy, 1 - slot)


def _experts(tables, xs, w_gu, b_gu, w_down, b_down, nt):
    nb = tables[0].shape[0]

    def bias_blk(b, *t):
        return (t[0][jnp.minimum(b, t[5][0] - 1)], 0, 0)

    grid_spec = pltpu.PrefetchScalarGridSpec(
        num_scalar_prefetch=len(tables),
        grid=(nb,),
        in_specs=[
            pl.BlockSpec(memory_space=pl.ANY),
            pl.BlockSpec(memory_space=pl.ANY),
            pl.BlockSpec((1, 1, 2 * D_FF), bias_blk),
            pl.BlockSpec(memory_space=pl.ANY),
            pl.BlockSpec((1, 1, D_MODEL), bias_blk),
        ],
        out_specs=pl.BlockSpec(memory_space=pl.ANY),
        scratch_shapes=[
            pltpu.VMEM((2, MOE_BLK, D_MODEL), F32),
            pltpu.VMEM((2, MOE_BLK, D_MODEL), F32),
            pltpu.SemaphoreType.DMA((2,)),
            pltpu.SemaphoreType.DMA((2,)),
            pltpu.VMEM((D_MODEL, 2 * D_FF), F32),
            pltpu.VMEM((D_FF, D_MODEL), F32),
            pltpu.SemaphoreType.DMA((2,)),
            pltpu.VMEM((2, D_MODEL, 2 * D_FF), BF16),
            pltpu.VMEM((2, D_FF, D_MODEL), BF16),
        ],
    )
    return pl.pallas_call(
        functools.partial(_expert_kernel, nt=nt),
        grid_spec=grid_spec,
        out_shape=jax.ShapeDtypeStruct(xs.shape, xs.dtype),
        input_output_aliases={len(tables): 0},
        compiler_params=pltpu.CompilerParams(dimension_semantics=("arbitrary",), vmem_limit_bytes=VMEM_LIMIT),
        name="experts",
    )(*tables, xs, w_gu, b_gu, w_down, b_down)


def _combine_kernel(ys_ref, info_ref, x1_ref, pp_ref, ps_ref, gple_ref, wg_ref, wp_ref, gfin_ref,
                    outp_ref, outs_ref, *, n_p_tiles):
    tm = x1_ref.shape[0]

    def body(seg):
        p_ref, out_ref = (pp_ref, ps_ref)[seg], (outp_ref, outs_ref)[seg]
        info = info_ref[...]
        j_iota = lax.broadcasted_iota(jnp.int32, (tm, MOE_CAP), 1).astype(F32)
        gmat = jnp.zeros((tm, MOE_CAP), F32)
        for kk in range(TOP_K):
            gmat = jnp.where(j_iota == info[:, kk:kk + 1], info[:, TOP_K + kk:TOP_K + kk + 1], gmat)
        x2 = x1_ref[...] + jnp.dot(gmat.astype(BF16), ys_ref[...].astype(BF16), preferred_element_type=F32)
        hn = _rms(x2, gple_ref[...]).astype(BF16)
        gate = _sigmoid(jnp.dot(hn, wg_ref[...], preferred_element_type=F32))
        pe = jnp.dot(p_ref[...].astype(BF16), wp_ref[...], preferred_element_type=F32)
        x3 = x2 + gate * pe
        out_ref[...] = _rms(x3, gfin_ref[...])

    _for_segment(n_p_tiles, body)


def _combine(ys, info, x1, pp, ps, g_ple, w_gate, w_p, g_fin, tm):
    n_p, n_s = pp.shape[0], ps.shape[0]
    n = n_p + n_s
    nt = n // tm
    npt = n_p // tm
    return pl.pallas_call(
        functools.partial(_combine_kernel, n_p_tiles=npt),
        grid=(nt,),
        in_specs=[
            pl.BlockSpec((MOE_CAP, D_MODEL), lambda i: (i, 0)),
            pl.BlockSpec((tm, LANE), lambda i: (i, 0)),
            pl.BlockSpec((tm, D_MODEL), lambda i: (i, 0)),
        ] + _two_segment_specs(tm, PLE_DIM, npt) + [
            pl.BlockSpec((1, D_MODEL), lambda i: (0, 0)),
            pl.BlockSpec((D_MODEL, D_MODEL), lambda i: (0, 0)),
            pl.BlockSpec((PLE_DIM, D_MODEL), lambda i: (0, 0)),
            pl.BlockSpec((1, D_MODEL), lambda i: (0, 0)),
        ],
        out_specs=_two_segment_specs(tm, D_MODEL, npt),
        out_shape=[jax.ShapeDtypeStruct((n_p, D_MODEL), F32), jax.ShapeDtypeStruct((n_s, D_MODEL), F32)],
        compiler_params=pltpu.CompilerParams(dimension_semantics=("arbitrary",), vmem_limit_bytes=VMEM_LIMIT),
        name="combine",
    )(ys, info, x1, pp, ps, g_ple, w_gate, w_p, g_fin)


def _block_tables(seg_len, nb):
    nt = seg_len.shape[0]
    seg_off = jnp.cumsum(seg_len, axis=1) - seg_len
    seg_end = jnp.cumsum(seg_len, axis=0).T
    seg_start = seg_end - seg_len.T
    n_rows = seg_end[:, -1]
    n_blk = (n_rows + MOE_BLK - 1) // MOE_BLK
    blk_end = jnp.cumsum(n_blk)
    b = jnp.arange(nb, dtype=jnp.int32)
    block_e = jnp.minimum(jnp.sum((blk_end[None, :] <= b[:, None]).astype(jnp.int32), axis=1), N_EXPERTS - 1)
    idx = jnp.where(n_blk > 0, jnp.arange(N_EXPERTS, dtype=jnp.int32), N_EXPERTS)
    nxt = jnp.concatenate([lax.cummin(idx, axis=0, reverse=True)[1:], jnp.full((1,), N_EXPERTS, jnp.int32)])
    parity = (jnp.cumsum((n_blk > 0).astype(jnp.int32)) - 1) % 2
    per_e = jnp.concatenate([jnp.stack([blk_end - n_blk, n_rows, nxt, parity], axis=1), seg_start, seg_end],
                            axis=1).astype(F32)
    onehot = (block_e[:, None] == jnp.arange(N_EXPERTS, dtype=jnp.int32)[None, :]).astype(F32)
    per_b = jnp.dot(onehot, per_e, precision=HI).astype(jnp.int32)
    block_j = b - per_b[:, 0]
    base = block_j * MOE_BLK
    t_first = jnp.sum((per_b[:, 4 + nt:] <= base[:, None]).astype(jnp.int32), axis=1)
    t_last = jnp.sum((per_b[:, 4:4 + nt] < (base + MOE_BLK)[:, None]).astype(jnp.int32), axis=1) - 1
    cover = jnp.clip(per_b[:, 1] - base, 0, MOE_BLK)
    seg_shift = (jnp.arange(nt, dtype=jnp.int32)[:, None] * MOE_CAP + seg_off).T - seg_start
    tables = (block_e, block_j, t_first, t_last, cover, blk_end[-1:], seg_start.reshape(-1),
              seg_end.reshape(-1), seg_shift.reshape(-1), per_b[:, 2], per_b[:, 3])
    return tuple(t.astype(jnp.int32) for t in tables)


def _rearranged_in_weights(w_in):
    o = np.cumsum([0, CONV_CH, H_A * DV_A, H_A, H_A, H_B * DK_B, H_B * DK_B, H_B * DV_B, H_B * DV_B, H_B, H_B])
    conv_in, z_a, a_a, b_a, q_b, k_b, v_b, o_b, i_b, f_b = (w_in[:, int(o[j]):int(o[j + 1])] for j in range(10))
    zpad = jnp.zeros((D_MODEL, LANE - DK_B), w_in.dtype)

    def pad_heads(w):
        return jnp.concatenate([jnp.concatenate([w[:, h * DK_B:(h + 1) * DK_B], zpad], axis=1) for h in range(H_B)],
                               axis=1)

    small = jnp.concatenate([a_a, b_a, i_b, f_b], axis=1)
    w_gdn = w_in[:, :GDN_W]
    w_rest = jnp.concatenate([pad_heads(q_b), pad_heads(k_b), v_b, o_b,
                              small, jnp.zeros((D_MODEL, LANE - N_GATE), w_in.dtype)], axis=1)
    return w_gdn.astype(BF16), w_rest.astype(BF16), small.T.astype(BF16)


def _gate_params(a_log, dt_bias, i_bias, f_bias):
    z4 = jnp.zeros((4,), F32)
    alog = jnp.concatenate([a_log.astype(F32), z4, z4, z4])
    bias = jnp.concatenate([dt_bias.astype(F32), z4, i_bias.astype(F32), f_bias.astype(F32)])
    pad = jnp.zeros((LANE - N_GATE,), F32)
    pcol = jnp.zeros((SUBLANE, LANE), F32).at[0].set(jnp.concatenate([alog, pad])).at[1].set(
        jnp.concatenate([bias, pad]))
    prow = jnp.zeros((N_GATE, LANE), F32).at[:, 0].set(alog).at[:, 1].set(bias)
    return pcol, prow


def kernel(x_prompt, x_sample, p_prompt, p_sample, state_conv, state_gdn, state_mlstm_c, state_mlstm_n, state_mlstm_m, norm_attn_g, w_in, conv_w, gdn_a_log, gdn_dt_bias, gdn_norm_g, mlstm_i_bias, mlstm_f_bias, mlstm_norm_g, w_out, norm_moe_g, router_w, router_b, expert_w_gu, expert_b_gu, expert_w_down, expert_b_down, norm_ple_g, ple_gate_w, ple_w, final_norm_g):
    bp, tp, _ = x_prompt.shape
    bs, ts, _ = x_sample.shape
    n_p, n_s = bp * tp, bs * ts
    n = n_p + n_s
    lp, ls = min(tp, CHUNK), min(ts, CHUNK)
    tm = 256
    gp_a = 4 if bp % 4 == 0 else 1
    gp_b = 2 if bp % 2 == 0 else 1
    gs = 8 if bs % 8 == 0 else 1
    assert tp % lp == 0 and ts % ls == 0 and tp % tm == 0 and n_s % tm == 0 and ls % SUBLANE == 0

    xp = x_prompt.reshape(n_p, D_MODEL)
    xs = x_sample.reshape(n_s, D_MODEL)

    w_gdn, w_rest, ws_t = _rearranged_in_weights(w_in[0])
    pcol, prow = _gate_params(gdn_a_log[0], gdn_dt_bias[0], mlstm_i_bias[0], mlstm_f_bias[0])
    cw = jnp.zeros((SUBLANE, CONV_CH), F32).at[:CONV_W].set(conv_w[0].astype(F32))
    gdn_p, gdn_s, ml_p, ml_s, gate_p, gate_s, gatet_p, gatet_s, conv_p = _inproj(
        xp, xs, norm_attn_g[0].reshape(1, D_MODEL), w_gdn, w_rest, ws_t, pcol, prow, cw, tm, bp)
    gt_p = gatet_p.reshape(N_GATE, bp, tp // lp, lp).transpose(1, 2, 0, 3)
    gt_s = gatet_s.reshape(N_GATE, bs, ts // ls, ls).transpose(1, 2, 0, 3)

    ng_a = gdn_norm_g[0].reshape(1, DV_A).astype(F32)
    ng_b = mlstm_norm_g[0].reshape(H_B, DV_B).astype(F32)
    ma_p, gdn_st_p = _gdn(gdn_p.reshape(bp, tp, GDN_W), gate_p.reshape(bp, tp, LANE), gt_p, cw, ng_a,
                          L=lp, G=gp_a)
    ma_s, conv_s, gdn_st_s = _gdn(gdn_s.reshape(bs, ts, GDN_W), gate_s.reshape(bs, ts, LANE), gt_s, cw, ng_a,
                                  L=ls, G=gs, state=(state_conv[0], state_gdn[0]))
    mb_p, c_p, nn_p, m_p = _mlstm(ml_p.reshape(bp, tp, MLP_W), gate_p.reshape(bp, tp, LANE), gt_p, ng_b,
                                  L=lp, G=gp_b)
    mb_s, c_s, nn_s, m_s = _mlstm(ml_s.reshape(bs, ts, MLP_W), gate_s.reshape(bs, ts, LANE), gt_s, ng_b,
                                  L=ls, G=gs,
                                  state=(state_mlstm_c[0], state_mlstm_n[0], state_mlstm_m[0].reshape(bs, 1, H_B)))
    half = H_A * DV_A

    rw = jnp.zeros((D_MODEL, LANE), F32).at[:, :N_EXPERTS].set(router_w[0])
    rw_hi = rw.astype(BF16)
    rw = jnp.stack([rw_hi, (rw - rw_hi.astype(F32)).astype(BF16)])
    rb = jnp.full((1, LANE), NEG, F32).at[0, :N_EXPERTS].set(router_b[0])
    x1, x_sorted, info, seg_len = _outproj(xp, xs, ma_p.reshape(n_p, half), ma_s.reshape(n_s, half),
                                           mb_p.reshape(n_p, half), mb_s.reshape(n_s, half),
                                           w_out[0].astype(BF16), norm_moe_g[0].reshape(1, D_MODEL), rw, rb, MOE_TM)

    nt = n // MOE_TM
    nb = -(-(n * TOP_K + nt * N_EXPERTS * (SEG_ALIGN - 1)) // MOE_BLK) + N_EXPERTS
    tables = _block_tables(seg_len[:, :, 0].astype(jnp.int32), nb)
    y_sorted = _experts(tables, x_sorted, expert_w_gu[0], expert_b_gu[0].reshape(N_EXPERTS, 1, 2 * D_FF),
                        expert_w_down[0], expert_b_down[0].reshape(N_EXPERTS, 1, D_MODEL), nt)
    y_p, y_s = _combine(y_sorted, info, x1, p_prompt[0].reshape(n_p, PLE_DIM),
                        p_sample[0].reshape(n_s, PLE_DIM), norm_ple_g[0].reshape(1, D_MODEL),
                        ple_gate_w[0].astype(BF16), ple_w[0].astype(BF16), final_norm_g.reshape(1, D_MODEL), MOE_TM)

    return (y_p.reshape(bp, tp, D_MODEL), y_s.reshape(bs, ts, D_MODEL),
            conv_p[None], gdn_st_p[None], c_p[None], nn_p[None], m_p.reshape(1, bp, H_B),
            conv_s[None], gdn_st_s[None], c_s[None], nn_s[None], m_s.reshape(1, bs, H_B))
```

```python
import functools

import numpy as np
import jax
import jax.numpy as jnp
from jax import lax
from jax.experimental import pallas as pl
from jax.experimental.pallas import tpu as pltpu

F32 = jnp.float32
BF16 = jnp.bfloat16

D_MODEL = 1024
H_A, DK_A, DV_A = 4, 128, 128
H_B, DK_B, DV_B = 4, 64, 128
CONV_W = 4
CONV_CH = H_A * (2 * DK_A + DV_A)
N_EXPERTS = 32
TOP_K = 4
D_FF = 1024
SWIGLU_LIMIT = 7.0
SWIGLU_ALPHA = 1.702
PLE_DIM = 256
EPS = 1e-6
NEG = -1e30
CHUNK = 64

LANE = 128
SUBLANE = 8
GDN_W = CONV_CH + H_A * DV_A
MLP_W = 2 * H_B * LANE + 2 * H_B * DV_B
N_GATE = 16

VMEM_LIMIT = 48 * 1024 * 1024

MOE_TM = 256
MOE_BLK = 256
SEG_ALIGN = SUBLANE
MOE_CAP = -(-(MOE_TM * TOP_K + N_EXPERTS * (SEG_ALIGN - 1)) // LANE) * LANE

HI = lax.Precision.HIGHEST

_NN = (((1,), (0,)), ((), ()))
_NT = (((1,), (1,)), ((), ()))
_TN = (((0,), (0,)), ((), ()))


def _dot(a, b, dims=_NN):
    return lax.dot_general(a.astype(BF16), b.astype(BF16), dims, preferred_element_type=F32)


def _dot_hi(a, b, dims=_NN):
    return lax.dot_general(a, b, dims, precision=HI, preferred_element_type=F32)


def _rms(x, g):
    return x * lax.rsqrt(jnp.mean(x * x, axis=-1, keepdims=True) + EPS) * g


def _softplus(t):
    return jnp.maximum(t, 0.0) + jnp.log1p(jnp.exp(-jnp.abs(t)))


def _sigmoid(t):
    return 1.0 / (1.0 + jnp.exp(-t))


def _silu(t):
    return t * _sigmoid(t)


def _activate_gates(raw, idx, alog, bias):
    t = raw + bias
    g = -jnp.exp(alog) * _softplus(t)
    beta = _sigmoid(t)
    lf = -_softplus(-t)
    return jnp.where(idx < 4, g, jnp.where(idx < 8, beta, jnp.where(idx < 12, t, lf)))


def _two_segment_specs(tm, width, n_p_tiles):
    return [pl.BlockSpec((tm, width), lambda i: (jnp.minimum(i, n_p_tiles - 1), 0)),
            pl.BlockSpec((tm, width), lambda i: (jnp.maximum(i - n_p_tiles, 0), 0))]


def _for_segment(n_p_tiles, body):
    i = pl.program_id(0)

    @pl.when(i < n_p_tiles)
    def _():
        body(0)

    @pl.when(i >= n_p_tiles)
    def _():
        body(1)


def _gdn_preactivate(raw, xc_ref, cw_ref, first_of_seq):
    tm = raw.shape[0]
    xc_ref[0:SUBLANE, :] = jnp.where(first_of_seq, 0.0, xc_ref[tm:tm + SUBLANE, :])
    xc_ref[SUBLANE:SUBLANE + tm, :] = raw[:, :CONV_CH]
    base = SUBLANE - (CONV_W - 1)
    conv = xc_ref[base:base + tm, :] * cw_ref[0:1, :]
    for j in range(1, CONV_W):
        conv = conv + xc_ref[base + j:base + j + tm, :] * cw_ref[j:j + 1, :]
    act = _silu(conv)
    parts = []
    for h in range(H_A):
        qh = act[:, h * DK_A:(h + 1) * DK_A]
        parts.append(qh * (lax.rsqrt(jnp.sum(qh * qh, axis=-1, keepdims=True) + EPS) * (DK_A ** -0.5)))
    for h in range(H_A):
        kh = act[:, H_A * DK_A + h * DK_A:H_A * DK_A + (h + 1) * DK_A]
        parts.append(kh * lax.rsqrt(jnp.sum(kh * kh, axis=-1, keepdims=True) + EPS))
    parts.append(act[:, 2 * H_A * DK_A:])
    parts.append(_silu(raw[:, CONV_CH:]))
    new_conv = xc_ref[SUBLANE + tm - (CONV_W - 1):SUBLANE + tm, :]
    return jnp.concatenate(parts, axis=-1), new_conv


def _inproj_kernel(xp_ref, xs_ref, g_ref, wa_ref, wb_ref, wst_ref, pc_ref, pr_ref, cw_ref,
                   gdnp_ref, gdns_ref, mlp_ref, mls_ref, gatep_ref, gates_ref, gatetp_ref, gatets_ref, cnew_ref,
                   xc_ref, raw_ref, *, n_p_tiles, tiles_per_seq):
    tm = xp_ref.shape[0]
    i = pl.program_id(0)

    @pl.when(i == 0)
    def _():
        xc_ref[...] = jnp.zeros_like(xc_ref)
        raw_ref[...] = jnp.zeros_like(raw_ref)

    def preactivate_previous_tile():
        pre, new_conv = _gdn_preactivate(raw_ref[...], xc_ref, cw_ref, (i - 1) % tiles_per_seq == 0)
        gdnp_ref[...] = pre
        cnew_ref[0] = new_conv

    def body(seg):
        x_ref = (xp_ref, xs_ref)[seg]
        ml_ref = (mlp_ref, mls_ref)[seg]
        gate_ref, gatet_ref = (gatep_ref, gates_ref)[seg], (gatetp_ref, gatets_ref)[seg]
        if seg == 0:
            preactivate_previous_tile()
        else:
            pl.when(i == n_p_tiles)(preactivate_previous_tile)
        hn = _rms(x_ref[...], g_ref[...]).astype(BF16)
        raw_gdn = jnp.dot(hn, wa_ref[...], preferred_element_type=F32)
        if seg == 0:
            raw_ref[...] = raw_gdn
        else:
            gdns_ref[...] = raw_gdn
        ml_ref[...] = jnp.dot(hn, wb_ref[:, :MLP_W], preferred_element_type=F32)
        raw = jnp.dot(hn, wb_ref[:, MLP_W:], preferred_element_type=F32)
        lane = lax.broadcasted_iota(jnp.int32, (tm, LANE), 1)
        gate_ref[...] = _activate_gates(raw, lane, pc_ref[0:1, :], pc_ref[1:2, :])
        raw_t = lax.dot_general(wst_ref[...], hn, _NT, preferred_element_type=F32)
        row = lax.broadcasted_iota(jnp.int32, (N_GATE, tm), 0)
        gatet_ref[...] = _activate_gates(raw_t, row, pr_ref[:, 0:1], pr_ref[:, 1:2])

    _for_segment(n_p_tiles, body)


def _inproj(xp, xs, g, w_gdn, w_rest, ws_t, pcol, prow, cw, tm, n_seq_p):
    n_p, n_s = xp.shape[0], xs.shape[0]
    npt = n_p // tm
    tiles_per_seq = npt // n_seq_p

    def out2(width):
        return _two_segment_specs(tm, width, npt)

    def shp2(width):
        return [jax.ShapeDtypeStruct((n_p, width), F32), jax.ShapeDtypeStruct((n_s, width), F32)]

    def prev_tile(i):
        return jnp.clip(i - 1, 0, npt - 1)

    return pl.pallas_call(
        functools.partial(_inproj_kernel, n_p_tiles=npt, tiles_per_seq=tiles_per_seq),
        grid=((n_p + n_s) // tm,),
        in_specs=_two_segment_specs(tm, D_MODEL, npt) + [
            pl.BlockSpec((1, D_MODEL), lambda i: (0, 0)),
            pl.BlockSpec((D_MODEL, GDN_W), lambda i: (0, 0)),
            pl.BlockSpec((D_MODEL, MLP_W + LANE), lambda i: (0, 0)),
            pl.BlockSpec((N_GATE, D_MODEL), lambda i: (0, 0)),
            pl.BlockSpec((SUBLANE, LANE), lambda i: (0, 0)),
            pl.BlockSpec((N_GATE, LANE), lambda i: (0, 0)),
            pl.BlockSpec((SUBLANE, CONV_CH), lambda i: (0, 0)),
        ],
        out_specs=[
            pl.BlockSpec((tm, GDN_W), lambda i: (prev_tile(i), 0)),
            pl.BlockSpec((tm, GDN_W), lambda i: (jnp.maximum(i - npt, 0), 0)),
        ] + out2(MLP_W) + out2(LANE) + [
            pl.BlockSpec((N_GATE, tm), lambda i: (0, jnp.minimum(i, npt - 1))),
            pl.BlockSpec((N_GATE, tm), lambda i: (0, jnp.maximum(i - npt, 0))),
            pl.BlockSpec((1, CONV_W - 1, CONV_CH), lambda i: (prev_tile(i) // tiles_per_seq, 0, 0)),
        ],
        out_shape=shp2(GDN_W) + shp2(MLP_W) + shp2(LANE) + [
            jax.ShapeDtypeStruct((N_GATE, n_p), F32), jax.ShapeDtypeStruct((N_GATE, n_s), F32),
            jax.ShapeDtypeStruct((n_seq_p, CONV_W - 1, CONV_CH), F32)],
        scratch_shapes=[pltpu.VMEM((tm + SUBLANE, CONV_CH), F32), pltpu.VMEM((tm, GDN_W), F32)],
        compiler_params=pltpu.CompilerParams(dimension_semantics=("arbitrary",), vmem_limit_bytes=VMEM_LIMIT),
        name="inproj",
    )(xp, xs, g, w_gdn, w_rest, ws_t, pcol, prow, cw)


def _chunk_masks(L):
    ri = lax.broadcasted_iota(jnp.int32, (L, L), 0)
    ci = lax.broadcasted_iota(jnp.int32, (L, L), 1)
    return ri >= ci, ri > ci, ri <= ci


def _run_interleaved(*stage_generators):
    live = list(stage_generators)
    while live:
        for gen in list(live):
            if next(gen, StopIteration) is StopIteration:
                live.remove(gen)


def _gdn_kernel(*refs, L, G, has_state):
    _run_interleaved(_gdn_stages(*refs, L=L, G=G, has_state=has_state))


def _gdn_stages(*refs, L, G, has_state):
    if has_state:
        (xin_ref, gate_ref, gatet_ref, cw_ref, ng_ref, cst_ref, s0_ref,
         mix_ref, cnew_ref, snew_ref, xc_ref, s_ref) = refs
    else:
        xin_ref, gate_ref, gatet_ref, ng_ref, mix_ref, snew_ref, s_ref = refs
    c = pl.program_id(1)

    @pl.when(c == 0)
    def _():
        if has_state:
            xc_ref[:, 0:SUBLANE, :] = jnp.zeros((G, SUBLANE, CONV_CH), F32)
            xc_ref[:, SUBLANE - (CONV_W - 1):SUBLANE, :] = cst_ref[...]
            s_ref[...] = s0_ref[...]
        else:
            s_ref[...] = jnp.zeros_like(s_ref)

    if has_state:
        @pl.when(c > 0)
        def _():
            xc_ref[:, 0:SUBLANE, :] = xc_ref[:, L:L + SUBLANE, :]

    yield
    tril, strict, triu = _chunk_masks(L)
    tril_f, triu_f = tril.astype(F32), triu.astype(F32)
    base = SUBLANE - (CONV_W - 1)

    chains = [(g, h) for g in range(G) for h in range(H_A)]
    s_old = [s_ref[g, h] for g, h in chains]
    if has_state:
        for g in range(G):
            xc_ref[g, SUBLANE:SUBLANE + L, :] = xin_ref[g, :, :CONV_CH]

    q, k, v, beta, gc, gl, decay = [], [], [], [], [], [], []
    for g in range(G):
        if has_state:
            conv = xc_ref[g, base:base + L, :] * cw_ref[0:1, :]
            for j in range(1, CONV_W):
                conv = conv + xc_ref[g, base + j:base + j + L, :] * cw_ref[j:j + 1, :]
            cnew_ref[g] = xc_ref[g, SUBLANE + L - (CONV_W - 1):SUBLANE + L, :]
            act = _silu(conv)
        else:
            act = xin_ref[g, :, :CONV_CH]
        gact = gate_ref[g]
        cum_c = _dot_hi(tril_f, gact)
        cum_r = _dot_hi(gatet_ref[g, 0], triu_f)
        for h in range(H_A):
            q.append(act[:, h * DK_A:(h + 1) * DK_A])
            k.append(act[:, H_A * DK_A + h * DK_A:H_A * DK_A + (h + 1) * DK_A])
            v.append(act[:, 2 * H_A * DK_A + h * DV_A:2 * H_A * DK_A + (h + 1) * DV_A])
            beta.append(gact[:, 4 + h:5 + h])
            gc.append(cum_c[:, h:h + 1])
            gl.append(cum_c[L - 1:L, h:h + 1])
            gr = cum_r[h:h + 1, :]
            decay.append(jnp.where(tril, jnp.exp(jnp.where(tril, cum_c[:, h:h + 1] - gr, 0.0)), 0.0))
        yield

    nc = range(len(chains))
    if has_state:
        qss = [jnp.sum(q[i] * q[i], axis=-1, keepdims=True) for i in nc]
        kss = [jnp.sum(k[i] * k[i], axis=-1, keepdims=True) for i in nc]
        q = [q[i] * (lax.rsqrt(qss[i] + EPS) * (DK_A ** -0.5)) for i in nc]
        k = [k[i] * lax.rsqrt(kss[i] + EPS) for i in nc]
    kb = [k[i] * beta[i] for i in nc]
    egc = [jnp.exp(gc[i]) for i in nc]
    yield
    kk = [_dot(kb[i], k[i], _NT) for i in nc]
    yield
    qk = [_dot(q[i], k[i], _NT) for i in nc]
    yield
    eye = (lax.broadcasted_iota(jnp.int32, (L, L), 0) == lax.broadcasted_iota(jnp.int32, (L, L), 1)).astype(F32)
    pw = [-jnp.where(strict, kk[i] * decay[i], 0.0) for i in nc]
    t_inv = [eye + pw[i] for i in nc]
    span = 2
    while span < L:
        yield
        pw = [_dot(pw[i], pw[i]) for i in nc]
        yield
        t_inv = [t_inv[i] + _dot(t_inv[i], pw[i]) for i in nc]
        span *= 2
    yield
    sol = [_dot(t_inv[i], jnp.concatenate([v[i] * beta[i], kb[i] * egc[i]], axis=-1)) for i in nc]
    yield
    qs = [_dot(q[i] * egc[i], s_old[i]) for i in nc]
    yield
    ws = [_dot(sol[i][:, DV_A:], s_old[i]) for i in nc]
    v_new = [sol[i][:, :DV_A] - ws[i] for i in nc]
    yield
    o = [qs[i] + _dot(jnp.where(tril, qk[i] * decay[i], 0.0), v_new[i]) for i in nc]
    yield
    s_new = [s_old[i] * jnp.exp(gl[i]) + _dot(k[i] * jnp.exp(gl[i] - gc[i]), v_new[i], _TN) for i in nc]
    yield
    ms = [jnp.mean(o[i] * o[i], axis=-1, keepdims=True) for i in nc]
    on = [o[i] * lax.rsqrt(ms[i] + EPS) for i in nc]
    yield
    for i, (g, h) in enumerate(chains):
        z = xin_ref[g, :, CONV_CH + h * DV_A:CONV_CH + (h + 1) * DV_A]
        mix_ref[g, :, h * DV_A:(h + 1) * DV_A] = on[i] * ng_ref[...] * (_silu(z) if has_state else z)
    yield
    for i, (g, h) in enumerate(chains):
        s_ref[g, h] = s_new[i]
        snew_ref[g, h] = s_new[i]


def _gdn(gdn_in, gates, gates_t, cw, ng, *, L, G, state=None):
    n_seq, T, _ = gdn_in.shape
    n_c = T // L
    has_state = state is not None
    data_specs = [
        pl.BlockSpec((G, L, GDN_W), lambda b, c: (b, c, 0)),
        pl.BlockSpec((G, L, LANE), lambda b, c: (b, c, 0)),
        pl.BlockSpec((G, 1, N_GATE, L), lambda b, c: (b, c, 0, 0)),
    ]
    ng_spec = pl.BlockSpec((1, DV_A), lambda b, c: (0, 0))
    mix_spec = pl.BlockSpec((G, L, H_A * DV_A), lambda b, c: (b, c, 0))
    conv_spec = pl.BlockSpec((G, CONV_W - 1, CONV_CH), lambda b, c: (b, 0, 0))
    s_spec = pl.BlockSpec((G, H_A, DK_A, DV_A), lambda b, c: (b, 0, 0, 0))
    mix_shape = jax.ShapeDtypeStruct((n_seq, T, H_A * DV_A), F32)
    conv_shape = jax.ShapeDtypeStruct((n_seq, CONV_W - 1, CONV_CH), F32)
    s_shape = jax.ShapeDtypeStruct((n_seq, H_A, DK_A, DV_A), F32)
    s_scratch = pltpu.VMEM((G, H_A, DK_A, DV_A), F32)
    if has_state:
        conv_st, s0 = state
        in_specs = data_specs + [pl.BlockSpec((SUBLANE, CONV_CH), lambda b, c: (0, 0)), ng_spec, conv_spec, s_spec]
        args = [gdn_in, gates, gates_t, cw, ng, conv_st, s0]
        out_specs, out_shape = [mix_spec, conv_spec, s_spec], [mix_shape, conv_shape, s_shape]
        scratch = [pltpu.VMEM((G, L + SUBLANE, CONV_CH), F32), s_scratch]
    else:
        in_specs = data_specs + [ng_spec]
        args = [gdn_in, gates, gates_t, ng]
        out_specs, out_shape = [mix_spec, s_spec], [mix_shape, s_shape]
        scratch = [s_scratch]
    return pl.pallas_call(
        functools.partial(_gdn_kernel, L=L, G=G, has_state=has_state),
        grid=(n_seq // G, n_c),
        in_specs=in_specs,
        out_specs=out_specs,
        out_shape=out_shape,
        scratch_shapes=scratch,
        compiler_params=pltpu.CompilerParams(dimension_semantics=("parallel", "arbitrary"),
                                             vmem_limit_bytes=VMEM_LIMIT),
        name=f"gdn_L{L}",
    )(*args)


def _mlstm_kernel(*refs, L, G, has_state):
    _run_interleaved(_mlstm_stages(*refs, L=L, G=G, has_state=has_state))


def _mlstm_stages(*refs, L, G, has_state):
    if has_state:
        (xin_ref, gate_ref, gatet_ref, ng_ref, c0_ref, n0_ref, m0_ref,
         mix_ref, cnew_ref, nnew_ref, mnew_ref, c_ref, n_ref, m_ref) = refs
    else:
        (xin_ref, gate_ref, gatet_ref, ng_ref,
         mix_ref, cnew_ref, nnew_ref, mnew_ref, c_ref, n_ref, m_ref) = refs
    c = pl.program_id(1)

    @pl.when(c == 0)
    def _():
        c_ref[...] = jnp.zeros_like(c_ref)
        n_ref[...] = jnp.zeros_like(n_ref)
        m_ref[...] = jnp.zeros_like(m_ref)
        if has_state:
            c_ref[:, :, 0:DK_B, :] = c0_ref[...]
            n_ref[:, 0:H_B, 0:DK_B] = n0_ref[...]
            m_ref[:, 0:1, 0:H_B] = m0_ref[...]

    yield
    tril, _, triu = _chunk_masks(L)
    tril_f, triu_f = tril.astype(F32), triu.astype(F32)

    chains = [(g, h) for g in range(G) for h in range(H_B)]
    nc = range(len(chains))
    c_old = [c_ref[g, h] for g, h in chains]
    n_old = [n_ref[g, h:h + 1, :] for g, h in chains]
    m_old = [m_ref[g, 0:1, h:h + 1] for g, h in chains]

    v0 = 2 * H_B * LANE
    q = [xin_ref[g, :, h * LANE:(h + 1) * LANE] * (DK_B ** -0.5) for g, h in chains]
    k = [xin_ref[g, :, (H_B + h) * LANE:(H_B + h + 1) * LANE] for g, h in chains]
    v = [xin_ref[g, :, v0 + h * DV_B:v0 + (h + 1) * DV_B] for g, h in chains]
    ig_c, b_c, b_last, d_log = [], [], [], []
    for g in range(G):
        gact = gate_ref[g]
        gact_t = gatet_ref[g, 0]
        cum_c = _dot_hi(tril_f, gact)
        cum_r = _dot_hi(gact_t, triu_f)
        for h in range(H_B):
            ig_c.append(gact[:, 8 + h:9 + h])
            b_c.append(cum_c[:, 12 + h:13 + h])
            b_last.append(cum_c[L - 1:L, 12 + h:13 + h])
            d_log.append(jnp.where(tril, cum_c[:, 12 + h:13 + h] - cum_r[12 + h:13 + h, :]
                                   + gact_t[8 + h:9 + h, :], NEG))
        yield
    qk = [_dot(q[i], k[i], _NT) for i in nc]
    yield
    qc = [_dot(q[i], c_old[i]) for i in nc]
    yield
    inter = [b_c[i] + m_old[i] for i in nc]
    m_t = [jnp.maximum(inter[i], jnp.max(d_log[i], axis=-1, keepdims=True)) for i in nc]
    yield
    s = [qk[i] * jnp.exp(d_log[i] - m_t[i]) for i in nc]
    e_inter = [jnp.exp(inter[i] - m_t[i]) for i in nc]
    yield
    sv = [_dot(s[i], v[i]) for i in nc]
    yield
    m_new = [m_t[i][L - 1:L, :] for i in nc]
    kw = [k[i] * jnp.exp(b_last[i] - b_c[i] + ig_c[i] - m_new[i]) for i in nc]
    f_tot = [jnp.exp(b_last[i] + m_old[i] - m_new[i]) for i in nc]
    yield
    c_new = [f_tot[i] * c_old[i] + _dot(kw[i], v[i], _TN) for i in nc]
    yield
    n_new = [f_tot[i] * n_old[i] + jnp.sum(kw[i], axis=0, keepdims=True) for i in nc]
    qn = [jnp.sum(q[i] * n_old[i], axis=-1, keepdims=True) for i in nc]
    yield
    ssum = [jnp.sum(s[i], axis=-1, keepdims=True) for i in nc]
    yield
    den = [jnp.maximum(jnp.abs(e_inter[i] * qn[i] + ssum[i]), jnp.exp(-m_t[i])) for i in nc]
    hh = [(e_inter[i] * qc[i] + sv[i]) / den[i] for i in nc]
    yield
    ms = [jnp.mean(hh[i] * hh[i], axis=-1, keepdims=True) for i in nc]
    hn = [hh[i] * lax.rsqrt(ms[i] + EPS) for i in nc]
    yield
    for i, (g, h) in enumerate(chains):
        og = xin_ref[g, :, v0 + H_B * DV_B + h * DV_B:v0 + H_B * DV_B + (h + 1) * DV_B]
        mix_ref[g, :, h * DV_B:(h + 1) * DV_B] = hn[i] * ng_ref[h:h + 1, :] * _sigmoid(og)
    yield
    for i, (g, h) in enumerate(chains):
        c_ref[g, h] = c_new[i]
        n_ref[g, h:h + 1, :] = n_new[i]
        m_ref[g, 0:1, h:h + 1] = m_new[i]
        cnew_ref[g, h] = c_new[i][0:DK_B, :]
        nnew_ref[g, h:h + 1, :] = n_new[i][:, 0:DK_B]
        mnew_ref[g, 0:1, h:h + 1] = m_new[i]


def _mlstm(ml_in, gates, gates_t, ng, *, L, G, state=None):
    n_seq, T, _ = ml_in.shape
    n_c = T // L
    has_state = state is not None
    in_specs = [
        pl.BlockSpec((G, L, MLP_W), lambda b, c: (b, c, 0)),
        pl.BlockSpec((G, L, LANE), lambda b, c: (b, c, 0)),
        pl.BlockSpec((G, 1, N_GATE, L), lambda b, c: (b, c, 0, 0)),
        pl.BlockSpec((H_B, DV_B), lambda b, c: (0, 0)),
    ]
    args = [ml_in, gates, gates_t, ng]
    if has_state:
        c0, n0, m0 = state
        in_specs += [
            pl.BlockSpec((G, H_B, DK_B, DV_B), lambda b, c: (b, 0, 0, 0)),
            pl.BlockSpec((G, H_B, DK_B), lambda b, c: (b, 0, 0)),
            pl.BlockSpec((G, 1, H_B), lambda b, c: (b, 0, 0)),
        ]
        args += [c0, n0, m0]
    return pl.pallas_call(
        functools.partial(_mlstm_kernel, L=L, G=G, has_state=has_state),
        grid=(n_seq // G, n_c),
        in_specs=in_specs,
        out_specs=[
            pl.BlockSpec((G, L, H_B * DV_B), lambda b, c: (b, c, 0)),
            pl.BlockSpec((G, H_B, DK_B, DV_B), lambda b, c: (b, 0, 0, 0)),
            pl.BlockSpec((G, H_B, DK_B), lambda b, c: (b, 0, 0)),
            pl.BlockSpec((G, 1, H_B), lambda b, c: (b, 0, 0)),
        ],
        out_shape=[
            jax.ShapeDtypeStruct((n_seq, T, H_B * DV_B), F32),
            jax.ShapeDtypeStruct((n_seq, H_B, DK_B, DV_B), F32),
            jax.ShapeDtypeStruct((n_seq, H_B, DK_B), F32),
            jax.ShapeDtypeStruct((n_seq, 1, H_B), F32),
        ],
        scratch_shapes=[pltpu.VMEM((G, H_B, LANE, DV_B), F32), pltpu.VMEM((G, SUBLANE, LANE), F32),
                        pltpu.VMEM((G, SUBLANE, LANE), F32)],
        compiler_params=pltpu.CompilerParams(dimension_semantics=("parallel", "arbitrary"),
                                             vmem_limit_bytes=VMEM_LIMIT),
        name=f"mlstm_L{L}",
    )(*args)


def _mixers_prompt_kernel(gdn_ref, gate_ref, gatet_ref, nga_ref, ml_ref, ngb_ref,
                          mixa_ref, snew_ref, mixb_ref, cnew_ref, nnew_ref, mnew_ref,
                          s_ref, c_ref, n_ref, m_ref, *, L, G):
    _run_interleaved(
        _gdn_stages(gdn_ref, gate_ref, gatet_ref, nga_ref, mixa_ref, snew_ref, s_ref, L=L, G=G, has_state=False),
        _mlstm_stages(ml_ref, gate_ref, gatet_ref, ngb_ref, mixb_ref, cnew_ref, nnew_ref, mnew_ref,
                      c_ref, n_ref, m_ref, L=L, G=G, has_state=False))


def _mixers_prompt(gdn_in, ml_in, gates, gates_t, ng_a, ng_b, *, L, G):
    n_seq, T, _ = gdn_in.shape
    n_c = T // L

    def seq_blk(*tail):
        return pl.BlockSpec((G,) + tail, lambda b, c: (b,) + (0,) * len(tail))

    def tok_blk(width):
        return pl.BlockSpec((G, L, width), lambda b, c: (b, c, 0))

    return pl.pallas_call(
        functools.partial(_mixers_prompt_kernel, L=L, G=G),
        grid=(n_seq // G, n_c),
        in_specs=[
            tok_blk(GDN_W), tok_blk(LANE),
            pl.BlockSpec((G, 1, N_GATE, L), lambda b, c: (b, c, 0, 0)),
            pl.BlockSpec((1, DV_A), lambda b, c: (0, 0)),
            tok_blk(MLP_W),
            pl.BlockSpec((H_B, DV_B), lambda b, c: (0, 0)),
        ],
        out_specs=[
            tok_blk(H_A * DV_A), seq_blk(H_A, DK_A, DV_A),
            tok_blk(H_B * DV_B), seq_blk(H_B, DK_B, DV_B), seq_blk(H_B, DK_B), seq_blk(1, H_B),
        ],
        out_shape=[
            jax.ShapeDtypeStruct((n_seq, T, H_A * DV_A), F32),
            jax.ShapeDtypeStruct((n_seq, H_A, DK_A, DV_A), F32),
            jax.ShapeDtypeStruct((n_seq, T, H_B * DV_B), F32),
            jax.ShapeDtypeStruct((n_seq, H_B, DK_B, DV_B), F32),
            jax.ShapeDtypeStruct((n_seq, H_B, DK_B), F32),
            jax.ShapeDtypeStruct((n_seq, 1, H_B), F32),
        ],
        scratch_shapes=[pltpu.VMEM((G, H_A, DK_A, DV_A), F32), pltpu.VMEM((G, H_B, LANE, DV_B), F32),
                        pltpu.VMEM((G, SUBLANE, LANE), F32), pltpu.VMEM((G, SUBLANE, LANE), F32)],
        compiler_params=pltpu.CompilerParams(dimension_semantics=("parallel", "arbitrary"),
                                             vmem_limit_bytes=VMEM_LIMIT),
        name=f"mixers_L{L}",
    )(gdn_in, gates, gates_t, ng_a, ml_in, ng_b)


def _outproj_kernel(xp_ref, xs_ref, map_ref, mas_ref, mbp_ref, mbs_ref, wo_ref, g_ref, rw_ref, rb_ref,
                    x1_ref, xsort_ref, info_ref, cpad_ref, *, n_p_tiles):
    half = H_A * DV_A
    tm = xp_ref.shape[0]

    def body(seg):
        x_ref, ma_ref, mb_ref = (xp_ref, xs_ref)[seg], (map_ref, mas_ref)[seg], (mbp_ref, mbs_ref)[seg]
        x1 = (x_ref[...] + jnp.dot(ma_ref[...].astype(BF16), wo_ref[:half, :], preferred_element_type=F32)
              + jnp.dot(mb_ref[...].astype(BF16), wo_ref[half:, :], preferred_element_type=F32))
        x1_ref[...] = x1
        hn = _rms(x1, g_ref[...])
        hn_hi = hn.astype(BF16)
        hn_lo = (hn - hn_hi.astype(F32)).astype(BF16)
        logits = (jnp.dot(hn_hi, rw_ref[0], preferred_element_type=F32)
                  + jnp.dot(hn_hi, rw_ref[1], preferred_element_type=F32)
                  + jnp.dot(hn_lo, rw_ref[0], preferred_element_type=F32)) + rb_ref[...]

        vals = logits.T[:N_EXPERTS, :]
        e_iota = lax.broadcasted_iota(jnp.int32, (N_EXPERTS, tm), 0)
        sels, tops = [], []
        for _ in range(TOP_K):
            m = jnp.max(vals, axis=0, keepdims=True)
            first = jnp.min(jnp.where(vals == m, e_iota, N_EXPERTS), axis=0, keepdims=True)
            sel = e_iota == first
            vals = jnp.where(sel, -jnp.inf, vals)
            sels.append(sel)
            tops.append(m)
        ex = [jnp.exp(t - tops[0]) for t in tops]
        den = ex[0] + ex[1] + ex[2] + ex[3]
        gates = [e / den for e in ex]
        mask = sels[0].astype(F32) + sels[1].astype(F32) + sels[2].astype(F32) + sels[3].astype(F32)
        ri = lax.broadcasted_iota(jnp.int32, (tm, tm), 0)
        ci = lax.broadcasted_iota(jnp.int32, (tm, tm), 1)
        rank = _dot(mask, (ri < ci).astype(F32))
        cnt = jnp.sum(mask, axis=1, keepdims=True)
        cpad = jnp.ceil(cnt * (1.0 / SEG_ALIGN)) * SEG_ALIGN
        cpad_b = jnp.broadcast_to(cpad, (N_EXPERTS, tm))
        er = lax.broadcasted_iota(jnp.int32, (N_EXPERTS, N_EXPERTS), 0)
        ec = lax.broadcasted_iota(jnp.int32, (N_EXPERTS, N_EXPERTS), 1)
        seg_off = _dot((er > ec).astype(F32), cpad_b)
        pos = seg_off + rank
        q = [jnp.sum(jnp.where(s, pos, 0.0), axis=0, keepdims=True) for s in sels]

        j_iota = lax.broadcasted_iota(jnp.int32, (MOE_CAP, tm), 0).astype(F32)
        perm = jnp.zeros((MOE_CAP, tm), F32)
        for kk in range(TOP_K):
            perm = jnp.where(j_iota == q[kk], 1.0, perm)
        xsorted = _dot(perm, hn)
        xsort_ref[...] = xsorted

        r_iota = lax.broadcasted_iota(jnp.int32, (LANE, tm), 0)
        info = jnp.zeros((LANE, tm), F32)
        for kk in range(TOP_K):
            info = jnp.where(r_iota == kk, q[kk], info)
            info = jnp.where(r_iota == TOP_K + kk, gates[kk], info)
        info_ref[...] = info.T
        cpad_ref[0] = cpad_b[:, :LANE]

    _for_segment(n_p_tiles, body)


def _outproj(xp, xs, ma_p, ma_s, mb_p, mb_s, w_out, g, rw, rb, tm):
    n_p, n_s = xp.shape[0], xs.shape[0]
    n = n_p + n_s
    nt = n // tm
    npt = n_p // tm
    half = H_A * DV_A
    return pl.pallas_call(
        functools.partial(_outproj_kernel, n_p_tiles=npt),
        grid=(nt,),
        in_specs=_two_segment_specs(tm, D_MODEL, npt) + _two_segment_specs(tm, half, npt)
        + _two_segment_specs(tm, half, npt) + [
            pl.BlockSpec((D_MODEL, D_MODEL), lambda i: (0, 0)),
            pl.BlockSpec((1, D_MODEL), lambda i: (0, 0)),
            pl.BlockSpec((2, D_MODEL, LANE), lambda i: (0, 0, 0)),
            pl.BlockSpec((1, LANE), lambda i: (0, 0)),
        ],
        out_specs=[
            pl.BlockSpec((tm, D_MODEL), lambda i: (i, 0)),
            pl.BlockSpec((MOE_CAP, D_MODEL), lambda i: (i, 0)),
            pl.BlockSpec((tm, LANE), lambda i: (i, 0)),
            pl.BlockSpec((1, N_EXPERTS, LANE), lambda i: (i, 0, 0)),
        ],
        out_shape=[
            jax.ShapeDtypeStruct((n, D_MODEL), F32),
            jax.ShapeDtypeStruct((nt * MOE_CAP, D_MODEL), F32),
            jax.ShapeDtypeStruct((n, LANE), F32),
            jax.ShapeDtypeStruct((nt, N_EXPERTS, LANE), F32),
        ],
        compiler_params=pltpu.CompilerParams(dimension_semantics=("arbitrary",), vmem_limit_bytes=VMEM_LIMIT),
        name="outproj",
    )(xp, xs, ma_p, ma_s, mb_p, mb_s, w_out, g, rw, rb)


def _expert_kernel(be_ref, bj_ref, tf_ref, tl_ref, cov_ref, nu_ref, vt_ref, ct_ref, lt_ref, nx_ref, ws_ref,
                   xs_hbm, wgu_hbm, bgu_ref, wd_hbm, bd_ref, ys_hbm,
                   xbuf, ybuf, gsem, ssem, wgu_st, wd_st, wsem, wgu_bf, wd_bf, *, nt):
    b = pl.program_id(0)
    n_used = nu_ref[0]
    slot = b % 2

    def start_pieces(bb, copy, s):
        e = be_ref[bb]
        base = bj_ref[bb] * MOE_BLK

        def body(t, carry):
            k = e * nt + t
            lo = jnp.maximum(vt_ref[k], base)
            ln = jnp.minimum(ct_ref[k], base + MOE_BLK) - lo

            @pl.when(ln > 0)
            def _():
                copy(s, pl.multiple_of(lt_ref[k] + lo, SEG_ALIGN), pl.multiple_of(lo - base, SEG_ALIGN),
                     pl.multiple_of(ln, SEG_ALIGN)).start()
            return carry

        lax.fori_loop(tf_ref[bb], tl_ref[bb] + 1, body, 0)

    def weight_copies(e):
        return (pltpu.make_async_copy(wgu_hbm.at[e], wgu_st, wsem.at[0]),
                pltpu.make_async_copy(wd_hbm.at[e], wd_st, wsem.at[1]))

    def cast_weights(p):
        wgu_bf[p] = wgu_st[...].astype(BF16)
        wd_bf[p] = wd_st[...].astype(BF16)

    def gather_copy(s, src, dst, size):
        return pltpu.make_async_copy(xs_hbm.at[pl.ds(src, size)], xbuf.at[s, pl.ds(dst, size)], gsem.at[s])

    def scatter_copy(s, src, dst, size):
        return pltpu.make_async_copy(ybuf.at[s, pl.ds(dst, size)], ys_hbm.at[pl.ds(src, size)], ssem.at[s])

    def wait_rows(count, copy, s):
        @pl.when(count > 0)
        def _():
            copy(s, 0, 0, pl.multiple_of(count, SEG_ALIGN)).wait()

    @pl.when(b == 0)
    def _():
        xbuf[...] = jnp.zeros_like(xbuf)
        start_pieces(0, gather_copy, 0)

    @pl.when(b + 1 < n_used)
    def _():
        start_pieces(b + 1, gather_copy, 1 - slot)

    @pl.when(b < n_used)
    def _():
        e = be_ref[b]
        first = jnp.logical_or(b == 0, be_ref[jnp.maximum(b - 1, 0)] != e)
        last = jnp.logical_or(b == n_used - 1, be_ref[jnp.minimum(b + 1, n_used - 1)] != e)
        has_next = nx_ref[b] < N_EXPERTS
        p = ws_ref[b]

        @pl.when(b == 0)
        def _():
            for cp in weight_copies(e):
                cp.start()
            for cp in weight_copies(e):
                cp.wait()
            cast_weights(p)

        @pl.when(jnp.logical_and(first, has_next))
        def _():
            for cp in weight_copies(nx_ref[b]):
                cp.start()

        wait_rows(cov_ref[b], gather_copy, slot)
        hgu = jnp.dot(xbuf[slot].astype(BF16), wgu_bf[p], preferred_element_type=F32) + bgu_ref[0]
        gate = jnp.minimum(hgu[:, :D_FF], SWIGLU_LIMIT)
        up = jnp.clip(hgu[:, D_FF:], -SWIGLU_LIMIT, SWIGLU_LIMIT)
        act = (up + 1.0) * gate * _sigmoid(SWIGLU_ALPHA * gate)
        y = jnp.dot(act.astype(BF16), wd_bf[p], preferred_element_type=F32) + bd_ref[0]

        @pl.when(jnp.logical_and(last, has_next))
        def _():
            for cp in weight_copies(nx_ref[b]):
                cp.wait()
            cast_weights(1 - p)

        @pl.when(b >= 2)
        def _():
            wait_rows(cov_ref[jnp.maximum(b - 2, 0)], scatter_copy, slot)

        ybuf[slot] = y
        start_pieces(b, scatter_copy, slot)

        @pl.when(b == n_used - 1)
        def _():
            wait_rows(cov_ref[b], scatter_copy, slot)
            wait_rows(jnp.where(b >= 1, cov_ref[jnp.maximum(b - 1, 0)], 0), scatter_copy, 1 - slot)


def _experts(tables, xs, w_gu, b_gu, w_down, b_down, nt):
    nb = tables[0].shape[0]

    def bias_blk(b, *t):
        return (t[0][jnp.minimum(b, t[5][0] - 1)], 0, 0)

    grid_spec = pltpu.PrefetchScalarGridSpec(
        num_scalar_prefetch=len(tables),
        grid=(nb,),
        in_specs=[
            pl.BlockSpec(memory_space=pl.ANY),
            pl.BlockSpec(memory_space=pl.ANY),
            pl.BlockSpec((1, 1, 2 * D_FF), bias_blk),
            pl.BlockSpec(memory_space=pl.ANY),
            pl.BlockSpec((1, 1, D_MODEL), bias_blk),
        ],
        out_specs=pl.BlockSpec(memory_space=pl.ANY),
        scratch_shapes=[
            pltpu.VMEM((2, MOE_BLK, D_MODEL), F32),
            pltpu.VMEM((2, MOE_BLK, D_MODEL), F32),
            pltpu.SemaphoreType.DMA((2,)),
            pltpu.SemaphoreType.DMA((2,)),
            pltpu.VMEM((D_MODEL, 2 * D_FF), F32),
            pltpu.VMEM((D_FF, D_MODEL), F32),
            pltpu.SemaphoreType.DMA((2,)),
            pltpu.VMEM((2, D_MODEL, 2 * D_FF), BF16),
            pltpu.VMEM((2, D_FF, D_MODEL), BF16),
        ],
    )
    return pl.pallas_call(
        functools.partial(_expert_kernel, nt=nt),
        grid_spec=grid_spec,
        out_shape=jax.ShapeDtypeStruct(xs.shape, xs.dtype),
        input_output_aliases={len(tables): 0},
        compiler_params=pltpu.CompilerParams(dimension_semantics=("arbitrary",), vmem_limit_bytes=VMEM_LIMIT),
        name="experts",
    )(*tables, xs, w_gu, b_gu, w_down, b_down)


def _combine_kernel(ys_ref, info_ref, x1_ref, pp_ref, ps_ref, gple_ref, wg_ref, wp_ref, gfin_ref,
                    outp_ref, outs_ref, *, n_p_tiles):
    tm = x1_ref.shape[0]

    def body(seg):
        p_ref, out_ref = (pp_ref, ps_ref)[seg], (outp_ref, outs_ref)[seg]
        info = info_ref[...]
        j_iota = lax.broadcasted_iota(jnp.int32, (tm, MOE_CAP), 1).astype(F32)
        gmat = jnp.zeros((tm, MOE_CAP), F32)
        for kk in range(TOP_K):
            gmat = jnp.where(j_iota == info[:, kk:kk + 1], info[:, TOP_K + kk:TOP_K + kk + 1], gmat)
        x2 = x1_ref[...] + jnp.dot(gmat.astype(BF16), ys_ref[...].astype(BF16), preferred_element_type=F32)
        hn = _rms(x2, gple_ref[...]).astype(BF16)
        gate = _sigmoid(jnp.dot(hn, wg_ref[...], preferred_element_type=F32))
        pe = jnp.dot(p_ref[...].astype(BF16), wp_ref[...], preferred_element_type=F32)
        x3 = x2 + gate * pe
        out_ref[...] = _rms(x3, gfin_ref[...])

    _for_segment(n_p_tiles, body)


def _combine(ys, info, x1, pp, ps, g_ple, w_gate, w_p, g_fin, tm):
    n_p, n_s = pp.shape[0], ps.shape[0]
    n = n_p + n_s
    nt = n // tm
    npt = n_p // tm
    return pl.pallas_call(
        functools.partial(_combine_kernel, n_p_tiles=npt),
        grid=(nt,),
        in_specs=[
            pl.BlockSpec((MOE_CAP, D_MODEL), lambda i: (i, 0)),
            pl.BlockSpec((tm, LANE), lambda i: (i, 0)),
            pl.BlockSpec((tm, D_MODEL), lambda i: (i, 0)),
        ] + _two_segment_specs(tm, PLE_DIM, npt) + [
            pl.BlockSpec((1, D_MODEL), lambda i: (0, 0)),
            pl.BlockSpec((D_MODEL, D_MODEL), lambda i: (0, 0)),
            pl.BlockSpec((PLE_DIM, D_MODEL), lambda i: (0, 0)),
            pl.BlockSpec((1, D_MODEL), lambda i: (0, 0)),
        ],
        out_specs=_two_segment_specs(tm, D_MODEL, npt),
        out_shape=[jax.ShapeDtypeStruct((n_p, D_MODEL), F32), jax.ShapeDtypeStruct((n_s, D_MODEL), F32)],
        compiler_params=pltpu.CompilerParams(dimension_semantics=("arbitrary",), vmem_limit_bytes=VMEM_LIMIT),
        name="combine",
    )(ys, info, x1, pp, ps, g_ple, w_gate, w_p, g_fin)


def _block_tables(seg_len, nb):
    nt = seg_len.shape[0]
    seg_off = jnp.cumsum(seg_len, axis=1) - seg_len
    seg_end = jnp.cumsum(seg_len, axis=0).T
    seg_start = seg_end - seg_len.T
    n_rows = seg_end[:, -1]
    n_blk = (n_rows + MOE_BLK - 1) // MOE_BLK
    blk_end = jnp.cumsum(n_blk)
    b = jnp.arange(nb, dtype=jnp.int32)
    block_e = jnp.minimum(jnp.sum((blk_end[None, :] <= b[:, None]).astype(jnp.int32), axis=1), N_EXPERTS - 1)
    idx = jnp.where(n_blk > 0, jnp.arange(N_EXPERTS, dtype=jnp.int32), N_EXPERTS)
    nxt = jnp.concatenate([lax.cummin(idx, axis=0, reverse=True)[1:], jnp.full((1,), N_EXPERTS, jnp.int32)])
    parity = (jnp.cumsum((n_blk > 0).astype(jnp.int32)) - 1) % 2
    per_e = jnp.concatenate([jnp.stack([blk_end - n_blk, n_rows, nxt, parity], axis=1), seg_start, seg_end],
                            axis=1).astype(F32)
    onehot = (block_e[:, None] == jnp.arange(N_EXPERTS, dtype=jnp.int32)[None, :]).astype(F32)
    per_b = jnp.dot(onehot, per_e, precision=HI).astype(jnp.int32)
    block_j = b - per_b[:, 0]
    base = block_j * MOE_BLK
    t_first = jnp.sum((per_b[:, 4 + nt:] <= base[:, None]).astype(jnp.int32), axis=1)
    t_last = jnp.sum((per_b[:, 4:4 + nt] < (base + MOE_BLK)[:, None]).astype(jnp.int32), axis=1) - 1
    cover = jnp.clip(per_b[:, 1] - base, 0, MOE_BLK)
    seg_shift = (jnp.arange(nt, dtype=jnp.int32)[:, None] * MOE_CAP + seg_off).T - seg_start
    tables = (block_e, block_j, t_first, t_last, cover, blk_end[-1:], seg_start.reshape(-1),
              seg_end.reshape(-1), seg_shift.reshape(-1), per_b[:, 2], per_b[:, 3])
    return tuple(t.astype(jnp.int32) for t in tables)


def _rearranged_in_weights(w_in):
    o = np.cumsum([0, CONV_CH, H_A * DV_A, H_A, H_A, H_B * DK_B, H_B * DK_B, H_B * DV_B, H_B * DV_B, H_B, H_B])
    conv_in, z_a, a_a, b_a, q_b, k_b, v_b, o_b, i_b, f_b = (w_in[:, int(o[j]):int(o[j + 1])] for j in range(10))
    zpad = jnp.zeros((D_MODEL, LANE - DK_B), w_in.dtype)

    def pad_heads(w):
        return jnp.concatenate([jnp.concatenate([w[:, h * DK_B:(h + 1) * DK_B], zpad], axis=1) for h in range(H_B)],
                               axis=1)

    small = jnp.concatenate([a_a, b_a, i_b, f_b], axis=1)
    w_gdn = w_in[:, :GDN_W]
    w_rest = jnp.concatenate([pad_heads(q_b), pad_heads(k_b), v_b, o_b,
                              small, jnp.zeros((D_MODEL, LANE - N_GATE), w_in.dtype)], axis=1)
    return w_gdn.astype(BF16), w_rest.astype(BF16), small.T.astype(BF16)


def _gate_params(a_log, dt_bias, i_bias, f_bias):
    z4 = jnp.zeros((4,), F32)
    alog = jnp.concatenate([a_log.astype(F32), z4, z4, z4])
    bias = jnp.concatenate([dt_bias.astype(F32), z4, i_bias.astype(F32), f_bias.astype(F32)])
    pad = jnp.zeros((LANE - N_GATE,), F32)
    pcol = jnp.zeros((SUBLANE, LANE), F32).at[0].set(jnp.concatenate([alog, pad])).at[1].set(
        jnp.concatenate([bias, pad]))
    prow = jnp.zeros((N_GATE, LANE), F32).at[:, 0].set(alog).at[:, 1].set(bias)
    return pcol, prow


def kernel(x_prompt, x_sample, p_prompt, p_sample, state_conv, state_gdn, state_mlstm_c, state_mlstm_n, state_mlstm_m, norm_attn_g, w_in, conv_w, gdn_a_log, gdn_dt_bias, gdn_norm_g, mlstm_i_bias, mlstm_f_bias, mlstm_norm_g, w_out, norm_moe_g, router_w, router_b, expert_w_gu, expert_b_gu, expert_w_down, expert_b_down, norm_ple_g, ple_gate_w, ple_w, final_norm_g):
    bp, tp, _ = x_prompt.shape
    bs, ts, _ = x_sample.shape
    n_p, n_s = bp * tp, bs * ts
    n = n_p + n_s
    lp, ls = min(tp, CHUNK), min(ts, CHUNK)
    tm = 256
    gp = 4 if bp % 4 == 0 else 1
    gs = 8 if bs % 8 == 0 else 1
    assert tp % lp == 0 and ts % ls == 0 and tp % tm == 0 and n_s % tm == 0 and ls % SUBLANE == 0

    xp = x_prompt.reshape(n_p, D_MODEL)
    xs = x_sample.reshape(n_s, D_MODEL)

    w_gdn, w_rest, ws_t = _rearranged_in_weights(w_in[0])
    pcol, prow = _gate_params(gdn_a_log[0], gdn_dt_bias[0], mlstm_i_bias[0], mlstm_f_bias[0])
    cw = jnp.zeros((SUBLANE, CONV_CH), F32).at[:CONV_W].set(conv_w[0].astype(F32))
    gdn_p, gdn_s, ml_p, ml_s, gate_p, gate_s, gatet_p, gatet_s, conv_p = _inproj(
        xp, xs, norm_attn_g[0].reshape(1, D_MODEL), w_gdn, w_rest, ws_t, pcol, prow, cw, tm, bp)
    gt_p = gatet_p.reshape(N_GATE, bp, tp // lp, lp).transpose(1, 2, 0, 3)
    gt_s = gatet_s.reshape(N_GATE, bs, ts // ls, ls).transpose(1, 2, 0, 3)

    ng_a = gdn_norm_g[0].reshape(1, DV_A).astype(F32)
    ng_b = mlstm_norm_g[0].reshape(H_B, DV_B).astype(F32)
    ma_p, gdn_st_p, mb_p, c_p, nn_p, m_p = _mixers_prompt(
        gdn_p.reshape(bp, tp, GDN_W), ml_p.reshape(bp, tp, MLP_W), gate_p.reshape(bp, tp, LANE), gt_p, ng_a, ng_b,
        L=lp, G=gp)
    ma_s, conv_s, gdn_st_s = _gdn(gdn_s.reshape(bs, ts, GDN_W), gate_s.reshape(bs, ts, LANE), gt_s, cw, ng_a,
                                  L=ls, G=gs, state=(state_conv[0], state_gdn[0]))
    mb_s, c_s, nn_s, m_s = _mlstm(ml_s.reshape(bs, ts, MLP_W), gate_s.reshape(bs, ts, LANE), gt_s, ng_b,
                                  L=ls, G=gs,
                                  state=(state_mlstm_c[0], state_mlstm_n[0], state_mlstm_m[0].reshape(bs, 1, H_B)))
    half = H_A * DV_A

    rw = jnp.zeros((D_MODEL, LANE), F32).at[:, :N_EXPERTS].set(router_w[0])
    rw_hi = rw.astype(BF16)
    rw = jnp.stack([rw_hi, (rw - rw_hi.astype(F32)).astype(BF16)])
    rb = jnp.full((1, LANE), NEG, F32).at[0, :N_EXPERTS].set(router_b[0])
    x1, x_sorted, info, seg_len = _outproj(xp, xs, ma_p.reshape(n_p, half), ma_s.reshape(n_s, half),
                                           mb_p.reshape(n_p, half), mb_s.reshape(n_s, half),
                                           w_out[0].astype(BF16), norm_moe_g[0].reshape(1, D_MODEL), rw, rb, MOE_TM)

    nt = n // MOE_TM
    nb = -(-(n * TOP_K + nt * N_EXPERTS * (SEG_ALIGN - 1)) // MOE_BLK) + N_EXPERTS
    tables = _block_tables(seg_len[:, :, 0].astype(jnp.int32), nb)
    y_sorted = _experts(tables, x_sorted, expert_w_gu[0], expert_b_gu[0].reshape(N_EXPERTS, 1, 2 * D_FF),
                        expert_w_down[0], expert_b_down[0].reshape(N_EXPERTS, 1, D_MODEL), nt)
    y_p, y_s = _combine(y_sorted, info, x1, p_prompt[0].reshape(n_p, PLE_DIM),
                        p_sample[0].reshape(n_s, PLE_DIM), norm_ple_g[0].reshape(1, D_MODEL),
                        ple_gate_w[0].astype(BF16), ple_w[0].astype(BF16), final_norm_g.reshape(1, D_MODEL), MOE_TM)

    return (y_p.reshape(bp, tp, D_MODEL), y_s.reshape(bs, ts, D_MODEL),
            conv_p[None], gdn_st_p[None], c_p[None], nn_p[None], m_p.reshape(1, bp, H_B),
            conv_s[None], gdn_st_s[None], c_s[None], nn_s[None], m_s.reshape(1, bs, H_B))
```

```python
import functools

import numpy as np
import jax
import jax.numpy as jnp
from jax import lax
from jax.experimental import pallas as pl
from jax.experimental.pallas import tpu as pltpu

F32 = jnp.float32
BF16 = jnp.bfloat16

D_MODEL = 1024
H_A, DK_A, DV_A = 4, 128, 128
H_B, DK_B, DV_B = 4, 64, 128
CONV_W = 4
CONV_CH = H_A * (2 * DK_A + DV_A)
N_EXPERTS = 32
TOP_K = 4
D_FF = 1024
SWIGLU_LIMIT = 7.0
SWIGLU_ALPHA = 1.702
PLE_DIM = 256
EPS = 1e-6
NEG = -1e30
CHUNK = 64

LANE = 128
SUBLANE = 8
GDN_W = CONV_CH + H_A * DV_A
MLP_W = 2 * H_B * LANE + 2 * H_B * DV_B
N_GATE = 16

VMEM_LIMIT = 48 * 1024 * 1024

MOE_TM = 256
MOE_BLK = 512
SEG_ALIGN = SUBLANE
MOE_CAP = -(-(MOE_TM * TOP_K + N_EXPERTS * (SEG_ALIGN - 1)) // LANE) * LANE

HI = lax.Precision.HIGHEST

_NN = (((1,), (0,)), ((), ()))
_NT = (((1,), (1,)), ((), ()))
_TN = (((0,), (0,)), ((), ()))


def _dot(a, b, dims=_NN):
    return lax.dot_general(a.astype(BF16), b.astype(BF16), dims, preferred_element_type=F32)


def _dot_hi(a, b, dims=_NN):
    return lax.dot_general(a, b, dims, precision=HI, preferred_element_type=F32)


def _rms(x, g):
    return x * lax.rsqrt(jnp.mean(x * x, axis=-1, keepdims=True) + EPS) * g


def _softplus(t):
    return jnp.maximum(t, 0.0) + jnp.log1p(jnp.exp(-jnp.abs(t)))


def _sigmoid(t):
    return 1.0 / (1.0 + jnp.exp(-t))


def _silu(t):
    return t * _sigmoid(t)


def _activate_gates(raw, idx, alog, bias):
    t = raw + bias
    g = -jnp.exp(alog) * _softplus(t)
    beta = _sigmoid(t)
    lf = -_softplus(-t)
    return jnp.where(idx < 4, g, jnp.where(idx < 8, beta, jnp.where(idx < 12, t, lf)))


def _two_segment_specs(tm, width, n_p_tiles):
    return [pl.BlockSpec((tm, width), lambda i: (jnp.minimum(i, n_p_tiles - 1), 0)),
            pl.BlockSpec((tm, width), lambda i: (jnp.maximum(i - n_p_tiles, 0), 0))]


def _for_segment(n_p_tiles, body):
    i = pl.program_id(0)

    @pl.when(i < n_p_tiles)
    def _():
        body(0)

    @pl.when(i >= n_p_tiles)
    def _():
        body(1)


def _gdn_preactivate(raw, xc_ref, cw_ref, first_of_seq):
    tm = raw.shape[0]
    xc_ref[0:SUBLANE, :] = jnp.where(first_of_seq, 0.0, xc_ref[tm:tm + SUBLANE, :])
    xc_ref[SUBLANE:SUBLANE + tm, :] = raw[:, :CONV_CH]
    base = SUBLANE - (CONV_W - 1)
    conv = xc_ref[base:base + tm, :] * cw_ref[0:1, :]
    for j in range(1, CONV_W):
        conv = conv + xc_ref[base + j:base + j + tm, :] * cw_ref[j:j + 1, :]
    act = _silu(conv)
    parts = []
    for h in range(H_A):
        qh = act[:, h * DK_A:(h + 1) * DK_A]
        parts.append(qh * (lax.rsqrt(jnp.sum(qh * qh, axis=-1, keepdims=True) + EPS) * (DK_A ** -0.5)))
    for h in range(H_A):
        kh = act[:, H_A * DK_A + h * DK_A:H_A * DK_A + (h + 1) * DK_A]
        parts.append(kh * lax.rsqrt(jnp.sum(kh * kh, axis=-1, keepdims=True) + EPS))
    parts.append(act[:, 2 * H_A * DK_A:])
    parts.append(_silu(raw[:, CONV_CH:]))
    new_conv = xc_ref[SUBLANE + tm - (CONV_W - 1):SUBLANE + tm, :]
    return jnp.concatenate(parts, axis=-1), new_conv


def _inproj_kernel(xp_ref, xs_ref, g_ref, wa_ref, wb_ref, wst_ref, pc_ref, pr_ref, cw_ref,
                   gdnp_ref, gdns_ref, mlp_ref, mls_ref, gatep_ref, gates_ref, gatetp_ref, gatets_ref, cnew_ref,
                   xc_ref, raw_ref, *, n_p_tiles, tiles_per_seq):
    tm = xp_ref.shape[0]
    i = pl.program_id(0)

    @pl.when(i == 0)
    def _():
        xc_ref[...] = jnp.zeros_like(xc_ref)
        raw_ref[...] = jnp.zeros_like(raw_ref)

    def preactivate_previous_tile():
        pre, new_conv = _gdn_preactivate(raw_ref[...], xc_ref, cw_ref, (i - 1) % tiles_per_seq == 0)
        gdnp_ref[...] = pre
        cnew_ref[0] = new_conv

    def body(seg):
        x_ref = (xp_ref, xs_ref)[seg]
        ml_ref = (mlp_ref, mls_ref)[seg]
        gate_ref, gatet_ref = (gatep_ref, gates_ref)[seg], (gatetp_ref, gatets_ref)[seg]
        if seg == 0:
            preactivate_previous_tile()
        else:
            pl.when(i == n_p_tiles)(preactivate_previous_tile)
        hn = _rms(x_ref[...], g_ref[...]).astype(BF16)
        raw_gdn = jnp.dot(hn, wa_ref[...], preferred_element_type=F32)
        if seg == 0:
            raw_ref[...] = raw_gdn
        else:
            gdns_ref[...] = raw_gdn
        ml_ref[...] = jnp.dot(hn, wb_ref[:, :MLP_W], preferred_element_type=F32)
        raw = jnp.dot(hn, wb_ref[:, MLP_W:], preferred_element_type=F32)
        lane = lax.broadcasted_iota(jnp.int32, (tm, LANE), 1)
        gate_ref[...] = _activate_gates(raw, lane, pc_ref[0:1, :], pc_ref[1:2, :])
        raw_t = lax.dot_general(wst_ref[...], hn, _NT, preferred_element_type=F32)
        row = lax.broadcasted_iota(jnp.int32, (N_GATE, tm), 0)
        gatet_ref[...] = _activate_gates(raw_t, row, pr_ref[:, 0:1], pr_ref[:, 1:2])

    _for_segment(n_p_tiles, body)


def _inproj(xp, xs, g, w_gdn, w_rest, ws_t, pcol, prow, cw, tm, n_seq_p):
    n_p, n_s = xp.shape[0], xs.shape[0]
    npt = n_p // tm
    tiles_per_seq = npt // n_seq_p

    def out2(width):
        return _two_segment_specs(tm, width, npt)

    def shp2(width):
        return [jax.ShapeDtypeStruct((n_p, width), F32), jax.ShapeDtypeStruct((n_s, width), F32)]

    def prev_tile(i):
        return jnp.clip(i - 1, 0, npt - 1)

    return pl.pallas_call(
        functools.partial(_inproj_kernel, n_p_tiles=npt, tiles_per_seq=tiles_per_seq),
        grid=((n_p + n_s) // tm,),
        in_specs=_two_segment_specs(tm, D_MODEL, npt) + [
            pl.BlockSpec((1, D_MODEL), lambda i: (0, 0)),
            pl.BlockSpec((D_MODEL, GDN_W), lambda i: (0, 0)),
            pl.BlockSpec((D_MODEL, MLP_W + LANE), lambda i: (0, 0)),
            pl.BlockSpec((N_GATE, D_MODEL), lambda i: (0, 0)),
            pl.BlockSpec((SUBLANE, LANE), lambda i: (0, 0)),
            pl.BlockSpec((N_GATE, LANE), lambda i: (0, 0)),
            pl.BlockSpec((SUBLANE, CONV_CH), lambda i: (0, 0)),
        ],
        out_specs=[
            pl.BlockSpec((tm, GDN_W), lambda i: (prev_tile(i), 0)),
            pl.BlockSpec((tm, GDN_W), lambda i: (jnp.maximum(i - npt, 0), 0)),
        ] + out2(MLP_W) + out2(LANE) + [
            pl.BlockSpec((N_GATE, tm), lambda i: (0, jnp.minimum(i, npt - 1))),
            pl.BlockSpec((N_GATE, tm), lambda i: (0, jnp.maximum(i - npt, 0))),
            pl.BlockSpec((1, CONV_W - 1, CONV_CH), lambda i: (prev_tile(i) // tiles_per_seq, 0, 0)),
        ],
        out_shape=shp2(GDN_W) + shp2(MLP_W) + shp2(LANE) + [
            jax.ShapeDtypeStruct((N_GATE, n_p), F32), jax.ShapeDtypeStruct((N_GATE, n_s), F32),
            jax.ShapeDtypeStruct((n_seq_p, CONV_W - 1, CONV_CH), F32)],
        scratch_shapes=[pltpu.VMEM((tm + SUBLANE, CONV_CH), F32), pltpu.VMEM((tm, GDN_W), F32)],
        compiler_params=pltpu.CompilerParams(dimension_semantics=("arbitrary",), vmem_limit_bytes=VMEM_LIMIT),
        name="inproj",
    )(xp, xs, g, w_gdn, w_rest, ws_t, pcol, prow, cw)


def _chunk_masks(L):
    ri = lax.broadcasted_iota(jnp.int32, (L, L), 0)
    ci = lax.broadcasted_iota(jnp.int32, (L, L), 1)
    return ri >= ci, ri > ci, ri <= ci


def _run_interleaved(*stage_generators):
    live = list(stage_generators)
    while live:
        for gen in list(live):
            if next(gen, StopIteration) is StopIteration:
                live.remove(gen)


def _gdn_kernel(*refs, L, G, has_state):
    _run_interleaved(_gdn_stages(*refs, L=L, G=G, has_state=has_state))


def _gdn_stages(*refs, L, G, has_state):
    if has_state:
        (xin_ref, gate_ref, gatet_ref, cw_ref, ng_ref, cst_ref, s0_ref,
         mix_ref, cnew_ref, snew_ref, xc_ref, s_ref) = refs
    else:
        xin_ref, gate_ref, gatet_ref, ng_ref, mix_ref, snew_ref, s_ref = refs
    c = pl.program_id(1)

    @pl.when(c == 0)
    def _():
        if has_state:
            xc_ref[:, 0:SUBLANE, :] = jnp.zeros((G, SUBLANE, CONV_CH), F32)
            xc_ref[:, SUBLANE - (CONV_W - 1):SUBLANE, :] = cst_ref[...]
            s_ref[...] = s0_ref[...]
        else:
            s_ref[...] = jnp.zeros_like(s_ref)

    if has_state:
        @pl.when(c > 0)
        def _():
            xc_ref[:, 0:SUBLANE, :] = xc_ref[:, L:L + SUBLANE, :]

    yield
    tril, strict, triu = _chunk_masks(L)
    tril_f, triu_f = tril.astype(F32), triu.astype(F32)
    base = SUBLANE - (CONV_W - 1)

    chains = [(g, h) for g in range(G) for h in range(H_A)]
    s_old = [s_ref[g, h] for g, h in chains]
    if has_state:
        for g in range(G):
            xc_ref[g, SUBLANE:SUBLANE + L, :] = xin_ref[g, :, :CONV_CH]

    q, k, v, beta, gc, gl, decay = [], [], [], [], [], [], []
    for g in range(G):
        if has_state:
            conv = xc_ref[g, base:base + L, :] * cw_ref[0:1, :]
            for j in range(1, CONV_W):
                conv = conv + xc_ref[g, base + j:base + j + L, :] * cw_ref[j:j + 1, :]
            cnew_ref[g] = xc_ref[g, SUBLANE + L - (CONV_W - 1):SUBLANE + L, :]
            act = _silu(conv)
        else:
            act = xin_ref[g, :, :CONV_CH]
        gact = gate_ref[g]
        cum_c = _dot_hi(tril_f, gact)
        cum_r = _dot_hi(gatet_ref[g, 0], triu_f)
        for h in range(H_A):
            q.append(act[:, h * DK_A:(h + 1) * DK_A])
            k.append(act[:, H_A * DK_A + h * DK_A:H_A * DK_A + (h + 1) * DK_A])
            v.append(act[:, 2 * H_A * DK_A + h * DV_A:2 * H_A * DK_A + (h + 1) * DV_A])
            beta.append(gact[:, 4 + h:5 + h])
            gc.append(cum_c[:, h:h + 1])
            gl.append(cum_c[L - 1:L, h:h + 1])
            gr = cum_r[h:h + 1, :]
            decay.append(jnp.where(tril, jnp.exp(jnp.where(tril, cum_c[:, h:h + 1] - gr, 0.0)), 0.0))
        yield

    nc = range(len(chains))
    if has_state:
        qss = [jnp.sum(q[i] * q[i], axis=-1, keepdims=True) for i in nc]
        kss = [jnp.sum(k[i] * k[i], axis=-1, keepdims=True) for i in nc]
        q = [q[i] * (lax.rsqrt(qss[i] + EPS) * (DK_A ** -0.5)) for i in nc]
        k = [k[i] * lax.rsqrt(kss[i] + EPS) for i in nc]
    kb = [k[i] * beta[i] for i in nc]
    egc = [jnp.exp(gc[i]) for i in nc]
    yield
    kk = [_dot(kb[i], k[i], _NT) for i in nc]
    yield
    qk = [_dot(q[i], k[i], _NT) for i in nc]
    yield
    eye = (lax.broadcasted_iota(jnp.int32, (L, L), 0) == lax.broadcasted_iota(jnp.int32, (L, L), 1)).astype(F32)
    pw = [-jnp.where(strict, kk[i] * decay[i], 0.0) for i in nc]
    t_inv = [eye + pw[i] for i in nc]
    span = 2
    while span < L:
        yield
        pw = [_dot(pw[i], pw[i]) for i in nc]
        yield
        t_inv = [t_inv[i] + _dot(t_inv[i], pw[i]) for i in nc]
        span *= 2
    yield
    sol = [_dot(t_inv[i], jnp.concatenate([v[i] * beta[i], kb[i] * egc[i]], axis=-1)) for i in nc]
    yield
    qs = [_dot(q[i] * egc[i], s_old[i]) for i in nc]
    yield
    ws = [_dot(sol[i][:, DV_A:], s_old[i]) for i in nc]
    v_new = [sol[i][:, :DV_A] - ws[i] for i in nc]
    yield
    o = [qs[i] + _dot(jnp.where(tril, qk[i] * decay[i], 0.0), v_new[i]) for i in nc]
    yield
    s_new = [s_old[i] * jnp.exp(gl[i]) + _dot(k[i] * jnp.exp(gl[i] - gc[i]), v_new[i], _TN) for i in nc]
    yield
    ms = [jnp.mean(o[i] * o[i], axis=-1, keepdims=True) for i in nc]
    on = [o[i] * lax.rsqrt(ms[i] + EPS) for i in nc]
    yield
    for i, (g, h) in enumerate(chains):
        z = xin_ref[g, :, CONV_CH + h * DV_A:CONV_CH + (h + 1) * DV_A]
        mix_ref[g, :, h * DV_A:(h + 1) * DV_A] = on[i] * ng_ref[...] * (_silu(z) if has_state else z)
    yield
    for i, (g, h) in enumerate(chains):
        s_ref[g, h] = s_new[i]
        snew_ref[g, h] = s_new[i]


def _gdn(gdn_in, gates, gates_t, cw, ng, *, L, G, state=None):
    n_seq, T, _ = gdn_in.shape
    n_c = T // L
    has_state = state is not None
    data_specs = [
        pl.BlockSpec((G, L, GDN_W), lambda b, c: (b, c, 0)),
        pl.BlockSpec((G, L, LANE), lambda b, c: (b, c, 0)),
        pl.BlockSpec((G, 1, N_GATE, L), lambda b, c: (b, c, 0, 0)),
    ]
    ng_spec = pl.BlockSpec((1, DV_A), lambda b, c: (0, 0))
    mix_spec = pl.BlockSpec((G, L, H_A * DV_A), lambda b, c: (b, c, 0))
    conv_spec = pl.BlockSpec((G, CONV_W - 1, CONV_CH), lambda b, c: (b, 0, 0))
    s_spec = pl.BlockSpec((G, H_A, DK_A, DV_A), lambda b, c: (b, 0, 0, 0))
    mix_shape = jax.ShapeDtypeStruct((n_seq, T, H_A * DV_A), F32)
    conv_shape = jax.ShapeDtypeStruct((n_seq, CONV_W - 1, CONV_CH), F32)
    s_shape = jax.ShapeDtypeStruct((n_seq, H_A, DK_A, DV_A), F32)
    s_scratch = pltpu.VMEM((G, H_A, DK_A, DV_A), F32)
    if has_state:
        conv_st, s0 = state
        in_specs = data_specs + [pl.BlockSpec((SUBLANE, CONV_CH), lambda b, c: (0, 0)), ng_spec, conv_spec, s_spec]
        args = [gdn_in, gates, gates_t, cw, ng, conv_st, s0]
        out_specs, out_shape = [mix_spec, conv_spec, s_spec], [mix_shape, conv_shape, s_shape]
        scratch = [pltpu.VMEM((G, L + SUBLANE, CONV_CH), F32), s_scratch]
    else:
        in_specs = data_specs + [ng_spec]
        args = [gdn_in, gates, gates_t, ng]
        out_specs, out_shape = [mix_spec, s_spec], [mix_shape, s_shape]
        scratch = [s_scratch]
    return pl.pallas_call(
        functools.partial(_gdn_kernel, L=L, G=G, has_state=has_state),
        grid=(n_seq // G, n_c),
        in_specs=in_specs,
        out_specs=out_specs,
        out_shape=out_shape,
        scratch_shapes=scratch,
        compiler_params=pltpu.CompilerParams(dimension_semantics=("parallel", "arbitrary"),
                                             vmem_limit_bytes=VMEM_LIMIT),
        name=f"gdn_L{L}",
    )(*args)


def _mlstm_kernel(*refs, L, G, has_state):
    _run_interleaved(_mlstm_stages(*refs, L=L, G=G, has_state=has_state))


def _mlstm_stages(*refs, L, G, has_state):
    if has_state:
        (xin_ref, gate_ref, gatet_ref, ng_ref, c0_ref, n0_ref, m0_ref,
         mix_ref, cnew_ref, nnew_ref, mnew_ref, c_ref, n_ref, m_ref) = refs
    else:
        (xin_ref, gate_ref, gatet_ref, ng_ref,
         mix_ref, cnew_ref, nnew_ref, mnew_ref, c_ref, n_ref, m_ref) = refs
    c = pl.program_id(1)

    @pl.when(c == 0)
    def _():
        c_ref[...] = jnp.zeros_like(c_ref)
        n_ref[...] = jnp.zeros_like(n_ref)
        m_ref[...] = jnp.zeros_like(m_ref)
        if has_state:
            c_ref[:, :, 0:DK_B, :] = c0_ref[...]
            n_ref[:, 0:H_B, 0:DK_B] = n0_ref[...]
            m_ref[:, 0:1, 0:H_B] = m0_ref[...]

    yield
    tril, _, triu = _chunk_masks(L)
    tril_f, triu_f = tril.astype(F32), triu.astype(F32)

    chains = [(g, h) for g in range(G) for h in range(H_B)]
    nc = range(len(chains))
    c_old = [c_ref[g, h] for g, h in chains]
    n_old = [n_ref[g, h:h + 1, :] for g, h in chains]
    m_old = [m_ref[g, 0:1, h:h + 1] for g, h in chains]

    v0 = 2 * H_B * LANE
    q = [xin_ref[g, :, h * LANE:(h + 1) * LANE] * (DK_B ** -0.5) for g, h in chains]
    k = [xin_ref[g, :, (H_B + h) * LANE:(H_B + h + 1) * LANE] for g, h in chains]
    v = [xin_ref[g, :, v0 + h * DV_B:v0 + (h + 1) * DV_B] for g, h in chains]
    ig_c, b_c, b_last, d_log = [], [], [], []
    for g in range(G):
        gact = gate_ref[g]
        gact_t = gatet_ref[g, 0]
        cum_c = _dot_hi(tril_f, gact)
        cum_r = _dot_hi(gact_t, triu_f)
        for h in range(H_B):
            ig_c.append(gact[:, 8 + h:9 + h])
            b_c.append(cum_c[:, 12 + h:13 + h])
            b_last.append(cum_c[L - 1:L, 12 + h:13 + h])
            d_log.append(jnp.where(tril, cum_c[:, 12 + h:13 + h] - cum_r[12 + h:13 + h, :]
                                   + gact_t[8 + h:9 + h, :], NEG))
        yield
    qk = [_dot(q[i], k[i], _NT) for i in nc]
    yield
    qc = [_dot(q[i], c_old[i]) for i in nc]
    yield
    inter = [b_c[i] + m_old[i] for i in nc]
    m_t = [jnp.maximum(inter[i], jnp.max(d_log[i], axis=-1, keepdims=True)) for i in nc]
    yield
    s = [qk[i] * jnp.exp(d_log[i] - m_t[i]) for i in nc]
    e_inter = [jnp.exp(inter[i] - m_t[i]) for i in nc]
    yield
    sv = [_dot(s[i], v[i]) for i in nc]
    yield
    m_new = [m_t[i][L - 1:L, :] for i in nc]
    kw = [k[i] * jnp.exp(b_last[i] - b_c[i] + ig_c[i] - m_new[i]) for i in nc]
    f_tot = [jnp.exp(b_last[i] + m_old[i] - m_new[i]) for i in nc]
    yield
    c_new = [f_tot[i] * c_old[i] + _dot(kw[i], v[i], _TN) for i in nc]
    yield
    n_new = [f_tot[i] * n_old[i] + jnp.sum(kw[i], axis=0, keepdims=True) for i in nc]
    qn = [jnp.sum(q[i] * n_old[i], axis=-1, keepdims=True) for i in nc]
    yield
    ssum = [jnp.sum(s[i], axis=-1, keepdims=True) for i in nc]
    yield
    den = [jnp.maximum(jnp.abs(e_inter[i] * qn[i] + ssum[i]), jnp.exp(-m_t[i])) for i in nc]
    hh = [(e_inter[i] * qc[i] + sv[i]) / den[i] for i in nc]
    yield
    ms = [jnp.mean(hh[i] * hh[i], axis=-1, keepdims=True) for i in nc]
    hn = [hh[i] * lax.rsqrt(ms[i] + EPS) for i in nc]
    yield
    for i, (g, h) in enumerate(chains):
        og = xin_ref[g, :, v0 + H_B * DV_B + h * DV_B:v0 + H_B * DV_B + (h + 1) * DV_B]
        mix_ref[g, :, h * DV_B:(h + 1) * DV_B] = hn[i] * ng_ref[h:h + 1, :] * _sigmoid(og)
    yield
    for i, (g, h) in enumerate(chains):
        c_ref[g, h] = c_new[i]
        n_ref[g, h:h + 1, :] = n_new[i]
        m_ref[g, 0:1, h:h + 1] = m_new[i]
        cnew_ref[g, h] = c_new[i][0:DK_B, :]
        nnew_ref[g, h:h + 1, :] = n_new[i][:, 0:DK_B]
        mnew_ref[g, 0:1, h:h + 1] = m_new[i]


def _mlstm(ml_in, gates, gates_t, ng, *, L, G, state=None):
    n_seq, T, _ = ml_in.shape
    n_c = T // L
    has_state = state is not None
    in_specs = [
        pl.BlockSpec((G, L, MLP_W), lambda b, c: (b, c, 0)),
        pl.BlockSpec((G, L, LANE), lambda b, c: (b, c, 0)),
        pl.BlockSpec((G, 1, N_GATE, L), lambda b, c: (b, c, 0, 0)),
        pl.BlockSpec((H_B, DV_B), lambda b, c: (0, 0)),
    ]
    args = [ml_in, gates, gates_t, ng]
    if has_state:
        c0, n0, m0 = state
        in_specs += [
            pl.BlockSpec((G, H_B, DK_B, DV_B), lambda b, c: (b, 0, 0, 0)),
            pl.BlockSpec((G, H_B, DK_B), lambda b, c: (b, 0, 0)),
            pl.BlockSpec((G, 1, H_B), lambda b, c: (b, 0, 0)),
        ]
        args += [c0, n0, m0]
    return pl.pallas_call(
        functools.partial(_mlstm_kernel, L=L, G=G, has_state=has_state),
        grid=(n_seq // G, n_c),
        in_specs=in_specs,
        out_specs=[
            pl.BlockSpec((G, L, H_B * DV_B), lambda b, c: (b, c, 0)),
            pl.BlockSpec((G, H_B, DK_B, DV_B), lambda b, c: (b, 0, 0, 0)),
            pl.BlockSpec((G, H_B, DK_B), lambda b, c: (b, 0, 0)),
            pl.BlockSpec((G, 1, H_B), lambda b, c: (b, 0, 0)),
        ],
        out_shape=[
            jax.ShapeDtypeStruct((n_seq, T, H_B * DV_B), F32),
            jax.ShapeDtypeStruct((n_seq, H_B, DK_B, DV_B), F32),
            jax.ShapeDtypeStruct((n_seq, H_B, DK_B), F32),
            jax.ShapeDtypeStruct((n_seq, 1, H_B), F32),
        ],
        scratch_shapes=[pltpu.VMEM((G, H_B, LANE, DV_B), F32), pltpu.VMEM((G, SUBLANE, LANE), F32),
                        pltpu.VMEM((G, SUBLANE, LANE), F32)],
        compiler_params=pltpu.CompilerParams(dimension_semantics=("parallel", "arbitrary"),
                                             vmem_limit_bytes=VMEM_LIMIT),
        name=f"mlstm_L{L}",
    )(*args)


def _mixers_prompt_kernel(gdn_ref, gate_ref, gatet_ref, nga_ref, ml_ref, ngb_ref,
                          mixa_ref, snew_ref, mixb_ref, cnew_ref, nnew_ref, mnew_ref,
                          s_ref, c_ref, n_ref, m_ref, *, L, G):
    _run_interleaved(
        _gdn_stages(gdn_ref, gate_ref, gatet_ref, nga_ref, mixa_ref, snew_ref, s_ref, L=L, G=G, has_state=False),
        _mlstm_stages(ml_ref, gate_ref, gatet_ref, ngb_ref, mixb_ref, cnew_ref, nnew_ref, mnew_ref,
                      c_ref, n_ref, m_ref, L=L, G=G, has_state=False))


def _mixers_prompt(gdn_in, ml_in, gates, gates_t, ng_a, ng_b, *, L, G):
    n_seq, T, _ = gdn_in.shape
    n_c = T // L

    def seq_blk(*tail):
        return pl.BlockSpec((G,) + tail, lambda b, c: (b,) + (0,) * len(tail))

    def tok_blk(width):
        return pl.BlockSpec((G, L, width), lambda b, c: (b, c, 0))

    return pl.pallas_call(
        functools.partial(_mixers_prompt_kernel, L=L, G=G),
        grid=(n_seq // G, n_c),
        in_specs=[
            tok_blk(GDN_W), tok_blk(LANE),
            pl.BlockSpec((G, 1, N_GATE, L), lambda b, c: (b, c, 0, 0)),
            pl.BlockSpec((1, DV_A), lambda b, c: (0, 0)),
            tok_blk(MLP_W),
            pl.BlockSpec((H_B, DV_B), lambda b, c: (0, 0)),
        ],
        out_specs=[
            tok_blk(H_A * DV_A), seq_blk(H_A, DK_A, DV_A),
            tok_blk(H_B * DV_B), seq_blk(H_B, DK_B, DV_B), seq_blk(H_B, DK_B), seq_blk(1, H_B),
        ],
        out_shape=[
            jax.ShapeDtypeStruct((n_seq, T, H_A * DV_A), F32),
            jax.ShapeDtypeStruct((n_seq, H_A, DK_A, DV_A), F32),
            jax.ShapeDtypeStruct((n_seq, T, H_B * DV_B), F32),
            jax.ShapeDtypeStruct((n_seq, H_B, DK_B, DV_B), F32),
            jax.ShapeDtypeStruct((n_seq, H_B, DK_B), F32),
            jax.ShapeDtypeStruct((n_seq, 1, H_B), F32),
        ],
        scratch_shapes=[pltpu.VMEM((G, H_A, DK_A, DV_A), F32), pltpu.VMEM((G, H_B, LANE, DV_B), F32),
                        pltpu.VMEM((G, SUBLANE, LANE), F32), pltpu.VMEM((G, SUBLANE, LANE), F32)],
        compiler_params=pltpu.CompilerParams(dimension_semantics=("parallel", "arbitrary"),
                                             vmem_limit_bytes=VMEM_LIMIT),
        name=f"mixers_L{L}",
    )(gdn_in, gates, gates_t, ng_a, ml_in, ng_b)


def _outproj_kernel(xp_ref, xs_ref, map_ref, mas_ref, mbp_ref, mbs_ref, wo_ref, g_ref, rw_ref, rb_ref,
                    x1_ref, xsort_ref, info_ref, cpad_ref, *, n_p_tiles):
    half = H_A * DV_A
    tm = xp_ref.shape[0]

    def body(seg):
        x_ref, ma_ref, mb_ref = (xp_ref, xs_ref)[seg], (map_ref, mas_ref)[seg], (mbp_ref, mbs_ref)[seg]
        x1 = (x_ref[...] + jnp.dot(ma_ref[...].astype(BF16), wo_ref[:half, :], preferred_element_type=F32)
              + jnp.dot(mb_ref[...].astype(BF16), wo_ref[half:, :], preferred_element_type=F32))
        x1_ref[...] = x1
        hn = _rms(x1, g_ref[...])
        hn_hi = hn.astype(BF16)
        hn_lo = (hn - hn_hi.astype(F32)).astype(BF16)
        logits = (jnp.dot(hn_hi, rw_ref[0], preferred_element_type=F32)
                  + jnp.dot(hn_hi, rw_ref[1], preferred_element_type=F32)
                  + jnp.dot(hn_lo, rw_ref[0], preferred_element_type=F32)) + rb_ref[...]

        vals = logits.T[:N_EXPERTS, :]
        e_iota = lax.broadcasted_iota(jnp.int32, (N_EXPERTS, tm), 0)
        sels, tops = [], []
        for _ in range(TOP_K):
            m = jnp.max(vals, axis=0, keepdims=True)
            first = jnp.min(jnp.where(vals == m, e_iota, N_EXPERTS), axis=0, keepdims=True)
            sel = e_iota == first
            vals = jnp.where(sel, -jnp.inf, vals)
            sels.append(sel)
            tops.append(m)
        ex = [jnp.exp(t - tops[0]) for t in tops]
        den = ex[0] + ex[1] + ex[2] + ex[3]
        gates = [e / den for e in ex]
        mask = sels[0].astype(F32) + sels[1].astype(F32) + sels[2].astype(F32) + sels[3].astype(F32)
        ri = lax.broadcasted_iota(jnp.int32, (tm, tm), 0)
        ci = lax.broadcasted_iota(jnp.int32, (tm, tm), 1)
        rank = _dot(mask, (ri < ci).astype(F32))
        cnt = jnp.sum(mask, axis=1, keepdims=True)
        cpad = jnp.ceil(cnt * (1.0 / SEG_ALIGN)) * SEG_ALIGN
        cpad_b = jnp.broadcast_to(cpad, (N_EXPERTS, tm))
        er = lax.broadcasted_iota(jnp.int32, (N_EXPERTS, N_EXPERTS), 0)
        ec = lax.broadcasted_iota(jnp.int32, (N_EXPERTS, N_EXPERTS), 1)
        seg_off = _dot((er > ec).astype(F32), cpad_b)
        pos = seg_off + rank
        q = [jnp.sum(jnp.where(s, pos, 0.0), axis=0, keepdims=True) for s in sels]

        j_iota = lax.broadcasted_iota(jnp.int32, (MOE_CAP, tm), 0).astype(F32)
        perm = jnp.zeros((MOE_CAP, tm), F32)
        for kk in range(TOP_K):
            perm = jnp.where(j_iota == q[kk], 1.0, perm)
        xsorted = _dot(perm, hn)
        xsort_ref[...] = xsorted

        r_iota = lax.broadcasted_iota(jnp.int32, (LANE, tm), 0)
        info = jnp.zeros((LANE, tm), F32)
        for kk in range(TOP_K):
            info = jnp.where(r_iota == kk, q[kk], info)
            info = jnp.where(r_iota == TOP_K + kk, gates[kk], info)
        info_ref[...] = info.T
        cpad_ref[0] = cpad_b[:, :LANE]

    _for_segment(n_p_tiles, body)


def _outproj(xp, xs, ma_p, ma_s, mb_p, mb_s, w_out, g, rw, rb, tm):
    n_p, n_s = xp.shape[0], xs.shape[0]
    n = n_p + n_s
    nt = n // tm
    npt = n_p // tm
    half = H_A * DV_A
    return pl.pallas_call(
        functools.partial(_outproj_kernel, n_p_tiles=npt),
        grid=(nt,),
        in_specs=_two_segment_specs(tm, D_MODEL, npt) + _two_segment_specs(tm, half, npt)
        + _two_segment_specs(tm, half, npt) + [
            pl.BlockSpec((D_MODEL, D_MODEL), lambda i: (0, 0)),
            pl.BlockSpec((1, D_MODEL), lambda i: (0, 0)),
            pl.BlockSpec((2, D_MODEL, LANE), lambda i: (0, 0, 0)),
            pl.BlockSpec((1, LANE), lambda i: (0, 0)),
        ],
        out_specs=[
            pl.BlockSpec((tm, D_MODEL), lambda i: (i, 0)),
            pl.BlockSpec((MOE_CAP, D_MODEL), lambda i: (i, 0)),
            pl.BlockSpec((tm, LANE), lambda i: (i, 0)),
            pl.BlockSpec((1, N_EXPERTS, LANE), lambda i: (i, 0, 0)),
        ],
        out_shape=[
            jax.ShapeDtypeStruct((n, D_MODEL), F32),
            jax.ShapeDtypeStruct((nt * MOE_CAP, D_MODEL), F32),
            jax.ShapeDtypeStruct((n, LANE), F32),
            jax.ShapeDtypeStruct((nt, N_EXPERTS, LANE), F32),
        ],
        compiler_params=pltpu.CompilerParams(dimension_semantics=("arbitrary",), vmem_limit_bytes=VMEM_LIMIT),
        name="outproj",
    )(xp, xs, ma_p, ma_s, mb_p, mb_s, w_out, g, rw, rb)


def _expert_kernel(be_ref, bj_ref, tf_ref, tl_ref, cov_ref, nu_ref, vt_ref, ct_ref, lt_ref, nx_ref, ws_ref,
                   xs_hbm, wgu_hbm, bgu_ref, wd_hbm, bd_ref, ys_hbm,
                   xbuf, ybuf, gsem, ssem, wgu_st, wd_st, wsem, wgu_bf, wd_bf, *, nt):
    b = pl.program_id(0)
    n_used = nu_ref[0]
    slot = b % 2

    def start_pieces(bb, copy, s):
        e = be_ref[bb]
        base = bj_ref[bb] * MOE_BLK

        def body(t, carry):
            k = e * nt + t
            lo = jnp.maximum(vt_ref[k], base)
            ln = jnp.minimum(ct_ref[k], base + MOE_BLK) - lo

            @pl.when(ln > 0)
            def _():
                copy(s, pl.multiple_of(lt_ref[k] + lo, SEG_ALIGN), pl.multiple_of(lo - base, SEG_ALIGN),
                     pl.multiple_of(ln, SEG_ALIGN)).start()
            return carry

        lax.fori_loop(tf_ref[bb], tl_ref[bb] + 1, body, 0)

    def weight_copies(e):
        return (pltpu.make_async_copy(wgu_hbm.at[e], wgu_st, wsem.at[0]),
                pltpu.make_async_copy(wd_hbm.at[e], wd_st, wsem.at[1]))

    def cast_weights(p):
        wgu_bf[p] = wgu_st[...].astype(BF16)
        wd_bf[p] = wd_st[...].astype(BF16)

    def gather_copy(s, src, dst, size):
        return pltpu.make_async_copy(xs_hbm.at[pl.ds(src, size)], xbuf.at[s, pl.ds(dst, size)], gsem.at[s])

    def scatter_copy(s, src, dst, size):
        return pltpu.make_async_copy(ybuf.at[s, pl.ds(dst, size)], ys_hbm.at[pl.ds(src, size)], ssem.at[s])

    def wait_rows(count, copy, s):
        @pl.when(count > 0)
        def _():
            copy(s, 0, 0, pl.multiple_of(count, SEG_ALIGN)).wait()

    @pl.when(b == 0)
    def _():
        xbuf[...] = jnp.zeros_like(xbuf)
        start_pieces(0, gather_copy, 0)

    @pl.when(b + 1 < n_used)
    def _():
        start_pieces(b + 1, gather_copy, 1 - slot)

    @pl.when(b < n_used)
    def _():
        e = be_ref[b]
        first = jnp.logical_or(b == 0, be_ref[jnp.maximum(b - 1, 0)] != e)
        last = jnp.logical_or(b == n_used - 1, be_ref[jnp.minimum(b + 1, n_used - 1)] != e)
        has_next = nx_ref[b] < N_EXPERTS
        p = ws_ref[b]

        @pl.when(b == 0)
        def _():
            for cp in weight_copies(e):
                cp.start()
            for cp in weight_copies(e):
                cp.wait()
            cast_weights(p)

        @pl.when(jnp.logical_and(first, has_next))
        def _():
            for cp in weight_copies(nx_ref[b]):
                cp.start()

        wait_rows(cov_ref[b], gather_copy, slot)
        hgu = jnp.dot(xbuf[slot].astype(BF16), wgu_bf[p], preferred_element_type=F32) + bgu_ref[0]
        gate = jnp.minimum(hgu[:, :D_FF], SWIGLU_LIMIT)
        up = jnp.clip(hgu[:, D_FF:], -SWIGLU_LIMIT, SWIGLU_LIMIT)
        act = (up + 1.0) * gate * _sigmoid(SWIGLU_ALPHA * gate)
        y = jnp.dot(act.astype(BF16), wd_bf[p], preferred_element_type=F32) + bd_ref[0]

        @pl.when(jnp.logical_and(last, has_next))
        def _():
            for cp in weight_copies(nx_ref[b]):
                cp.wait()
            cast_weights(1 - p)

        @pl.when(b >= 2)
        def _():
            wait_rows(cov_ref[jnp.maximum(b - 2, 0)], scatter_copy, slot)

        ybuf[slot] = y
        start_pieces(b, scatter_copy, slot)

        @pl.when(b == n_used - 1)
        def _():
            wait_rows(cov_ref[b], scatter_copy, slot)
            wait_rows(jnp.where(b >= 1, cov_ref[jnp.maximum(b - 1, 0)], 0), scatter_copy, 1 - slot)


def _experts(tables, xs, w_gu, b_gu, w_down, b_down, nt):
    nb = tables[0].shape[0]

    def bias_blk(b, *t):
        return (t[0][jnp.minimum(b, t[5][0] - 1)], 0, 0)

    grid_spec = pltpu.PrefetchScalarGridSpec(
        num_scalar_prefetch=len(tables),
        grid=(nb,),
        in_specs=[
            pl.BlockSpec(memory_space=pl.ANY),
            pl.BlockSpec(memory_space=pl.ANY),
            pl.BlockSpec((1, 1, 2 * D_FF), bias_blk),
            pl.BlockSpec(memory_space=pl.ANY),
            pl.BlockSpec((1, 1, D_MODEL), bias_blk),
        ],
        out_specs=pl.BlockSpec(memory_space=pl.ANY),
        scratch_shapes=[
            pltpu.VMEM((2, MOE_BLK, D_MODEL), F32),
            pltpu.VMEM((2, MOE_BLK, D_MODEL), F32),
            pltpu.SemaphoreType.DMA((2,)),
            pltpu.SemaphoreType.DMA((2,)),
            pltpu.VMEM((D_MODEL, 2 * D_FF), F32),
            pltpu.VMEM((D_FF, D_MODEL), F32),
            pltpu.SemaphoreType.DMA((2,)),
            pltpu.VMEM((2, D_MODEL, 2 * D_FF), BF16),
            pltpu.VMEM((2, D_FF, D_MODEL), BF16),
        ],
    )
    return pl.pallas_call(
        functools.partial(_expert_kernel, nt=nt),
        grid_spec=grid_spec,
        out_shape=jax.ShapeDtypeStruct(xs.shape, xs.dtype),
        input_output_aliases={len(tables): 0},
        compiler_params=pltpu.CompilerParams(dimension_semantics=("arbitrary",), vmem_limit_bytes=VMEM_LIMIT),
        name="experts",
    )(*tables, xs, w_gu, b_gu, w_down, b_down)


def _combine_kernel(ys_ref, info_ref, x1_ref, pp_ref, ps_ref, gple_ref, wg_ref, wp_ref, gfin_ref,
                    outp_ref, outs_ref, *, n_p_tiles):
    tm = x1_ref.shape[0]

    def body(seg):
        p_ref, out_ref = (pp_ref, ps_ref)[seg], (outp_ref, outs_ref)[seg]
        info = info_ref[...]
        j_iota = lax.broadcasted_iota(jnp.int32, (tm, MOE_CAP), 1).astype(F32)
        gmat = jnp.zeros((tm, MOE_CAP), F32)
        for kk in range(TOP_K):
            gmat = jnp.where(j_iota == info[:, kk:kk + 1], info[:, TOP_K + kk:TOP_K + kk + 1], gmat)
        x2 = x1_ref[...] + jnp.dot(gmat.astype(BF16), ys_ref[...].astype(BF16), preferred_element_type=F32)
        hn = _rms(x2, gple_ref[...]).astype(BF16)
        gate = _sigmoid(jnp.dot(hn, wg_ref[...], preferred_element_type=F32))
        pe = jnp.dot(p_ref[...].astype(BF16), wp_ref[...], preferred_element_type=F32)
        x3 = x2 + gate * pe
        out_ref[...] = _rms(x3, gfin_ref[...])

    _for_segment(n_p_tiles, body)


def _combine(ys, info, x1, pp, ps, g_ple, w_gate, w_p, g_fin, tm):
    n_p, n_s = pp.shape[0], ps.shape[0]
    n = n_p + n_s
    nt = n // tm
    npt = n_p // tm
    return pl.pallas_call(
        functools.partial(_combine_kernel, n_p_tiles=npt),
        grid=(nt,),
        in_specs=[
            pl.BlockSpec((MOE_CAP, D_MODEL), lambda i: (i, 0)),
            pl.BlockSpec((tm, LANE), lambda i: (i, 0)),
            pl.BlockSpec((tm, D_MODEL), lambda i: (i, 0)),
        ] + _two_segment_specs(tm, PLE_DIM, npt) + [
            pl.BlockSpec((1, D_MODEL), lambda i: (0, 0)),
            pl.BlockSpec((D_MODEL, D_MODEL), lambda i: (0, 0)),
            pl.BlockSpec((PLE_DIM, D_MODEL), lambda i: (0, 0)),
            pl.BlockSpec((1, D_MODEL), lambda i: (0, 0)),
        ],
        out_specs=_two_segment_specs(tm, D_MODEL, npt),
        out_shape=[jax.ShapeDtypeStruct((n_p, D_MODEL), F32), jax.ShapeDtypeStruct((n_s, D_MODEL), F32)],
        compiler_params=pltpu.CompilerParams(dimension_semantics=("arbitrary",), vmem_limit_bytes=VMEM_LIMIT),
        name="combine",
    )(ys, info, x1, pp, ps, g_ple, w_gate, w_p, g_fin)


def _block_tables(seg_len, nb):
    nt = seg_len.shape[0]
    seg_off = jnp.cumsum(seg_len, axis=1) - seg_len
    seg_end = jnp.cumsum(seg_len, axis=0).T
    seg_start = seg_end - seg_len.T
    n_rows = seg_end[:, -1]
    n_blk = (n_rows + MOE_BLK - 1) // MOE_BLK
    blk_end = jnp.cumsum(n_blk)
    b = jnp.arange(nb, dtype=jnp.int32)
    block_e = jnp.minimum(jnp.sum((blk_end[None, :] <= b[:, None]).astype(jnp.int32), axis=1), N_EXPERTS - 1)
    idx = jnp.where(n_blk > 0, jnp.arange(N_EXPERTS, dtype=jnp.int32), N_EXPERTS)
    nxt = jnp.concatenate([lax.cummin(idx, axis=0, reverse=True)[1:], jnp.full((1,), N_EXPERTS, jnp.int32)])
    parity = (jnp.cumsum((n_blk > 0).astype(jnp.int32)) - 1) % 2
    per_e = jnp.concatenate([jnp.stack([blk_end - n_blk, n_rows, nxt, parity], axis=1), seg_start, seg_end],
                            axis=1).astype(F32)
    onehot = (block_e[:, None] == jnp.arange(N_EXPERTS, dtype=jnp.int32)[None, :]).astype(F32)
    per_b = jnp.dot(onehot, per_e, precision=HI).astype(jnp.int32)
    block_j = b - per_b[:, 0]
    base = block_j * MOE_BLK
    t_first = jnp.sum((per_b[:, 4 + nt:] <= base[:, None]).astype(jnp.int32), axis=1)
    t_last = jnp.sum((per_b[:, 4:4 + nt] < (base + MOE_BLK)[:, None]).astype(jnp.int32), axis=1) - 1
    cover = jnp.clip(per_b[:, 1] - base, 0, MOE_BLK)
    seg_shift = (jnp.arange(nt, dtype=jnp.int32)[:, None] * MOE_CAP + seg_off).T - seg_start
    tables = (block_e, block_j, t_first, t_last, cover, blk_end[-1:], seg_start.reshape(-1),
              seg_end.reshape(-1), seg_shift.reshape(-1), per_b[:, 2], per_b[:, 3])
    return tuple(t.astype(jnp.int32) for t in tables)


def _rearranged_in_weights(w_in):
    o = np.cumsum([0, CONV_CH, H_A * DV_A, H_A, H_A, H_B * DK_B, H_B * DK_B, H_B * DV_B, H_B * DV_B, H_B, H_B])
    conv_in, z_a, a_a, b_a, q_b, k_b, v_b, o_b, i_b, f_b = (w_in[:, int(o[j]):int(o[j + 1])] for j in range(10))
    zpad = jnp.zeros((D_MODEL, LANE - DK_B), w_in.dtype)

    def pad_heads(w):
        return jnp.concatenate([jnp.concatenate([w[:, h * DK_B:(h + 1) * DK_B], zpad], axis=1) for h in range(H_B)],
                               axis=1)

    small = jnp.concatenate([a_a, b_a, i_b, f_b], axis=1)
    w_gdn = w_in[:, :GDN_W]
    w_rest = jnp.concatenate([pad_heads(q_b), pad_heads(k_b), v_b, o_b,
                              small, jnp.zeros((D_MODEL, LANE - N_GATE), w_in.dtype)], axis=1)
    return w_gdn.astype(BF16), w_rest.astype(BF16), small.T.astype(BF16)


def _gate_params(a_log, dt_bias, i_bias, f_bias):
    z4 = jnp.zeros((4,), F32)
    alog = jnp.concatenate([a_log.astype(F32), z4, z4, z4])
    bias = jnp.concatenate([dt_bias.astype(F32), z4, i_bias.astype(F32), f_bias.astype(F32)])
    pad = jnp.zeros((LANE - N_GATE,), F32)
    pcol = jnp.zeros((SUBLANE, LANE), F32).at[0].set(jnp.concatenate([alog, pad])).at[1].set(
        jnp.concatenate([bias, pad]))
    prow = jnp.zeros((N_GATE, LANE), F32).at[:, 0].set(alog).at[:, 1].set(bias)
    return pcol, prow


def kernel(x_prompt, x_sample, p_prompt, p_sample, state_conv, state_gdn, state_mlstm_c, state_mlstm_n, state_mlstm_m, norm_attn_g, w_in, conv_w, gdn_a_log, gdn_dt_bias, gdn_norm_g, mlstm_i_bias, mlstm_f_bias, mlstm_norm_g, w_out, norm_moe_g, router_w, router_b, expert_w_gu, expert_b_gu, expert_w_down, expert_b_down, norm_ple_g, ple_gate_w, ple_w, final_norm_g):
    bp, tp, _ = x_prompt.shape
    bs, ts, _ = x_sample.shape
    n_p, n_s = bp * tp, bs * ts
    n = n_p + n_s
    lp, ls = min(tp, CHUNK), min(ts, CHUNK)
    tm = 256
    gp = 4 if bp % 4 == 0 else 1
    gs = 8 if bs % 8 == 0 else 1
    assert tp % lp == 0 and ts % ls == 0 and tp % tm == 0 and n_s % tm == 0 and ls % SUBLANE == 0

    xp = x_prompt.reshape(n_p, D_MODEL)
    xs = x_sample.reshape(n_s, D_MODEL)

    w_gdn, w_rest, ws_t = _rearranged_in_weights(w_in[0])
    pcol, prow = _gate_params(gdn_a_log[0], gdn_dt_bias[0], mlstm_i_bias[0], mlstm_f_bias[0])
    cw = jnp.zeros((SUBLANE, CONV_CH), F32).at[:CONV_W].set(conv_w[0].astype(F32))
    gdn_p, gdn_s, ml_p, ml_s, gate_p, gate_s, gatet_p, gatet_s, conv_p = _inproj(
        xp, xs, norm_attn_g[0].reshape(1, D_MODEL), w_gdn, w_rest, ws_t, pcol, prow, cw, tm, bp)
    gt_p = gatet_p.reshape(N_GATE, bp, tp // lp, lp).transpose(1, 2, 0, 3)
    gt_s = gatet_s.reshape(N_GATE, bs, ts // ls, ls).transpose(1, 2, 0, 3)

    ng_a = gdn_norm_g[0].reshape(1, DV_A).astype(F32)
    ng_b = mlstm_norm_g[0].reshape(H_B, DV_B).astype(F32)
    ma_p, gdn_st_p, mb_p, c_p, nn_p, m_p = _mixers_prompt(
        gdn_p.reshape(bp, tp, GDN_W), ml_p.reshape(bp, tp, MLP_W), gate_p.reshape(bp, tp, LANE), gt_p, ng_a, ng_b,
        L=lp, G=gp)
    ma_s, conv_s, gdn_st_s = _gdn(gdn_s.reshape(bs, ts, GDN_W), gate_s.reshape(bs, ts, LANE), gt_s, cw, ng_a,
                                  L=ls, G=gs, state=(state_conv[0], state_gdn[0]))
    mb_s, c_s, nn_s, m_s = _mlstm(ml_s.reshape(bs, ts, MLP_W), gate_s.reshape(bs, ts, LANE), gt_s, ng_b,
                                  L=ls, G=gs,
                                  state=(state_mlstm_c[0], state_mlstm_n[0], state_mlstm_m[0].reshape(bs, 1, H_B)))
    half = H_A * DV_A

    rw = jnp.zeros((D_MODEL, LANE), F32).at[:, :N_EXPERTS].set(router_w[0])
    rw_hi = rw.astype(BF16)
    rw = jnp.stack([rw_hi, (rw - rw_hi.astype(F32)).astype(BF16)])
    rb = jnp.full((1, LANE), NEG, F32).at[0, :N_EXPERTS].set(router_b[0])
    x1, x_sorted, info, seg_len = _outproj(xp, xs, ma_p.reshape(n_p, half), ma_s.reshape(n_s, half),
                                           mb_p.reshape(n_p, half), mb_s.reshape(n_s, half),
                                           w_out[0].astype(BF16), norm_moe_g[0].reshape(1, D_MODEL), rw, rb, MOE_TM)

    nt = n // MOE_TM
    nb = -(-(n * TOP_K + nt * N_EXPERTS * (SEG_ALIGN - 1)) // MOE_BLK) + N_EXPERTS
    tables = _block_tables(seg_len[:, :, 0].astype(jnp.int32), nb)
    y_sorted = _experts(tables, x_sorted, expert_w_gu[0], expert_b_gu[0].reshape(N_EXPERTS, 1, 2 * D_FF),
                        expert_w_down[0], expert_b_down[0].reshape(N_EXPERTS, 1, D_MODEL), nt)
    y_p, y_s = _combine(y_sorted, info, x1, p_prompt[0].reshape(n_p, PLE_DIM),
                        p_sample[0].reshape(n_s, PLE_DIM), norm_ple_g[0].reshape(1, D_MODEL),
                        ple_gate_w[0].astype(BF16), ple_w[0].astype(BF16), final_norm_g.reshape(1, D_MODEL), MOE_TM)

    return (y_p.reshape(bp, tp, D_MODEL), y_s.reshape(bs, ts, D_MODEL),
            conv_p[None], gdn_st_p[None], c_p[None], nn_p[None], m_p.reshape(1, bp, H_B),
            conv_s[None], gdn_st_s[None], c_s[None], nn_s[None], m_s.reshape(1, bs, H_B))
```

```python
import functools

import numpy as np
import jax
import jax.numpy as jnp
from jax import lax
from jax.experimental import pallas as pl
from jax.experimental.pallas import tpu as pltpu

F32 = jnp.float32
BF16 = jnp.bfloat16

D_MODEL = 1024
H_A, DK_A, DV_A = 4, 128, 128
H_B, DK_B, DV_B = 4, 64, 128
CONV_W = 4
CONV_CH = H_A * (2 * DK_A + DV_A)
N_EXPERTS = 32
TOP_K = 4
D_FF = 1024
SWIGLU_LIMIT = 7.0
SWIGLU_ALPHA = 1.702
PLE_DIM = 256
EPS = 1e-6
NEG = -1e30
CHUNK = 64

LANE = 128
SUBLANE = 8
GDN_W = CONV_CH + H_A * DV_A
MLP_W = 2 * H_B * LANE + 2 * H_B * DV_B
N_GATE = 16

VMEM_LIMIT = 48 * 1024 * 1024

MOE_TM = 256
MOE_BLK = 512
SEG_ALIGN = SUBLANE
MOE_CAP = -(-(MOE_TM * TOP_K + N_EXPERTS * (SEG_ALIGN - 1)) // LANE) * LANE

HI = lax.Precision.HIGHEST

_NN = (((1,), (0,)), ((), ()))
_NT = (((1,), (1,)), ((), ()))
_TN = (((0,), (0,)), ((), ()))


def _dot(a, b, dims=_NN):
    return lax.dot_general(a.astype(BF16), b.astype(BF16), dims, preferred_element_type=F32)


def _dot_hi(a, b, dims=_NN):
    return lax.dot_general(a, b, dims, precision=HI, preferred_element_type=F32)


def _rms(x, g):
    return x * lax.rsqrt(jnp.mean(x * x, axis=-1, keepdims=True) + EPS) * g


def _softplus(t):
    return jnp.maximum(t, 0.0) + jnp.log1p(jnp.exp(-jnp.abs(t)))


def _sigmoid(t):
    return 1.0 / (1.0 + jnp.exp(-t))


def _silu(t):
    return t * _sigmoid(t)


def _activate_gates(raw, idx, alog, bias):
    t = raw + bias
    g = -jnp.exp(alog) * _softplus(t)
    beta = _sigmoid(t)
    lf = -_softplus(-t)
    return jnp.where(idx < 4, g, jnp.where(idx < 8, beta, jnp.where(idx < 12, t, lf)))


def _two_segment_specs(tm, width, n_p_tiles):
    return [pl.BlockSpec((tm, width), lambda i: (jnp.minimum(i, n_p_tiles - 1), 0)),
            pl.BlockSpec((tm, width), lambda i: (jnp.maximum(i - n_p_tiles, 0), 0))]


def _for_segment(n_p_tiles, body):
    i = pl.program_id(0)

    @pl.when(i < n_p_tiles)
    def _():
        body(0)

    @pl.when(i >= n_p_tiles)
    def _():
        body(1)


def _gdn_preactivate(raw, xc_ref, cw_ref, first_of_seq):
    tm = raw.shape[0]
    xc_ref[0:SUBLANE, :] = jnp.where(first_of_seq, 0.0, xc_ref[tm:tm + SUBLANE, :])
    xc_ref[SUBLANE:SUBLANE + tm, :] = raw[:, :CONV_CH]
    base = SUBLANE - (CONV_W - 1)
    conv = xc_ref[base:base + tm, :] * cw_ref[0:1, :]
    for j in range(1, CONV_W):
        conv = conv + xc_ref[base + j:base + j + tm, :] * cw_ref[j:j + 1, :]
    act = _silu(conv)
    parts = []
    for h in range(H_A):
        qh = act[:, h * DK_A:(h + 1) * DK_A]
        parts.append(qh * (lax.rsqrt(jnp.sum(qh * qh, axis=-1, keepdims=True) + EPS) * (DK_A ** -0.5)))
    for h in range(H_A):
        kh = act[:, H_A * DK_A + h * DK_A:H_A * DK_A + (h + 1) * DK_A]
        parts.append(kh * lax.rsqrt(jnp.sum(kh * kh, axis=-1, keepdims=True) + EPS))
    parts.append(act[:, 2 * H_A * DK_A:])
    parts.append(_silu(raw[:, CONV_CH:]))
    new_conv = xc_ref[SUBLANE + tm - (CONV_W - 1):SUBLANE + tm, :]
    return jnp.concatenate(parts, axis=-1), new_conv


def _inproj_kernel(xp_ref, xs_ref, g_ref, wa_ref, wb_ref, wst_ref, pc_ref, pr_ref, cw_ref,
                   gdnp_ref, gdns_ref, mlp_ref, mls_ref, gatep_ref, gates_ref, gatetp_ref, gatets_ref, cnew_ref,
                   xc_ref, raw_ref, *, n_p_tiles, tiles_per_seq):
    tm = xp_ref.shape[0]
    i = pl.program_id(0)

    @pl.when(i == 0)
    def _():
        xc_ref[...] = jnp.zeros_like(xc_ref)
        raw_ref[...] = jnp.zeros_like(raw_ref)

    def preactivate_previous_tile():
        pre, new_conv = _gdn_preactivate(raw_ref[...], xc_ref, cw_ref, (i - 1) % tiles_per_seq == 0)
        gdnp_ref[...] = pre
        cnew_ref[0] = new_conv

    def body(seg):
        x_ref = (xp_ref, xs_ref)[seg]
        ml_ref = (mlp_ref, mls_ref)[seg]
        gate_ref, gatet_ref = (gatep_ref, gates_ref)[seg], (gatetp_ref, gatets_ref)[seg]
        if seg == 0:
            preactivate_previous_tile()
        else:
            pl.when(i == n_p_tiles)(preactivate_previous_tile)
        hn = _rms(x_ref[...], g_ref[...]).astype(BF16)
        raw_gdn = jnp.dot(hn, wa_ref[...], preferred_element_type=F32)
        if seg == 0:
            raw_ref[...] = raw_gdn
        else:
            gdns_ref[...] = raw_gdn
        ml_ref[...] = jnp.dot(hn, wb_ref[:, :MLP_W], preferred_element_type=F32)
        raw = jnp.dot(hn, wb_ref[:, MLP_W:], preferred_element_type=F32)
        lane = lax.broadcasted_iota(jnp.int32, (tm, LANE), 1)
        gate_ref[...] = _activate_gates(raw, lane, pc_ref[0:1, :], pc_ref[1:2, :])
        raw_t = lax.dot_general(wst_ref[...], hn, _NT, preferred_element_type=F32)
        row = lax.broadcasted_iota(jnp.int32, (N_GATE, tm), 0)
        gatet_ref[...] = _activate_gates(raw_t, row, pr_ref[:, 0:1], pr_ref[:, 1:2])

    _for_segment(n_p_tiles, body)


def _inproj(xp, xs, g, w_gdn, w_rest, ws_t, pcol, prow, cw, tm, n_seq_p):
    n_p, n_s = xp.shape[0], xs.shape[0]
    npt = n_p // tm
    tiles_per_seq = npt // n_seq_p

    def out2(width):
        return _two_segment_specs(tm, width, npt)

    def shp2(width):
        return [jax.ShapeDtypeStruct((n_p, width), F32), jax.ShapeDtypeStruct((n_s, width), F32)]

    def prev_tile(i):
        return jnp.clip(i - 1, 0, npt - 1)

    return pl.pallas_call(
        functools.partial(_inproj_kernel, n_p_tiles=npt, tiles_per_seq=tiles_per_seq),
        grid=((n_p + n_s) // tm,),
        in_specs=_two_segment_specs(tm, D_MODEL, npt) + [
            pl.BlockSpec((1, D_MODEL), lambda i: (0, 0)),
            pl.BlockSpec((D_MODEL, GDN_W), lambda i: (0, 0)),
            pl.BlockSpec((D_MODEL, MLP_W + LANE), lambda i: (0, 0)),
            pl.BlockSpec((N_GATE, D_MODEL), lambda i: (0, 0)),
            pl.BlockSpec((SUBLANE, LANE), lambda i: (0, 0)),
            pl.BlockSpec((N_GATE, LANE), lambda i: (0, 0)),
            pl.BlockSpec((SUBLANE, CONV_CH), lambda i: (0, 0)),
        ],
        out_specs=[
            pl.BlockSpec((tm, GDN_W), lambda i: (prev_tile(i), 0)),
            pl.BlockSpec((tm, GDN_W), lambda i: (jnp.maximum(i - npt, 0), 0)),
        ] + out2(MLP_W) + out2(LANE) + [
            pl.BlockSpec((N_GATE, tm), lambda i: (0, jnp.minimum(i, npt - 1))),
            pl.BlockSpec((N_GATE, tm), lambda i: (0, jnp.maximum(i - npt, 0))),
            pl.BlockSpec((1, CONV_W - 1, CONV_CH), lambda i: (prev_tile(i) // tiles_per_seq, 0, 0)),
        ],
        out_shape=shp2(GDN_W) + shp2(MLP_W) + shp2(LANE) + [
            jax.ShapeDtypeStruct((N_GATE, n_p), F32), jax.ShapeDtypeStruct((N_GATE, n_s), F32),
            jax.ShapeDtypeStruct((n_seq_p, CONV_W - 1, CONV_CH), F32)],
        scratch_shapes=[pltpu.VMEM((tm + SUBLANE, CONV_CH), F32), pltpu.VMEM((tm, GDN_W), F32)],
        compiler_params=pltpu.CompilerParams(dimension_semantics=("arbitrary",), vmem_limit_bytes=VMEM_LIMIT),
        name="inproj",
    )(xp, xs, g, w_gdn, w_rest, ws_t, pcol, prow, cw)


def _chunk_masks(L):
    ri = lax.broadcasted_iota(jnp.int32, (L, L), 0)
    ci = lax.broadcasted_iota(jnp.int32, (L, L), 1)
    return ri >= ci, ri > ci, ri <= ci


def _run_interleaved(*stage_generators):
    live = list(stage_generators)
    while live:
        for gen in list(live):
            if next(gen, StopIteration) is StopIteration:
                live.remove(gen)


def _gdn_stages(*refs, L, G, has_state):
    if has_state:
        (xin_ref, gate_ref, gatet_ref, cw_ref, ng_ref, cst_ref, s0_ref,
         mix_ref, cnew_ref, snew_ref, xc_ref, s_ref) = refs
    else:
        xin_ref, gate_ref, gatet_ref, ng_ref, mix_ref, snew_ref, s_ref = refs
    c = pl.program_id(1)

    @pl.when(c == 0)
    def _():
        if has_state:
            xc_ref[:, 0:SUBLANE, :] = jnp.zeros((G, SUBLANE, CONV_CH), F32)
            xc_ref[:, SUBLANE - (CONV_W - 1):SUBLANE, :] = cst_ref[...]
            s_ref[...] = s0_ref[...]
        else:
            s_ref[...] = jnp.zeros_like(s_ref)

    if has_state:
        @pl.when(c > 0)
        def _():
            xc_ref[:, 0:SUBLANE, :] = xc_ref[:, L:L + SUBLANE, :]

    yield
    tril, strict, triu = _chunk_masks(L)
    tril_f, triu_f = tril.astype(F32), triu.astype(F32)
    base = SUBLANE - (CONV_W - 1)

    chains = [(g, h) for g in range(G) for h in range(H_A)]
    s_old = [s_ref[g, h] for g, h in chains]
    if has_state:
        for g in range(G):
            xc_ref[g, SUBLANE:SUBLANE + L, :] = xin_ref[g, :, :CONV_CH]

    q, k, v, beta, gc, gl, decay = [], [], [], [], [], [], []
    for g in range(G):
        if has_state:
            conv = xc_ref[g, base:base + L, :] * cw_ref[0:1, :]
            for j in range(1, CONV_W):
                conv = conv + xc_ref[g, base + j:base + j + L, :] * cw_ref[j:j + 1, :]
            cnew_ref[g] = xc_ref[g, SUBLANE + L - (CONV_W - 1):SUBLANE + L, :]
            act = _silu(conv)
        else:
            act = xin_ref[g, :, :CONV_CH]
        gact = gate_ref[g]
        cum_c = _dot_hi(tril_f, gact)
        cum_r = _dot_hi(gatet_ref[g, 0], triu_f)
        for h in range(H_A):
            q.append(act[:, h * DK_A:(h + 1) * DK_A])
            k.append(act[:, H_A * DK_A + h * DK_A:H_A * DK_A + (h + 1) * DK_A])
            v.append(act[:, 2 * H_A * DK_A + h * DV_A:2 * H_A * DK_A + (h + 1) * DV_A])
            beta.append(gact[:, 4 + h:5 + h])
            gc.append(cum_c[:, h:h + 1])
            gl.append(cum_c[L - 1:L, h:h + 1])
            gr = cum_r[h:h + 1, :]
            decay.append(jnp.where(tril, jnp.exp(jnp.where(tril, cum_c[:, h:h + 1] - gr, 0.0)), 0.0))
        yield

    nc = range(len(chains))
    if has_state:
        qss = [jnp.sum(q[i] * q[i], axis=-1, keepdims=True) for i in nc]
        kss = [jnp.sum(k[i] * k[i], axis=-1, keepdims=True) for i in nc]
        q = [q[i] * (lax.rsqrt(qss[i] + EPS) * (DK_A ** -0.5)) for i in nc]
        k = [k[i] * lax.rsqrt(kss[i] + EPS) for i in nc]
    kb = [k[i] * beta[i] for i in nc]
    egc = [jnp.exp(gc[i]) for i in nc]
    yield
    kk = [_dot(kb[i], k[i], _NT) for i in nc]
    yield
    qk = [_dot(q[i], k[i], _NT) for i in nc]
    yield
    eye = (lax.broadcasted_iota(jnp.int32, (L, L), 0) == lax.broadcasted_iota(jnp.int32, (L, L), 1)).astype(F32)
    pw = [-jnp.where(strict, kk[i] * decay[i], 0.0) for i in nc]
    t_inv = [eye + pw[i] for i in nc]
    span = 2
    while span < L:
        yield
        pw = [_dot(pw[i], pw[i]) for i in nc]
        yield
        t_inv = [t_inv[i] + _dot(t_inv[i], pw[i]) for i in nc]
        span *= 2
    yield
    sol = [_dot(t_inv[i], jnp.concatenate([v[i] * beta[i], kb[i] * egc[i]], axis=-1)) for i in nc]
    yield
    qs = [_dot(q[i] * egc[i], s_old[i]) for i in nc]
    yield
    ws = [_dot(sol[i][:, DV_A:], s_old[i]) for i in nc]
    v_new = [sol[i][:, :DV_A] - ws[i] for i in nc]
    yield
    o = [qs[i] + _dot(jnp.where(tril, qk[i] * decay[i], 0.0), v_new[i]) for i in nc]
    yield
    s_new = [s_old[i] * jnp.exp(gl[i]) + _dot(k[i] * jnp.exp(gl[i] - gc[i]), v_new[i], _TN) for i in nc]
    yield
    ms = [jnp.mean(o[i] * o[i], axis=-1, keepdims=True) for i in nc]
    on = [o[i] * lax.rsqrt(ms[i] + EPS) for i in nc]
    yield
    for i, (g, h) in enumerate(chains):
        z = xin_ref[g, :, CONV_CH + h * DV_A:CONV_CH + (h + 1) * DV_A]
        mix_ref[g, :, h * DV_A:(h + 1) * DV_A] = on[i] * ng_ref[...] * (_silu(z) if has_state else z)
    yield
    for i, (g, h) in enumerate(chains):
        s_ref[g, h] = s_new[i]
        snew_ref[g, h] = s_new[i]


def _mlstm_stages(*refs, L, G, has_state):
    if has_state:
        (xin_ref, gate_ref, gatet_ref, ng_ref, c0_ref, n0_ref, m0_ref,
         mix_ref, cnew_ref, nnew_ref, mnew_ref, c_ref, n_ref, m_ref) = refs
    else:
        (xin_ref, gate_ref, gatet_ref, ng_ref,
         mix_ref, cnew_ref, nnew_ref, mnew_ref, c_ref, n_ref, m_ref) = refs
    c = pl.program_id(1)

    @pl.when(c == 0)
    def _():
        c_ref[...] = jnp.zeros_like(c_ref)
        n_ref[...] = jnp.zeros_like(n_ref)
        m_ref[...] = jnp.zeros_like(m_ref)
        if has_state:
            c_ref[:, :, 0:DK_B, :] = c0_ref[...]
            n_ref[:, 0:H_B, 0:DK_B] = n0_ref[...]
            m_ref[:, 0:1, 0:H_B] = m0_ref[...]

    yield
    tril, _, triu = _chunk_masks(L)
    tril_f, triu_f = tril.astype(F32), triu.astype(F32)

    chains = [(g, h) for g in range(G) for h in range(H_B)]
    nc = range(len(chains))
    c_old = [c_ref[g, h] for g, h in chains]
    n_old = [n_ref[g, h:h + 1, :] for g, h in chains]
    m_old = [m_ref[g, 0:1, h:h + 1] for g, h in chains]

    v0 = 2 * H_B * LANE
    q = [xin_ref[g, :, h * LANE:(h + 1) * LANE] * (DK_B ** -0.5) for g, h in chains]
    k = [xin_ref[g, :, (H_B + h) * LANE:(H_B + h + 1) * LANE] for g, h in chains]
    v = [xin_ref[g, :, v0 + h * DV_B:v0 + (h + 1) * DV_B] for g, h in chains]
    ig_c, b_c, b_last, d_log = [], [], [], []
    for g in range(G):
        gact = gate_ref[g]
        gact_t = gatet_ref[g, 0]
        cum_c = _dot_hi(tril_f, gact)
        cum_r = _dot_hi(gact_t, triu_f)
        for h in range(H_B):
            ig_c.append(gact[:, 8 + h:9 + h])
            b_c.append(cum_c[:, 12 + h:13 + h])
            b_last.append(cum_c[L - 1:L, 12 + h:13 + h])
            d_log.append(jnp.where(tril, cum_c[:, 12 + h:13 + h] - cum_r[12 + h:13 + h, :]
                                   + gact_t[8 + h:9 + h, :], NEG))
        yield
    qk = [_dot(q[i], k[i], _NT) for i in nc]
    yield
    qc = [_dot(q[i], c_old[i]) for i in nc]
    yield
    inter = [b_c[i] + m_old[i] for i in nc]
    m_t = [jnp.maximum(inter[i], jnp.max(d_log[i], axis=-1, keepdims=True)) for i in nc]
    yield
    s = [qk[i] * jnp.exp(d_log[i] - m_t[i]) for i in nc]
    e_inter = [jnp.exp(inter[i] - m_t[i]) for i in nc]
    yield
    sv = [_dot(s[i], v[i]) for i in nc]
    yield
    m_new = [m_t[i][L - 1:L, :] for i in nc]
    kw = [k[i] * jnp.exp(b_last[i] - b_c[i] + ig_c[i] - m_new[i]) for i in nc]
    f_tot = [jnp.exp(b_last[i] + m_old[i] - m_new[i]) for i in nc]
    yield
    c_new = [f_tot[i] * c_old[i] + _dot(kw[i], v[i], _TN) for i in nc]
    yield
    n_new = [f_tot[i] * n_old[i] + jnp.sum(kw[i], axis=0, keepdims=True) for i in nc]
    qn = [jnp.sum(q[i] * n_old[i], axis=-1, keepdims=True) for i in nc]
    yield
    ssum = [jnp.sum(s[i], axis=-1, keepdims=True) for i in nc]
    yield
    den = [jnp.maximum(jnp.abs(e_inter[i] * qn[i] + ssum[i]), jnp.exp(-m_t[i])) for i in nc]
    hh = [(e_inter[i] * qc[i] + sv[i]) / den[i] for i in nc]
    yield
    ms = [jnp.mean(hh[i] * hh[i], axis=-1, keepdims=True) for i in nc]
    hn = [hh[i] * lax.rsqrt(ms[i] + EPS) for i in nc]
    yield
    for i, (g, h) in enumerate(chains):
        og = xin_ref[g, :, v0 + H_B * DV_B + h * DV_B:v0 + H_B * DV_B + (h + 1) * DV_B]
        mix_ref[g, :, h * DV_B:(h + 1) * DV_B] = hn[i] * ng_ref[h:h + 1, :] * _sigmoid(og)
    yield
    for i, (g, h) in enumerate(chains):
        c_ref[g, h] = c_new[i]
        n_ref[g, h:h + 1, :] = n_new[i]
        m_ref[g, 0:1, h:h + 1] = m_new[i]
        cnew_ref[g, h] = c_new[i][0:DK_B, :]
        nnew_ref[g, h:h + 1, :] = n_new[i][:, 0:DK_B]
        mnew_ref[g, 0:1, h:h + 1] = m_new[i]


def _mixers_kernel(*refs, L, G, has_state):
    if has_state:
        (gdn_ref, gate_ref, gatet_ref, nga_ref, ml_ref, ngb_ref, cw_ref, cst_ref, s0_ref, c0_ref, n0_ref, m0_ref,
         mixa_ref, snew_ref, mixb_ref, cnew_ref, nnew_ref, mnew_ref, convnew_ref,
         s_ref, c_ref, n_ref, m_ref, xc_ref) = refs
        gdn_refs = (gdn_ref, gate_ref, gatet_ref, cw_ref, nga_ref, cst_ref, s0_ref,
                    mixa_ref, convnew_ref, snew_ref, xc_ref, s_ref)
        ml_refs = (ml_ref, gate_ref, gatet_ref, ngb_ref, c0_ref, n0_ref, m0_ref,
                   mixb_ref, cnew_ref, nnew_ref, mnew_ref, c_ref, n_ref, m_ref)
    else:
        (gdn_ref, gate_ref, gatet_ref, nga_ref, ml_ref, ngb_ref,
         mixa_ref, snew_ref, mixb_ref, cnew_ref, nnew_ref, mnew_ref, s_ref, c_ref, n_ref, m_ref) = refs
        gdn_refs = (gdn_ref, gate_ref, gatet_ref, nga_ref, mixa_ref, snew_ref, s_ref)
        ml_refs = (ml_ref, gate_ref, gatet_ref, ngb_ref, mixb_ref, cnew_ref, nnew_ref, mnew_ref, c_ref, n_ref, m_ref)
    _run_interleaved(_gdn_stages(*gdn_refs, L=L, G=G, has_state=has_state),
                     _mlstm_stages(*ml_refs, L=L, G=G, has_state=has_state))


def _mixers(gdn_in, ml_in, gates, gates_t, ng_a, ng_b, *, L, G, cw=None, state=None):
    n_seq, T, _ = gdn_in.shape
    n_c = T // L
    has_state = state is not None

    def seq_blk(*tail):
        return pl.BlockSpec((G,) + tail, lambda b, c: (b,) + (0,) * len(tail))

    def tok_blk(width):
        return pl.BlockSpec((G, L, width), lambda b, c: (b, c, 0))

    def seq_shape(*tail):
        return jax.ShapeDtypeStruct((n_seq,) + tail, F32)

    state_specs = [seq_blk(H_A, DK_A, DV_A), seq_blk(H_B, DK_B, DV_B), seq_blk(H_B, DK_B), seq_blk(1, H_B)]
    state_shapes = [seq_shape(H_A, DK_A, DV_A), seq_shape(H_B, DK_B, DV_B), seq_shape(H_B, DK_B), seq_shape(1, H_B)]
    conv_spec, conv_shape = seq_blk(CONV_W - 1, CONV_CH), seq_shape(CONV_W - 1, CONV_CH)
    in_specs = [
        tok_blk(GDN_W), tok_blk(LANE),
        pl.BlockSpec((G, 1, N_GATE, L), lambda b, c: (b, c, 0, 0)),
        pl.BlockSpec((1, DV_A), lambda b, c: (0, 0)),
        tok_blk(MLP_W),
        pl.BlockSpec((H_B, DV_B), lambda b, c: (0, 0)),
    ]
    args = [gdn_in, gates, gates_t, ng_a, ml_in, ng_b]
    out_specs = [tok_blk(H_A * DV_A), state_specs[0], tok_blk(H_B * DV_B)] + state_specs[1:]
    out_shape = [seq_shape(T, H_A * DV_A), state_shapes[0], seq_shape(T, H_B * DV_B)] + state_shapes[1:]
    scratch = [pltpu.VMEM((G, H_A, DK_A, DV_A), F32), pltpu.VMEM((G, H_B, LANE, DV_B), F32),
               pltpu.VMEM((G, SUBLANE, LANE), F32), pltpu.VMEM((G, SUBLANE, LANE), F32)]
    if has_state:
        in_specs += [pl.BlockSpec((SUBLANE, CONV_CH), lambda b, c: (0, 0)), conv_spec] + state_specs
        args += [cw] + list(state)
        out_specs.append(conv_spec)
        out_shape.append(conv_shape)
        scratch.append(pltpu.VMEM((G, L + SUBLANE, CONV_CH), F32))
    return pl.pallas_call(
        functools.partial(_mixers_kernel, L=L, G=G, has_state=has_state),
        grid=(n_seq // G, n_c),
        in_specs=in_specs,
        out_specs=out_specs,
        out_shape=out_shape,
        scratch_shapes=scratch,
        compiler_params=pltpu.CompilerParams(dimension_semantics=("parallel", "arbitrary"),
                                             vmem_limit_bytes=VMEM_LIMIT),
        name=f"mixers_L{L}",
    )(*args)


def _outproj_kernel(xp_ref, xs_ref, map_ref, mas_ref, mbp_ref, mbs_ref, wo_ref, g_ref, rw_ref, rb_ref,
                    x1_ref, xsort_ref, info_ref, cpad_ref, *, n_p_tiles):
    half = H_A * DV_A
    tm = xp_ref.shape[0]

    def body(seg):
        x_ref, ma_ref, mb_ref = (xp_ref, xs_ref)[seg], (map_ref, mas_ref)[seg], (mbp_ref, mbs_ref)[seg]
        x1 = (x_ref[...] + jnp.dot(ma_ref[...].astype(BF16), wo_ref[:half, :], preferred_element_type=F32)
              + jnp.dot(mb_ref[...].astype(BF16), wo_ref[half:, :], preferred_element_type=F32))
        x1_ref[...] = x1
        hn = _rms(x1, g_ref[...])
        hn_hi = hn.astype(BF16)
        hn_lo = (hn - hn_hi.astype(F32)).astype(BF16)
        logits = (jnp.dot(hn_hi, rw_ref[0], preferred_element_type=F32)
                  + jnp.dot(hn_hi, rw_ref[1], preferred_element_type=F32)
                  + jnp.dot(hn_lo, rw_ref[0], preferred_element_type=F32)) + rb_ref[...]

        vals = logits.T[:N_EXPERTS, :]
        e_iota = lax.broadcasted_iota(jnp.int32, (N_EXPERTS, tm), 0)
        sels, tops = [], []
        for _ in range(TOP_K):
            m = jnp.max(vals, axis=0, keepdims=True)
            first = jnp.min(jnp.where(vals == m, e_iota, N_EXPERTS), axis=0, keepdims=True)
            sel = e_iota == first
            vals = jnp.where(sel, -jnp.inf, vals)
            sels.append(sel)
            tops.append(m)
        ex = [jnp.exp(t - tops[0]) for t in tops]
        den = ex[0] + ex[1] + ex[2] + ex[3]
        gates = [e / den for e in ex]
        mask = sels[0].astype(F32) + sels[1].astype(F32) + sels[2].astype(F32) + sels[3].astype(F32)
        ri = lax.broadcasted_iota(jnp.int32, (tm, tm), 0)
        ci = lax.broadcasted_iota(jnp.int32, (tm, tm), 1)
        rank = _dot(mask, (ri < ci).astype(F32))
        cnt = jnp.sum(mask, axis=1, keepdims=True)
        cpad = jnp.ceil(cnt * (1.0 / SEG_ALIGN)) * SEG_ALIGN
        cpad_b = jnp.broadcast_to(cpad, (N_EXPERTS, tm))
        er = lax.broadcasted_iota(jnp.int32, (N_EXPERTS, N_EXPERTS), 0)
        ec = lax.broadcasted_iota(jnp.int32, (N_EXPERTS, N_EXPERTS), 1)
        seg_off = _dot((er > ec).astype(F32), cpad_b)
        pos = seg_off + rank
        q = [jnp.sum(jnp.where(s, pos, 0.0), axis=0, keepdims=True) for s in sels]

        j_iota = lax.broadcasted_iota(jnp.int32, (MOE_CAP, tm), 0).astype(F32)
        perm = jnp.zeros((MOE_CAP, tm), F32)
        for kk in range(TOP_K):
            perm = jnp.where(j_iota == q[kk], 1.0, perm)
        xsorted = _dot(perm, hn)
        xsort_ref[...] = xsorted

        r_iota = lax.broadcasted_iota(jnp.int32, (LANE, tm), 0)
        info = jnp.zeros((LANE, tm), F32)
        for kk in range(TOP_K):
            info = jnp.where(r_iota == kk, q[kk], info)
            info = jnp.where(r_iota == TOP_K + kk, gates[kk], info)
        info_ref[...] = info.T
        cpad_ref[0] = cpad_b[:, :LANE]

    _for_segment(n_p_tiles, body)


def _outproj(xp, xs, ma_p, ma_s, mb_p, mb_s, w_out, g, rw, rb, tm):
    n_p, n_s = xp.shape[0], xs.shape[0]
    n = n_p + n_s
    nt = n // tm
    npt = n_p // tm
    half = H_A * DV_A
    return pl.pallas_call(
        functools.partial(_outproj_kernel, n_p_tiles=npt),
        grid=(nt,),
        in_specs=_two_segment_specs(tm, D_MODEL, npt) + _two_segment_specs(tm, half, npt)
        + _two_segment_specs(tm, half, npt) + [
            pl.BlockSpec((D_MODEL, D_MODEL), lambda i: (0, 0)),
            pl.BlockSpec((1, D_MODEL), lambda i: (0, 0)),
            pl.BlockSpec((2, D_MODEL, LANE), lambda i: (0, 0, 0)),
            pl.BlockSpec((1, LANE), lambda i: (0, 0)),
        ],
        out_specs=[
            pl.BlockSpec((tm, D_MODEL), lambda i: (i, 0)),
            pl.BlockSpec((MOE_CAP, D_MODEL), lambda i: (i, 0)),
            pl.BlockSpec((tm, LANE), lambda i: (i, 0)),
            pl.BlockSpec((1, N_EXPERTS, LANE), lambda i: (i, 0, 0)),
        ],
        out_shape=[
            jax.ShapeDtypeStruct((n, D_MODEL), F32),
            jax.ShapeDtypeStruct((nt * MOE_CAP, D_MODEL), F32),
            jax.ShapeDtypeStruct((n, LANE), F32),
            jax.ShapeDtypeStruct((nt, N_EXPERTS, LANE), F32),
        ],
        compiler_params=pltpu.CompilerParams(dimension_semantics=("arbitrary",), vmem_limit_bytes=VMEM_LIMIT),
        name="outproj",
    )(xp, xs, ma_p, ma_s, mb_p, mb_s, w_out, g, rw, rb)


def _expert_kernel(be_ref, bj_ref, tf_ref, tl_ref, cov_ref, nu_ref, vt_ref, ct_ref, lt_ref, nx_ref, ws_ref,
                   xs_hbm, wgu_hbm, bgu_ref, wd_hbm, bd_ref, ys_hbm,
                   xbuf, ybuf, gsem, ssem, wgu_st, wd_st, wsem, wgu_bf, wd_bf, *, nt):
    b = pl.program_id(0)
    n_used = nu_ref[0]
    slot = b % 2

    def start_pieces(bb, copy, s):
        e = be_ref[bb]
        base = bj_ref[bb] * MOE_BLK

        def body(t, carry):
            k = e * nt + t
            lo = jnp.maximum(vt_ref[k], base)
            ln = jnp.minimum(ct_ref[k], base + MOE_BLK) - lo

            @pl.when(ln > 0)
            def _():
                copy(s, pl.multiple_of(lt_ref[k] + lo, SEG_ALIGN), pl.multiple_of(lo - base, SEG_ALIGN),
                     pl.multiple_of(ln, SEG_ALIGN)).start()
            return carry

        lax.fori_loop(tf_ref[bb], tl_ref[bb] + 1, body, 0)

    def weight_copies(e):
        return (pltpu.make_async_copy(wgu_hbm.at[e], wgu_st, wsem.at[0]),
                pltpu.make_async_copy(wd_hbm.at[e], wd_st, wsem.at[1]))

    def cast_weights(p):
        wgu_bf[p] = wgu_st[...].astype(BF16)
        wd_bf[p] = wd_st[...].astype(BF16)

    def gather_copy(s, src, dst, size):
        return pltpu.make_async_copy(xs_hbm.at[pl.ds(src, size)], xbuf.at[s, pl.ds(dst, size)], gsem.at[s])

    def scatter_copy(s, src, dst, size):
        return pltpu.make_async_copy(ybuf.at[s, pl.ds(dst, size)], ys_hbm.at[pl.ds(src, size)], ssem.at[s])

    def wait_rows(count, copy, s):
        @pl.when(count > 0)
        def _():
            copy(s, 0, 0, pl.multiple_of(count, SEG_ALIGN)).wait()

    @pl.when(b == 0)
    def _():
        xbuf[...] = jnp.zeros_like(xbuf)
        start_pieces(0, gather_copy, 0)

    @pl.when(b + 1 < n_used)
    def _():
        start_pieces(b + 1, gather_copy, 1 - slot)

    @pl.when(b < n_used)
    def _():
        e = be_ref[b]
        first = jnp.logical_or(b == 0, be_ref[jnp.maximum(b - 1, 0)] != e)
        last = jnp.logical_or(b == n_used - 1, be_ref[jnp.minimum(b + 1, n_used - 1)] != e)
        has_next = nx_ref[b] < N_EXPERTS
        p = ws_ref[b]

        @pl.when(b == 0)
        def _():
            for cp in weight_copies(e):
                cp.start()
            for cp in weight_copies(e):
                cp.wait()
            cast_weights(p)

        @pl.when(jnp.logical_and(first, has_next))
        def _():
            for cp in weight_copies(nx_ref[b]):
                cp.start()

        wait_rows(cov_ref[b], gather_copy, slot)
        hgu = jnp.dot(xbuf[slot].astype(BF16), wgu_bf[p], preferred_element_type=F32) + bgu_ref[0]
        gate = jnp.minimum(hgu[:, :D_FF], SWIGLU_LIMIT)
        up = jnp.clip(hgu[:, D_FF:], -SWIGLU_LIMIT, SWIGLU_LIMIT)
        act = (up + 1.0) * gate * _sigmoid(SWIGLU_ALPHA * gate)
        y = jnp.dot(act.astype(BF16), wd_bf[p], preferred_element_type=F32) + bd_ref[0]

        @pl.when(jnp.logical_and(last, has_next))
        def _():
            for cp in weight_copies(nx_ref[b]):
                cp.wait()
            cast_weights(1 - p)

        @pl.when(b >= 2)
        def _():
            wait_rows(cov_ref[jnp.maximum(b - 2, 0)], scatter_copy, slot)

        ybuf[slot] = y
        start_pieces(b, scatter_copy, slot)

        @pl.when(b == n_used - 1)
        def _():
            wait_rows(cov_ref[b], scatter_copy, slot)
            wait_rows(jnp.where(b >= 1, cov_ref[jnp.maximum(b - 1, 0)], 0), scatter_copy, 1 - slot)


def _experts(tables, xs, w_gu, b_gu, w_down, b_down, nt):
    nb = tables[0].shape[0]

    def bias_blk(b, *t):
        return (t[0][jnp.minimum(b, t[5][0] - 1)], 0, 0)

    grid_spec = pltpu.PrefetchScalarGridSpec(
        num_scalar_prefetch=len(tables),
        grid=(nb,),
        in_specs=[
            pl.BlockSpec(memory_space=pl.ANY),
            pl.BlockSpec(memory_space=pl.ANY),
            pl.BlockSpec((1, 1, 2 * D_FF), bias_blk),
            pl.BlockSpec(memory_space=pl.ANY),
            pl.BlockSpec((1, 1, D_MODEL), bias_blk),
        ],
        out_specs=pl.BlockSpec(memory_space=pl.ANY),
        scratch_shapes=[
            pltpu.VMEM((2, MOE_BLK, D_MODEL), F32),
            pltpu.VMEM((2, MOE_BLK, D_MODEL), F32),
            pltpu.SemaphoreType.DMA((2,)),
            pltpu.SemaphoreType.DMA((2,)),
            pltpu.VMEM((D_MODEL, 2 * D_FF), F32),
            pltpu.VMEM((D_FF, D_MODEL), F32),
            pltpu.SemaphoreType.DMA((2,)),
            pltpu.VMEM((2, D_MODEL, 2 * D_FF), BF16),
            pltpu.VMEM((2, D_FF, D_MODEL), BF16),
        ],
    )
    return pl.pallas_call(
        functools.partial(_expert_kernel, nt=nt),
        grid_spec=grid_spec,
        out_shape=jax.ShapeDtypeStruct(xs.shape, xs.dtype),
        input_output_aliases={len(tables): 0},
        compiler_params=pltpu.CompilerParams(dimension_semantics=("arbitrary",), vmem_limit_bytes=VMEM_LIMIT),
        name="experts",
    )(*tables, xs, w_gu, b_gu, w_down, b_down)


def _combine_kernel(ys_ref, info_ref, x1_ref, pp_ref, ps_ref, gple_ref, wg_ref, wp_ref, gfin_ref,
                    outp_ref, outs_ref, *, n_p_tiles):
    tm = x1_ref.shape[0]

    def body(seg):
        p_ref, out_ref = (pp_ref, ps_ref)[seg], (outp_ref, outs_ref)[seg]
        info = info_ref[...]
        j_iota = lax.broadcasted_iota(jnp.int32, (tm, MOE_CAP), 1).astype(F32)
        gmat = jnp.zeros((tm, MOE_CAP), F32)
        for kk in range(TOP_K):
            gmat = jnp.where(j_iota == info[:, kk:kk + 1], info[:, TOP_K + kk:TOP_K + kk + 1], gmat)
        x2 = x1_ref[...] + jnp.dot(gmat.astype(BF16), ys_ref[...].astype(BF16), preferred_element_type=F32)
        hn = _rms(x2, gple_ref[...]).astype(BF16)
        gate = _sigmoid(jnp.dot(hn, wg_ref[...], preferred_element_type=F32))
        pe = jnp.dot(p_ref[...].astype(BF16), wp_ref[...], preferred_element_type=F32)
        x3 = x2 + gate * pe
        out_ref[...] = _rms(x3, gfin_ref[...])

    _for_segment(n_p_tiles, body)


def _combine(ys, info, x1, pp, ps, g_ple, w_gate, w_p, g_fin, tm):
    n_p, n_s = pp.shape[0], ps.shape[0]
    n = n_p + n_s
    nt = n // tm
    npt = n_p // tm
    return pl.pallas_call(
        functools.partial(_combine_kernel, n_p_tiles=npt),
        grid=(nt,),
        in_specs=[
            pl.BlockSpec((MOE_CAP, D_MODEL), lambda i: (i, 0)),
            pl.BlockSpec((tm, LANE), lambda i: (i, 0)),
            pl.BlockSpec((tm, D_MODEL), lambda i: (i, 0)),
        ] + _two_segment_specs(tm, PLE_DIM, npt) + [
            pl.BlockSpec((1, D_MODEL), lambda i: (0, 0)),
            pl.BlockSpec((D_MODEL, D_MODEL), lambda i: (0, 0)),
            pl.BlockSpec((PLE_DIM, D_MODEL), lambda i: (0, 0)),
            pl.BlockSpec((1, D_MODEL), lambda i: (0, 0)),
        ],
        out_specs=_two_segment_specs(tm, D_MODEL, npt),
        out_shape=[jax.ShapeDtypeStruct((n_p, D_MODEL), F32), jax.ShapeDtypeStruct((n_s, D_MODEL), F32)],
        compiler_params=pltpu.CompilerParams(dimension_semantics=("arbitrary",), vmem_limit_bytes=VMEM_LIMIT),
        name="combine",
    )(ys, info, x1, pp, ps, g_ple, w_gate, w_p, g_fin)


def _block_tables(seg_len, nb):
    nt = seg_len.shape[0]
    seg_off = jnp.cumsum(seg_len, axis=1) - seg_len
    seg_end = jnp.cumsum(seg_len, axis=0).T
    seg_start = seg_end - seg_len.T
    n_rows = seg_end[:, -1]
    n_blk = (n_rows + MOE_BLK - 1) // MOE_BLK
    blk_end = jnp.cumsum(n_blk)
    b = jnp.arange(nb, dtype=jnp.int32)
    block_e = jnp.minimum(jnp.sum((blk_end[None, :] <= b[:, None]).astype(jnp.int32), axis=1), N_EXPERTS - 1)
    idx = jnp.where(n_blk > 0, jnp.arange(N_EXPERTS, dtype=jnp.int32), N_EXPERTS)
    nxt = jnp.concatenate([lax.cummin(idx, axis=0, reverse=True)[1:], jnp.full((1,), N_EXPERTS, jnp.int32)])
    parity = (jnp.cumsum((n_blk > 0).astype(jnp.int32)) - 1) % 2
    per_e = jnp.concatenate([jnp.stack([blk_end - n_blk, n_rows, nxt, parity], axis=1), seg_start, seg_end],
                            axis=1).astype(F32)
    onehot = (block_e[:, None] == jnp.arange(N_EXPERTS, dtype=jnp.int32)[None, :]).astype(F32)
    per_b = jnp.dot(onehot, per_e, precision=HI).astype(jnp.int32)
    block_j = b - per_b[:, 0]
    base = block_j * MOE_BLK
    t_first = jnp.sum((per_b[:, 4 + nt:] <= base[:, None]).astype(jnp.int32), axis=1)
    t_last = jnp.sum((per_b[:, 4:4 + nt] < (base + MOE_BLK)[:, None]).astype(jnp.int32), axis=1) - 1
    cover = jnp.clip(per_b[:, 1] - base, 0, MOE_BLK)
    seg_shift = (jnp.arange(nt, dtype=jnp.int32)[:, None] * MOE_CAP + seg_off).T - seg_start
    tables = (block_e, block_j, t_first, t_last, cover, blk_end[-1:], seg_start.reshape(-1),
              seg_end.reshape(-1), seg_shift.reshape(-1), per_b[:, 2], per_b[:, 3])
    return tuple(t.astype(jnp.int32) for t in tables)


def _rearranged_in_weights(w_in):
    o = np.cumsum([0, CONV_CH, H_A * DV_A, H_A, H_A, H_B * DK_B, H_B * DK_B, H_B * DV_B, H_B * DV_B, H_B, H_B])
    conv_in, z_a, a_a, b_a, q_b, k_b, v_b, o_b, i_b, f_b = (w_in[:, int(o[j]):int(o[j + 1])] for j in range(10))
    zpad = jnp.zeros((D_MODEL, LANE - DK_B), w_in.dtype)

    def pad_heads(w):
        return jnp.concatenate([jnp.concatenate([w[:, h * DK_B:(h + 1) * DK_B], zpad], axis=1) for h in range(H_B)],
                               axis=1)

    small = jnp.concatenate([a_a, b_a, i_b, f_b], axis=1)
    w_gdn = w_in[:, :GDN_W]
    w_rest = jnp.concatenate([pad_heads(q_b), pad_heads(k_b), v_b, o_b,
                              small, jnp.zeros((D_MODEL, LANE - N_GATE), w_in.dtype)], axis=1)
    return w_gdn.astype(BF16), w_rest.astype(BF16), small.T.astype(BF16)


def _gate_params(a_log, dt_bias, i_bias, f_bias):
    z4 = jnp.zeros((4,), F32)
    alog = jnp.concatenate([a_log.astype(F32), z4, z4, z4])
    bias = jnp.concatenate([dt_bias.astype(F32), z4, i_bias.astype(F32), f_bias.astype(F32)])
    pad = jnp.zeros((LANE - N_GATE,), F32)
    pcol = jnp.zeros((SUBLANE, LANE), F32).at[0].set(jnp.concatenate([alog, pad])).at[1].set(
        jnp.concatenate([bias, pad]))
    prow = jnp.zeros((N_GATE, LANE), F32).at[:, 0].set(alog).at[:, 1].set(bias)
    return pcol, prow


def kernel(x_prompt, x_sample, p_prompt, p_sample, state_conv, state_gdn, state_mlstm_c, state_mlstm_n, state_mlstm_m, norm_attn_g, w_in, conv_w, gdn_a_log, gdn_dt_bias, gdn_norm_g, mlstm_i_bias, mlstm_f_bias, mlstm_norm_g, w_out, norm_moe_g, router_w, router_b, expert_w_gu, expert_b_gu, expert_w_down, expert_b_down, norm_ple_g, ple_gate_w, ple_w, final_norm_g):
    bp, tp, _ = x_prompt.shape
    bs, ts, _ = x_sample.shape
    n_p, n_s = bp * tp, bs * ts
    n = n_p + n_s
    lp, ls = min(tp, CHUNK), min(ts, CHUNK)
    tm = 256
    gp = 4 if bp % 4 == 0 else 1
    gs = 8 if bs % 8 == 0 else 1
    assert tp % lp == 0 and ts % ls == 0 and tp % tm == 0 and n_s % tm == 0 and ls % SUBLANE == 0

    xp = x_prompt.reshape(n_p, D_MODEL)
    xs = x_sample.reshape(n_s, D_MODEL)

    w_gdn, w_rest, ws_t = _rearranged_in_weights(w_in[0])
    pcol, prow = _gate_params(gdn_a_log[0], gdn_dt_bias[0], mlstm_i_bias[0], mlstm_f_bias[0])
    cw = jnp.zeros((SUBLANE, CONV_CH), F32).at[:CONV_W].set(conv_w[0].astype(F32))
    gdn_p, gdn_s, ml_p, ml_s, gate_p, gate_s, gatet_p, gatet_s, conv_p = _inproj(
        xp, xs, norm_attn_g[0].reshape(1, D_MODEL), w_gdn, w_rest, ws_t, pcol, prow, cw, tm, bp)
    gt_p = gatet_p.reshape(N_GATE, bp, tp // lp, lp).transpose(1, 2, 0, 3)
    gt_s = gatet_s.reshape(N_GATE, bs, ts // ls, ls).transpose(1, 2, 0, 3)

    ng_a = gdn_norm_g[0].reshape(1, DV_A).astype(F32)
    ng_b = mlstm_norm_g[0].reshape(H_B, DV_B).astype(F32)
    ma_p, gdn_st_p, mb_p, c_p, nn_p, m_p = _mixers(
        gdn_p.reshape(bp, tp, GDN_W), ml_p.reshape(bp, tp, MLP_W), gate_p.reshape(bp, tp, LANE), gt_p, ng_a, ng_b,
        L=lp, G=gp)
    ma_s, gdn_st_s, mb_s, c_s, nn_s, m_s, conv_s = _mixers(
        gdn_s.reshape(bs, ts, GDN_W), ml_s.reshape(bs, ts, MLP_W), gate_s.reshape(bs, ts, LANE), gt_s, ng_a, ng_b,
        L=ls, G=gs, cw=cw,
        state=(state_conv[0], state_gdn[0], state_mlstm_c[0], state_mlstm_n[0], state_mlstm_m[0].reshape(bs, 1, H_B)))
    half = H_A * DV_A

    rw = jnp.zeros((D_MODEL, LANE), F32).at[:, :N_EXPERTS].set(router_w[0])
    rw_hi = rw.astype(BF16)
    rw = jnp.stack([rw_hi, (rw - rw_hi.astype(F32)).astype(BF16)])
    rb = jnp.full((1, LANE), NEG, F32).at[0, :N_EXPERTS].set(router_b[0])
    x1, x_sorted, info, seg_len = _outproj(xp, xs, ma_p.reshape(n_p, half), ma_s.reshape(n_s, half),
                                           mb_p.reshape(n_p, half), mb_s.reshape(n_s, half),
                                           w_out[0].astype(BF16), norm_moe_g[0].reshape(1, D_MODEL), rw, rb, MOE_TM)

    nt = n // MOE_TM
    nb = -(-(n * TOP_K + nt * N_EXPERTS * (SEG_ALIGN - 1)) // MOE_BLK) + N_EXPERTS
    tables = _block_tables(seg_len[:, :, 0].astype(jnp.int32), nb)
    y_sorted = _experts(tables, x_sorted, expert_w_gu[0], expert_b_gu[0].reshape(N_EXPERTS, 1, 2 * D_FF),
                        expert_w_down[0], expert_b_down[0].reshape(N_EXPERTS, 1, D_MODEL), nt)
    y_p, y_s = _combine(y_sorted, info, x1, p_prompt[0].reshape(n_p, PLE_DIM),
                        p_sample[0].reshape(n_s, PLE_DIM), norm_ple_g[0].reshape(1, D_MODEL),
                        ple_gate_w[0].astype(BF16), ple_w[0].astype(BF16), final_norm_g.reshape(1, D_MODEL), MOE_TM)

    return (y_p.reshape(bp, tp, D_MODEL), y_s.reshape(bs, ts, D_MODEL),
            conv_p[None], gdn_st_p[None], c_p[None], nn_p[None], m_p.reshape(1, bp, H_B),
            conv_s[None], gdn_st_s[None], c_s[None], nn_s[None], m_s.reshape(1, bs, H_B))
```

```python
import functools

import numpy as np
import jax
import jax.numpy as jnp
from jax import lax
from jax.experimental import pallas as pl
from jax.experimental.pallas import tpu as pltpu

F32 = jnp.float32
BF16 = jnp.bfloat16

D_MODEL = 1024
H_A, DK_A, DV_A = 4, 128, 128
H_B, DK_B, DV_B = 4, 64, 128
CONV_W = 4
CONV_CH = H_A * (2 * DK_A + DV_A)
N_EXPERTS = 32
TOP_K = 4
D_FF = 1024
SWIGLU_LIMIT = 7.0
SWIGLU_ALPHA = 1.702
PLE_DIM = 256
EPS = 1e-6
NEG = -1e30
CHUNK = 64

LANE = 128
SUBLANE = 8
GDN_W = CONV_CH + H_A * DV_A
MLP_W = 2 * H_B * LANE + 2 * H_B * DV_B
N_GATE = 16

VMEM_LIMIT = 48 * 1024 * 1024

MOE_TM = 256
MOE_BLK = 512
SEG_ALIGN = SUBLANE
MOE_CAP = -(-(MOE_TM * TOP_K + N_EXPERTS * (SEG_ALIGN - 1)) // LANE) * LANE

HI = lax.Precision.HIGHEST

_NN = (((1,), (0,)), ((), ()))
_NT = (((1,), (1,)), ((), ()))
_TN = (((0,), (0,)), ((), ()))


def _dot(a, b, dims=_NN):
    return lax.dot_general(a.astype(BF16), b.astype(BF16), dims, preferred_element_type=F32)


def _dot_hi(a, b, dims=_NN):
    return lax.dot_general(a, b, dims, precision=HI, preferred_element_type=F32)


def _rms(x, g):
    return x * lax.rsqrt(jnp.mean(x * x, axis=-1, keepdims=True) + EPS) * g


def _softplus(t):
    return jnp.maximum(t, 0.0) + jnp.log1p(jnp.exp(-jnp.abs(t)))


def _sigmoid(t):
    return 1.0 / (1.0 + jnp.exp(-t))


def _silu(t):
    return t * _sigmoid(t)


def _activate_gates(raw, idx, alog, bias):
    t = raw + bias
    g = -jnp.exp(alog) * _softplus(t)
    beta = _sigmoid(t)
    lf = -_softplus(-t)
    return jnp.where(idx < 4, g, jnp.where(idx < 8, beta, jnp.where(idx < 12, t, lf)))


def _two_segment_specs(tm, width, n_p_tiles):
    return [pl.BlockSpec((tm, width), lambda i: (jnp.minimum(i, n_p_tiles - 1), 0)),
            pl.BlockSpec((tm, width), lambda i: (jnp.maximum(i - n_p_tiles, 0), 0))]


def _for_segment(n_p_tiles, body):
    i = pl.program_id(0)

    @pl.when(i < n_p_tiles)
    def _():
        body(0)

    @pl.when(i >= n_p_tiles)
    def _():
        body(1)


def _gdn_preactivate(raw, xc_ref, cw_ref, first_of_seq):
    tm = raw.shape[0]
    xc_ref[0:SUBLANE, :] = jnp.where(first_of_seq, 0.0, xc_ref[tm:tm + SUBLANE, :])
    xc_ref[SUBLANE:SUBLANE + tm, :] = raw[:, :CONV_CH]
    base = SUBLANE - (CONV_W - 1)
    conv = xc_ref[base:base + tm, :] * cw_ref[0:1, :]
    for j in range(1, CONV_W):
        conv = conv + xc_ref[base + j:base + j + tm, :] * cw_ref[j:j + 1, :]
    act = _silu(conv)
    parts = []
    for h in range(H_A):
        qh = act[:, h * DK_A:(h + 1) * DK_A]
        parts.append(qh * (lax.rsqrt(jnp.sum(qh * qh, axis=-1, keepdims=True) + EPS) * (DK_A ** -0.5)))
    for h in range(H_A):
        kh = act[:, H_A * DK_A + h * DK_A:H_A * DK_A + (h + 1) * DK_A]
        parts.append(kh * lax.rsqrt(jnp.sum(kh * kh, axis=-1, keepdims=True) + EPS))
    parts.append(act[:, 2 * H_A * DK_A:])
    parts.append(_silu(raw[:, CONV_CH:]))
    new_conv = xc_ref[SUBLANE + tm - (CONV_W - 1):SUBLANE + tm, :]
    return jnp.concatenate(parts, axis=-1), new_conv


def _inproj_kernel(xp_ref, xs_ref, g_ref, wa_ref, wb_ref, wst_ref, pc_ref, pr_ref, cw_ref,
                   gdnp_ref, gdns_ref, mlp_ref, mls_ref, gatep_ref, gates_ref, gatetp_ref, gatets_ref, cnew_ref,
                   xc_ref, raw_ref, *, n_p_tiles, tiles_per_seq):
    tm = xp_ref.shape[0]
    i = pl.program_id(0)

    @pl.when(i == 0)
    def _():
        xc_ref[...] = jnp.zeros_like(xc_ref)
        raw_ref[...] = jnp.zeros_like(raw_ref)

    def preactivate_previous_tile():
        pre, new_conv = _gdn_preactivate(raw_ref[...], xc_ref, cw_ref, (i - 1) % tiles_per_seq == 0)
        gdnp_ref[...] = pre
        cnew_ref[0] = new_conv

    def body(seg):
        x_ref = (xp_ref, xs_ref)[seg]
        ml_ref = (mlp_ref, mls_ref)[seg]
        gate_ref, gatet_ref = (gatep_ref, gates_ref)[seg], (gatetp_ref, gatets_ref)[seg]
        if seg == 0:
            preactivate_previous_tile()
        else:
            pl.when(i == n_p_tiles)(preactivate_previous_tile)
        hn = _rms(x_ref[...], g_ref[...]).astype(BF16)
        raw_gdn = jnp.dot(hn, wa_ref[...], preferred_element_type=F32)
        if seg == 0:
            raw_ref[...] = raw_gdn
        else:
            gdns_ref[...] = raw_gdn
        ml_ref[...] = jnp.dot(hn, wb_ref[:, :MLP_W], preferred_element_type=F32)
        raw = jnp.dot(hn, wb_ref[:, MLP_W:], preferred_element_type=F32)
        lane = lax.broadcasted_iota(jnp.int32, (tm, LANE), 1)
        gate_ref[...] = _activate_gates(raw, lane, pc_ref[0:1, :], pc_ref[1:2, :])
        raw_t = lax.dot_general(wst_ref[...], hn, _NT, preferred_element_type=F32)
        row = lax.broadcasted_iota(jnp.int32, (N_GATE, tm), 0)
        gatet_ref[...] = _activate_gates(raw_t, row, pr_ref[:, 0:1], pr_ref[:, 1:2])

    _for_segment(n_p_tiles, body)


def _inproj(xp, xs, g, w_gdn, w_rest, ws_t, pcol, prow, cw, tm, n_seq_p):
    n_p, n_s = xp.shape[0], xs.shape[0]
    npt = n_p // tm
    tiles_per_seq = npt // n_seq_p

    def out2(width):
        return _two_segment_specs(tm, width, npt)

    def shp2(width):
        return [jax.ShapeDtypeStruct((n_p, width), F32), jax.ShapeDtypeStruct((n_s, width), F32)]

    def prev_tile(i):
        return jnp.clip(i - 1, 0, npt - 1)

    return pl.pallas_call(
        functools.partial(_inproj_kernel, n_p_tiles=npt, tiles_per_seq=tiles_per_seq),
        grid=((n_p + n_s) // tm,),
        in_specs=_two_segment_specs(tm, D_MODEL, npt) + [
            pl.BlockSpec((1, D_MODEL), lambda i: (0, 0)),
            pl.BlockSpec((D_MODEL, GDN_W), lambda i: (0, 0)),
            pl.BlockSpec((D_MODEL, MLP_W + LANE), lambda i: (0, 0)),
            pl.BlockSpec((N_GATE, D_MODEL), lambda i: (0, 0)),
            pl.BlockSpec((SUBLANE, LANE), lambda i: (0, 0)),
            pl.BlockSpec((N_GATE, LANE), lambda i: (0, 0)),
            pl.BlockSpec((SUBLANE, CONV_CH), lambda i: (0, 0)),
        ],
        out_specs=[
            pl.BlockSpec((tm, GDN_W), lambda i: (prev_tile(i), 0)),
            pl.BlockSpec((tm, GDN_W), lambda i: (jnp.maximum(i - npt, 0), 0)),
        ] + out2(MLP_W) + out2(LANE) + [
            pl.BlockSpec((N_GATE, tm), lambda i: (0, jnp.minimum(i, npt - 1))),
            pl.BlockSpec((N_GATE, tm), lambda i: (0, jnp.maximum(i - npt, 0))),
            pl.BlockSpec((1, CONV_W - 1, CONV_CH), lambda i: (prev_tile(i) // tiles_per_seq, 0, 0)),
        ],
        out_shape=shp2(GDN_W) + shp2(MLP_W) + shp2(LANE) + [
            jax.ShapeDtypeStruct((N_GATE, n_p), F32), jax.ShapeDtypeStruct((N_GATE, n_s), F32),
            jax.ShapeDtypeStruct((n_seq_p, CONV_W - 1, CONV_CH), F32)],
        scratch_shapes=[pltpu.VMEM((tm + SUBLANE, CONV_CH), F32), pltpu.VMEM((tm, GDN_W), F32)],
        compiler_params=pltpu.CompilerParams(dimension_semantics=("arbitrary",), vmem_limit_bytes=VMEM_LIMIT),
        name="inproj",
    )(xp, xs, g, w_gdn, w_rest, ws_t, pcol, prow, cw)


def _chunk_masks(L):
    ri = lax.broadcasted_iota(jnp.int32, (L, L), 0)
    ci = lax.broadcasted_iota(jnp.int32, (L, L), 1)
    return ri >= ci, ri > ci, ri <= ci


def _run_interleaved(*stage_generators):
    live = list(stage_generators)
    while live:
        for gen in list(live):
            if next(gen, StopIteration) is StopIteration:
                live.remove(gen)


def _gdn_stages(*refs, L, G, has_state):
    if has_state:
        (xin_ref, gate_ref, gatet_ref, cw_ref, ng_ref, cst_ref, s0_ref,
         mix_ref, cnew_ref, snew_ref, xc_ref, s_ref) = refs
    else:
        xin_ref, gate_ref, gatet_ref, ng_ref, mix_ref, snew_ref, s_ref = refs
    c = pl.program_id(1)

    @pl.when(c == 0)
    def _():
        if has_state:
            xc_ref[:, 0:SUBLANE, :] = jnp.zeros((G, SUBLANE, CONV_CH), F32)
            xc_ref[:, SUBLANE - (CONV_W - 1):SUBLANE, :] = cst_ref[...]
            s_ref[...] = s0_ref[...]
        else:
            s_ref[...] = jnp.zeros_like(s_ref)

    if has_state:
        @pl.when(c > 0)
        def _():
            xc_ref[:, 0:SUBLANE, :] = xc_ref[:, L:L + SUBLANE, :]

    yield
    tril, strict, triu = _chunk_masks(L)
    tril_f, triu_f = tril.astype(F32), triu.astype(F32)
    base = SUBLANE - (CONV_W - 1)

    chains = [(g, h) for g in range(G) for h in range(H_A)]
    s_old = [s_ref[g, h] for g, h in chains]
    if has_state:
        for g in range(G):
            xc_ref[g, SUBLANE:SUBLANE + L, :] = xin_ref[g, :, :CONV_CH]

    q, k, v, beta, gc, gl, decay = [], [], [], [], [], [], []
    for g in range(G):
        if has_state:
            conv = xc_ref[g, base:base + L, :] * cw_ref[0:1, :]
            for j in range(1, CONV_W):
                conv = conv + xc_ref[g, base + j:base + j + L, :] * cw_ref[j:j + 1, :]
            cnew_ref[g] = xc_ref[g, SUBLANE + L - (CONV_W - 1):SUBLANE + L, :]
            act = _silu(conv)
        else:
            act = xin_ref[g, :, :CONV_CH]
        gact = gate_ref[g]
        cum_c = _dot_hi(tril_f, gact)
        cum_r = _dot_hi(gatet_ref[g, 0], triu_f)
        for h in range(H_A):
            q.append(act[:, h * DK_A:(h + 1) * DK_A])
            k.append(act[:, H_A * DK_A + h * DK_A:H_A * DK_A + (h + 1) * DK_A])
            v.append(act[:, 2 * H_A * DK_A + h * DV_A:2 * H_A * DK_A + (h + 1) * DV_A])
            beta.append(gact[:, 4 + h:5 + h])
            gc.append(cum_c[:, h:h + 1])
            gl.append(cum_c[L - 1:L, h:h + 1])
            gr = cum_r[h:h + 1, :]
            decay.append(jnp.where(tril, jnp.exp(jnp.where(tril, cum_c[:, h:h + 1] - gr, 0.0)), 0.0))
        yield

    nc = range(len(chains))
    if has_state:
        qss = [jnp.sum(q[i] * q[i], axis=-1, keepdims=True) for i in nc]
        kss = [jnp.sum(k[i] * k[i], axis=-1, keepdims=True) for i in nc]
        q = [q[i] * (lax.rsqrt(qss[i] + EPS) * (DK_A ** -0.5)) for i in nc]
        k = [k[i] * lax.rsqrt(kss[i] + EPS) for i in nc]
    kb = [k[i] * beta[i] for i in nc]
    egc = [jnp.exp(gc[i]) for i in nc]
    yield
    kk = [_dot(kb[i], k[i], _NT) for i in nc]
    yield
    qk = [_dot(q[i], k[i], _NT) for i in nc]
    yield
    eye = (lax.broadcasted_iota(jnp.int32, (L, L), 0) == lax.broadcasted_iota(jnp.int32, (L, L), 1)).astype(F32)
    pw = [-jnp.where(strict, kk[i] * decay[i], 0.0) for i in nc]
    t_inv = [eye + pw[i] for i in nc]
    span = 2
    while span < L:
        yield
        pw = [_dot(pw[i], pw[i]) for i in nc]
        yield
        t_inv = [t_inv[i] + _dot(t_inv[i], pw[i]) for i in nc]
        span *= 2
    yield
    sol = [_dot(t_inv[i], jnp.concatenate([v[i] * beta[i], kb[i] * egc[i]], axis=-1)) for i in nc]
    yield
    qs = [_dot(q[i] * egc[i], s_old[i]) for i in nc]
    yield
    ws = [_dot(sol[i][:, DV_A:], s_old[i]) for i in nc]
    v_new = [sol[i][:, :DV_A] - ws[i] for i in nc]
    yield
    o = [qs[i] + _dot(jnp.where(tril, qk[i] * decay[i], 0.0), v_new[i]) for i in nc]
    yield
    s_new = [s_old[i] * jnp.exp(gl[i]) + _dot(k[i] * jnp.exp(gl[i] - gc[i]), v_new[i], _TN) for i in nc]
    yield
    ms = [jnp.mean(o[i] * o[i], axis=-1, keepdims=True) for i in nc]
    on = [o[i] * lax.rsqrt(ms[i] + EPS) for i in nc]
    yield
    for i, (g, h) in enumerate(chains):
        z = xin_ref[g, :, CONV_CH + h * DV_A:CONV_CH + (h + 1) * DV_A]
        out = on[i] * ng_ref[...] * (_silu(z) if has_state else z)
        mix_ref[g, :, h * DV_A:(h + 1) * DV_A] = out.astype(mix_ref.dtype)
    yield
    for i, (g, h) in enumerate(chains):
        s_ref[g, h] = s_new[i]
        snew_ref[g, h] = s_new[i]


def _mlstm_stages(*refs, L, G, has_state):
    if has_state:
        (xin_ref, gate_ref, gatet_ref, ng_ref, c0_ref, n0_ref, m0_ref,
         mix_ref, cnew_ref, nnew_ref, mnew_ref, c_ref, n_ref, m_ref) = refs
    else:
        (xin_ref, gate_ref, gatet_ref, ng_ref,
         mix_ref, cnew_ref, nnew_ref, mnew_ref, c_ref, n_ref, m_ref) = refs
    c = pl.program_id(1)

    @pl.when(c == 0)
    def _():
        c_ref[...] = jnp.zeros_like(c_ref)
        n_ref[...] = jnp.zeros_like(n_ref)
        m_ref[...] = jnp.zeros_like(m_ref)
        if has_state:
            c_ref[:, :, 0:DK_B, :] = c0_ref[...]
            n_ref[:, 0:H_B, 0:DK_B] = n0_ref[...]
            m_ref[:, 0:1, 0:H_B] = m0_ref[...]

    yield
    tril, _, triu = _chunk_masks(L)
    tril_f, triu_f = tril.astype(F32), triu.astype(F32)

    chains = [(g, h) for g in range(G) for h in range(H_B)]
    nc = range(len(chains))
    c_old = [c_ref[g, h] for g, h in chains]
    n_old = [n_ref[g, h:h + 1, :] for g, h in chains]
    m_old = [m_ref[g, 0:1, h:h + 1] for g, h in chains]

    v0 = 2 * H_B * LANE
    q = [xin_ref[g, :, h * LANE:(h + 1) * LANE] * (DK_B ** -0.5) for g, h in chains]
    k = [xin_ref[g, :, (H_B + h) * LANE:(H_B + h + 1) * LANE] for g, h in chains]
    v = [xin_ref[g, :, v0 + h * DV_B:v0 + (h + 1) * DV_B] for g, h in chains]
    ig_c, b_c, b_last, d_log = [], [], [], []
    for g in range(G):
        gact = gate_ref[g]
        gact_t = gatet_ref[g, 0]
        cum_c = _dot_hi(tril_f, gact)
        cum_r = _dot_hi(gact_t, triu_f)
        for h in range(H_B):
            ig_c.append(gact[:, 8 + h:9 + h])
            b_c.append(cum_c[:, 12 + h:13 + h])
            b_last.append(cum_c[L - 1:L, 12 + h:13 + h])
            d_log.append(jnp.where(tril, cum_c[:, 12 + h:13 + h] - cum_r[12 + h:13 + h, :]
                                   + gact_t[8 + h:9 + h, :], NEG))
        yield
    qk = [_dot(q[i], k[i], _NT) for i in nc]
    yield
    qc = [_dot(q[i], c_old[i]) for i in nc]
    yield
    inter = [b_c[i] + m_old[i] for i in nc]
    m_t = [jnp.maximum(inter[i], jnp.max(d_log[i], axis=-1, keepdims=True)) for i in nc]
    yield
    s = [qk[i] * jnp.exp(d_log[i] - m_t[i]) for i in nc]
    e_inter = [jnp.exp(inter[i] - m_t[i]) for i in nc]
    yield
    sv = [_dot(s[i], v[i]) for i in nc]
    yield
    m_new = [m_t[i][L - 1:L, :] for i in nc]
    kw = [k[i] * jnp.exp(b_last[i] - b_c[i] + ig_c[i] - m_new[i]) for i in nc]
    f_tot = [jnp.exp(b_last[i] + m_old[i] - m_new[i]) for i in nc]
    yield
    c_new = [f_tot[i] * c_old[i] + _dot(kw[i], v[i], _TN) for i in nc]
    yield
    n_new = [f_tot[i] * n_old[i] + jnp.sum(kw[i], axis=0, keepdims=True) for i in nc]
    qn = [jnp.sum(q[i] * n_old[i], axis=-1, keepdims=True) for i in nc]
    yield
    ssum = [jnp.sum(s[i], axis=-1, keepdims=True) for i in nc]
    yield
    den = [jnp.maximum(jnp.abs(e_inter[i] * qn[i] + ssum[i]), jnp.exp(-m_t[i])) for i in nc]
    hh = [(e_inter[i] * qc[i] + sv[i]) / den[i] for i in nc]
    yield
    ms = [jnp.mean(hh[i] * hh[i], axis=-1, keepdims=True) for i in nc]
    hn = [hh[i] * lax.rsqrt(ms[i] + EPS) for i in nc]
    yield
    for i, (g, h) in enumerate(chains):
        og = xin_ref[g, :, v0 + H_B * DV_B + h * DV_B:v0 + H_B * DV_B + (h + 1) * DV_B]
        mix_ref[g, :, h * DV_B:(h + 1) * DV_B] = (hn[i] * ng_ref[h:h + 1, :] * _sigmoid(og)).astype(mix_ref.dtype)
    yield
    for i, (g, h) in enumerate(chains):
        c_ref[g, h] = c_new[i]
        n_ref[g, h:h + 1, :] = n_new[i]
        m_ref[g, 0:1, h:h + 1] = m_new[i]
        cnew_ref[g, h] = c_new[i][0:DK_B, :]
        nnew_ref[g, h:h + 1, :] = n_new[i][:, 0:DK_B]
        mnew_ref[g, 0:1, h:h + 1] = m_new[i]


def _mixers_kernel(*refs, L, G, has_state):
    if has_state:
        (gdn_ref, gate_ref, gatet_ref, nga_ref, ml_ref, ngb_ref, cw_ref, cst_ref, s0_ref, c0_ref, n0_ref, m0_ref,
         mixa_ref, snew_ref, mixb_ref, cnew_ref, nnew_ref, mnew_ref, convnew_ref,
         s_ref, c_ref, n_ref, m_ref, xc_ref) = refs
        gdn_refs = (gdn_ref, gate_ref, gatet_ref, cw_ref, nga_ref, cst_ref, s0_ref,
                    mixa_ref, convnew_ref, snew_ref, xc_ref, s_ref)
        ml_refs = (ml_ref, gate_ref, gatet_ref, ngb_ref, c0_ref, n0_ref, m0_ref,
                   mixb_ref, cnew_ref, nnew_ref, mnew_ref, c_ref, n_ref, m_ref)
    else:
        (gdn_ref, gate_ref, gatet_ref, nga_ref, ml_ref, ngb_ref,
         mixa_ref, snew_ref, mixb_ref, cnew_ref, nnew_ref, mnew_ref, s_ref, c_ref, n_ref, m_ref) = refs
        gdn_refs = (gdn_ref, gate_ref, gatet_ref, nga_ref, mixa_ref, snew_ref, s_ref)
        ml_refs = (ml_ref, gate_ref, gatet_ref, ngb_ref, mixb_ref, cnew_ref, nnew_ref, mnew_ref, c_ref, n_ref, m_ref)
    _run_interleaved(_gdn_stages(*gdn_refs, L=L, G=G, has_state=has_state),
                     _mlstm_stages(*ml_refs, L=L, G=G, has_state=has_state))


def _mixers(gdn_in, ml_in, gates, gates_t, ng_a, ng_b, *, L, G, cw=None, state=None):
    n_seq, T, _ = gdn_in.shape
    n_c = T // L
    has_state = state is not None

    def seq_blk(*tail):
        return pl.BlockSpec((G,) + tail, lambda b, c: (b,) + (0,) * len(tail))

    def tok_blk(width):
        return pl.BlockSpec((G, L, width), lambda b, c: (b, c, 0))

    def seq_shape(*tail):
        return jax.ShapeDtypeStruct((n_seq,) + tail, F32)

    state_specs = [seq_blk(H_A, DK_A, DV_A), seq_blk(H_B, DK_B, DV_B), seq_blk(H_B, DK_B), seq_blk(1, H_B)]
    state_shapes = [seq_shape(H_A, DK_A, DV_A), seq_shape(H_B, DK_B, DV_B), seq_shape(H_B, DK_B), seq_shape(1, H_B)]
    conv_spec, conv_shape = seq_blk(CONV_W - 1, CONV_CH), seq_shape(CONV_W - 1, CONV_CH)
    in_specs = [
        tok_blk(GDN_W), tok_blk(LANE),
        pl.BlockSpec((G, 1, N_GATE, L), lambda b, c: (b, c, 0, 0)),
        pl.BlockSpec((1, DV_A), lambda b, c: (0, 0)),
        tok_blk(MLP_W),
        pl.BlockSpec((H_B, DV_B), lambda b, c: (0, 0)),
    ]
    args = [gdn_in, gates, gates_t, ng_a, ml_in, ng_b]
    out_specs = [tok_blk(H_A * DV_A), state_specs[0], tok_blk(H_B * DV_B)] + state_specs[1:]
    mix_dtype = BF16 if L % (2 * SUBLANE) == 0 else F32
    mix_a = jax.ShapeDtypeStruct((n_seq, T, H_A * DV_A), mix_dtype)
    mix_b = jax.ShapeDtypeStruct((n_seq, T, H_B * DV_B), mix_dtype)
    out_shape = [mix_a, state_shapes[0], mix_b] + state_shapes[1:]
    scratch = [pltpu.VMEM((G, H_A, DK_A, DV_A), F32), pltpu.VMEM((G, H_B, LANE, DV_B), F32),
               pltpu.VMEM((G, SUBLANE, LANE), F32), pltpu.VMEM((G, SUBLANE, LANE), F32)]
    if has_state:
        in_specs += [pl.BlockSpec((SUBLANE, CONV_CH), lambda b, c: (0, 0)), conv_spec] + state_specs
        args += [cw] + list(state)
        out_specs.append(conv_spec)
        out_shape.append(conv_shape)
        scratch.append(pltpu.VMEM((G, L + SUBLANE, CONV_CH), F32))
    return pl.pallas_call(
        functools.partial(_mixers_kernel, L=L, G=G, has_state=has_state),
        grid=(n_seq // G, n_c),
        in_specs=in_specs,
        out_specs=out_specs,
        out_shape=out_shape,
        scratch_shapes=scratch,
        compiler_params=pltpu.CompilerParams(dimension_semantics=("parallel", "arbitrary"),
                                             vmem_limit_bytes=VMEM_LIMIT),
        name=f"mixers_L{L}",
    )(*args)


def _outproj_kernel(xp_ref, xs_ref, map_ref, mas_ref, mbp_ref, mbs_ref, wo_ref, g_ref, rw_ref, rb_ref,
                    x1_ref, xsort_ref, info_ref, cpad_ref, *, n_p_tiles):
    half = H_A * DV_A
    tm = xp_ref.shape[0]

    def body(seg):
        x_ref, ma_ref, mb_ref = (xp_ref, xs_ref)[seg], (map_ref, mas_ref)[seg], (mbp_ref, mbs_ref)[seg]
        x1 = (x_ref[...] + jnp.dot(ma_ref[...].astype(BF16), wo_ref[:half, :], preferred_element_type=F32)
              + jnp.dot(mb_ref[...].astype(BF16), wo_ref[half:, :], preferred_element_type=F32))
        x1_ref[...] = x1
        hn = _rms(x1, g_ref[...])
        hn_hi = hn.astype(BF16)
        hn_lo = (hn - hn_hi.astype(F32)).astype(BF16)
        logits = (jnp.dot(hn_hi, rw_ref[0], preferred_element_type=F32)
                  + jnp.dot(hn_hi, rw_ref[1], preferred_element_type=F32)
                  + jnp.dot(hn_lo, rw_ref[0], preferred_element_type=F32)) + rb_ref[...]

        vals = logits.T[:N_EXPERTS, :]
        e_iota = lax.broadcasted_iota(jnp.int32, (N_EXPERTS, tm), 0)
        sels, tops = [], []
        for _ in range(TOP_K):
            m = jnp.max(vals, axis=0, keepdims=True)
            first = jnp.min(jnp.where(vals == m, e_iota, N_EXPERTS), axis=0, keepdims=True)
            sel = e_iota == first
            vals = jnp.where(sel, -jnp.inf, vals)
            sels.append(sel)
            tops.append(m)
        ex = [jnp.exp(t - tops[0]) for t in tops]
        den = ex[0] + ex[1] + ex[2] + ex[3]
        gates = [e / den for e in ex]
        mask = sels[0].astype(F32) + sels[1].astype(F32) + sels[2].astype(F32) + sels[3].astype(F32)
        ri = lax.broadcasted_iota(jnp.int32, (tm, tm), 0)
        ci = lax.broadcasted_iota(jnp.int32, (tm, tm), 1)
        rank = _dot(mask, (ri < ci).astype(F32))
        cnt = jnp.sum(mask, axis=1, keepdims=True)
        cpad = jnp.ceil(cnt * (1.0 / SEG_ALIGN)) * SEG_ALIGN
        cpad_b = jnp.broadcast_to(cpad, (N_EXPERTS, tm))
        er = lax.broadcasted_iota(jnp.int32, (N_EXPERTS, N_EXPERTS), 0)
        ec = lax.broadcasted_iota(jnp.int32, (N_EXPERTS, N_EXPERTS), 1)
        seg_off = _dot((er > ec).astype(F32), cpad_b)
        pos = seg_off + rank
        q = [jnp.sum(jnp.where(s, pos, 0.0), axis=0, keepdims=True) for s in sels]

        j_iota = lax.broadcasted_iota(jnp.int32, (MOE_CAP, tm), 0).astype(F32)
        perm = jnp.zeros((MOE_CAP, tm), F32)
        for kk in range(TOP_K):
            perm = jnp.where(j_iota == q[kk], 1.0, perm)
        xsorted = _dot(perm, hn)
        xsort_ref[...] = xsorted

        r_iota = lax.broadcasted_iota(jnp.int32, (LANE, tm), 0)
        info = jnp.zeros((LANE, tm), F32)
        for kk in range(TOP_K):
            info = jnp.where(r_iota == kk, q[kk], info)
            info = jnp.where(r_iota == TOP_K + kk, gates[kk], info)
        info_ref[...] = info.T
        cpad_ref[0] = cpad_b[:, :LANE]

    _for_segment(n_p_tiles, body)


def _outproj(xp, xs, ma_p, ma_s, mb_p, mb_s, w_out, g, rw, rb, tm):
    n_p, n_s = xp.shape[0], xs.shape[0]
    n = n_p + n_s
    nt = n // tm
    npt = n_p // tm
    half = H_A * DV_A
    return pl.pallas_call(
        functools.partial(_outproj_kernel, n_p_tiles=npt),
        grid=(nt,),
        in_specs=_two_segment_specs(tm, D_MODEL, npt) + _two_segment_specs(tm, half, npt)
        + _two_segment_specs(tm, half, npt) + [
            pl.BlockSpec((D_MODEL, D_MODEL), lambda i: (0, 0)),
            pl.BlockSpec((1, D_MODEL), lambda i: (0, 0)),
            pl.BlockSpec((2, D_MODEL, LANE), lambda i: (0, 0, 0)),
            pl.BlockSpec((1, LANE), lambda i: (0, 0)),
        ],
        out_specs=[
            pl.BlockSpec((tm, D_MODEL), lambda i: (i, 0)),
            pl.BlockSpec((MOE_CAP, D_MODEL), lambda i: (i, 0)),
            pl.BlockSpec((tm, LANE), lambda i: (i, 0)),
            pl.BlockSpec((1, N_EXPERTS, LANE), lambda i: (i, 0, 0)),
        ],
        out_shape=[
            jax.ShapeDtypeStruct((n, D_MODEL), F32),
            jax.ShapeDtypeStruct((nt * MOE_CAP, D_MODEL), F32),
            jax.ShapeDtypeStruct((n, LANE), F32),
            jax.ShapeDtypeStruct((nt, N_EXPERTS, LANE), F32),
        ],
        compiler_params=pltpu.CompilerParams(dimension_semantics=("arbitrary",), vmem_limit_bytes=VMEM_LIMIT),
        name="outproj",
    )(xp, xs, ma_p, ma_s, mb_p, mb_s, w_out, g, rw, rb)


def _expert_kernel(be_ref, bj_ref, tf_ref, tl_ref, cov_ref, nu_ref, vt_ref, ct_ref, lt_ref, nx_ref, ws_ref,
                   xs_hbm, wgu_hbm, bgu_ref, wd_hbm, bd_ref, ys_hbm,
                   xbuf, ybuf, gsem, ssem, wgu_st, wd_st, wsem, wgu_bf, wd_bf, *, nt):
    b = pl.program_id(0)
    n_used = nu_ref[0]
    slot = b % 2

    def start_pieces(bb, copy, s):
        e = be_ref[bb]
        base = bj_ref[bb] * MOE_BLK

        def body(t, carry):
            k = e * nt + t
            lo = jnp.maximum(vt_ref[k], base)
            ln = jnp.minimum(ct_ref[k], base + MOE_BLK) - lo

            @pl.when(ln > 0)
            def _():
                copy(s, pl.multiple_of(lt_ref[k] + lo, SEG_ALIGN), pl.multiple_of(lo - base, SEG_ALIGN),
                     pl.multiple_of(ln, SEG_ALIGN)).start()
            return carry

        lax.fori_loop(tf_ref[bb], tl_ref[bb] + 1, body, 0)

    def weight_copies(e):
        return (pltpu.make_async_copy(wgu_hbm.at[e], wgu_st, wsem.at[0]),
                pltpu.make_async_copy(wd_hbm.at[e], wd_st, wsem.at[1]))

    def cast_weights(p):
        wgu_bf[p] = wgu_st[...].astype(BF16)
        wd_bf[p] = wd_st[...].astype(BF16)

    def gather_copy(s, src, dst, size):
        return pltpu.make_async_copy(xs_hbm.at[pl.ds(src, size)], xbuf.at[s, pl.ds(dst, size)], gsem.at[s])

    def scatter_copy(s, src, dst, size):
        return pltpu.make_async_copy(ybuf.at[s, pl.ds(dst, size)], ys_hbm.at[pl.ds(src, size)], ssem.at[s])

    def wait_rows(count, copy, s):
        @pl.when(count > 0)
        def _():
            copy(s, 0, 0, pl.multiple_of(count, SEG_ALIGN)).wait()

    @pl.when(b == 0)
    def _():
        xbuf[...] = jnp.zeros_like(xbuf)
        start_pieces(0, gather_copy, 0)

    @pl.when(b + 1 < n_used)
    def _():
        start_pieces(b + 1, gather_copy, 1 - slot)

    @pl.when(b < n_used)
    def _():
        e = be_ref[b]
        first = jnp.logical_or(b == 0, be_ref[jnp.maximum(b - 1, 0)] != e)
        last = jnp.logical_or(b == n_used - 1, be_ref[jnp.minimum(b + 1, n_used - 1)] != e)
        has_next = nx_ref[b] < N_EXPERTS
        p = ws_ref[b]

        @pl.when(b == 0)
        def _():
            for cp in weight_copies(e):
                cp.start()
            for cp in weight_copies(e):
                cp.wait()
            cast_weights(p)

        @pl.when(jnp.logical_and(first, has_next))
        def _():
            for cp in weight_copies(nx_ref[b]):
                cp.start()

        wait_rows(cov_ref[b], gather_copy, slot)

        @pl.when(b >= 2)
        def _():
            wait_rows(cov_ref[jnp.maximum(b - 2, 0)], scatter_copy, slot)

        def expert_mlp(rows):
            hgu = jnp.dot(xbuf[slot, :rows].astype(BF16), wgu_bf[p], preferred_element_type=F32) + bgu_ref[0]
            gate = jnp.minimum(hgu[:, :D_FF], SWIGLU_LIMIT)
            up = jnp.clip(hgu[:, D_FF:], -SWIGLU_LIMIT, SWIGLU_LIMIT)
            act = (up + 1.0) * gate * _sigmoid(SWIGLU_ALPHA * gate)
            ybuf[slot, :rows] = jnp.dot(act.astype(BF16), wd_bf[p], preferred_element_type=F32) + bd_ref[0]

        pl.when(cov_ref[b] > MOE_BLK // 2)(lambda: expert_mlp(MOE_BLK))
        pl.when(cov_ref[b] <= MOE_BLK // 2)(lambda: expert_mlp(MOE_BLK // 2))

        @pl.when(jnp.logical_and(last, has_next))
        def _():
            for cp in weight_copies(nx_ref[b]):
                cp.wait()
            cast_weights(1 - p)

        start_pieces(b, scatter_copy, slot)

        @pl.when(b == n_used - 1)
        def _():
            wait_rows(cov_ref[b], scatter_copy, slot)
            wait_rows(jnp.where(b >= 1, cov_ref[jnp.maximum(b - 1, 0)], 0), scatter_copy, 1 - slot)


def _experts(tables, xs, w_gu, b_gu, w_down, b_down, nt):
    nb = tables[0].shape[0]

    def bias_blk(b, *t):
        return (t[0][jnp.minimum(b, t[5][0] - 1)], 0, 0)

    grid_spec = pltpu.PrefetchScalarGridSpec(
        num_scalar_prefetch=len(tables),
        grid=(nb,),
        in_specs=[
            pl.BlockSpec(memory_space=pl.ANY),
            pl.BlockSpec(memory_space=pl.ANY),
            pl.BlockSpec((1, 1, 2 * D_FF), bias_blk),
            pl.BlockSpec(memory_space=pl.ANY),
            pl.BlockSpec((1, 1, D_MODEL), bias_blk),
        ],
        out_specs=pl.BlockSpec(memory_space=pl.ANY),
        scratch_shapes=[
            pltpu.VMEM((2, MOE_BLK, D_MODEL), F32),
            pltpu.VMEM((2, MOE_BLK, D_MODEL), F32),
            pltpu.SemaphoreType.DMA((2,)),
            pltpu.SemaphoreType.DMA((2,)),
            pltpu.VMEM((D_MODEL, 2 * D_FF), F32),
            pltpu.VMEM((D_FF, D_MODEL), F32),
            pltpu.SemaphoreType.DMA((2,)),
            pltpu.VMEM((2, D_MODEL, 2 * D_FF), BF16),
            pltpu.VMEM((2, D_FF, D_MODEL), BF16),
        ],
    )
    return pl.pallas_call(
        functools.partial(_expert_kernel, nt=nt),
        grid_spec=grid_spec,
        out_shape=jax.ShapeDtypeStruct(xs.shape, xs.dtype),
        input_output_aliases={len(tables): 0},
        compiler_params=pltpu.CompilerParams(dimension_semantics=("arbitrary",), vmem_limit_bytes=VMEM_LIMIT),
        name="experts",
    )(*tables, xs, w_gu, b_gu, w_down, b_down)


def _combine_kernel(ys_ref, info_ref, x1_ref, pp_ref, ps_ref, gple_ref, wg_ref, wp_ref, gfin_ref,
                    outp_ref, outs_ref, *, n_p_tiles):
    tm = x1_ref.shape[0]

    def body(seg):
        p_ref, out_ref = (pp_ref, ps_ref)[seg], (outp_ref, outs_ref)[seg]
        info = info_ref[...]
        j_iota = lax.broadcasted_iota(jnp.int32, (tm, MOE_CAP), 1).astype(F32)
        gmat = jnp.zeros((tm, MOE_CAP), F32)
        for kk in range(TOP_K):
            gmat = jnp.where(j_iota == info[:, kk:kk + 1], info[:, TOP_K + kk:TOP_K + kk + 1], gmat)
        x2 = x1_ref[...] + jnp.dot(gmat.astype(BF16), ys_ref[...].astype(BF16), preferred_element_type=F32)
        hn = _rms(x2, gple_ref[...]).astype(BF16)
        gate = _sigmoid(jnp.dot(hn, wg_ref[...], preferred_element_type=F32))
        pe = jnp.dot(p_ref[...].astype(BF16), wp_ref[...], preferred_element_type=F32)
        x3 = x2 + gate * pe
        out_ref[...] = _rms(x3, gfin_ref[...])

    _for_segment(n_p_tiles, body)


def _combine(ys, info, x1, pp, ps, g_ple, w_gate, w_p, g_fin, tm):
    n_p, n_s = pp.shape[0], ps.shape[0]
    n = n_p + n_s
    nt = n // tm
    npt = n_p // tm
    return pl.pallas_call(
        functools.partial(_combine_kernel, n_p_tiles=npt),
        grid=(nt,),
        in_specs=[
            pl.BlockSpec((MOE_CAP, D_MODEL), lambda i: (i, 0)),
            pl.BlockSpec((tm, LANE), lambda i: (i, 0)),
            pl.BlockSpec((tm, D_MODEL), lambda i: (i, 0)),
        ] + _two_segment_specs(tm, PLE_DIM, npt) + [
            pl.BlockSpec((1, D_MODEL), lambda i: (0, 0)),
            pl.BlockSpec((D_MODEL, D_MODEL), lambda i: (0, 0)),
            pl.BlockSpec((PLE_DIM, D_MODEL), lambda i: (0, 0)),
            pl.BlockSpec((1, D_MODEL), lambda i: (0, 0)),
        ],
        out_specs=_two_segment_specs(tm, D_MODEL, npt),
        out_shape=[jax.ShapeDtypeStruct((n_p, D_MODEL), F32), jax.ShapeDtypeStruct((n_s, D_MODEL), F32)],
        compiler_params=pltpu.CompilerParams(dimension_semantics=("arbitrary",), vmem_limit_bytes=VMEM_LIMIT),
        name="combine",
    )(ys, info, x1, pp, ps, g_ple, w_gate, w_p, g_fin)


def _block_tables(seg_len, nb):
    nt = seg_len.shape[0]
    seg_off = jnp.cumsum(seg_len, axis=1) - seg_len
    seg_end = jnp.cumsum(seg_len, axis=0).T
    seg_start = seg_end - seg_len.T
    n_rows = seg_end[:, -1]
    n_blk = (n_rows + MOE_BLK - 1) // MOE_BLK
    blk_end = jnp.cumsum(n_blk)
    b = jnp.arange(nb, dtype=jnp.int32)
    block_e = jnp.minimum(jnp.sum((blk_end[None, :] <= b[:, None]).astype(jnp.int32), axis=1), N_EXPERTS - 1)
    idx = jnp.where(n_blk > 0, jnp.arange(N_EXPERTS, dtype=jnp.int32), N_EXPERTS)
    nxt = jnp.concatenate([lax.cummin(idx, axis=0, reverse=True)[1:], jnp.full((1,), N_EXPERTS, jnp.int32)])
    parity = (jnp.cumsum((n_blk > 0).astype(jnp.int32)) - 1) % 2
    per_e = jnp.concatenate([jnp.stack([blk_end - n_blk, n_rows, nxt, parity], axis=1), seg_start, seg_end],
                            axis=1).astype(F32)
    onehot = (block_e[:, None] == jnp.arange(N_EXPERTS, dtype=jnp.int32)[None, :]).astype(F32)
    per_b = jnp.dot(onehot, per_e, precision=HI).astype(jnp.int32)
    block_j = b - per_b[:, 0]
    base = block_j * MOE_BLK
    t_first = jnp.sum((per_b[:, 4 + nt:] <= base[:, None]).astype(jnp.int32), axis=1)
    t_last = jnp.sum((per_b[:, 4:4 + nt] < (base + MOE_BLK)[:, None]).astype(jnp.int32), axis=1) - 1
    cover = jnp.clip(per_b[:, 1] - base, 0, MOE_BLK)
    seg_shift = (jnp.arange(nt, dtype=jnp.int32)[:, None] * MOE_CAP + seg_off).T - seg_start
    tables = (block_e, block_j, t_first, t_last, cover, blk_end[-1:], seg_start.reshape(-1),
              seg_end.reshape(-1), seg_shift.reshape(-1), per_b[:, 2], per_b[:, 3])
    return tuple(t.astype(jnp.int32) for t in tables)


def _rearranged_in_weights(w_in):
    o = np.cumsum([0, CONV_CH, H_A * DV_A, H_A, H_A, H_B * DK_B, H_B * DK_B, H_B * DV_B, H_B * DV_B, H_B, H_B])
    conv_in, z_a, a_a, b_a, q_b, k_b, v_b, o_b, i_b, f_b = (w_in[:, int(o[j]):int(o[j + 1])] for j in range(10))
    zpad = jnp.zeros((D_MODEL, LANE - DK_B), w_in.dtype)

    def pad_heads(w):
        return jnp.concatenate([jnp.concatenate([w[:, h * DK_B:(h + 1) * DK_B], zpad], axis=1) for h in range(H_B)],
                               axis=1)

    small = jnp.concatenate([a_a, b_a, i_b, f_b], axis=1)
    w_gdn = w_in[:, :GDN_W]
    w_rest = jnp.concatenate([pad_heads(q_b), pad_heads(k_b), v_b, o_b,
                              small, jnp.zeros((D_MODEL, LANE - N_GATE), w_in.dtype)], axis=1)
    return w_gdn.astype(BF16), w_rest.astype(BF16), small.T.astype(BF16)


def _gate_params(a_log, dt_bias, i_bias, f_bias):
    z4 = jnp.zeros((4,), F32)
    alog = jnp.concatenate([a_log.astype(F32), z4, z4, z4])
    bias = jnp.concatenate([dt_bias.astype(F32), z4, i_bias.astype(F32), f_bias.astype(F32)])
    pad = jnp.zeros((LANE - N_GATE,), F32)
    pcol = jnp.zeros((SUBLANE, LANE), F32).at[0].set(jnp.concatenate([alog, pad])).at[1].set(
        jnp.concatenate([bias, pad]))
    prow = jnp.zeros((N_GATE, LANE), F32).at[:, 0].set(alog).at[:, 1].set(bias)
    return pcol, prow


def kernel(x_prompt, x_sample, p_prompt, p_sample, state_conv, state_gdn, state_mlstm_c, state_mlstm_n, state_mlstm_m, norm_attn_g, w_in, conv_w, gdn_a_log, gdn_dt_bias, gdn_norm_g, mlstm_i_bias, mlstm_f_bias, mlstm_norm_g, w_out, norm_moe_g, router_w, router_b, expert_w_gu, expert_b_gu, expert_w_down, expert_b_down, norm_ple_g, ple_gate_w, ple_w, final_norm_g):
    bp, tp, _ = x_prompt.shape
    bs, ts, _ = x_sample.shape
    n_p, n_s = bp * tp, bs * ts
    n = n_p + n_s
    lp, ls = min(tp, CHUNK), min(ts, CHUNK)
    tm = 256
    gp = 4 if bp % 4 == 0 else 1
    gs = 8 if bs % 8 == 0 else 1
    assert tp % lp == 0 and ts % ls == 0 and tp % tm == 0 and n_s % tm == 0 and ls % SUBLANE == 0

    xp = x_prompt.reshape(n_p, D_MODEL)
    xs = x_sample.reshape(n_s, D_MODEL)

    w_gdn, w_rest, ws_t = _rearranged_in_weights(w_in[0])
    pcol, prow = _gate_params(gdn_a_log[0], gdn_dt_bias[0], mlstm_i_bias[0], mlstm_f_bias[0])
    cw = jnp.zeros((SUBLANE, CONV_CH), F32).at[:CONV_W].set(conv_w[0].astype(F32))
    gdn_p, gdn_s, ml_p, ml_s, gate_p, gate_s, gatet_p, gatet_s, conv_p = _inproj(
        xp, xs, norm_attn_g[0].reshape(1, D_MODEL), w_gdn, w_rest, ws_t, pcol, prow, cw, tm, bp)
    gt_p = gatet_p.reshape(N_GATE, bp, tp // lp, lp).transpose(1, 2, 0, 3)
    gt_s = gatet_s.reshape(N_GATE, bs, ts // ls, ls).transpose(1, 2, 0, 3)

    ng_a = gdn_norm_g[0].reshape(1, DV_A).astype(F32)
    ng_b = mlstm_norm_g[0].reshape(H_B, DV_B).astype(F32)
    ma_p, gdn_st_p, mb_p, c_p, nn_p, m_p = _mixers(
        gdn_p.reshape(bp, tp, GDN_W), ml_p.reshape(bp, tp, MLP_W), gate_p.reshape(bp, tp, LANE), gt_p, ng_a, ng_b,
        L=lp, G=gp)
    ma_s, gdn_st_s, mb_s, c_s, nn_s, m_s, conv_s = _mixers(
        gdn_s.reshape(bs, ts, GDN_W), ml_s.reshape(bs, ts, MLP_W), gate_s.reshape(bs, ts, LANE), gt_s, ng_a, ng_b,
        L=ls, G=gs, cw=cw,
        state=(state_conv[0], state_gdn[0], state_mlstm_c[0], state_mlstm_n[0], state_mlstm_m[0].reshape(bs, 1, H_B)))
    half = H_A * DV_A

    rw = jnp.zeros((D_MODEL, LANE), F32).at[:, :N_EXPERTS].set(router_w[0])
    rw_hi = rw.astype(BF16)
    rw = jnp.stack([rw_hi, (rw - rw_hi.astype(F32)).astype(BF16)])
    rb = jnp.full((1, LANE), NEG, F32).at[0, :N_EXPERTS].set(router_b[0])
    x1, x_sorted, info, seg_len = _outproj(xp, xs, ma_p.reshape(n_p, half), ma_s.reshape(n_s, half),
                                           mb_p.reshape(n_p, half), mb_s.reshape(n_s, half),
                                           w_out[0].astype(BF16), norm_moe_g[0].reshape(1, D_MODEL), rw, rb, MOE_TM)

    nt = n // MOE_TM
    nb = -(-(n * TOP_K + nt * N_EXPERTS * (SEG_ALIGN - 1)) // MOE_BLK) + N_EXPERTS
    tables = _block_tables(seg_len[:, :, 0].astype(jnp.int32), nb)
    y_sorted = _experts(tables, x_sorted, expert_w_gu[0], expert_b_gu[0].reshape(N_EXPERTS, 1, 2 * D_FF),
                        expert_w_down[0], expert_b_down[0].reshape(N_EXPERTS, 1, D_MODEL), nt)
    y_p, y_s = _combine(y_sorted, info, x1, p_prompt[0].reshape(n_p, PLE_DIM),
                        p_sample[0].reshape(n_s, PLE_DIM), norm_ple_g[0].reshape(1, D_MODEL),
                        ple_gate_w[0].astype(BF16), ple_w[0].astype(BF16), final_norm_g.reshape(1, D_MODEL), MOE_TM)

    return (y_p.reshape(bp, tp, D_MODEL), y_s.reshape(bs, ts, D_MODEL),
            conv_p[None], gdn_st_p[None], c_p[None], nn_p[None], m_p.reshape(1, bp, H_B),
            conv_s[None], gdn_st_s[None], c_s[None], nn_s[None], m_s.reshape(1, bs, H_B))
```

```python
import functools

import numpy as np
import jax
import jax.numpy as jnp
from jax import lax
from jax.experimental import pallas as pl
from jax.experimental.pallas import tpu as pltpu

F32 = jnp.float32
BF16 = jnp.bfloat16

D_MODEL = 1024
H_A, DK_A, DV_A = 4, 128, 128
H_B, DK_B, DV_B = 4, 64, 128
CONV_W = 4
CONV_CH = H_A * (2 * DK_A + DV_A)
N_EXPERTS = 32
TOP_K = 4
D_FF = 1024
SWIGLU_LIMIT = 7.0
SWIGLU_ALPHA = 1.702
PLE_DIM = 256
EPS = 1e-6
NEG = -1e30
CHUNK = 64

LANE = 128
SUBLANE = 8
GDN_W = CONV_CH + H_A * DV_A
MLP_W = 2 * H_B * LANE + 2 * H_B * DV_B
N_GATE = 16
PROJ_CHUNK = 512

VMEM_LIMIT = 48 * 1024 * 1024

MOE_TM = 256
MOE_BLK = 512
SEG_ALIGN = SUBLANE
MOE_CAP = -(-(MOE_TM * TOP_K + N_EXPERTS * (SEG_ALIGN - 1)) // LANE) * LANE

HI = lax.Precision.HIGHEST

_NN = (((1,), (0,)), ((), ()))
_NT = (((1,), (1,)), ((), ()))
_TN = (((0,), (0,)), ((), ()))


def _dot(a, b, dims=_NN):
    return lax.dot_general(a.astype(BF16), b.astype(BF16), dims, preferred_element_type=F32)


def _dot_hi(a, b, dims=_NN):
    return lax.dot_general(a, b, dims, precision=HI, preferred_element_type=F32)


def _rms(x, g):
    return x * lax.rsqrt(jnp.mean(x * x, axis=-1, keepdims=True) + EPS) * g


def _softplus(t):
    return jnp.maximum(t, 0.0) + jnp.log1p(jnp.exp(-jnp.abs(t)))


def _sigmoid(t):
    return 1.0 / (1.0 + jnp.exp(-t))


def _silu(t):
    return t * _sigmoid(t)


def _activate_gates(raw, idx, alog, bias):
    t = raw + bias
    g = -jnp.exp(alog) * _softplus(t)
    beta = _sigmoid(t)
    lf = -_softplus(-t)
    return jnp.where(idx < 4, g, jnp.where(idx < 8, beta, jnp.where(idx < 12, t, lf)))


def _two_segment_specs(tm, width, n_p_tiles):
    return [pl.BlockSpec((tm, width), lambda i: (jnp.minimum(i, n_p_tiles - 1), 0)),
            pl.BlockSpec((tm, width), lambda i: (jnp.maximum(i - n_p_tiles, 0), 0))]


def _for_segment(n_p_tiles, body):
    i = pl.program_id(0)

    @pl.when(i < n_p_tiles)
    def _():
        body(0)

    @pl.when(i >= n_p_tiles)
    def _():
        body(1)


def _gdn_preactivate_stages(raw_ref, xc_ref, cw_ref, out_ref, cnew_ref, first_of_seq):
    tm = raw_ref.shape[0]
    xc_ref[0:SUBLANE, :] = jnp.where(first_of_seq, 0.0, xc_ref[tm:tm + SUBLANE, :])
    xc_ref[SUBLANE:SUBLANE + tm, :] = raw_ref[:, :CONV_CH]
    out_ref[:, CONV_CH:] = _silu(raw_ref[:, CONV_CH:])
    cnew_ref[0] = xc_ref[SUBLANE + tm - (CONV_W - 1):SUBLANE + tm, :]
    yield
    base = SUBLANE - (CONV_W - 1)
    for c0 in range(0, CONV_CH, DK_A):
        conv = xc_ref[base:base + tm, c0:c0 + DK_A] * cw_ref[0:1, c0:c0 + DK_A]
        for j in range(1, CONV_W):
            conv = conv + xc_ref[base + j:base + j + tm, c0:c0 + DK_A] * cw_ref[j:j + 1, c0:c0 + DK_A]
        act = _silu(conv)
        if c0 < H_A * DK_A:
            act = act * (lax.rsqrt(jnp.sum(act * act, axis=-1, keepdims=True) + EPS) * (DK_A ** -0.5))
        elif c0 < 2 * H_A * DK_A:
            act = act * lax.rsqrt(jnp.sum(act * act, axis=-1, keepdims=True) + EPS)
        out_ref[:, c0:c0 + DK_A] = act
        yield


def _inproj_kernel(xp_ref, xs_ref, g_ref, wa_ref, wb_ref, wst_ref, pc_ref, pr_ref, cw_ref,
                   gdnp_ref, gdns_ref, mlp_ref, mls_ref, gatep_ref, gates_ref, gatetp_ref, gatets_ref, cnew_ref,
                   xc_ref, raw_ref, *, n_p_tiles, tiles_per_seq):
    tm = xp_ref.shape[0]
    i = pl.program_id(0)

    @pl.when(i == 0)
    def _():
        xc_ref[...] = jnp.zeros_like(xc_ref)
        raw_ref[...] = jnp.zeros_like(raw_ref)

    def preactivate_previous_tile():
        return _gdn_preactivate_stages(raw_ref, xc_ref, cw_ref, gdnp_ref, cnew_ref, (i - 1) % tiles_per_seq == 0)

    def projection_stages(seg):
        x_ref = (xp_ref, xs_ref)[seg]
        gdn_dst, ml_ref = (raw_ref, gdns_ref)[seg], (mlp_ref, mls_ref)[seg]
        gate_ref, gatet_ref = (gatep_ref, gates_ref)[seg], (gatetp_ref, gatets_ref)[seg]
        hn = _rms(x_ref[...], g_ref[...]).astype(BF16)
        yield
        for c0 in range(0, GDN_W, PROJ_CHUNK):
            gdn_dst[:, c0:c0 + PROJ_CHUNK] = jnp.dot(hn, wa_ref[:, c0:c0 + PROJ_CHUNK], preferred_element_type=F32)
            yield
        for c0 in range(0, MLP_W, PROJ_CHUNK):
            ml_ref[:, c0:c0 + PROJ_CHUNK] = jnp.dot(hn, wb_ref[:, c0:c0 + PROJ_CHUNK], preferred_element_type=F32)
            yield
        raw = jnp.dot(hn, wb_ref[:, MLP_W:], preferred_element_type=F32)
        lane = lax.broadcasted_iota(jnp.int32, (tm, LANE), 1)
        gate_ref[...] = _activate_gates(raw, lane, pc_ref[0:1, :], pc_ref[1:2, :])
        raw_t = lax.dot_general(wst_ref[...], hn, _NT, preferred_element_type=F32)
        row = lax.broadcasted_iota(jnp.int32, (N_GATE, tm), 0)
        gatet_ref[...] = _activate_gates(raw_t, row, pr_ref[:, 0:1], pr_ref[:, 1:2])

    def body(seg):
        if seg == 0:
            _run_interleaved(preactivate_previous_tile(), projection_stages(0))
        else:
            pl.when(i == n_p_tiles)(lambda: _run_interleaved(preactivate_previous_tile()))
            _run_interleaved(projection_stages(1))

    _for_segment(n_p_tiles, body)


def _inproj(xp, xs, g, w_gdn, w_rest, ws_t, pcol, prow, cw, tm, n_seq_p):
    n_p, n_s = xp.shape[0], xs.shape[0]
    npt = n_p // tm
    tiles_per_seq = npt // n_seq_p

    def out2(width):
        return _two_segment_specs(tm, width, npt)

    def shp2(width):
        return [jax.ShapeDtypeStruct((n_p, width), F32), jax.ShapeDtypeStruct((n_s, width), F32)]

    def prev_tile(i):
        return jnp.clip(i - 1, 0, npt - 1)

    return pl.pallas_call(
        functools.partial(_inproj_kernel, n_p_tiles=npt, tiles_per_seq=tiles_per_seq),
        grid=((n_p + n_s) // tm,),
        in_specs=_two_segment_specs(tm, D_MODEL, npt) + [
            pl.BlockSpec((1, D_MODEL), lambda i: (0, 0)),
            pl.BlockSpec((D_MODEL, GDN_W), lambda i: (0, 0)),
            pl.BlockSpec((D_MODEL, MLP_W + LANE), lambda i: (0, 0)),
            pl.BlockSpec((N_GATE, D_MODEL), lambda i: (0, 0)),
            pl.BlockSpec((SUBLANE, LANE), lambda i: (0, 0)),
            pl.BlockSpec((N_GATE, LANE), lambda i: (0, 0)),
            pl.BlockSpec((SUBLANE, CONV_CH), lambda i: (0, 0)),
        ],
        out_specs=[
            pl.BlockSpec((tm, GDN_W), lambda i: (prev_tile(i), 0)),
            pl.BlockSpec((tm, GDN_W), lambda i: (jnp.maximum(i - npt, 0), 0)),
        ] + out2(MLP_W) + out2(LANE) + [
            pl.BlockSpec((N_GATE, tm), lambda i: (0, jnp.minimum(i, npt - 1))),
            pl.BlockSpec((N_GATE, tm), lambda i: (0, jnp.maximum(i - npt, 0))),
            pl.BlockSpec((1, CONV_W - 1, CONV_CH), lambda i: (prev_tile(i) // tiles_per_seq, 0, 0)),
        ],
        out_shape=shp2(GDN_W) + shp2(MLP_W) + shp2(LANE) + [
            jax.ShapeDtypeStruct((N_GATE, n_p), F32), jax.ShapeDtypeStruct((N_GATE, n_s), F32),
            jax.ShapeDtypeStruct((n_seq_p, CONV_W - 1, CONV_CH), F32)],
        scratch_shapes=[pltpu.VMEM((tm + SUBLANE, CONV_CH), F32), pltpu.VMEM((tm, GDN_W), F32)],
        compiler_params=pltpu.CompilerParams(dimension_semantics=("arbitrary",), vmem_limit_bytes=VMEM_LIMIT),
        name="inproj",
    )(xp, xs, g, w_gdn, w_rest, ws_t, pcol, prow, cw)


def _chunk_masks(L):
    ri = lax.broadcasted_iota(jnp.int32, (L, L), 0)
    ci = lax.broadcasted_iota(jnp.int32, (L, L), 1)
    return ri >= ci, ri > ci, ri <= ci


def _run_interleaved(*stage_generators):
    live = list(stage_generators)
    while live:
        for gen in list(live):
            if next(gen, StopIteration) is StopIteration:
                live.remove(gen)


def _gdn_stages(*refs, L, G, has_state):
    if has_state:
        (xin_ref, gate_ref, gatet_ref, cw_ref, ng_ref, cst_ref, s0_ref,
         mix_ref, cnew_ref, snew_ref, xc_ref, s_ref) = refs
    else:
        xin_ref, gate_ref, gatet_ref, ng_ref, mix_ref, snew_ref, s_ref = refs
    c = pl.program_id(1)

    @pl.when(c == 0)
    def _():
        if has_state:
            xc_ref[:, 0:SUBLANE, :] = jnp.zeros((G, SUBLANE, CONV_CH), F32)
            xc_ref[:, SUBLANE - (CONV_W - 1):SUBLANE, :] = cst_ref[...]
            s_ref[...] = s0_ref[...]
        else:
            s_ref[...] = jnp.zeros_like(s_ref)

    if has_state:
        @pl.when(c > 0)
        def _():
            xc_ref[:, 0:SUBLANE, :] = xc_ref[:, L:L + SUBLANE, :]

    yield
    tril, strict, triu = _chunk_masks(L)
    tril_f, triu_f = tril.astype(F32), triu.astype(F32)
    base = SUBLANE - (CONV_W - 1)

    chains = [(g, h) for g in range(G) for h in range(H_A)]
    s_old = [s_ref[g, h] for g, h in chains]
    if has_state:
        for g in range(G):
            xc_ref[g, SUBLANE:SUBLANE + L, :] = xin_ref[g, :, :CONV_CH]

    q, k, v, beta, gc, gl, decay = [], [], [], [], [], [], []
    for g in range(G):
        if has_state:
            conv = xc_ref[g, base:base + L, :] * cw_ref[0:1, :]
            for j in range(1, CONV_W):
                conv = conv + xc_ref[g, base + j:base + j + L, :] * cw_ref[j:j + 1, :]
            cnew_ref[g] = xc_ref[g, SUBLANE + L - (CONV_W - 1):SUBLANE + L, :]
            act = _silu(conv)
        else:
            act = xin_ref[g, :, :CONV_CH]
        gact = gate_ref[g]
        cum_c = _dot_hi(tril_f, gact)
        cum_r = _dot_hi(gatet_ref[g, 0], triu_f)
        for h in range(H_A):
            q.append(act[:, h * DK_A:(h + 1) * DK_A])
            k.append(act[:, H_A * DK_A + h * DK_A:H_A * DK_A + (h + 1) * DK_A])
            v.append(act[:, 2 * H_A * DK_A + h * DV_A:2 * H_A * DK_A + (h + 1) * DV_A])
            beta.append(gact[:, 4 + h:5 + h])
            gc.append(cum_c[:, h:h + 1])
            gl.append(cum_c[L - 1:L, h:h + 1])
            gr = cum_r[h:h + 1, :]
            decay.append(jnp.where(tril, jnp.exp(jnp.where(tril, cum_c[:, h:h + 1] - gr, 0.0)), 0.0))
        yield

    nc = range(len(chains))
    if has_state:
        qss = [jnp.sum(q[i] * q[i], axis=-1, keepdims=True) for i in nc]
        kss = [jnp.sum(k[i] * k[i], axis=-1, keepdims=True) for i in nc]
        q = [q[i] * (lax.rsqrt(qss[i] + EPS) * (DK_A ** -0.5)) for i in nc]
        k = [k[i] * lax.rsqrt(kss[i] + EPS) for i in nc]
    kb = [k[i] * beta[i] for i in nc]
    egc = [jnp.exp(gc[i]) for i in nc]
    yield
    kk = [_dot(kb[i], k[i], _NT) for i in nc]
    yield
    qk = [_dot(q[i], k[i], _NT) for i in nc]
    yield
    eye = (lax.broadcasted_iota(jnp.int32, (L, L), 0) == lax.broadcasted_iota(jnp.int32, (L, L), 1)).astype(F32)
    pw = [-jnp.where(strict, kk[i] * decay[i], 0.0) for i in nc]
    t_inv = [eye + pw[i] for i in nc]
    span = 2
    while span < L:
        yield
        pw = [_dot(pw[i], pw[i]) for i in nc]
        yield
        t_inv = [t_inv[i] + _dot(t_inv[i], pw[i]) for i in nc]
        span *= 2
    yield
    sol = [_dot(t_inv[i], jnp.concatenate([v[i] * beta[i], kb[i] * egc[i]], axis=-1)) for i in nc]
    yield
    qs = [_dot(q[i] * egc[i], s_old[i]) for i in nc]
    yield
    ws = [_dot(sol[i][:, DV_A:], s_old[i]) for i in nc]
    v_new = [sol[i][:, :DV_A] - ws[i] for i in nc]
    yield
    o = [qs[i] + _dot(jnp.where(tril, qk[i] * decay[i], 0.0), v_new[i]) for i in nc]
    yield
    s_new = [s_old[i] * jnp.exp(gl[i]) + _dot(k[i] * jnp.exp(gl[i] - gc[i]), v_new[i], _TN) for i in nc]
    yield
    ms = [jnp.mean(o[i] * o[i], axis=-1, keepdims=True) for i in nc]
    on = [o[i] * lax.rsqrt(ms[i] + EPS) for i in nc]
    yield
    for i, (g, h) in enumerate(chains):
        z = xin_ref[g, :, CONV_CH + h * DV_A:CONV_CH + (h + 1) * DV_A]
        out = on[i] * ng_ref[...] * (_silu(z) if has_state else z)
        mix_ref[g, :, h * DV_A:(h + 1) * DV_A] = out.astype(mix_ref.dtype)
    yield
    for i, (g, h) in enumerate(chains):
        s_ref[g, h] = s_new[i]
        snew_ref[g, h] = s_new[i]


def _mlstm_stages(*refs, L, G, has_state):
    if has_state:
        (xin_ref, gate_ref, gatet_ref, ng_ref, c0_ref, n0_ref, m0_ref,
         mix_ref, cnew_ref, nnew_ref, mnew_ref, c_ref, n_ref, m_ref) = refs
    else:
        (xin_ref, gate_ref, gatet_ref, ng_ref,
         mix_ref, cnew_ref, nnew_ref, mnew_ref, c_ref, n_ref, m_ref) = refs
    c = pl.program_id(1)

    @pl.when(c == 0)
    def _():
        c_ref[...] = jnp.zeros_like(c_ref)
        n_ref[...] = jnp.zeros_like(n_ref)
        m_ref[...] = jnp.zeros_like(m_ref)
        if has_state:
            c_ref[:, :, 0:DK_B, :] = c0_ref[...]
            n_ref[:, 0:H_B, 0:DK_B] = n0_ref[...]
            m_ref[:, 0:1, 0:H_B] = m0_ref[...]

    yield
    tril, _, triu = _chunk_masks(L)
    tril_f, triu_f = tril.astype(F32), triu.astype(F32)

    chains = [(g, h) for g in range(G) for h in range(H_B)]
    nc = range(len(chains))
    c_old = [c_ref[g, h] for g, h in chains]
    n_old = [n_ref[g, h:h + 1, :] for g, h in chains]
    m_old = [m_ref[g, 0:1, h:h + 1] for g, h in chains]

    v0 = 2 * H_B * LANE
    q = [xin_ref[g, :, h * LANE:(h + 1) * LANE] * (DK_B ** -0.5) for g, h in chains]
    k = [xin_ref[g, :, (H_B + h) * LANE:(H_B + h + 1) * LANE] for g, h in chains]
    v = [xin_ref[g, :, v0 + h * DV_B:v0 + (h + 1) * DV_B] for g, h in chains]
    ig_c, b_c, b_last, d_log = [], [], [], []
    for g in range(G):
        gact = gate_ref[g]
        gact_t = gatet_ref[g, 0]
        cum_c = _dot_hi(tril_f, gact)
        cum_r = _dot_hi(gact_t, triu_f)
        for h in range(H_B):
            ig_c.append(gact[:, 8 + h:9 + h])
            b_c.append(cum_c[:, 12 + h:13 + h])
            b_last.append(cum_c[L - 1:L, 12 + h:13 + h])
            d_log.append(jnp.where(tril, cum_c[:, 12 + h:13 + h] - cum_r[12 + h:13 + h, :]
                                   + gact_t[8 + h:9 + h, :], NEG))
        yield
    qk = [_dot(q[i], k[i], _NT) for i in nc]
    yield
    qc = [_dot(q[i], c_old[i]) for i in nc]
    yield
    inter = [b_c[i] + m_old[i] for i in nc]
    m_t = [jnp.maximum(inter[i], jnp.max(d_log[i], axis=-1, keepdims=True)) for i in nc]
    yield
    s = [qk[i] * jnp.exp(d_log[i] - m_t[i]) for i in nc]
    e_inter = [jnp.exp(inter[i] - m_t[i]) for i in nc]
    yield
    sv = [_dot(s[i], v[i]) for i in nc]
    yield
    m_new = [m_t[i][L - 1:L, :] for i in nc]
    kw = [k[i] * jnp.exp(b_last[i] - b_c[i] + ig_c[i] - m_new[i]) for i in nc]
    f_tot = [jnp.exp(b_last[i] + m_old[i] - m_new[i]) for i in nc]
    yield
    c_new = [f_tot[i] * c_old[i] + _dot(kw[i], v[i], _TN) for i in nc]
    yield
    n_new = [f_tot[i] * n_old[i] + jnp.sum(kw[i], axis=0, keepdims=True) for i in nc]
    qn = [jnp.sum(q[i] * n_old[i], axis=-1, keepdims=True) for i in nc]
    yield
    ssum = [jnp.sum(s[i], axis=-1, keepdims=True) for i in nc]
    yield
    den = [jnp.maximum(jnp.abs(e_inter[i] * qn[i] + ssum[i]), jnp.exp(-m_t[i])) for i in nc]
    hh = [(e_inter[i] * qc[i] + sv[i]) / den[i] for i in nc]
    yield
    ms = [jnp.mean(hh[i] * hh[i], axis=-1, keepdims=True) for i in nc]
    hn = [hh[i] * lax.rsqrt(ms[i] + EPS) for i in nc]
    yield
    for i, (g, h) in enumerate(chains):
        og = xin_ref[g, :, v0 + H_B * DV_B + h * DV_B:v0 + H_B * DV_B + (h + 1) * DV_B]
        mix_ref[g, :, h * DV_B:(h + 1) * DV_B] = (hn[i] * ng_ref[h:h + 1, :] * _sigmoid(og)).astype(mix_ref.dtype)
    yield
    for i, (g, h) in enumerate(chains):
        c_ref[g, h] = c_new[i]
        n_ref[g, h:h + 1, :] = n_new[i]
        m_ref[g, 0:1, h:h + 1] = m_new[i]
        cnew_ref[g, h] = c_new[i][0:DK_B, :]
        nnew_ref[g, h:h + 1, :] = n_new[i][:, 0:DK_B]
        mnew_ref[g, 0:1, h:h + 1] = m_new[i]


def _mixers_kernel(*refs, L, G, has_state):
    if has_state:
        (gdn_ref, gate_ref, gatet_ref, nga_ref, ml_ref, ngb_ref, cw_ref, cst_ref, s0_ref, c0_ref, n0_ref, m0_ref,
         mixa_ref, snew_ref, mixb_ref, cnew_ref, nnew_ref, mnew_ref, convnew_ref,
         s_ref, c_ref, n_ref, m_ref, xc_ref) = refs
        gdn_refs = (gdn_ref, gate_ref, gatet_ref, cw_ref, nga_ref, cst_ref, s0_ref,
                    mixa_ref, convnew_ref, snew_ref, xc_ref, s_ref)
        ml_refs = (ml_ref, gate_ref, gatet_ref, ngb_ref, c0_ref, n0_ref, m0_ref,
                   mixb_ref, cnew_ref, nnew_ref, mnew_ref, c_ref, n_ref, m_ref)
    else:
        (gdn_ref, gate_ref, gatet_ref, nga_ref, ml_ref, ngb_ref,
         mixa_ref, snew_ref, mixb_ref, cnew_ref, nnew_ref, mnew_ref, s_ref, c_ref, n_ref, m_ref) = refs
        gdn_refs = (gdn_ref, gate_ref, gatet_ref, nga_ref, mixa_ref, snew_ref, s_ref)
        ml_refs = (ml_ref, gate_ref, gatet_ref, ngb_ref, mixb_ref, cnew_ref, nnew_ref, mnew_ref, c_ref, n_ref, m_ref)
    _run_interleaved(_gdn_stages(*gdn_refs, L=L, G=G, has_state=has_state),
                     _mlstm_stages(*ml_refs, L=L, G=G, has_state=has_state))


def _mixers(gdn_in, ml_in, gates, gates_t, ng_a, ng_b, *, L, G, cw=None, state=None):
    n_seq, T, _ = gdn_in.shape
    n_c = T // L
    has_state = state is not None

    def seq_blk(*tail):
        return pl.BlockSpec((G,) + tail, lambda b, c: (b,) + (0,) * len(tail))

    def tok_blk(width):
        return pl.BlockSpec((G, L, width), lambda b, c: (b, c, 0))

    def seq_shape(*tail):
        return jax.ShapeDtypeStruct((n_seq,) + tail, F32)

    state_specs = [seq_blk(H_A, DK_A, DV_A), seq_blk(H_B, DK_B, DV_B), seq_blk(H_B, DK_B), seq_blk(1, H_B)]
    state_shapes = [seq_shape(H_A, DK_A, DV_A), seq_shape(H_B, DK_B, DV_B), seq_shape(H_B, DK_B), seq_shape(1, H_B)]
    conv_spec, conv_shape = seq_blk(CONV_W - 1, CONV_CH), seq_shape(CONV_W - 1, CONV_CH)
    in_specs = [
        tok_blk(GDN_W), tok_blk(LANE),
        pl.BlockSpec((G, 1, N_GATE, L), lambda b, c: (b, c, 0, 0)),
        pl.BlockSpec((1, DV_A), lambda b, c: (0, 0)),
        tok_blk(MLP_W),
        pl.BlockSpec((H_B, DV_B), lambda b, c: (0, 0)),
    ]
    args = [gdn_in, gates, gates_t, ng_a, ml_in, ng_b]
    out_specs = [tok_blk(H_A * DV_A), state_specs[0], tok_blk(H_B * DV_B)] + state_specs[1:]
    mix_dtype = BF16 if L % (2 * SUBLANE) == 0 else F32
    mix_a = jax.ShapeDtypeStruct((n_seq, T, H_A * DV_A), mix_dtype)
    mix_b = jax.ShapeDtypeStruct((n_seq, T, H_B * DV_B), mix_dtype)
    out_shape = [mix_a, state_shapes[0], mix_b] + state_shapes[1:]
    scratch = [pltpu.VMEM((G, H_A, DK_A, DV_A), F32), pltpu.VMEM((G, H_B, LANE, DV_B), F32),
               pltpu.VMEM((G, SUBLANE, LANE), F32), pltpu.VMEM((G, SUBLANE, LANE), F32)]
    if has_state:
        in_specs += [pl.BlockSpec((SUBLANE, CONV_CH), lambda b, c: (0, 0)), conv_spec] + state_specs
        args += [cw] + list(state)
        out_specs.append(conv_spec)
        out_shape.append(conv_shape)
        scratch.append(pltpu.VMEM((G, L + SUBLANE, CONV_CH), F32))
    return pl.pallas_call(
        functools.partial(_mixers_kernel, L=L, G=G, has_state=has_state),
        grid=(n_seq // G, n_c),
        in_specs=in_specs,
        out_specs=out_specs,
        out_shape=out_shape,
        scratch_shapes=scratch,
        compiler_params=pltpu.CompilerParams(dimension_semantics=("parallel", "arbitrary"),
                                             vmem_limit_bytes=VMEM_LIMIT),
        name=f"mixers_L{L}",
    )(*args)


def _outproj_kernel(xp_ref, xs_ref, map_ref, mas_ref, mbp_ref, mbs_ref, wo_ref, g_ref, rw_ref, rb_ref,
                    x1_ref, xsort_ref, info_ref, cpad_ref, *, n_p_tiles):
    half = H_A * DV_A
    tm = xp_ref.shape[0]

    def body(seg):
        x_ref, ma_ref, mb_ref = (xp_ref, xs_ref)[seg], (map_ref, mas_ref)[seg], (mbp_ref, mbs_ref)[seg]
        x1 = (x_ref[...] + jnp.dot(ma_ref[...].astype(BF16), wo_ref[:half, :], preferred_element_type=F32)
              + jnp.dot(mb_ref[...].astype(BF16), wo_ref[half:, :], preferred_element_type=F32))
        x1_ref[...] = x1
        hn = _rms(x1, g_ref[...])
        hn_hi = hn.astype(BF16)
        hn_lo = (hn - hn_hi.astype(F32)).astype(BF16)
        logits = (jnp.dot(hn_hi, rw_ref[0], preferred_element_type=F32)
                  + jnp.dot(hn_hi, rw_ref[1], preferred_element_type=F32)
                  + jnp.dot(hn_lo, rw_ref[0], preferred_element_type=F32)) + rb_ref[...]

        vals = logits.T[:N_EXPERTS, :]
        e_iota = lax.broadcasted_iota(jnp.int32, (N_EXPERTS, tm), 0)
        sels, tops = [], []
        for _ in range(TOP_K):
            m = jnp.max(vals, axis=0, keepdims=True)
            first = jnp.min(jnp.where(vals == m, e_iota, N_EXPERTS), axis=0, keepdims=True)
            sel = e_iota == first
            vals = jnp.where(sel, -jnp.inf, vals)
            sels.append(sel)
            tops.append(m)
        ex = [jnp.exp(t - tops[0]) for t in tops]
        den = ex[0] + ex[1] + ex[2] + ex[3]
        gates = [e / den for e in ex]
        mask = sels[0].astype(F32) + sels[1].astype(F32) + sels[2].astype(F32) + sels[3].astype(F32)
        ri = lax.broadcasted_iota(jnp.int32, (tm, tm), 0)
        ci = lax.broadcasted_iota(jnp.int32, (tm, tm), 1)
        rank = _dot(mask, (ri < ci).astype(F32))
        cnt = jnp.sum(mask, axis=1, keepdims=True)
        cpad = jnp.ceil(cnt * (1.0 / SEG_ALIGN)) * SEG_ALIGN
        cpad_b = jnp.broadcast_to(cpad, (N_EXPERTS, tm))
        er = lax.broadcasted_iota(jnp.int32, (N_EXPERTS, N_EXPERTS), 0)
        ec = lax.broadcasted_iota(jnp.int32, (N_EXPERTS, N_EXPERTS), 1)
        seg_off = _dot((er > ec).astype(F32), cpad_b)
        pos = seg_off + rank
        q = [jnp.sum(jnp.where(s, pos, 0.0), axis=0, keepdims=True) for s in sels]

        j_iota = lax.broadcasted_iota(jnp.int32, (MOE_CAP, tm), 0).astype(F32)
        perm = jnp.zeros((MOE_CAP, tm), F32)
        for kk in range(TOP_K):
            perm = jnp.where(j_iota == q[kk], 1.0, perm)
        xsorted = _dot(perm, hn)
        xsort_ref[...] = xsorted

        r_iota = lax.broadcasted_iota(jnp.int32, (LANE, tm), 0)
        info = jnp.zeros((LANE, tm), F32)
        for kk in range(TOP_K):
            info = jnp.where(r_iota == kk, q[kk], info)
            info = jnp.where(r_iota == TOP_K + kk, gates[kk], info)
        info_ref[...] = info.T
        cpad_ref[0] = cpad_b[:, :LANE]

    _for_segment(n_p_tiles, body)


def _outproj(xp, xs, ma_p, ma_s, mb_p, mb_s, w_out, g, rw, rb, tm):
    n_p, n_s = xp.shape[0], xs.shape[0]
    n = n_p + n_s
    nt = n // tm
    npt = n_p // tm
    half = H_A * DV_A
    return pl.pallas_call(
        functools.partial(_outproj_kernel, n_p_tiles=npt),
        grid=(nt,),
        in_specs=_two_segment_specs(tm, D_MODEL, npt) + _two_segment_specs(tm, half, npt)
        + _two_segment_specs(tm, half, npt) + [
            pl.BlockSpec((D_MODEL, D_MODEL), lambda i: (0, 0)),
            pl.BlockSpec((1, D_MODEL), lambda i: (0, 0)),
            pl.BlockSpec((2, D_MODEL, LANE), lambda i: (0, 0, 0)),
            pl.BlockSpec((1, LANE), lambda i: (0, 0)),
        ],
        out_specs=[
            pl.BlockSpec((tm, D_MODEL), lambda i: (i, 0)),
            pl.BlockSpec((MOE_CAP, D_MODEL), lambda i: (i, 0)),
            pl.BlockSpec((tm, LANE), lambda i: (i, 0)),
            pl.BlockSpec((1, N_EXPERTS, LANE), lambda i: (i, 0, 0)),
        ],
        out_shape=[
            jax.ShapeDtypeStruct((n, D_MODEL), F32),
            jax.ShapeDtypeStruct((nt * MOE_CAP, D_MODEL), F32),
            jax.ShapeDtypeStruct((n, LANE), F32),
            jax.ShapeDtypeStruct((nt, N_EXPERTS, LANE), F32),
        ],
        compiler_params=pltpu.CompilerParams(dimension_semantics=("arbitrary",), vmem_limit_bytes=VMEM_LIMIT),
        name="outproj",
    )(xp, xs, ma_p, ma_s, mb_p, mb_s, w_out, g, rw, rb)


def _expert_kernel(be_ref, bj_ref, tf_ref, tl_ref, cov_ref, nu_ref, vt_ref, ct_ref, lt_ref, nx_ref, ws_ref,
                   xs_hbm, wgu_hbm, bgu_ref, wd_hbm, bd_ref, ys_hbm,
                   xbuf, ybuf, gsem, ssem, wgu_st, wd_st, wsem, wgu_bf, wd_bf, *, nt):
    b = pl.program_id(0)
    n_used = nu_ref[0]
    slot = b % 2

    def start_pieces(bb, copy, s):
        e = be_ref[bb]
        base = bj_ref[bb] * MOE_BLK

        def body(t, carry):
            k = e * nt + t
            lo = jnp.maximum(vt_ref[k], base)
            ln = jnp.minimum(ct_ref[k], base + MOE_BLK) - lo

            @pl.when(ln > 0)
            def _():
                copy(s, pl.multiple_of(lt_ref[k] + lo, SEG_ALIGN), pl.multiple_of(lo - base, SEG_ALIGN),
                     pl.multiple_of(ln, SEG_ALIGN)).start()
            return carry

        lax.fori_loop(tf_ref[bb], tl_ref[bb] + 1, body, 0)

    def weight_copies(e):
        return (pltpu.make_async_copy(wgu_hbm.at[e], wgu_st, wsem.at[0]),
                pltpu.make_async_copy(wd_hbm.at[e], wd_st, wsem.at[1]))

    def cast_weights(p):
        wgu_bf[p] = wgu_st[...].astype(BF16)
        wd_bf[p] = wd_st[...].astype(BF16)

    def gather_copy(s, src, dst, size):
        return pltpu.make_async_copy(xs_hbm.at[pl.ds(src, size)], xbuf.at[s, pl.ds(dst, size)], gsem.at[s])

    def scatter_copy(s, src, dst, size):
        return pltpu.make_async_copy(ybuf.at[s, pl.ds(dst, size)], ys_hbm.at[pl.ds(src, size)], ssem.at[s])

    def wait_rows(count, copy, s):
        @pl.when(count > 0)
        def _():
            copy(s, 0, 0, pl.multiple_of(count, SEG_ALIGN)).wait()

    @pl.when(b == 0)
    def _():
        xbuf[...] = jnp.zeros_like(xbuf)
        start_pieces(0, gather_copy, 0)

    @pl.when(b + 1 < n_used)
    def _():
        start_pieces(b + 1, gather_copy, 1 - slot)

    @pl.when(b < n_used)
    def _():
        e = be_ref[b]
        first = jnp.logical_or(b == 0, be_ref[jnp.maximum(b - 1, 0)] != e)
        last = jnp.logical_or(b == n_used - 1, be_ref[jnp.minimum(b + 1, n_used - 1)] != e)
        has_next = nx_ref[b] < N_EXPERTS
        p = ws_ref[b]

        @pl.when(b == 0)
        def _():
            for cp in weight_copies(e):
                cp.start()
            for cp in weight_copies(e):
                cp.wait()
            cast_weights(p)

        @pl.when(jnp.logical_and(first, has_next))
        def _():
            for cp in weight_copies(nx_ref[b]):
                cp.start()

        wait_rows(cov_ref[b], gather_copy, slot)

        @pl.when(b >= 2)
        def _():
            wait_rows(cov_ref[jnp.maximum(b - 2, 0)], scatter_copy, slot)

        def expert_mlp(rows):
            hgu = jnp.dot(xbuf[slot, :rows].astype(BF16), wgu_bf[p], preferred_element_type=F32) + bgu_ref[0]
            gate = jnp.minimum(hgu[:, :D_FF], SWIGLU_LIMIT)
            up = jnp.clip(hgu[:, D_FF:], -SWIGLU_LIMIT, SWIGLU_LIMIT)
            act = (up + 1.0) * gate * _sigmoid(SWIGLU_ALPHA * gate)
            ybuf[slot, :rows] = jnp.dot(act.astype(BF16), wd_bf[p], preferred_element_type=F32) + bd_ref[0]

        quarter = MOE_BLK // 4
        for nq in range(1, 5):
            pl.when(jnp.logical_and(cov_ref[b] > (nq - 1) * quarter, cov_ref[b] <= nq * quarter))(
                functools.partial(expert_mlp, nq * quarter))

        @pl.when(jnp.logical_and(last, has_next))
        def _():
            for cp in weight_copies(nx_ref[b]):
                cp.wait()
            cast_weights(1 - p)

        start_pieces(b, scatter_copy, slot)

        @pl.when(b == n_used - 1)
        def _():
            wait_rows(cov_ref[b], scatter_copy, slot)
            wait_rows(jnp.where(b >= 1, cov_ref[jnp.maximum(b - 1, 0)], 0), scatter_copy, 1 - slot)


def _experts(tables, xs, w_gu, b_gu, w_down, b_down, nt):
    nb = tables[0].shape[0]

    def bias_blk(b, *t):
        return (t[0][jnp.minimum(b, t[5][0] - 1)], 0, 0)

    grid_spec = pltpu.PrefetchScalarGridSpec(
        num_scalar_prefetch=len(tables),
        grid=(nb,),
        in_specs=[
            pl.BlockSpec(memory_space=pl.ANY),
            pl.BlockSpec(memory_space=pl.ANY),
            pl.BlockSpec((1, 1, 2 * D_FF), bias_blk),
            pl.BlockSpec(memory_space=pl.ANY),
            pl.BlockSpec((1, 1, D_MODEL), bias_blk),
        ],
        out_specs=pl.BlockSpec(memory_space=pl.ANY),
        scratch_shapes=[
            pltpu.VMEM((2, MOE_BLK, D_MODEL), F32),
            pltpu.VMEM((2, MOE_BLK, D_MODEL), F32),
            pltpu.SemaphoreType.DMA((2,)),
            pltpu.SemaphoreType.DMA((2,)),
            pltpu.VMEM((D_MODEL, 2 * D_FF), F32),
            pltpu.VMEM((D_FF, D_MODEL), F32),
            pltpu.SemaphoreType.DMA((2,)),
            pltpu.VMEM((2, D_MODEL, 2 * D_FF), BF16),
            pltpu.VMEM((2, D_FF, D_MODEL), BF16),
        ],
    )
    return pl.pallas_call(
        functools.partial(_expert_kernel, nt=nt),
        grid_spec=grid_spec,
        out_shape=jax.ShapeDtypeStruct(xs.shape, xs.dtype),
        input_output_aliases={len(tables): 0},
        compiler_params=pltpu.CompilerParams(dimension_semantics=("arbitrary",), vmem_limit_bytes=VMEM_LIMIT),
        name="experts",
    )(*tables, xs, w_gu, b_gu, w_down, b_down)


def _combine_kernel(ys_ref, info_ref, x1_ref, pp_ref, ps_ref, gple_ref, wg_ref, wp_ref, gfin_ref,
                    outp_ref, outs_ref, *, n_p_tiles):
    tm = x1_ref.shape[0]

    def body(seg):
        p_ref, out_ref = (pp_ref, ps_ref)[seg], (outp_ref, outs_ref)[seg]
        info = info_ref[...]
        j_iota = lax.broadcasted_iota(jnp.int32, (tm, MOE_CAP), 1).astype(F32)
        gmat = jnp.zeros((tm, MOE_CAP), F32)
        for kk in range(TOP_K):
            gmat = jnp.where(j_iota == info[:, kk:kk + 1], info[:, TOP_K + kk:TOP_K + kk + 1], gmat)
        x2 = x1_ref[...] + jnp.dot(gmat.astype(BF16), ys_ref[...].astype(BF16), preferred_element_type=F32)
        hn = _rms(x2, gple_ref[...]).astype(BF16)
        gate = _sigmoid(jnp.dot(hn, wg_ref[...], preferred_element_type=F32))
        pe = jnp.dot(p_ref[...].astype(BF16), wp_ref[...], preferred_element_type=F32)
        x3 = x2 + gate * pe
        out_ref[...] = _rms(x3, gfin_ref[...])

    _for_segment(n_p_tiles, body)


def _combine(ys, info, x1, pp, ps, g_ple, w_gate, w_p, g_fin, tm):
    n_p, n_s = pp.shape[0], ps.shape[0]
    n = n_p + n_s
    nt = n // tm
    npt = n_p // tm
    return pl.pallas_call(
        functools.partial(_combine_kernel, n_p_tiles=npt),
        grid=(nt,),
        in_specs=[
            pl.BlockSpec((MOE_CAP, D_MODEL), lambda i: (i, 0)),
            pl.BlockSpec((tm, LANE), lambda i: (i, 0)),
            pl.BlockSpec((tm, D_MODEL), lambda i: (i, 0)),
        ] + _two_segment_specs(tm, PLE_DIM, npt) + [
            pl.BlockSpec((1, D_MODEL), lambda i: (0, 0)),
            pl.BlockSpec((D_MODEL, D_MODEL), lambda i: (0, 0)),
            pl.BlockSpec((PLE_DIM, D_MODEL), lambda i: (0, 0)),
            pl.BlockSpec((1, D_MODEL), lambda i: (0, 0)),
        ],
        out_specs=_two_segment_specs(tm, D_MODEL, npt),
        out_shape=[jax.ShapeDtypeStruct((n_p, D_MODEL), F32), jax.ShapeDtypeStruct((n_s, D_MODEL), F32)],
        compiler_params=pltpu.CompilerParams(dimension_semantics=("arbitrary",), vmem_limit_bytes=VMEM_LIMIT),
        name="combine",
    )(ys, info, x1, pp, ps, g_ple, w_gate, w_p, g_fin)


def _block_tables(seg_len, nb):
    nt = seg_len.shape[0]
    seg_off = jnp.cumsum(seg_len, axis=1) - seg_len
    seg_end = jnp.cumsum(seg_len, axis=0).T
    seg_start = seg_end - seg_len.T
    n_rows = seg_end[:, -1]
    n_blk = (n_rows + MOE_BLK - 1) // MOE_BLK
    blk_end = jnp.cumsum(n_blk)
    b = jnp.arange(nb, dtype=jnp.int32)
    block_e = jnp.minimum(jnp.sum((blk_end[None, :] <= b[:, None]).astype(jnp.int32), axis=1), N_EXPERTS - 1)
    idx = jnp.where(n_blk > 0, jnp.arange(N_EXPERTS, dtype=jnp.int32), N_EXPERTS)
    nxt = jnp.concatenate([lax.cummin(idx, axis=0, reverse=True)[1:], jnp.full((1,), N_EXPERTS, jnp.int32)])
    parity = (jnp.cumsum((n_blk > 0).astype(jnp.int32)) - 1) % 2
    per_e = jnp.concatenate([jnp.stack([blk_end - n_blk, n_rows, nxt, parity], axis=1), seg_start, seg_end],
                            axis=1).astype(F32)
    onehot = (block_e[:, None] == jnp.arange(N_EXPERTS, dtype=jnp.int32)[None, :]).astype(F32)
    per_b = jnp.dot(onehot, per_e, precision=HI).astype(jnp.int32)
    block_j = b - per_b[:, 0]
    base = block_j * MOE_BLK
    t_first = jnp.sum((per_b[:, 4 + nt:] <= base[:, None]).astype(jnp.int32), axis=1)
    t_last = jnp.sum((per_b[:, 4:4 + nt] < (base + MOE_BLK)[:, None]).astype(jnp.int32), axis=1) - 1
    cover = jnp.clip(per_b[:, 1] - base, 0, MOE_BLK)
    seg_shift = (jnp.arange(nt, dtype=jnp.int32)[:, None] * MOE_CAP + seg_off).T - seg_start
    tables = (block_e, block_j, t_first, t_last, cover, blk_end[-1:], seg_start.reshape(-1),
              seg_end.reshape(-1), seg_shift.reshape(-1), per_b[:, 2], per_b[:, 3])
    return tuple(t.astype(jnp.int32) for t in tables)


def _rearranged_in_weights(w_in):
    o = np.cumsum([0, CONV_CH, H_A * DV_A, H_A, H_A, H_B * DK_B, H_B * DK_B, H_B * DV_B, H_B * DV_B, H_B, H_B])
    conv_in, z_a, a_a, b_a, q_b, k_b, v_b, o_b, i_b, f_b = (w_in[:, int(o[j]):int(o[j + 1])] for j in range(10))
    zpad = jnp.zeros((D_MODEL, LANE - DK_B), w_in.dtype)

    def pad_heads(w):
        return jnp.concatenate([jnp.concatenate([w[:, h * DK_B:(h + 1) * DK_B], zpad], axis=1) for h in range(H_B)],
                               axis=1)

    small = jnp.concatenate([a_a, b_a, i_b, f_b], axis=1)
    w_gdn = w_in[:, :GDN_W]
    w_rest = jnp.concatenate([pad_heads(q_b), pad_heads(k_b), v_b, o_b,
                              small, jnp.zeros((D_MODEL, LANE - N_GATE), w_in.dtype)], axis=1)
    return w_gdn.astype(BF16), w_rest.astype(BF16), small.T.astype(BF16)


def _gate_params(a_log, dt_bias, i_bias, f_bias):
    z4 = jnp.zeros((4,), F32)
    alog = jnp.concatenate([a_log.astype(F32), z4, z4, z4])
    bias = jnp.concatenate([dt_bias.astype(F32), z4, i_bias.astype(F32), f_bias.astype(F32)])
    pad = jnp.zeros((LANE - N_GATE,), F32)
    pcol = jnp.zeros((SUBLANE, LANE), F32).at[0].set(jnp.concatenate([alog, pad])).at[1].set(
        jnp.concatenate([bias, pad]))
    prow = jnp.zeros((N_GATE, LANE), F32).at[:, 0].set(alog).at[:, 1].set(bias)
    return pcol, prow


def kernel(x_prompt, x_sample, p_prompt, p_sample, state_conv, state_gdn, state_mlstm_c, state_mlstm_n, state_mlstm_m, norm_attn_g, w_in, conv_w, gdn_a_log, gdn_dt_bias, gdn_norm_g, mlstm_i_bias, mlstm_f_bias, mlstm_norm_g, w_out, norm_moe_g, router_w, router_b, expert_w_gu, expert_b_gu, expert_w_down, expert_b_down, norm_ple_g, ple_gate_w, ple_w, final_norm_g):
    bp, tp, _ = x_prompt.shape
    bs, ts, _ = x_sample.shape
    n_p, n_s = bp * tp, bs * ts
    n = n_p + n_s
    lp, ls = min(tp, CHUNK), min(ts, CHUNK)
    tm = 256
    gp = 4 if bp % 4 == 0 else 1
    gs = 8 if bs % 8 == 0 else 1
    assert tp % lp == 0 and ts % ls == 0 and tp % tm == 0 and n_s % tm == 0 and ls % SUBLANE == 0

    xp = x_prompt.reshape(n_p, D_MODEL)
    xs = x_sample.reshape(n_s, D_MODEL)

    w_gdn, w_rest, ws_t = _rearranged_in_weights(w_in[0])
    pcol, prow = _gate_params(gdn_a_log[0], gdn_dt_bias[0], mlstm_i_bias[0], mlstm_f_bias[0])
    cw = jnp.zeros((SUBLANE, CONV_CH), F32).at[:CONV_W].set(conv_w[0].astype(F32))
    gdn_p, gdn_s, ml_p, ml_s, gate_p, gate_s, gatet_p, gatet_s, conv_p = _inproj(
        xp, xs, norm_attn_g[0].reshape(1, D_MODEL), w_gdn, w_rest, ws_t, pcol, prow, cw, tm, bp)
    gt_p = gatet_p.reshape(N_GATE, bp, tp // lp, lp).transpose(1, 2, 0, 3)
    gt_s = gatet_s.reshape(N_GATE, bs, ts // ls, ls).transpose(1, 2, 0, 3)

    ng_a = gdn_norm_g[0].reshape(1, DV_A).astype(F32)
    ng_b = mlstm_norm_g[0].reshape(H_B, DV_B).astype(F32)
    ma_p, gdn_st_p, mb_p, c_p, nn_p, m_p = _mixers(
        gdn_p.reshape(bp, tp, GDN_W), ml_p.reshape(bp, tp, MLP_W), gate_p.reshape(bp, tp, LANE), gt_p, ng_a, ng_b,
        L=lp, G=gp)
    ma_s, gdn_st_s, mb_s, c_s, nn_s, m_s, conv_s = _mixers(
        gdn_s.reshape(bs, ts, GDN_W), ml_s.reshape(bs, ts, MLP_W), gate_s.reshape(bs, ts, LANE), gt_s, ng_a, ng_b,
        L=ls, G=gs, cw=cw,
        state=(state_conv[0], state_gdn[0], state_mlstm_c[0], state_mlstm_n[0], state_mlstm_m[0].reshape(bs, 1, H_B)))
    half = H_A * DV_A

    rw = jnp.zeros((D_MODEL, LANE), F32).at[:, :N_EXPERTS].set(router_w[0])
    rw_hi = rw.astype(BF16)
    rw = jnp.stack([rw_hi, (rw - rw_hi.astype(F32)).astype(BF16)])
    rb = jnp.full((1, LANE), NEG, F32).at[0, :N_EXPERTS].set(router_b[0])
    x1, x_sorted, info, seg_len = _outproj(xp, xs, ma_p.reshape(n_p, half), ma_s.reshape(n_s, half),
                                           mb_p.reshape(n_p, half), mb_s.reshape(n_s, half),
                                           w_out[0].astype(BF16), norm_moe_g[0].reshape(1, D_MODEL), rw, rb, MOE_TM)

    nt = n // MOE_TM
    nb = -(-(n * TOP_K + nt * N_EXPERTS * (SEG_ALIGN - 1)) // MOE_BLK) + N_EXPERTS
    tables = _block_tables(seg_len[:, :, 0].astype(jnp.int32), nb)
    y_sorted = _experts(tables, x_sorted, expert_w_gu[0], expert_b_gu[0].reshape(N_EXPERTS, 1, 2 * D_FF),
                        expert_w_down[0], expert_b_down[0].reshape(N_EXPERTS, 1, D_MODEL), nt)
    y_p, y_s = _combine(y_sorted, info, x1, p_prompt[0].reshape(n_p, PLE_DIM),
                        p_sample[0].reshape(n_s, PLE_DIM), norm_ple_g[0].reshape(1, D_MODEL),
                        ple_gate_w[0].astype(BF16), ple_w[0].astype(BF16), final_norm_g.reshape(1, D_MODEL), MOE_TM)

    return (y_p.reshape(bp, tp, D_MODEL), y_s.reshape(bs, ts, D_MODEL),
            conv_p[None], gdn_st_p[None], c_p[None], nn_p[None], m_p.reshape(1, bp, H_B),
            conv_s[None], gdn_st_s[None], c_s[None], nn_s[None], m_s.reshape(1, bs, H_B))
```

```python
import functools

import numpy as np
import jax
import jax.numpy as jnp
from jax import lax
from jax.experimental import pallas as pl
from jax.experimental.pallas import tpu as pltpu

F32 = jnp.float32
BF16 = jnp.bfloat16

D_MODEL = 1024
H_A, DK_A, DV_A = 4, 128, 128
H_B, DK_B, DV_B = 4, 64, 128
CONV_W = 4
CONV_CH = H_A * (2 * DK_A + DV_A)
N_EXPERTS = 32
TOP_K = 4
D_FF = 1024
SWIGLU_LIMIT = 7.0
SWIGLU_ALPHA = 1.702
PLE_DIM = 256
EPS = 1e-6
NEG = -1e30
CHUNK = 64

LANE = 128
SUBLANE = 8
GDN_W = CONV_CH + H_A * DV_A
MLP_W = 2 * H_B * LANE + 2 * H_B * DV_B
N_GATE = 16
PROJ_CHUNK = 512

VMEM_LIMIT = 48 * 1024 * 1024

MOE_TM = 256
MOE_BLK = 512
MOE_DTYPE = BF16
SEG_ALIGN = 2 * SUBLANE
MOE_CAP = -(-(MOE_TM * TOP_K + N_EXPERTS * (SEG_ALIGN - 1)) // LANE) * LANE

HI = lax.Precision.HIGHEST

_NN = (((1,), (0,)), ((), ()))
_NT = (((1,), (1,)), ((), ()))
_TN = (((0,), (0,)), ((), ()))


def _dot(a, b, dims=_NN):
    return lax.dot_general(a.astype(BF16), b.astype(BF16), dims, preferred_element_type=F32)


def _dot_hi(a, b, dims=_NN):
    return lax.dot_general(a, b, dims, precision=HI, preferred_element_type=F32)


def _rms(x, g):
    return x * lax.rsqrt(jnp.mean(x * x, axis=-1, keepdims=True) + EPS) * g


def _softplus(t):
    return jnp.maximum(t, 0.0) + jnp.log1p(jnp.exp(-jnp.abs(t)))


def _sigmoid(t):
    return 1.0 / (1.0 + jnp.exp(-t))


def _silu(t):
    return t * _sigmoid(t)


def _activate_gates(raw, idx, alog, bias):
    t = raw + bias
    g = -jnp.exp(alog) * _softplus(t)
    beta = _sigmoid(t)
    lf = -_softplus(-t)
    return jnp.where(idx < 4, g, jnp.where(idx < 8, beta, jnp.where(idx < 12, t, lf)))


def _two_segment_specs(tm, width, n_p_tiles):
    return [pl.BlockSpec((tm, width), lambda i: (jnp.minimum(i, n_p_tiles - 1), 0)),
            pl.BlockSpec((tm, width), lambda i: (jnp.maximum(i - n_p_tiles, 0), 0))]


def _for_segment(n_p_tiles, body):
    i = pl.program_id(0)

    @pl.when(i < n_p_tiles)
    def _():
        body(0)

    @pl.when(i >= n_p_tiles)
    def _():
        body(1)


def _gdn_preactivate_stages(raw_ref, xc_ref, cw_ref, out_ref, cnew_ref, first_of_seq):
    tm = raw_ref.shape[0]
    xc_ref[0:SUBLANE, :] = jnp.where(first_of_seq, 0.0, xc_ref[tm:tm + SUBLANE, :])
    xc_ref[SUBLANE:SUBLANE + tm, :] = raw_ref[:, :CONV_CH]
    out_ref[:, CONV_CH:] = _silu(raw_ref[:, CONV_CH:])
    cnew_ref[0] = xc_ref[SUBLANE + tm - (CONV_W - 1):SUBLANE + tm, :]
    yield
    base = SUBLANE - (CONV_W - 1)
    for c0 in range(0, CONV_CH, DK_A):
        conv = xc_ref[base:base + tm, c0:c0 + DK_A] * cw_ref[0:1, c0:c0 + DK_A]
        for j in range(1, CONV_W):
            conv = conv + xc_ref[base + j:base + j + tm, c0:c0 + DK_A] * cw_ref[j:j + 1, c0:c0 + DK_A]
        act = _silu(conv)
        if c0 < H_A * DK_A:
            act = act * (lax.rsqrt(jnp.sum(act * act, axis=-1, keepdims=True) + EPS) * (DK_A ** -0.5))
        elif c0 < 2 * H_A * DK_A:
            act = act * lax.rsqrt(jnp.sum(act * act, axis=-1, keepdims=True) + EPS)
        out_ref[:, c0:c0 + DK_A] = act
        yield


def _inproj_kernel(xp_ref, xs_ref, g_ref, wa_ref, wb_ref, wst_ref, pc_ref, pr_ref, cw_ref,
                   gdnp_ref, gdns_ref, mlp_ref, mls_ref, gatep_ref, gates_ref, gatetp_ref, gatets_ref, cnew_ref,
                   xc_ref, raw_ref, *, n_p_tiles, tiles_per_seq):
    tm = xp_ref.shape[0]
    i = pl.program_id(0)

    @pl.when(i == 0)
    def _():
        xc_ref[...] = jnp.zeros_like(xc_ref)
        raw_ref[...] = jnp.zeros_like(raw_ref)

    def preactivate_previous_tile():
        return _gdn_preactivate_stages(raw_ref, xc_ref, cw_ref, gdnp_ref, cnew_ref, (i - 1) % tiles_per_seq == 0)

    def projection_stages(seg):
        x_ref = (xp_ref, xs_ref)[seg]
        gdn_dst, ml_ref = (raw_ref, gdns_ref)[seg], (mlp_ref, mls_ref)[seg]
        gate_ref, gatet_ref = (gatep_ref, gates_ref)[seg], (gatetp_ref, gatets_ref)[seg]
        hn = _rms(x_ref[...], g_ref[...]).astype(BF16)
        yield
        for c0 in range(0, GDN_W, PROJ_CHUNK):
            gdn_dst[:, c0:c0 + PROJ_CHUNK] = jnp.dot(hn, wa_ref[:, c0:c0 + PROJ_CHUNK], preferred_element_type=F32)
            yield
        for c0 in range(0, MLP_W, PROJ_CHUNK):
            ml_ref[:, c0:c0 + PROJ_CHUNK] = jnp.dot(hn, wb_ref[:, c0:c0 + PROJ_CHUNK], preferred_element_type=F32)
            yield
        raw = jnp.dot(hn, wb_ref[:, MLP_W:], preferred_element_type=F32)
        lane = lax.broadcasted_iota(jnp.int32, (tm, LANE), 1)
        gate_ref[...] = _activate_gates(raw, lane, pc_ref[0:1, :], pc_ref[1:2, :])
        raw_t = lax.dot_general(wst_ref[...], hn, _NT, preferred_element_type=F32)
        row = lax.broadcasted_iota(jnp.int32, (N_GATE, tm), 0)
        gatet_ref[...] = _activate_gates(raw_t, row, pr_ref[:, 0:1], pr_ref[:, 1:2])

    def body(seg):
        if seg == 0:
            _run_interleaved(preactivate_previous_tile(), projection_stages(0))
        else:
            pl.when(i == n_p_tiles)(lambda: _run_interleaved(preactivate_previous_tile()))
            _run_interleaved(projection_stages(1))

    _for_segment(n_p_tiles, body)


def _inproj(xp, xs, g, w_gdn, w_rest, ws_t, pcol, prow, cw, tm, n_seq_p):
    n_p, n_s = xp.shape[0], xs.shape[0]
    npt = n_p // tm
    tiles_per_seq = npt // n_seq_p

    def out2(width):
        return _two_segment_specs(tm, width, npt)

    def shp2(width):
        return [jax.ShapeDtypeStruct((n_p, width), F32), jax.ShapeDtypeStruct((n_s, width), F32)]

    def prev_tile(i):
        return jnp.clip(i - 1, 0, npt - 1)

    return pl.pallas_call(
        functools.partial(_inproj_kernel, n_p_tiles=npt, tiles_per_seq=tiles_per_seq),
        grid=((n_p + n_s) // tm,),
        in_specs=_two_segment_specs(tm, D_MODEL, npt) + [
            pl.BlockSpec((1, D_MODEL), lambda i: (0, 0)),
            pl.BlockSpec((D_MODEL, GDN_W), lambda i: (0, 0)),
            pl.BlockSpec((D_MODEL, MLP_W + LANE), lambda i: (0, 0)),
            pl.BlockSpec((N_GATE, D_MODEL), lambda i: (0, 0)),
            pl.BlockSpec((SUBLANE, LANE), lambda i: (0, 0)),
            pl.BlockSpec((N_GATE, LANE), lambda i: (0, 0)),
            pl.BlockSpec((SUBLANE, CONV_CH), lambda i: (0, 0)),
        ],
        out_specs=[
            pl.BlockSpec((tm, GDN_W), lambda i: (prev_tile(i), 0)),
            pl.BlockSpec((tm, GDN_W), lambda i: (jnp.maximum(i - npt, 0), 0)),
        ] + out2(MLP_W) + out2(LANE) + [
            pl.BlockSpec((N_GATE, tm), lambda i: (0, jnp.minimum(i, npt - 1))),
            pl.BlockSpec((N_GATE, tm), lambda i: (0, jnp.maximum(i - npt, 0))),
            pl.BlockSpec((1, CONV_W - 1, CONV_CH), lambda i: (prev_tile(i) // tiles_per_seq, 0, 0)),
        ],
        out_shape=shp2(GDN_W) + shp2(MLP_W) + shp2(LANE) + [
            jax.ShapeDtypeStruct((N_GATE, n_p), F32), jax.ShapeDtypeStruct((N_GATE, n_s), F32),
            jax.ShapeDtypeStruct((n_seq_p, CONV_W - 1, CONV_CH), F32)],
        scratch_shapes=[pltpu.VMEM((tm + SUBLANE, CONV_CH), F32), pltpu.VMEM((tm, GDN_W), F32)],
        compiler_params=pltpu.CompilerParams(dimension_semantics=("arbitrary",), vmem_limit_bytes=VMEM_LIMIT),
        name="inproj",
    )(xp, xs, g, w_gdn, w_rest, ws_t, pcol, prow, cw)


def _chunk_masks(L):
    ri = lax.broadcasted_iota(jnp.int32, (L, L), 0)
    ci = lax.broadcasted_iota(jnp.int32, (L, L), 1)
    return ri >= ci, ri > ci, ri <= ci


def _run_interleaved(*stage_generators):
    live = list(stage_generators)
    while live:
        for gen in list(live):
            if next(gen, StopIteration) is StopIteration:
                live.remove(gen)


def _gdn_stages(*refs, L, G, has_state):
    if has_state:
        (xin_ref, gate_ref, gatet_ref, cw_ref, ng_ref, cst_ref, s0_ref,
         mix_ref, cnew_ref, snew_ref, xc_ref, s_ref) = refs
    else:
        xin_ref, gate_ref, gatet_ref, ng_ref, mix_ref, snew_ref, s_ref = refs
    c = pl.program_id(1)

    @pl.when(c == 0)
    def _():
        if has_state:
            xc_ref[:, 0:SUBLANE, :] = jnp.zeros((G, SUBLANE, CONV_CH), F32)
            xc_ref[:, SUBLANE - (CONV_W - 1):SUBLANE, :] = cst_ref[...]
            s_ref[...] = s0_ref[...]
        else:
            s_ref[...] = jnp.zeros_like(s_ref)

    if has_state:
        @pl.when(c > 0)
        def _():
            xc_ref[:, 0:SUBLANE, :] = xc_ref[:, L:L + SUBLANE, :]

    yield
    tril, strict, triu = _chunk_masks(L)
    tril_f, triu_f = tril.astype(F32), triu.astype(F32)
    base = SUBLANE - (CONV_W - 1)

    chains = [(g, h) for g in range(G) for h in range(H_A)]
    s_old = [s_ref[g, h] for g, h in chains]
    if has_state:
        for g in range(G):
            xc_ref[g, SUBLANE:SUBLANE + L, :] = xin_ref[g, :, :CONV_CH]

    q, k, v, beta, gc, gl, decay = [], [], [], [], [], [], []
    for g in range(G):
        if has_state:
            conv = xc_ref[g, base:base + L, :] * cw_ref[0:1, :]
            for j in range(1, CONV_W):
                conv = conv + xc_ref[g, base + j:base + j + L, :] * cw_ref[j:j + 1, :]
            cnew_ref[g] = xc_ref[g, SUBLANE + L - (CONV_W - 1):SUBLANE + L, :]
            act = _silu(conv)
        else:
            act = xin_ref[g, :, :CONV_CH]
        gact = gate_ref[g]
        cum_c = _dot_hi(tril_f, gact)
        cum_r = _dot_hi(gatet_ref[g, 0], triu_f)
        for h in range(H_A):
            q.append(act[:, h * DK_A:(h + 1) * DK_A])
            k.append(act[:, H_A * DK_A + h * DK_A:H_A * DK_A + (h + 1) * DK_A])
            v.append(act[:, 2 * H_A * DK_A + h * DV_A:2 * H_A * DK_A + (h + 1) * DV_A])
            beta.append(gact[:, 4 + h:5 + h])
            gc.append(cum_c[:, h:h + 1])
            gl.append(cum_c[L - 1:L, h:h + 1])
            gr = cum_r[h:h + 1, :]
            decay.append(jnp.where(tril, jnp.exp(jnp.where(tril, cum_c[:, h:h + 1] - gr, 0.0)), 0.0))
        yield

    nc = range(len(chains))
    if has_state:
        qss = [jnp.sum(q[i] * q[i], axis=-1, keepdims=True) for i in nc]
        kss = [jnp.sum(k[i] * k[i], axis=-1, keepdims=True) for i in nc]
        q = [q[i] * (lax.rsqrt(qss[i] + EPS) * (DK_A ** -0.5)) for i in nc]
        k = [k[i] * lax.rsqrt(kss[i] + EPS) for i in nc]
    kb = [k[i] * beta[i] for i in nc]
    egc = [jnp.exp(gc[i]) for i in nc]
    yield
    kk = [_dot(kb[i], k[i], _NT) for i in nc]
    yield
    qk = [_dot(q[i], k[i], _NT) for i in nc]
    yield
    eye = (lax.broadcasted_iota(jnp.int32, (L, L), 0) == lax.broadcasted_iota(jnp.int32, (L, L), 1)).astype(F32)
    pw = [-jnp.where(strict, kk[i] * decay[i], 0.0) for i in nc]
    t_inv = [eye + pw[i] for i in nc]
    span = 2
    while span < L:
        yield
        pw = [_dot(pw[i], pw[i]) for i in nc]
        yield
        t_inv = [t_inv[i] + _dot(t_inv[i], pw[i]) for i in nc]
        span *= 2
    yield
    sol = [_dot(t_inv[i], jnp.concatenate([v[i] * beta[i], kb[i] * egc[i]], axis=-1)) for i in nc]
    yield
    qs = [_dot(q[i] * egc[i], s_old[i]) for i in nc]
    yield
    ws = [_dot(sol[i][:, DV_A:], s_old[i]) for i in nc]
    v_new = [sol[i][:, :DV_A] - ws[i] for i in nc]
    yield
    o = [qs[i] + _dot(jnp.where(tril, qk[i] * decay[i], 0.0), v_new[i]) for i in nc]
    yield
    s_new = [s_old[i] * jnp.exp(gl[i]) + _dot(k[i] * jnp.exp(gl[i] - gc[i]), v_new[i], _TN) for i in nc]
    yield
    ms = [jnp.mean(o[i] * o[i], axis=-1, keepdims=True) for i in nc]
    on = [o[i] * lax.rsqrt(ms[i] + EPS) for i in nc]
    yield
    for i, (g, h) in enumerate(chains):
        z = xin_ref[g, :, CONV_CH + h * DV_A:CONV_CH + (h + 1) * DV_A]
        out = on[i] * ng_ref[...] * (_silu(z) if has_state else z)
        mix_ref[g, :, h * DV_A:(h + 1) * DV_A] = out.astype(mix_ref.dtype)
    yield
    for i, (g, h) in enumerate(chains):
        s_ref[g, h] = s_new[i]
        snew_ref[g, h] = s_new[i]


def _mlstm_stages(*refs, L, G, has_state):
    if has_state:
        (xin_ref, gate_ref, gatet_ref, ng_ref, c0_ref, n0_ref, m0_ref,
         mix_ref, cnew_ref, nnew_ref, mnew_ref, c_ref, n_ref, m_ref) = refs
    else:
        (xin_ref, gate_ref, gatet_ref, ng_ref,
         mix_ref, cnew_ref, nnew_ref, mnew_ref, c_ref, n_ref, m_ref) = refs
    c = pl.program_id(1)

    @pl.when(c == 0)
    def _():
        c_ref[...] = jnp.zeros_like(c_ref)
        n_ref[...] = jnp.zeros_like(n_ref)
        m_ref[...] = jnp.zeros_like(m_ref)
        if has_state:
            c_ref[:, :, 0:DK_B, :] = c0_ref[...]
            n_ref[:, 0:H_B, 0:DK_B] = n0_ref[...]
            m_ref[:, 0:1, 0:H_B] = m0_ref[...]

    yield
    tril, _, triu = _chunk_masks(L)
    tril_f, triu_f = tril.astype(F32), triu.astype(F32)

    chains = [(g, h) for g in range(G) for h in range(H_B)]
    nc = range(len(chains))
    c_old = [c_ref[g, h] for g, h in chains]
    n_old = [n_ref[g, h:h + 1, :] for g, h in chains]
    m_old = [m_ref[g, 0:1, h:h + 1] for g, h in chains]

    v0 = 2 * H_B * LANE
    q = [xin_ref[g, :, h * LANE:(h + 1) * LANE] * (DK_B ** -0.5) for g, h in chains]
    k = [xin_ref[g, :, (H_B + h) * LANE:(H_B + h + 1) * LANE] for g, h in chains]
    v = [xin_ref[g, :, v0 + h * DV_B:v0 + (h + 1) * DV_B] for g, h in chains]
    ig_c, b_c, b_last, d_log = [], [], [], []
    for g in range(G):
        gact = gate_ref[g]
        gact_t = gatet_ref[g, 0]
        cum_c = _dot_hi(tril_f, gact)
        cum_r = _dot_hi(gact_t, triu_f)
        for h in range(H_B):
            ig_c.append(gact[:, 8 + h:9 + h])
            b_c.append(cum_c[:, 12 + h:13 + h])
            b_last.append(cum_c[L - 1:L, 12 + h:13 + h])
            d_log.append(jnp.where(tril, cum_c[:, 12 + h:13 + h] - cum_r[12 + h:13 + h, :]
                                   + gact_t[8 + h:9 + h, :], NEG))
        yield
    qk = [_dot(q[i], k[i], _NT) for i in nc]
    yield
    qc = [_dot(q[i], c_old[i]) for i in nc]
    yield
    inter = [b_c[i] + m_old[i] for i in nc]
    m_t = [jnp.maximum(inter[i], jnp.max(d_log[i], axis=-1, keepdims=True)) for i in nc]
    yield
    s = [qk[i] * jnp.exp(d_log[i] - m_t[i]) for i in nc]
    e_inter = [jnp.exp(inter[i] - m_t[i]) for i in nc]
    yield
    sv = [_dot(s[i], v[i]) for i in nc]
    yield
    m_new = [m_t[i][L - 1:L, :] for i in nc]
    kw = [k[i] * jnp.exp(b_last[i] - b_c[i] + ig_c[i] - m_new[i]) for i in nc]
    f_tot = [jnp.exp(b_last[i] + m_old[i] - m_new[i]) for i in nc]
    yield
    c_new = [f_tot[i] * c_old[i] + _dot(kw[i], v[i], _TN) for i in nc]
    yield
    n_new = [f_tot[i] * n_old[i] + jnp.sum(kw[i], axis=0, keepdims=True) for i in nc]
    qn = [jnp.sum(q[i] * n_old[i], axis=-1, keepdims=True) for i in nc]
    yield
    ssum = [jnp.sum(s[i], axis=-1, keepdims=True) for i in nc]
    yield
    den = [jnp.maximum(jnp.abs(e_inter[i] * qn[i] + ssum[i]), jnp.exp(-m_t[i])) for i in nc]
    hh = [(e_inter[i] * qc[i] + sv[i]) / den[i] for i in nc]
    yield
    ms = [jnp.mean(hh[i] * hh[i], axis=-1, keepdims=True) for i in nc]
    hn = [hh[i] * lax.rsqrt(ms[i] + EPS) for i in nc]
    yield
    for i, (g, h) in enumerate(chains):
        og = xin_ref[g, :, v0 + H_B * DV_B + h * DV_B:v0 + H_B * DV_B + (h + 1) * DV_B]
        mix_ref[g, :, h * DV_B:(h + 1) * DV_B] = (hn[i] * ng_ref[h:h + 1, :] * _sigmoid(og)).astype(mix_ref.dtype)
    yield
    for i, (g, h) in enumerate(chains):
        c_ref[g, h] = c_new[i]
        n_ref[g, h:h + 1, :] = n_new[i]
        m_ref[g, 0:1, h:h + 1] = m_new[i]
        cnew_ref[g, h] = c_new[i][0:DK_B, :]
        nnew_ref[g, h:h + 1, :] = n_new[i][:, 0:DK_B]
        mnew_ref[g, 0:1, h:h + 1] = m_new[i]


def _mixers_kernel(*refs, L, G, has_state):
    if has_state:
        (gdn_ref, gate_ref, gatet_ref, nga_ref, ml_ref, ngb_ref, cw_ref, cst_ref, s0_ref, c0_ref, n0_ref, m0_ref,
         mixa_ref, snew_ref, mixb_ref, cnew_ref, nnew_ref, mnew_ref, convnew_ref,
         s_ref, c_ref, n_ref, m_ref, xc_ref) = refs
        gdn_refs = (gdn_ref, gate_ref, gatet_ref, cw_ref, nga_ref, cst_ref, s0_ref,
                    mixa_ref, convnew_ref, snew_ref, xc_ref, s_ref)
        ml_refs = (ml_ref, gate_ref, gatet_ref, ngb_ref, c0_ref, n0_ref, m0_ref,
                   mixb_ref, cnew_ref, nnew_ref, mnew_ref, c_ref, n_ref, m_ref)
    else:
        (gdn_ref, gate_ref, gatet_ref, nga_ref, ml_ref, ngb_ref,
         mixa_ref, snew_ref, mixb_ref, cnew_ref, nnew_ref, mnew_ref, s_ref, c_ref, n_ref, m_ref) = refs
        gdn_refs = (gdn_ref, gate_ref, gatet_ref, nga_ref, mixa_ref, snew_ref, s_ref)
        ml_refs = (ml_ref, gate_ref, gatet_ref, ngb_ref, mixb_ref, cnew_ref, nnew_ref, mnew_ref, c_ref, n_ref, m_ref)
    _run_interleaved(_gdn_stages(*gdn_refs, L=L, G=G, has_state=has_state),
                     _mlstm_stages(*ml_refs, L=L, G=G, has_state=has_state))


def _mixers(gdn_in, ml_in, gates, gates_t, ng_a, ng_b, *, L, G, cw=None, state=None):
    n_seq, T, _ = gdn_in.shape
    n_c = T // L
    has_state = state is not None

    def seq_blk(*tail):
        return pl.BlockSpec((G,) + tail, lambda b, c: (b,) + (0,) * len(tail))

    def tok_blk(width):
        return pl.BlockSpec((G, L, width), lambda b, c: (b, c, 0))

    def seq_shape(*tail):
        return jax.ShapeDtypeStruct((n_seq,) + tail, F32)

    state_specs = [seq_blk(H_A, DK_A, DV_A), seq_blk(H_B, DK_B, DV_B), seq_blk(H_B, DK_B), seq_blk(1, H_B)]
    state_shapes = [seq_shape(H_A, DK_A, DV_A), seq_shape(H_B, DK_B, DV_B), seq_shape(H_B, DK_B), seq_shape(1, H_B)]
    conv_spec, conv_shape = seq_blk(CONV_W - 1, CONV_CH), seq_shape(CONV_W - 1, CONV_CH)
    in_specs = [
        tok_blk(GDN_W), tok_blk(LANE),
        pl.BlockSpec((G, 1, N_GATE, L), lambda b, c: (b, c, 0, 0)),
        pl.BlockSpec((1, DV_A), lambda b, c: (0, 0)),
        tok_blk(MLP_W),
        pl.BlockSpec((H_B, DV_B), lambda b, c: (0, 0)),
    ]
    args = [gdn_in, gates, gates_t, ng_a, ml_in, ng_b]
    out_specs = [tok_blk(H_A * DV_A), state_specs[0], tok_blk(H_B * DV_B)] + state_specs[1:]
    mix_dtype = BF16 if L % (2 * SUBLANE) == 0 else F32
    mix_a = jax.ShapeDtypeStruct((n_seq, T, H_A * DV_A), mix_dtype)
    mix_b = jax.ShapeDtypeStruct((n_seq, T, H_B * DV_B), mix_dtype)
    out_shape = [mix_a, state_shapes[0], mix_b] + state_shapes[1:]
    scratch = [pltpu.VMEM((G, H_A, DK_A, DV_A), F32), pltpu.VMEM((G, H_B, LANE, DV_B), F32),
               pltpu.VMEM((G, SUBLANE, LANE), F32), pltpu.VMEM((G, SUBLANE, LANE), F32)]
    if has_state:
        in_specs += [pl.BlockSpec((SUBLANE, CONV_CH), lambda b, c: (0, 0)), conv_spec] + state_specs
        args += [cw] + list(state)
        out_specs.append(conv_spec)
        out_shape.append(conv_shape)
        scratch.append(pltpu.VMEM((G, L + SUBLANE, CONV_CH), F32))
    return pl.pallas_call(
        functools.partial(_mixers_kernel, L=L, G=G, has_state=has_state),
        grid=(n_seq // G, n_c),
        in_specs=in_specs,
        out_specs=out_specs,
        out_shape=out_shape,
        scratch_shapes=scratch,
        compiler_params=pltpu.CompilerParams(dimension_semantics=("parallel", "arbitrary"),
                                             vmem_limit_bytes=VMEM_LIMIT),
        name=f"mixers_L{L}",
    )(*args)


def _outproj_kernel(xp_ref, xs_ref, map_ref, mas_ref, mbp_ref, mbs_ref, wo_ref, g_ref, rw_ref, rb_ref,
                    x1_ref, xsort_ref, info_ref, cpad_ref, *, n_p_tiles):
    half = H_A * DV_A
    tm = xp_ref.shape[0]

    def body(seg):
        x_ref, ma_ref, mb_ref = (xp_ref, xs_ref)[seg], (map_ref, mas_ref)[seg], (mbp_ref, mbs_ref)[seg]
        x1 = (x_ref[...] + jnp.dot(ma_ref[...].astype(BF16), wo_ref[:half, :], preferred_element_type=F32)
              + jnp.dot(mb_ref[...].astype(BF16), wo_ref[half:, :], preferred_element_type=F32))
        x1_ref[...] = x1
        hn = _rms(x1, g_ref[...])
        hn_hi = hn.astype(BF16)
        hn_lo = (hn - hn_hi.astype(F32)).astype(BF16)
        logits = (jnp.dot(hn_hi, rw_ref[0], preferred_element_type=F32)
                  + jnp.dot(hn_hi, rw_ref[1], preferred_element_type=F32)
                  + jnp.dot(hn_lo, rw_ref[0], preferred_element_type=F32)) + rb_ref[...]

        vals = logits.T[:N_EXPERTS, :]
        e_iota = lax.broadcasted_iota(jnp.int32, (N_EXPERTS, tm), 0)
        sels, tops = [], []
        for _ in range(TOP_K):
            m = jnp.max(vals, axis=0, keepdims=True)
            first = jnp.min(jnp.where(vals == m, e_iota, N_EXPERTS), axis=0, keepdims=True)
            sel = e_iota == first
            vals = jnp.where(sel, -jnp.inf, vals)
            sels.append(sel)
            tops.append(m)
        ex = [jnp.exp(t - tops[0]) for t in tops]
        den = ex[0] + ex[1] + ex[2] + ex[3]
        gates = [e / den for e in ex]
        mask = sels[0].astype(F32) + sels[1].astype(F32) + sels[2].astype(F32) + sels[3].astype(F32)
        ri = lax.broadcasted_iota(jnp.int32, (tm, tm), 0)
        ci = lax.broadcasted_iota(jnp.int32, (tm, tm), 1)
        rank = _dot(mask, (ri < ci).astype(F32))
        cnt = jnp.sum(mask, axis=1, keepdims=True)
        cpad = jnp.ceil(cnt * (1.0 / SEG_ALIGN)) * SEG_ALIGN
        cpad_b = jnp.broadcast_to(cpad, (N_EXPERTS, tm))
        er = lax.broadcasted_iota(jnp.int32, (N_EXPERTS, N_EXPERTS), 0)
        ec = lax.broadcasted_iota(jnp.int32, (N_EXPERTS, N_EXPERTS), 1)
        seg_off = _dot((er > ec).astype(F32), cpad_b)
        pos = seg_off + rank
        q = [jnp.sum(jnp.where(s, pos, 0.0), axis=0, keepdims=True) for s in sels]

        j_iota = lax.broadcasted_iota(jnp.int32, (MOE_CAP, tm), 0).astype(F32)
        perm = jnp.zeros((MOE_CAP, tm), F32)
        for kk in range(TOP_K):
            perm = jnp.where(j_iota == q[kk], 1.0, perm)
        xsorted = _dot(perm, hn)
        xsort_ref[...] = xsorted.astype(MOE_DTYPE)

        r_iota = lax.broadcasted_iota(jnp.int32, (LANE, tm), 0)
        info = jnp.zeros((LANE, tm), F32)
        for kk in range(TOP_K):
            info = jnp.where(r_iota == kk, q[kk], info)
            info = jnp.where(r_iota == TOP_K + kk, gates[kk], info)
        info_ref[...] = info.T
        cpad_ref[0] = cpad_b[:, :LANE]

    _for_segment(n_p_tiles, body)


def _outproj(xp, xs, ma_p, ma_s, mb_p, mb_s, w_out, g, rw, rb, tm):
    n_p, n_s = xp.shape[0], xs.shape[0]
    n = n_p + n_s
    nt = n // tm
    npt = n_p // tm
    half = H_A * DV_A
    return pl.pallas_call(
        functools.partial(_outproj_kernel, n_p_tiles=npt),
        grid=(nt,),
        in_specs=_two_segment_specs(tm, D_MODEL, npt) + _two_segment_specs(tm, half, npt)
        + _two_segment_specs(tm, half, npt) + [
            pl.BlockSpec((D_MODEL, D_MODEL), lambda i: (0, 0)),
            pl.BlockSpec((1, D_MODEL), lambda i: (0, 0)),
            pl.BlockSpec((2, D_MODEL, LANE), lambda i: (0, 0, 0)),
            pl.BlockSpec((1, LANE), lambda i: (0, 0)),
        ],
        out_specs=[
            pl.BlockSpec((tm, D_MODEL), lambda i: (i, 0)),
            pl.BlockSpec((MOE_CAP, D_MODEL), lambda i: (i, 0)),
            pl.BlockSpec((tm, LANE), lambda i: (i, 0)),
            pl.BlockSpec((1, N_EXPERTS, LANE), lambda i: (i, 0, 0)),
        ],
        out_shape=[
            jax.ShapeDtypeStruct((n, D_MODEL), F32),
            jax.ShapeDtypeStruct((nt * MOE_CAP, D_MODEL), MOE_DTYPE),
            jax.ShapeDtypeStruct((n, LANE), F32),
            jax.ShapeDtypeStruct((nt, N_EXPERTS, LANE), F32),
        ],
        compiler_params=pltpu.CompilerParams(dimension_semantics=("arbitrary",), vmem_limit_bytes=VMEM_LIMIT),
        name="outproj",
    )(xp, xs, ma_p, ma_s, mb_p, mb_s, w_out, g, rw, rb)


def _expert_kernel(be_ref, bj_ref, tf_ref, tl_ref, cov_ref, nu_ref, vt_ref, ct_ref, lt_ref, nx_ref, ws_ref,
                   xs_hbm, wgu_hbm, bgu_ref, wd_hbm, bd_ref, ys_hbm,
                   xbuf, ybuf, gsem, ssem, wgu_st, wd_st, wsem, wgu_bf, wd_bf, *, nt):
    b = pl.program_id(0)
    n_used = nu_ref[0]
    slot = b % 2

    def start_pieces(bb, copy, s):
        e = be_ref[bb]
        base = bj_ref[bb] * MOE_BLK

        def body(t, carry):
            k = e * nt + t
            lo = jnp.maximum(vt_ref[k], base)
            ln = jnp.minimum(ct_ref[k], base + MOE_BLK) - lo

            @pl.when(ln > 0)
            def _():
                copy(s, pl.multiple_of(lt_ref[k] + lo, SEG_ALIGN), pl.multiple_of(lo - base, SEG_ALIGN),
                     pl.multiple_of(ln, SEG_ALIGN)).start()
            return carry

        lax.fori_loop(tf_ref[bb], tl_ref[bb] + 1, body, 0)

    def weight_copies(e):
        return (pltpu.make_async_copy(wgu_hbm.at[e], wgu_st, wsem.at[0]),
                pltpu.make_async_copy(wd_hbm.at[e], wd_st, wsem.at[1]))

    def cast_weights(p):
        wgu_bf[p] = wgu_st[...].astype(BF16)
        wd_bf[p] = wd_st[...].astype(BF16)

    def gather_copy(s, src, dst, size):
        return pltpu.make_async_copy(xs_hbm.at[pl.ds(src, size)], xbuf.at[s, pl.ds(dst, size)], gsem.at[s])

    def scatter_copy(s, src, dst, size):
        return pltpu.make_async_copy(ybuf.at[s, pl.ds(dst, size)], ys_hbm.at[pl.ds(src, size)], ssem.at[s])

    def wait_rows(count, copy, s):
        @pl.when(count > 0)
        def _():
            copy(s, 0, 0, pl.multiple_of(count, SEG_ALIGN)).wait()

    @pl.when(b == 0)
    def _():
        xbuf[...] = jnp.zeros_like(xbuf)
        start_pieces(0, gather_copy, 0)

    @pl.when(b + 1 < n_used)
    def _():
        start_pieces(b + 1, gather_copy, 1 - slot)

    @pl.when(b < n_used)
    def _():
        e = be_ref[b]
        first = jnp.logical_or(b == 0, be_ref[jnp.maximum(b - 1, 0)] != e)
        last = jnp.logical_or(b == n_used - 1, be_ref[jnp.minimum(b + 1, n_used - 1)] != e)
        has_next = nx_ref[b] < N_EXPERTS
        p = ws_ref[b]

        @pl.when(b == 0)
        def _():
            for cp in weight_copies(e):
                cp.start()
            for cp in weight_copies(e):
                cp.wait()
            cast_weights(p)

        @pl.when(jnp.logical_and(first, has_next))
        def _():
            for cp in weight_copies(nx_ref[b]):
                cp.start()

        wait_rows(cov_ref[b], gather_copy, slot)

        @pl.when(b >= 2)
        def _():
            wait_rows(cov_ref[jnp.maximum(b - 2, 0)], scatter_copy, slot)

        def expert_mlp(rows):
            hgu = jnp.dot(xbuf[slot, :rows].astype(BF16), wgu_bf[p], preferred_element_type=F32) + bgu_ref[0]
            gate = jnp.minimum(hgu[:, :D_FF], SWIGLU_LIMIT)
            up = jnp.clip(hgu[:, D_FF:], -SWIGLU_LIMIT, SWIGLU_LIMIT)
            act = (up + 1.0) * gate * _sigmoid(SWIGLU_ALPHA * gate)
            y = jnp.dot(act.astype(BF16), wd_bf[p], preferred_element_type=F32) + bd_ref[0]
            ybuf[slot, :rows] = y.astype(MOE_DTYPE)

        quarter = MOE_BLK // 4
        for nq in range(1, 5):
            pl.when(jnp.logical_and(cov_ref[b] > (nq - 1) * quarter, cov_ref[b] <= nq * quarter))(
                functools.partial(expert_mlp, nq * quarter))

        @pl.when(jnp.logical_and(last, has_next))
        def _():
            for cp in weight_copies(nx_ref[b]):
                cp.wait()
            cast_weights(1 - p)

        start_pieces(b, scatter_copy, slot)

        @pl.when(b == n_used - 1)
        def _():
            wait_rows(cov_ref[b], scatter_copy, slot)
            wait_rows(jnp.where(b >= 1, cov_ref[jnp.maximum(b - 1, 0)], 0), scatter_copy, 1 - slot)


def _experts(tables, xs, w_gu, b_gu, w_down, b_down, nt):
    nb = tables[0].shape[0]

    def bias_blk(b, *t):
        return (t[0][jnp.minimum(b, t[5][0] - 1)], 0, 0)

    grid_spec = pltpu.PrefetchScalarGridSpec(
        num_scalar_prefetch=len(tables),
        grid=(nb,),
        in_specs=[
            pl.BlockSpec(memory_space=pl.ANY),
            pl.BlockSpec(memory_space=pl.ANY),
            pl.BlockSpec((1, 1, 2 * D_FF), bias_blk),
            pl.BlockSpec(memory_space=pl.ANY),
            pl.BlockSpec((1, 1, D_MODEL), bias_blk),
        ],
        out_specs=pl.BlockSpec(memory_space=pl.ANY),
        scratch_shapes=[
            pltpu.VMEM((2, MOE_BLK, D_MODEL), MOE_DTYPE),
            pltpu.VMEM((2, MOE_BLK, D_MODEL), MOE_DTYPE),
            pltpu.SemaphoreType.DMA((2,)),
            pltpu.SemaphoreType.DMA((2,)),
            pltpu.VMEM((D_MODEL, 2 * D_FF), F32),
            pltpu.VMEM((D_FF, D_MODEL), F32),
            pltpu.SemaphoreType.DMA((2,)),
            pltpu.VMEM((2, D_MODEL, 2 * D_FF), BF16),
            pltpu.VMEM((2, D_FF, D_MODEL), BF16),
        ],
    )
    return pl.pallas_call(
        functools.partial(_expert_kernel, nt=nt),
        grid_spec=grid_spec,
        out_shape=jax.ShapeDtypeStruct(xs.shape, xs.dtype),
        input_output_aliases={len(tables): 0},
        compiler_params=pltpu.CompilerParams(dimension_semantics=("arbitrary",), vmem_limit_bytes=VMEM_LIMIT),
        name="experts",
    )(*tables, xs, w_gu, b_gu, w_down, b_down)


def _combine_kernel(ys_ref, info_ref, x1_ref, pp_ref, ps_ref, gple_ref, wg_ref, wp_ref, gfin_ref,
                    outp_ref, outs_ref, *, n_p_tiles):
    tm = x1_ref.shape[0]

    def body(seg):
        p_ref, out_ref = (pp_ref, ps_ref)[seg], (outp_ref, outs_ref)[seg]
        info = info_ref[...]
        j_iota = lax.broadcasted_iota(jnp.int32, (tm, MOE_CAP), 1).astype(F32)
        gmat = jnp.zeros((tm, MOE_CAP), F32)
        for kk in range(TOP_K):
            gmat = jnp.where(j_iota == info[:, kk:kk + 1], info[:, TOP_K + kk:TOP_K + kk + 1], gmat)
        x2 = x1_ref[...] + jnp.dot(gmat.astype(BF16), ys_ref[...].astype(BF16), preferred_element_type=F32)
        hn = _rms(x2, gple_ref[...]).astype(BF16)
        gate = _sigmoid(jnp.dot(hn, wg_ref[...], preferred_element_type=F32))
        pe = jnp.dot(p_ref[...].astype(BF16), wp_ref[...], preferred_element_type=F32)
        x3 = x2 + gate * pe
        out_ref[...] = _rms(x3, gfin_ref[...])

    _for_segment(n_p_tiles, body)


def _combine(ys, info, x1, pp, ps, g_ple, w_gate, w_p, g_fin, tm):
    n_p, n_s = pp.shape[0], ps.shape[0]
    n = n_p + n_s
    nt = n // tm
    npt = n_p // tm
    return pl.pallas_call(
        functools.partial(_combine_kernel, n_p_tiles=npt),
        grid=(nt,),
        in_specs=[
            pl.BlockSpec((MOE_CAP, D_MODEL), lambda i: (i, 0)),
            pl.BlockSpec((tm, LANE), lambda i: (i, 0)),
            pl.BlockSpec((tm, D_MODEL), lambda i: (i, 0)),
        ] + _two_segment_specs(tm, PLE_DIM, npt) + [
            pl.BlockSpec((1, D_MODEL), lambda i: (0, 0)),
            pl.BlockSpec((D_MODEL, D_MODEL), lambda i: (0, 0)),
            pl.BlockSpec((PLE_DIM, D_MODEL), lambda i: (0, 0)),
            pl.BlockSpec((1, D_MODEL), lambda i: (0, 0)),
        ],
        out_specs=_two_segment_specs(tm, D_MODEL, npt),
        out_shape=[jax.ShapeDtypeStruct((n_p, D_MODEL), F32), jax.ShapeDtypeStruct((n_s, D_MODEL), F32)],
        compiler_params=pltpu.CompilerParams(dimension_semantics=("arbitrary",), vmem_limit_bytes=VMEM_LIMIT),
        name="combine",
    )(ys, info, x1, pp, ps, g_ple, w_gate, w_p, g_fin)


def _block_tables(seg_len, nb):
    nt = seg_len.shape[0]
    seg_off = jnp.cumsum(seg_len, axis=1) - seg_len
    seg_end = jnp.cumsum(seg_len, axis=0).T
    seg_start = seg_end - seg_len.T
    n_rows = seg_end[:, -1]
    n_blk = (n_rows + MOE_BLK - 1) // MOE_BLK
    blk_end = jnp.cumsum(n_blk)
    b = jnp.arange(nb, dtype=jnp.int32)
    block_e = jnp.minimum(jnp.sum((blk_end[None, :] <= b[:, None]).astype(jnp.int32), axis=1), N_EXPERTS - 1)
    idx = jnp.where(n_blk > 0, jnp.arange(N_EXPERTS, dtype=jnp.int32), N_EXPERTS)
    nxt = jnp.concatenate([lax.cummin(idx, axis=0, reverse=True)[1:], jnp.full((1,), N_EXPERTS, jnp.int32)])
    parity = (jnp.cumsum((n_blk > 0).astype(jnp.int32)) - 1) % 2
    per_e = jnp.concatenate([jnp.stack([blk_end - n_blk, n_rows, nxt, parity], axis=1), seg_start, seg_end],
                            axis=1).astype(F32)
    onehot = (block_e[:, None] == jnp.arange(N_EXPERTS, dtype=jnp.int32)[None, :]).astype(F32)
    per_b = jnp.dot(onehot, per_e, precision=HI).astype(jnp.int32)
    block_j = b - per_b[:, 0]
    base = block_j * MOE_BLK
    t_first = jnp.sum((per_b[:, 4 + nt:] <= base[:, None]).astype(jnp.int32), axis=1)
    t_last = jnp.sum((per_b[:, 4:4 + nt] < (base + MOE_BLK)[:, None]).astype(jnp.int32), axis=1) - 1
    cover = jnp.clip(per_b[:, 1] - base, 0, MOE_BLK)
    seg_shift = (jnp.arange(nt, dtype=jnp.int32)[:, None] * MOE_CAP + seg_off).T - seg_start
    tables = (block_e, block_j, t_first, t_last, cover, blk_end[-1:], seg_start.reshape(-1),
              seg_end.reshape(-1), seg_shift.reshape(-1), per_b[:, 2], per_b[:, 3])
    return tuple(t.astype(jnp.int32) for t in tables)


def _rearranged_in_weights(w_in):
    o = np.cumsum([0, CONV_CH, H_A * DV_A, H_A, H_A, H_B * DK_B, H_B * DK_B, H_B * DV_B, H_B * DV_B, H_B, H_B])
    conv_in, z_a, a_a, b_a, q_b, k_b, v_b, o_b, i_b, f_b = (w_in[:, int(o[j]):int(o[j + 1])] for j in range(10))
    zpad = jnp.zeros((D_MODEL, LANE - DK_B), w_in.dtype)

    def pad_heads(w):
        return jnp.concatenate([jnp.concatenate([w[:, h * DK_B:(h + 1) * DK_B], zpad], axis=1) for h in range(H_B)],
                               axis=1)

    small = jnp.concatenate([a_a, b_a, i_b, f_b], axis=1)
    w_gdn = w_in[:, :GDN_W]
    w_rest = jnp.concatenate([pad_heads(q_b), pad_heads(k_b), v_b, o_b,
                              small, jnp.zeros((D_MODEL, LANE - N_GATE), w_in.dtype)], axis=1)
    return w_gdn.astype(BF16), w_rest.astype(BF16), small.T.astype(BF16)


def _gate_params(a_log, dt_bias, i_bias, f_bias):
    z4 = jnp.zeros((4,), F32)
    alog = jnp.concatenate([a_log.astype(F32), z4, z4, z4])
    bias = jnp.concatenate([dt_bias.astype(F32), z4, i_bias.astype(F32), f_bias.astype(F32)])
    pad = jnp.zeros((LANE - N_GATE,), F32)
    pcol = jnp.zeros((SUBLANE, LANE), F32).at[0].set(jnp.concatenate([alog, pad])).at[1].set(
        jnp.concatenate([bias, pad]))
    prow = jnp.zeros((N_GATE, LANE), F32).at[:, 0].set(alog).at[:, 1].set(bias)
    return pcol, prow


def kernel(x_prompt, x_sample, p_prompt, p_sample, state_conv, state_gdn, state_mlstm_c, state_mlstm_n, state_mlstm_m, norm_attn_g, w_in, conv_w, gdn_a_log, gdn_dt_bias, gdn_norm_g, mlstm_i_bias, mlstm_f_bias, mlstm_norm_g, w_out, norm_moe_g, router_w, router_b, expert_w_gu, expert_b_gu, expert_w_down, expert_b_down, norm_ple_g, ple_gate_w, ple_w, final_norm_g):
    bp, tp, _ = x_prompt.shape
    bs, ts, _ = x_sample.shape
    n_p, n_s = bp * tp, bs * ts
    n = n_p + n_s
    lp, ls = min(tp, CHUNK), min(ts, CHUNK)
    tm = 256
    gp = 4 if bp % 4 == 0 else 1
    gs = 8 if bs % 8 == 0 else 1
    assert tp % lp == 0 and ts % ls == 0 and tp % tm == 0 and n_s % tm == 0 and ls % SUBLANE == 0

    xp = x_prompt.reshape(n_p, D_MODEL)
    xs = x_sample.reshape(n_s, D_MODEL)

    w_gdn, w_rest, ws_t = _rearranged_in_weights(w_in[0])
    pcol, prow = _gate_params(gdn_a_log[0], gdn_dt_bias[0], mlstm_i_bias[0], mlstm_f_bias[0])
    cw = jnp.zeros((SUBLANE, CONV_CH), F32).at[:CONV_W].set(conv_w[0].astype(F32))
    gdn_p, gdn_s, ml_p, ml_s, gate_p, gate_s, gatet_p, gatet_s, conv_p = _inproj(
        xp, xs, norm_attn_g[0].reshape(1, D_MODEL), w_gdn, w_rest, ws_t, pcol, prow, cw, tm, bp)
    gt_p = gatet_p.reshape(N_GATE, bp, tp // lp, lp).transpose(1, 2, 0, 3)
    gt_s = gatet_s.reshape(N_GATE, bs, ts // ls, ls).transpose(1, 2, 0, 3)

    ng_a = gdn_norm_g[0].reshape(1, DV_A).astype(F32)
    ng_b = mlstm_norm_g[0].reshape(H_B, DV_B).astype(F32)
    ma_p, gdn_st_p, mb_p, c_p, nn_p, m_p = _mixers(
        gdn_p.reshape(bp, tp, GDN_W), ml_p.reshape(bp, tp, MLP_W), gate_p.reshape(bp, tp, LANE), gt_p, ng_a, ng_b,
        L=lp, G=gp)
    ma_s, gdn_st_s, mb_s, c_s, nn_s, m_s, conv_s = _mixers(
        gdn_s.reshape(bs, ts, GDN_W), ml_s.reshape(bs, ts, MLP_W), gate_s.reshape(bs, ts, LANE), gt_s, ng_a, ng_b,
        L=ls, G=gs, cw=cw,
        state=(state_conv[0], state_gdn[0], state_mlstm_c[0], state_mlstm_n[0], state_mlstm_m[0].reshape(bs, 1, H_B)))
    half = H_A * DV_A

    rw = jnp.zeros((D_MODEL, LANE), F32).at[:, :N_EXPERTS].set(router_w[0])
    rw_hi = rw.astype(BF16)
    rw = jnp.stack([rw_hi, (rw - rw_hi.astype(F32)).astype(BF16)])
    rb = jnp.full((1, LANE), NEG, F32).at[0, :N_EXPERTS].set(router_b[0])
    x1, x_sorted, info, seg_len = _outproj(xp, xs, ma_p.reshape(n_p, half), ma_s.reshape(n_s, half),
                                           mb_p.reshape(n_p, half), mb_s.reshape(n_s, half),
                                           w_out[0].astype(BF16), norm_moe_g[0].reshape(1, D_MODEL), rw, rb, MOE_TM)

    nt = n // MOE_TM
    nb = -(-(n * TOP_K + nt * N_EXPERTS * (SEG_ALIGN - 1)) // MOE_BLK) + N_EXPERTS
    tables = _block_tables(seg_len[:, :, 0].astype(jnp.int32), nb)
    y_sorted = _experts(tables, x_sorted, expert_w_gu[0], expert_b_gu[0].reshape(N_EXPERTS, 1, 2 * D_FF),
                        expert_w_down[0], expert_b_down[0].reshape(N_EXPERTS, 1, D_MODEL), nt)
    y_p, y_s = _combine(y_sorted, info, x1, p_prompt[0].reshape(n_p, PLE_DIM),
                        p_sample[0].reshape(n_s, PLE_DIM), norm_ple_g[0].reshape(1, D_MODEL),
                        ple_gate_w[0].astype(BF16), ple_w[0].astype(BF16), final_norm_g.reshape(1, D_MODEL), MOE_TM)

    return (y_p.reshape(bp, tp, D_MODEL), y_s.reshape(bs, ts, D_MODEL),
            conv_p[None], gdn_st_p[None], c_p[None], nn_p[None], m_p.reshape(1, bp, H_B),
            conv_s[None], gdn_st_s[None], c_s[None], nn_s[None], m_s.reshape(1, bs, H_B))
```

```python
import functools

import numpy as np
import jax
import jax.numpy as jnp
from jax import lax
from jax.experimental import pallas as pl
from jax.experimental.pallas import tpu as pltpu

F32 = jnp.float32
BF16 = jnp.bfloat16

D_MODEL = 1024
H_A, DK_A, DV_A = 4, 128, 128
H_B, DK_B, DV_B = 4, 64, 128
CONV_W = 4
CONV_CH = H_A * (2 * DK_A + DV_A)
N_EXPERTS = 32
TOP_K = 4
D_FF = 1024
SWIGLU_LIMIT = 7.0
SWIGLU_ALPHA = 1.702
PLE_DIM = 256
EPS = 1e-6
NEG = -1e30
CHUNK = 64

LANE = 128
SUBLANE = 8
GDN_W = CONV_CH + H_A * DV_A
MLP_W = 2 * H_B * LANE + 2 * H_B * DV_B
N_GATE = 16
PROJ_CHUNK = 512

VMEM_LIMIT = 48 * 1024 * 1024

MOE_TM = 256
MOE_BLK = 512
MOE_DTYPE = F32
SEG_ALIGN = SUBLANE
MOE_CAP = -(-(MOE_TM * TOP_K + N_EXPERTS * (SEG_ALIGN - 1)) // LANE) * LANE

HI = lax.Precision.HIGHEST

_NN = (((1,), (0,)), ((), ()))
_NT = (((1,), (1,)), ((), ()))
_TN = (((0,), (0,)), ((), ()))


def _dot(a, b, dims=_NN):
    return lax.dot_general(a.astype(BF16), b.astype(BF16), dims, preferred_element_type=F32)


def _dot_hi(a, b, dims=_NN):
    return lax.dot_general(a, b, dims, precision=HI, preferred_element_type=F32)


def _rms(x, g):
    return x * lax.rsqrt(jnp.mean(x * x, axis=-1, keepdims=True) + EPS) * g


def _softplus(t):
    return jnp.maximum(t, 0.0) + jnp.log1p(jnp.exp(-jnp.abs(t)))


def _sigmoid(t):
    return 1.0 / (1.0 + jnp.exp(-t))


def _silu(t):
    return t * _sigmoid(t)


def _activate_gates(raw, idx, alog, bias):
    t = raw + bias
    g = -jnp.exp(alog) * _softplus(t)
    beta = _sigmoid(t)
    lf = -_softplus(-t)
    return jnp.where(idx < 4, g, jnp.where(idx < 8, beta, jnp.where(idx < 12, t, lf)))


def _two_segment_specs(tm, width, n_p_tiles):
    return [pl.BlockSpec((tm, width), lambda i: (jnp.minimum(i, n_p_tiles - 1), 0)),
            pl.BlockSpec((tm, width), lambda i: (jnp.maximum(i - n_p_tiles, 0), 0))]


def _for_segment(n_p_tiles, body):
    i = pl.program_id(0)

    @pl.when(i < n_p_tiles)
    def _():
        body(0)

    @pl.when(i >= n_p_tiles)
    def _():
        body(1)


def _gdn_preactivate_stages(raw_ref, xc_ref, cw_ref, out_ref, cnew_ref, first_of_seq):
    tm = raw_ref.shape[0]
    xc_ref[0:SUBLANE, :] = jnp.where(first_of_seq, 0.0, xc_ref[tm:tm + SUBLANE, :])
    xc_ref[SUBLANE:SUBLANE + tm, :] = raw_ref[:, :CONV_CH]
    out_ref[:, CONV_CH:] = _silu(raw_ref[:, CONV_CH:])
    cnew_ref[0] = xc_ref[SUBLANE + tm - (CONV_W - 1):SUBLANE + tm, :]
    yield
    base = SUBLANE - (CONV_W - 1)
    for c0 in range(0, CONV_CH, DK_A):
        conv = xc_ref[base:base + tm, c0:c0 + DK_A] * cw_ref[0:1, c0:c0 + DK_A]
        for j in range(1, CONV_W):
            conv = conv + xc_ref[base + j:base + j + tm, c0:c0 + DK_A] * cw_ref[j:j + 1, c0:c0 + DK_A]
        act = _silu(conv)
        if c0 < H_A * DK_A:
            act = act * (lax.rsqrt(jnp.sum(act * act, axis=-1, keepdims=True) + EPS) * (DK_A ** -0.5))
        elif c0 < 2 * H_A * DK_A:
            act = act * lax.rsqrt(jnp.sum(act * act, axis=-1, keepdims=True) + EPS)
        out_ref[:, c0:c0 + DK_A] = act
        yield


def _inproj_kernel(xp_ref, xs_ref, g_ref, wa_ref, wb_ref, wst_ref, pc_ref, pr_ref, cw_ref,
                   gdnp_ref, gdns_ref, mlp_ref, mls_ref, gatep_ref, gates_ref, gatetp_ref, gatets_ref, cnew_ref,
                   xc_ref, raw_ref, *, n_p_tiles, tiles_per_seq):
    tm = xp_ref.shape[0]
    i = pl.program_id(0)

    @pl.when(i == 0)
    def _():
        xc_ref[...] = jnp.zeros_like(xc_ref)
        raw_ref[...] = jnp.zeros_like(raw_ref)

    def preactivate_previous_tile():
        return _gdn_preactivate_stages(raw_ref, xc_ref, cw_ref, gdnp_ref, cnew_ref, (i - 1) % tiles_per_seq == 0)

    def projection_stages(seg):
        x_ref = (xp_ref, xs_ref)[seg]
        gdn_dst, ml_ref = (raw_ref, gdns_ref)[seg], (mlp_ref, mls_ref)[seg]
        gate_ref, gatet_ref = (gatep_ref, gates_ref)[seg], (gatetp_ref, gatets_ref)[seg]
        hn = _rms(x_ref[...], g_ref[...]).astype(BF16)
        yield
        for c0 in range(0, GDN_W, PROJ_CHUNK):
            gdn_dst[:, c0:c0 + PROJ_CHUNK] = jnp.dot(hn, wa_ref[:, c0:c0 + PROJ_CHUNK], preferred_element_type=F32)
            yield
        for c0 in range(0, MLP_W, PROJ_CHUNK):
            ml_ref[:, c0:c0 + PROJ_CHUNK] = jnp.dot(hn, wb_ref[:, c0:c0 + PROJ_CHUNK], preferred_element_type=F32)
            yield
        raw = jnp.dot(hn, wb_ref[:, MLP_W:], preferred_element_type=F32)
        lane = lax.broadcasted_iota(jnp.int32, (tm, LANE), 1)
        gate_ref[...] = _activate_gates(raw, lane, pc_ref[0:1, :], pc_ref[1:2, :])
        raw_t = lax.dot_general(wst_ref[...], hn, _NT, preferred_element_type=F32)
        row = lax.broadcasted_iota(jnp.int32, (N_GATE, tm), 0)
        gatet_ref[...] = _activate_gates(raw_t, row, pr_ref[:, 0:1], pr_ref[:, 1:2])

    def body(seg):
        if seg == 0:
            _run_interleaved(preactivate_previous_tile(), projection_stages(0))
        else:
            pl.when(i == n_p_tiles)(lambda: _run_interleaved(preactivate_previous_tile()))
            _run_interleaved(projection_stages(1))

    _for_segment(n_p_tiles, body)


def _inproj(xp, xs, g, w_gdn, w_rest, ws_t, pcol, prow, cw, tm, n_seq_p):
    n_p, n_s = xp.shape[0], xs.shape[0]
    npt = n_p // tm
    tiles_per_seq = npt // n_seq_p

    def out2(width):
        return _two_segment_specs(tm, width, npt)

    def shp2(width):
        return [jax.ShapeDtypeStruct((n_p, width), F32), jax.ShapeDtypeStruct((n_s, width), F32)]

    def prev_tile(i):
        return jnp.clip(i - 1, 0, npt - 1)

    return pl.pallas_call(
        functools.partial(_inproj_kernel, n_p_tiles=npt, tiles_per_seq=tiles_per_seq),
        grid=((n_p + n_s) // tm,),
        in_specs=_two_segment_specs(tm, D_MODEL, npt) + [
            pl.BlockSpec((1, D_MODEL), lambda i: (0, 0)),
            pl.BlockSpec((D_MODEL, GDN_W), lambda i: (0, 0)),
            pl.BlockSpec((D_MODEL, MLP_W + LANE), lambda i: (0, 0)),
            pl.BlockSpec((N_GATE, D_MODEL), lambda i: (0, 0)),
            pl.BlockSpec((SUBLANE, LANE), lambda i: (0, 0)),
            pl.BlockSpec((N_GATE, LANE), lambda i: (0, 0)),
            pl.BlockSpec((SUBLANE, CONV_CH), lambda i: (0, 0)),
        ],
        out_specs=[
            pl.BlockSpec((tm, GDN_W), lambda i: (prev_tile(i), 0)),
            pl.BlockSpec((tm, GDN_W), lambda i: (jnp.maximum(i - npt, 0), 0)),
        ] + out2(MLP_W) + out2(LANE) + [
            pl.BlockSpec((N_GATE, tm), lambda i: (0, jnp.minimum(i, npt - 1))),
            pl.BlockSpec((N_GATE, tm), lambda i: (0, jnp.maximum(i - npt, 0))),
            pl.BlockSpec((1, CONV_W - 1, CONV_CH), lambda i: (prev_tile(i) // tiles_per_seq, 0, 0)),
        ],
        out_shape=shp2(GDN_W) + shp2(MLP_W) + shp2(LANE) + [
            jax.ShapeDtypeStruct((N_GATE, n_p), F32), jax.ShapeDtypeStruct((N_GATE, n_s), F32),
            jax.ShapeDtypeStruct((n_seq_p, CONV_W - 1, CONV_CH), F32)],
        scratch_shapes=[pltpu.VMEM((tm + SUBLANE, CONV_CH), F32), pltpu.VMEM((tm, GDN_W), F32)],
        compiler_params=pltpu.CompilerParams(dimension_semantics=("arbitrary",), vmem_limit_bytes=VMEM_LIMIT),
        name="inproj",
    )(xp, xs, g, w_gdn, w_rest, ws_t, pcol, prow, cw)


def _chunk_masks(L):
    ri = lax.broadcasted_iota(jnp.int32, (L, L), 0)
    ci = lax.broadcasted_iota(jnp.int32, (L, L), 1)
    return ri >= ci, ri > ci, ri <= ci


def _run_interleaved(*stage_generators):
    live = list(stage_generators)
    while live:
        for gen in list(live):
            if next(gen, StopIteration) is StopIteration:
                live.remove(gen)


def _gdn_stages(*refs, L, G, has_state):
    if has_state:
        (xin_ref, gate_ref, gatet_ref, cw_ref, ng_ref, cst_ref, s0_ref,
         mix_ref, cnew_ref, snew_ref, xc_ref, s_ref) = refs
    else:
        xin_ref, gate_ref, gatet_ref, ng_ref, mix_ref, snew_ref, s_ref = refs
    c = pl.program_id(1)

    @pl.when(c == 0)
    def _():
        if has_state:
            xc_ref[:, 0:SUBLANE, :] = jnp.zeros((G, SUBLANE, CONV_CH), F32)
            xc_ref[:, SUBLANE - (CONV_W - 1):SUBLANE, :] = cst_ref[...]
            s_ref[...] = s0_ref[...]
        else:
            s_ref[...] = jnp.zeros_like(s_ref)

    if has_state:
        @pl.when(c > 0)
        def _():
            xc_ref[:, 0:SUBLANE, :] = xc_ref[:, L:L + SUBLANE, :]

    yield
    tril, strict, triu = _chunk_masks(L)
    tril_f, triu_f = tril.astype(F32), triu.astype(F32)
    base = SUBLANE - (CONV_W - 1)

    chains = [(g, h) for g in range(G) for h in range(H_A)]
    s_old = [s_ref[g, h] for g, h in chains]
    if has_state:
        for g in range(G):
            xc_ref[g, SUBLANE:SUBLANE + L, :] = xin_ref[g, :, :CONV_CH]

    q, k, v, beta, gc, gl, decay = [], [], [], [], [], [], []
    for g in range(G):
        if has_state:
            conv = xc_ref[g, base:base + L, :] * cw_ref[0:1, :]
            for j in range(1, CONV_W):
                conv = conv + xc_ref[g, base + j:base + j + L, :] * cw_ref[j:j + 1, :]
            cnew_ref[g] = xc_ref[g, SUBLANE + L - (CONV_W - 1):SUBLANE + L, :]
            act = _silu(conv)
        else:
            act = xin_ref[g, :, :CONV_CH]
        gact = gate_ref[g]
        cum_c = _dot_hi(tril_f, gact)
        cum_r = _dot_hi(gatet_ref[g, 0], triu_f)
        for h in range(H_A):
            q.append(act[:, h * DK_A:(h + 1) * DK_A])
            k.append(act[:, H_A * DK_A + h * DK_A:H_A * DK_A + (h + 1) * DK_A])
            v.append(act[:, 2 * H_A * DK_A + h * DV_A:2 * H_A * DK_A + (h + 1) * DV_A])
            beta.append(gact[:, 4 + h:5 + h])
            gc.append(cum_c[:, h:h + 1])
            gl.append(cum_c[L - 1:L, h:h + 1])
            gr = cum_r[h:h + 1, :]
            decay.append(jnp.where(tril, jnp.exp(jnp.where(tril, cum_c[:, h:h + 1] - gr, 0.0)), 0.0))
        yield

    nc = range(len(chains))
    if has_state:
        qss = [jnp.sum(q[i] * q[i], axis=-1, keepdims=True) for i in nc]
        kss = [jnp.sum(k[i] * k[i], axis=-1, keepdims=True) for i in nc]
        q = [q[i] * (lax.rsqrt(qss[i] + EPS) * (DK_A ** -0.5)) for i in nc]
        k = [k[i] * lax.rsqrt(kss[i] + EPS) for i in nc]
    kb = [k[i] * beta[i] for i in nc]
    egc = [jnp.exp(gc[i]) for i in nc]
    yield
    kk = [_dot(kb[i], k[i], _NT) for i in nc]
    yield
    qk = [_dot(q[i], k[i], _NT) for i in nc]
    yield
    eye = (lax.broadcasted_iota(jnp.int32, (L, L), 0) == lax.broadcasted_iota(jnp.int32, (L, L), 1)).astype(F32)
    pw = [-jnp.where(strict, kk[i] * decay[i], 0.0) for i in nc]
    t_inv = [eye + pw[i] for i in nc]
    span = 2
    while span < L:
        yield
        pw = [_dot(pw[i], pw[i]) for i in nc]
        yield
        t_inv = [t_inv[i] + _dot(t_inv[i], pw[i]) for i in nc]
        span *= 2
    yield
    sol = [_dot(t_inv[i], jnp.concatenate([v[i] * beta[i], kb[i] * egc[i]], axis=-1)) for i in nc]
    yield
    qs = [_dot(q[i] * egc[i], s_old[i]) for i in nc]
    yield
    ws = [_dot(sol[i][:, DV_A:], s_old[i]) for i in nc]
    v_new = [sol[i][:, :DV_A] - ws[i] for i in nc]
    yield
    o = [qs[i] + _dot(jnp.where(tril, qk[i] * decay[i], 0.0), v_new[i]) for i in nc]
    yield
    s_new = [s_old[i] * jnp.exp(gl[i]) + _dot(k[i] * jnp.exp(gl[i] - gc[i]), v_new[i], _TN) for i in nc]
    yield
    ms = [jnp.mean(o[i] * o[i], axis=-1, keepdims=True) for i in nc]
    on = [o[i] * lax.rsqrt(ms[i] + EPS) for i in nc]
    yield
    for i, (g, h) in enumerate(chains):
        z = xin_ref[g, :, CONV_CH + h * DV_A:CONV_CH + (h + 1) * DV_A]
        out = on[i] * ng_ref[...] * (_silu(z) if has_state else z)
        mix_ref[g, :, h * DV_A:(h + 1) * DV_A] = out.astype(mix_ref.dtype)
    yield
    for i, (g, h) in enumerate(chains):
        s_ref[g, h] = s_new[i]
        snew_ref[g, h] = s_new[i]


def _mlstm_stages(*refs, L, G, has_state):
    if has_state:
        (xin_ref, gate_ref, gatet_ref, ng_ref, c0_ref, n0_ref, m0_ref,
         mix_ref, cnew_ref, nnew_ref, mnew_ref, c_ref, n_ref, m_ref) = refs
    else:
        (xin_ref, gate_ref, gatet_ref, ng_ref,
         mix_ref, cnew_ref, nnew_ref, mnew_ref, c_ref, n_ref, m_ref) = refs
    c = pl.program_id(1)

    @pl.when(c == 0)
    def _():
        c_ref[...] = jnp.zeros_like(c_ref)
        n_ref[...] = jnp.zeros_like(n_ref)
        m_ref[...] = jnp.zeros_like(m_ref)
        if has_state:
            c_ref[:, :, 0:DK_B, :] = c0_ref[...]
            n_ref[:, 0:H_B, 0:DK_B] = n0_ref[...]
            m_ref[:, 0:1, 0:H_B] = m0_ref[...]

    yield
    tril, _, triu = _chunk_masks(L)
    tril_f, triu_f = tril.astype(F32), triu.astype(F32)

    chains = [(g, h) for g in range(G) for h in range(H_B)]
    nc = range(len(chains))
    c_old = [c_ref[g, h] for g, h in chains]
    n_old = [n_ref[g, h:h + 1, :] for g, h in chains]
    m_old = [m_ref[g, 0:1, h:h + 1] for g, h in chains]

    v0 = 2 * H_B * LANE
    q = [xin_ref[g, :, h * LANE:(h + 1) * LANE] * (DK_B ** -0.5) for g, h in chains]
    k = [xin_ref[g, :, (H_B + h) * LANE:(H_B + h + 1) * LANE] for g, h in chains]
    v = [xin_ref[g, :, v0 + h * DV_B:v0 + (h + 1) * DV_B] for g, h in chains]
    ig_c, b_c, b_last, d_log = [], [], [], []
    for g in range(G):
        gact = gate_ref[g]
        gact_t = gatet_ref[g, 0]
        cum_c = _dot_hi(tril_f, gact)
        cum_r = _dot_hi(gact_t, triu_f)
        for h in range(H_B):
            ig_c.append(gact[:, 8 + h:9 + h])
            b_c.append(cum_c[:, 12 + h:13 + h])
            b_last.append(cum_c[L - 1:L, 12 + h:13 + h])
            d_log.append(jnp.where(tril, cum_c[:, 12 + h:13 + h] - cum_r[12 + h:13 + h, :]
                                   + gact_t[8 + h:9 + h, :], NEG))
        yield
    qk = [_dot(q[i], k[i], _NT) for i in nc]
    yield
    qc = [_dot(q[i], c_old[i]) for i in nc]
    yield
    inter = [b_c[i] + m_old[i] for i in nc]
    m_t = [jnp.maximum(inter[i], jnp.max(d_log[i], axis=-1, keepdims=True)) for i in nc]
    yield
    s = [qk[i] * jnp.exp(d_log[i] - m_t[i]) for i in nc]
    e_inter = [jnp.exp(inter[i] - m_t[i]) for i in nc]
    yield
    sv = [_dot(s[i], v[i]) for i in nc]
    yield
    m_new = [m_t[i][L - 1:L, :] for i in nc]
    kw = [k[i] * jnp.exp(b_last[i] - b_c[i] + ig_c[i] - m_new[i]) for i in nc]
    f_tot = [jnp.exp(b_last[i] + m_old[i] - m_new[i]) for i in nc]
    yield
    c_new = [f_tot[i] * c_old[i] + _dot(kw[i], v[i], _TN) for i in nc]
    yield
    n_new = [f_tot[i] * n_old[i] + jnp.sum(kw[i], axis=0, keepdims=True) for i in nc]
    qn = [jnp.sum(q[i] * n_old[i], axis=-1, keepdims=True) for i in nc]
    yield
    ssum = [jnp.sum(s[i], axis=-1, keepdims=True) for i in nc]
    yield
    den = [jnp.maximum(jnp.abs(e_inter[i] * qn[i] + ssum[i]), jnp.exp(-m_t[i])) for i in nc]
    hh = [(e_inter[i] * qc[i] + sv[i]) / den[i] for i in nc]
    yield
    ms = [jnp.mean(hh[i] * hh[i], axis=-1, keepdims=True) for i in nc]
    hn = [hh[i] * lax.rsqrt(ms[i] + EPS) for i in nc]
    yield
    for i, (g, h) in enumerate(chains):
        og = xin_ref[g, :, v0 + H_B * DV_B + h * DV_B:v0 + H_B * DV_B + (h + 1) * DV_B]
        mix_ref[g, :, h * DV_B:(h + 1) * DV_B] = (hn[i] * ng_ref[h:h + 1, :] * _sigmoid(og)).astype(mix_ref.dtype)
    yield
    for i, (g, h) in enumerate(chains):
        c_ref[g, h] = c_new[i]
        n_ref[g, h:h + 1, :] = n_new[i]
        m_ref[g, 0:1, h:h + 1] = m_new[i]
        cnew_ref[g, h] = c_new[i][0:DK_B, :]
        nnew_ref[g, h:h + 1, :] = n_new[i][:, 0:DK_B]
        mnew_ref[g, 0:1, h:h + 1] = m_new[i]


def _mixers_kernel(*refs, L, G, has_state):
    if has_state:
        (gdn_ref, gate_ref, gatet_ref, nga_ref, ml_ref, ngb_ref, cw_ref, cst_ref, s0_ref, c0_ref, n0_ref, m0_ref,
         mixa_ref, snew_ref, mixb_ref, cnew_ref, nnew_ref, mnew_ref, convnew_ref,
         s_ref, c_ref, n_ref, m_ref, xc_ref) = refs
        gdn_refs = (gdn_ref, gate_ref, gatet_ref, cw_ref, nga_ref, cst_ref, s0_ref,
                    mixa_ref, convnew_ref, snew_ref, xc_ref, s_ref)
        ml_refs = (ml_ref, gate_ref, gatet_ref, ngb_ref, c0_ref, n0_ref, m0_ref,
                   mixb_ref, cnew_ref, nnew_ref, mnew_ref, c_ref, n_ref, m_ref)
    else:
        (gdn_ref, gate_ref, gatet_ref, nga_ref, ml_ref, ngb_ref,
         mixa_ref, snew_ref, mixb_ref, cnew_ref, nnew_ref, mnew_ref, s_ref, c_ref, n_ref, m_ref) = refs
        gdn_refs = (gdn_ref, gate_ref, gatet_ref, nga_ref, mixa_ref, snew_ref, s_ref)
        ml_refs = (ml_ref, gate_ref, gatet_ref, ngb_ref, mixb_ref, cnew_ref, nnew_ref, mnew_ref, c_ref, n_ref, m_ref)
    _run_interleaved(_gdn_stages(*gdn_refs, L=L, G=G, has_state=has_state),
                     _mlstm_stages(*ml_refs, L=L, G=G, has_state=has_state))


def _mixers(gdn_in, ml_in, gates, gates_t, ng_a, ng_b, *, L, G, cw=None, state=None):
    n_seq, T, _ = gdn_in.shape
    n_c = T // L
    has_state = state is not None

    def seq_blk(*tail):
        return pl.BlockSpec((G,) + tail, lambda b, c: (b,) + (0,) * len(tail))

    def tok_blk(width):
        return pl.BlockSpec((G, L, width), lambda b, c: (b, c, 0))

    def seq_shape(*tail):
        return jax.ShapeDtypeStruct((n_seq,) + tail, F32)

    state_specs = [seq_blk(H_A, DK_A, DV_A), seq_blk(H_B, DK_B, DV_B), seq_blk(H_B, DK_B), seq_blk(1, H_B)]
    state_shapes = [seq_shape(H_A, DK_A, DV_A), seq_shape(H_B, DK_B, DV_B), seq_shape(H_B, DK_B), seq_shape(1, H_B)]
    conv_spec, conv_shape = seq_blk(CONV_W - 1, CONV_CH), seq_shape(CONV_W - 1, CONV_CH)
    in_specs = [
        tok_blk(GDN_W), tok_blk(LANE),
        pl.BlockSpec((G, 1, N_GATE, L), lambda b, c: (b, c, 0, 0)),
        pl.BlockSpec((1, DV_A), lambda b, c: (0, 0)),
        tok_blk(MLP_W),
        pl.BlockSpec((H_B, DV_B), lambda b, c: (0, 0)),
    ]
    args = [gdn_in, gates, gates_t, ng_a, ml_in, ng_b]
    out_specs = [tok_blk(H_A * DV_A), state_specs[0], tok_blk(H_B * DV_B)] + state_specs[1:]
    mix_dtype = BF16 if L % (2 * SUBLANE) == 0 else F32
    mix_a = jax.ShapeDtypeStruct((n_seq, T, H_A * DV_A), mix_dtype)
    mix_b = jax.ShapeDtypeStruct((n_seq, T, H_B * DV_B), mix_dtype)
    out_shape = [mix_a, state_shapes[0], mix_b] + state_shapes[1:]
    scratch = [pltpu.VMEM((G, H_A, DK_A, DV_A), F32), pltpu.VMEM((G, H_B, LANE, DV_B), F32),
               pltpu.VMEM((G, SUBLANE, LANE), F32), pltpu.VMEM((G, SUBLANE, LANE), F32)]
    if has_state:
        in_specs += [pl.BlockSpec((SUBLANE, CONV_CH), lambda b, c: (0, 0)), conv_spec] + state_specs
        args += [cw] + list(state)
        out_specs.append(conv_spec)
        out_shape.append(conv_shape)
        scratch.append(pltpu.VMEM((G, L + SUBLANE, CONV_CH), F32))
    return pl.pallas_call(
        functools.partial(_mixers_kernel, L=L, G=G, has_state=has_state),
        grid=(n_seq // G, n_c),
        in_specs=in_specs,
        out_specs=out_specs,
        out_shape=out_shape,
        scratch_shapes=scratch,
        compiler_params=pltpu.CompilerParams(dimension_semantics=("parallel", "arbitrary"),
                                             vmem_limit_bytes=VMEM_LIMIT),
        name=f"mixers_L{L}",
    )(*args)


def _outproj_kernel(xp_ref, xs_ref, map_ref, mas_ref, mbp_ref, mbs_ref, wo_ref, g_ref, rw_ref, rb_ref,
                    x1_ref, xsort_ref, info_ref, cpad_ref, *, n_p_tiles):
    half = H_A * DV_A
    tm = xp_ref.shape[0]

    def body(seg):
        x_ref, ma_ref, mb_ref = (xp_ref, xs_ref)[seg], (map_ref, mas_ref)[seg], (mbp_ref, mbs_ref)[seg]
        x1 = (x_ref[...] + jnp.dot(ma_ref[...].astype(BF16), wo_ref[:half, :], preferred_element_type=F32)
              + jnp.dot(mb_ref[...].astype(BF16), wo_ref[half:, :], preferred_element_type=F32))
        x1_ref[...] = x1
        hn = _rms(x1, g_ref[...])
        hn_hi = hn.astype(BF16)
        hn_lo = (hn - hn_hi.astype(F32)).astype(BF16)
        logits = (jnp.dot(hn_hi, rw_ref[0], preferred_element_type=F32)
                  + jnp.dot(hn_hi, rw_ref[1], preferred_element_type=F32)
                  + jnp.dot(hn_lo, rw_ref[0], preferred_element_type=F32)) + rb_ref[...]

        vals = logits.T[:N_EXPERTS, :]
        e_iota = lax.broadcasted_iota(jnp.int32, (N_EXPERTS, tm), 0)
        sels, tops = [], []
        for _ in range(TOP_K):
            m = jnp.max(vals, axis=0, keepdims=True)
            first = jnp.min(jnp.where(vals == m, e_iota, N_EXPERTS), axis=0, keepdims=True)
            sel = e_iota == first
            vals = jnp.where(sel, -jnp.inf, vals)
            sels.append(sel)
            tops.append(m)
        ex = [jnp.exp(t - tops[0]) for t in tops]
        den = ex[0] + ex[1] + ex[2] + ex[3]
        gates = [e / den for e in ex]
        mask = sels[0].astype(F32) + sels[1].astype(F32) + sels[2].astype(F32) + sels[3].astype(F32)
        ri = lax.broadcasted_iota(jnp.int32, (tm, tm), 0)
        ci = lax.broadcasted_iota(jnp.int32, (tm, tm), 1)
        rank = _dot(mask, (ri < ci).astype(F32))
        cnt = jnp.sum(mask, axis=1, keepdims=True)
        cpad = jnp.ceil(cnt * (1.0 / SEG_ALIGN)) * SEG_ALIGN
        cpad_b = jnp.broadcast_to(cpad, (N_EXPERTS, tm))
        er = lax.broadcasted_iota(jnp.int32, (N_EXPERTS, N_EXPERTS), 0)
        ec = lax.broadcasted_iota(jnp.int32, (N_EXPERTS, N_EXPERTS), 1)
        seg_off = _dot((er > ec).astype(F32), cpad_b)
        pos = seg_off + rank
        q = [jnp.sum(jnp.where(s, pos, 0.0), axis=0, keepdims=True) for s in sels]

        j_iota = lax.broadcasted_iota(jnp.int32, (MOE_CAP, tm), 0).astype(F32)
        perm = jnp.zeros((MOE_CAP, tm), F32)
        for kk in range(TOP_K):
            perm = jnp.where(j_iota == q[kk], 1.0, perm)
        xsorted = _dot(perm, hn)
        xsort_ref[...] = xsorted.astype(MOE_DTYPE)

        r_iota = lax.broadcasted_iota(jnp.int32, (LANE, tm), 0)
        info = jnp.zeros((LANE, tm), F32)
        for kk in range(TOP_K):
            info = jnp.where(r_iota == kk, q[kk], info)
            info = jnp.where(r_iota == TOP_K + kk, gates[kk], info)
        info_ref[...] = info.T
        cpad_ref[0] = cpad_b[:, :LANE]

    _for_segment(n_p_tiles, body)


def _outproj(xp, xs, ma_p, ma_s, mb_p, mb_s, w_out, g, rw, rb, tm):
    n_p, n_s = xp.shape[0], xs.shape[0]
    n = n_p + n_s
    nt = n // tm
    npt = n_p // tm
    half = H_A * DV_A
    return pl.pallas_call(
        functools.partial(_outproj_kernel, n_p_tiles=npt),
        grid=(nt,),
        in_specs=_two_segment_specs(tm, D_MODEL, npt) + _two_segment_specs(tm, half, npt)
        + _two_segment_specs(tm, half, npt) + [
            pl.BlockSpec((D_MODEL, D_MODEL), lambda i: (0, 0)),
            pl.BlockSpec((1, D_MODEL), lambda i: (0, 0)),
            pl.BlockSpec((2, D_MODEL, LANE), lambda i: (0, 0, 0)),
            pl.BlockSpec((1, LANE), lambda i: (0, 0)),
        ],
        out_specs=[
            pl.BlockSpec((tm, D_MODEL), lambda i: (i, 0)),
            pl.BlockSpec((MOE_CAP, D_MODEL), lambda i: (i, 0)),
            pl.BlockSpec((tm, LANE), lambda i: (i, 0)),
            pl.BlockSpec((1, N_EXPERTS, LANE), lambda i: (i, 0, 0)),
        ],
        out_shape=[
            jax.ShapeDtypeStruct((n, D_MODEL), F32),
            jax.ShapeDtypeStruct((nt * MOE_CAP, D_MODEL), MOE_DTYPE),
            jax.ShapeDtypeStruct((n, LANE), F32),
            jax.ShapeDtypeStruct((nt, N_EXPERTS, LANE), F32),
        ],
        compiler_params=pltpu.CompilerParams(dimension_semantics=("arbitrary",), vmem_limit_bytes=VMEM_LIMIT),
        name="outproj",
    )(xp, xs, ma_p, ma_s, mb_p, mb_s, w_out, g, rw, rb)


def _expert_kernel(be_ref, bj_ref, tf_ref, tl_ref, cov_ref, nu_ref, vt_ref, ct_ref, lt_ref, nx_ref, ws_ref,
                   xs_hbm, wgu_hbm, bgu_ref, wd_hbm, bd_ref, ys_hbm,
                   xbuf, ybuf, gsem, ssem, wgu_st, wd_st, wsem, wgu_bf, wd_bf, *, nt):
    b = pl.program_id(0)
    n_used = nu_ref[0]
    slot = b % 2

    def start_pieces(bb, copy, s):
        e = be_ref[bb]
        base = bj_ref[bb] * MOE_BLK

        def body(t, carry):
            k = e * nt + t
            lo = jnp.maximum(vt_ref[k], base)
            ln = jnp.minimum(ct_ref[k], base + MOE_BLK) - lo

            @pl.when(ln > 0)
            def _():
                copy(s, pl.multiple_of(lt_ref[k] + lo, SEG_ALIGN), pl.multiple_of(lo - base, SEG_ALIGN),
                     pl.multiple_of(ln, SEG_ALIGN)).start()
            return carry

        lax.fori_loop(tf_ref[bb], tl_ref[bb] + 1, body, 0)

    def weight_copies(e):
        return (pltpu.make_async_copy(wgu_hbm.at[e], wgu_st, wsem.at[0]),
                pltpu.make_async_copy(wd_hbm.at[e], wd_st, wsem.at[1]))

    def cast_weights(p):
        wgu_bf[p] = wgu_st[...].astype(BF16)
        wd_bf[p] = wd_st[...].astype(BF16)

    def gather_copy(s, src, dst, size):
        return pltpu.make_async_copy(xs_hbm.at[pl.ds(src, size)], xbuf.at[s, pl.ds(dst, size)], gsem.at[s])

    def scatter_copy(s, src, dst, size):
        return pltpu.make_async_copy(ybuf.at[s, pl.ds(dst, size)], ys_hbm.at[pl.ds(src, size)], ssem.at[s])

    def wait_rows(count, copy, s):
        @pl.when(count > 0)
        def _():
            copy(s, 0, 0, pl.multiple_of(count, SEG_ALIGN)).wait()

    @pl.when(b == 0)
    def _():
        xbuf[...] = jnp.zeros_like(xbuf)
        start_pieces(0, gather_copy, 0)

    @pl.when(b + 1 < n_used)
    def _():
        start_pieces(b + 1, gather_copy, 1 - slot)

    @pl.when(b < n_used)
    def _():
        e = be_ref[b]
        first = jnp.logical_or(b == 0, be_ref[jnp.maximum(b - 1, 0)] != e)
        last = jnp.logical_or(b == n_used - 1, be_ref[jnp.minimum(b + 1, n_used - 1)] != e)
        has_next = nx_ref[b] < N_EXPERTS
        p = ws_ref[b]

        @pl.when(b == 0)
        def _():
            for cp in weight_copies(e):
                cp.start()
            for cp in weight_copies(e):
                cp.wait()
            cast_weights(p)

        @pl.when(jnp.logical_and(first, has_next))
        def _():
            for cp in weight_copies(nx_ref[b]):
                cp.start()

        wait_rows(cov_ref[b], gather_copy, slot)

        @pl.when(b >= 2)
        def _():
            wait_rows(cov_ref[jnp.maximum(b - 2, 0)], scatter_copy, slot)

        def expert_mlp(rows):
            hgu = jnp.dot(xbuf[slot, :rows].astype(BF16), wgu_bf[p], preferred_element_type=F32) + bgu_ref[0]
            gate = jnp.minimum(hgu[:, :D_FF], SWIGLU_LIMIT)
            up = jnp.clip(hgu[:, D_FF:], -SWIGLU_LIMIT, SWIGLU_LIMIT)
            act = (up + 1.0) * gate * _sigmoid(SWIGLU_ALPHA * gate)
            y = jnp.dot(act.astype(BF16), wd_bf[p], preferred_element_type=F32) + bd_ref[0]
            ybuf[slot, :rows] = y.astype(MOE_DTYPE)

        quarter = MOE_BLK // 4
        for nq in range(1, 5):
            pl.when(jnp.logical_and(cov_ref[b] > (nq - 1) * quarter, cov_ref[b] <= nq * quarter))(
                functools.partial(expert_mlp, nq * quarter))

        @pl.when(jnp.logical_and(last, has_next))
        def _():
            for cp in weight_copies(nx_ref[b]):
                cp.wait()
            cast_weights(1 - p)

        start_pieces(b, scatter_copy, slot)

        @pl.when(b == n_used - 1)
        def _():
            wait_rows(cov_ref[b], scatter_copy, slot)
            wait_rows(jnp.where(b >= 1, cov_ref[jnp.maximum(b - 1, 0)], 0), scatter_copy, 1 - slot)


def _experts(tables, xs, w_gu, b_gu, w_down, b_down, nt):
    nb = tables[0].shape[0]

    def bias_blk(b, *t):
        return (t[0][jnp.minimum(b, t[5][0] - 1)], 0, 0)

    grid_spec = pltpu.PrefetchScalarGridSpec(
        num_scalar_prefetch=len(tables),
        grid=(nb,),
        in_specs=[
            pl.BlockSpec(memory_space=pl.ANY),
            pl.BlockSpec(memory_space=pl.ANY),
            pl.BlockSpec((1, 1, 2 * D_FF), bias_blk),
            pl.BlockSpec(memory_space=pl.ANY),
            pl.BlockSpec((1, 1, D_MODEL), bias_blk),
        ],
        out_specs=pl.BlockSpec(memory_space=pl.ANY),
        scratch_shapes=[
            pltpu.VMEM((2, MOE_BLK, D_MODEL), MOE_DTYPE),
            pltpu.VMEM((2, MOE_BLK, D_MODEL), MOE_DTYPE),
            pltpu.SemaphoreType.DMA((2,)),
            pltpu.SemaphoreType.DMA((2,)),
            pltpu.VMEM((D_MODEL, 2 * D_FF), F32),
            pltpu.VMEM((D_FF, D_MODEL), F32),
            pltpu.SemaphoreType.DMA((2,)),
            pltpu.VMEM((2, D_MODEL, 2 * D_FF), BF16),
            pltpu.VMEM((2, D_FF, D_MODEL), BF16),
        ],
    )
    return pl.pallas_call(
        functools.partial(_expert_kernel, nt=nt),
        grid_spec=grid_spec,
        out_shape=jax.ShapeDtypeStruct(xs.shape, xs.dtype),
        input_output_aliases={len(tables): 0},
        compiler_params=pltpu.CompilerParams(dimension_semantics=("arbitrary",), vmem_limit_bytes=VMEM_LIMIT),
        name="experts",
    )(*tables, xs, w_gu, b_gu, w_down, b_down)


def _combine_kernel(ys_ref, info_ref, x1_ref, pp_ref, ps_ref, gple_ref, wg_ref, wp_ref, gfin_ref,
                    outp_ref, outs_ref, *, n_p_tiles):
    tm = x1_ref.shape[0]

    def body(seg):
        p_ref, out_ref = (pp_ref, ps_ref)[seg], (outp_ref, outs_ref)[seg]
        info = info_ref[...]
        j_iota = lax.broadcasted_iota(jnp.int32, (tm, MOE_CAP), 1).astype(F32)
        gmat = jnp.zeros((tm, MOE_CAP), F32)
        for kk in range(TOP_K):
            gmat = jnp.where(j_iota == info[:, kk:kk + 1], info[:, TOP_K + kk:TOP_K + kk + 1], gmat)
        x2 = x1_ref[...] + jnp.dot(gmat.astype(BF16), ys_ref[...].astype(BF16), preferred_element_type=F32)
        hn = _rms(x2, gple_ref[...]).astype(BF16)
        gate = _sigmoid(jnp.dot(hn, wg_ref[...], preferred_element_type=F32))
        pe = jnp.dot(p_ref[...].astype(BF16), wp_ref[...], preferred_element_type=F32)
        x3 = x2 + gate * pe
        out_ref[...] = _rms(x3, gfin_ref[...])

    _for_segment(n_p_tiles, body)


def _combine(ys, info, x1, pp, ps, g_ple, w_gate, w_p, g_fin, tm):
    n_p, n_s = pp.shape[0], ps.shape[0]
    n = n_p + n_s
    nt = n // tm
    npt = n_p // tm
    return pl.pallas_call(
        functools.partial(_combine_kernel, n_p_tiles=npt),
        grid=(nt,),
        in_specs=[
            pl.BlockSpec((MOE_CAP, D_MODEL), lambda i: (i, 0)),
            pl.BlockSpec((tm, LANE), lambda i: (i, 0)),
            pl.BlockSpec((tm, D_MODEL), lambda i: (i, 0)),
        ] + _two_segment_specs(tm, PLE_DIM, npt) + [
            pl.BlockSpec((1, D_MODEL), lambda i: (0, 0)),
            pl.BlockSpec((D_MODEL, D_MODEL), lambda i: (0, 0)),
            pl.BlockSpec((PLE_DIM, D_MODEL), lambda i: (0, 0)),
            pl.BlockSpec((1, D_MODEL), lambda i: (0, 0)),
        ],
        out_specs=_two_segment_specs(tm, D_MODEL, npt),
        out_shape=[jax.ShapeDtypeStruct((n_p, D_MODEL), F32), jax.ShapeDtypeStruct((n_s, D_MODEL), F32)],
        compiler_params=pltpu.CompilerParams(dimension_semantics=("arbitrary",), vmem_limit_bytes=VMEM_LIMIT),
        name="combine",
    )(ys, info, x1, pp, ps, g_ple, w_gate, w_p, g_fin)


def _block_tables(seg_len, nb):
    nt = seg_len.shape[0]
    seg_off = jnp.cumsum(seg_len, axis=1) - seg_len
    seg_end = jnp.cumsum(seg_len, axis=0).T
    seg_start = seg_end - seg_len.T
    n_rows = seg_end[:, -1]
    n_blk = (n_rows + MOE_BLK - 1) // MOE_BLK
    blk_end = jnp.cumsum(n_blk)
    b = jnp.arange(nb, dtype=jnp.int32)
    block_e = jnp.minimum(jnp.sum((blk_end[None, :] <= b[:, None]).astype(jnp.int32), axis=1), N_EXPERTS - 1)
    idx = jnp.where(n_blk > 0, jnp.arange(N_EXPERTS, dtype=jnp.int32), N_EXPERTS)
    nxt = jnp.concatenate([lax.cummin(idx, axis=0, reverse=True)[1:], jnp.full((1,), N_EXPERTS, jnp.int32)])
    parity = (jnp.cumsum((n_blk > 0).astype(jnp.int32)) - 1) % 2
    per_e = jnp.concatenate([jnp.stack([blk_end - n_blk, n_rows, nxt, parity], axis=1), seg_start, seg_end],
                            axis=1).astype(F32)
    onehot = (block_e[:, None] == jnp.arange(N_EXPERTS, dtype=jnp.int32)[None, :]).astype(F32)
    per_b = jnp.dot(onehot, per_e, precision=HI).astype(jnp.int32)
    block_j = b - per_b[:, 0]
    base = block_j * MOE_BLK
    t_first = jnp.sum((per_b[:, 4 + nt:] <= base[:, None]).astype(jnp.int32), axis=1)
    t_last = jnp.sum((per_b[:, 4:4 + nt] < (base + MOE_BLK)[:, None]).astype(jnp.int32), axis=1) - 1
    cover = jnp.clip(per_b[:, 1] - base, 0, MOE_BLK)
    seg_shift = (jnp.arange(nt, dtype=jnp.int32)[:, None] * MOE_CAP + seg_off).T - seg_start
    tables = (block_e, block_j, t_first, t_last, cover, blk_end[-1:], seg_start.reshape(-1),
              seg_end.reshape(-1), seg_shift.reshape(-1), per_b[:, 2], per_b[:, 3])
    return tuple(t.astype(jnp.int32) for t in tables)


def _rearranged_in_weights(w_in):
    o = np.cumsum([0, CONV_CH, H_A * DV_A, H_A, H_A, H_B * DK_B, H_B * DK_B, H_B * DV_B, H_B * DV_B, H_B, H_B])
    conv_in, z_a, a_a, b_a, q_b, k_b, v_b, o_b, i_b, f_b = (w_in[:, int(o[j]):int(o[j + 1])] for j in range(10))
    zpad = jnp.zeros((D_MODEL, LANE - DK_B), w_in.dtype)

    def pad_heads(w):
        return jnp.concatenate([jnp.concatenate([w[:, h * DK_B:(h + 1) * DK_B], zpad], axis=1) for h in range(H_B)],
                               axis=1)

    small = jnp.concatenate([a_a, b_a, i_b, f_b], axis=1)
    w_gdn = w_in[:, :GDN_W]
    w_rest = jnp.concatenate([pad_heads(q_b), pad_heads(k_b), v_b, o_b,
                              small, jnp.zeros((D_MODEL, LANE - N_GATE), w_in.dtype)], axis=1)
    return w_gdn.astype(BF16), w_rest.astype(BF16), small.T.astype(BF16)


def _gate_params(a_log, dt_bias, i_bias, f_bias):
    z4 = jnp.zeros((4,), F32)
    alog = jnp.concatenate([a_log.astype(F32), z4, z4, z4])
    bias = jnp.concatenate([dt_bias.astype(F32), z4, i_bias.astype(F32), f_bias.astype(F32)])
    pad = jnp.zeros((LANE - N_GATE,), F32)
    pcol = jnp.zeros((SUBLANE, LANE), F32).at[0].set(jnp.concatenate([alog, pad])).at[1].set(
        jnp.concatenate([bias, pad]))
    prow = jnp.zeros((N_GATE, LANE), F32).at[:, 0].set(alog).at[:, 1].set(bias)
    return pcol, prow


def kernel(x_prompt, x_sample, p_prompt, p_sample, state_conv, state_gdn, state_mlstm_c, state_mlstm_n, state_mlstm_m, norm_attn_g, w_in, conv_w, gdn_a_log, gdn_dt_bias, gdn_norm_g, mlstm_i_bias, mlstm_f_bias, mlstm_norm_g, w_out, norm_moe_g, router_w, router_b, expert_w_gu, expert_b_gu, expert_w_down, expert_b_down, norm_ple_g, ple_gate_w, ple_w, final_norm_g):
    bp, tp, _ = x_prompt.shape
    bs, ts, _ = x_sample.shape
    n_p, n_s = bp * tp, bs * ts
    n = n_p + n_s
    lp, ls = min(tp, CHUNK), min(ts, CHUNK)
    tm = 256
    gp = 4 if bp % 4 == 0 else 1
    gs = 16 if bs % 16 == 0 else 1
    assert tp % lp == 0 and ts % ls == 0 and tp % tm == 0 and n_s % tm == 0 and ls % SUBLANE == 0

    xp = x_prompt.reshape(n_p, D_MODEL)
    xs = x_sample.reshape(n_s, D_MODEL)

    w_gdn, w_rest, ws_t = _rearranged_in_weights(w_in[0])
    pcol, prow = _gate_params(gdn_a_log[0], gdn_dt_bias[0], mlstm_i_bias[0], mlstm_f_bias[0])
    cw = jnp.zeros((SUBLANE, CONV_CH), F32).at[:CONV_W].set(conv_w[0].astype(F32))
    gdn_p, gdn_s, ml_p, ml_s, gate_p, gate_s, gatet_p, gatet_s, conv_p = _inproj(
        xp, xs, norm_attn_g[0].reshape(1, D_MODEL), w_gdn, w_rest, ws_t, pcol, prow, cw, tm, bp)
    gt_p = gatet_p.reshape(N_GATE, bp, tp // lp, lp).transpose(1, 2, 0, 3)
    gt_s = gatet_s.reshape(N_GATE, bs, ts // ls, ls).transpose(1, 2, 0, 3)

    ng_a = gdn_norm_g[0].reshape(1, DV_A).astype(F32)
    ng_b = mlstm_norm_g[0].reshape(H_B, DV_B).astype(F32)
    ma_p, gdn_st_p, mb_p, c_p, nn_p, m_p = _mixers(
        gdn_p.reshape(bp, tp, GDN_W), ml_p.reshape(bp, tp, MLP_W), gate_p.reshape(bp, tp, LANE), gt_p, ng_a, ng_b,
        L=lp, G=gp)
    ma_s, gdn_st_s, mb_s, c_s, nn_s, m_s, conv_s = _mixers(
        gdn_s.reshape(bs, ts, GDN_W), ml_s.reshape(bs, ts, MLP_W), gate_s.reshape(bs, ts, LANE), gt_s, ng_a, ng_b,
        L=ls, G=gs, cw=cw,
        state=(state_conv[0], state_gdn[0], state_mlstm_c[0], state_mlstm_n[0], state_mlstm_m[0].reshape(bs, 1, H_B)))
    half = H_A * DV_A

    rw = jnp.zeros((D_MODEL, LANE), F32).at[:, :N_EXPERTS].set(router_w[0])
    rw_hi = rw.astype(BF16)
    rw = jnp.stack([rw_hi, (rw - rw_hi.astype(F32)).astype(BF16)])
    rb = jnp.full((1, LANE), NEG, F32).at[0, :N_EXPERTS].set(router_b[0])
    x1, x_sorted, info, seg_len = _outproj(xp, xs, ma_p.reshape(n_p, half), ma_s.reshape(n_s, half),
                                           mb_p.reshape(n_p, half), mb_s.reshape(n_s, half),
                                           w_out[0].astype(BF16), norm_moe_g[0].reshape(1, D_MODEL), rw, rb, MOE_TM)

    nt = n // MOE_TM
    nb = -(-(n * TOP_K + nt * N_EXPERTS * (SEG_ALIGN - 1)) // MOE_BLK) + N_EXPERTS
    tables = _block_tables(seg_len[:, :, 0].astype(jnp.int32), nb)
    y_sorted = _experts(tables, x_sorted, expert_w_gu[0], expert_b_gu[0].reshape(N_EXPERTS, 1, 2 * D_FF),
                        expert_w_down[0], expert_b_down[0].reshape(N_EXPERTS, 1, D_MODEL), nt)
    y_p, y_s = _combine(y_sorted, info, x1, p_prompt[0].reshape(n_p, PLE_DIM),
                        p_sample[0].reshape(n_s, PLE_DIM), norm_ple_g[0].reshape(1, D_MODEL),
                        ple_gate_w[0].astype(BF16), ple_w[0].astype(BF16), final_norm_g.reshape(1, D_MODEL), MOE_TM)

    return (y_p.reshape(bp, tp, D_MODEL), y_s.reshape(bs, ts, D_MODEL),
            conv_p[None], gdn_st_p[None], c_p[None], nn_p[None], m_p.reshape(1, bp, H_B),
            conv_s[None], gdn_st_s[None], c_s[None], nn_s[None], m_s.reshape(1, bs, H_B))
```

```python
import functools

import numpy as np
import jax
import jax.numpy as jnp
from jax import lax
from jax.experimental import pallas as pl
from jax.experimental.pallas import tpu as pltpu

F32 = jnp.float32
BF16 = jnp.bfloat16

D_MODEL = 1024
H_A, DK_A, DV_A = 4, 128, 128
H_B, DK_B, DV_B = 4, 64, 128
CONV_W = 4
CONV_CH = H_A * (2 * DK_A + DV_A)
N_EXPERTS = 32
TOP_K = 4
D_FF = 1024
SWIGLU_LIMIT = 7.0
SWIGLU_ALPHA = 1.702
PLE_DIM = 256
EPS = 1e-6
NEG = -1e30
CHUNK = 64

LANE = 128
SUBLANE = 8
GDN_W = CONV_CH + H_A * DV_A
MLP_W = 2 * H_B * DK_B + 2 * H_B * DV_B
N_GATE = 16
PROJ_CHUNK = 512

VMEM_LIMIT = 48 * 1024 * 1024

MOE_TM = 256
MOE_BLK = 512
MOE_DTYPE = F32
SEG_ALIGN = SUBLANE
MOE_CAP = -(-(MOE_TM * TOP_K + N_EXPERTS * (SEG_ALIGN - 1)) // LANE) * LANE

HI = lax.Precision.HIGHEST

_NN = (((1,), (0,)), ((), ()))
_NT = (((1,), (1,)), ((), ()))
_TN = (((0,), (0,)), ((), ()))


def _dot(a, b, dims=_NN):
    return lax.dot_general(a.astype(BF16), b.astype(BF16), dims, preferred_element_type=F32)


def _split3(x):
    hi = x.astype(BF16)
    r = x - hi.astype(F32)
    mid = r.astype(BF16)
    return hi, mid, (r - mid.astype(F32)).astype(BF16)


def _chunk_cumsums(gact, gact_t, tril, triu, cache, key):
    if key not in cache:
        tril_b, triu_b = tril.astype(BF16), triu.astype(BF16)
        cum_c = sum(jnp.dot(tril_b, part, preferred_element_type=F32) for part in _split3(gact))
        cum_r = sum(jnp.dot(part, triu_b, preferred_element_type=F32) for part in _split3(gact_t))
        cache[key] = (cum_c, cum_r)
    return cache[key]


def _rms(x, g):
    return x * lax.rsqrt(jnp.mean(x * x, axis=-1, keepdims=True) + EPS) * g


def _softplus(t):
    return jnp.maximum(t, 0.0) + jnp.log1p(jnp.exp(-jnp.abs(t)))


def _sigmoid(t):
    return 1.0 / (1.0 + jnp.exp(-t))


def _silu(t):
    return t * _sigmoid(t)


def _activate_gates(raw, idx, alog, bias):
    t = raw + bias
    g = -jnp.exp(alog) * _softplus(t)
    beta = _sigmoid(t)
    lf = -_softplus(-t)
    return jnp.where(idx < 4, g, jnp.where(idx < 8, beta, jnp.where(idx < 12, t, lf)))


def _two_segment_specs(tm, width, n_p_tiles):
    return [pl.BlockSpec((tm, width), lambda i: (jnp.minimum(i, n_p_tiles - 1), 0)),
            pl.BlockSpec((tm, width), lambda i: (jnp.maximum(i - n_p_tiles, 0), 0))]


def _for_segment(n_p_tiles, body):
    i = pl.program_id(0)

    @pl.when(i < n_p_tiles)
    def _():
        body(0)

    @pl.when(i >= n_p_tiles)
    def _():
        body(1)


def _gdn_preactivate_stages(raw_ref, xc_ref, cw_ref, out_ref, cnew_ref, first_of_seq):
    tm = raw_ref.shape[0]
    xc_ref[0:SUBLANE, :] = jnp.where(first_of_seq, 0.0, xc_ref[tm:tm + SUBLANE, :])
    xc_ref[SUBLANE:SUBLANE + tm, :] = raw_ref[:, :CONV_CH]
    out_ref[:, CONV_CH:] = _silu(raw_ref[:, CONV_CH:])
    cnew_ref[0] = xc_ref[SUBLANE + tm - (CONV_W - 1):SUBLANE + tm, :]
    yield
    base = SUBLANE - (CONV_W - 1)
    for c0 in range(0, CONV_CH, DK_A):
        conv = xc_ref[base:base + tm, c0:c0 + DK_A] * cw_ref[0:1, c0:c0 + DK_A]
        for j in range(1, CONV_W):
            conv = conv + xc_ref[base + j:base + j + tm, c0:c0 + DK_A] * cw_ref[j:j + 1, c0:c0 + DK_A]
        act = _silu(conv)
        if c0 < H_A * DK_A:
            act = act * (lax.rsqrt(jnp.sum(act * act, axis=-1, keepdims=True) + EPS) * (DK_A ** -0.5))
        elif c0 < 2 * H_A * DK_A:
            act = act * lax.rsqrt(jnp.sum(act * act, axis=-1, keepdims=True) + EPS)
        out_ref[:, c0:c0 + DK_A] = act
        yield


def _inproj_kernel(xp_ref, xs_ref, g_ref, wa_ref, wb_ref, wst_ref, pc_ref, pr_ref, cw_ref,
                   gdnp_ref, gdns_ref, mlp_ref, mls_ref, gatep_ref, gates_ref, gatetp_ref, gatets_ref, cnew_ref,
                   xc_ref, raw_ref, *, n_p_tiles, tiles_per_seq):
    tm = xp_ref.shape[0]
    i = pl.program_id(0)

    @pl.when(i == 0)
    def _():
        xc_ref[...] = jnp.zeros_like(xc_ref)
        raw_ref[...] = jnp.zeros_like(raw_ref)

    def preactivate_previous_tile():
        return _gdn_preactivate_stages(raw_ref, xc_ref, cw_ref, gdnp_ref, cnew_ref, (i - 1) % tiles_per_seq == 0)

    def projection_stages(seg):
        x_ref = (xp_ref, xs_ref)[seg]
        gdn_dst, ml_ref = (raw_ref, gdns_ref)[seg], (mlp_ref, mls_ref)[seg]
        gate_ref, gatet_ref = (gatep_ref, gates_ref)[seg], (gatetp_ref, gatets_ref)[seg]
        hn = _rms(x_ref[...], g_ref[...]).astype(BF16)
        yield
        for c0 in range(0, GDN_W, PROJ_CHUNK):
            gdn_dst[:, c0:c0 + PROJ_CHUNK] = jnp.dot(hn, wa_ref[:, c0:c0 + PROJ_CHUNK], preferred_element_type=F32)
            yield
        for c0 in range(0, MLP_W, PROJ_CHUNK):
            ml_ref[:, c0:c0 + PROJ_CHUNK] = jnp.dot(hn, wb_ref[:, c0:c0 + PROJ_CHUNK], preferred_element_type=F32)
            yield
        raw = jnp.dot(hn, wb_ref[:, MLP_W:], preferred_element_type=F32)
        lane = lax.broadcasted_iota(jnp.int32, (tm, LANE), 1)
        gate_ref[...] = _activate_gates(raw, lane, pc_ref[0:1, :], pc_ref[1:2, :])
        raw_t = lax.dot_general(wst_ref[...], hn, _NT, preferred_element_type=F32)
        row = lax.broadcasted_iota(jnp.int32, (N_GATE, tm), 0)
        gatet_ref[...] = _activate_gates(raw_t, row, pr_ref[:, 0:1], pr_ref[:, 1:2])

    def body(seg):
        if seg == 0:
            _run_interleaved(preactivate_previous_tile(), projection_stages(0))
        else:
            pl.when(i == n_p_tiles)(lambda: _run_interleaved(preactivate_previous_tile()))
            _run_interleaved(projection_stages(1))

    _for_segment(n_p_tiles, body)


def _inproj(xp, xs, g, w_gdn, w_rest, ws_t, pcol, prow, cw, tm, n_seq_p):
    n_p, n_s = xp.shape[0], xs.shape[0]
    npt = n_p // tm
    tiles_per_seq = npt // n_seq_p

    def out2(width):
        return _two_segment_specs(tm, width, npt)

    def shp2(width):
        return [jax.ShapeDtypeStruct((n_p, width), F32), jax.ShapeDtypeStruct((n_s, width), F32)]

    def prev_tile(i):
        return jnp.clip(i - 1, 0, npt - 1)

    return pl.pallas_call(
        functools.partial(_inproj_kernel, n_p_tiles=npt, tiles_per_seq=tiles_per_seq),
        grid=((n_p + n_s) // tm,),
        in_specs=_two_segment_specs(tm, D_MODEL, npt) + [
            pl.BlockSpec((1, D_MODEL), lambda i: (0, 0)),
            pl.BlockSpec((D_MODEL, GDN_W), lambda i: (0, 0)),
            pl.BlockSpec((D_MODEL, MLP_W + LANE), lambda i: (0, 0)),
            pl.BlockSpec((N_GATE, D_MODEL), lambda i: (0, 0)),
            pl.BlockSpec((SUBLANE, LANE), lambda i: (0, 0)),
            pl.BlockSpec((N_GATE, LANE), lambda i: (0, 0)),
            pl.BlockSpec((SUBLANE, CONV_CH), lambda i: (0, 0)),
        ],
        out_specs=[
            pl.BlockSpec((tm, GDN_W), lambda i: (prev_tile(i), 0)),
            pl.BlockSpec((tm, GDN_W), lambda i: (jnp.maximum(i - npt, 0), 0)),
        ] + out2(MLP_W) + out2(LANE) + [
            pl.BlockSpec((N_GATE, tm), lambda i: (0, jnp.minimum(i, npt - 1))),
            pl.BlockSpec((N_GATE, tm), lambda i: (0, jnp.maximum(i - npt, 0))),
            pl.BlockSpec((1, CONV_W - 1, CONV_CH), lambda i: (prev_tile(i) // tiles_per_seq, 0, 0)),
        ],
        out_shape=shp2(GDN_W) + shp2(MLP_W) + shp2(LANE) + [
            jax.ShapeDtypeStruct((N_GATE, n_p), F32), jax.ShapeDtypeStruct((N_GATE, n_s), F32),
            jax.ShapeDtypeStruct((n_seq_p, CONV_W - 1, CONV_CH), F32)],
        scratch_shapes=[pltpu.VMEM((tm + SUBLANE, CONV_CH), F32), pltpu.VMEM((tm, GDN_W), F32)],
        compiler_params=pltpu.CompilerParams(dimension_semantics=("arbitrary",), vmem_limit_bytes=VMEM_LIMIT),
        name="inproj",
    )(xp, xs, g, w_gdn, w_rest, ws_t, pcol, prow, cw)


def _chunk_masks(L):
    ri = lax.broadcasted_iota(jnp.int32, (L, L), 0)
    ci = lax.broadcasted_iota(jnp.int32, (L, L), 1)
    return ri >= ci, ri > ci, ri <= ci


def _run_interleaved(*stage_generators):
    live = list(stage_generators)
    while live:
        for gen in list(live):
            if next(gen, StopIteration) is StopIteration:
                live.remove(gen)


def _gdn_stages(*refs, L, G, has_state, cumsum_cache):
    if has_state:
        (xin_ref, gate_ref, gatet_ref, cw_ref, ng_ref, cst_ref, s0_ref,
         mix_ref, cnew_ref, snew_ref, xc_ref, s_ref) = refs
    else:
        xin_ref, gate_ref, gatet_ref, ng_ref, mix_ref, snew_ref, s_ref = refs
    c = pl.program_id(1)

    @pl.when(c == 0)
    def _():
        if has_state:
            xc_ref[:, 0:SUBLANE, :] = jnp.zeros((G, SUBLANE, CONV_CH), F32)
            xc_ref[:, SUBLANE - (CONV_W - 1):SUBLANE, :] = cst_ref[...]
            s_ref[...] = s0_ref[...]
        else:
            s_ref[...] = jnp.zeros_like(s_ref)

    if has_state:
        @pl.when(c > 0)
        def _():
            xc_ref[:, 0:SUBLANE, :] = xc_ref[:, L:L + SUBLANE, :]

    yield
    tril, strict, triu = _chunk_masks(L)
    base = SUBLANE - (CONV_W - 1)

    chains = [(g, h) for g in range(G) for h in range(H_A)]
    s_old = [s_ref[g, h] for g, h in chains]
    if has_state:
        for g in range(G):
            xc_ref[g, SUBLANE:SUBLANE + L, :] = xin_ref[g, :, :CONV_CH]

    q, k, v, beta, gc, gl, decay = [], [], [], [], [], [], []
    for g in range(G):
        if has_state:
            conv = xc_ref[g, base:base + L, :] * cw_ref[0:1, :]
            for j in range(1, CONV_W):
                conv = conv + xc_ref[g, base + j:base + j + L, :] * cw_ref[j:j + 1, :]
            cnew_ref[g] = xc_ref[g, SUBLANE + L - (CONV_W - 1):SUBLANE + L, :]
            act = _silu(conv)
        else:
            act = xin_ref[g, :, :CONV_CH]
        gact = gate_ref[g]
        cum_c, cum_r = _chunk_cumsums(gact, gatet_ref[g, 0], tril, triu, cumsum_cache, g)
        for h in range(H_A):
            q.append(act[:, h * DK_A:(h + 1) * DK_A])
            k.append(act[:, H_A * DK_A + h * DK_A:H_A * DK_A + (h + 1) * DK_A])
            v.append(act[:, 2 * H_A * DK_A + h * DV_A:2 * H_A * DK_A + (h + 1) * DV_A])
            beta.append(gact[:, 4 + h:5 + h])
            gc.append(cum_c[:, h:h + 1])
            gl.append(cum_c[L - 1:L, h:h + 1])
            gr = cum_r[h:h + 1, :]
            decay.append(jnp.where(tril, jnp.exp(jnp.where(tril, cum_c[:, h:h + 1] - gr, 0.0)), 0.0))
        yield

    nc = range(len(chains))
    if has_state:
        qss = [jnp.sum(q[i] * q[i], axis=-1, keepdims=True) for i in nc]
        kss = [jnp.sum(k[i] * k[i], axis=-1, keepdims=True) for i in nc]
        q = [q[i] * (lax.rsqrt(qss[i] + EPS) * (DK_A ** -0.5)) for i in nc]
        k = [k[i] * lax.rsqrt(kss[i] + EPS) for i in nc]
    kb = [k[i] * beta[i] for i in nc]
    egc = [jnp.exp(gc[i]) for i in nc]
    yield
    kk = [_dot(kb[i], k[i], _NT) for i in nc]
    yield
    qk = [_dot(q[i], k[i], _NT) for i in nc]
    yield
    eye = (lax.broadcasted_iota(jnp.int32, (L, L), 0) == lax.broadcasted_iota(jnp.int32, (L, L), 1)).astype(F32)
    pw = [-jnp.where(strict, kk[i] * decay[i], 0.0) for i in nc]
    t_inv = [eye + pw[i] for i in nc]
    span = 2
    while span < L:
        yield
        pw = [_dot(pw[i], pw[i]) for i in nc]
        yield
        t_inv = [t_inv[i] + _dot(t_inv[i], pw[i]) for i in nc]
        span *= 2
    yield
    sol = [_dot(t_inv[i], jnp.concatenate([v[i] * beta[i], kb[i] * egc[i]], axis=-1)) for i in nc]
    yield
    qs = [_dot(q[i] * egc[i], s_old[i]) for i in nc]
    yield
    ws = [_dot(sol[i][:, DV_A:], s_old[i]) for i in nc]
    v_new = [sol[i][:, :DV_A] - ws[i] for i in nc]
    yield
    o = [qs[i] + _dot(jnp.where(tril, qk[i] * decay[i], 0.0), v_new[i]) for i in nc]
    yield
    s_new = [s_old[i] * jnp.exp(gl[i]) + _dot(k[i] * jnp.exp(gl[i] - gc[i]), v_new[i], _TN) for i in nc]
    yield
    ms = [jnp.mean(o[i] * o[i], axis=-1, keepdims=True) for i in nc]
    on = [o[i] * lax.rsqrt(ms[i] + EPS) for i in nc]
    yield
    for i, (g, h) in enumerate(chains):
        z = xin_ref[g, :, CONV_CH + h * DV_A:CONV_CH + (h + 1) * DV_A]
        out = on[i] * ng_ref[...] * (_silu(z) if has_state else z)
        mix_ref[g, :, h * DV_A:(h + 1) * DV_A] = out.astype(mix_ref.dtype)
    yield
    for i, (g, h) in enumerate(chains):
        s_ref[g, h] = s_new[i]
        snew_ref[g, h] = s_new[i]


def _mlstm_stages(*refs, L, G, has_state, cumsum_cache):
    if has_state:
        (xin_ref, gate_ref, gatet_ref, ng_ref, c0_ref, n0_ref, m0_ref,
         mix_ref, cnew_ref, nnew_ref, mnew_ref, c_ref, n_ref, m_ref) = refs
    else:
        (xin_ref, gate_ref, gatet_ref, ng_ref,
         mix_ref, cnew_ref, nnew_ref, mnew_ref, c_ref, n_ref, m_ref) = refs
    c = pl.program_id(1)

    @pl.when(c == 0)
    def _():
        c_ref[...] = jnp.zeros_like(c_ref)
        n_ref[...] = jnp.zeros_like(n_ref)
        m_ref[...] = jnp.zeros_like(m_ref)
        if has_state:
            for h in range(H_B):
                off = (h % 2) * DK_B
                c_ref[:, h, off:off + DK_B, :] = c0_ref[:, h]
                n_ref[:, h:h + 1, off:off + DK_B] = n0_ref[:, h:h + 1, :]
            m_ref[:, 0:1, 0:H_B] = m0_ref[...]

    yield
    tril, _, triu = _chunk_masks(L)

    chains = [(g, h) for g in range(G) for h in range(H_B)]
    nc = range(len(chains))
    c_old = [c_ref[g, h] for g, h in chains]
    n_old = [n_ref[g, h:h + 1, :] for g, h in chains]
    m_old = [m_ref[g, 0:1, h:h + 1] for g, h in chains]

    k0, v0 = H_B * DK_B, 2 * H_B * DK_B
    low_half = lax.broadcasted_iota(jnp.int32, (L, LANE), 1) < DK_B

    def own_lanes(pair, h):
        return jnp.where(low_half if h % 2 == 0 else jnp.logical_not(low_half), pair, 0.0)

    q = [own_lanes(xin_ref[g, :, (h // 2) * LANE:(h // 2 + 1) * LANE], h) * (DK_B ** -0.5) for g, h in chains]
    k = [own_lanes(xin_ref[g, :, k0 + (h // 2) * LANE:k0 + (h // 2 + 1) * LANE], h) for g, h in chains]
    v = [xin_ref[g, :, v0 + h * DV_B:v0 + (h + 1) * DV_B] for g, h in chains]
    ig_c, b_c, b_last, d_log = [], [], [], []
    for g in range(G):
        gact = gate_ref[g]
        gact_t = gatet_ref[g, 0]
        cum_c, cum_r = _chunk_cumsums(gact, gact_t, tril, triu, cumsum_cache, g)
        for h in range(H_B):
            ig_c.append(gact[:, 8 + h:9 + h])
            b_c.append(cum_c[:, 12 + h:13 + h])
            b_last.append(cum_c[L - 1:L, 12 + h:13 + h])
            d_log.append(jnp.where(tril, cum_c[:, 12 + h:13 + h] - cum_r[12 + h:13 + h, :]
                                   + gact_t[8 + h:9 + h, :], NEG))
        yield
    qk = [_dot(q[i], k[i], _NT) for i in nc]
    yield
    qc = [_dot(q[i], c_old[i]) for i in nc]
    yield
    inter = [b_c[i] + m_old[i] for i in nc]
    m_t = [jnp.maximum(inter[i], jnp.max(d_log[i], axis=-1, keepdims=True)) for i in nc]
    yield
    s = [qk[i] * jnp.exp(d_log[i] - m_t[i]) for i in nc]
    e_inter = [jnp.exp(inter[i] - m_t[i]) for i in nc]
    yield
    sv = [_dot(s[i], v[i]) for i in nc]
    yield
    m_new = [m_t[i][L - 1:L, :] for i in nc]
    kw = [k[i] * jnp.exp(b_last[i] - b_c[i] + ig_c[i] - m_new[i]) for i in nc]
    f_tot = [jnp.exp(b_last[i] + m_old[i] - m_new[i]) for i in nc]
    yield
    c_new = [f_tot[i] * c_old[i] + _dot(kw[i], v[i], _TN) for i in nc]
    yield
    n_new = [f_tot[i] * n_old[i] + jnp.sum(kw[i], axis=0, keepdims=True) for i in nc]
    qn = [jnp.sum(q[i] * n_old[i], axis=-1, keepdims=True) for i in nc]
    yield
    ssum = [jnp.sum(s[i], axis=-1, keepdims=True) for i in nc]
    yield
    den = [jnp.maximum(jnp.abs(e_inter[i] * qn[i] + ssum[i]), jnp.exp(-m_t[i])) for i in nc]
    hh = [(e_inter[i] * qc[i] + sv[i]) / den[i] for i in nc]
    yield
    ms = [jnp.mean(hh[i] * hh[i], axis=-1, keepdims=True) for i in nc]
    hn = [hh[i] * lax.rsqrt(ms[i] + EPS) for i in nc]
    yield
    for i, (g, h) in enumerate(chains):
        og = xin_ref[g, :, v0 + H_B * DV_B + h * DV_B:v0 + H_B * DV_B + (h + 1) * DV_B]
        mix_ref[g, :, h * DV_B:(h + 1) * DV_B] = (hn[i] * ng_ref[h:h + 1, :] * _sigmoid(og)).astype(mix_ref.dtype)
    yield
    for i, (g, h) in enumerate(chains):
        c_ref[g, h] = c_new[i]
        n_ref[g, h:h + 1, :] = n_new[i]
        m_ref[g, 0:1, h:h + 1] = m_new[i]
        off = (h % 2) * DK_B
        cnew_ref[g, h] = c_new[i][off:off + DK_B, :]
        nnew_ref[g, h:h + 1, :] = n_new[i][:, off:off + DK_B]
        mnew_ref[g, 0:1, h:h + 1] = m_new[i]


def _mixers_kernel(*refs, L, G, has_state):
    if has_state:
        (gdn_ref, gate_ref, gatet_ref, nga_ref, ml_ref, ngb_ref, cw_ref, cst_ref, s0_ref, c0_ref, n0_ref, m0_ref,
         mixa_ref, snew_ref, mixb_ref, cnew_ref, nnew_ref, mnew_ref, convnew_ref,
         s_ref, c_ref, n_ref, m_ref, xc_ref) = refs
        gdn_refs = (gdn_ref, gate_ref, gatet_ref, cw_ref, nga_ref, cst_ref, s0_ref,
                    mixa_ref, convnew_ref, snew_ref, xc_ref, s_ref)
        ml_refs = (ml_ref, gate_ref, gatet_ref, ngb_ref, c0_ref, n0_ref, m0_ref,
                   mixb_ref, cnew_ref, nnew_ref, mnew_ref, c_ref, n_ref, m_ref)
    else:
        (gdn_ref, gate_ref, gatet_ref, nga_ref, ml_ref, ngb_ref,
         mixa_ref, snew_ref, mixb_ref, cnew_ref, nnew_ref, mnew_ref, s_ref, c_ref, n_ref, m_ref) = refs
        gdn_refs = (gdn_ref, gate_ref, gatet_ref, nga_ref, mixa_ref, snew_ref, s_ref)
        ml_refs = (ml_ref, gate_ref, gatet_ref, ngb_ref, mixb_ref, cnew_ref, nnew_ref, mnew_ref, c_ref, n_ref, m_ref)
    cumsum_cache = {}
    _run_interleaved(_gdn_stages(*gdn_refs, L=L, G=G, has_state=has_state, cumsum_cache=cumsum_cache),
                     _mlstm_stages(*ml_refs, L=L, G=G, has_state=has_state, cumsum_cache=cumsum_cache))


def _mixers(gdn_in, ml_in, gates, gates_t, ng_a, ng_b, *, L, G, cw=None, state=None):
    n_seq, T, _ = gdn_in.shape
    n_c = T // L
    has_state = state is not None

    def seq_blk(*tail):
        return pl.BlockSpec((G,) + tail, lambda b, c: (b,) + (0,) * len(tail))

    def tok_blk(width):
        return pl.BlockSpec((G, L, width), lambda b, c: (b, c, 0))

    def seq_shape(*tail):
        return jax.ShapeDtypeStruct((n_seq,) + tail, F32)

    state_specs = [seq_blk(H_A, DK_A, DV_A), seq_blk(H_B, DK_B, DV_B), seq_blk(H_B, DK_B), seq_blk(1, H_B)]
    state_shapes = [seq_shape(H_A, DK_A, DV_A), seq_shape(H_B, DK_B, DV_B), seq_shape(H_B, DK_B), seq_shape(1, H_B)]
    conv_spec, conv_shape = seq_blk(CONV_W - 1, CONV_CH), seq_shape(CONV_W - 1, CONV_CH)
    in_specs = [
        tok_blk(GDN_W), tok_blk(LANE),
        pl.BlockSpec((G, 1, N_GATE, L), lambda b, c: (b, c, 0, 0)),
        pl.BlockSpec((1, DV_A), lambda b, c: (0, 0)),
        tok_blk(MLP_W),
        pl.BlockSpec((H_B, DV_B), lambda b, c: (0, 0)),
    ]
    args = [gdn_in, gates, gates_t, ng_a, ml_in, ng_b]
    out_specs = [tok_blk(H_A * DV_A), state_specs[0], tok_blk(H_B * DV_B)] + state_specs[1:]
    mix_dtype = BF16 if L % (2 * SUBLANE) == 0 else F32
    mix_a = jax.ShapeDtypeStruct((n_seq, T, H_A * DV_A), mix_dtype)
    mix_b = jax.ShapeDtypeStruct((n_seq, T, H_B * DV_B), mix_dtype)
    out_shape = [mix_a, state_shapes[0], mix_b] + state_shapes[1:]
    scratch = [pltpu.VMEM((G, H_A, DK_A, DV_A), F32), pltpu.VMEM((G, H_B, LANE, DV_B), F32),
               pltpu.VMEM((G, SUBLANE, LANE), F32), pltpu.VMEM((G, SUBLANE, LANE), F32)]
    if has_state:
        in_specs += [pl.BlockSpec((SUBLANE, CONV_CH), lambda b, c: (0, 0)), conv_spec] + state_specs
        args += [cw] + list(state)
        out_specs.append(conv_spec)
        out_shape.append(conv_shape)
        scratch.append(pltpu.VMEM((G, L + SUBLANE, CONV_CH), F32))
    return pl.pallas_call(
        functools.partial(_mixers_kernel, L=L, G=G, has_state=has_state),
        grid=(n_seq // G, n_c),
        in_specs=in_specs,
        out_specs=out_specs,
        out_shape=out_shape,
        scratch_shapes=scratch,
        compiler_params=pltpu.CompilerParams(dimension_semantics=("parallel", "arbitrary"),
                                             vmem_limit_bytes=VMEM_LIMIT),
        name=f"mixers_L{L}",
    )(*args)


def _outproj_kernel(xp_ref, xs_ref, map_ref, mas_ref, mbp_ref, mbs_ref, wo_ref, g_ref, rw_ref, rb_ref,
                    x1_ref, xsort_ref, info_ref, cpad_ref, *, n_p_tiles):
    half = H_A * DV_A
    tm = xp_ref.shape[0]

    def body(seg):
        x_ref, ma_ref, mb_ref = (xp_ref, xs_ref)[seg], (map_ref, mas_ref)[seg], (mbp_ref, mbs_ref)[seg]
        x1 = (x_ref[...] + jnp.dot(ma_ref[...].astype(BF16), wo_ref[:half, :], preferred_element_type=F32)
              + jnp.dot(mb_ref[...].astype(BF16), wo_ref[half:, :], preferred_element_type=F32))
        x1_ref[...] = x1
        hn = _rms(x1, g_ref[...])
        hn_hi = hn.astype(BF16)
        hn_lo = (hn - hn_hi.astype(F32)).astype(BF16)
        logits = (jnp.dot(hn_hi, rw_ref[0], preferred_element_type=F32)
                  + jnp.dot(hn_hi, rw_ref[1], preferred_element_type=F32)
                  + jnp.dot(hn_lo, rw_ref[0], preferred_element_type=F32)) + rb_ref[...]

        vals = logits.T[:N_EXPERTS, :]
        e_iota = lax.broadcasted_iota(jnp.int32, (N_EXPERTS, tm), 0)
        sels, tops = [], []
        for _ in range(TOP_K):
            m = jnp.max(vals, axis=0, keepdims=True)
            first = jnp.min(jnp.where(vals == m, e_iota, N_EXPERTS), axis=0, keepdims=True)
            sel = e_iota == first
            vals = jnp.where(sel, -jnp.inf, vals)
            sels.append(sel)
            tops.append(m)
        ex = [jnp.exp(t - tops[0]) for t in tops]
        den = ex[0] + ex[1] + ex[2] + ex[3]
        gates = [e / den for e in ex]
        mask = sels[0].astype(F32) + sels[1].astype(F32) + sels[2].astype(F32) + sels[3].astype(F32)
        ri = lax.broadcasted_iota(jnp.int32, (tm, tm), 0)
        ci = lax.broadcasted_iota(jnp.int32, (tm, tm), 1)
        rank = _dot(mask, (ri < ci).astype(F32))
        cnt = jnp.sum(mask, axis=1, keepdims=True)
        cpad = jnp.ceil(cnt * (1.0 / SEG_ALIGN)) * SEG_ALIGN
        cpad_b = jnp.broadcast_to(cpad, (N_EXPERTS, tm))
        er = lax.broadcasted_iota(jnp.int32, (N_EXPERTS, N_EXPERTS), 0)
        ec = lax.broadcasted_iota(jnp.int32, (N_EXPERTS, N_EXPERTS), 1)
        seg_off = _dot((er > ec).astype(F32), cpad_b)
        pos = seg_off + rank
        q = [jnp.sum(jnp.where(s, pos, 0.0), axis=0, keepdims=True) for s in sels]

        j_iota = lax.broadcasted_iota(jnp.int32, (MOE_CAP, tm), 0).astype(F32)
        perm = jnp.zeros((MOE_CAP, tm), F32)
        for kk in range(TOP_K):
            perm = jnp.where(j_iota == q[kk], 1.0, perm)
        xsorted = _dot(perm, hn)
        xsort_ref[...] = xsorted.astype(MOE_DTYPE)

        r_iota = lax.broadcasted_iota(jnp.int32, (LANE, tm), 0)
        info = jnp.zeros((LANE, tm), F32)
        for kk in range(TOP_K):
            info = jnp.where(r_iota == kk, q[kk], info)
            info = jnp.where(r_iota == TOP_K + kk, gates[kk], info)
        info_ref[...] = info.T
        cpad_ref[0] = cpad_b[:, :LANE]

    _for_segment(n_p_tiles, body)


def _outproj(xp, xs, ma_p, ma_s, mb_p, mb_s, w_out, g, rw, rb, tm):
    n_p, n_s = xp.shape[0], xs.shape[0]
    n = n_p + n_s
    nt = n // tm
    npt = n_p // tm
    half = H_A * DV_A
    return pl.pallas_call(
        functools.partial(_outproj_kernel, n_p_tiles=npt),
        grid=(nt,),
        in_specs=_two_segment_specs(tm, D_MODEL, npt) + _two_segment_specs(tm, half, npt)
        + _two_segment_specs(tm, half, npt) + [
            pl.BlockSpec((D_MODEL, D_MODEL), lambda i: (0, 0)),
            pl.BlockSpec((1, D_MODEL), lambda i: (0, 0)),
            pl.BlockSpec((2, D_MODEL, LANE), lambda i: (0, 0, 0)),
            pl.BlockSpec((1, LANE), lambda i: (0, 0)),
        ],
        out_specs=[
            pl.BlockSpec((tm, D_MODEL), lambda i: (i, 0)),
            pl.BlockSpec((MOE_CAP, D_MODEL), lambda i: (i, 0)),
            pl.BlockSpec((tm, LANE), lambda i: (i, 0)),
            pl.BlockSpec((1, N_EXPERTS, LANE), lambda i: (i, 0, 0)),
        ],
        out_shape=[
            jax.ShapeDtypeStruct((n, D_MODEL), F32),
            jax.ShapeDtypeStruct((nt * MOE_CAP, D_MODEL), MOE_DTYPE),
            jax.ShapeDtypeStruct((n, LANE), F32),
            jax.ShapeDtypeStruct((nt, N_EXPERTS, LANE), F32),
        ],
        compiler_params=pltpu.CompilerParams(dimension_semantics=("arbitrary",), vmem_limit_bytes=VMEM_LIMIT),
        name="outproj",
    )(xp, xs, ma_p, ma_s, mb_p, mb_s, w_out, g, rw, rb)


def _expert_kernel(be_ref, bj_ref, tf_ref, tl_ref, cov_ref, nu_ref, vt_ref, ct_ref, lt_ref, nx_ref, ws_ref,
                   xs_hbm, wgu_hbm, bgu_ref, wd_hbm, bd_ref, ys_hbm,
                   xbuf, ybuf, gsem, ssem, wgu_st, wd_st, wsem, wgu_bf, wd_bf, *, nt):
    b = pl.program_id(0)
    n_used = nu_ref[0]
    slot = b % 2

    def start_pieces(bb, copy, s):
        e = be_ref[bb]
        base = bj_ref[bb] * MOE_BLK

        def body(t, carry):
            k = e * nt + t
            lo = jnp.maximum(vt_ref[k], base)
            ln = jnp.minimum(ct_ref[k], base + MOE_BLK) - lo

            @pl.when(ln > 0)
            def _():
                copy(s, pl.multiple_of(lt_ref[k] + lo, SEG_ALIGN), pl.multiple_of(lo - base, SEG_ALIGN),
                     pl.multiple_of(ln, SEG_ALIGN)).start()
            return carry

        lax.fori_loop(tf_ref[bb], tl_ref[bb] + 1, body, 0)

    def weight_copies(e):
        return (pltpu.make_async_copy(wgu_hbm.at[e], wgu_st, wsem.at[0]),
                pltpu.make_async_copy(wd_hbm.at[e], wd_st, wsem.at[1]))

    def cast_weights(p):
        wgu_bf[p] = wgu_st[...].astype(BF16)
        wd_bf[p] = wd_st[...].astype(BF16)

    def gather_copy(s, src, dst, size):
        return pltpu.make_async_copy(xs_hbm.at[pl.ds(src, size)], xbuf.at[s, pl.ds(dst, size)], gsem.at[s])

    def scatter_copy(s, src, dst, size):
        return pltpu.make_async_copy(ybuf.at[s, pl.ds(dst, size)], ys_hbm.at[pl.ds(src, size)], ssem.at[s])

    def wait_rows(count, copy, s):
        @pl.when(count > 0)
        def _():
            copy(s, 0, 0, pl.multiple_of(count, SEG_ALIGN)).wait()

    @pl.when(b == 0)
    def _():
        xbuf[...] = jnp.zeros_like(xbuf)
        start_pieces(0, gather_copy, 0)

    @pl.when(b + 1 < n_used)
    def _():
        start_pieces(b + 1, gather_copy, 1 - slot)

    @pl.when(b < n_used)
    def _():
        e = be_ref[b]
        first = jnp.logical_or(b == 0, be_ref[jnp.maximum(b - 1, 0)] != e)
        last = jnp.logical_or(b == n_used - 1, be_ref[jnp.minimum(b + 1, n_used - 1)] != e)
        has_next = nx_ref[b] < N_EXPERTS
        p = ws_ref[b]

        @pl.when(b == 0)
        def _():
            for cp in weight_copies(e):
                cp.start()
            for cp in weight_copies(e):
                cp.wait()
            cast_weights(p)

        @pl.when(jnp.logical_and(first, has_next))
        def _():
            for cp in weight_copies(nx_ref[b]):
                cp.start()

        wait_rows(cov_ref[b], gather_copy, slot)

        @pl.when(b >= 2)
        def _():
            wait_rows(cov_ref[jnp.maximum(b - 2, 0)], scatter_copy, slot)

        def expert_mlp(rows):
            hgu = jnp.dot(xbuf[slot, :rows].astype(BF16), wgu_bf[p], preferred_element_type=F32) + bgu_ref[0]
            gate = jnp.minimum(hgu[:, :D_FF], SWIGLU_LIMIT)
            up = jnp.clip(hgu[:, D_FF:], -SWIGLU_LIMIT, SWIGLU_LIMIT)
            act = (up + 1.0) * gate * _sigmoid(SWIGLU_ALPHA * gate)
            y = jnp.dot(act.astype(BF16), wd_bf[p], preferred_element_type=F32) + bd_ref[0]
            ybuf[slot, :rows] = y.astype(MOE_DTYPE)

        quarter = MOE_BLK // 4
        for nq in range(1, 5):
            pl.when(jnp.logical_and(cov_ref[b] > (nq - 1) * quarter, cov_ref[b] <= nq * quarter))(
                functools.partial(expert_mlp, nq * quarter))

        @pl.when(jnp.logical_and(last, has_next))
        def _():
            for cp in weight_copies(nx_ref[b]):
                cp.wait()
            cast_weights(1 - p)

        start_pieces(b, scatter_copy, slot)

        @pl.when(b == n_used - 1)
        def _():
            wait_rows(cov_ref[b], scatter_copy, slot)
            wait_rows(jnp.where(b >= 1, cov_ref[jnp.maximum(b - 1, 0)], 0), scatter_copy, 1 - slot)


def _experts(tables, xs, w_gu, b_gu, w_down, b_down, nt):
    nb = tables[0].shape[0]

    def bias_blk(b, *t):
        return (t[0][jnp.minimum(b, t[5][0] - 1)], 0, 0)

    grid_spec = pltpu.PrefetchScalarGridSpec(
        num_scalar_prefetch=len(tables),
        grid=(nb,),
        in_specs=[
            pl.BlockSpec(memory_space=pl.ANY),
            pl.BlockSpec(memory_space=pl.ANY),
            pl.BlockSpec((1, 1, 2 * D_FF), bias_blk),
            pl.BlockSpec(memory_space=pl.ANY),
            pl.BlockSpec((1, 1, D_MODEL), bias_blk),
        ],
        out_specs=pl.BlockSpec(memory_space=pl.ANY),
        scratch_shapes=[
            pltpu.VMEM((2, MOE_BLK, D_MODEL), MOE_DTYPE),
            pltpu.VMEM((2, MOE_BLK, D_MODEL), MOE_DTYPE),
            pltpu.SemaphoreType.DMA((2,)),
            pltpu.SemaphoreType.DMA((2,)),
            pltpu.VMEM((D_MODEL, 2 * D_FF), F32),
            pltpu.VMEM((D_FF, D_MODEL), F32),
            pltpu.SemaphoreType.DMA((2,)),
            pltpu.VMEM((2, D_MODEL, 2 * D_FF), BF16),
            pltpu.VMEM((2, D_FF, D_MODEL), BF16),
        ],
    )
    return pl.pallas_call(
        functools.partial(_expert_kernel, nt=nt),
        grid_spec=grid_spec,
        out_shape=jax.ShapeDtypeStruct(xs.shape, xs.dtype),
        input_output_aliases={len(tables): 0},
        compiler_params=pltpu.CompilerParams(dimension_semantics=("arbitrary",), vmem_limit_bytes=VMEM_LIMIT),
        name="experts",
    )(*tables, xs, w_gu, b_gu, w_down, b_down)


def _combine_kernel(ys_ref, info_ref, x1_ref, pp_ref, ps_ref, gple_ref, wg_ref, wp_ref, gfin_ref,
                    outp_ref, outs_ref, *, n_p_tiles):
    tm = x1_ref.shape[0]

    def body(seg):
        p_ref, out_ref = (pp_ref, ps_ref)[seg], (outp_ref, outs_ref)[seg]
        info = info_ref[...]
        j_iota = lax.broadcasted_iota(jnp.int32, (tm, MOE_CAP), 1).astype(F32)
        gmat = jnp.zeros((tm, MOE_CAP), F32)
        for kk in range(TOP_K):
            gmat = jnp.where(j_iota == info[:, kk:kk + 1], info[:, TOP_K + kk:TOP_K + kk + 1], gmat)
        x2 = x1_ref[...] + jnp.dot(gmat.astype(BF16), ys_ref[...].astype(BF16), preferred_element_type=F32)
        hn = _rms(x2, gple_ref[...]).astype(BF16)
        gate = _sigmoid(jnp.dot(hn, wg_ref[...], preferred_element_type=F32))
        pe = jnp.dot(p_ref[...].astype(BF16), wp_ref[...], preferred_element_type=F32)
        x3 = x2 + gate * pe
        out_ref[...] = _rms(x3, gfin_ref[...])

    _for_segment(n_p_tiles, body)


def _combine(ys, info, x1, pp, ps, g_ple, w_gate, w_p, g_fin, tm):
    n_p, n_s = pp.shape[0], ps.shape[0]
    n = n_p + n_s
    nt = n // tm
    npt = n_p // tm
    return pl.pallas_call(
        functools.partial(_combine_kernel, n_p_tiles=npt),
        grid=(nt,),
        in_specs=[
            pl.BlockSpec((MOE_CAP, D_MODEL), lambda i: (i, 0)),
            pl.BlockSpec((tm, LANE), lambda i: (i, 0)),
            pl.BlockSpec((tm, D_MODEL), lambda i: (i, 0)),
        ] + _two_segment_specs(tm, PLE_DIM, npt) + [
            pl.BlockSpec((1, D_MODEL), lambda i: (0, 0)),
            pl.BlockSpec((D_MODEL, D_MODEL), lambda i: (0, 0)),
            pl.BlockSpec((PLE_DIM, D_MODEL), lambda i: (0, 0)),
            pl.BlockSpec((1, D_MODEL), lambda i: (0, 0)),
        ],
        out_specs=_two_segment_specs(tm, D_MODEL, npt),
        out_shape=[jax.ShapeDtypeStruct((n_p, D_MODEL), F32), jax.ShapeDtypeStruct((n_s, D_MODEL), F32)],
        compiler_params=pltpu.CompilerParams(dimension_semantics=("arbitrary",), vmem_limit_bytes=VMEM_LIMIT),
        name="combine",
    )(ys, info, x1, pp, ps, g_ple, w_gate, w_p, g_fin)


def _block_tables(seg_len, nb):
    nt = seg_len.shape[0]
    seg_off = jnp.cumsum(seg_len, axis=1) - seg_len
    seg_end = jnp.cumsum(seg_len, axis=0).T
    seg_start = seg_end - seg_len.T
    n_rows = seg_end[:, -1]
    n_blk = (n_rows + MOE_BLK - 1) // MOE_BLK
    blk_end = jnp.cumsum(n_blk)
    b = jnp.arange(nb, dtype=jnp.int32)
    block_e = jnp.minimum(jnp.sum((blk_end[None, :] <= b[:, None]).astype(jnp.int32), axis=1), N_EXPERTS - 1)
    idx = jnp.where(n_blk > 0, jnp.arange(N_EXPERTS, dtype=jnp.int32), N_EXPERTS)
    nxt = jnp.concatenate([lax.cummin(idx, axis=0, reverse=True)[1:], jnp.full((1,), N_EXPERTS, jnp.int32)])
    parity = (jnp.cumsum((n_blk > 0).astype(jnp.int32)) - 1) % 2
    per_e = jnp.concatenate([jnp.stack([blk_end - n_blk, n_rows, nxt, parity], axis=1), seg_start, seg_end],
                            axis=1).astype(F32)
    onehot = (block_e[:, None] == jnp.arange(N_EXPERTS, dtype=jnp.int32)[None, :]).astype(F32)
    per_b = jnp.dot(onehot, per_e, precision=HI).astype(jnp.int32)
    block_j = b - per_b[:, 0]
    base = block_j * MOE_BLK
    t_first = jnp.sum((per_b[:, 4 + nt:] <= base[:, None]).astype(jnp.int32), axis=1)
    t_last = jnp.sum((per_b[:, 4:4 + nt] < (base + MOE_BLK)[:, None]).astype(jnp.int32), axis=1) - 1
    cover = jnp.clip(per_b[:, 1] - base, 0, MOE_BLK)
    seg_shift = (jnp.arange(nt, dtype=jnp.int32)[:, None] * MOE_CAP + seg_off).T - seg_start
    tables = (block_e, block_j, t_first, t_last, cover, blk_end[-1:], seg_start.reshape(-1),
              seg_end.reshape(-1), seg_shift.reshape(-1), per_b[:, 2], per_b[:, 3])
    return tuple(t.astype(jnp.int32) for t in tables)


def _rearranged_in_weights(w_in):
    o = np.cumsum([0, CONV_CH, H_A * DV_A, H_A, H_A, H_B * DK_B, H_B * DK_B, H_B * DV_B, H_B * DV_B, H_B, H_B])
    conv_in, z_a, a_a, b_a, q_b, k_b, v_b, o_b, i_b, f_b = (w_in[:, int(o[j]):int(o[j + 1])] for j in range(10))
    small = jnp.concatenate([a_a, b_a, i_b, f_b], axis=1)
    w_gdn = w_in[:, :GDN_W]
    w_rest = jnp.concatenate([q_b, k_b, v_b, o_b, small, jnp.zeros((D_MODEL, LANE - N_GATE), w_in.dtype)], axis=1)
    return w_gdn.astype(BF16), w_rest.astype(BF16), small.T.astype(BF16)


def _gate_params(a_log, dt_bias, i_bias, f_bias):
    z4 = jnp.zeros((4,), F32)
    alog = jnp.concatenate([a_log.astype(F32), z4, z4, z4])
    bias = jnp.concatenate([dt_bias.astype(F32), z4, i_bias.astype(F32), f_bias.astype(F32)])
    pad = jnp.zeros((LANE - N_GATE,), F32)
    pcol = jnp.zeros((SUBLANE, LANE), F32).at[0].set(jnp.concatenate([alog, pad])).at[1].set(
        jnp.concatenate([bias, pad]))
    prow = jnp.zeros((N_GATE, LANE), F32).at[:, 0].set(alog).at[:, 1].set(bias)
    return pcol, prow


def kernel(x_prompt, x_sample, p_prompt, p_sample, state_conv, state_gdn, state_mlstm_c, state_mlstm_n, state_mlstm_m, norm_attn_g, w_in, conv_w, gdn_a_log, gdn_dt_bias, gdn_norm_g, mlstm_i_bias, mlstm_f_bias, mlstm_norm_g, w_out, norm_moe_g, router_w, router_b, expert_w_gu, expert_b_gu, expert_w_down, expert_b_down, norm_ple_g, ple_gate_w, ple_w, final_norm_g):
    bp, tp, _ = x_prompt.shape
    bs, ts, _ = x_sample.shape
    n_p, n_s = bp * tp, bs * ts
    n = n_p + n_s
    lp, ls = min(tp, CHUNK), min(ts, CHUNK)
    tm = 256
    gp = 4 if bp % 4 == 0 else 1
    gs = 16 if bs % 16 == 0 else 1
    assert tp % lp == 0 and ts % ls == 0 and tp % tm == 0 and n_s % tm == 0 and ls % SUBLANE == 0

    xp = x_prompt.reshape(n_p, D_MODEL)
    xs = x_sample.reshape(n_s, D_MODEL)

    w_gdn, w_rest, ws_t = _rearranged_in_weights(w_in[0])
    pcol, prow = _gate_params(gdn_a_log[0], gdn_dt_bias[0], mlstm_i_bias[0], mlstm_f_bias[0])
    cw = jnp.zeros((SUBLANE, CONV_CH), F32).at[:CONV_W].set(conv_w[0].astype(F32))
    gdn_p, gdn_s, ml_p, ml_s, gate_p, gate_s, gatet_p, gatet_s, conv_p = _inproj(
        xp, xs, norm_attn_g[0].reshape(1, D_MODEL), w_gdn, w_rest, ws_t, pcol, prow, cw, tm, bp)
    gt_p = gatet_p.reshape(N_GATE, bp, tp // lp, lp).transpose(1, 2, 0, 3)
    gt_s = gatet_s.reshape(N_GATE, bs, ts // ls, ls).transpose(1, 2, 0, 3)

    ng_a = gdn_norm_g[0].reshape(1, DV_A).astype(F32)
    ng_b = mlstm_norm_g[0].reshape(H_B, DV_B).astype(F32)
    ma_p, gdn_st_p, mb_p, c_p, nn_p, m_p = _mixers(
        gdn_p.reshape(bp, tp, GDN_W), ml_p.reshape(bp, tp, MLP_W), gate_p.reshape(bp, tp, LANE), gt_p, ng_a, ng_b,
        L=lp, G=gp)
    ma_s, gdn_st_s, mb_s, c_s, nn_s, m_s, conv_s = _mixers(
        gdn_s.reshape(bs, ts, GDN_W), ml_s.reshape(bs, ts, MLP_W), gate_s.reshape(bs, ts, LANE), gt_s, ng_a, ng_b,
        L=ls, G=gs, cw=cw,
        state=(state_conv[0], state_gdn[0], state_mlstm_c[0], state_mlstm_n[0], state_mlstm_m[0].reshape(bs, 1, H_B)))
    half = H_A * DV_A

    rw = jnp.zeros((D_MODEL, LANE), F32).at[:, :N_EXPERTS].set(router_w[0])
    rw_hi = rw.astype(BF16)
    rw = jnp.stack([rw_hi, (rw - rw_hi.astype(F32)).astype(BF16)])
    rb = jnp.full((1, LANE), NEG, F32).at[0, :N_EXPERTS].set(router_b[0])
    x1, x_sorted, info, seg_len = _outproj(xp, xs, ma_p.reshape(n_p, half), ma_s.reshape(n_s, half),
                                           mb_p.reshape(n_p, half), mb_s.reshape(n_s, half),
                                           w_out[0].astype(BF16), norm_moe_g[0].reshape(1, D_MODEL), rw, rb, MOE_TM)

    nt = n // MOE_TM
    nb = -(-(n * TOP_K + nt * N_EXPERTS * (SEG_ALIGN - 1)) // MOE_BLK) + N_EXPERTS
    tables = _block_tables(seg_len[:, :, 0].astype(jnp.int32), nb)
    y_sorted = _experts(tables, x_sorted, expert_w_gu[0], expert_b_gu[0].reshape(N_EXPERTS, 1, 2 * D_FF),
                        expert_w_down[0], expert_b_down[0].reshape(N_EXPERTS, 1, D_MODEL), nt)
    y_p, y_s = _combine(y_sorted, info, x1, p_prompt[0].reshape(n_p, PLE_DIM),
                        p_sample[0].reshape(n_s, PLE_DIM), norm_ple_g[0].reshape(1, D_MODEL),
                        ple_gate_w[0].astype(BF16), ple_w[0].astype(BF16), final_norm_g.reshape(1, D_MODEL), MOE_TM)

    return (y_p.reshape(bp, tp, D_MODEL), y_s.reshape(bs, ts, D_MODEL),
            conv_p[None], gdn_st_p[None], c_p[None], nn_p[None], m_p.reshape(1, bp, H_B),
            conv_s[None], gdn_st_s[None], c_s[None], nn_s[None], m_s.reshape(1, bs, H_B))
```

```python
import functools

import numpy as np
import jax
import jax.numpy as jnp
from jax import lax
from jax.experimental import pallas as pl
from jax.experimental.pallas import tpu as pltpu

F32 = jnp.float32
BF16 = jnp.bfloat16

D_MODEL = 1024
H_A, DK_A, DV_A = 4, 128, 128
H_B, DK_B, DV_B = 4, 64, 128
CONV_W = 4
CONV_CH = H_A * (2 * DK_A + DV_A)
N_EXPERTS = 32
TOP_K = 4
D_FF = 1024
SWIGLU_LIMIT = 7.0
SWIGLU_ALPHA = 1.702
PLE_DIM = 256
EPS = 1e-6
NEG = -1e30
CHUNK = 64

LANE = 128
SUBLANE = 8
GDN_W = CONV_CH + H_A * DV_A
MLP_W = 2 * H_B * DK_B + 2 * H_B * DV_B
N_GATE = 16
PROJ_CHUNK = 512

VMEM_LIMIT = 48 * 1024 * 1024

MOE_TM = 256
MOE_BLK = 512
MOE_DTYPE = F32
SEG_ALIGN = SUBLANE
MOE_CAP = -(-(MOE_TM * TOP_K + N_EXPERTS * (SEG_ALIGN - 1)) // LANE) * LANE

HI = lax.Precision.HIGHEST

_NN = (((1,), (0,)), ((), ()))
_NT = (((1,), (1,)), ((), ()))
_TN = (((0,), (0,)), ((), ()))


def _dot(a, b, dims=_NN):
    return lax.dot_general(a.astype(BF16), b.astype(BF16), dims, preferred_element_type=F32)


def _split3(x):
    hi = x.astype(BF16)
    r = x - hi.astype(F32)
    mid = r.astype(BF16)
    return hi, mid, (r - mid.astype(F32)).astype(BF16)


def _chunk_cumsums(gact, gact_t, tril, triu, cache, key):
    if key not in cache:
        tril_b, triu_b = tril.astype(BF16), triu.astype(BF16)
        cum_c = sum(jnp.dot(tril_b, part, preferred_element_type=F32) for part in _split3(gact))
        cum_r = sum(jnp.dot(part, triu_b, preferred_element_type=F32) for part in _split3(gact_t))
        cache[key] = (cum_c, cum_r)
    return cache[key]


def _rms(x, g):
    return x * lax.rsqrt(jnp.mean(x * x, axis=-1, keepdims=True) + EPS) * g


def _softplus(t):
    return jnp.maximum(t, 0.0) + jnp.log1p(jnp.exp(-jnp.abs(t)))


def _sigmoid(t):
    return 1.0 / (1.0 + jnp.exp(-t))


def _silu(t):
    return t * _sigmoid(t)


def _activate_gates(raw, idx, alog, bias):
    t = raw + bias
    g = -jnp.exp(alog) * _softplus(t)
    beta = _sigmoid(t)
    lf = -_softplus(-t)
    return jnp.where(idx < 4, g, jnp.where(idx < 8, beta, jnp.where(idx < 12, t, lf)))


def _two_segment_specs(tm, width, n_p_tiles):
    return [pl.BlockSpec((tm, width), lambda i, *_: (jnp.minimum(i, n_p_tiles - 1), 0)),
            pl.BlockSpec((tm, width), lambda i, *_: (jnp.maximum(i - n_p_tiles, 0), 0))]


def _for_segment(n_p_tiles, body):
    i = pl.program_id(0)

    @pl.when(i < n_p_tiles)
    def _():
        body(0)

    @pl.when(i >= n_p_tiles)
    def _():
        body(1)


def _gdn_preactivate_stages(raw_ref, xc_ref, cw_ref, out_ref, cnew_ref, first_of_seq):
    tm = raw_ref.shape[0]
    xc_ref[0:SUBLANE, :] = jnp.where(first_of_seq, 0.0, xc_ref[tm:tm + SUBLANE, :])
    xc_ref[SUBLANE:SUBLANE + tm, :] = raw_ref[:, :CONV_CH]
    out_ref[:, CONV_CH:] = _silu(raw_ref[:, CONV_CH:])
    cnew_ref[0] = xc_ref[SUBLANE + tm - (CONV_W - 1):SUBLANE + tm, :]
    yield
    base = SUBLANE - (CONV_W - 1)
    for c0 in range(0, CONV_CH, DK_A):
        conv = xc_ref[base:base + tm, c0:c0 + DK_A] * cw_ref[0:1, c0:c0 + DK_A]
        for j in range(1, CONV_W):
            conv = conv + xc_ref[base + j:base + j + tm, c0:c0 + DK_A] * cw_ref[j:j + 1, c0:c0 + DK_A]
        act = _silu(conv)
        if c0 < H_A * DK_A:
            act = act * (lax.rsqrt(jnp.sum(act * act, axis=-1, keepdims=True) + EPS) * (DK_A ** -0.5))
        elif c0 < 2 * H_A * DK_A:
            act = act * lax.rsqrt(jnp.sum(act * act, axis=-1, keepdims=True) + EPS)
        out_ref[:, c0:c0 + DK_A] = act
        yield


def _inproj_kernel(xp_ref, xs_ref, g_ref, wa_ref, wb_ref, wst_ref, pc_ref, pr_ref, cw_ref,
                   gdnp_ref, gdns_ref, mlp_ref, mls_ref, gatep_ref, gates_ref, gatetp_ref, gatets_ref, cnew_ref,
                   xc_ref, raw_ref, *, n_p_tiles, tiles_per_seq):
    tm = xp_ref.shape[0]
    i = pl.program_id(0)

    @pl.when(i == 0)
    def _():
        xc_ref[...] = jnp.zeros_like(xc_ref)
        raw_ref[...] = jnp.zeros_like(raw_ref)

    def preactivate_previous_tile():
        return _gdn_preactivate_stages(raw_ref, xc_ref, cw_ref, gdnp_ref, cnew_ref, (i - 1) % tiles_per_seq == 0)

    def projection_stages(seg):
        x_ref = (xp_ref, xs_ref)[seg]
        gdn_dst, ml_ref = (raw_ref, gdns_ref)[seg], (mlp_ref, mls_ref)[seg]
        gate_ref, gatet_ref = (gatep_ref, gates_ref)[seg], (gatetp_ref, gatets_ref)[seg]
        hn = _rms(x_ref[...], g_ref[...]).astype(BF16)
        yield
        for c0 in range(0, GDN_W, PROJ_CHUNK):
            gdn_dst[:, c0:c0 + PROJ_CHUNK] = jnp.dot(hn, wa_ref[:, c0:c0 + PROJ_CHUNK], preferred_element_type=F32)
            yield
        for c0 in range(0, MLP_W, PROJ_CHUNK):
            ml_ref[:, c0:c0 + PROJ_CHUNK] = jnp.dot(hn, wb_ref[:, c0:c0 + PROJ_CHUNK], preferred_element_type=F32)
            yield
        raw = jnp.dot(hn, wb_ref[:, MLP_W:], preferred_element_type=F32)
        lane = lax.broadcasted_iota(jnp.int32, (tm, LANE), 1)
        gate_ref[...] = _activate_gates(raw, lane, pc_ref[0:1, :], pc_ref[1:2, :])
        raw_t = lax.dot_general(wst_ref[...], hn, _NT, preferred_element_type=F32)
        row = lax.broadcasted_iota(jnp.int32, (N_GATE, tm), 0)
        gatet_ref[...] = _activate_gates(raw_t, row, pr_ref[:, 0:1], pr_ref[:, 1:2])

    def body(seg):
        if seg == 0:
            _run_interleaved(preactivate_previous_tile(), projection_stages(0))
        else:
            pl.when(i == n_p_tiles)(lambda: _run_interleaved(preactivate_previous_tile()))
            _run_interleaved(projection_stages(1))

    _for_segment(n_p_tiles, body)


def _inproj(xp, xs, g, w_gdn, w_rest, ws_t, pcol, prow, cw, tm, n_seq_p):
    n_p, n_s = xp.shape[0], xs.shape[0]
    npt = n_p // tm
    tiles_per_seq = npt // n_seq_p

    def out2(width):
        return _two_segment_specs(tm, width, npt)

    def shp2(width):
        return [jax.ShapeDtypeStruct((n_p, width), F32), jax.ShapeDtypeStruct((n_s, width), F32)]

    def prev_tile(i):
        return jnp.clip(i - 1, 0, npt - 1)

    return pl.pallas_call(
        functools.partial(_inproj_kernel, n_p_tiles=npt, tiles_per_seq=tiles_per_seq),
        grid=((n_p + n_s) // tm,),
        in_specs=_two_segment_specs(tm, D_MODEL, npt) + [
            pl.BlockSpec((1, D_MODEL), lambda i: (0, 0)),
            pl.BlockSpec((D_MODEL, GDN_W), lambda i: (0, 0)),
            pl.BlockSpec((D_MODEL, MLP_W + LANE), lambda i: (0, 0)),
            pl.BlockSpec((N_GATE, D_MODEL), lambda i: (0, 0)),
            pl.BlockSpec((SUBLANE, LANE), lambda i: (0, 0)),
            pl.BlockSpec((N_GATE, LANE), lambda i: (0, 0)),
            pl.BlockSpec((SUBLANE, CONV_CH), lambda i: (0, 0)),
        ],
        out_specs=[
            pl.BlockSpec((tm, GDN_W), lambda i: (prev_tile(i), 0)),
            pl.BlockSpec((tm, GDN_W), lambda i: (jnp.maximum(i - npt, 0), 0)),
        ] + out2(MLP_W) + out2(LANE) + [
            pl.BlockSpec((N_GATE, tm), lambda i: (0, jnp.minimum(i, npt - 1))),
            pl.BlockSpec((N_GATE, tm), lambda i: (0, jnp.maximum(i - npt, 0))),
            pl.BlockSpec((1, CONV_W - 1, CONV_CH), lambda i: (prev_tile(i) // tiles_per_seq, 0, 0)),
        ],
        out_shape=shp2(GDN_W) + shp2(MLP_W) + shp2(LANE) + [
            jax.ShapeDtypeStruct((N_GATE, n_p), F32), jax.ShapeDtypeStruct((N_GATE, n_s), F32),
            jax.ShapeDtypeStruct((n_seq_p, CONV_W - 1, CONV_CH), F32)],
        scratch_shapes=[pltpu.VMEM((tm + SUBLANE, CONV_CH), F32), pltpu.VMEM((tm, GDN_W), F32)],
        compiler_params=pltpu.CompilerParams(dimension_semantics=("arbitrary",), vmem_limit_bytes=VMEM_LIMIT),
        name="inproj",
    )(xp, xs, g, w_gdn, w_rest, ws_t, pcol, prow, cw)


def _chunk_masks(L):
    ri = lax.broadcasted_iota(jnp.int32, (L, L), 0)
    ci = lax.broadcasted_iota(jnp.int32, (L, L), 1)
    return ri >= ci, ri > ci, ri <= ci


def _run_interleaved(*stage_generators):
    live = list(stage_generators)
    while live:
        for gen in list(live):
            if next(gen, StopIteration) is StopIteration:
                live.remove(gen)


def _gdn_stages(*refs, L, G, has_state, cumsum_cache):
    if has_state:
        (xin_ref, gate_ref, gatet_ref, cw_ref, ng_ref, cst_ref, s0_ref,
         mix_ref, cnew_ref, snew_ref, xc_ref, s_ref) = refs
    else:
        xin_ref, gate_ref, gatet_ref, ng_ref, mix_ref, snew_ref, s_ref = refs
    c = pl.program_id(1)

    @pl.when(c == 0)
    def _():
        if has_state:
            xc_ref[:, 0:SUBLANE, :] = jnp.zeros((G, SUBLANE, CONV_CH), F32)
            xc_ref[:, SUBLANE - (CONV_W - 1):SUBLANE, :] = cst_ref[...]
            s_ref[...] = s0_ref[...]
        else:
            s_ref[...] = jnp.zeros_like(s_ref)

    if has_state:
        @pl.when(c > 0)
        def _():
            xc_ref[:, 0:SUBLANE, :] = xc_ref[:, L:L + SUBLANE, :]

    yield
    tril, strict, triu = _chunk_masks(L)
    base = SUBLANE - (CONV_W - 1)

    chains = [(g, h) for g in range(G) for h in range(H_A)]
    s_old = [s_ref[g, h] for g, h in chains]
    if has_state:
        for g in range(G):
            xc_ref[g, SUBLANE:SUBLANE + L, :] = xin_ref[g, :, :CONV_CH]

    q, k, v, beta, gc, gl, decay = [], [], [], [], [], [], []
    for g in range(G):
        if has_state:
            conv = xc_ref[g, base:base + L, :] * cw_ref[0:1, :]
            for j in range(1, CONV_W):
                conv = conv + xc_ref[g, base + j:base + j + L, :] * cw_ref[j:j + 1, :]
            cnew_ref[g] = xc_ref[g, SUBLANE + L - (CONV_W - 1):SUBLANE + L, :]
            act = _silu(conv)
        else:
            act = xin_ref[g, :, :CONV_CH]
        gact = gate_ref[g]
        cum_c, cum_r = _chunk_cumsums(gact, gatet_ref[g, 0], tril, triu, cumsum_cache, g)
        for h in range(H_A):
            q.append(act[:, h * DK_A:(h + 1) * DK_A])
            k.append(act[:, H_A * DK_A + h * DK_A:H_A * DK_A + (h + 1) * DK_A])
            v.append(act[:, 2 * H_A * DK_A + h * DV_A:2 * H_A * DK_A + (h + 1) * DV_A])
            beta.append(gact[:, 4 + h:5 + h])
            gc.append(cum_c[:, h:h + 1])
            gl.append(cum_c[L - 1:L, h:h + 1])
            gr = cum_r[h:h + 1, :]
            decay.append(jnp.where(tril, jnp.exp(jnp.where(tril, cum_c[:, h:h + 1] - gr, 0.0)), 0.0))
        yield

    nc = range(len(chains))
    if has_state:
        qss = [jnp.sum(q[i] * q[i], axis=-1, keepdims=True) for i in nc]
        kss = [jnp.sum(k[i] * k[i], axis=-1, keepdims=True) for i in nc]
        q = [q[i] * (lax.rsqrt(qss[i] + EPS) * (DK_A ** -0.5)) for i in nc]
        k = [k[i] * lax.rsqrt(kss[i] + EPS) for i in nc]
    kb = [k[i] * beta[i] for i in nc]
    egc = [jnp.exp(gc[i]) for i in nc]
    yield
    kk = [_dot(kb[i], k[i], _NT) for i in nc]
    yield
    qk = [_dot(q[i], k[i], _NT) for i in nc]
    yield
    eye = (lax.broadcasted_iota(jnp.int32, (L, L), 0) == lax.broadcasted_iota(jnp.int32, (L, L), 1)).astype(F32)
    pw = [-jnp.where(strict, kk[i] * decay[i], 0.0) for i in nc]
    t_inv = [eye + pw[i] for i in nc]
    span = 2
    while span < L:
        yield
        pw = [_dot(pw[i], pw[i]) for i in nc]
        yield
        t_inv = [t_inv[i] + _dot(t_inv[i], pw[i]) for i in nc]
        span *= 2
    yield
    sol = [_dot(t_inv[i], jnp.concatenate([v[i] * beta[i], kb[i] * egc[i]], axis=-1)) for i in nc]
    yield
    qs = [_dot(q[i] * egc[i], s_old[i]) for i in nc]
    yield
    ws = [_dot(sol[i][:, DV_A:], s_old[i]) for i in nc]
    v_new = [sol[i][:, :DV_A] - ws[i] for i in nc]
    yield
    o = [qs[i] + _dot(jnp.where(tril, qk[i] * decay[i], 0.0), v_new[i]) for i in nc]
    yield
    s_new = [s_old[i] * jnp.exp(gl[i]) + _dot(k[i] * jnp.exp(gl[i] - gc[i]), v_new[i], _TN) for i in nc]
    yield
    ms = [jnp.mean(o[i] * o[i], axis=-1, keepdims=True) for i in nc]
    on = [o[i] * lax.rsqrt(ms[i] + EPS) for i in nc]
    yield
    for i, (g, h) in enumerate(chains):
        z = xin_ref[g, :, CONV_CH + h * DV_A:CONV_CH + (h + 1) * DV_A]
        out = on[i] * ng_ref[...] * (_silu(z) if has_state else z)
        mix_ref[g, :, h * DV_A:(h + 1) * DV_A] = out.astype(mix_ref.dtype)
    yield
    for i, (g, h) in enumerate(chains):
        s_ref[g, h] = s_new[i]
        snew_ref[g, h] = s_new[i]


def _mlstm_stages(*refs, L, G, has_state, cumsum_cache):
    if has_state:
        (xin_ref, gate_ref, gatet_ref, ng_ref, c0_ref, n0_ref, m0_ref,
         mix_ref, cnew_ref, nnew_ref, mnew_ref, c_ref, n_ref, m_ref) = refs
    else:
        (xin_ref, gate_ref, gatet_ref, ng_ref,
         mix_ref, cnew_ref, nnew_ref, mnew_ref, c_ref, n_ref, m_ref) = refs
    c = pl.program_id(1)

    @pl.when(c == 0)
    def _():
        c_ref[...] = jnp.zeros_like(c_ref)
        n_ref[...] = jnp.zeros_like(n_ref)
        m_ref[...] = jnp.zeros_like(m_ref)
        if has_state:
            for h in range(H_B):
                off = (h % 2) * DK_B
                c_ref[:, h, off:off + DK_B, :] = c0_ref[:, h]
                n_ref[:, h:h + 1, off:off + DK_B] = n0_ref[:, h:h + 1, :]
            m_ref[:, 0:1, 0:H_B] = m0_ref[...]

    yield
    tril, _, triu = _chunk_masks(L)

    chains = [(g, h) for g in range(G) for h in range(H_B)]
    nc = range(len(chains))
    c_old = [c_ref[g, h] for g, h in chains]
    n_old = [n_ref[g, h:h + 1, :] for g, h in chains]
    m_old = [m_ref[g, 0:1, h:h + 1] for g, h in chains]

    k0, v0 = H_B * DK_B, 2 * H_B * DK_B
    low_half = lax.broadcasted_iota(jnp.int32, (L, LANE), 1) < DK_B

    def own_lanes(pair, h):
        return jnp.where(low_half if h % 2 == 0 else jnp.logical_not(low_half), pair, 0.0)

    q = [own_lanes(xin_ref[g, :, (h // 2) * LANE:(h // 2 + 1) * LANE], h) * (DK_B ** -0.5) for g, h in chains]
    k = [own_lanes(xin_ref[g, :, k0 + (h // 2) * LANE:k0 + (h // 2 + 1) * LANE], h) for g, h in chains]
    v = [xin_ref[g, :, v0 + h * DV_B:v0 + (h + 1) * DV_B] for g, h in chains]
    ig_c, b_c, b_last, d_log = [], [], [], []
    for g in range(G):
        gact = gate_ref[g]
        gact_t = gatet_ref[g, 0]
        cum_c, cum_r = _chunk_cumsums(gact, gact_t, tril, triu, cumsum_cache, g)
        for h in range(H_B):
            ig_c.append(gact[:, 8 + h:9 + h])
            b_c.append(cum_c[:, 12 + h:13 + h])
            b_last.append(cum_c[L - 1:L, 12 + h:13 + h])
            d_log.append(jnp.where(tril, cum_c[:, 12 + h:13 + h] - cum_r[12 + h:13 + h, :]
                                   + gact_t[8 + h:9 + h, :], NEG))
        yield
    qk = [_dot(q[i], k[i], _NT) for i in nc]
    yield
    qc = [_dot(q[i], c_old[i]) for i in nc]
    yield
    inter = [b_c[i] + m_old[i] for i in nc]
    m_t = [jnp.maximum(inter[i], jnp.max(d_log[i], axis=-1, keepdims=True)) for i in nc]
    yield
    s = [qk[i] * jnp.exp(d_log[i] - m_t[i]) for i in nc]
    e_inter = [jnp.exp(inter[i] - m_t[i]) for i in nc]
    yield
    sv = [_dot(s[i], v[i]) for i in nc]
    yield
    m_new = [m_t[i][L - 1:L, :] for i in nc]
    kw = [k[i] * jnp.exp(b_last[i] - b_c[i] + ig_c[i] - m_new[i]) for i in nc]
    f_tot = [jnp.exp(b_last[i] + m_old[i] - m_new[i]) for i in nc]
    yield
    c_new = [f_tot[i] * c_old[i] + _dot(kw[i], v[i], _TN) for i in nc]
    yield
    n_new = [f_tot[i] * n_old[i] + jnp.sum(kw[i], axis=0, keepdims=True) for i in nc]
    qn = [jnp.sum(q[i] * n_old[i], axis=-1, keepdims=True) for i in nc]
    yield
    ssum = [jnp.sum(s[i], axis=-1, keepdims=True) for i in nc]
    yield
    den = [jnp.maximum(jnp.abs(e_inter[i] * qn[i] + ssum[i]), jnp.exp(-m_t[i])) for i in nc]
    hh = [(e_inter[i] * qc[i] + sv[i]) / den[i] for i in nc]
    yield
    ms = [jnp.mean(hh[i] * hh[i], axis=-1, keepdims=True) for i in nc]
    hn = [hh[i] * lax.rsqrt(ms[i] + EPS) for i in nc]
    yield
    for i, (g, h) in enumerate(chains):
        og = xin_ref[g, :, v0 + H_B * DV_B + h * DV_B:v0 + H_B * DV_B + (h + 1) * DV_B]
        mix_ref[g, :, h * DV_B:(h + 1) * DV_B] = (hn[i] * ng_ref[h:h + 1, :] * _sigmoid(og)).astype(mix_ref.dtype)
    yield
    for i, (g, h) in enumerate(chains):
        c_ref[g, h] = c_new[i]
        n_ref[g, h:h + 1, :] = n_new[i]
        m_ref[g, 0:1, h:h + 1] = m_new[i]
        off = (h % 2) * DK_B
        cnew_ref[g, h] = c_new[i][off:off + DK_B, :]
        nnew_ref[g, h:h + 1, :] = n_new[i][:, off:off + DK_B]
        mnew_ref[g, 0:1, h:h + 1] = m_new[i]


def _mixers_kernel(*refs, L, G, has_state):
    if has_state:
        (gdn_ref, gate_ref, gatet_ref, nga_ref, ml_ref, ngb_ref, cw_ref, cst_ref, s0_ref, c0_ref, n0_ref, m0_ref,
         mixa_ref, snew_ref, mixb_ref, cnew_ref, nnew_ref, mnew_ref, convnew_ref,
         s_ref, c_ref, n_ref, m_ref, xc_ref) = refs
        gdn_refs = (gdn_ref, gate_ref, gatet_ref, cw_ref, nga_ref, cst_ref, s0_ref,
                    mixa_ref, convnew_ref, snew_ref, xc_ref, s_ref)
        ml_refs = (ml_ref, gate_ref, gatet_ref, ngb_ref, c0_ref, n0_ref, m0_ref,
                   mixb_ref, cnew_ref, nnew_ref, mnew_ref, c_ref, n_ref, m_ref)
    else:
        (gdn_ref, gate_ref, gatet_ref, nga_ref, ml_ref, ngb_ref,
         mixa_ref, snew_ref, mixb_ref, cnew_ref, nnew_ref, mnew_ref, s_ref, c_ref, n_ref, m_ref) = refs
        gdn_refs = (gdn_ref, gate_ref, gatet_ref, nga_ref, mixa_ref, snew_ref, s_ref)
        ml_refs = (ml_ref, gate_ref, gatet_ref, ngb_ref, mixb_ref, cnew_ref, nnew_ref, mnew_ref, c_ref, n_ref, m_ref)
    cumsum_cache = {}
    _run_interleaved(_gdn_stages(*gdn_refs, L=L, G=G, has_state=has_state, cumsum_cache=cumsum_cache),
                     _mlstm_stages(*ml_refs, L=L, G=G, has_state=has_state, cumsum_cache=cumsum_cache))


def _mixers(gdn_in, ml_in, gates, gates_t, ng_a, ng_b, *, L, G, cw=None, state=None):
    n_seq, T, _ = gdn_in.shape
    n_c = T // L
    has_state = state is not None

    def seq_blk(*tail):
        return pl.BlockSpec((G,) + tail, lambda b, c: (b,) + (0,) * len(tail))

    def tok_blk(width):
        return pl.BlockSpec((G, L, width), lambda b, c: (b, c, 0))

    def seq_shape(*tail):
        return jax.ShapeDtypeStruct((n_seq,) + tail, F32)

    state_specs = [seq_blk(H_A, DK_A, DV_A), seq_blk(H_B, DK_B, DV_B), seq_blk(H_B, DK_B), seq_blk(1, H_B)]
    state_shapes = [seq_shape(H_A, DK_A, DV_A), seq_shape(H_B, DK_B, DV_B), seq_shape(H_B, DK_B), seq_shape(1, H_B)]
    conv_spec, conv_shape = seq_blk(CONV_W - 1, CONV_CH), seq_shape(CONV_W - 1, CONV_CH)
    in_specs = [
        tok_blk(GDN_W), tok_blk(LANE),
        pl.BlockSpec((G, 1, N_GATE, L), lambda b, c: (b, c, 0, 0)),
        pl.BlockSpec((1, DV_A), lambda b, c: (0, 0)),
        tok_blk(MLP_W),
        pl.BlockSpec((H_B, DV_B), lambda b, c: (0, 0)),
    ]
    args = [gdn_in, gates, gates_t, ng_a, ml_in, ng_b]
    out_specs = [tok_blk(H_A * DV_A), state_specs[0], tok_blk(H_B * DV_B)] + state_specs[1:]
    mix_dtype = BF16 if L % (2 * SUBLANE) == 0 else F32
    mix_a = jax.ShapeDtypeStruct((n_seq, T, H_A * DV_A), mix_dtype)
    mix_b = jax.ShapeDtypeStruct((n_seq, T, H_B * DV_B), mix_dtype)
    out_shape = [mix_a, state_shapes[0], mix_b] + state_shapes[1:]
    scratch = [pltpu.VMEM((G, H_A, DK_A, DV_A), F32), pltpu.VMEM((G, H_B, LANE, DV_B), F32),
               pltpu.VMEM((G, SUBLANE, LANE), F32), pltpu.VMEM((G, SUBLANE, LANE), F32)]
    if has_state:
        in_specs += [pl.BlockSpec((SUBLANE, CONV_CH), lambda b, c: (0, 0)), conv_spec] + state_specs
        args += [cw] + list(state)
        out_specs.append(conv_spec)
        out_shape.append(conv_shape)
        scratch.append(pltpu.VMEM((G, L + SUBLANE, CONV_CH), F32))
    return pl.pallas_call(
        functools.partial(_mixers_kernel, L=L, G=G, has_state=has_state),
        grid=(n_seq // G, n_c),
        in_specs=in_specs,
        out_specs=out_specs,
        out_shape=out_shape,
        scratch_shapes=scratch,
        compiler_params=pltpu.CompilerParams(dimension_semantics=("parallel", "arbitrary"),
                                             vmem_limit_bytes=VMEM_LIMIT),
        name=f"mixers_L{L}",
    )(*args)


def _outproj_kernel(xp_ref, xs_ref, map_ref, mas_ref, mbp_ref, mbs_ref, wo_ref, g_ref, rw_ref, rb_ref,
                    x1_ref, xsort_ref, info_ref, cpad_ref, *, n_p_tiles):
    half = H_A * DV_A
    tm = xp_ref.shape[0]

    def body(seg):
        x_ref, ma_ref, mb_ref = (xp_ref, xs_ref)[seg], (map_ref, mas_ref)[seg], (mbp_ref, mbs_ref)[seg]
        x1 = (x_ref[...] + jnp.dot(ma_ref[...].astype(BF16), wo_ref[:half, :], preferred_element_type=F32)
              + jnp.dot(mb_ref[...].astype(BF16), wo_ref[half:, :], preferred_element_type=F32))
        x1_ref[...] = x1
        hn = _rms(x1, g_ref[...])
        hn_hi = hn.astype(BF16)
        hn_lo = (hn - hn_hi.astype(F32)).astype(BF16)
        logits = (jnp.dot(hn_hi, rw_ref[0], preferred_element_type=F32)
                  + jnp.dot(hn_hi, rw_ref[1], preferred_element_type=F32)
                  + jnp.dot(hn_lo, rw_ref[0], preferred_element_type=F32)) + rb_ref[...]

        vals = logits.T[:N_EXPERTS, :]
        e_iota = lax.broadcasted_iota(jnp.int32, (N_EXPERTS, tm), 0)
        sels, tops = [], []
        for _ in range(TOP_K):
            m = jnp.max(vals, axis=0, keepdims=True)
            first = jnp.min(jnp.where(vals == m, e_iota, N_EXPERTS), axis=0, keepdims=True)
            sel = e_iota == first
            vals = jnp.where(sel, -jnp.inf, vals)
            sels.append(sel)
            tops.append(m)
        ex = [jnp.exp(t - tops[0]) for t in tops]
        den = ex[0] + ex[1] + ex[2] + ex[3]
        gates = [e / den for e in ex]
        mask = sels[0].astype(F32) + sels[1].astype(F32) + sels[2].astype(F32) + sels[3].astype(F32)
        ri = lax.broadcasted_iota(jnp.int32, (tm, tm), 0)
        ci = lax.broadcasted_iota(jnp.int32, (tm, tm), 1)
        rank = _dot(mask, (ri < ci).astype(F32))
        cnt = jnp.sum(mask, axis=1, keepdims=True)
        cpad = jnp.ceil(cnt * (1.0 / SEG_ALIGN)) * SEG_ALIGN
        cpad_b = jnp.broadcast_to(cpad, (N_EXPERTS, tm))
        er = lax.broadcasted_iota(jnp.int32, (N_EXPERTS, N_EXPERTS), 0)
        ec = lax.broadcasted_iota(jnp.int32, (N_EXPERTS, N_EXPERTS), 1)
        seg_off = _dot((er > ec).astype(F32), cpad_b)
        pos = seg_off + rank
        q = [jnp.sum(jnp.where(s, pos, 0.0), axis=0, keepdims=True) for s in sels]

        j_iota = lax.broadcasted_iota(jnp.int32, (MOE_CAP, tm), 0).astype(F32)
        perm = jnp.zeros((MOE_CAP, tm), F32)
        for kk in range(TOP_K):
            perm = jnp.where(j_iota == q[kk], 1.0, perm)
        xsorted = _dot(perm, hn)
        xsort_ref[...] = xsorted.astype(MOE_DTYPE)

        r_iota = lax.broadcasted_iota(jnp.int32, (LANE, tm), 0)
        info = jnp.zeros((LANE, tm), F32)
        for kk in range(TOP_K):
            info = jnp.where(r_iota == kk, q[kk], info)
            info = jnp.where(r_iota == TOP_K + kk, gates[kk], info)
        info_ref[...] = info.T
        cpad_ref[0] = cpad_b[:, :LANE]

    _for_segment(n_p_tiles, body)


def _outproj(xp, xs, ma_p, ma_s, mb_p, mb_s, w_out, g, rw, rb, tm):
    n_p, n_s = xp.shape[0], xs.shape[0]
    n = n_p + n_s
    nt = n // tm
    npt = n_p // tm
    half = H_A * DV_A
    return pl.pallas_call(
        functools.partial(_outproj_kernel, n_p_tiles=npt),
        grid=(nt,),
        in_specs=_two_segment_specs(tm, D_MODEL, npt) + _two_segment_specs(tm, half, npt)
        + _two_segment_specs(tm, half, npt) + [
            pl.BlockSpec((D_MODEL, D_MODEL), lambda i: (0, 0)),
            pl.BlockSpec((1, D_MODEL), lambda i: (0, 0)),
            pl.BlockSpec((2, D_MODEL, LANE), lambda i: (0, 0, 0)),
            pl.BlockSpec((1, LANE), lambda i: (0, 0)),
        ],
        out_specs=[
            pl.BlockSpec((tm, D_MODEL), lambda i: (i, 0)),
            pl.BlockSpec((MOE_CAP, D_MODEL), lambda i: (i, 0)),
            pl.BlockSpec((tm, LANE), lambda i: (i, 0)),
            pl.BlockSpec((1, N_EXPERTS, LANE), lambda i: (i, 0, 0)),
        ],
        out_shape=[
            jax.ShapeDtypeStruct((n, D_MODEL), F32),
            jax.ShapeDtypeStruct((nt * MOE_CAP, D_MODEL), MOE_DTYPE),
            jax.ShapeDtypeStruct((n, LANE), F32),
            jax.ShapeDtypeStruct((nt, N_EXPERTS, LANE), F32),
        ],
        compiler_params=pltpu.CompilerParams(dimension_semantics=("arbitrary",), vmem_limit_bytes=VMEM_LIMIT),
        name="outproj",
    )(xp, xs, ma_p, ma_s, mb_p, mb_s, w_out, g, rw, rb)


def _expert_kernel(be_ref, bj_ref, tf_ref, tl_ref, cov_ref, nu_ref, vt_ref, ct_ref, lt_ref, nx_ref, ws_ref,
                   xs_hbm, wgu_hbm, bgu_ref, wd_hbm, bd_ref, y_ref,
                   xbuf, gsem, wgu_st, wd_st, wsem, wgu_bf, wd_bf, *, nt):
    b = pl.program_id(0)
    n_used = nu_ref[0]
    slot = b % 2

    def start_pieces(bb, copy, s):
        e = be_ref[bb]
        base = bj_ref[bb] * MOE_BLK

        def body(t, carry):
            k = e * nt + t
            lo = jnp.maximum(vt_ref[k], base)
            ln = jnp.minimum(ct_ref[k], base + MOE_BLK) - lo

            @pl.when(ln > 0)
            def _():
                copy(s, pl.multiple_of(lt_ref[k] + lo, SEG_ALIGN), pl.multiple_of(lo - base, SEG_ALIGN),
                     pl.multiple_of(ln, SEG_ALIGN)).start()
            return carry

        lax.fori_loop(tf_ref[bb], tl_ref[bb] + 1, body, 0)

    def weight_copies(e):
        return (pltpu.make_async_copy(wgu_hbm.at[e], wgu_st, wsem.at[0]),
                pltpu.make_async_copy(wd_hbm.at[e], wd_st, wsem.at[1]))

    def cast_weights(p):
        wgu_bf[p] = wgu_st[...].astype(BF16)
        wd_bf[p] = wd_st[...].astype(BF16)

    def gather_copy(s, src, dst, size):
        return pltpu.make_async_copy(xs_hbm.at[pl.ds(src, size)], xbuf.at[s, pl.ds(dst, size)], gsem.at[s])

    def wait_rows(count, copy, s):
        @pl.when(count > 0)
        def _():
            copy(s, 0, 0, pl.multiple_of(count, SEG_ALIGN)).wait()

    @pl.when(b == 0)
    def _():
        xbuf[...] = jnp.zeros_like(xbuf)
        start_pieces(0, gather_copy, 0)

    @pl.when(b + 1 < n_used)
    def _():
        start_pieces(b + 1, gather_copy, 1 - slot)

    @pl.when(b < n_used)
    def _():
        e = be_ref[b]
        first = jnp.logical_or(b == 0, be_ref[jnp.maximum(b - 1, 0)] != e)
        last = jnp.logical_or(b == n_used - 1, be_ref[jnp.minimum(b + 1, n_used - 1)] != e)
        has_next = nx_ref[b] < N_EXPERTS
        p = ws_ref[b]

        @pl.when(b == 0)
        def _():
            for cp in weight_copies(e):
                cp.start()
            for cp in weight_copies(e):
                cp.wait()
            cast_weights(p)

        @pl.when(jnp.logical_and(first, has_next))
        def _():
            for cp in weight_copies(nx_ref[b]):
                cp.start()

        wait_rows(cov_ref[b], gather_copy, slot)

        def expert_mlp(rows):
            hgu = jnp.dot(xbuf[slot, :rows].astype(BF16), wgu_bf[p], preferred_element_type=F32) + bgu_ref[0]
            gate = jnp.minimum(hgu[:, :D_FF], SWIGLU_LIMIT)
            up = jnp.clip(hgu[:, D_FF:], -SWIGLU_LIMIT, SWIGLU_LIMIT)
            act = (up + 1.0) * gate * _sigmoid(SWIGLU_ALPHA * gate)
            y = jnp.dot(act.astype(BF16), wd_bf[p], preferred_element_type=F32) + bd_ref[0]
            y_ref[:rows] = y.astype(MOE_DTYPE)
            if rows < MOE_BLK:
                y_ref[rows:] = jnp.zeros((MOE_BLK - rows, D_MODEL), MOE_DTYPE)

        quarter = MOE_BLK // 4
        for nq in range(1, 5):
            pl.when(jnp.logical_and(cov_ref[b] > (nq - 1) * quarter, cov_ref[b] <= nq * quarter))(
                functools.partial(expert_mlp, nq * quarter))

        @pl.when(jnp.logical_and(last, has_next))
        def _():
            for cp in weight_copies(nx_ref[b]):
                cp.wait()
            cast_weights(1 - p)


def _experts(tables, xs, w_gu, b_gu, w_down, b_down, nt):
    nb = tables[0].shape[0]

    def bias_blk(b, *t):
        return (t[0][jnp.minimum(b, t[5][0] - 1)], 0, 0)

    def out_blk(b, *t):
        return (jnp.minimum(b, t[5][0] - 1), 0)

    grid_spec = pltpu.PrefetchScalarGridSpec(
        num_scalar_prefetch=len(tables),
        grid=(nb,),
        in_specs=[
            pl.BlockSpec(memory_space=pl.ANY),
            pl.BlockSpec(memory_space=pl.ANY),
            pl.BlockSpec((1, 1, 2 * D_FF), bias_blk),
            pl.BlockSpec(memory_space=pl.ANY),
            pl.BlockSpec((1, 1, D_MODEL), bias_blk),
        ],
        out_specs=pl.BlockSpec((MOE_BLK, D_MODEL), out_blk),
        scratch_shapes=[
            pltpu.VMEM((2, MOE_BLK, D_MODEL), MOE_DTYPE),
            pltpu.SemaphoreType.DMA((2,)),
            pltpu.VMEM((D_MODEL, 2 * D_FF), F32),
            pltpu.VMEM((D_FF, D_MODEL), F32),
            pltpu.SemaphoreType.DMA((2,)),
            pltpu.VMEM((2, D_MODEL, 2 * D_FF), BF16),
            pltpu.VMEM((2, D_FF, D_MODEL), BF16),
        ],
    )
    return pl.pallas_call(
        functools.partial(_expert_kernel, nt=nt),
        grid_spec=grid_spec,
        out_shape=jax.ShapeDtypeStruct((nb * MOE_BLK, D_MODEL), MOE_DTYPE),
        compiler_params=pltpu.CompilerParams(dimension_semantics=("arbitrary",), vmem_limit_bytes=VMEM_LIMIT),
        name="experts",
    )(*tables, xs, w_gu, b_gu, w_down, b_down)


def _combine_kernel(src_ref, len_ref, off_ref, used_ref,
                    y_hbm, info_ref, x1_ref, pp_ref, ps_ref, gple_ref, wg_ref, wp_ref, gfin_ref,
                    outp_ref, outs_ref, ybuf, sem, *, n_p_tiles):
    tm = x1_ref.shape[0]
    i = pl.program_id(0)
    slot = i % 2

    def tile_copy(s, src, dst, size):
        return pltpu.make_async_copy(y_hbm.at[pl.ds(src, size)], ybuf.at[s, pl.ds(dst, size)], sem.at[s])

    def gather_tile(t, s):
        def body(e, carry):
            k = t * N_EXPERTS + e
            ln = len_ref[k]

            @pl.when(ln > 0)
            def _():
                tile_copy(s, pl.multiple_of(src_ref[k], SEG_ALIGN), pl.multiple_of(off_ref[k], SEG_ALIGN),
                          pl.multiple_of(ln, SEG_ALIGN)).start()
            return carry
        lax.fori_loop(0, N_EXPERTS, body, 0)

    @pl.when(i == 0)
    def _():
        ybuf[...] = jnp.zeros_like(ybuf)
        gather_tile(0, 0)

    @pl.when(i + 1 < pl.num_programs(0))
    def _():
        gather_tile(i + 1, 1 - slot)

    tile_copy(slot, 0, 0, pl.multiple_of(used_ref[i], SEG_ALIGN)).wait()
    ys_ref = ybuf.at[slot]

    def body(seg):
        p_ref, out_ref = (pp_ref, ps_ref)[seg], (outp_ref, outs_ref)[seg]
        info = info_ref[...]
        j_iota = lax.broadcasted_iota(jnp.int32, (tm, MOE_CAP), 1).astype(F32)
        gmat = jnp.zeros((tm, MOE_CAP), F32)
        for kk in range(TOP_K):
            gmat = jnp.where(j_iota == info[:, kk:kk + 1], info[:, TOP_K + kk:TOP_K + kk + 1], gmat)
        x2 = x1_ref[...] + jnp.dot(gmat.astype(BF16), ys_ref[...].astype(BF16), preferred_element_type=F32)
        hn = _rms(x2, gple_ref[...]).astype(BF16)
        gate = _sigmoid(jnp.dot(hn, wg_ref[...], preferred_element_type=F32))
        pe = jnp.dot(p_ref[...].astype(BF16), wp_ref[...], preferred_element_type=F32)
        x3 = x2 + gate * pe
        out_ref[...] = _rms(x3, gfin_ref[...])

    _for_segment(n_p_tiles, body)


def _combine(tile_tables, ys, info, x1, pp, ps, g_ple, w_gate, w_p, g_fin, tm):
    n_p, n_s = pp.shape[0], ps.shape[0]
    n = n_p + n_s
    nt = n // tm
    npt = n_p // tm
    grid_spec = pltpu.PrefetchScalarGridSpec(
        num_scalar_prefetch=len(tile_tables),
        grid=(nt,),
        in_specs=[
            pl.BlockSpec(memory_space=pl.ANY),
            pl.BlockSpec((tm, LANE), lambda i, *_: (i, 0)),
            pl.BlockSpec((tm, D_MODEL), lambda i, *_: (i, 0)),
        ] + _two_segment_specs(tm, PLE_DIM, npt) + [
            pl.BlockSpec((1, D_MODEL), lambda i, *_: (0, 0)),
            pl.BlockSpec((D_MODEL, D_MODEL), lambda i, *_: (0, 0)),
            pl.BlockSpec((PLE_DIM, D_MODEL), lambda i, *_: (0, 0)),
            pl.BlockSpec((1, D_MODEL), lambda i, *_: (0, 0)),
        ],
        out_specs=_two_segment_specs(tm, D_MODEL, npt),
        scratch_shapes=[pltpu.VMEM((2, MOE_CAP, D_MODEL), MOE_DTYPE), pltpu.SemaphoreType.DMA((2,))],
    )
    return pl.pallas_call(
        functools.partial(_combine_kernel, n_p_tiles=npt),
        grid_spec=grid_spec,
        out_shape=[jax.ShapeDtypeStruct((n_p, D_MODEL), F32), jax.ShapeDtypeStruct((n_s, D_MODEL), F32)],
        compiler_params=pltpu.CompilerParams(dimension_semantics=("arbitrary",), vmem_limit_bytes=VMEM_LIMIT),
        name="combine",
    )(*tile_tables, ys, info, x1, pp, ps, g_ple, w_gate, w_p, g_fin)


def _block_tables(seg_len, nb):
    nt = seg_len.shape[0]
    seg_off = jnp.cumsum(seg_len, axis=1) - seg_len
    seg_end = jnp.cumsum(seg_len, axis=0).T
    seg_start = seg_end - seg_len.T
    n_rows = seg_end[:, -1]
    n_blk = (n_rows + MOE_BLK - 1) // MOE_BLK
    blk_end = jnp.cumsum(n_blk)
    b = jnp.arange(nb, dtype=jnp.int32)
    block_e = jnp.minimum(jnp.sum((blk_end[None, :] <= b[:, None]).astype(jnp.int32), axis=1), N_EXPERTS - 1)
    idx = jnp.where(n_blk > 0, jnp.arange(N_EXPERTS, dtype=jnp.int32), N_EXPERTS)
    nxt = jnp.concatenate([lax.cummin(idx, axis=0, reverse=True)[1:], jnp.full((1,), N_EXPERTS, jnp.int32)])
    parity = (jnp.cumsum((n_blk > 0).astype(jnp.int32)) - 1) % 2
    per_e = jnp.concatenate([jnp.stack([blk_end - n_blk, n_rows, nxt, parity], axis=1), seg_start, seg_end],
                            axis=1).astype(F32)
    onehot = (block_e[:, None] == jnp.arange(N_EXPERTS, dtype=jnp.int32)[None, :]).astype(F32)
    per_b = jnp.dot(onehot, per_e, precision=HI).astype(jnp.int32)
    block_j = b - per_b[:, 0]
    base = block_j * MOE_BLK
    t_first = jnp.sum((per_b[:, 4 + nt:] <= base[:, None]).astype(jnp.int32), axis=1)
    t_last = jnp.sum((per_b[:, 4:4 + nt] < (base + MOE_BLK)[:, None]).astype(jnp.int32), axis=1) - 1
    cover = jnp.clip(per_b[:, 1] - base, 0, MOE_BLK)
    seg_shift = (jnp.arange(nt, dtype=jnp.int32)[:, None] * MOE_CAP + seg_off).T - seg_start
    tables = (block_e, block_j, t_first, t_last, cover, blk_end[-1:], seg_start.reshape(-1),
              seg_end.reshape(-1), seg_shift.reshape(-1), per_b[:, 2], per_b[:, 3])
    y_src = ((blk_end - n_blk) * MOE_BLK)[None, :] + seg_start.T
    tile_tables = (y_src.reshape(-1), seg_len.reshape(-1), seg_off.reshape(-1), jnp.sum(seg_len, axis=1))
    return tuple(t.astype(jnp.int32) for t in tables), tuple(t.astype(jnp.int32) for t in tile_tables)


def _rearranged_in_weights(w_in):
    o = np.cumsum([0, CONV_CH, H_A * DV_A, H_A, H_A, H_B * DK_B, H_B * DK_B, H_B * DV_B, H_B * DV_B, H_B, H_B])
    conv_in, z_a, a_a, b_a, q_b, k_b, v_b, o_b, i_b, f_b = (w_in[:, int(o[j]):int(o[j + 1])] for j in range(10))
    small = jnp.concatenate([a_a, b_a, i_b, f_b], axis=1)
    w_gdn = w_in[:, :GDN_W]
    w_rest = jnp.concatenate([q_b, k_b, v_b, o_b, small, jnp.zeros((D_MODEL, LANE - N_GATE), w_in.dtype)], axis=1)
    return w_gdn.astype(BF16), w_rest.astype(BF16), small.T.astype(BF16)


def _gate_params(a_log, dt_bias, i_bias, f_bias):
    z4 = jnp.zeros((4,), F32)
    alog = jnp.concatenate([a_log.astype(F32), z4, z4, z4])
    bias = jnp.concatenate([dt_bias.astype(F32), z4, i_bias.astype(F32), f_bias.astype(F32)])
    pad = jnp.zeros((LANE - N_GATE,), F32)
    pcol = jnp.zeros((SUBLANE, LANE), F32).at[0].set(jnp.concatenate([alog, pad])).at[1].set(
        jnp.concatenate([bias, pad]))
    prow = jnp.zeros((N_GATE, LANE), F32).at[:, 0].set(alog).at[:, 1].set(bias)
    return pcol, prow


def kernel(x_prompt, x_sample, p_prompt, p_sample, state_conv, state_gdn, state_mlstm_c, state_mlstm_n, state_mlstm_m, norm_attn_g, w_in, conv_w, gdn_a_log, gdn_dt_bias, gdn_norm_g, mlstm_i_bias, mlstm_f_bias, mlstm_norm_g, w_out, norm_moe_g, router_w, router_b, expert_w_gu, expert_b_gu, expert_w_down, expert_b_down, norm_ple_g, ple_gate_w, ple_w, final_norm_g):
    bp, tp, _ = x_prompt.shape
    bs, ts, _ = x_sample.shape
    n_p, n_s = bp * tp, bs * ts
    n = n_p + n_s
    lp, ls = min(tp, CHUNK), min(ts, CHUNK)
    tm = 256
    gp = 4 if bp % 4 == 0 else 1
    gs = 16 if bs % 16 == 0 else 1
    assert tp % lp == 0 and ts % ls == 0 and tp % tm == 0 and n_s % tm == 0 and ls % SUBLANE == 0

    xp = x_prompt.reshape(n_p, D_MODEL)
    xs = x_sample.reshape(n_s, D_MODEL)

    w_gdn, w_rest, ws_t = _rearranged_in_weights(w_in[0])
    pcol, prow = _gate_params(gdn_a_log[0], gdn_dt_bias[0], mlstm_i_bias[0], mlstm_f_bias[0])
    cw = jnp.zeros((SUBLANE, CONV_CH), F32).at[:CONV_W].set(conv_w[0].astype(F32))
    gdn_p, gdn_s, ml_p, ml_s, gate_p, gate_s, gatet_p, gatet_s, conv_p = _inproj(
        xp, xs, norm_attn_g[0].reshape(1, D_MODEL), w_gdn, w_rest, ws_t, pcol, prow, cw, tm, bp)
    gt_p = gatet_p.reshape(N_GATE, bp, tp // lp, lp).transpose(1, 2, 0, 3)
    gt_s = gatet_s.reshape(N_GATE, bs, ts // ls, ls).transpose(1, 2, 0, 3)

    ng_a = gdn_norm_g[0].reshape(1, DV_A).astype(F32)
    ng_b = mlstm_norm_g[0].reshape(H_B, DV_B).astype(F32)
    ma_p, gdn_st_p, mb_p, c_p, nn_p, m_p = _mixers(
        gdn_p.reshape(bp, tp, GDN_W), ml_p.reshape(bp, tp, MLP_W), gate_p.reshape(bp, tp, LANE), gt_p, ng_a, ng_b,
        L=lp, G=gp)
    ma_s, gdn_st_s, mb_s, c_s, nn_s, m_s, conv_s = _mixers(
        gdn_s.reshape(bs, ts, GDN_W), ml_s.reshape(bs, ts, MLP_W), gate_s.reshape(bs, ts, LANE), gt_s, ng_a, ng_b,
        L=ls, G=gs, cw=cw,
        state=(state_conv[0], state_gdn[0], state_mlstm_c[0], state_mlstm_n[0], state_mlstm_m[0].reshape(bs, 1, H_B)))
    half = H_A * DV_A

    rw = jnp.zeros((D_MODEL, LANE), F32).at[:, :N_EXPERTS].set(router_w[0])
    rw_hi = rw.astype(BF16)
    rw = jnp.stack([rw_hi, (rw - rw_hi.astype(F32)).astype(BF16)])
    rb = jnp.full((1, LANE), NEG, F32).at[0, :N_EXPERTS].set(router_b[0])
    x1, x_sorted, info, seg_len = _outproj(xp, xs, ma_p.reshape(n_p, half), ma_s.reshape(n_s, half),
                                           mb_p.reshape(n_p, half), mb_s.reshape(n_s, half),
                                           w_out[0].astype(BF16), norm_moe_g[0].reshape(1, D_MODEL), rw, rb, MOE_TM)

    nt = n // MOE_TM
    nb = -(-(n * TOP_K + nt * N_EXPERTS * (SEG_ALIGN - 1)) // MOE_BLK) + N_EXPERTS
    tables, tile_tables = _block_tables(seg_len[:, :, 0].astype(jnp.int32), nb)
    y_blocks = _experts(tables, x_sorted, expert_w_gu[0], expert_b_gu[0].reshape(N_EXPERTS, 1, 2 * D_FF),
                        expert_w_down[0], expert_b_down[0].reshape(N_EXPERTS, 1, D_MODEL), nt)
    y_p, y_s = _combine(tile_tables, y_blocks, info, x1, p_prompt[0].reshape(n_p, PLE_DIM),
                        p_sample[0].reshape(n_s, PLE_DIM), norm_ple_g[0].reshape(1, D_MODEL),
                        ple_gate_w[0].astype(BF16), ple_w[0].astype(BF16), final_norm_g.reshape(1, D_MODEL), MOE_TM)

    return (y_p.reshape(bp, tp, D_MODEL), y_s.reshape(bs, ts, D_MODEL),
            conv_p[None], gdn_st_p[None], c_p[None], nn_p[None], m_p.reshape(1, bp, H_B),
            conv_s[None], gdn_st_s[None], c_s[None], nn_s[None], m_s.reshape(1, bs, H_B))
```

```python
import functools

import numpy as np
import jax
import jax.numpy as jnp
from jax import lax
from jax.experimental import pallas as pl
from jax.experimental.pallas import tpu as pltpu

F32 = jnp.float32
BF16 = jnp.bfloat16

D_MODEL = 1024
H_A, DK_A, DV_A = 4, 128, 128
H_B, DK_B, DV_B = 4, 64, 128
CONV_W = 4
CONV_CH = H_A * (2 * DK_A + DV_A)
N_EXPERTS = 32
TOP_K = 4
D_FF = 1024
SWIGLU_LIMIT = 7.0
SWIGLU_ALPHA = 1.702
PLE_DIM = 256
EPS = 1e-6
NEG = -1e30
CHUNK = 64

LANE = 128
SUBLANE = 8
GDN_W = CONV_CH + H_A * DV_A
MLP_W = 2 * H_B * DK_B + 2 * H_B * DV_B
N_GATE = 16
PROJ_CHUNK = 512

VMEM_LIMIT = 48 * 1024 * 1024

MOE_TM = 256
MOE_BLK = 512
MOE_DTYPE = F32
SEG_ALIGN = SUBLANE
MOE_CAP = -(-(MOE_TM * TOP_K + N_EXPERTS * (SEG_ALIGN - 1)) // LANE) * LANE

HI = lax.Precision.HIGHEST

_NN = (((1,), (0,)), ((), ()))
_NT = (((1,), (1,)), ((), ()))
_TN = (((0,), (0,)), ((), ()))


def _dot(a, b, dims=_NN):
    return lax.dot_general(a.astype(BF16), b.astype(BF16), dims, preferred_element_type=F32)


def _split3(x):
    hi = x.astype(BF16)
    r = x - hi.astype(F32)
    mid = r.astype(BF16)
    return hi, mid, (r - mid.astype(F32)).astype(BF16)


def _chunk_cumsums(gact, gact_t, tril, triu, cache, key):
    if key not in cache:
        tril_b, triu_b = tril.astype(BF16), triu.astype(BF16)
        cum_c = sum(jnp.dot(tril_b, part, preferred_element_type=F32) for part in _split3(gact))
        cum_r = sum(jnp.dot(part, triu_b, preferred_element_type=F32) for part in _split3(gact_t))
        cache[key] = (cum_c, cum_r)
    return cache[key]


def _rms(x, g):
    return x * lax.rsqrt(jnp.mean(x * x, axis=-1, keepdims=True) + EPS) * g


def _softplus(t):
    return jnp.maximum(t, 0.0) + jnp.log1p(jnp.exp(-jnp.abs(t)))


def _sigmoid(t):
    return 1.0 / (1.0 + jnp.exp(-t))


def _silu(t):
    return t * _sigmoid(t)


def _activate_gates(raw, idx, alog, bias):
    t = raw + bias
    g = -jnp.exp(alog) * _softplus(t)
    beta = _sigmoid(t)
    lf = -_softplus(-t)
    return jnp.where(idx < 4, g, jnp.where(idx < 8, beta, jnp.where(idx < 12, t, lf)))


def _two_segment_specs(tm, width, n_p_tiles):
    return [pl.BlockSpec((tm, width), lambda i, *_: (jnp.minimum(i, n_p_tiles - 1), 0)),
            pl.BlockSpec((tm, width), lambda i, *_: (jnp.maximum(i - n_p_tiles, 0), 0))]


def _for_segment(n_p_tiles, body):
    i = pl.program_id(0)

    @pl.when(i < n_p_tiles)
    def _():
        body(0)

    @pl.when(i >= n_p_tiles)
    def _():
        body(1)


def _gdn_preactivate_stages(raw_ref, xc_ref, cw_ref, out_ref, cnew_ref, first_of_seq):
    tm = raw_ref.shape[0]
    xc_ref[0:SUBLANE, :] = jnp.where(first_of_seq, 0.0, xc_ref[tm:tm + SUBLANE, :])
    xc_ref[SUBLANE:SUBLANE + tm, :] = raw_ref[:, :CONV_CH]
    out_ref[:, CONV_CH:] = _silu(raw_ref[:, CONV_CH:])
    cnew_ref[0] = xc_ref[SUBLANE + tm - (CONV_W - 1):SUBLANE + tm, :]
    yield
    base = SUBLANE - (CONV_W - 1)
    for c0 in range(0, CONV_CH, DK_A):
        conv = xc_ref[base:base + tm, c0:c0 + DK_A] * cw_ref[0:1, c0:c0 + DK_A]
        for j in range(1, CONV_W):
            conv = conv + xc_ref[base + j:base + j + tm, c0:c0 + DK_A] * cw_ref[j:j + 1, c0:c0 + DK_A]
        act = _silu(conv)
        if c0 < H_A * DK_A:
            act = act * (lax.rsqrt(jnp.sum(act * act, axis=-1, keepdims=True) + EPS) * (DK_A ** -0.5))
        elif c0 < 2 * H_A * DK_A:
            act = act * lax.rsqrt(jnp.sum(act * act, axis=-1, keepdims=True) + EPS)
        out_ref[:, c0:c0 + DK_A] = act
        yield


def _inproj_kernel(xp_ref, xs_ref, g_ref, wa_ref, wb_ref, wst_ref, pc_ref, pr_ref, cw_ref,
                   gdnp_ref, gdns_ref, mlp_ref, mls_ref, gatep_ref, gates_ref, gatetp_ref, gatets_ref, cnew_ref,
                   xc_ref, raw_ref, *, n_p_tiles, tiles_per_seq):
    tm = xp_ref.shape[0]
    i = pl.program_id(0)

    @pl.when(i == 0)
    def _():
        xc_ref[...] = jnp.zeros_like(xc_ref)
        raw_ref[...] = jnp.zeros_like(raw_ref)

    def preactivate_previous_tile():
        return _gdn_preactivate_stages(raw_ref, xc_ref, cw_ref, gdnp_ref, cnew_ref, (i - 1) % tiles_per_seq == 0)

    def projection_stages(seg):
        x_ref = (xp_ref, xs_ref)[seg]
        gdn_dst, ml_ref = (raw_ref, gdns_ref)[seg], (mlp_ref, mls_ref)[seg]
        gate_ref, gatet_ref = (gatep_ref, gates_ref)[seg], (gatetp_ref, gatets_ref)[seg]
        hn = _rms(x_ref[...], g_ref[...]).astype(BF16)
        yield
        for c0 in range(0, GDN_W, PROJ_CHUNK):
            gdn_dst[:, c0:c0 + PROJ_CHUNK] = jnp.dot(hn, wa_ref[:, c0:c0 + PROJ_CHUNK], preferred_element_type=F32)
            yield
        for c0 in range(0, MLP_W, PROJ_CHUNK):
            ml_ref[:, c0:c0 + PROJ_CHUNK] = jnp.dot(hn, wb_ref[:, c0:c0 + PROJ_CHUNK], preferred_element_type=F32)
            yield
        raw = jnp.dot(hn, wb_ref[:, MLP_W:], preferred_element_type=F32)
        lane = lax.broadcasted_iota(jnp.int32, (tm, LANE), 1)
        gate_ref[...] = _activate_gates(raw, lane, pc_ref[0:1, :], pc_ref[1:2, :])
        raw_t = lax.dot_general(wst_ref[...], hn, _NT, preferred_element_type=F32)
        row = lax.broadcasted_iota(jnp.int32, (N_GATE, tm), 0)
        gatet_ref[...] = _activate_gates(raw_t, row, pr_ref[:, 0:1], pr_ref[:, 1:2])

    def body(seg):
        if seg == 0:
            _run_interleaved(preactivate_previous_tile(), projection_stages(0))
        else:
            pl.when(i == n_p_tiles)(lambda: _run_interleaved(preactivate_previous_tile()))
            _run_interleaved(projection_stages(1))

    _for_segment(n_p_tiles, body)


def _inproj(xp, xs, g, w_gdn, w_rest, ws_t, pcol, prow, cw, tm, n_seq_p):
    n_p, n_s = xp.shape[0], xs.shape[0]
    npt = n_p // tm
    tiles_per_seq = npt // n_seq_p

    def out2(width):
        return _two_segment_specs(tm, width, npt)

    def shp2(width):
        return [jax.ShapeDtypeStruct((n_p, width), F32), jax.ShapeDtypeStruct((n_s, width), F32)]

    def prev_tile(i):
        return jnp.clip(i - 1, 0, npt - 1)

    return pl.pallas_call(
        functools.partial(_inproj_kernel, n_p_tiles=npt, tiles_per_seq=tiles_per_seq),
        grid=((n_p + n_s) // tm,),
        in_specs=_two_segment_specs(tm, D_MODEL, npt) + [
            pl.BlockSpec((1, D_MODEL), lambda i: (0, 0)),
            pl.BlockSpec((D_MODEL, GDN_W), lambda i: (0, 0)),
            pl.BlockSpec((D_MODEL, MLP_W + LANE), lambda i: (0, 0)),
            pl.BlockSpec((N_GATE, D_MODEL), lambda i: (0, 0)),
            pl.BlockSpec((SUBLANE, LANE), lambda i: (0, 0)),
            pl.BlockSpec((N_GATE, LANE), lambda i: (0, 0)),
            pl.BlockSpec((SUBLANE, CONV_CH), lambda i: (0, 0)),
        ],
        out_specs=[
            pl.BlockSpec((tm, GDN_W), lambda i: (prev_tile(i), 0)),
            pl.BlockSpec((tm, GDN_W), lambda i: (jnp.maximum(i - npt, 0), 0)),
        ] + out2(MLP_W) + out2(LANE) + [
            pl.BlockSpec((N_GATE, tm), lambda i: (0, jnp.minimum(i, npt - 1))),
            pl.BlockSpec((N_GATE, tm), lambda i: (0, jnp.maximum(i - npt, 0))),
            pl.BlockSpec((1, CONV_W - 1, CONV_CH), lambda i: (prev_tile(i) // tiles_per_seq, 0, 0)),
        ],
        out_shape=shp2(GDN_W) + shp2(MLP_W) + shp2(LANE) + [
            jax.ShapeDtypeStruct((N_GATE, n_p), F32), jax.ShapeDtypeStruct((N_GATE, n_s), F32),
            jax.ShapeDtypeStruct((n_seq_p, CONV_W - 1, CONV_CH), F32)],
        scratch_shapes=[pltpu.VMEM((tm + SUBLANE, CONV_CH), F32), pltpu.VMEM((tm, GDN_W), F32)],
        compiler_params=pltpu.CompilerParams(dimension_semantics=("arbitrary",), vmem_limit_bytes=VMEM_LIMIT),
        name="inproj",
    )(xp, xs, g, w_gdn, w_rest, ws_t, pcol, prow, cw)


def _chunk_masks(L):
    ri = lax.broadcasted_iota(jnp.int32, (L, L), 0)
    ci = lax.broadcasted_iota(jnp.int32, (L, L), 1)
    return ri >= ci, ri > ci, ri <= ci


def _run_interleaved(*stage_generators):
    live = list(stage_generators)
    while live:
        for gen in list(live):
            if next(gen, StopIteration) is StopIteration:
                live.remove(gen)


def _gdn_stages(*refs, L, G, has_state, cumsum_cache):
    if has_state:
        (xin_ref, gate_ref, gatet_ref, cw_ref, ng_ref, cst_ref, s0_ref,
         mix_ref, cnew_ref, snew_ref, xc_ref, s_ref) = refs
    else:
        xin_ref, gate_ref, gatet_ref, ng_ref, mix_ref, snew_ref, s_ref = refs
    c = pl.program_id(1)

    @pl.when(c == 0)
    def _():
        if has_state:
            xc_ref[:, 0:SUBLANE, :] = jnp.zeros((G, SUBLANE, CONV_CH), F32)
            xc_ref[:, SUBLANE - (CONV_W - 1):SUBLANE, :] = cst_ref[...]
            s_ref[...] = s0_ref[...]
        else:
            s_ref[...] = jnp.zeros_like(s_ref)

    if has_state:
        @pl.when(c > 0)
        def _():
            xc_ref[:, 0:SUBLANE, :] = xc_ref[:, L:L + SUBLANE, :]

    yield
    tril, strict, triu = _chunk_masks(L)
    base = SUBLANE - (CONV_W - 1)

    chains = [(g, h) for g in range(G) for h in range(H_A)]
    s_old = [s_ref[g, h] for g, h in chains]
    if has_state:
        for g in range(G):
            xc_ref[g, SUBLANE:SUBLANE + L, :] = xin_ref[g, :, :CONV_CH]

    q, k, v, beta, gc, gl, decay = [], [], [], [], [], [], []
    for g in range(G):
        if has_state:
            conv = xc_ref[g, base:base + L, :] * cw_ref[0:1, :]
            for j in range(1, CONV_W):
                conv = conv + xc_ref[g, base + j:base + j + L, :] * cw_ref[j:j + 1, :]
            cnew_ref[g] = xc_ref[g, SUBLANE + L - (CONV_W - 1):SUBLANE + L, :]
            act = _silu(conv)
        else:
            act = xin_ref[g, :, :CONV_CH]
        gact = gate_ref[g]
        cum_c, cum_r = _chunk_cumsums(gact, gatet_ref[g, 0], tril, triu, cumsum_cache, g)
        for h in range(H_A):
            q.append(act[:, h * DK_A:(h + 1) * DK_A])
            k.append(act[:, H_A * DK_A + h * DK_A:H_A * DK_A + (h + 1) * DK_A])
            v.append(act[:, 2 * H_A * DK_A + h * DV_A:2 * H_A * DK_A + (h + 1) * DV_A])
            beta.append(gact[:, 4 + h:5 + h])
            gc.append(cum_c[:, h:h + 1])
            gl.append(cum_c[L - 1:L, h:h + 1])
            gr = cum_r[h:h + 1, :]
            decay.append(jnp.where(tril, jnp.exp(jnp.where(tril, cum_c[:, h:h + 1] - gr, 0.0)), 0.0))
        yield

    nc = range(len(chains))
    if has_state:
        qss = [jnp.sum(q[i] * q[i], axis=-1, keepdims=True) for i in nc]
        kss = [jnp.sum(k[i] * k[i], axis=-1, keepdims=True) for i in nc]
        q = [q[i] * (lax.rsqrt(qss[i] + EPS) * (DK_A ** -0.5)) for i in nc]
        k = [k[i] * lax.rsqrt(kss[i] + EPS) for i in nc]
    kb = [k[i] * beta[i] for i in nc]
    egc = [jnp.exp(gc[i]) for i in nc]
    yield
    kk = [_dot(kb[i], k[i], _NT) for i in nc]
    yield
    qk = [_dot(q[i], k[i], _NT) for i in nc]
    yield
    eye = (lax.broadcasted_iota(jnp.int32, (L, L), 0) == lax.broadcasted_iota(jnp.int32, (L, L), 1)).astype(F32)
    pw = [-jnp.where(strict, kk[i] * decay[i], 0.0) for i in nc]
    t_inv = [eye + pw[i] for i in nc]
    span = 2
    while span < L:
        yield
        pw = [_dot(pw[i], pw[i]) for i in nc]
        yield
        t_inv = [t_inv[i] + _dot(t_inv[i], pw[i]) for i in nc]
        span *= 2
    yield
    sol = [_dot(t_inv[i], jnp.concatenate([v[i] * beta[i], kb[i] * egc[i]], axis=-1)) for i in nc]
    yield
    qs = [_dot(q[i] * egc[i], s_old[i]) for i in nc]
    yield
    ws = [_dot(sol[i][:, DV_A:], s_old[i]) for i in nc]
    v_new = [sol[i][:, :DV_A] - ws[i] for i in nc]
    yield
    o = [qs[i] + _dot(jnp.where(tril, qk[i] * decay[i], 0.0), v_new[i]) for i in nc]
    yield
    s_new = [s_old[i] * jnp.exp(gl[i]) + _dot(k[i] * jnp.exp(gl[i] - gc[i]), v_new[i], _TN) for i in nc]
    yield
    ms = [jnp.mean(o[i] * o[i], axis=-1, keepdims=True) for i in nc]
    on = [o[i] * lax.rsqrt(ms[i] + EPS) for i in nc]
    yield
    for i, (g, h) in enumerate(chains):
        z = xin_ref[g, :, CONV_CH + h * DV_A:CONV_CH + (h + 1) * DV_A]
        out = on[i] * ng_ref[...] * (_silu(z) if has_state else z)
        mix_ref[g, :, h * DV_A:(h + 1) * DV_A] = out.astype(mix_ref.dtype)
    yield
    for i, (g, h) in enumerate(chains):
        s_ref[g, h] = s_new[i]
        snew_ref[g, h] = s_new[i]


def _mlstm_stages(*refs, L, G, has_state, cumsum_cache):
    if has_state:
        (xin_ref, gate_ref, gatet_ref, ng_ref, c0_ref, n0_ref, m0_ref,
         mix_ref, cnew_ref, nnew_ref, mnew_ref, c_ref, n_ref, m_ref) = refs
    else:
        (xin_ref, gate_ref, gatet_ref, ng_ref,
         mix_ref, cnew_ref, nnew_ref, mnew_ref, c_ref, n_ref, m_ref) = refs
    c = pl.program_id(1)

    @pl.when(c == 0)
    def _():
        c_ref[...] = jnp.zeros_like(c_ref)
        n_ref[...] = jnp.zeros_like(n_ref)
        m_ref[...] = jnp.zeros_like(m_ref)
        if has_state:
            for h in range(H_B):
                off = (h % 2) * DK_B
                c_ref[:, h, off:off + DK_B, :] = c0_ref[:, h]
                n_ref[:, h:h + 1, off:off + DK_B] = n0_ref[:, h:h + 1, :]
            m_ref[:, 0:1, 0:H_B] = m0_ref[...]

    yield
    tril, _, triu = _chunk_masks(L)

    chains = [(g, h) for g in range(G) for h in range(H_B)]
    nc = range(len(chains))
    c_old = [c_ref[g, h] for g, h in chains]
    n_old = [n_ref[g, h:h + 1, :] for g, h in chains]
    m_old = [m_ref[g, 0:1, h:h + 1] for g, h in chains]

    k0, v0 = H_B * DK_B, 2 * H_B * DK_B
    low_half = lax.broadcasted_iota(jnp.int32, (L, LANE), 1) < DK_B

    def own_lanes(pair, h):
        return jnp.where(low_half if h % 2 == 0 else jnp.logical_not(low_half), pair, 0.0)

    q = [own_lanes(xin_ref[g, :, (h // 2) * LANE:(h // 2 + 1) * LANE], h) * (DK_B ** -0.5) for g, h in chains]
    k = [own_lanes(xin_ref[g, :, k0 + (h // 2) * LANE:k0 + (h // 2 + 1) * LANE], h) for g, h in chains]
    v = [xin_ref[g, :, v0 + h * DV_B:v0 + (h + 1) * DV_B] for g, h in chains]
    ig_c, b_c, b_last, d_log = [], [], [], []
    for g in range(G):
        gact = gate_ref[g]
        gact_t = gatet_ref[g, 0]
        cum_c, cum_r = _chunk_cumsums(gact, gact_t, tril, triu, cumsum_cache, g)
        for h in range(H_B):
            ig_c.append(gact[:, 8 + h:9 + h])
            b_c.append(cum_c[:, 12 + h:13 + h])
            b_last.append(cum_c[L - 1:L, 12 + h:13 + h])
            d_log.append(jnp.where(tril, cum_c[:, 12 + h:13 + h] - cum_r[12 + h:13 + h, :]
                                   + gact_t[8 + h:9 + h, :], NEG))
        yield
    qk = [_dot(q[i], k[i], _NT) for i in nc]
    yield
    qc = [_dot(q[i], c_old[i]) for i in nc]
    yield
    inter = [b_c[i] + m_old[i] for i in nc]
    m_t = [jnp.maximum(inter[i], jnp.max(d_log[i], axis=-1, keepdims=True)) for i in nc]
    yield
    s = [qk[i] * jnp.exp(d_log[i] - m_t[i]) for i in nc]
    e_inter = [jnp.exp(inter[i] - m_t[i]) for i in nc]
    yield
    sv = [_dot(s[i], v[i]) for i in nc]
    yield
    m_new = [m_t[i][L - 1:L, :] for i in nc]
    kw = [k[i] * jnp.exp(b_last[i] - b_c[i] + ig_c[i] - m_new[i]) for i in nc]
    f_tot = [jnp.exp(b_last[i] + m_old[i] - m_new[i]) for i in nc]
    yield
    c_new = [f_tot[i] * c_old[i] + _dot(kw[i], v[i], _TN) for i in nc]
    yield
    n_new = [f_tot[i] * n_old[i] + jnp.sum(kw[i], axis=0, keepdims=True) for i in nc]
    qn = [jnp.sum(q[i] * n_old[i], axis=-1, keepdims=True) for i in nc]
    yield
    ssum = [jnp.sum(s[i], axis=-1, keepdims=True) for i in nc]
    yield
    den = [jnp.maximum(jnp.abs(e_inter[i] * qn[i] + ssum[i]), jnp.exp(-m_t[i])) for i in nc]
    hh = [(e_inter[i] * qc[i] + sv[i]) / den[i] for i in nc]
    yield
    ms = [jnp.mean(hh[i] * hh[i], axis=-1, keepdims=True) for i in nc]
    hn = [hh[i] * lax.rsqrt(ms[i] + EPS) for i in nc]
    yield
    for i, (g, h) in enumerate(chains):
        og = xin_ref[g, :, v0 + H_B * DV_B + h * DV_B:v0 + H_B * DV_B + (h + 1) * DV_B]
        mix_ref[g, :, h * DV_B:(h + 1) * DV_B] = (hn[i] * ng_ref[h:h + 1, :] * _sigmoid(og)).astype(mix_ref.dtype)
    yield
    for i, (g, h) in enumerate(chains):
        c_ref[g, h] = c_new[i]
        n_ref[g, h:h + 1, :] = n_new[i]
        m_ref[g, 0:1, h:h + 1] = m_new[i]
        off = (h % 2) * DK_B
        cnew_ref[g, h] = c_new[i][off:off + DK_B, :]
        nnew_ref[g, h:h + 1, :] = n_new[i][:, off:off + DK_B]
        mnew_ref[g, 0:1, h:h + 1] = m_new[i]


def _mixers_kernel(*refs, L, G, has_state):
    if has_state:
        (gdn_ref, gate_ref, gatet_ref, nga_ref, ml_ref, ngb_ref, cw_ref, cst_ref, s0_ref, c0_ref, n0_ref, m0_ref,
         mixa_ref, snew_ref, mixb_ref, cnew_ref, nnew_ref, mnew_ref, convnew_ref,
         s_ref, c_ref, n_ref, m_ref, xc_ref) = refs
        gdn_refs = (gdn_ref, gate_ref, gatet_ref, cw_ref, nga_ref, cst_ref, s0_ref,
                    mixa_ref, convnew_ref, snew_ref, xc_ref, s_ref)
        ml_refs = (ml_ref, gate_ref, gatet_ref, ngb_ref, c0_ref, n0_ref, m0_ref,
                   mixb_ref, cnew_ref, nnew_ref, mnew_ref, c_ref, n_ref, m_ref)
    else:
        (gdn_ref, gate_ref, gatet_ref, nga_ref, ml_ref, ngb_ref,
         mixa_ref, snew_ref, mixb_ref, cnew_ref, nnew_ref, mnew_ref, s_ref, c_ref, n_ref, m_ref) = refs
        gdn_refs = (gdn_ref, gate_ref, gatet_ref, nga_ref, mixa_ref, snew_ref, s_ref)
        ml_refs = (ml_ref, gate_ref, gatet_ref, ngb_ref, mixb_ref, cnew_ref, nnew_ref, mnew_ref, c_ref, n_ref, m_ref)
    cumsum_cache = {}
    _run_interleaved(_gdn_stages(*gdn_refs, L=L, G=G, has_state=has_state, cumsum_cache=cumsum_cache),
                     _mlstm_stages(*ml_refs, L=L, G=G, has_state=has_state, cumsum_cache=cumsum_cache))


def _mixers(gdn_in, ml_in, gates, gates_t, ng_a, ng_b, *, L, G, cw=None, state=None):
    n_seq, T, _ = gdn_in.shape
    n_c = T // L
    has_state = state is not None

    def seq_blk(*tail):
        return pl.BlockSpec((G,) + tail, lambda b, c: (b,) + (0,) * len(tail))

    def tok_blk(width):
        return pl.BlockSpec((G, L, width), lambda b, c: (b, c, 0))

    def seq_shape(*tail):
        return jax.ShapeDtypeStruct((n_seq,) + tail, F32)

    state_specs = [seq_blk(H_A, DK_A, DV_A), seq_blk(H_B, DK_B, DV_B), seq_blk(H_B, DK_B), seq_blk(1, H_B)]
    state_shapes = [seq_shape(H_A, DK_A, DV_A), seq_shape(H_B, DK_B, DV_B), seq_shape(H_B, DK_B), seq_shape(1, H_B)]
    conv_spec, conv_shape = seq_blk(CONV_W - 1, CONV_CH), seq_shape(CONV_W - 1, CONV_CH)
    in_specs = [
        tok_blk(GDN_W), tok_blk(LANE),
        pl.BlockSpec((G, 1, N_GATE, L), lambda b, c: (b, c, 0, 0)),
        pl.BlockSpec((1, DV_A), lambda b, c: (0, 0)),
        tok_blk(MLP_W),
        pl.BlockSpec((H_B, DV_B), lambda b, c: (0, 0)),
    ]
    args = [gdn_in, gates, gates_t, ng_a, ml_in, ng_b]
    out_specs = [tok_blk(H_A * DV_A), state_specs[0], tok_blk(H_B * DV_B)] + state_specs[1:]
    mix_dtype = BF16 if L % (2 * SUBLANE) == 0 else F32
    mix_a = jax.ShapeDtypeStruct((n_seq, T, H_A * DV_A), mix_dtype)
    mix_b = jax.ShapeDtypeStruct((n_seq, T, H_B * DV_B), mix_dtype)
    out_shape = [mix_a, state_shapes[0], mix_b] + state_shapes[1:]
    scratch = [pltpu.VMEM((G, H_A, DK_A, DV_A), F32), pltpu.VMEM((G, H_B, LANE, DV_B), F32),
               pltpu.VMEM((G, SUBLANE, LANE), F32), pltpu.VMEM((G, SUBLANE, LANE), F32)]
    if has_state:
        in_specs += [pl.BlockSpec((SUBLANE, CONV_CH), lambda b, c: (0, 0)), conv_spec] + state_specs
        args += [cw] + list(state)
        out_specs.append(conv_spec)
        out_shape.append(conv_shape)
        scratch.append(pltpu.VMEM((G, L + SUBLANE, CONV_CH), F32))
    return pl.pallas_call(
        functools.partial(_mixers_kernel, L=L, G=G, has_state=has_state),
        grid=(n_seq // G, n_c),
        in_specs=in_specs,
        out_specs=out_specs,
        out_shape=out_shape,
        scratch_shapes=scratch,
        compiler_params=pltpu.CompilerParams(dimension_semantics=("parallel", "arbitrary"),
                                             vmem_limit_bytes=VMEM_LIMIT),
        name=f"mixers_L{L}",
    )(*args)


def _outproj_kernel(xp_ref, xs_ref, map_ref, mas_ref, mbp_ref, mbs_ref, wo_ref, g_ref, rw_ref, rb_ref,
                    x1_ref, xsort_ref, info_ref, cpad_ref, *, n_p_tiles):
    half = H_A * DV_A
    tm = xp_ref.shape[0]

    def body(seg):
        x_ref, ma_ref, mb_ref = (xp_ref, xs_ref)[seg], (map_ref, mas_ref)[seg], (mbp_ref, mbs_ref)[seg]
        x1 = (x_ref[...] + jnp.dot(ma_ref[...].astype(BF16), wo_ref[:half, :], preferred_element_type=F32)
              + jnp.dot(mb_ref[...].astype(BF16), wo_ref[half:, :], preferred_element_type=F32))
        x1_ref[...] = x1
        hn = _rms(x1, g_ref[...])
        hn_hi = hn.astype(BF16)
        hn_lo = (hn - hn_hi.astype(F32)).astype(BF16)
        logits = (jnp.dot(hn_hi, rw_ref[0], preferred_element_type=F32)
                  + jnp.dot(hn_hi, rw_ref[1], preferred_element_type=F32)
                  + jnp.dot(hn_lo, rw_ref[0], preferred_element_type=F32)) + rb_ref[...]

        vals = logits.T[:N_EXPERTS, :]
        e_iota = lax.broadcasted_iota(jnp.int32, (N_EXPERTS, tm), 0)
        sels, tops = [], []
        for _ in range(TOP_K):
            m = jnp.max(vals, axis=0, keepdims=True)
            first = jnp.min(jnp.where(vals == m, e_iota, N_EXPERTS), axis=0, keepdims=True)
            sel = e_iota == first
            vals = jnp.where(sel, -jnp.inf, vals)
            sels.append(sel)
            tops.append(m)
        ex = [jnp.exp(t - tops[0]) for t in tops]
        den = ex[0] + ex[1] + ex[2] + ex[3]
        gates = [e / den for e in ex]
        mask = sels[0].astype(F32) + sels[1].astype(F32) + sels[2].astype(F32) + sels[3].astype(F32)
        ri = lax.broadcasted_iota(jnp.int32, (tm, tm), 0)
        ci = lax.broadcasted_iota(jnp.int32, (tm, tm), 1)
        rank = _dot(mask, (ri < ci).astype(F32))
        cnt = jnp.sum(mask, axis=1, keepdims=True)
        cpad = jnp.ceil(cnt * (1.0 / SEG_ALIGN)) * SEG_ALIGN
        cpad_b = jnp.broadcast_to(cpad, (N_EXPERTS, tm))
        er = lax.broadcasted_iota(jnp.int32, (N_EXPERTS, N_EXPERTS), 0)
        ec = lax.broadcasted_iota(jnp.int32, (N_EXPERTS, N_EXPERTS), 1)
        seg_off = _dot((er > ec).astype(F32), cpad_b)
        pos = seg_off + rank
        q = [jnp.sum(jnp.where(s, pos, 0.0), axis=0, keepdims=True) for s in sels]

        j_iota = lax.broadcasted_iota(jnp.int32, (MOE_CAP, tm), 0).astype(F32)
        perm = jnp.zeros((MOE_CAP, tm), F32)
        for kk in range(TOP_K):
            perm = jnp.where(j_iota == q[kk], 1.0, perm)
        xsorted = _dot(perm, hn)
        xsort_ref[...] = xsorted.astype(MOE_DTYPE)

        r_iota = lax.broadcasted_iota(jnp.int32, (LANE, tm), 0)
        info = jnp.zeros((LANE, tm), F32)
        for kk in range(TOP_K):
            info = jnp.where(r_iota == kk, q[kk], info)
            info = jnp.where(r_iota == TOP_K + kk, gates[kk], info)
        info_ref[...] = info.T
        cpad_ref[0] = cpad_b[:, :LANE]

    _for_segment(n_p_tiles, body)


def _outproj(xp, xs, ma_p, ma_s, mb_p, mb_s, w_out, g, rw, rb, tm):
    n_p, n_s = xp.shape[0], xs.shape[0]
    n = n_p + n_s
    nt = n // tm
    npt = n_p // tm
    half = H_A * DV_A
    return pl.pallas_call(
        functools.partial(_outproj_kernel, n_p_tiles=npt),
        grid=(nt,),
        in_specs=_two_segment_specs(tm, D_MODEL, npt) + _two_segment_specs(tm, half, npt)
        + _two_segment_specs(tm, half, npt) + [
            pl.BlockSpec((D_MODEL, D_MODEL), lambda i: (0, 0)),
            pl.BlockSpec((1, D_MODEL), lambda i: (0, 0)),
            pl.BlockSpec((2, D_MODEL, LANE), lambda i: (0, 0, 0)),
            pl.BlockSpec((1, LANE), lambda i: (0, 0)),
        ],
        out_specs=[
            pl.BlockSpec((tm, D_MODEL), lambda i: (i, 0)),
            pl.BlockSpec((MOE_CAP, D_MODEL), lambda i: (i, 0)),
            pl.BlockSpec((tm, LANE), lambda i: (i, 0)),
            pl.BlockSpec((1, N_EXPERTS, LANE), lambda i: (i, 0, 0)),
        ],
        out_shape=[
            jax.ShapeDtypeStruct((n, D_MODEL), F32),
            jax.ShapeDtypeStruct((nt * MOE_CAP, D_MODEL), MOE_DTYPE),
            jax.ShapeDtypeStruct((n, LANE), F32),
            jax.ShapeDtypeStruct((nt, N_EXPERTS, LANE), F32),
        ],
        compiler_params=pltpu.CompilerParams(dimension_semantics=("arbitrary",), vmem_limit_bytes=VMEM_LIMIT),
        name="outproj",
    )(xp, xs, ma_p, ma_s, mb_p, mb_s, w_out, g, rw, rb)


def _expert_kernel(be_ref, bj_ref, tf_ref, tl_ref, cov_ref, nu_ref, vt_ref, ct_ref, lt_ref, nx_ref, ws_ref,
                   xs_hbm, wgu_hbm, bgu_ref, wd_hbm, bd_ref, y_ref,
                   xbuf, gsem, wgu_st, wd_st, wsem, wgu_bf, wd_bf, *, nt):
    b = pl.program_id(0)
    n_used = nu_ref[0]
    slot = b % 2

    def start_pieces(bb, copy, s):
        e = be_ref[bb]
        base = bj_ref[bb] * MOE_BLK

        def body(t, carry):
            k = e * nt + t
            lo = jnp.maximum(vt_ref[k], base)
            ln = jnp.minimum(ct_ref[k], base + MOE_BLK) - lo

            @pl.when(ln > 0)
            def _():
                copy(s, pl.multiple_of(lt_ref[k] + lo, SEG_ALIGN), pl.multiple_of(lo - base, SEG_ALIGN),
                     pl.multiple_of(ln, SEG_ALIGN)).start()
            return carry

        lax.fori_loop(tf_ref[bb], tl_ref[bb] + 1, body, 0)

    def weight_copies(e):
        return (pltpu.make_async_copy(wgu_hbm.at[e], wgu_st, wsem.at[0]),
                pltpu.make_async_copy(wd_hbm.at[e], wd_st, wsem.at[1]))

    def cast_weights(p):
        wgu_bf[p] = wgu_st[...].astype(BF16)
        wd_bf[p] = wd_st[...].astype(BF16)

    def gather_copy(s, src, dst, size):
        return pltpu.make_async_copy(xs_hbm.at[pl.ds(src, size)], xbuf.at[s, pl.ds(dst, size)], gsem.at[s])

    def wait_rows(count, copy, s):
        @pl.when(count > 0)
        def _():
            copy(s, 0, 0, pl.multiple_of(count, SEG_ALIGN)).wait()

    @pl.when(b == 0)
    def _():
        xbuf[...] = jnp.zeros_like(xbuf)
        start_pieces(0, gather_copy, 0)

    @pl.when(b + 1 < n_used)
    def _():
        start_pieces(b + 1, gather_copy, 1 - slot)

    @pl.when(b >= n_used)
    def _():
        y_ref[...] = jnp.zeros_like(y_ref)

    @pl.when(b < n_used)
    def _():
        e = be_ref[b]
        first = jnp.logical_or(b == 0, be_ref[jnp.maximum(b - 1, 0)] != e)
        last = jnp.logical_or(b == n_used - 1, be_ref[jnp.minimum(b + 1, n_used - 1)] != e)
        has_next = nx_ref[b] < N_EXPERTS
        p = ws_ref[b]

        @pl.when(b == 0)
        def _():
            for cp in weight_copies(e):
                cp.start()
            for cp in weight_copies(e):
                cp.wait()
            cast_weights(p)

        @pl.when(jnp.logical_and(first, has_next))
        def _():
            for cp in weight_copies(nx_ref[b]):
                cp.start()

        wait_rows(cov_ref[b], gather_copy, slot)

        def expert_mlp(rows):
            hgu = jnp.dot(xbuf[slot, :rows].astype(BF16), wgu_bf[p], preferred_element_type=F32) + bgu_ref[0]
            gate = jnp.minimum(hgu[:, :D_FF], SWIGLU_LIMIT)
            up = jnp.clip(hgu[:, D_FF:], -SWIGLU_LIMIT, SWIGLU_LIMIT)
            act = (up + 1.0) * gate * _sigmoid(SWIGLU_ALPHA * gate)
            y = jnp.dot(act.astype(BF16), wd_bf[p], preferred_element_type=F32) + bd_ref[0]
            y_ref[:rows] = y.astype(MOE_DTYPE)
            if rows < MOE_BLK:
                y_ref[rows:] = jnp.zeros((MOE_BLK - rows, D_MODEL), MOE_DTYPE)

        quarter = MOE_BLK // 4
        for nq in range(1, 5):
            pl.when(jnp.logical_and(cov_ref[b] > (nq - 1) * quarter, cov_ref[b] <= nq * quarter))(
                functools.partial(expert_mlp, nq * quarter))

        @pl.when(jnp.logical_and(last, has_next))
        def _():
            for cp in weight_copies(nx_ref[b]):
                cp.wait()
            cast_weights(1 - p)


def _experts(tables, xs, w_gu, b_gu, w_down, b_down, nt):
    nb = tables[0].shape[0]

    def bias_blk(b, *t):
        return (t[0][jnp.minimum(b, t[5][0] - 1)], 0, 0)

    def out_blk(b, *t):
        return (b, 0)

    grid_spec = pltpu.PrefetchScalarGridSpec(
        num_scalar_prefetch=len(tables),
        grid=(nb,),
        in_specs=[
            pl.BlockSpec(memory_space=pl.ANY),
            pl.BlockSpec(memory_space=pl.ANY),
            pl.BlockSpec((1, 1, 2 * D_FF), bias_blk),
            pl.BlockSpec(memory_space=pl.ANY),
            pl.BlockSpec((1, 1, D_MODEL), bias_blk),
        ],
        out_specs=pl.BlockSpec((MOE_BLK, D_MODEL), out_blk),
        scratch_shapes=[
            pltpu.VMEM((2, MOE_BLK, D_MODEL), MOE_DTYPE),
            pltpu.SemaphoreType.DMA((2,)),
            pltpu.VMEM((D_MODEL, 2 * D_FF), F32),
            pltpu.VMEM((D_FF, D_MODEL), F32),
            pltpu.SemaphoreType.DMA((2,)),
            pltpu.VMEM((2, D_MODEL, 2 * D_FF), BF16),
            pltpu.VMEM((2, D_FF, D_MODEL), BF16),
        ],
    )
    return pl.pallas_call(
        functools.partial(_expert_kernel, nt=nt),
        grid_spec=grid_spec,
        out_shape=jax.ShapeDtypeStruct((nb * MOE_BLK, D_MODEL), MOE_DTYPE),
        compiler_params=pltpu.CompilerParams(dimension_semantics=("arbitrary",), vmem_limit_bytes=VMEM_LIMIT),
        name="experts",
    )(*tables, xs, w_gu, b_gu, w_down, b_down)


def _combine_kernel(src_ref, len_ref, off_ref, used_ref,
                    y_hbm, info_ref, x1_ref, pp_ref, ps_ref, gple_ref, wg_ref, wp_ref, gfin_ref,
                    outp_ref, outs_ref, ybuf, sem, *, n_p_tiles):
    tm = x1_ref.shape[0]
    i = pl.program_id(0)
    slot = i % 2

    def tile_copy(s, src, dst, size):
        return pltpu.make_async_copy(y_hbm.at[pl.ds(src, size)], ybuf.at[s, pl.ds(dst, size)], sem.at[s])

    def gather_tile(t, s):
        def body(e, carry):
            k = t * N_EXPERTS + e
            ln = len_ref[k]

            @pl.when(ln > 0)
            def _():
                tile_copy(s, pl.multiple_of(src_ref[k], SEG_ALIGN), pl.multiple_of(off_ref[k], SEG_ALIGN),
                          pl.multiple_of(ln, SEG_ALIGN)).start()
            return carry
        lax.fori_loop(0, N_EXPERTS, body, 0)

    @pl.when(i == 0)
    def _():
        ybuf[...] = jnp.zeros_like(ybuf)
        gather_tile(0, 0)

    @pl.when(i + 1 < pl.num_programs(0))
    def _():
        gather_tile(i + 1, 1 - slot)

    tile_copy(slot, 0, 0, pl.multiple_of(used_ref[i], SEG_ALIGN)).wait()
    ys_ref = ybuf.at[slot]

    def body(seg):
        p_ref, out_ref = (pp_ref, ps_ref)[seg], (outp_ref, outs_ref)[seg]
        info = info_ref[...]
        j_iota = lax.broadcasted_iota(jnp.int32, (tm, MOE_CAP), 1).astype(F32)
        gmat = jnp.zeros((tm, MOE_CAP), F32)
        for kk in range(TOP_K):
            gmat = jnp.where(j_iota == info[:, kk:kk + 1], info[:, TOP_K + kk:TOP_K + kk + 1], gmat)
        x2 = x1_ref[...] + jnp.dot(gmat.astype(BF16), ys_ref[...].astype(BF16), preferred_element_type=F32)
        hn = _rms(x2, gple_ref[...]).astype(BF16)
        gate = _sigmoid(jnp.dot(hn, wg_ref[...], preferred_element_type=F32))
        pe = jnp.dot(p_ref[...].astype(BF16), wp_ref[...], preferred_element_type=F32)
        x3 = x2 + gate * pe
        out_ref[...] = _rms(x3, gfin_ref[...])

    _for_segment(n_p_tiles, body)


def _combine(tile_tables, ys, info, x1, pp, ps, g_ple, w_gate, w_p, g_fin, tm):
    n_p, n_s = pp.shape[0], ps.shape[0]
    n = n_p + n_s
    nt = n // tm
    npt = n_p // tm
    grid_spec = pltpu.PrefetchScalarGridSpec(
        num_scalar_prefetch=len(tile_tables),
        grid=(nt,),
        in_specs=[
            pl.BlockSpec(memory_space=pl.ANY),
            pl.BlockSpec((tm, LANE), lambda i, *_: (i, 0)),
            pl.BlockSpec((tm, D_MODEL), lambda i, *_: (i, 0)),
        ] + _two_segment_specs(tm, PLE_DIM, npt) + [
            pl.BlockSpec((1, D_MODEL), lambda i, *_: (0, 0)),
            pl.BlockSpec((D_MODEL, D_MODEL), lambda i, *_: (0, 0)),
            pl.BlockSpec((PLE_DIM, D_MODEL), lambda i, *_: (0, 0)),
            pl.BlockSpec((1, D_MODEL), lambda i, *_: (0, 0)),
        ],
        out_specs=_two_segment_specs(tm, D_MODEL, npt),
        scratch_shapes=[pltpu.VMEM((2, MOE_CAP, D_MODEL), MOE_DTYPE), pltpu.SemaphoreType.DMA((2,))],
    )
    return pl.pallas_call(
        functools.partial(_combine_kernel, n_p_tiles=npt),
        grid_spec=grid_spec,
        out_shape=[jax.ShapeDtypeStruct((n_p, D_MODEL), F32), jax.ShapeDtypeStruct((n_s, D_MODEL), F32)],
        compiler_params=pltpu.CompilerParams(dimension_semantics=("arbitrary",), vmem_limit_bytes=VMEM_LIMIT),
        name="combine",
    )(*tile_tables, ys, info, x1, pp, ps, g_ple, w_gate, w_p, g_fin)


def _block_tables(seg_len, nb):
    nt = seg_len.shape[0]
    seg_off = jnp.cumsum(seg_len, axis=1) - seg_len
    seg_end = jnp.cumsum(seg_len, axis=0).T
    seg_start = seg_end - seg_len.T
    n_rows = seg_end[:, -1]
    n_blk = (n_rows + MOE_BLK - 1) // MOE_BLK
    blk_end = jnp.cumsum(n_blk)
    b = jnp.arange(nb, dtype=jnp.int32)
    block_e = jnp.minimum(jnp.sum((blk_end[None, :] <= b[:, None]).astype(jnp.int32), axis=1), N_EXPERTS - 1)
    idx = jnp.where(n_blk > 0, jnp.arange(N_EXPERTS, dtype=jnp.int32), N_EXPERTS)
    nxt = jnp.concatenate([lax.cummin(idx, axis=0, reverse=True)[1:], jnp.full((1,), N_EXPERTS, jnp.int32)])
    parity = (jnp.cumsum((n_blk > 0).astype(jnp.int32)) - 1) % 2
    per_e = jnp.concatenate([jnp.stack([blk_end - n_blk, n_rows, nxt, parity], axis=1), seg_start, seg_end],
                            axis=1).astype(F32)
    onehot = (block_e[:, None] == jnp.arange(N_EXPERTS, dtype=jnp.int32)[None, :]).astype(F32)
    per_b = jnp.dot(onehot, per_e, precision=HI).astype(jnp.int32)
    block_j = b - per_b[:, 0]
    base = block_j * MOE_BLK
    t_first = jnp.sum((per_b[:, 4 + nt:] <= base[:, None]).astype(jnp.int32), axis=1)
    t_last = jnp.sum((per_b[:, 4:4 + nt] < (base + MOE_BLK)[:, None]).astype(jnp.int32), axis=1) - 1
    cover = jnp.clip(per_b[:, 1] - base, 0, MOE_BLK)
    seg_shift = (jnp.arange(nt, dtype=jnp.int32)[:, None] * MOE_CAP + seg_off).T - seg_start
    tables = (block_e, block_j, t_first, t_last, cover, blk_end[-1:], seg_start.reshape(-1),
              seg_end.reshape(-1), seg_shift.reshape(-1), per_b[:, 2], per_b[:, 3])
    y_src = ((blk_end - n_blk) * MOE_BLK)[None, :] + seg_start.T
    tile_tables = (y_src.reshape(-1), seg_len.reshape(-1), seg_off.reshape(-1), jnp.sum(seg_len, axis=1))
    return tuple(t.astype(jnp.int32) for t in tables), tuple(t.astype(jnp.int32) for t in tile_tables)


def _rearranged_in_weights(w_in):
    o = np.cumsum([0, CONV_CH, H_A * DV_A, H_A, H_A, H_B * DK_B, H_B * DK_B, H_B * DV_B, H_B * DV_B, H_B, H_B])
    conv_in, z_a, a_a, b_a, q_b, k_b, v_b, o_b, i_b, f_b = (w_in[:, int(o[j]):int(o[j + 1])] for j in range(10))
    small = jnp.concatenate([a_a, b_a, i_b, f_b], axis=1)
    w_gdn = w_in[:, :GDN_W]
    w_rest = jnp.concatenate([q_b, k_b, v_b, o_b, small, jnp.zeros((D_MODEL, LANE - N_GATE), w_in.dtype)], axis=1)
    return w_gdn.astype(BF16), w_rest.astype(BF16), small.T.astype(BF16)


def _gate_params(a_log, dt_bias, i_bias, f_bias):
    z4 = jnp.zeros((4,), F32)
    alog = jnp.concatenate([a_log.astype(F32), z4, z4, z4])
    bias = jnp.concatenate([dt_bias.astype(F32), z4, i_bias.astype(F32), f_bias.astype(F32)])
    pad = jnp.zeros((LANE - N_GATE,), F32)
    pcol = jnp.zeros((SUBLANE, LANE), F32).at[0].set(jnp.concatenate([alog, pad])).at[1].set(
        jnp.concatenate([bias, pad]))
    prow = jnp.zeros((N_GATE, LANE), F32).at[:, 0].set(alog).at[:, 1].set(bias)
    return pcol, prow


def kernel(x_prompt, x_sample, p_prompt, p_sample, state_conv, state_gdn, state_mlstm_c, state_mlstm_n, state_mlstm_m, norm_attn_g, w_in, conv_w, gdn_a_log, gdn_dt_bias, gdn_norm_g, mlstm_i_bias, mlstm_f_bias, mlstm_norm_g, w_out, norm_moe_g, router_w, router_b, expert_w_gu, expert_b_gu, expert_w_down, expert_b_down, norm_ple_g, ple_gate_w, ple_w, final_norm_g):
    bp, tp, _ = x_prompt.shape
    bs, ts, _ = x_sample.shape
    n_p, n_s = bp * tp, bs * ts
    n = n_p + n_s
    lp, ls = min(tp, CHUNK), min(ts, CHUNK)
    tm = 256
    gp = 4 if bp % 4 == 0 else 1
    gs = 16 if bs % 16 == 0 else 1
    assert tp % lp == 0 and ts % ls == 0 and tp % tm == 0 and n_s % tm == 0 and ls % SUBLANE == 0

    xp = x_prompt.reshape(n_p, D_MODEL)
    xs = x_sample.reshape(n_s, D_MODEL)

    w_gdn, w_rest, ws_t = _rearranged_in_weights(w_in[0])
    pcol, prow = _gate_params(gdn_a_log[0], gdn_dt_bias[0], mlstm_i_bias[0], mlstm_f_bias[0])
    cw = jnp.zeros((SUBLANE, CONV_CH), F32).at[:CONV_W].set(conv_w[0].astype(F32))
    gdn_p, gdn_s, ml_p, ml_s, gate_p, gate_s, gatet_p, gatet_s, conv_p = _inproj(
        xp, xs, norm_attn_g[0].reshape(1, D_MODEL), w_gdn, w_rest, ws_t, pcol, prow, cw, tm, bp)
    gt_p = gatet_p.reshape(N_GATE, bp, tp // lp, lp).transpose(1, 2, 0, 3)
    gt_s = gatet_s.reshape(N_GATE, bs, ts // ls, ls).transpose(1, 2, 0, 3)

    ng_a = gdn_norm_g[0].reshape(1, DV_A).astype(F32)
    ng_b = mlstm_norm_g[0].reshape(H_B, DV_B).astype(F32)
    ma_p, gdn_st_p, mb_p, c_p, nn_p, m_p = _mixers(
        gdn_p.reshape(bp, tp, GDN_W), ml_p.reshape(bp, tp, MLP_W), gate_p.reshape(bp, tp, LANE), gt_p, ng_a, ng_b,
        L=lp, G=gp)
    ma_s, gdn_st_s, mb_s, c_s, nn_s, m_s, conv_s = _mixers(
        gdn_s.reshape(bs, ts, GDN_W), ml_s.reshape(bs, ts, MLP_W), gate_s.reshape(bs, ts, LANE), gt_s, ng_a, ng_b,
        L=ls, G=gs, cw=cw,
        state=(state_conv[0], state_gdn[0], state_mlstm_c[0], state_mlstm_n[0], state_mlstm_m[0].reshape(bs, 1, H_B)))
    half = H_A * DV_A

    rw = jnp.zeros((D_MODEL, LANE), F32).at[:, :N_EXPERTS].set(router_w[0])
    rw_hi = rw.astype(BF16)
    rw = jnp.stack([rw_hi, (rw - rw_hi.astype(F32)).astype(BF16)])
    rb = jnp.full((1, LANE), NEG, F32).at[0, :N_EXPERTS].set(router_b[0])
    x1, x_sorted, info, seg_len = _outproj(xp, xs, ma_p.reshape(n_p, half), ma_s.reshape(n_s, half),
                                           mb_p.reshape(n_p, half), mb_s.reshape(n_s, half),
                                           w_out[0].astype(BF16), norm_moe_g[0].reshape(1, D_MODEL), rw, rb, MOE_TM)

    nt = n // MOE_TM
    nb = -(-(n * TOP_K + nt * N_EXPERTS * (SEG_ALIGN - 1)) // MOE_BLK) + N_EXPERTS
    tables, tile_tables = _block_tables(seg_len[:, :, 0].astype(jnp.int32), nb)
    y_blocks = _experts(tables, x_sorted, expert_w_gu[0], expert_b_gu[0].reshape(N_EXPERTS, 1, 2 * D_FF),
                        expert_w_down[0], expert_b_down[0].reshape(N_EXPERTS, 1, D_MODEL), nt)
    y_p, y_s = _combine(tile_tables, y_blocks, info, x1, p_prompt[0].reshape(n_p, PLE_DIM),
                        p_sample[0].reshape(n_s, PLE_DIM), norm_ple_g[0].reshape(1, D_MODEL),
                        ple_gate_w[0].astype(BF16), ple_w[0].astype(BF16), final_norm_g.reshape(1, D_MODEL), MOE_TM)

    return (y_p.reshape(bp, tp, D_MODEL), y_s.reshape(bs, ts, D_MODEL),
            conv_p[None], gdn_st_p[None], c_p[None], nn_p[None], m_p.reshape(1, bp, H_B),
            conv_s[None], gdn_st_s[None], c_s[None], nn_s[None], m_s.reshape(1, bs, H_B))
```

```python
import functools

import numpy as np
import jax
import jax.numpy as jnp
from jax import lax
from jax.experimental import pallas as pl
from jax.experimental.pallas import tpu as pltpu

F32 = jnp.float32
BF16 = jnp.bfloat16

D_MODEL = 1024
H_A, DK_A, DV_A = 4, 128, 128
H_B, DK_B, DV_B = 4, 64, 128
CONV_W = 4
CONV_CH = H_A * (2 * DK_A + DV_A)
N_EXPERTS = 32
TOP_K = 4
D_FF = 1024
SWIGLU_LIMIT = 7.0
SWIGLU_ALPHA = 1.702
PLE_DIM = 256
EPS = 1e-6
NEG = -1e30
CHUNK = 64

LANE = 128
SUBLANE = 8
GDN_W = CONV_CH + H_A * DV_A
MLP_W = 2 * H_B * DK_B + 2 * H_B * DV_B
N_GATE = 16
PROJ_CHUNK = 512

VMEM_LIMIT = 48 * 1024 * 1024

MOE_TM = 256
RING = 3
MOE_BLK = 512
MOE_DTYPE = F32
SEG_ALIGN = SUBLANE
MOE_CAP = -(-(MOE_TM * TOP_K + N_EXPERTS * (SEG_ALIGN - 1)) // LANE) * LANE

HI = lax.Precision.HIGHEST

_NN = (((1,), (0,)), ((), ()))
_NT = (((1,), (1,)), ((), ()))
_TN = (((0,), (0,)), ((), ()))


def _dot(a, b, dims=_NN):
    return lax.dot_general(a.astype(BF16), b.astype(BF16), dims, preferred_element_type=F32)


def _split3(x):
    hi = x.astype(BF16)
    r = x - hi.astype(F32)
    mid = r.astype(BF16)
    return hi, mid, (r - mid.astype(F32)).astype(BF16)


def _chunk_cumsums(gact, gact_t, tril, triu, cache, key):
    if key not in cache:
        tril_b, triu_b = tril.astype(BF16), triu.astype(BF16)
        cum_c = sum(jnp.dot(tril_b, part, preferred_element_type=F32) for part in _split3(gact))
        cum_r = sum(jnp.dot(part, triu_b, preferred_element_type=F32) for part in _split3(gact_t))
        cache[key] = (cum_c, cum_r)
    return cache[key]


def _rms(x, g):
    return x * lax.rsqrt(jnp.mean(x * x, axis=-1, keepdims=True) + EPS) * g


def _softplus(t):
    return jnp.maximum(t, 0.0) + jnp.log1p(jnp.exp(-jnp.abs(t)))


def _sigmoid(t):
    return 1.0 / (1.0 + jnp.exp(-t))


def _silu(t):
    return t * _sigmoid(t)


def _activate_gates(raw, idx, alog, bias):
    t = raw + bias
    g = -jnp.exp(alog) * _softplus(t)
    beta = _sigmoid(t)
    lf = -_softplus(-t)
    return jnp.where(idx < 4, g, jnp.where(idx < 8, beta, jnp.where(idx < 12, t, lf)))


def _two_segment_specs(tm, width, n_p_tiles):
    return [pl.BlockSpec((tm, width), lambda i, *_: (jnp.minimum(i, n_p_tiles - 1), 0)),
            pl.BlockSpec((tm, width), lambda i, *_: (jnp.maximum(i - n_p_tiles, 0), 0))]


def _for_segment(n_p_tiles, body):
    i = pl.program_id(0)

    @pl.when(i < n_p_tiles)
    def _():
        body(0)

    @pl.when(i >= n_p_tiles)
    def _():
        body(1)


def _gdn_preactivate_stages(raw_ref, xc_ref, cw_ref, out_ref, cnew_ref, first_of_seq):
    tm = raw_ref.shape[0]
    xc_ref[0:SUBLANE, :] = jnp.where(first_of_seq, 0.0, xc_ref[tm:tm + SUBLANE, :])
    xc_ref[SUBLANE:SUBLANE + tm, :] = raw_ref[:, :CONV_CH]
    out_ref[:, CONV_CH:] = _silu(raw_ref[:, CONV_CH:])
    cnew_ref[0] = xc_ref[SUBLANE + tm - (CONV_W - 1):SUBLANE + tm, :]
    yield
    base = SUBLANE - (CONV_W - 1)
    for c0 in range(0, CONV_CH, DK_A):
        conv = xc_ref[base:base + tm, c0:c0 + DK_A] * cw_ref[0:1, c0:c0 + DK_A]
        for j in range(1, CONV_W):
            conv = conv + xc_ref[base + j:base + j + tm, c0:c0 + DK_A] * cw_ref[j:j + 1, c0:c0 + DK_A]
        act = _silu(conv)
        if c0 < H_A * DK_A:
            act = act * (lax.rsqrt(jnp.sum(act * act, axis=-1, keepdims=True) + EPS) * (DK_A ** -0.5))
        elif c0 < 2 * H_A * DK_A:
            act = act * lax.rsqrt(jnp.sum(act * act, axis=-1, keepdims=True) + EPS)
        out_ref[:, c0:c0 + DK_A] = act
        yield


def _inproj_kernel(xp_ref, xs_ref, g_ref, wa_ref, wb_ref, wst_ref, pc_ref, pr_ref, cw_ref,
                   gdnp_ref, gdns_ref, mlp_ref, mls_ref, gatep_ref, gates_ref, gatetp_ref, gatets_ref, cnew_ref,
                   xc_ref, raw_ref, *, n_p_tiles, tiles_per_seq):
    tm = xp_ref.shape[0]
    i = pl.program_id(0)

    @pl.when(i == 0)
    def _():
        xc_ref[...] = jnp.zeros_like(xc_ref)
        raw_ref[...] = jnp.zeros_like(raw_ref)

    def preactivate_previous_tile():
        return _gdn_preactivate_stages(raw_ref, xc_ref, cw_ref, gdnp_ref, cnew_ref, (i - 1) % tiles_per_seq == 0)

    def projection_stages(seg):
        x_ref = (xp_ref, xs_ref)[seg]
        gdn_dst, ml_ref = (raw_ref, gdns_ref)[seg], (mlp_ref, mls_ref)[seg]
        gate_ref, gatet_ref = (gatep_ref, gates_ref)[seg], (gatetp_ref, gatets_ref)[seg]
        hn = _rms(x_ref[...], g_ref[...]).astype(BF16)
        yield
        for c0 in range(0, GDN_W, PROJ_CHUNK):
            gdn_dst[:, c0:c0 + PROJ_CHUNK] = jnp.dot(hn, wa_ref[:, c0:c0 + PROJ_CHUNK], preferred_element_type=F32)
            yield
        for c0 in range(0, MLP_W, PROJ_CHUNK):
            ml_ref[:, c0:c0 + PROJ_CHUNK] = jnp.dot(hn, wb_ref[:, c0:c0 + PROJ_CHUNK], preferred_element_type=F32)
            yield
        raw = jnp.dot(hn, wb_ref[:, MLP_W:], preferred_element_type=F32)
        lane = lax.broadcasted_iota(jnp.int32, (tm, LANE), 1)
        gate_ref[...] = _activate_gates(raw, lane, pc_ref[0:1, :], pc_ref[1:2, :])
        raw_t = lax.dot_general(wst_ref[...], hn, _NT, preferred_element_type=F32)
        row = lax.broadcasted_iota(jnp.int32, (N_GATE, tm), 0)
        gatet_ref[...] = _activate_gates(raw_t, row, pr_ref[:, 0:1], pr_ref[:, 1:2])

    def body(seg):
        if seg == 0:
            _run_interleaved(preactivate_previous_tile(), projection_stages(0))
        else:
            pl.when(i == n_p_tiles)(lambda: _run_interleaved(preactivate_previous_tile()))
            _run_interleaved(projection_stages(1))

    _for_segment(n_p_tiles, body)


def _inproj(xp, xs, g, w_gdn, w_rest, ws_t, pcol, prow, cw, tm, n_seq_p):
    n_p, n_s = xp.shape[0], xs.shape[0]
    npt = n_p // tm
    tiles_per_seq = npt // n_seq_p

    def out2(width):
        return _two_segment_specs(tm, width, npt)

    def shp2(width):
        return [jax.ShapeDtypeStruct((n_p, width), F32), jax.ShapeDtypeStruct((n_s, width), F32)]

    def prev_tile(i):
        return jnp.clip(i - 1, 0, npt - 1)

    return pl.pallas_call(
        functools.partial(_inproj_kernel, n_p_tiles=npt, tiles_per_seq=tiles_per_seq),
        grid=((n_p + n_s) // tm,),
        in_specs=_two_segment_specs(tm, D_MODEL, npt) + [
            pl.BlockSpec((1, D_MODEL), lambda i: (0, 0)),
            pl.BlockSpec((D_MODEL, GDN_W), lambda i: (0, 0)),
            pl.BlockSpec((D_MODEL, MLP_W + LANE), lambda i: (0, 0)),
            pl.BlockSpec((N_GATE, D_MODEL), lambda i: (0, 0)),
            pl.BlockSpec((SUBLANE, LANE), lambda i: (0, 0)),
            pl.BlockSpec((N_GATE, LANE), lambda i: (0, 0)),
            pl.BlockSpec((SUBLANE, CONV_CH), lambda i: (0, 0)),
        ],
        out_specs=[
            pl.BlockSpec((tm, GDN_W), lambda i: (prev_tile(i), 0)),
            pl.BlockSpec((tm, GDN_W), lambda i: (jnp.maximum(i - npt, 0), 0)),
        ] + out2(MLP_W) + out2(LANE) + [
            pl.BlockSpec((N_GATE, tm), lambda i: (0, jnp.minimum(i, npt - 1))),
            pl.BlockSpec((N_GATE, tm), lambda i: (0, jnp.maximum(i - npt, 0))),
            pl.BlockSpec((1, CONV_W - 1, CONV_CH), lambda i: (prev_tile(i) // tiles_per_seq, 0, 0)),
        ],
        out_shape=shp2(GDN_W) + shp2(MLP_W) + shp2(LANE) + [
            jax.ShapeDtypeStruct((N_GATE, n_p), F32), jax.ShapeDtypeStruct((N_GATE, n_s), F32),
            jax.ShapeDtypeStruct((n_seq_p, CONV_W - 1, CONV_CH), F32)],
        scratch_shapes=[pltpu.VMEM((tm + SUBLANE, CONV_CH), F32), pltpu.VMEM((tm, GDN_W), F32)],
        compiler_params=pltpu.CompilerParams(dimension_semantics=("arbitrary",), vmem_limit_bytes=VMEM_LIMIT),
        name="inproj",
    )(xp, xs, g, w_gdn, w_rest, ws_t, pcol, prow, cw)


def _chunk_masks(L):
    ri = lax.broadcasted_iota(jnp.int32, (L, L), 0)
    ci = lax.broadcasted_iota(jnp.int32, (L, L), 1)
    return ri >= ci, ri > ci, ri <= ci


def _run_interleaved(*stage_generators):
    live = list(stage_generators)
    while live:
        for gen in list(live):
            if next(gen, StopIteration) is StopIteration:
                live.remove(gen)


def _gdn_stages(*refs, L, G, has_state, cumsum_cache):
    if has_state:
        (xin_ref, gate_ref, gatet_ref, cw_ref, ng_ref, cst_ref, s0_ref,
         mix_ref, cnew_ref, snew_ref, xc_ref, s_ref) = refs
    else:
        xin_ref, gate_ref, gatet_ref, ng_ref, mix_ref, snew_ref, s_ref = refs
    c = pl.program_id(1)

    @pl.when(c == 0)
    def _():
        if has_state:
            xc_ref[:, 0:SUBLANE, :] = jnp.zeros((G, SUBLANE, CONV_CH), F32)
            xc_ref[:, SUBLANE - (CONV_W - 1):SUBLANE, :] = cst_ref[...]
            s_ref[...] = s0_ref[...]
        else:
            s_ref[...] = jnp.zeros_like(s_ref)

    if has_state:
        @pl.when(c > 0)
        def _():
            xc_ref[:, 0:SUBLANE, :] = xc_ref[:, L:L + SUBLANE, :]

    yield
    tril, strict, triu = _chunk_masks(L)
    base = SUBLANE - (CONV_W - 1)

    chains = [(g, h) for g in range(G) for h in range(H_A)]
    s_old = [s_ref[g, h] for g, h in chains]
    if has_state:
        for g in range(G):
            xc_ref[g, SUBLANE:SUBLANE + L, :] = xin_ref[g, :, :CONV_CH]

    q, k, v, beta, gc, gl, decay = [], [], [], [], [], [], []
    for g in range(G):
        if has_state:
            conv = xc_ref[g, base:base + L, :] * cw_ref[0:1, :]
            for j in range(1, CONV_W):
                conv = conv + xc_ref[g, base + j:base + j + L, :] * cw_ref[j:j + 1, :]
            cnew_ref[g] = xc_ref[g, SUBLANE + L - (CONV_W - 1):SUBLANE + L, :]
            act = _silu(conv)
        else:
            act = xin_ref[g, :, :CONV_CH]
        gact = gate_ref[g]
        cum_c, cum_r = _chunk_cumsums(gact, gatet_ref[g, 0], tril, triu, cumsum_cache, g)
        for h in range(H_A):
            q.append(act[:, h * DK_A:(h + 1) * DK_A])
            k.append(act[:, H_A * DK_A + h * DK_A:H_A * DK_A + (h + 1) * DK_A])
            v.append(act[:, 2 * H_A * DK_A + h * DV_A:2 * H_A * DK_A + (h + 1) * DV_A])
            beta.append(gact[:, 4 + h:5 + h])
            gc.append(cum_c[:, h:h + 1])
            gl.append(cum_c[L - 1:L, h:h + 1])
            gr = cum_r[h:h + 1, :]
            decay.append(jnp.where(tril, jnp.exp(jnp.where(tril, cum_c[:, h:h + 1] - gr, 0.0)), 0.0))
        yield

    nc = range(len(chains))
    if has_state:
        qss = [jnp.sum(q[i] * q[i], axis=-1, keepdims=True) for i in nc]
        kss = [jnp.sum(k[i] * k[i], axis=-1, keepdims=True) for i in nc]
        q = [q[i] * (lax.rsqrt(qss[i] + EPS) * (DK_A ** -0.5)) for i in nc]
        k = [k[i] * lax.rsqrt(kss[i] + EPS) for i in nc]
    kb = [k[i] * beta[i] for i in nc]
    egc = [jnp.exp(gc[i]) for i in nc]
    yield
    kk = [_dot(kb[i], k[i], _NT) for i in nc]
    yield
    qk = [_dot(q[i], k[i], _NT) for i in nc]
    yield
    eye = (lax.broadcasted_iota(jnp.int32, (L, L), 0) == lax.broadcasted_iota(jnp.int32, (L, L), 1)).astype(F32)
    pw = [-jnp.where(strict, kk[i] * decay[i], 0.0) for i in nc]
    t_inv = [eye + pw[i] for i in nc]
    span = 2
    while span < L:
        yield
        pw = [_dot(pw[i], pw[i]) for i in nc]
        yield
        t_inv = [t_inv[i] + _dot(t_inv[i], pw[i]) for i in nc]
        span *= 2
    yield
    sol = [_dot(t_inv[i], jnp.concatenate([v[i] * beta[i], kb[i] * egc[i]], axis=-1)) for i in nc]
    yield
    qs = [_dot(q[i] * egc[i], s_old[i]) for i in nc]
    yield
    ws = [_dot(sol[i][:, DV_A:], s_old[i]) for i in nc]
    v_new = [sol[i][:, :DV_A] - ws[i] for i in nc]
    yield
    o = [qs[i] + _dot(jnp.where(tril, qk[i] * decay[i], 0.0), v_new[i]) for i in nc]
    yield
    s_new = [s_old[i] * jnp.exp(gl[i]) + _dot(k[i] * jnp.exp(gl[i] - gc[i]), v_new[i], _TN) for i in nc]
    yield
    ms = [jnp.mean(o[i] * o[i], axis=-1, keepdims=True) for i in nc]
    on = [o[i] * lax.rsqrt(ms[i] + EPS) for i in nc]
    yield
    for i, (g, h) in enumerate(chains):
        z = xin_ref[g, :, CONV_CH + h * DV_A:CONV_CH + (h + 1) * DV_A]
        out = on[i] * ng_ref[...] * (_silu(z) if has_state else z)
        mix_ref[g, :, h * DV_A:(h + 1) * DV_A] = out.astype(mix_ref.dtype)
    yield
    for i, (g, h) in enumerate(chains):
        s_ref[g, h] = s_new[i]
        snew_ref[g, h] = s_new[i]


def _mlstm_stages(*refs, L, G, has_state, cumsum_cache):
    if has_state:
        (xin_ref, gate_ref, gatet_ref, ng_ref, c0_ref, n0_ref, m0_ref,
         mix_ref, cnew_ref, nnew_ref, mnew_ref, c_ref, n_ref, m_ref) = refs
    else:
        (xin_ref, gate_ref, gatet_ref, ng_ref,
         mix_ref, cnew_ref, nnew_ref, mnew_ref, c_ref, n_ref, m_ref) = refs
    c = pl.program_id(1)

    @pl.when(c == 0)
    def _():
        c_ref[...] = jnp.zeros_like(c_ref)
        n_ref[...] = jnp.zeros_like(n_ref)
        m_ref[...] = jnp.zeros_like(m_ref)
        if has_state:
            for h in range(H_B):
                off = (h % 2) * DK_B
                c_ref[:, h, off:off + DK_B, :] = c0_ref[:, h]
                n_ref[:, h:h + 1, off:off + DK_B] = n0_ref[:, h:h + 1, :]
            m_ref[:, 0:1, 0:H_B] = m0_ref[...]

    yield
    tril, _, triu = _chunk_masks(L)

    chains = [(g, h) for g in range(G) for h in range(H_B)]
    nc = range(len(chains))
    c_old = [c_ref[g, h] for g, h in chains]
    n_old = [n_ref[g, h:h + 1, :] for g, h in chains]
    m_old = [m_ref[g, 0:1, h:h + 1] for g, h in chains]

    k0, v0 = H_B * DK_B, 2 * H_B * DK_B
    low_half = lax.broadcasted_iota(jnp.int32, (L, LANE), 1) < DK_B

    def own_lanes(pair, h):
        return jnp.where(low_half if h % 2 == 0 else jnp.logical_not(low_half), pair, 0.0)

    q = [own_lanes(xin_ref[g, :, (h // 2) * LANE:(h // 2 + 1) * LANE], h) * (DK_B ** -0.5) for g, h in chains]
    k = [own_lanes(xin_ref[g, :, k0 + (h // 2) * LANE:k0 + (h // 2 + 1) * LANE], h) for g, h in chains]
    v = [xin_ref[g, :, v0 + h * DV_B:v0 + (h + 1) * DV_B] for g, h in chains]
    ig_c, b_c, b_last, d_log = [], [], [], []
    for g in range(G):
        gact = gate_ref[g]
        gact_t = gatet_ref[g, 0]
        cum_c, cum_r = _chunk_cumsums(gact, gact_t, tril, triu, cumsum_cache, g)
        for h in range(H_B):
            ig_c.append(gact[:, 8 + h:9 + h])
            b_c.append(cum_c[:, 12 + h:13 + h])
            b_last.append(cum_c[L - 1:L, 12 + h:13 + h])
            d_log.append(jnp.where(tril, cum_c[:, 12 + h:13 + h] - cum_r[12 + h:13 + h, :]
                                   + gact_t[8 + h:9 + h, :], NEG))
        yield
    qk = [_dot(q[i], k[i], _NT) for i in nc]
    yield
    qc = [_dot(q[i], c_old[i]) for i in nc]
    yield
    inter = [b_c[i] + m_old[i] for i in nc]
    m_t = [jnp.maximum(inter[i], jnp.max(d_log[i], axis=-1, keepdims=True)) for i in nc]
    yield
    s = [qk[i] * jnp.exp(d_log[i] - m_t[i]) for i in nc]
    e_inter = [jnp.exp(inter[i] - m_t[i]) for i in nc]
    yield
    sv = [_dot(s[i], v[i]) for i in nc]
    yield
    m_new = [m_t[i][L - 1:L, :] for i in nc]
    kw = [k[i] * jnp.exp(b_last[i] - b_c[i] + ig_c[i] - m_new[i]) for i in nc]
    f_tot = [jnp.exp(b_last[i] + m_old[i] - m_new[i]) for i in nc]
    yield
    c_new = [f_tot[i] * c_old[i] + _dot(kw[i], v[i], _TN) for i in nc]
    yield
    n_new = [f_tot[i] * n_old[i] + jnp.sum(kw[i], axis=0, keepdims=True) for i in nc]
    qn = [jnp.sum(q[i] * n_old[i], axis=-1, keepdims=True) for i in nc]
    yield
    ssum = [jnp.sum(s[i], axis=-1, keepdims=True) for i in nc]
    yield
    den = [jnp.maximum(jnp.abs(e_inter[i] * qn[i] + ssum[i]), jnp.exp(-m_t[i])) for i in nc]
    hh = [(e_inter[i] * qc[i] + sv[i]) / den[i] for i in nc]
    yield
    ms = [jnp.mean(hh[i] * hh[i], axis=-1, keepdims=True) for i in nc]
    hn = [hh[i] * lax.rsqrt(ms[i] + EPS) for i in nc]
    yield
    for i, (g, h) in enumerate(chains):
        og = xin_ref[g, :, v0 + H_B * DV_B + h * DV_B:v0 + H_B * DV_B + (h + 1) * DV_B]
        mix_ref[g, :, h * DV_B:(h + 1) * DV_B] = (hn[i] * ng_ref[h:h + 1, :] * _sigmoid(og)).astype(mix_ref.dtype)
    yield
    for i, (g, h) in enumerate(chains):
        c_ref[g, h] = c_new[i]
        n_ref[g, h:h + 1, :] = n_new[i]
        m_ref[g, 0:1, h:h + 1] = m_new[i]
        off = (h % 2) * DK_B
        cnew_ref[g, h] = c_new[i][off:off + DK_B, :]
        nnew_ref[g, h:h + 1, :] = n_new[i][:, off:off + DK_B]
        mnew_ref[g, 0:1, h:h + 1] = m_new[i]


def _mixers_kernel(*refs, L, G, has_state):
    if has_state:
        (gdn_ref, gate_ref, gatet_ref, nga_ref, ml_ref, ngb_ref, cw_ref, cst_ref, s0_ref, c0_ref, n0_ref, m0_ref,
         mixa_ref, snew_ref, mixb_ref, cnew_ref, nnew_ref, mnew_ref, convnew_ref,
         s_ref, c_ref, n_ref, m_ref, xc_ref) = refs
        gdn_refs = (gdn_ref, gate_ref, gatet_ref, cw_ref, nga_ref, cst_ref, s0_ref,
                    mixa_ref, convnew_ref, snew_ref, xc_ref, s_ref)
        ml_refs = (ml_ref, gate_ref, gatet_ref, ngb_ref, c0_ref, n0_ref, m0_ref,
                   mixb_ref, cnew_ref, nnew_ref, mnew_ref, c_ref, n_ref, m_ref)
    else:
        (gdn_ref, gate_ref, gatet_ref, nga_ref, ml_ref, ngb_ref,
         mixa_ref, snew_ref, mixb_ref, cnew_ref, nnew_ref, mnew_ref, s_ref, c_ref, n_ref, m_ref) = refs
        gdn_refs = (gdn_ref, gate_ref, gatet_ref, nga_ref, mixa_ref, snew_ref, s_ref)
        ml_refs = (ml_ref, gate_ref, gatet_ref, ngb_ref, mixb_ref, cnew_ref, nnew_ref, mnew_ref, c_ref, n_ref, m_ref)
    cumsum_cache = {}
    _run_interleaved(_gdn_stages(*gdn_refs, L=L, G=G, has_state=has_state, cumsum_cache=cumsum_cache),
                     _mlstm_stages(*ml_refs, L=L, G=G, has_state=has_state, cumsum_cache=cumsum_cache))


def _mixers(gdn_in, ml_in, gates, gates_t, ng_a, ng_b, *, L, G, cw=None, state=None):
    n_seq, T, _ = gdn_in.shape
    n_c = T // L
    has_state = state is not None

    def seq_blk(*tail):
        return pl.BlockSpec((G,) + tail, lambda b, c: (b,) + (0,) * len(tail))

    def tok_blk(width):
        return pl.BlockSpec((G, L, width), lambda b, c: (b, c, 0))

    def seq_shape(*tail):
        return jax.ShapeDtypeStruct((n_seq,) + tail, F32)

    state_specs = [seq_blk(H_A, DK_A, DV_A), seq_blk(H_B, DK_B, DV_B), seq_blk(H_B, DK_B), seq_blk(1, H_B)]
    state_shapes = [seq_shape(H_A, DK_A, DV_A), seq_shape(H_B, DK_B, DV_B), seq_shape(H_B, DK_B), seq_shape(1, H_B)]
    conv_spec, conv_shape = seq_blk(CONV_W - 1, CONV_CH), seq_shape(CONV_W - 1, CONV_CH)
    in_specs = [
        tok_blk(GDN_W), tok_blk(LANE),
        pl.BlockSpec((G, 1, N_GATE, L), lambda b, c: (b, c, 0, 0)),
        pl.BlockSpec((1, DV_A), lambda b, c: (0, 0)),
        tok_blk(MLP_W),
        pl.BlockSpec((H_B, DV_B), lambda b, c: (0, 0)),
    ]
    args = [gdn_in, gates, gates_t, ng_a, ml_in, ng_b]
    out_specs = [tok_blk(H_A * DV_A), state_specs[0], tok_blk(H_B * DV_B)] + state_specs[1:]
    mix_dtype = BF16 if L % (2 * SUBLANE) == 0 else F32
    mix_a = jax.ShapeDtypeStruct((n_seq, T, H_A * DV_A), mix_dtype)
    mix_b = jax.ShapeDtypeStruct((n_seq, T, H_B * DV_B), mix_dtype)
    out_shape = [mix_a, state_shapes[0], mix_b] + state_shapes[1:]
    scratch = [pltpu.VMEM((G, H_A, DK_A, DV_A), F32), pltpu.VMEM((G, H_B, LANE, DV_B), F32),
               pltpu.VMEM((G, SUBLANE, LANE), F32), pltpu.VMEM((G, SUBLANE, LANE), F32)]
    if has_state:
        in_specs += [pl.BlockSpec((SUBLANE, CONV_CH), lambda b, c: (0, 0)), conv_spec] + state_specs
        args += [cw] + list(state)
        out_specs.append(conv_spec)
        out_shape.append(conv_shape)
        scratch.append(pltpu.VMEM((G, L + SUBLANE, CONV_CH), F32))
    return pl.pallas_call(
        functools.partial(_mixers_kernel, L=L, G=G, has_state=has_state),
        grid=(n_seq // G, n_c),
        in_specs=in_specs,
        out_specs=out_specs,
        out_shape=out_shape,
        scratch_shapes=scratch,
        compiler_params=pltpu.CompilerParams(dimension_semantics=("parallel", "arbitrary"),
                                             vmem_limit_bytes=VMEM_LIMIT),
        name=f"mixers_L{L}",
    )(*args)


def _outproj_kernel(xp_ref, xs_ref, map_ref, mas_ref, mbp_ref, mbs_ref, wo_ref, g_ref, rw_ref, rb_ref,
                    x1_ref, xsort_ref, info_ref, cpad_ref, *, n_p_tiles):
    half = H_A * DV_A
    tm = xp_ref.shape[0]

    def body(seg):
        x_ref, ma_ref, mb_ref = (xp_ref, xs_ref)[seg], (map_ref, mas_ref)[seg], (mbp_ref, mbs_ref)[seg]
        x1 = (x_ref[...] + jnp.dot(ma_ref[...].astype(BF16), wo_ref[:half, :], preferred_element_type=F32)
              + jnp.dot(mb_ref[...].astype(BF16), wo_ref[half:, :], preferred_element_type=F32))
        x1_ref[...] = x1
        hn = _rms(x1, g_ref[...])
        hn_hi = hn.astype(BF16)
        hn_lo = (hn - hn_hi.astype(F32)).astype(BF16)
        logits = (jnp.dot(hn_hi, rw_ref[0], preferred_element_type=F32)
                  + jnp.dot(hn_hi, rw_ref[1], preferred_element_type=F32)
                  + jnp.dot(hn_lo, rw_ref[0], preferred_element_type=F32)) + rb_ref[...]

        vals = logits.T[:N_EXPERTS, :]
        e_iota = lax.broadcasted_iota(jnp.int32, (N_EXPERTS, tm), 0)
        sels, tops = [], []
        for _ in range(TOP_K):
            m = jnp.max(vals, axis=0, keepdims=True)
            first = jnp.min(jnp.where(vals == m, e_iota, N_EXPERTS), axis=0, keepdims=True)
            sel = e_iota == first
            vals = jnp.where(sel, -jnp.inf, vals)
            sels.append(sel)
            tops.append(m)
        ex = [jnp.exp(t - tops[0]) for t in tops]
        den = ex[0] + ex[1] + ex[2] + ex[3]
        gates = [e / den for e in ex]
        mask = sels[0].astype(F32) + sels[1].astype(F32) + sels[2].astype(F32) + sels[3].astype(F32)
        ri = lax.broadcasted_iota(jnp.int32, (tm, tm), 0)
        ci = lax.broadcasted_iota(jnp.int32, (tm, tm), 1)
        rank = _dot(mask, (ri < ci).astype(F32))
        cnt = jnp.sum(mask, axis=1, keepdims=True)
        cpad = jnp.ceil(cnt * (1.0 / SEG_ALIGN)) * SEG_ALIGN
        cpad_b = jnp.broadcast_to(cpad, (N_EXPERTS, tm))
        er = lax.broadcasted_iota(jnp.int32, (N_EXPERTS, N_EXPERTS), 0)
        ec = lax.broadcasted_iota(jnp.int32, (N_EXPERTS, N_EXPERTS), 1)
        seg_off = _dot((er > ec).astype(F32), cpad_b)
        pos = seg_off + rank
        q = [jnp.sum(jnp.where(s, pos, 0.0), axis=0, keepdims=True) for s in sels]

        j_iota = lax.broadcasted_iota(jnp.int32, (MOE_CAP, tm), 0).astype(F32)
        perm = jnp.zeros((MOE_CAP, tm), F32)
        for kk in range(TOP_K):
            perm = jnp.where(j_iota == q[kk], 1.0, perm)
        xsorted = _dot(perm, hn)
        xsort_ref[...] = xsorted.astype(MOE_DTYPE)

        r_iota = lax.broadcasted_iota(jnp.int32, (LANE, tm), 0)
        info = jnp.zeros((LANE, tm), F32)
        for kk in range(TOP_K):
            info = jnp.where(r_iota == kk, q[kk], info)
            info = jnp.where(r_iota == TOP_K + kk, gates[kk], info)
        info_ref[...] = info.T
        cpad_ref[0] = cpad_b[:, :LANE]

    _for_segment(n_p_tiles, body)


def _outproj(xp, xs, ma_p, ma_s, mb_p, mb_s, w_out, g, rw, rb, tm):
    n_p, n_s = xp.shape[0], xs.shape[0]
    n = n_p + n_s
    nt = n // tm
    npt = n_p // tm
    half = H_A * DV_A
    return pl.pallas_call(
        functools.partial(_outproj_kernel, n_p_tiles=npt),
        grid=(nt,),
        in_specs=_two_segment_specs(tm, D_MODEL, npt) + _two_segment_specs(tm, half, npt)
        + _two_segment_specs(tm, half, npt) + [
            pl.BlockSpec((D_MODEL, D_MODEL), lambda i: (0, 0)),
            pl.BlockSpec((1, D_MODEL), lambda i: (0, 0)),
            pl.BlockSpec((2, D_MODEL, LANE), lambda i: (0, 0, 0)),
            pl.BlockSpec((1, LANE), lambda i: (0, 0)),
        ],
        out_specs=[
            pl.BlockSpec((tm, D_MODEL), lambda i: (i, 0)),
            pl.BlockSpec((MOE_CAP, D_MODEL), lambda i: (i, 0)),
            pl.BlockSpec((tm, LANE), lambda i: (i, 0)),
            pl.BlockSpec((1, N_EXPERTS, LANE), lambda i: (i, 0, 0)),
        ],
        out_shape=[
            jax.ShapeDtypeStruct((n, D_MODEL), F32),
            jax.ShapeDtypeStruct((nt * MOE_CAP, D_MODEL), MOE_DTYPE),
            jax.ShapeDtypeStruct((n, LANE), F32),
            jax.ShapeDtypeStruct((nt, N_EXPERTS, LANE), F32),
        ],
        compiler_params=pltpu.CompilerParams(dimension_semantics=("arbitrary",), vmem_limit_bytes=VMEM_LIMIT),
        name="outproj",
    )(xp, xs, ma_p, ma_s, mb_p, mb_s, w_out, g, rw, rb)


def _expert_kernel(be_ref, bj_ref, tf_ref, tl_ref, cov_ref, nu_ref, vt_ref, ct_ref, lt_ref, nx_ref, ws_ref,
                   xs_hbm, wgu_hbm, bgu_ref, wd_hbm, bd_ref, y_ref,
                   xbuf, gsem, wgu_st, wd_st, wsem, wgu_bf, wd_bf, *, nt):
    b = pl.program_id(0)
    n_used = nu_ref[0]
    slot = b % 2

    def start_pieces(bb, copy, s):
        e = be_ref[bb]
        base = bj_ref[bb] * MOE_BLK

        def body(t, carry):
            k = e * nt + t
            lo = jnp.maximum(vt_ref[k], base)
            ln = jnp.minimum(ct_ref[k], base + MOE_BLK) - lo

            @pl.when(ln > 0)
            def _():
                copy(s, pl.multiple_of(lt_ref[k] + lo, SEG_ALIGN), pl.multiple_of(lo - base, SEG_ALIGN),
                     pl.multiple_of(ln, SEG_ALIGN)).start()
            return carry

        lax.fori_loop(tf_ref[bb], tl_ref[bb] + 1, body, 0)

    def weight_copies(e):
        return (pltpu.make_async_copy(wgu_hbm.at[e], wgu_st, wsem.at[0]),
                pltpu.make_async_copy(wd_hbm.at[e], wd_st, wsem.at[1]))

    def cast_weights(p):
        wgu_bf[p] = wgu_st[...].astype(BF16)
        wd_bf[p] = wd_st[...].astype(BF16)

    def gather_copy(s, src, dst, size):
        return pltpu.make_async_copy(xs_hbm.at[pl.ds(src, size)], xbuf.at[s, pl.ds(dst, size)], gsem.at[s])

    def wait_rows(count, copy, s):
        @pl.when(count > 0)
        def _():
            copy(s, 0, 0, pl.multiple_of(count, SEG_ALIGN)).wait()

    @pl.when(b == 0)
    def _():
        xbuf[...] = jnp.zeros_like(xbuf)
        start_pieces(0, gather_copy, 0)

    @pl.when(b + 1 < n_used)
    def _():
        start_pieces(b + 1, gather_copy, 1 - slot)

    @pl.when(b >= n_used)
    def _():
        y_ref[...] = jnp.zeros_like(y_ref)

    @pl.when(b < n_used)
    def _():
        e = be_ref[b]
        first = jnp.logical_or(b == 0, be_ref[jnp.maximum(b - 1, 0)] != e)
        last = jnp.logical_or(b == n_used - 1, be_ref[jnp.minimum(b + 1, n_used - 1)] != e)
        has_next = nx_ref[b] < N_EXPERTS
        p = ws_ref[b]

        @pl.when(b == 0)
        def _():
            for cp in weight_copies(e):
                cp.start()
            for cp in weight_copies(e):
                cp.wait()
            cast_weights(p)

        @pl.when(jnp.logical_and(first, has_next))
        def _():
            for cp in weight_copies(nx_ref[b]):
                cp.start()

        wait_rows(cov_ref[b], gather_copy, slot)

        def expert_mlp(rows):
            hgu = jnp.dot(xbuf[slot, :rows].astype(BF16), wgu_bf[p], preferred_element_type=F32) + bgu_ref[0]
            gate = jnp.minimum(hgu[:, :D_FF], SWIGLU_LIMIT)
            up = jnp.clip(hgu[:, D_FF:], -SWIGLU_LIMIT, SWIGLU_LIMIT)
            act = (up + 1.0) * gate * _sigmoid(SWIGLU_ALPHA * gate)
            y = jnp.dot(act.astype(BF16), wd_bf[p], preferred_element_type=F32) + bd_ref[0]
            y_ref[:rows] = y.astype(MOE_DTYPE)
            if rows < MOE_BLK:
                y_ref[rows:] = jnp.zeros((MOE_BLK - rows, D_MODEL), MOE_DTYPE)

        quarter = MOE_BLK // 4
        for nq in range(1, 5):
            pl.when(jnp.logical_and(cov_ref[b] > (nq - 1) * quarter, cov_ref[b] <= nq * quarter))(
                functools.partial(expert_mlp, nq * quarter))

        @pl.when(jnp.logical_and(last, has_next))
        def _():
            for cp in weight_copies(nx_ref[b]):
                cp.wait()
            cast_weights(1 - p)


def _experts(tables, xs, w_gu, b_gu, w_down, b_down, nt):
    nb = tables[0].shape[0]

    def bias_blk(b, *t):
        return (t[0][jnp.minimum(b, t[5][0] - 1)], 0, 0)

    def out_blk(b, *t):
        return (b, 0)

    grid_spec = pltpu.PrefetchScalarGridSpec(
        num_scalar_prefetch=len(tables),
        grid=(nb,),
        in_specs=[
            pl.BlockSpec(memory_space=pl.ANY),
            pl.BlockSpec(memory_space=pl.ANY),
            pl.BlockSpec((1, 1, 2 * D_FF), bias_blk),
            pl.BlockSpec(memory_space=pl.ANY),
            pl.BlockSpec((1, 1, D_MODEL), bias_blk),
        ],
        out_specs=pl.BlockSpec((MOE_BLK, D_MODEL), out_blk),
        scratch_shapes=[
            pltpu.VMEM((2, MOE_BLK, D_MODEL), MOE_DTYPE),
            pltpu.SemaphoreType.DMA((2,)),
            pltpu.VMEM((D_MODEL, 2 * D_FF), F32),
            pltpu.VMEM((D_FF, D_MODEL), F32),
            pltpu.SemaphoreType.DMA((2,)),
            pltpu.VMEM((2, D_MODEL, 2 * D_FF), BF16),
            pltpu.VMEM((2, D_FF, D_MODEL), BF16),
        ],
    )
    return pl.pallas_call(
        functools.partial(_expert_kernel, nt=nt),
        grid_spec=grid_spec,
        out_shape=jax.ShapeDtypeStruct((nb * MOE_BLK, D_MODEL), MOE_DTYPE),
        compiler_params=pltpu.CompilerParams(dimension_semantics=("arbitrary",), vmem_limit_bytes=VMEM_LIMIT),
        name="experts",
    )(*tables, xs, w_gu, b_gu, w_down, b_down)


def _combine_kernel(src_ref, len_ref, off_ref, used_ref,
                    y_hbm, info_ref, x1_hbm, pp_ref, ps_ref, gple_ref, wg_ref, wp_ref, gfin_ref,
                    outp_ref, outs_ref, ybuf, sem, x1buf, xsem, *, n_p_tiles):
    tm = x1buf.shape[1]
    i = pl.program_id(0)
    n_i = pl.num_programs(0)
    slot = i % RING

    def tile_copy(s, src, dst, size):
        return pltpu.make_async_copy(y_hbm.at[pl.ds(src, size)], ybuf.at[s, pl.ds(dst, size)], sem.at[s])

    def x1_copy(t, s):
        return pltpu.make_async_copy(x1_hbm.at[pl.ds(pl.multiple_of(t * tm, tm), tm)], x1buf.at[s], xsem.at[s])

    def gather_tile(t, s):
        x1_copy(t, s).start()

        def body(e, carry):
            k = t * N_EXPERTS + e
            ln = len_ref[k]

            @pl.when(ln > 0)
            def _():
                tile_copy(s, pl.multiple_of(src_ref[k], SEG_ALIGN), pl.multiple_of(off_ref[k], SEG_ALIGN),
                          pl.multiple_of(ln, SEG_ALIGN)).start()
            return carry
        lax.fori_loop(0, N_EXPERTS, body, 0)

    @pl.when(i == 0)
    def _():
        ybuf[...] = jnp.zeros_like(ybuf)
        for t in range(RING - 1):
            pl.when(t < n_i)(functools.partial(gather_tile, t, t))

    @pl.when(i + RING - 1 < n_i)
    def _():
        gather_tile(i + RING - 1, (i + RING - 1) % RING)

    tile_copy(slot, 0, 0, pl.multiple_of(used_ref[i], SEG_ALIGN)).wait()
    x1_copy(i, slot).wait()
    ys_ref, x1_ref = ybuf.at[slot], x1buf.at[slot]

    def body(seg):
        p_ref, out_ref = (pp_ref, ps_ref)[seg], (outp_ref, outs_ref)[seg]
        info = info_ref[...]
        j_iota = lax.broadcasted_iota(jnp.int32, (tm, MOE_CAP), 1).astype(F32)
        gmat = jnp.zeros((tm, MOE_CAP), F32)
        for kk in range(TOP_K):
            gmat = jnp.where(j_iota == info[:, kk:kk + 1], info[:, TOP_K + kk:TOP_K + kk + 1], gmat)
        x2 = x1_ref[...] + jnp.dot(gmat.astype(BF16), ys_ref[...].astype(BF16), preferred_element_type=F32)
        hn = _rms(x2, gple_ref[...]).astype(BF16)
        gate = _sigmoid(jnp.dot(hn, wg_ref[...], preferred_element_type=F32))
        pe = jnp.dot(p_ref[...].astype(BF16), wp_ref[...], preferred_element_type=F32)
        x3 = x2 + gate * pe
        out_ref[...] = _rms(x3, gfin_ref[...])

    _for_segment(n_p_tiles, body)


def _combine(tile_tables, ys, info, x1, pp, ps, g_ple, w_gate, w_p, g_fin, tm):
    n_p, n_s = pp.shape[0], ps.shape[0]
    n = n_p + n_s
    nt = n // tm
    npt = n_p // tm
    grid_spec = pltpu.PrefetchScalarGridSpec(
        num_scalar_prefetch=len(tile_tables),
        grid=(nt,),
        in_specs=[
            pl.BlockSpec(memory_space=pl.ANY),
            pl.BlockSpec((tm, LANE), lambda i, *_: (i, 0)),
            pl.BlockSpec(memory_space=pl.ANY),
        ] + _two_segment_specs(tm, PLE_DIM, npt) + [
            pl.BlockSpec((1, D_MODEL), lambda i, *_: (0, 0)),
            pl.BlockSpec((D_MODEL, D_MODEL), lambda i, *_: (0, 0)),
            pl.BlockSpec((PLE_DIM, D_MODEL), lambda i, *_: (0, 0)),
            pl.BlockSpec((1, D_MODEL), lambda i, *_: (0, 0)),
        ],
        out_specs=_two_segment_specs(tm, D_MODEL, npt),
        scratch_shapes=[pltpu.VMEM((RING, MOE_CAP, D_MODEL), MOE_DTYPE), pltpu.SemaphoreType.DMA((RING,)),
                        pltpu.VMEM((RING, tm, D_MODEL), F32), pltpu.SemaphoreType.DMA((RING,))],
    )
    return pl.pallas_call(
        functools.partial(_combine_kernel, n_p_tiles=npt),
        grid_spec=grid_spec,
        out_shape=[jax.ShapeDtypeStruct((n_p, D_MODEL), F32), jax.ShapeDtypeStruct((n_s, D_MODEL), F32)],
        compiler_params=pltpu.CompilerParams(dimension_semantics=("arbitrary",), vmem_limit_bytes=VMEM_LIMIT),
        name="combine",
    )(*tile_tables, ys, info, x1, pp, ps, g_ple, w_gate, w_p, g_fin)


def _block_tables(seg_len, nb):
    nt = seg_len.shape[0]
    seg_off = jnp.cumsum(seg_len, axis=1) - seg_len
    seg_end = jnp.cumsum(seg_len, axis=0).T
    seg_start = seg_end - seg_len.T
    n_rows = seg_end[:, -1]
    n_blk = (n_rows + MOE_BLK - 1) // MOE_BLK
    blk_end = jnp.cumsum(n_blk)
    b = jnp.arange(nb, dtype=jnp.int32)
    block_e = jnp.minimum(jnp.sum((blk_end[None, :] <= b[:, None]).astype(jnp.int32), axis=1), N_EXPERTS - 1)
    idx = jnp.where(n_blk > 0, jnp.arange(N_EXPERTS, dtype=jnp.int32), N_EXPERTS)
    nxt = jnp.concatenate([lax.cummin(idx, axis=0, reverse=True)[1:], jnp.full((1,), N_EXPERTS, jnp.int32)])
    parity = (jnp.cumsum((n_blk > 0).astype(jnp.int32)) - 1) % 2
    per_e = jnp.concatenate([jnp.stack([blk_end - n_blk, n_rows, nxt, parity], axis=1), seg_start, seg_end],
                            axis=1).astype(F32)
    onehot = (block_e[:, None] == jnp.arange(N_EXPERTS, dtype=jnp.int32)[None, :]).astype(F32)
    per_b = jnp.dot(onehot, per_e, precision=HI).astype(jnp.int32)
    block_j = b - per_b[:, 0]
    base = block_j * MOE_BLK
    t_first = jnp.sum((per_b[:, 4 + nt:] <= base[:, None]).astype(jnp.int32), axis=1)
    t_last = jnp.sum((per_b[:, 4:4 + nt] < (base + MOE_BLK)[:, None]).astype(jnp.int32), axis=1) - 1
    cover = jnp.clip(per_b[:, 1] - base, 0, MOE_BLK)
    seg_shift = (jnp.arange(nt, dtype=jnp.int32)[:, None] * MOE_CAP + seg_off).T - seg_start
    tables = (block_e, block_j, t_first, t_last, cover, blk_end[-1:], seg_start.reshape(-1),
              seg_end.reshape(-1), seg_shift.reshape(-1), per_b[:, 2], per_b[:, 3])
    y_src = ((blk_end - n_blk) * MOE_BLK)[None, :] + seg_start.T
    tile_tables = (y_src.reshape(-1), seg_len.reshape(-1), seg_off.reshape(-1), jnp.sum(seg_len, axis=1))
    return tuple(t.astype(jnp.int32) for t in tables), tuple(t.astype(jnp.int32) for t in tile_tables)


def _rearranged_in_weights(w_in):
    o = np.cumsum([0, CONV_CH, H_A * DV_A, H_A, H_A, H_B * DK_B, H_B * DK_B, H_B * DV_B, H_B * DV_B, H_B, H_B])
    conv_in, z_a, a_a, b_a, q_b, k_b, v_b, o_b, i_b, f_b = (w_in[:, int(o[j]):int(o[j + 1])] for j in range(10))
    small = jnp.concatenate([a_a, b_a, i_b, f_b], axis=1)
    w_gdn = w_in[:, :GDN_W]
    w_rest = jnp.concatenate([q_b, k_b, v_b, o_b, small, jnp.zeros((D_MODEL, LANE - N_GATE), w_in.dtype)], axis=1)
    return w_gdn.astype(BF16), w_rest.astype(BF16), small.T.astype(BF16)


def _gate_params(a_log, dt_bias, i_bias, f_bias):
    z4 = jnp.zeros((4,), F32)
    alog = jnp.concatenate([a_log.astype(F32), z4, z4, z4])
    bias = jnp.concatenate([dt_bias.astype(F32), z4, i_bias.astype(F32), f_bias.astype(F32)])
    pad = jnp.zeros((LANE - N_GATE,), F32)
    pcol = jnp.zeros((SUBLANE, LANE), F32).at[0].set(jnp.concatenate([alog, pad])).at[1].set(
        jnp.concatenate([bias, pad]))
    prow = jnp.zeros((N_GATE, LANE), F32).at[:, 0].set(alog).at[:, 1].set(bias)
    return pcol, prow


def kernel(x_prompt, x_sample, p_prompt, p_sample, state_conv, state_gdn, state_mlstm_c, state_mlstm_n, state_mlstm_m, norm_attn_g, w_in, conv_w, gdn_a_log, gdn_dt_bias, gdn_norm_g, mlstm_i_bias, mlstm_f_bias, mlstm_norm_g, w_out, norm_moe_g, router_w, router_b, expert_w_gu, expert_b_gu, expert_w_down, expert_b_down, norm_ple_g, ple_gate_w, ple_w, final_norm_g):
    bp, tp, _ = x_prompt.shape
    bs, ts, _ = x_sample.shape
    n_p, n_s = bp * tp, bs * ts
    n = n_p + n_s
    lp, ls = min(tp, CHUNK), min(ts, CHUNK)
    tm = 256
    gp = 4 if bp % 4 == 0 else 1
    gs = 16 if bs % 16 == 0 else 1
    assert tp % lp == 0 and ts % ls == 0 and tp % tm == 0 and n_s % tm == 0 and ls % SUBLANE == 0

    xp = x_prompt.reshape(n_p, D_MODEL)
    xs = x_sample.reshape(n_s, D_MODEL)

    w_gdn, w_rest, ws_t = _rearranged_in_weights(w_in[0])
    pcol, prow = _gate_params(gdn_a_log[0], gdn_dt_bias[0], mlstm_i_bias[0], mlstm_f_bias[0])
    cw = jnp.zeros((SUBLANE, CONV_CH), F32).at[:CONV_W].set(conv_w[0].astype(F32))
    gdn_p, gdn_s, ml_p, ml_s, gate_p, gate_s, gatet_p, gatet_s, conv_p = _inproj(
        xp, xs, norm_attn_g[0].reshape(1, D_MODEL), w_gdn, w_rest, ws_t, pcol, prow, cw, tm, bp)
    gt_p = gatet_p.reshape(N_GATE, bp, tp // lp, lp).transpose(1, 2, 0, 3)
    gt_s = gatet_s.reshape(N_GATE, bs, ts // ls, ls).transpose(1, 2, 0, 3)

    ng_a = gdn_norm_g[0].reshape(1, DV_A).astype(F32)
    ng_b = mlstm_norm_g[0].reshape(H_B, DV_B).astype(F32)
    ma_p, gdn_st_p, mb_p, c_p, nn_p, m_p = _mixers(
        gdn_p.reshape(bp, tp, GDN_W), ml_p.reshape(bp, tp, MLP_W), gate_p.reshape(bp, tp, LANE), gt_p, ng_a, ng_b,
        L=lp, G=gp)
    ma_s, gdn_st_s, mb_s, c_s, nn_s, m_s, conv_s = _mixers(
        gdn_s.reshape(bs, ts, GDN_W), ml_s.reshape(bs, ts, MLP_W), gate_s.reshape(bs, ts, LANE), gt_s, ng_a, ng_b,
        L=ls, G=gs, cw=cw,
        state=(state_conv[0], state_gdn[0], state_mlstm_c[0], state_mlstm_n[0], state_mlstm_m[0].reshape(bs, 1, H_B)))
    half = H_A * DV_A

    rw = jnp.zeros((D_MODEL, LANE), F32).at[:, :N_EXPERTS].set(router_w[0])
    rw_hi = rw.astype(BF16)
    rw = jnp.stack([rw_hi, (rw - rw_hi.astype(F32)).astype(BF16)])
    rb = jnp.full((1, LANE), NEG, F32).at[0, :N_EXPERTS].set(router_b[0])
    x1, x_sorted, info, seg_len = _outproj(xp, xs, ma_p.reshape(n_p, half), ma_s.reshape(n_s, half),
                                           mb_p.reshape(n_p, half), mb_s.reshape(n_s, half),
                                           w_out[0].astype(BF16), norm_moe_g[0].reshape(1, D_MODEL), rw, rb, MOE_TM)

    nt = n // MOE_TM
    nb = -(-(n * TOP_K + nt * N_EXPERTS * (SEG_ALIGN - 1)) // MOE_BLK) + N_EXPERTS
    tables, tile_tables = _block_tables(seg_len[:, :, 0].astype(jnp.int32), nb)
    y_blocks = _experts(tables, x_sorted, expert_w_gu[0], expert_b_gu[0].reshape(N_EXPERTS, 1, 2 * D_FF),
                        expert_w_down[0], expert_b_down[0].reshape(N_EXPERTS, 1, D_MODEL), nt)
    y_p, y_s = _combine(tile_tables, y_blocks, info, x1, p_prompt[0].reshape(n_p, PLE_DIM),
                        p_sample[0].reshape(n_s, PLE_DIM), norm_ple_g[0].reshape(1, D_MODEL),
                        ple_gate_w[0].astype(BF16), ple_w[0].astype(BF16), final_norm_g.reshape(1, D_MODEL), MOE_TM)

    return (y_p.reshape(bp, tp, D_MODEL), y_s.reshape(bs, ts, D_MODEL),
            conv_p[None], gdn_st_p[None], c_p[None], nn_p[None], m_p.reshape(1, bp, H_B),
            conv_s[None], gdn_st_s[None], c_s[None], nn_s[None], m_s.reshape(1, bs, H_B))
```

```python
import functools

import numpy as np
import jax
import jax.numpy as jnp
from jax import lax
from jax.experimental import pallas as pl
from jax.experimental.pallas import tpu as pltpu

F32 = jnp.float32
BF16 = jnp.bfloat16

D_MODEL = 1024
H_A, DK_A, DV_A = 4, 128, 128
H_B, DK_B, DV_B = 4, 64, 128
CONV_W = 4
CONV_CH = H_A * (2 * DK_A + DV_A)
N_EXPERTS = 32
TOP_K = 4
D_FF = 1024
SWIGLU_LIMIT = 7.0
SWIGLU_ALPHA = 1.702
PLE_DIM = 256
EPS = 1e-6
NEG = -1e30
CHUNK = 64

LANE = 128
SUBLANE = 8
GDN_W = CONV_CH + H_A * DV_A
MLP_W = 2 * H_B * DK_B + 2 * H_B * DV_B
N_GATE = 16
PROJ_CHUNK = 512

VMEM_LIMIT = 48 * 1024 * 1024

MOE_TM = 256
MOE_BLK = 512
MOE_DTYPE = F32
SEG_ALIGN = SUBLANE
MOE_CAP = -(-(MOE_TM * TOP_K + N_EXPERTS * (SEG_ALIGN - 1)) // LANE) * LANE

HI = lax.Precision.HIGHEST

_NN = (((1,), (0,)), ((), ()))
_NT = (((1,), (1,)), ((), ()))
_TN = (((0,), (0,)), ((), ()))


def _dot(a, b, dims=_NN):
    return lax.dot_general(a.astype(BF16), b.astype(BF16), dims, preferred_element_type=F32)


def _split3(x):
    hi = x.astype(BF16)
    r = x - hi.astype(F32)
    mid = r.astype(BF16)
    return hi, mid, (r - mid.astype(F32)).astype(BF16)


def _chunk_cumsums(gact, gact_t, tril, triu, cache, key):
    if key not in cache:
        tril_b, triu_b = tril.astype(BF16), triu.astype(BF16)
        cum_c = sum(jnp.dot(tril_b, part, preferred_element_type=F32) for part in _split3(gact))
        cum_r = sum(jnp.dot(part, triu_b, preferred_element_type=F32) for part in _split3(gact_t))
        cache[key] = (cum_c, cum_r)
    return cache[key]


def _rms(x, g):
    return x * lax.rsqrt(jnp.mean(x * x, axis=-1, keepdims=True) + EPS) * g


def _softplus(t):
    return jnp.maximum(t, 0.0) + jnp.log1p(jnp.exp(-jnp.abs(t)))


def _sigmoid(t):
    return 1.0 / (1.0 + jnp.exp(-t))


def _silu(t):
    return t * _sigmoid(t)


def _activate_gates(raw, idx, alog, bias):
    t = raw + bias
    g = -jnp.exp(alog) * _softplus(t)
    beta = _sigmoid(t)
    lf = -_softplus(-t)
    return jnp.where(idx < 4, g, jnp.where(idx < 8, beta, jnp.where(idx < 12, t, lf)))


def _two_segment_specs(tm, width, n_p_tiles):
    return [pl.BlockSpec((tm, width), lambda i, *_: (jnp.minimum(i, n_p_tiles - 1), 0)),
            pl.BlockSpec((tm, width), lambda i, *_: (jnp.maximum(i - n_p_tiles, 0), 0))]


def _for_segment(n_p_tiles, body):
    i = pl.program_id(0)

    @pl.when(i < n_p_tiles)
    def _():
        body(0)

    @pl.when(i >= n_p_tiles)
    def _():
        body(1)


def _gdn_preactivate_stages(raw_ref, xc_ref, cw_ref, out_ref, cnew_ref, first_of_seq):
    tm = raw_ref.shape[0]
    xc_ref[0:SUBLANE, :] = jnp.where(first_of_seq, 0.0, xc_ref[tm:tm + SUBLANE, :])
    xc_ref[SUBLANE:SUBLANE + tm, :] = raw_ref[:, :CONV_CH]
    out_ref[:, CONV_CH:] = _silu(raw_ref[:, CONV_CH:])
    cnew_ref[0] = xc_ref[SUBLANE + tm - (CONV_W - 1):SUBLANE + tm, :]
    yield
    base = SUBLANE - (CONV_W - 1)
    for c0 in range(0, CONV_CH, DK_A):
        conv = xc_ref[base:base + tm, c0:c0 + DK_A] * cw_ref[0:1, c0:c0 + DK_A]
        for j in range(1, CONV_W):
            conv = conv + xc_ref[base + j:base + j + tm, c0:c0 + DK_A] * cw_ref[j:j + 1, c0:c0 + DK_A]
        act = _silu(conv)
        if c0 < H_A * DK_A:
            act = act * (lax.rsqrt(jnp.sum(act * act, axis=-1, keepdims=True) + EPS) * (DK_A ** -0.5))
        elif c0 < 2 * H_A * DK_A:
            act = act * lax.rsqrt(jnp.sum(act * act, axis=-1, keepdims=True) + EPS)
        out_ref[:, c0:c0 + DK_A] = act
        yield


def _inproj_kernel(xp_ref, xs_ref, g_ref, wa_ref, wb_ref, wst_ref, pc_ref, pr_ref, cw_ref,
                   gdnp_ref, gdns_ref, mlp_ref, mls_ref, gatep_ref, gates_ref, gatetp_ref, gatets_ref, cnew_ref,
                   xc_ref, raw_ref, *, n_p_tiles, tiles_per_seq):
    tm = xp_ref.shape[0]
    i = pl.program_id(0)

    @pl.when(i == 0)
    def _():
        xc_ref[...] = jnp.zeros_like(xc_ref)
        raw_ref[...] = jnp.zeros_like(raw_ref)

    def preactivate_previous_tile():
        return _gdn_preactivate_stages(raw_ref, xc_ref, cw_ref, gdnp_ref, cnew_ref, (i - 1) % tiles_per_seq == 0)

    def projection_stages(seg):
        x_ref = (xp_ref, xs_ref)[seg]
        gdn_dst, ml_ref = (raw_ref, gdns_ref)[seg], (mlp_ref, mls_ref)[seg]
        gate_ref, gatet_ref = (gatep_ref, gates_ref)[seg], (gatetp_ref, gatets_ref)[seg]
        hn = _rms(x_ref[...], g_ref[...]).astype(BF16)
        yield
        for c0 in range(0, GDN_W, PROJ_CHUNK):
            gdn_dst[:, c0:c0 + PROJ_CHUNK] = jnp.dot(hn, wa_ref[:, c0:c0 + PROJ_CHUNK], preferred_element_type=F32)
            yield
        for c0 in range(0, MLP_W, PROJ_CHUNK):
            ml_ref[:, c0:c0 + PROJ_CHUNK] = jnp.dot(hn, wb_ref[:, c0:c0 + PROJ_CHUNK], preferred_element_type=F32)
            yield
        raw = jnp.dot(hn, wb_ref[:, MLP_W:], preferred_element_type=F32)
        lane = lax.broadcasted_iota(jnp.int32, (tm, LANE), 1)
        gate_ref[...] = _activate_gates(raw, lane, pc_ref[0:1, :], pc_ref[1:2, :])
        raw_t = lax.dot_general(wst_ref[...], hn, _NT, preferred_element_type=F32)
        row = lax.broadcasted_iota(jnp.int32, (N_GATE, tm), 0)
        gatet_ref[...] = _activate_gates(raw_t, row, pr_ref[:, 0:1], pr_ref[:, 1:2])

    def body(seg):
        if seg == 0:
            _run_interleaved(preactivate_previous_tile(), projection_stages(0))
        else:
            pl.when(i == n_p_tiles)(lambda: _run_interleaved(preactivate_previous_tile()))
            _run_interleaved(projection_stages(1))

    _for_segment(n_p_tiles, body)


def _inproj(xp, xs, g, w_gdn, w_rest, ws_t, pcol, prow, cw, tm, n_seq_p):
    n_p, n_s = xp.shape[0], xs.shape[0]
    npt = n_p // tm
    tiles_per_seq = npt // n_seq_p

    def out2(width):
        return _two_segment_specs(tm, width, npt)

    def shp2(width):
        return [jax.ShapeDtypeStruct((n_p, width), F32), jax.ShapeDtypeStruct((n_s, width), F32)]

    def prev_tile(i):
        return jnp.clip(i - 1, 0, npt - 1)

    return pl.pallas_call(
        functools.partial(_inproj_kernel, n_p_tiles=npt, tiles_per_seq=tiles_per_seq),
        grid=((n_p + n_s) // tm,),
        in_specs=_two_segment_specs(tm, D_MODEL, npt) + [
            pl.BlockSpec((1, D_MODEL), lambda i: (0, 0)),
            pl.BlockSpec((D_MODEL, GDN_W), lambda i: (0, 0)),
            pl.BlockSpec((D_MODEL, MLP_W + LANE), lambda i: (0, 0)),
            pl.BlockSpec((N_GATE, D_MODEL), lambda i: (0, 0)),
            pl.BlockSpec((SUBLANE, LANE), lambda i: (0, 0)),
            pl.BlockSpec((N_GATE, LANE), lambda i: (0, 0)),
            pl.BlockSpec((SUBLANE, CONV_CH), lambda i: (0, 0)),
        ],
        out_specs=[
            pl.BlockSpec((tm, GDN_W), lambda i: (prev_tile(i), 0)),
            pl.BlockSpec((tm, GDN_W), lambda i: (jnp.maximum(i - npt, 0), 0)),
        ] + out2(MLP_W) + out2(LANE) + [
            pl.BlockSpec((N_GATE, tm), lambda i: (0, jnp.minimum(i, npt - 1))),
            pl.BlockSpec((N_GATE, tm), lambda i: (0, jnp.maximum(i - npt, 0))),
            pl.BlockSpec((1, CONV_W - 1, CONV_CH), lambda i: (prev_tile(i) // tiles_per_seq, 0, 0)),
        ],
        out_shape=shp2(GDN_W) + shp2(MLP_W) + shp2(LANE) + [
            jax.ShapeDtypeStruct((N_GATE, n_p), F32), jax.ShapeDtypeStruct((N_GATE, n_s), F32),
            jax.ShapeDtypeStruct((n_seq_p, CONV_W - 1, CONV_CH), F32)],
        scratch_shapes=[pltpu.VMEM((tm + SUBLANE, CONV_CH), F32), pltpu.VMEM((tm, GDN_W), F32)],
        compiler_params=pltpu.CompilerParams(dimension_semantics=("arbitrary",), vmem_limit_bytes=VMEM_LIMIT),
        name="inproj",
    )(xp, xs, g, w_gdn, w_rest, ws_t, pcol, prow, cw)


def _chunk_masks(L):
    ri = lax.broadcasted_iota(jnp.int32, (L, L), 0)
    ci = lax.broadcasted_iota(jnp.int32, (L, L), 1)
    return ri >= ci, ri > ci, ri <= ci


def _run_interleaved(*stage_generators):
    live = list(stage_generators)
    while live:
        for gen in list(live):
            if next(gen, StopIteration) is StopIteration:
                live.remove(gen)


def _gdn_stages(*refs, L, G, has_state, cumsum_cache):
    if has_state:
        (xin_ref, gate_ref, gatet_ref, cw_ref, ng_ref, cst_ref, s0_ref,
         mix_ref, cnew_ref, snew_ref, xc_ref, s_ref) = refs
    else:
        xin_ref, gate_ref, gatet_ref, ng_ref, mix_ref, snew_ref, s_ref = refs
    c = pl.program_id(1)

    @pl.when(c == 0)
    def _():
        if has_state:
            xc_ref[:, 0:SUBLANE, :] = jnp.zeros((G, SUBLANE, CONV_CH), F32)
            xc_ref[:, SUBLANE - (CONV_W - 1):SUBLANE, :] = cst_ref[...]
            s_ref[...] = s0_ref[...]
        else:
            s_ref[...] = jnp.zeros_like(s_ref)

    if has_state:
        @pl.when(c > 0)
        def _():
            xc_ref[:, 0:SUBLANE, :] = xc_ref[:, L:L + SUBLANE, :]

    yield
    tril, strict, triu = _chunk_masks(L)
    base = SUBLANE - (CONV_W - 1)

    chains = [(g, h) for g in range(G) for h in range(H_A)]
    s_old = [s_ref[g, h] for g, h in chains]
    if has_state:
        for g in range(G):
            xc_ref[g, SUBLANE:SUBLANE + L, :] = xin_ref[g, :, :CONV_CH]

    q, k, v, beta, gc, gl, decay = [], [], [], [], [], [], []
    for g in range(G):
        if has_state:
            conv = xc_ref[g, base:base + L, :] * cw_ref[0:1, :]
            for j in range(1, CONV_W):
                conv = conv + xc_ref[g, base + j:base + j + L, :] * cw_ref[j:j + 1, :]
            cnew_ref[g] = xc_ref[g, SUBLANE + L - (CONV_W - 1):SUBLANE + L, :]
            act = _silu(conv)
        else:
            act = xin_ref[g, :, :CONV_CH]
        gact = gate_ref[g]
        cum_c, cum_r = _chunk_cumsums(gact, gatet_ref[g, 0], tril, triu, cumsum_cache, g)
        for h in range(H_A):
            q.append(act[:, h * DK_A:(h + 1) * DK_A])
            k.append(act[:, H_A * DK_A + h * DK_A:H_A * DK_A + (h + 1) * DK_A])
            v.append(act[:, 2 * H_A * DK_A + h * DV_A:2 * H_A * DK_A + (h + 1) * DV_A])
            beta.append(gact[:, 4 + h:5 + h])
            gc.append(cum_c[:, h:h + 1])
            gl.append(cum_c[L - 1:L, h:h + 1])
            gr = cum_r[h:h + 1, :]
            decay.append(jnp.where(tril, jnp.exp(jnp.where(tril, cum_c[:, h:h + 1] - gr, 0.0)), 0.0))
        yield

    nc = range(len(chains))
    if has_state:
        qss = [jnp.sum(q[i] * q[i], axis=-1, keepdims=True) for i in nc]
        kss = [jnp.sum(k[i] * k[i], axis=-1, keepdims=True) for i in nc]
        q = [q[i] * (lax.rsqrt(qss[i] + EPS) * (DK_A ** -0.5)) for i in nc]
        k = [k[i] * lax.rsqrt(kss[i] + EPS) for i in nc]
    kb = [k[i] * beta[i] for i in nc]
    egc = [jnp.exp(gc[i]) for i in nc]
    yield
    kk = [_dot(kb[i], k[i], _NT) for i in nc]
    yield
    qk = [_dot(q[i], k[i], _NT) for i in nc]
    yield
    eye = (lax.broadcasted_iota(jnp.int32, (L, L), 0) == lax.broadcasted_iota(jnp.int32, (L, L), 1)).astype(F32)
    pw = [-jnp.where(strict, kk[i] * decay[i], 0.0) for i in nc]
    t_inv = [eye + pw[i] for i in nc]
    span = 2
    while span < L:
        yield
        pw = [_dot(pw[i], pw[i]) for i in nc]
        yield
        t_inv = [t_inv[i] + _dot(t_inv[i], pw[i]) for i in nc]
        span *= 2
    yield
    sol = [_dot(t_inv[i], jnp.concatenate([v[i] * beta[i], kb[i] * egc[i]], axis=-1)) for i in nc]
    yield
    qs = [_dot(q[i] * egc[i], s_old[i]) for i in nc]
    yield
    ws = [_dot(sol[i][:, DV_A:], s_old[i]) for i in nc]
    v_new = [sol[i][:, :DV_A] - ws[i] for i in nc]
    yield
    o = [qs[i] + _dot(jnp.where(tril, qk[i] * decay[i], 0.0), v_new[i]) for i in nc]
    yield
    s_new = [s_old[i] * jnp.exp(gl[i]) + _dot(k[i] * jnp.exp(gl[i] - gc[i]), v_new[i], _TN) for i in nc]
    yield
    ms = [jnp.mean(o[i] * o[i], axis=-1, keepdims=True) for i in nc]
    on = [o[i] * lax.rsqrt(ms[i] + EPS) for i in nc]
    yield
    for i, (g, h) in enumerate(chains):
        z = xin_ref[g, :, CONV_CH + h * DV_A:CONV_CH + (h + 1) * DV_A]
        out = on[i] * ng_ref[...] * (_silu(z) if has_state else z)
        mix_ref[g, :, h * DV_A:(h + 1) * DV_A] = out.astype(mix_ref.dtype)
    yield
    for i, (g, h) in enumerate(chains):
        s_ref[g, h] = s_new[i]
        snew_ref[g, h] = s_new[i]


def _mlstm_stages(*refs, L, G, has_state, cumsum_cache):
    if has_state:
        (xin_ref, gate_ref, gatet_ref, ng_ref, c0_ref, n0_ref, m0_ref,
         mix_ref, cnew_ref, nnew_ref, mnew_ref, c_ref, n_ref, m_ref) = refs
    else:
        (xin_ref, gate_ref, gatet_ref, ng_ref,
         mix_ref, cnew_ref, nnew_ref, mnew_ref, c_ref, n_ref, m_ref) = refs
    c = pl.program_id(1)

    @pl.when(c == 0)
    def _():
        c_ref[...] = jnp.zeros_like(c_ref)
        n_ref[...] = jnp.zeros_like(n_ref)
        m_ref[...] = jnp.zeros_like(m_ref)
        if has_state:
            for h in range(H_B):
                off = (h % 2) * DK_B
                c_ref[:, h, off:off + DK_B, :] = c0_ref[:, h]
                n_ref[:, h:h + 1, off:off + DK_B] = n0_ref[:, h:h + 1, :]
            m_ref[:, 0:1, 0:H_B] = m0_ref[...]

    yield
    tril, _, triu = _chunk_masks(L)

    chains = [(g, h) for g in range(G) for h in range(H_B)]
    nc = range(len(chains))
    c_old = [c_ref[g, h] for g, h in chains]
    n_old = [n_ref[g, h:h + 1, :] for g, h in chains]
    m_old = [m_ref[g, 0:1, h:h + 1] for g, h in chains]

    k0, v0 = H_B * DK_B, 2 * H_B * DK_B
    low_half = lax.broadcasted_iota(jnp.int32, (L, LANE), 1) < DK_B

    def own_lanes(pair, h):
        return jnp.where(low_half if h % 2 == 0 else jnp.logical_not(low_half), pair, 0.0)

    q = [own_lanes(xin_ref[g, :, (h // 2) * LANE:(h // 2 + 1) * LANE], h) * (DK_B ** -0.5) for g, h in chains]
    k = [own_lanes(xin_ref[g, :, k0 + (h // 2) * LANE:k0 + (h // 2 + 1) * LANE], h) for g, h in chains]
    v = [xin_ref[g, :, v0 + h * DV_B:v0 + (h + 1) * DV_B] for g, h in chains]
    ig_c, b_c, b_last, d_log = [], [], [], []
    for g in range(G):
        gact = gate_ref[g]
        gact_t = gatet_ref[g, 0]
        cum_c, cum_r = _chunk_cumsums(gact, gact_t, tril, triu, cumsum_cache, g)
        for h in range(H_B):
            ig_c.append(gact[:, 8 + h:9 + h])
            b_c.append(cum_c[:, 12 + h:13 + h])
            b_last.append(cum_c[L - 1:L, 12 + h:13 + h])
            d_log.append(jnp.where(tril, cum_c[:, 12 + h:13 + h] - cum_r[12 + h:13 + h, :]
                                   + gact_t[8 + h:9 + h, :], NEG))
        yield
    qk = [_dot(q[i], k[i], _NT) for i in nc]
    yield
    qc = [_dot(q[i], c_old[i]) for i in nc]
    yield
    inter = [b_c[i] + m_old[i] for i in nc]
    m_t = [jnp.maximum(inter[i], jnp.max(d_log[i], axis=-1, keepdims=True)) for i in nc]
    yield
    s = [qk[i] * jnp.exp(d_log[i] - m_t[i]) for i in nc]
    e_inter = [jnp.exp(inter[i] - m_t[i]) for i in nc]
    yield
    sv = [_dot(s[i], v[i]) for i in nc]
    yield
    m_new = [m_t[i][L - 1:L, :] for i in nc]
    kw = [k[i] * jnp.exp(b_last[i] - b_c[i] + ig_c[i] - m_new[i]) for i in nc]
    f_tot = [jnp.exp(b_last[i] + m_old[i] - m_new[i]) for i in nc]
    yield
    c_new = [f_tot[i] * c_old[i] + _dot(kw[i], v[i], _TN) for i in nc]
    yield
    n_new = [f_tot[i] * n_old[i] + jnp.sum(kw[i], axis=0, keepdims=True) for i in nc]
    qn = [jnp.sum(q[i] * n_old[i], axis=-1, keepdims=True) for i in nc]
    yield
    ssum = [jnp.sum(s[i], axis=-1, keepdims=True) for i in nc]
    yield
    den = [jnp.maximum(jnp.abs(e_inter[i] * qn[i] + ssum[i]), jnp.exp(-m_t[i])) for i in nc]
    hh = [(e_inter[i] * qc[i] + sv[i]) / den[i] for i in nc]
    yield
    ms = [jnp.mean(hh[i] * hh[i], axis=-1, keepdims=True) for i in nc]
    hn = [hh[i] * lax.rsqrt(ms[i] + EPS) for i in nc]
    yield
    for i, (g, h) in enumerate(chains):
        og = xin_ref[g, :, v0 + H_B * DV_B + h * DV_B:v0 + H_B * DV_B + (h + 1) * DV_B]
        mix_ref[g, :, h * DV_B:(h + 1) * DV_B] = (hn[i] * ng_ref[h:h + 1, :] * _sigmoid(og)).astype(mix_ref.dtype)
    yield
    for i, (g, h) in enumerate(chains):
        c_ref[g, h] = c_new[i]
        n_ref[g, h:h + 1, :] = n_new[i]
        m_ref[g, 0:1, h:h + 1] = m_new[i]
        off = (h % 2) * DK_B
        cnew_ref[g, h] = c_new[i][off:off + DK_B, :]
        nnew_ref[g, h:h + 1, :] = n_new[i][:, off:off + DK_B]
        mnew_ref[g, 0:1, h:h + 1] = m_new[i]


def _mixers_kernel(*refs, L, G, has_state):
    if has_state:
        (gdn_ref, gate_ref, gatet_ref, nga_ref, ml_ref, ngb_ref, cw_ref, cst_ref, s0_ref, c0_ref, n0_ref, m0_ref,
         mixa_ref, snew_ref, mixb_ref, cnew_ref, nnew_ref, mnew_ref, convnew_ref,
         s_ref, c_ref, n_ref, m_ref, xc_ref) = refs
        gdn_refs = (gdn_ref, gate_ref, gatet_ref, cw_ref, nga_ref, cst_ref, s0_ref,
                    mixa_ref, convnew_ref, snew_ref, xc_ref, s_ref)
        ml_refs = (ml_ref, gate_ref, gatet_ref, ngb_ref, c0_ref, n0_ref, m0_ref,
                   mixb_ref, cnew_ref, nnew_ref, mnew_ref, c_ref, n_ref, m_ref)
    else:
        (gdn_ref, gate_ref, gatet_ref, nga_ref, ml_ref, ngb_ref,
         mixa_ref, snew_ref, mixb_ref, cnew_ref, nnew_ref, mnew_ref, s_ref, c_ref, n_ref, m_ref) = refs
        gdn_refs = (gdn_ref, gate_ref, gatet_ref, nga_ref, mixa_ref, snew_ref, s_ref)
        ml_refs = (ml_ref, gate_ref, gatet_ref, ngb_ref, mixb_ref, cnew_ref, nnew_ref, mnew_ref, c_ref, n_ref, m_ref)
    cumsum_cache = {}
    _run_interleaved(_gdn_stages(*gdn_refs, L=L, G=G, has_state=has_state, cumsum_cache=cumsum_cache),
                     _mlstm_stages(*ml_refs, L=L, G=G, has_state=has_state, cumsum_cache=cumsum_cache))


def _mixers(gdn_in, ml_in, gates, gates_t, ng_a, ng_b, *, L, G, cw=None, state=None):
    n_seq, T, _ = gdn_in.shape
    n_c = T // L
    has_state = state is not None

    def seq_blk(*tail):
        return pl.BlockSpec((G,) + tail, lambda b, c: (b,) + (0,) * len(tail))

    def tok_blk(width):
        return pl.BlockSpec((G, L, width), lambda b, c: (b, c, 0))

    def seq_shape(*tail):
        return jax.ShapeDtypeStruct((n_seq,) + tail, F32)

    state_specs = [seq_blk(H_A, DK_A, DV_A), seq_blk(H_B, DK_B, DV_B), seq_blk(H_B, DK_B), seq_blk(1, H_B)]
    state_shapes = [seq_shape(H_A, DK_A, DV_A), seq_shape(H_B, DK_B, DV_B), seq_shape(H_B, DK_B), seq_shape(1, H_B)]
    conv_spec, conv_shape = seq_blk(CONV_W - 1, CONV_CH), seq_shape(CONV_W - 1, CONV_CH)
    in_specs = [
        tok_blk(GDN_W), tok_blk(LANE),
        pl.BlockSpec((G, 1, N_GATE, L), lambda b, c: (b, c, 0, 0)),
        pl.BlockSpec((1, DV_A), lambda b, c: (0, 0)),
        tok_blk(MLP_W),
        pl.BlockSpec((H_B, DV_B), lambda b, c: (0, 0)),
    ]
    args = [gdn_in, gates, gates_t, ng_a, ml_in, ng_b]
    out_specs = [tok_blk(H_A * DV_A), state_specs[0], tok_blk(H_B * DV_B)] + state_specs[1:]
    mix_dtype = BF16 if L % (2 * SUBLANE) == 0 else F32
    mix_a = jax.ShapeDtypeStruct((n_seq, T, H_A * DV_A), mix_dtype)
    mix_b = jax.ShapeDtypeStruct((n_seq, T, H_B * DV_B), mix_dtype)
    out_shape = [mix_a, state_shapes[0], mix_b] + state_shapes[1:]
    scratch = [pltpu.VMEM((G, H_A, DK_A, DV_A), F32), pltpu.VMEM((G, H_B, LANE, DV_B), F32),
               pltpu.VMEM((G, SUBLANE, LANE), F32), pltpu.VMEM((G, SUBLANE, LANE), F32)]
    if has_state:
        in_specs += [pl.BlockSpec((SUBLANE, CONV_CH), lambda b, c: (0, 0)), conv_spec] + state_specs
        args += [cw] + list(state)
        out_specs.append(conv_spec)
        out_shape.append(conv_shape)
        scratch.append(pltpu.VMEM((G, L + SUBLANE, CONV_CH), F32))
    return pl.pallas_call(
        functools.partial(_mixers_kernel, L=L, G=G, has_state=has_state),
        grid=(n_seq // G, n_c),
        in_specs=in_specs,
        out_specs=out_specs,
        out_shape=out_shape,
        scratch_shapes=scratch,
        compiler_params=pltpu.CompilerParams(dimension_semantics=("parallel", "arbitrary"),
                                             vmem_limit_bytes=VMEM_LIMIT),
        name=f"mixers_L{L}",
    )(*args)


def _outproj_kernel(xp_ref, xs_ref, map_ref, mas_ref, mbp_ref, mbs_ref, wo_ref, g_ref, rw_ref, rb_ref,
                    x1_ref, xsort_ref, info_ref, cpad_ref, *, n_p_tiles):
    half = H_A * DV_A
    tm = xp_ref.shape[0]

    def body(seg):
        x_ref, ma_ref, mb_ref = (xp_ref, xs_ref)[seg], (map_ref, mas_ref)[seg], (mbp_ref, mbs_ref)[seg]
        x1 = (x_ref[...] + jnp.dot(ma_ref[...].astype(BF16), wo_ref[:half, :], preferred_element_type=F32)
              + jnp.dot(mb_ref[...].astype(BF16), wo_ref[half:, :], preferred_element_type=F32))
        x1_ref[...] = x1
        hn = _rms(x1, g_ref[...])
        hn_hi = hn.astype(BF16)
        hn_lo = (hn - hn_hi.astype(F32)).astype(BF16)
        logits = (jnp.dot(hn_hi, rw_ref[0], preferred_element_type=F32)
                  + jnp.dot(hn_hi, rw_ref[1], preferred_element_type=F32)
                  + jnp.dot(hn_lo, rw_ref[0], preferred_element_type=F32)) + rb_ref[...]

        vals = logits.T[:N_EXPERTS, :]
        e_iota = lax.broadcasted_iota(jnp.int32, (N_EXPERTS, tm), 0)
        sels, tops = [], []
        for _ in range(TOP_K):
            m = jnp.max(vals, axis=0, keepdims=True)
            first = jnp.min(jnp.where(vals == m, e_iota, N_EXPERTS), axis=0, keepdims=True)
            sel = e_iota == first
            vals = jnp.where(sel, -jnp.inf, vals)
            sels.append(sel)
            tops.append(m)
        ex = [jnp.exp(t - tops[0]) for t in tops]
        den = ex[0] + ex[1] + ex[2] + ex[3]
        gates = [e / den for e in ex]
        mask = sels[0].astype(F32) + sels[1].astype(F32) + sels[2].astype(F32) + sels[3].astype(F32)
        ri = lax.broadcasted_iota(jnp.int32, (tm, tm), 0)
        ci = lax.broadcasted_iota(jnp.int32, (tm, tm), 1)
        rank = _dot(mask, (ri < ci).astype(F32))
        cnt = jnp.sum(mask, axis=1, keepdims=True)
        cpad = jnp.ceil(cnt * (1.0 / SEG_ALIGN)) * SEG_ALIGN
        cpad_b = jnp.broadcast_to(cpad, (N_EXPERTS, tm))
        er = lax.broadcasted_iota(jnp.int32, (N_EXPERTS, N_EXPERTS), 0)
        ec = lax.broadcasted_iota(jnp.int32, (N_EXPERTS, N_EXPERTS), 1)
        seg_off = _dot((er > ec).astype(F32), cpad_b)
        pos = seg_off + rank
        q = [jnp.sum(jnp.where(s, pos, 0.0), axis=0, keepdims=True) for s in sels]

        j_iota = lax.broadcasted_iota(jnp.int32, (MOE_CAP, tm), 0).astype(F32)
        perm = jnp.zeros((MOE_CAP, tm), F32)
        for kk in range(TOP_K):
            perm = jnp.where(j_iota == q[kk], 1.0, perm)
        xsorted = _dot(perm, hn)
        xsort_ref[...] = xsorted.astype(MOE_DTYPE)

        r_iota = lax.broadcasted_iota(jnp.int32, (LANE, tm), 0)
        info = jnp.zeros((LANE, tm), F32)
        for kk in range(TOP_K):
            info = jnp.where(r_iota == kk, q[kk], info)
            info = jnp.where(r_iota == TOP_K + kk, gates[kk], info)
        info_ref[...] = info.T
        cpad_ref[0] = cpad_b[:, :LANE]

    _for_segment(n_p_tiles, body)


def _outproj(xp, xs, ma_p, ma_s, mb_p, mb_s, w_out, g, rw, rb, tm):
    n_p, n_s = xp.shape[0], xs.shape[0]
    n = n_p + n_s
    nt = n // tm
    npt = n_p // tm
    half = H_A * DV_A
    return pl.pallas_call(
        functools.partial(_outproj_kernel, n_p_tiles=npt),
        grid=(nt,),
        in_specs=_two_segment_specs(tm, D_MODEL, npt) + _two_segment_specs(tm, half, npt)
        + _two_segment_specs(tm, half, npt) + [
            pl.BlockSpec((D_MODEL, D_MODEL), lambda i: (0, 0)),
            pl.BlockSpec((1, D_MODEL), lambda i: (0, 0)),
            pl.BlockSpec((2, D_MODEL, LANE), lambda i: (0, 0, 0)),
            pl.BlockSpec((1, LANE), lambda i: (0, 0)),
        ],
        out_specs=[
            pl.BlockSpec((tm, D_MODEL), lambda i: (i, 0)),
            pl.BlockSpec((MOE_CAP, D_MODEL), lambda i: (i, 0)),
            pl.BlockSpec((tm, LANE), lambda i: (i, 0)),
            pl.BlockSpec((1, N_EXPERTS, LANE), lambda i: (i, 0, 0)),
        ],
        out_shape=[
            jax.ShapeDtypeStruct((n, D_MODEL), F32),
            jax.ShapeDtypeStruct((nt * MOE_CAP, D_MODEL), MOE_DTYPE),
            jax.ShapeDtypeStruct((n, LANE), F32),
            jax.ShapeDtypeStruct((nt, N_EXPERTS, LANE), F32),
        ],
        compiler_params=pltpu.CompilerParams(dimension_semantics=("arbitrary",), vmem_limit_bytes=VMEM_LIMIT),
        name="outproj",
    )(xp, xs, ma_p, ma_s, mb_p, mb_s, w_out, g, rw, rb)


def _expert_kernel(be_ref, bj_ref, tf_ref, tl_ref, cov_ref, nu_ref, vt_ref, ct_ref, lt_ref, nx_ref, ws_ref,
                   xs_hbm, wgu_hbm, bgu_ref, wd_hbm, bd_ref, y_ref,
                   xbuf, gsem, wgu_st, wd_st, wsem, wgu_bf, wd_bf, *, nt):
    b = pl.program_id(0)
    n_used = nu_ref[0]
    slot = b % 2

    def start_pieces(bb, copy, s):
        e = be_ref[bb]
        base = bj_ref[bb] * MOE_BLK

        def body(t, carry):
            k = e * nt + t
            lo = jnp.maximum(vt_ref[k], base)
            ln = jnp.minimum(ct_ref[k], base + MOE_BLK) - lo

            @pl.when(ln > 0)
            def _():
                copy(s, pl.multiple_of(lt_ref[k] + lo, SEG_ALIGN), pl.multiple_of(lo - base, SEG_ALIGN),
                     pl.multiple_of(ln, SEG_ALIGN)).start()
            return carry

        lax.fori_loop(tf_ref[bb], tl_ref[bb] + 1, body, 0)

    def weight_copies(e):
        return (pltpu.make_async_copy(wgu_hbm.at[e], wgu_st, wsem.at[0]),
                pltpu.make_async_copy(wd_hbm.at[e], wd_st, wsem.at[1]))

    def cast_weights(p):
        wgu_bf[p] = wgu_st[...].astype(BF16)
        wd_bf[p] = wd_st[...].astype(BF16)

    def gather_copy(s, src, dst, size):
        return pltpu.make_async_copy(xs_hbm.at[pl.ds(src, size)], xbuf.at[s, pl.ds(dst, size)], gsem.at[s])

    def wait_rows(count, copy, s):
        @pl.when(count > 0)
        def _():
            copy(s, 0, 0, pl.multiple_of(count, SEG_ALIGN)).wait()

    @pl.when(b == 0)
    def _():
        xbuf[...] = jnp.zeros_like(xbuf)
        start_pieces(0, gather_copy, 0)

    @pl.when(b + 1 < n_used)
    def _():
        start_pieces(b + 1, gather_copy, 1 - slot)

    @pl.when(b >= n_used)
    def _():
        y_ref[...] = jnp.zeros_like(y_ref)

    @pl.when(b < n_used)
    def _():
        e = be_ref[b]
        first = jnp.logical_or(b == 0, be_ref[jnp.maximum(b - 1, 0)] != e)
        last = jnp.logical_or(b == n_used - 1, be_ref[jnp.minimum(b + 1, n_used - 1)] != e)
        has_next = nx_ref[b] < N_EXPERTS
        p = ws_ref[b]

        @pl.when(b == 0)
        def _():
            for cp in weight_copies(e):
                cp.start()
            for cp in weight_copies(e):
                cp.wait()
            cast_weights(p)

        @pl.when(jnp.logical_and(first, has_next))
        def _():
            for cp in weight_copies(nx_ref[b]):
                cp.start(priority=1)

        wait_rows(cov_ref[b], gather_copy, slot)

        def expert_mlp(rows):
            hgu = jnp.dot(xbuf[slot, :rows].astype(BF16), wgu_bf[p], preferred_element_type=F32) + bgu_ref[0]
            gate = jnp.minimum(hgu[:, :D_FF], SWIGLU_LIMIT)
            up = jnp.clip(hgu[:, D_FF:], -SWIGLU_LIMIT, SWIGLU_LIMIT)
            act = (up + 1.0) * gate * _sigmoid(SWIGLU_ALPHA * gate)
            y = jnp.dot(act.astype(BF16), wd_bf[p], preferred_element_type=F32) + bd_ref[0]
            y_ref[:rows] = y.astype(MOE_DTYPE)
            if rows < MOE_BLK:
                y_ref[rows:] = jnp.zeros((MOE_BLK - rows, D_MODEL), MOE_DTYPE)

        quarter = MOE_BLK // 4
        for nq in range(1, 5):
            pl.when(jnp.logical_and(cov_ref[b] > (nq - 1) * quarter, cov_ref[b] <= nq * quarter))(
                functools.partial(expert_mlp, nq * quarter))

        @pl.when(jnp.logical_and(last, has_next))
        def _():
            for cp in weight_copies(nx_ref[b]):
                cp.wait()
            cast_weights(1 - p)


def _experts(tables, xs, w_gu, b_gu, w_down, b_down, nt):
    nb = tables[0].shape[0]

    def bias_blk(b, *t):
        return (t[0][jnp.minimum(b, t[5][0] - 1)], 0, 0)

    def out_blk(b, *t):
        return (b, 0)

    grid_spec = pltpu.PrefetchScalarGridSpec(
        num_scalar_prefetch=len(tables),
        grid=(nb,),
        in_specs=[
            pl.BlockSpec(memory_space=pl.ANY),
            pl.BlockSpec(memory_space=pl.ANY),
            pl.BlockSpec((1, 1, 2 * D_FF), bias_blk),
            pl.BlockSpec(memory_space=pl.ANY),
            pl.BlockSpec((1, 1, D_MODEL), bias_blk),
        ],
        out_specs=pl.BlockSpec((MOE_BLK, D_MODEL), out_blk),
        scratch_shapes=[
            pltpu.VMEM((2, MOE_BLK, D_MODEL), MOE_DTYPE),
            pltpu.SemaphoreType.DMA((2,)),
            pltpu.VMEM((D_MODEL, 2 * D_FF), F32),
            pltpu.VMEM((D_FF, D_MODEL), F32),
            pltpu.SemaphoreType.DMA((2,)),
            pltpu.VMEM((2, D_MODEL, 2 * D_FF), BF16),
            pltpu.VMEM((2, D_FF, D_MODEL), BF16),
        ],
    )
    return pl.pallas_call(
        functools.partial(_expert_kernel, nt=nt),
        grid_spec=grid_spec,
        out_shape=jax.ShapeDtypeStruct((nb * MOE_BLK, D_MODEL), MOE_DTYPE),
        compiler_params=pltpu.CompilerParams(dimension_semantics=("arbitrary",), vmem_limit_bytes=VMEM_LIMIT),
        name="experts",
    )(*tables, xs, w_gu, b_gu, w_down, b_down)


def _combine_kernel(src_ref, len_ref, off_ref, used_ref,
                    y_hbm, info_ref, x1_ref, pp_ref, ps_ref, gple_ref, wg_ref, wp_ref, gfin_ref,
                    outp_ref, outs_ref, ybuf, sem, *, n_p_tiles):
    tm = x1_ref.shape[0]
    i = pl.program_id(0)
    slot = i % 2

    def tile_copy(s, src, dst, size):
        return pltpu.make_async_copy(y_hbm.at[pl.ds(src, size)], ybuf.at[s, pl.ds(dst, size)], sem.at[s])

    def gather_tile(t, s):
        def body(e, carry):
            k = t * N_EXPERTS + e
            ln = len_ref[k]

            @pl.when(ln > 0)
            def _():
                tile_copy(s, pl.multiple_of(src_ref[k], SEG_ALIGN), pl.multiple_of(off_ref[k], SEG_ALIGN),
                          pl.multiple_of(ln, SEG_ALIGN)).start()
            return carry
        lax.fori_loop(0, N_EXPERTS, body, 0)

    @pl.when(i == 0)
    def _():
        ybuf[...] = jnp.zeros_like(ybuf)
        gather_tile(0, 0)

    @pl.when(i + 1 < pl.num_programs(0))
    def _():
        gather_tile(i + 1, 1 - slot)

    tile_copy(slot, 0, 0, pl.multiple_of(used_ref[i], SEG_ALIGN)).wait()
    ys_ref = ybuf.at[slot]

    def body(seg):
        p_ref, out_ref = (pp_ref, ps_ref)[seg], (outp_ref, outs_ref)[seg]
        info = info_ref[...]
        j_iota = lax.broadcasted_iota(jnp.int32, (tm, MOE_CAP), 1).astype(F32)
        gmat = jnp.zeros((tm, MOE_CAP), F32)
        for kk in range(TOP_K):
            gmat = jnp.where(j_iota == info[:, kk:kk + 1], info[:, TOP_K + kk:TOP_K + kk + 1], gmat)
        x2 = x1_ref[...] + jnp.dot(gmat.astype(BF16), ys_ref[...].astype(BF16), preferred_element_type=F32)
        hn = _rms(x2, gple_ref[...]).astype(BF16)
        gate = _sigmoid(jnp.dot(hn, wg_ref[...], preferred_element_type=F32))
        pe = jnp.dot(p_ref[...].astype(BF16), wp_ref[...], preferred_element_type=F32)
        x3 = x2 + gate * pe
        out_ref[...] = _rms(x3, gfin_ref[...])

    _for_segment(n_p_tiles, body)


def _combine(tile_tables, ys, info, x1, pp, ps, g_ple, w_gate, w_p, g_fin, tm):
    n_p, n_s = pp.shape[0], ps.shape[0]
    n = n_p + n_s
    nt = n // tm
    npt = n_p // tm
    grid_spec = pltpu.PrefetchScalarGridSpec(
        num_scalar_prefetch=len(tile_tables),
        grid=(nt,),
        in_specs=[
            pl.BlockSpec(memory_space=pl.ANY),
            pl.BlockSpec((tm, LANE), lambda i, *_: (i, 0)),
            pl.BlockSpec((tm, D_MODEL), lambda i, *_: (i, 0)),
        ] + _two_segment_specs(tm, PLE_DIM, npt) + [
            pl.BlockSpec((1, D_MODEL), lambda i, *_: (0, 0)),
            pl.BlockSpec((D_MODEL, D_MODEL), lambda i, *_: (0, 0)),
            pl.BlockSpec((PLE_DIM, D_MODEL), lambda i, *_: (0, 0)),
            pl.BlockSpec((1, D_MODEL), lambda i, *_: (0, 0)),
        ],
        out_specs=_two_segment_specs(tm, D_MODEL, npt),
        scratch_shapes=[pltpu.VMEM((2, MOE_CAP, D_MODEL), MOE_DTYPE), pltpu.SemaphoreType.DMA((2,))],
    )
    return pl.pallas_call(
        functools.partial(_combine_kernel, n_p_tiles=npt),
        grid_spec=grid_spec,
        out_shape=[jax.ShapeDtypeStruct((n_p, D_MODEL), F32), jax.ShapeDtypeStruct((n_s, D_MODEL), F32)],
        compiler_params=pltpu.CompilerParams(dimension_semantics=("arbitrary",), vmem_limit_bytes=VMEM_LIMIT),
        name="combine",
    )(*tile_tables, ys, info, x1, pp, ps, g_ple, w_gate, w_p, g_fin)


def _block_tables(seg_len, nb):
    nt = seg_len.shape[0]
    seg_off = jnp.cumsum(seg_len, axis=1) - seg_len
    seg_end = jnp.cumsum(seg_len, axis=0).T
    seg_start = seg_end - seg_len.T
    n_rows = seg_end[:, -1]
    n_blk = (n_rows + MOE_BLK - 1) // MOE_BLK
    blk_end = jnp.cumsum(n_blk)
    b = jnp.arange(nb, dtype=jnp.int32)
    block_e = jnp.minimum(jnp.sum((blk_end[None, :] <= b[:, None]).astype(jnp.int32), axis=1), N_EXPERTS - 1)
    idx = jnp.where(n_blk > 0, jnp.arange(N_EXPERTS, dtype=jnp.int32), N_EXPERTS)
    nxt = jnp.concatenate([lax.cummin(idx, axis=0, reverse=True)[1:], jnp.full((1,), N_EXPERTS, jnp.int32)])
    parity = (jnp.cumsum((n_blk > 0).astype(jnp.int32)) - 1) % 2
    per_e = jnp.concatenate([jnp.stack([blk_end - n_blk, n_rows, nxt, parity], axis=1), seg_start, seg_end],
                            axis=1).astype(F32)
    onehot = (block_e[:, None] == jnp.arange(N_EXPERTS, dtype=jnp.int32)[None, :]).astype(F32)
    per_b = jnp.dot(onehot, per_e, precision=HI).astype(jnp.int32)
    block_j = b - per_b[:, 0]
    base = block_j * MOE_BLK
    t_first = jnp.sum((per_b[:, 4 + nt:] <= base[:, None]).astype(jnp.int32), axis=1)
    t_last = jnp.sum((per_b[:, 4:4 + nt] < (base + MOE_BLK)[:, None]).astype(jnp.int32), axis=1) - 1
    cover = jnp.clip(per_b[:, 1] - base, 0, MOE_BLK)
    seg_shift = (jnp.arange(nt, dtype=jnp.int32)[:, None] * MOE_CAP + seg_off).T - seg_start
    tables = (block_e, block_j, t_first, t_last, cover, blk_end[-1:], seg_start.reshape(-1),
              seg_end.reshape(-1), seg_shift.reshape(-1), per_b[:, 2], per_b[:, 3])
    y_src = ((blk_end - n_blk) * MOE_BLK)[None, :] + seg_start.T
    tile_tables = (y_src.reshape(-1), seg_len.reshape(-1), seg_off.reshape(-1), jnp.sum(seg_len, axis=1))
    return tuple(t.astype(jnp.int32) for t in tables), tuple(t.astype(jnp.int32) for t in tile_tables)


def _rearranged_in_weights(w_in):
    o = np.cumsum([0, CONV_CH, H_A * DV_A, H_A, H_A, H_B * DK_B, H_B * DK_B, H_B * DV_B, H_B * DV_B, H_B, H_B])
    conv_in, z_a, a_a, b_a, q_b, k_b, v_b, o_b, i_b, f_b = (w_in[:, int(o[j]):int(o[j + 1])] for j in range(10))
    small = jnp.concatenate([a_a, b_a, i_b, f_b], axis=1)
    w_gdn = w_in[:, :GDN_W]
    w_rest = jnp.concatenate([q_b, k_b, v_b, o_b, small, jnp.zeros((D_MODEL, LANE - N_GATE), w_in.dtype)], axis=1)
    return w_gdn.astype(BF16), w_rest.astype(BF16), small.T.astype(BF16)


def _gate_params(a_log, dt_bias, i_bias, f_bias):
    z4 = jnp.zeros((4,), F32)
    alog = jnp.concatenate([a_log.astype(F32), z4, z4, z4])
    bias = jnp.concatenate([dt_bias.astype(F32), z4, i_bias.astype(F32), f_bias.astype(F32)])
    pad = jnp.zeros((LANE - N_GATE,), F32)
    pcol = jnp.zeros((SUBLANE, LANE), F32).at[0].set(jnp.concatenate([alog, pad])).at[1].set(
        jnp.concatenate([bias, pad]))
    prow = jnp.zeros((N_GATE, LANE), F32).at[:, 0].set(alog).at[:, 1].set(bias)
    return pcol, prow


def kernel(x_prompt, x_sample, p_prompt, p_sample, state_conv, state_gdn, state_mlstm_c, state_mlstm_n, state_mlstm_m, norm_attn_g, w_in, conv_w, gdn_a_log, gdn_dt_bias, gdn_norm_g, mlstm_i_bias, mlstm_f_bias, mlstm_norm_g, w_out, norm_moe_g, router_w, router_b, expert_w_gu, expert_b_gu, expert_w_down, expert_b_down, norm_ple_g, ple_gate_w, ple_w, final_norm_g):
    bp, tp, _ = x_prompt.shape
    bs, ts, _ = x_sample.shape
    n_p, n_s = bp * tp, bs * ts
    n = n_p + n_s
    lp, ls = min(tp, CHUNK), min(ts, CHUNK)
    tm = 256
    gp = 4 if bp % 4 == 0 else 1
    gs = 16 if bs % 16 == 0 else 1
    assert tp % lp == 0 and ts % ls == 0 and tp % tm == 0 and n_s % tm == 0 and ls % SUBLANE == 0

    xp = x_prompt.reshape(n_p, D_MODEL)
    xs = x_sample.reshape(n_s, D_MODEL)

    w_gdn, w_rest, ws_t = _rearranged_in_weights(w_in[0])
    pcol, prow = _gate_params(gdn_a_log[0], gdn_dt_bias[0], mlstm_i_bias[0], mlstm_f_bias[0])
    cw = jnp.zeros((SUBLANE, CONV_CH), F32).at[:CONV_W].set(conv_w[0].astype(F32))
    gdn_p, gdn_s, ml_p, ml_s, gate_p, gate_s, gatet_p, gatet_s, conv_p = _inproj(
        xp, xs, norm_attn_g[0].reshape(1, D_MODEL), w_gdn, w_rest, ws_t, pcol, prow, cw, tm, bp)
    gt_p = gatet_p.reshape(N_GATE, bp, tp // lp, lp).transpose(1, 2, 0, 3)
    gt_s = gatet_s.reshape(N_GATE, bs, ts // ls, ls).transpose(1, 2, 0, 3)

    ng_a = gdn_norm_g[0].reshape(1, DV_A).astype(F32)
    ng_b = mlstm_norm_g[0].reshape(H_B, DV_B).astype(F32)
    ma_p, gdn_st_p, mb_p, c_p, nn_p, m_p = _mixers(
        gdn_p.reshape(bp, tp, GDN_W), ml_p.reshape(bp, tp, MLP_W), gate_p.reshape(bp, tp, LANE), gt_p, ng_a, ng_b,
        L=lp, G=gp)
    ma_s, gdn_st_s, mb_s, c_s, nn_s, m_s, conv_s = _mixers(
        gdn_s.reshape(bs, ts, GDN_W), ml_s.reshape(bs, ts, MLP_W), gate_s.reshape(bs, ts, LANE), gt_s, ng_a, ng_b,
        L=ls, G=gs, cw=cw,
        state=(state_conv[0], state_gdn[0], state_mlstm_c[0], state_mlstm_n[0], state_mlstm_m[0].reshape(bs, 1, H_B)))
    half = H_A * DV_A

    rw = jnp.zeros((D_MODEL, LANE), F32).at[:, :N_EXPERTS].set(router_w[0])
    rw_hi = rw.astype(BF16)
    rw = jnp.stack([rw_hi, (rw - rw_hi.astype(F32)).astype(BF16)])
    rb = jnp.full((1, LANE), NEG, F32).at[0, :N_EXPERTS].set(router_b[0])
    x1, x_sorted, info, seg_len = _outproj(xp, xs, ma_p.reshape(n_p, half), ma_s.reshape(n_s, half),
                                           mb_p.reshape(n_p, half), mb_s.reshape(n_s, half),
                                           w_out[0].astype(BF16), norm_moe_g[0].reshape(1, D_MODEL), rw, rb, MOE_TM)

    nt = n // MOE_TM
    nb = -(-(n * TOP_K + nt * N_EXPERTS * (SEG_ALIGN - 1)) // MOE_BLK) + N_EXPERTS
    tables, tile_tables = _block_tables(seg_len[:, :, 0].astype(jnp.int32), nb)
    y_blocks = _experts(tables, x_sorted, expert_w_gu[0], expert_b_gu[0].reshape(N_EXPERTS, 1, 2 * D_FF),
                        expert_w_down[0], expert_b_down[0].reshape(N_EXPERTS, 1, D_MODEL), nt)
    y_p, y_s = _combine(tile_tables, y_blocks, info, x1, p_prompt[0].reshape(n_p, PLE_DIM),
                        p_sample[0].reshape(n_s, PLE_DIM), norm_ple_g[0].reshape(1, D_MODEL),
                        ple_gate_w[0].astype(BF16), ple_w[0].astype(BF16), final_norm_g.reshape(1, D_MODEL), MOE_TM)

    return (y_p.reshape(bp, tp, D_MODEL), y_s.reshape(bs, ts, D_MODEL),
            conv_p[None], gdn_st_p[None], c_p[None], nn_p[None], m_p.reshape(1, bp, H_B),
            conv_s[None], gdn_st_s[None], c_s[None], nn_s[None], m_s.reshape(1, bs, H_B))
```

```python
import functools

import numpy as np
import jax
import jax.numpy as jnp
from jax import lax
from jax.experimental import pallas as pl
from jax.experimental.pallas import tpu as pltpu

F32 = jnp.float32
BF16 = jnp.bfloat16

D_MODEL = 1024
H_A, DK_A, DV_A = 4, 128, 128
H_B, DK_B, DV_B = 4, 64, 128
CONV_W = 4
CONV_CH = H_A * (2 * DK_A + DV_A)
N_EXPERTS = 32
TOP_K = 4
D_FF = 1024
SWIGLU_LIMIT = 7.0
SWIGLU_ALPHA = 1.702
PLE_DIM = 256
EPS = 1e-6
NEG = -1e30
CHUNK = 64

LANE = 128
SUBLANE = 8
GDN_W = CONV_CH + H_A * DV_A
MLP_W = 2 * H_B * DK_B + 2 * H_B * DV_B
N_GATE = 16
PROJ_CHUNK = 512

VMEM_LIMIT = 48 * 1024 * 1024

MOE_TM = 256
MOE_BLK = 512
MOE_DTYPE = F32
SEG_ALIGN = SUBLANE
MOE_CAP = -(-(MOE_TM * TOP_K + N_EXPERTS * (SEG_ALIGN - 1)) // LANE) * LANE

HI = lax.Precision.HIGHEST

_NN = (((1,), (0,)), ((), ()))
_NT = (((1,), (1,)), ((), ()))
_TN = (((0,), (0,)), ((), ()))


def _dot(a, b, dims=_NN):
    return lax.dot_general(a.astype(BF16), b.astype(BF16), dims, preferred_element_type=F32)


def _split3(x):
    hi = x.astype(BF16)
    r = x - hi.astype(F32)
    mid = r.astype(BF16)
    return hi, mid, (r - mid.astype(F32)).astype(BF16)


def _chunk_cumsums(gact, gact_t, tril, triu, cache, key):
    if key not in cache:
        tril_b, triu_b = tril.astype(BF16), triu.astype(BF16)
        cum_c = sum(jnp.dot(tril_b, part, preferred_element_type=F32) for part in _split3(gact))
        cum_r = sum(jnp.dot(part, triu_b, preferred_element_type=F32) for part in _split3(gact_t))
        cache[key] = (cum_c, cum_r)
    return cache[key]


def _rms(x, g):
    return x * lax.rsqrt(jnp.mean(x * x, axis=-1, keepdims=True) + EPS) * g


def _softplus(t):
    return jnp.maximum(t, 0.0) + jnp.log1p(jnp.exp(-jnp.abs(t)))


def _sigmoid(t):
    return 1.0 / (1.0 + jnp.exp(-t))


def _silu(t):
    return t * _sigmoid(t)


def _activate_gates(raw, idx, alog, bias):
    t = raw + bias
    g = -jnp.exp(alog) * _softplus(t)
    beta = _sigmoid(t)
    lf = -_softplus(-t)
    return jnp.where(idx < 4, g, jnp.where(idx < 8, beta, jnp.where(idx < 12, t, lf)))


def _two_segment_specs(tm, width, n_p_tiles):
    return [pl.BlockSpec((tm, width), lambda i, *_: (jnp.minimum(i, n_p_tiles - 1), 0)),
            pl.BlockSpec((tm, width), lambda i, *_: (jnp.maximum(i - n_p_tiles, 0), 0))]


def _for_segment(n_p_tiles, body):
    i = pl.program_id(0)

    @pl.when(i < n_p_tiles)
    def _():
        body(0)

    @pl.when(i >= n_p_tiles)
    def _():
        body(1)


def _gdn_preactivate_stages(raw_ref, xc_ref, cw_ref, out_ref, cnew_ref, first_of_seq):
    tm = raw_ref.shape[0]
    xc_ref[0:SUBLANE, :] = jnp.where(first_of_seq, 0.0, xc_ref[tm:tm + SUBLANE, :])
    xc_ref[SUBLANE:SUBLANE + tm, :] = raw_ref[:, :CONV_CH]
    out_ref[:, CONV_CH:] = _silu(raw_ref[:, CONV_CH:])
    cnew_ref[0] = xc_ref[SUBLANE + tm - (CONV_W - 1):SUBLANE + tm, :]
    yield
    base = SUBLANE - (CONV_W - 1)
    for c0 in range(0, CONV_CH, DK_A):
        conv = xc_ref[base:base + tm, c0:c0 + DK_A] * cw_ref[0:1, c0:c0 + DK_A]
        for j in range(1, CONV_W):
            conv = conv + xc_ref[base + j:base + j + tm, c0:c0 + DK_A] * cw_ref[j:j + 1, c0:c0 + DK_A]
        act = _silu(conv)
        if c0 < H_A * DK_A:
            act = act * (lax.rsqrt(jnp.sum(act * act, axis=-1, keepdims=True) + EPS) * (DK_A ** -0.5))
        elif c0 < 2 * H_A * DK_A:
            act = act * lax.rsqrt(jnp.sum(act * act, axis=-1, keepdims=True) + EPS)
        out_ref[:, c0:c0 + DK_A] = act
        yield


def _inproj_kernel(xp_ref, xs_ref, g_ref, wa_ref, wb_ref, wst_ref, pc_ref, pr_ref, cw_ref,
                   gdnp_ref, gdns_ref, mlp_ref, mls_ref, gatep_ref, gates_ref, gatetp_ref, gatets_ref, cnew_ref,
                   xc_ref, raw_ref, hn_ref, *, n_p_tiles, tiles_per_seq):
    tm = xp_ref.shape[0]
    i = pl.program_id(0)

    @pl.when(i == 0)
    def _():
        xc_ref[...] = jnp.zeros_like(xc_ref)
        raw_ref[...] = jnp.zeros_like(raw_ref)

    def preactivate_previous_tile():
        return _gdn_preactivate_stages(raw_ref, xc_ref, cw_ref, gdnp_ref, cnew_ref, (i - 1) % tiles_per_seq == 0)

    def projection_stages(seg):
        x_ref = (xp_ref, xs_ref)[seg]
        gdn_dst, ml_ref = (raw_ref, gdns_ref)[seg], (mlp_ref, mls_ref)[seg]
        gate_ref, gatet_ref = (gatep_ref, gates_ref)[seg], (gatetp_ref, gatets_ref)[seg]
        hn_ref[...] = _rms(x_ref[...], g_ref[...]).astype(BF16)
        yield
        for c0 in range(0, GDN_W, PROJ_CHUNK):
            gdn_dst[:, c0:c0 + PROJ_CHUNK] = jnp.dot(hn_ref[...], wa_ref[:, c0:c0 + PROJ_CHUNK],
                                                     preferred_element_type=F32)
            yield
        for c0 in range(0, MLP_W, PROJ_CHUNK):
            ml_ref[:, c0:c0 + PROJ_CHUNK] = jnp.dot(hn_ref[...], wb_ref[:, c0:c0 + PROJ_CHUNK],
                                                    preferred_element_type=F32)
            yield
        hn = hn_ref[...]
        raw = jnp.dot(hn, wb_ref[:, MLP_W:], preferred_element_type=F32)
        lane = lax.broadcasted_iota(jnp.int32, (tm, LANE), 1)
        gate_ref[...] = _activate_gates(raw, lane, pc_ref[0:1, :], pc_ref[1:2, :])
        raw_t = lax.dot_general(wst_ref[...], hn, _NT, preferred_element_type=F32)
        row = lax.broadcasted_iota(jnp.int32, (N_GATE, tm), 0)
        gatet_ref[...] = _activate_gates(raw_t, row, pr_ref[:, 0:1], pr_ref[:, 1:2])

    def body(seg):
        if seg == 0:
            _run_interleaved(preactivate_previous_tile(), projection_stages(0))
        else:
            pl.when(i == n_p_tiles)(lambda: _run_interleaved(preactivate_previous_tile()))
            _run_interleaved(projection_stages(1))

    _for_segment(n_p_tiles, body)


def _inproj(xp, xs, g, w_gdn, w_rest, ws_t, pcol, prow, cw, tm, n_seq_p):
    n_p, n_s = xp.shape[0], xs.shape[0]
    npt = n_p // tm
    tiles_per_seq = npt // n_seq_p

    def out2(width):
        return _two_segment_specs(tm, width, npt)

    def shp2(width):
        return [jax.ShapeDtypeStruct((n_p, width), F32), jax.ShapeDtypeStruct((n_s, width), F32)]

    def prev_tile(i):
        return jnp.clip(i - 1, 0, npt - 1)

    return pl.pallas_call(
        functools.partial(_inproj_kernel, n_p_tiles=npt, tiles_per_seq=tiles_per_seq),
        grid=((n_p + n_s) // tm,),
        in_specs=_two_segment_specs(tm, D_MODEL, npt) + [
            pl.BlockSpec((1, D_MODEL), lambda i: (0, 0)),
            pl.BlockSpec((D_MODEL, GDN_W), lambda i: (0, 0)),
            pl.BlockSpec((D_MODEL, MLP_W + LANE), lambda i: (0, 0)),
            pl.BlockSpec((N_GATE, D_MODEL), lambda i: (0, 0)),
            pl.BlockSpec((SUBLANE, LANE), lambda i: (0, 0)),
            pl.BlockSpec((N_GATE, LANE), lambda i: (0, 0)),
            pl.BlockSpec((SUBLANE, CONV_CH), lambda i: (0, 0)),
        ],
        out_specs=[
            pl.BlockSpec((tm, GDN_W), lambda i: (prev_tile(i), 0)),
            pl.BlockSpec((tm, GDN_W), lambda i: (jnp.maximum(i - npt, 0), 0)),
        ] + out2(MLP_W) + out2(LANE) + [
            pl.BlockSpec((N_GATE, tm), lambda i: (0, jnp.minimum(i, npt - 1))),
            pl.BlockSpec((N_GATE, tm), lambda i: (0, jnp.maximum(i - npt, 0))),
            pl.BlockSpec((1, CONV_W - 1, CONV_CH), lambda i: (prev_tile(i) // tiles_per_seq, 0, 0)),
        ],
        out_shape=shp2(GDN_W) + shp2(MLP_W) + shp2(LANE) + [
            jax.ShapeDtypeStruct((N_GATE, n_p), F32), jax.ShapeDtypeStruct((N_GATE, n_s), F32),
            jax.ShapeDtypeStruct((n_seq_p, CONV_W - 1, CONV_CH), F32)],
        scratch_shapes=[pltpu.VMEM((tm + SUBLANE, CONV_CH), F32), pltpu.VMEM((tm, GDN_W), F32),
                        pltpu.VMEM((tm, D_MODEL), BF16)],
        compiler_params=pltpu.CompilerParams(dimension_semantics=("arbitrary",), vmem_limit_bytes=VMEM_LIMIT),
        name="inproj",
    )(xp, xs, g, w_gdn, w_rest, ws_t, pcol, prow, cw)


def _chunk_masks(L):
    ri = lax.broadcasted_iota(jnp.int32, (L, L), 0)
    ci = lax.broadcasted_iota(jnp.int32, (L, L), 1)
    return ri >= ci, ri > ci, ri <= ci


def _run_interleaved(*stage_generators):
    live = list(stage_generators)
    while live:
        for gen in list(live):
            if next(gen, StopIteration) is StopIteration:
                live.remove(gen)


def _gdn_stages(*refs, L, G, has_state, cumsum_cache):
    if has_state:
        (xin_ref, gate_ref, gatet_ref, cw_ref, ng_ref, cst_ref, s0_ref,
         mix_ref, cnew_ref, snew_ref, xc_ref, s_ref) = refs
    else:
        xin_ref, gate_ref, gatet_ref, ng_ref, mix_ref, snew_ref, s_ref = refs
    c = pl.program_id(1)

    @pl.when(c == 0)
    def _():
        if has_state:
            xc_ref[:, 0:SUBLANE, :] = jnp.zeros((G, SUBLANE, CONV_CH), F32)
            xc_ref[:, SUBLANE - (CONV_W - 1):SUBLANE, :] = cst_ref[...]
            s_ref[...] = s0_ref[...]
        else:
            s_ref[...] = jnp.zeros_like(s_ref)

    if has_state:
        @pl.when(c > 0)
        def _():
            xc_ref[:, 0:SUBLANE, :] = xc_ref[:, L:L + SUBLANE, :]

    yield
    tril, strict, triu = _chunk_masks(L)
    base = SUBLANE - (CONV_W - 1)

    chains = [(g, h) for g in range(G) for h in range(H_A)]
    s_old = [s_ref[g, h] for g, h in chains]
    if has_state:
        for g in range(G):
            xc_ref[g, SUBLANE:SUBLANE + L, :] = xin_ref[g, :, :CONV_CH]

    q, k, v, beta, gc, gl, decay = [], [], [], [], [], [], []
    for g in range(G):
        if has_state:
            conv = xc_ref[g, base:base + L, :] * cw_ref[0:1, :]
            for j in range(1, CONV_W):
                conv = conv + xc_ref[g, base + j:base + j + L, :] * cw_ref[j:j + 1, :]
            cnew_ref[g] = xc_ref[g, SUBLANE + L - (CONV_W - 1):SUBLANE + L, :]
            act = _silu(conv)
        else:
            act = xin_ref[g, :, :CONV_CH]
        gact = gate_ref[g]
        cum_c, cum_r = _chunk_cumsums(gact, gatet_ref[g, 0], tril, triu, cumsum_cache, g)
        for h in range(H_A):
            q.append(act[:, h * DK_A:(h + 1) * DK_A])
            k.append(act[:, H_A * DK_A + h * DK_A:H_A * DK_A + (h + 1) * DK_A])
            v.append(act[:, 2 * H_A * DK_A + h * DV_A:2 * H_A * DK_A + (h + 1) * DV_A])
            beta.append(gact[:, 4 + h:5 + h])
            gc.append(cum_c[:, h:h + 1])
            gl.append(cum_c[L - 1:L, h:h + 1])
            gr = cum_r[h:h + 1, :]
            decay.append(jnp.where(tril, jnp.exp(jnp.where(tril, cum_c[:, h:h + 1] - gr, 0.0)), 0.0))
        yield

    nc = range(len(chains))
    if has_state:
        qss = [jnp.sum(q[i] * q[i], axis=-1, keepdims=True) for i in nc]
        kss = [jnp.sum(k[i] * k[i], axis=-1, keepdims=True) for i in nc]
        q = [q[i] * (lax.rsqrt(qss[i] + EPS) * (DK_A ** -0.5)) for i in nc]
        k = [k[i] * lax.rsqrt(kss[i] + EPS) for i in nc]
    kb = [k[i] * beta[i] for i in nc]
    egc = [jnp.exp(gc[i]) for i in nc]
    yield
    kk = [_dot(kb[i], k[i], _NT) for i in nc]
    yield
    qk = [_dot(q[i], k[i], _NT) for i in nc]
    yield
    eye = (lax.broadcasted_iota(jnp.int32, (L, L), 0) == lax.broadcasted_iota(jnp.int32, (L, L), 1)).astype(F32)
    pw = [-jnp.where(strict, kk[i] * decay[i], 0.0) for i in nc]
    t_inv = [eye + pw[i] for i in nc]
    span = 2
    while span < L:
        yield
        pw = [_dot(pw[i], pw[i]) for i in nc]
        yield
        t_inv = [t_inv[i] + _dot(t_inv[i], pw[i]) for i in nc]
        span *= 2
    yield
    sol = [_dot(t_inv[i], jnp.concatenate([v[i] * beta[i], kb[i] * egc[i]], axis=-1)) for i in nc]
    yield
    qs = [_dot(q[i] * egc[i], s_old[i]) for i in nc]
    yield
    ws = [_dot(sol[i][:, DV_A:], s_old[i]) for i in nc]
    v_new = [sol[i][:, :DV_A] - ws[i] for i in nc]
    yield
    o = [qs[i] + _dot(jnp.where(tril, qk[i] * decay[i], 0.0), v_new[i]) for i in nc]
    yield
    s_new = [s_old[i] * jnp.exp(gl[i]) + _dot(k[i] * jnp.exp(gl[i] - gc[i]), v_new[i], _TN) for i in nc]
    yield
    ms = [jnp.mean(o[i] * o[i], axis=-1, keepdims=True) for i in nc]
    on = [o[i] * lax.rsqrt(ms[i] + EPS) for i in nc]
    yield
    for i, (g, h) in enumerate(chains):
        z = xin_ref[g, :, CONV_CH + h * DV_A:CONV_CH + (h + 1) * DV_A]
        out = on[i] * ng_ref[...] * (_silu(z) if has_state else z)
        mix_ref[g, :, h * DV_A:(h + 1) * DV_A] = out.astype(mix_ref.dtype)
    yield
    for i, (g, h) in enumerate(chains):
        s_ref[g, h] = s_new[i]
        snew_ref[g, h] = s_new[i]


def _mlstm_stages(*refs, L, G, has_state, cumsum_cache):
    if has_state:
        (xin_ref, gate_ref, gatet_ref, ng_ref, c0_ref, n0_ref, m0_ref,
         mix_ref, cnew_ref, nnew_ref, mnew_ref, c_ref, n_ref, m_ref) = refs
    else:
        (xin_ref, gate_ref, gatet_ref, ng_ref,
         mix_ref, cnew_ref, nnew_ref, mnew_ref, c_ref, n_ref, m_ref) = refs
    c = pl.program_id(1)

    @pl.when(c == 0)
    def _():
        c_ref[...] = jnp.zeros_like(c_ref)
        n_ref[...] = jnp.zeros_like(n_ref)
        m_ref[...] = jnp.zeros_like(m_ref)
        if has_state:
            for h in range(H_B):
                off = (h % 2) * DK_B
                c_ref[:, h, off:off + DK_B, :] = c0_ref[:, h]
                n_ref[:, h:h + 1, off:off + DK_B] = n0_ref[:, h:h + 1, :]
            m_ref[:, 0:1, 0:H_B] = m0_ref[...]

    yield
    tril, _, triu = _chunk_masks(L)

    chains = [(g, h) for g in range(G) for h in range(H_B)]
    nc = range(len(chains))
    c_old = [c_ref[g, h] for g, h in chains]
    n_old = [n_ref[g, h:h + 1, :] for g, h in chains]
    m_old = [m_ref[g, 0:1, h:h + 1] for g, h in chains]

    k0, v0 = H_B * DK_B, 2 * H_B * DK_B
    low_half = lax.broadcasted_iota(jnp.int32, (L, LANE), 1) < DK_B

    def own_lanes(pair, h):
        return jnp.where(low_half if h % 2 == 0 else jnp.logical_not(low_half), pair, 0.0)

    q = [own_lanes(xin_ref[g, :, (h // 2) * LANE:(h // 2 + 1) * LANE], h) * (DK_B ** -0.5) for g, h in chains]
    k = [own_lanes(xin_ref[g, :, k0 + (h // 2) * LANE:k0 + (h // 2 + 1) * LANE], h) for g, h in chains]
    v = [xin_ref[g, :, v0 + h * DV_B:v0 + (h + 1) * DV_B] for g, h in chains]
    ig_c, b_c, b_last, d_log = [], [], [], []
    for g in range(G):
        gact = gate_ref[g]
        gact_t = gatet_ref[g, 0]
        cum_c, cum_r = _chunk_cumsums(gact, gact_t, tril, triu, cumsum_cache, g)
        for h in range(H_B):
            ig_c.append(gact[:, 8 + h:9 + h])
            b_c.append(cum_c[:, 12 + h:13 + h])
            b_last.append(cum_c[L - 1:L, 12 + h:13 + h])
            d_log.append(jnp.where(tril, cum_c[:, 12 + h:13 + h] - cum_r[12 + h:13 + h, :]
                                   + gact_t[8 + h:9 + h, :], NEG))
        yield
    qk = [_dot(q[i], k[i], _NT) for i in nc]
    yield
    qc = [_dot(q[i], c_old[i]) for i in nc]
    yield
    inter = [b_c[i] + m_old[i] for i in nc]
    m_t = [jnp.maximum(inter[i], jnp.max(d_log[i], axis=-1, keepdims=True)) for i in nc]
    yield
    s = [qk[i] * jnp.exp(d_log[i] - m_t[i]) for i in nc]
    e_inter = [jnp.exp(inter[i] - m_t[i]) for i in nc]
    yield
    sv = [_dot(s[i], v[i]) for i in nc]
    yield
    m_new = [m_t[i][L - 1:L, :] for i in nc]
    kw = [k[i] * jnp.exp(b_last[i] - b_c[i] + ig_c[i] - m_new[i]) for i in nc]
    f_tot = [jnp.exp(b_last[i] + m_old[i] - m_new[i]) for i in nc]
    yield
    c_new = [f_tot[i] * c_old[i] + _dot(kw[i], v[i], _TN) for i in nc]
    yield
    n_new = [f_tot[i] * n_old[i] + jnp.sum(kw[i], axis=0, keepdims=True) for i in nc]
    qn = [jnp.sum(q[i] * n_old[i], axis=-1, keepdims=True) for i in nc]
    yield
    ssum = [jnp.sum(s[i], axis=-1, keepdims=True) for i in nc]
    yield
    den = [jnp.maximum(jnp.abs(e_inter[i] * qn[i] + ssum[i]), jnp.exp(-m_t[i])) for i in nc]
    hh = [(e_inter[i] * qc[i] + sv[i]) / den[i] for i in nc]
    yield
    ms = [jnp.mean(hh[i] * hh[i], axis=-1, keepdims=True) for i in nc]
    hn = [hh[i] * lax.rsqrt(ms[i] + EPS) for i in nc]
    yield
    for i, (g, h) in enumerate(chains):
        og = xin_ref[g, :, v0 + H_B * DV_B + h * DV_B:v0 + H_B * DV_B + (h + 1) * DV_B]
        mix_ref[g, :, h * DV_B:(h + 1) * DV_B] = (hn[i] * ng_ref[h:h + 1, :] * _sigmoid(og)).astype(mix_ref.dtype)
    yield
    for i, (g, h) in enumerate(chains):
        c_ref[g, h] = c_new[i]
        n_ref[g, h:h + 1, :] = n_new[i]
        m_ref[g, 0:1, h:h + 1] = m_new[i]
        off = (h % 2) * DK_B
        cnew_ref[g, h] = c_new[i][off:off + DK_B, :]
        nnew_ref[g, h:h + 1, :] = n_new[i][:, off:off + DK_B]
        mnew_ref[g, 0:1, h:h + 1] = m_new[i]


def _mixers_kernel(*refs, L, G, has_state):
    if has_state:
        (gdn_ref, gate_ref, gatet_ref, nga_ref, ml_ref, ngb_ref, cw_ref, cst_ref, s0_ref, c0_ref, n0_ref, m0_ref,
         mixa_ref, snew_ref, mixb_ref, cnew_ref, nnew_ref, mnew_ref, convnew_ref,
         s_ref, c_ref, n_ref, m_ref, xc_ref) = refs
        gdn_refs = (gdn_ref, gate_ref, gatet_ref, cw_ref, nga_ref, cst_ref, s0_ref,
                    mixa_ref, convnew_ref, snew_ref, xc_ref, s_ref)
        ml_refs = (ml_ref, gate_ref, gatet_ref, ngb_ref, c0_ref, n0_ref, m0_ref,
                   mixb_ref, cnew_ref, nnew_ref, mnew_ref, c_ref, n_ref, m_ref)
    else:
        (gdn_ref, gate_ref, gatet_ref, nga_ref, ml_ref, ngb_ref,
         mixa_ref, snew_ref, mixb_ref, cnew_ref, nnew_ref, mnew_ref, s_ref, c_ref, n_ref, m_ref) = refs
        gdn_refs = (gdn_ref, gate_ref, gatet_ref, nga_ref, mixa_ref, snew_ref, s_ref)
        ml_refs = (ml_ref, gate_ref, gatet_ref, ngb_ref, mixb_ref, cnew_ref, nnew_ref, mnew_ref, c_ref, n_ref, m_ref)
    cumsum_cache = {}
    _run_interleaved(_gdn_stages(*gdn_refs, L=L, G=G, has_state=has_state, cumsum_cache=cumsum_cache),
                     _mlstm_stages(*ml_refs, L=L, G=G, has_state=has_state, cumsum_cache=cumsum_cache))


def _mixers(gdn_in, ml_in, gates, gates_t, ng_a, ng_b, *, L, G, cw=None, state=None):
    n_seq, T, _ = gdn_in.shape
    n_c = T // L
    has_state = state is not None

    def seq_blk(*tail):
        return pl.BlockSpec((G,) + tail, lambda b, c: (b,) + (0,) * len(tail))

    def tok_blk(width):
        return pl.BlockSpec((G, L, width), lambda b, c: (b, c, 0))

    def seq_shape(*tail):
        return jax.ShapeDtypeStruct((n_seq,) + tail, F32)

    state_specs = [seq_blk(H_A, DK_A, DV_A), seq_blk(H_B, DK_B, DV_B), seq_blk(H_B, DK_B), seq_blk(1, H_B)]
    state_shapes = [seq_shape(H_A, DK_A, DV_A), seq_shape(H_B, DK_B, DV_B), seq_shape(H_B, DK_B), seq_shape(1, H_B)]
    conv_spec, conv_shape = seq_blk(CONV_W - 1, CONV_CH), seq_shape(CONV_W - 1, CONV_CH)
    in_specs = [
        tok_blk(GDN_W), tok_blk(LANE),
        pl.BlockSpec((G, 1, N_GATE, L), lambda b, c: (b, c, 0, 0)),
        pl.BlockSpec((1, DV_A), lambda b, c: (0, 0)),
        tok_blk(MLP_W),
        pl.BlockSpec((H_B, DV_B), lambda b, c: (0, 0)),
    ]
    args = [gdn_in, gates, gates_t, ng_a, ml_in, ng_b]
    out_specs = [tok_blk(H_A * DV_A), state_specs[0], tok_blk(H_B * DV_B)] + state_specs[1:]
    mix_dtype = BF16 if L % (2 * SUBLANE) == 0 else F32
    mix_a = jax.ShapeDtypeStruct((n_seq, T, H_A * DV_A), mix_dtype)
    mix_b = jax.ShapeDtypeStruct((n_seq, T, H_B * DV_B), mix_dtype)
    out_shape = [mix_a, state_shapes[0], mix_b] + state_shapes[1:]
    scratch = [pltpu.VMEM((G, H_A, DK_A, DV_A), F32), pltpu.VMEM((G, H_B, LANE, DV_B), F32),
               pltpu.VMEM((G, SUBLANE, LANE), F32), pltpu.VMEM((G, SUBLANE, LANE), F32)]
    if has_state:
        in_specs += [pl.BlockSpec((SUBLANE, CONV_CH), lambda b, c: (0, 0)), conv_spec] + state_specs
        args += [cw] + list(state)
        out_specs.append(conv_spec)
        out_shape.append(conv_shape)
        scratch.append(pltpu.VMEM((G, L + SUBLANE, CONV_CH), F32))
    return pl.pallas_call(
        functools.partial(_mixers_kernel, L=L, G=G, has_state=has_state),
        grid=(n_seq // G, n_c),
        in_specs=in_specs,
        out_specs=out_specs,
        out_shape=out_shape,
        scratch_shapes=scratch,
        compiler_params=pltpu.CompilerParams(dimension_semantics=("parallel", "arbitrary"),
                                             vmem_limit_bytes=VMEM_LIMIT),
        name=f"mixers_L{L}",
    )(*args)


def _outproj_kernel(xp_ref, xs_ref, map_ref, mas_ref, mbp_ref, mbs_ref, wo_ref, g_ref, rw_ref, rb_ref,
                    x1_ref, xsort_ref, info_ref, cpad_ref, *, n_p_tiles):
    half = H_A * DV_A
    tm = xp_ref.shape[0]

    def body(seg):
        x_ref, ma_ref, mb_ref = (xp_ref, xs_ref)[seg], (map_ref, mas_ref)[seg], (mbp_ref, mbs_ref)[seg]
        x1 = (x_ref[...] + jnp.dot(ma_ref[...].astype(BF16), wo_ref[:half, :], preferred_element_type=F32)
              + jnp.dot(mb_ref[...].astype(BF16), wo_ref[half:, :], preferred_element_type=F32))
        x1_ref[...] = x1
        hn = _rms(x1, g_ref[...])
        hn_hi = hn.astype(BF16)
        hn_lo = (hn - hn_hi.astype(F32)).astype(BF16)
        logits = (jnp.dot(hn_hi, rw_ref[0], preferred_element_type=F32)
                  + jnp.dot(hn_hi, rw_ref[1], preferred_element_type=F32)
                  + jnp.dot(hn_lo, rw_ref[0], preferred_element_type=F32)) + rb_ref[...]

        vals = logits.T[:N_EXPERTS, :]
        e_iota = lax.broadcasted_iota(jnp.int32, (N_EXPERTS, tm), 0)
        sels, tops = [], []
        for _ in range(TOP_K):
            m = jnp.max(vals, axis=0, keepdims=True)
            first = jnp.min(jnp.where(vals == m, e_iota, N_EXPERTS), axis=0, keepdims=True)
            sel = e_iota == first
            vals = jnp.where(sel, -jnp.inf, vals)
            sels.append(sel)
            tops.append(m)
        ex = [jnp.exp(t - tops[0]) for t in tops]
        den = ex[0] + ex[1] + ex[2] + ex[3]
        gates = [e / den for e in ex]
        mask = sels[0].astype(F32) + sels[1].astype(F32) + sels[2].astype(F32) + sels[3].astype(F32)
        ri = lax.broadcasted_iota(jnp.int32, (tm, tm), 0)
        ci = lax.broadcasted_iota(jnp.int32, (tm, tm), 1)
        rank = _dot(mask, (ri < ci).astype(F32))
        cnt = jnp.sum(mask, axis=1, keepdims=True)
        cpad = jnp.ceil(cnt * (1.0 / SEG_ALIGN)) * SEG_ALIGN
        cpad_b = jnp.broadcast_to(cpad, (N_EXPERTS, tm))
        er = lax.broadcasted_iota(jnp.int32, (N_EXPERTS, N_EXPERTS), 0)
        ec = lax.broadcasted_iota(jnp.int32, (N_EXPERTS, N_EXPERTS), 1)
        seg_off = _dot((er > ec).astype(F32), cpad_b)
        pos = seg_off + rank
        q = [jnp.sum(jnp.where(s, pos, 0.0), axis=0, keepdims=True) for s in sels]

        j_iota = lax.broadcasted_iota(jnp.int32, (MOE_CAP, tm), 0).astype(F32)
        perm = jnp.zeros((MOE_CAP, tm), F32)
        for kk in range(TOP_K):
            perm = jnp.where(j_iota == q[kk], 1.0, perm)
        xsorted = _dot(perm, hn)
        xsort_ref[...] = xsorted.astype(MOE_DTYPE)

        r_iota = lax.broadcasted_iota(jnp.int32, (LANE, tm), 0)
        info = jnp.zeros((LANE, tm), F32)
        for kk in range(TOP_K):
            info = jnp.where(r_iota == kk, q[kk], info)
            info = jnp.where(r_iota == TOP_K + kk, gates[kk], info)
        info_ref[...] = info.T
        cpad_ref[0] = cpad_b[:, :LANE]

    _for_segment(n_p_tiles, body)


def _outproj(xp, xs, ma_p, ma_s, mb_p, mb_s, w_out, g, rw, rb, tm):
    n_p, n_s = xp.shape[0], xs.shape[0]
    n = n_p + n_s
    nt = n // tm
    npt = n_p // tm
    half = H_A * DV_A
    return pl.pallas_call(
        functools.partial(_outproj_kernel, n_p_tiles=npt),
        grid=(nt,),
        in_specs=_two_segment_specs(tm, D_MODEL, npt) + _two_segment_specs(tm, half, npt)
        + _two_segment_specs(tm, half, npt) + [
            pl.BlockSpec((D_MODEL, D_MODEL), lambda i: (0, 0)),
            pl.BlockSpec((1, D_MODEL), lambda i: (0, 0)),
            pl.BlockSpec((2, D_MODEL, LANE), lambda i: (0, 0, 0)),
            pl.BlockSpec((1, LANE), lambda i: (0, 0)),
        ],
        out_specs=[
            pl.BlockSpec((tm, D_MODEL), lambda i: (i, 0)),
            pl.BlockSpec((MOE_CAP, D_MODEL), lambda i: (i, 0)),
            pl.BlockSpec((tm, LANE), lambda i: (i, 0)),
            pl.BlockSpec((1, N_EXPERTS, LANE), lambda i: (i, 0, 0)),
        ],
        out_shape=[
            jax.ShapeDtypeStruct((n, D_MODEL), F32),
            jax.ShapeDtypeStruct((nt * MOE_CAP, D_MODEL), MOE_DTYPE),
            jax.ShapeDtypeStruct((n, LANE), F32),
            jax.ShapeDtypeStruct((nt, N_EXPERTS, LANE), F32),
        ],
        compiler_params=pltpu.CompilerParams(dimension_semantics=("arbitrary",), vmem_limit_bytes=VMEM_LIMIT),
        name="outproj",
    )(xp, xs, ma_p, ma_s, mb_p, mb_s, w_out, g, rw, rb)


def _expert_kernel(be_ref, bj_ref, tf_ref, tl_ref, cov_ref, nu_ref, vt_ref, ct_ref, lt_ref, nx_ref, ws_ref,
                   xs_hbm, wgu_hbm, bgu_ref, wd_hbm, bd_ref, y_ref,
                   xbuf, gsem, wgu_st, wd_st, wsem, wgu_bf, wd_bf, *, nt):
    b = pl.program_id(0)
    n_used = nu_ref[0]
    slot = b % 2

    def start_pieces(bb, copy, s):
        e = be_ref[bb]
        base = bj_ref[bb] * MOE_BLK

        def body(t, carry):
            k = e * nt + t
            lo = jnp.maximum(vt_ref[k], base)
            ln = jnp.minimum(ct_ref[k], base + MOE_BLK) - lo

            @pl.when(ln > 0)
            def _():
                copy(s, pl.multiple_of(lt_ref[k] + lo, SEG_ALIGN), pl.multiple_of(lo - base, SEG_ALIGN),
                     pl.multiple_of(ln, SEG_ALIGN)).start()
            return carry

        lax.fori_loop(tf_ref[bb], tl_ref[bb] + 1, body, 0)

    def weight_copies(e):
        return (pltpu.make_async_copy(wgu_hbm.at[e], wgu_st, wsem.at[0]),
                pltpu.make_async_copy(wd_hbm.at[e], wd_st, wsem.at[1]))

    def cast_weights(p):
        wgu_bf[p] = wgu_st[...].astype(BF16)
        wd_bf[p] = wd_st[...].astype(BF16)

    def gather_copy(s, src, dst, size):
        return pltpu.make_async_copy(xs_hbm.at[pl.ds(src, size)], xbuf.at[s, pl.ds(dst, size)], gsem.at[s])

    def wait_rows(count, copy, s):
        @pl.when(count > 0)
        def _():
            copy(s, 0, 0, pl.multiple_of(count, SEG_ALIGN)).wait()

    @pl.when(b == 0)
    def _():
        xbuf[...] = jnp.zeros_like(xbuf)
        start_pieces(0, gather_copy, 0)

    @pl.when(b + 1 < n_used)
    def _():
        start_pieces(b + 1, gather_copy, 1 - slot)

    @pl.when(b >= n_used)
    def _():
        y_ref[...] = jnp.zeros_like(y_ref)

    @pl.when(b < n_used)
    def _():
        e = be_ref[b]
        first = jnp.logical_or(b == 0, be_ref[jnp.maximum(b - 1, 0)] != e)
        last = jnp.logical_or(b == n_used - 1, be_ref[jnp.minimum(b + 1, n_used - 1)] != e)
        has_next = nx_ref[b] < N_EXPERTS
        p = ws_ref[b]

        @pl.when(b == 0)
        def _():
            for cp in weight_copies(e):
                cp.start()
            for cp in weight_copies(e):
                cp.wait()
            cast_weights(p)

        @pl.when(jnp.logical_and(first, has_next))
        def _():
            for cp in weight_copies(nx_ref[b]):
                cp.start()

        wait_rows(cov_ref[b], gather_copy, slot)

        def expert_mlp(rows):
            hgu = jnp.dot(xbuf[slot, :rows].astype(BF16), wgu_bf[p], preferred_element_type=F32) + bgu_ref[0]
            gate = jnp.minimum(hgu[:, :D_FF], SWIGLU_LIMIT)
            up = jnp.clip(hgu[:, D_FF:], -SWIGLU_LIMIT, SWIGLU_LIMIT)
            act = (up + 1.0) * gate * _sigmoid(SWIGLU_ALPHA * gate)
            y = jnp.dot(act.astype(BF16), wd_bf[p], preferred_element_type=F32) + bd_ref[0]
            y_ref[:rows] = y.astype(MOE_DTYPE)
            if rows < MOE_BLK:
                y_ref[rows:] = jnp.zeros((MOE_BLK - rows, D_MODEL), MOE_DTYPE)

        quarter = MOE_BLK // 4
        for nq in range(1, 5):
            pl.when(jnp.logical_and(cov_ref[b] > (nq - 1) * quarter, cov_ref[b] <= nq * quarter))(
                functools.partial(expert_mlp, nq * quarter))

        @pl.when(jnp.logical_and(last, has_next))
        def _():
            for cp in weight_copies(nx_ref[b]):
                cp.wait()
            cast_weights(1 - p)


def _experts(tables, xs, w_gu, b_gu, w_down, b_down, nt):
    nb = tables[0].shape[0]

    def bias_blk(b, *t):
        return (t[0][jnp.minimum(b, t[5][0] - 1)], 0, 0)

    def out_blk(b, *t):
        return (b, 0)

    grid_spec = pltpu.PrefetchScalarGridSpec(
        num_scalar_prefetch=len(tables),
        grid=(nb,),
        in_specs=[
            pl.BlockSpec(memory_space=pl.ANY),
            pl.BlockSpec(memory_space=pl.ANY),
            pl.BlockSpec((1, 1, 2 * D_FF), bias_blk),
            pl.BlockSpec(memory_space=pl.ANY),
            pl.BlockSpec((1, 1, D_MODEL), bias_blk),
        ],
        out_specs=pl.BlockSpec((MOE_BLK, D_MODEL), out_blk),
        scratch_shapes=[
            pltpu.VMEM((2, MOE_BLK, D_MODEL), MOE_DTYPE),
            pltpu.SemaphoreType.DMA((2,)),
            pltpu.VMEM((D_MODEL, 2 * D_FF), F32),
            pltpu.VMEM((D_FF, D_MODEL), F32),
            pltpu.SemaphoreType.DMA((2,)),
            pltpu.VMEM((2, D_MODEL, 2 * D_FF), BF16),
            pltpu.VMEM((2, D_FF, D_MODEL), BF16),
        ],
    )
    return pl.pallas_call(
        functools.partial(_expert_kernel, nt=nt),
        grid_spec=grid_spec,
        out_shape=jax.ShapeDtypeStruct((nb * MOE_BLK, D_MODEL), MOE_DTYPE),
        compiler_params=pltpu.CompilerParams(dimension_semantics=("arbitrary",), vmem_limit_bytes=VMEM_LIMIT),
        name="experts",
    )(*tables, xs, w_gu, b_gu, w_down, b_down)


def _combine_kernel(src_ref, len_ref, off_ref, used_ref,
                    y_hbm, info_ref, x1_ref, pp_ref, ps_ref, gple_ref, wg_ref, wp_ref, gfin_ref,
                    outp_ref, outs_ref, ybuf, sem, *, n_p_tiles):
    tm = x1_ref.shape[0]
    i = pl.program_id(0)
    slot = i % 2

    def tile_copy(s, src, dst, size):
        return pltpu.make_async_copy(y_hbm.at[pl.ds(src, size)], ybuf.at[s, pl.ds(dst, size)], sem.at[s])

    def gather_tile(t, s):
        def body(e, carry):
            k = t * N_EXPERTS + e
            ln = len_ref[k]

            @pl.when(ln > 0)
            def _():
                tile_copy(s, pl.multiple_of(src_ref[k], SEG_ALIGN), pl.multiple_of(off_ref[k], SEG_ALIGN),
                          pl.multiple_of(ln, SEG_ALIGN)).start()
            return carry
        lax.fori_loop(0, N_EXPERTS, body, 0)

    @pl.when(i == 0)
    def _():
        ybuf[...] = jnp.zeros_like(ybuf)
        gather_tile(0, 0)

    @pl.when(i + 1 < pl.num_programs(0))
    def _():
        gather_tile(i + 1, 1 - slot)

    tile_copy(slot, 0, 0, pl.multiple_of(used_ref[i], SEG_ALIGN)).wait()
    ys_ref = ybuf.at[slot]

    def body(seg):
        p_ref, out_ref = (pp_ref, ps_ref)[seg], (outp_ref, outs_ref)[seg]
        info = info_ref[...]
        j_iota = lax.broadcasted_iota(jnp.int32, (tm, MOE_CAP), 1).astype(F32)
        gmat = jnp.zeros((tm, MOE_CAP), F32)
        for kk in range(TOP_K):
            gmat = jnp.where(j_iota == info[:, kk:kk + 1], info[:, TOP_K + kk:TOP_K + kk + 1], gmat)
        x2 = x1_ref[...] + jnp.dot(gmat.astype(BF16), ys_ref[...].astype(BF16), preferred_element_type=F32)
        hn = _rms(x2, gple_ref[...]).astype(BF16)
        gate = _sigmoid(jnp.dot(hn, wg_ref[...], preferred_element_type=F32))
        pe = jnp.dot(p_ref[...].astype(BF16), wp_ref[...], preferred_element_type=F32)
        x3 = x2 + gate * pe
        out_ref[...] = _rms(x3, gfin_ref[...])

    _for_segment(n_p_tiles, body)


def _combine(tile_tables, ys, info, x1, pp, ps, g_ple, w_gate, w_p, g_fin, tm):
    n_p, n_s = pp.shape[0], ps.shape[0]
    n = n_p + n_s
    nt = n // tm
    npt = n_p // tm
    grid_spec = pltpu.PrefetchScalarGridSpec(
        num_scalar_prefetch=len(tile_tables),
        grid=(nt,),
        in_specs=[
            pl.BlockSpec(memory_space=pl.ANY),
            pl.BlockSpec((tm, LANE), lambda i, *_: (i, 0)),
            pl.BlockSpec((tm, D_MODEL), lambda i, *_: (i, 0)),
        ] + _two_segment_specs(tm, PLE_DIM, npt) + [
            pl.BlockSpec((1, D_MODEL), lambda i, *_: (0, 0)),
            pl.BlockSpec((D_MODEL, D_MODEL), lambda i, *_: (0, 0)),
            pl.BlockSpec((PLE_DIM, D_MODEL), lambda i, *_: (0, 0)),
            pl.BlockSpec((1, D_MODEL), lambda i, *_: (0, 0)),
        ],
        out_specs=_two_segment_specs(tm, D_MODEL, npt),
        scratch_shapes=[pltpu.VMEM((2, MOE_CAP, D_MODEL), MOE_DTYPE), pltpu.SemaphoreType.DMA((2,))],
    )
    return pl.pallas_call(
        functools.partial(_combine_kernel, n_p_tiles=npt),
        grid_spec=grid_spec,
        out_shape=[jax.ShapeDtypeStruct((n_p, D_MODEL), F32), jax.ShapeDtypeStruct((n_s, D_MODEL), F32)],
        compiler_params=pltpu.CompilerParams(dimension_semantics=("arbitrary",), vmem_limit_bytes=VMEM_LIMIT),
        name="combine",
    )(*tile_tables, ys, info, x1, pp, ps, g_ple, w_gate, w_p, g_fin)


def _block_tables(seg_len, nb):
    nt = seg_len.shape[0]
    seg_off = jnp.cumsum(seg_len, axis=1) - seg_len
    seg_end = jnp.cumsum(seg_len, axis=0).T
    seg_start = seg_end - seg_len.T
    n_rows = seg_end[:, -1]
    n_blk = (n_rows + MOE_BLK - 1) // MOE_BLK
    blk_end = jnp.cumsum(n_blk)
    b = jnp.arange(nb, dtype=jnp.int32)
    block_e = jnp.minimum(jnp.sum((blk_end[None, :] <= b[:, None]).astype(jnp.int32), axis=1), N_EXPERTS - 1)
    idx = jnp.where(n_blk > 0, jnp.arange(N_EXPERTS, dtype=jnp.int32), N_EXPERTS)
    nxt = jnp.concatenate([lax.cummin(idx, axis=0, reverse=True)[1:], jnp.full((1,), N_EXPERTS, jnp.int32)])
    parity = (jnp.cumsum((n_blk > 0).astype(jnp.int32)) - 1) % 2
    per_e = jnp.concatenate([jnp.stack([blk_end - n_blk, n_rows, nxt, parity], axis=1), seg_start, seg_end],
                            axis=1).astype(F32)
    onehot = (block_e[:, None] == jnp.arange(N_EXPERTS, dtype=jnp.int32)[None, :]).astype(F32)
    per_b = jnp.dot(onehot, per_e, precision=HI).astype(jnp.int32)
    block_j = b - per_b[:, 0]
    base = block_j * MOE_BLK
    t_first = jnp.sum((per_b[:, 4 + nt:] <= base[:, None]).astype(jnp.int32), axis=1)
    t_last = jnp.sum((per_b[:, 4:4 + nt] < (base + MOE_BLK)[:, None]).astype(jnp.int32), axis=1) - 1
    cover = jnp.clip(per_b[:, 1] - base, 0, MOE_BLK)
    seg_shift = (jnp.arange(nt, dtype=jnp.int32)[:, None] * MOE_CAP + seg_off).T - seg_start
    tables = (block_e, block_j, t_first, t_last, cover, blk_end[-1:], seg_start.reshape(-1),
              seg_end.reshape(-1), seg_shift.reshape(-1), per_b[:, 2], per_b[:, 3])
    y_src = ((blk_end - n_blk) * MOE_BLK)[None, :] + seg_start.T
    tile_tables = (y_src.reshape(-1), seg_len.reshape(-1), seg_off.reshape(-1), jnp.sum(seg_len, axis=1))
    return tuple(t.astype(jnp.int32) for t in tables), tuple(t.astype(jnp.int32) for t in tile_tables)


def _rearranged_in_weights(w_in):
    o = np.cumsum([0, CONV_CH, H_A * DV_A, H_A, H_A, H_B * DK_B, H_B * DK_B, H_B * DV_B, H_B * DV_B, H_B, H_B])
    conv_in, z_a, a_a, b_a, q_b, k_b, v_b, o_b, i_b, f_b = (w_in[:, int(o[j]):int(o[j + 1])] for j in range(10))
    small = jnp.concatenate([a_a, b_a, i_b, f_b], axis=1)
    w_gdn = w_in[:, :GDN_W]
    w_rest = jnp.concatenate([q_b, k_b, v_b, o_b, small, jnp.zeros((D_MODEL, LANE - N_GATE), w_in.dtype)], axis=1)
    return w_gdn.astype(BF16), w_rest.astype(BF16), small.T.astype(BF16)


def _gate_params(a_log, dt_bias, i_bias, f_bias):
    z4 = jnp.zeros((4,), F32)
    alog = jnp.concatenate([a_log.astype(F32), z4, z4, z4])
    bias = jnp.concatenate([dt_bias.astype(F32), z4, i_bias.astype(F32), f_bias.astype(F32)])
    pad = jnp.zeros((LANE - N_GATE,), F32)
    pcol = jnp.zeros((SUBLANE, LANE), F32).at[0].set(jnp.concatenate([alog, pad])).at[1].set(
        jnp.concatenate([bias, pad]))
    prow = jnp.zeros((N_GATE, LANE), F32).at[:, 0].set(alog).at[:, 1].set(bias)
    return pcol, prow


def kernel(x_prompt, x_sample, p_prompt, p_sample, state_conv, state_gdn, state_mlstm_c, state_mlstm_n, state_mlstm_m, norm_attn_g, w_in, conv_w, gdn_a_log, gdn_dt_bias, gdn_norm_g, mlstm_i_bias, mlstm_f_bias, mlstm_norm_g, w_out, norm_moe_g, router_w, router_b, expert_w_gu, expert_b_gu, expert_w_down, expert_b_down, norm_ple_g, ple_gate_w, ple_w, final_norm_g):
    bp, tp, _ = x_prompt.shape
    bs, ts, _ = x_sample.shape
    n_p, n_s = bp * tp, bs * ts
    n = n_p + n_s
    lp, ls = min(tp, CHUNK), min(ts, CHUNK)
    tm = 256
    gp = 4 if bp % 4 == 0 else 1
    gs = 16 if bs % 16 == 0 else 1
    assert tp % lp == 0 and ts % ls == 0 and tp % tm == 0 and n_s % tm == 0 and ls % SUBLANE == 0

    xp = x_prompt.reshape(n_p, D_MODEL)
    xs = x_sample.reshape(n_s, D_MODEL)

    w_gdn, w_rest, ws_t = _rearranged_in_weights(w_in[0])
    pcol, prow = _gate_params(gdn_a_log[0], gdn_dt_bias[0], mlstm_i_bias[0], mlstm_f_bias[0])
    cw = jnp.zeros((SUBLANE, CONV_CH), F32).at[:CONV_W].set(conv_w[0].astype(F32))
    gdn_p, gdn_s, ml_p, ml_s, gate_p, gate_s, gatet_p, gatet_s, conv_p = _inproj(
        xp, xs, norm_attn_g[0].reshape(1, D_MODEL), w_gdn, w_rest, ws_t, pcol, prow, cw, tm, bp)
    gt_p = gatet_p.reshape(N_GATE, bp, tp // lp, lp).transpose(1, 2, 0, 3)
    gt_s = gatet_s.reshape(N_GATE, bs, ts // ls, ls).transpose(1, 2, 0, 3)

    ng_a = gdn_norm_g[0].reshape(1, DV_A).astype(F32)
    ng_b = mlstm_norm_g[0].reshape(H_B, DV_B).astype(F32)
    ma_p, gdn_st_p, mb_p, c_p, nn_p, m_p = _mixers(
        gdn_p.reshape(bp, tp, GDN_W), ml_p.reshape(bp, tp, MLP_W), gate_p.reshape(bp, tp, LANE), gt_p, ng_a, ng_b,
        L=lp, G=gp)
    ma_s, gdn_st_s, mb_s, c_s, nn_s, m_s, conv_s = _mixers(
        gdn_s.reshape(bs, ts, GDN_W), ml_s.reshape(bs, ts, MLP_W), gate_s.reshape(bs, ts, LANE), gt_s, ng_a, ng_b,
        L=ls, G=gs, cw=cw,
        state=(state_conv[0], state_gdn[0], state_mlstm_c[0], state_mlstm_n[0], state_mlstm_m[0].reshape(bs, 1, H_B)))
    half = H_A * DV_A

    rw = jnp.zeros((D_MODEL, LANE), F32).at[:, :N_EXPERTS].set(router_w[0])
    rw_hi = rw.astype(BF16)
    rw = jnp.stack([rw_hi, (rw - rw_hi.astype(F32)).astype(BF16)])
    rb = jnp.full((1, LANE), NEG, F32).at[0, :N_EXPERTS].set(router_b[0])
    x1, x_sorted, info, seg_len = _outproj(xp, xs, ma_p.reshape(n_p, half), ma_s.reshape(n_s, half),
                                           mb_p.reshape(n_p, half), mb_s.reshape(n_s, half),
                                           w_out[0].astype(BF16), norm_moe_g[0].reshape(1, D_MODEL), rw, rb, MOE_TM)

    nt = n // MOE_TM
    nb = -(-(n * TOP_K + nt * N_EXPERTS * (SEG_ALIGN - 1)) // MOE_BLK) + N_EXPERTS
    tables, tile_tables = _block_tables(seg_len[:, :, 0].astype(jnp.int32), nb)
    y_blocks = _experts(tables, x_sorted, expert_w_gu[0], expert_b_gu[0].reshape(N_EXPERTS, 1, 2 * D_FF),
                        expert_w_down[0], expert_b_down[0].reshape(N_EXPERTS, 1, D_MODEL), nt)
    y_p, y_s = _combine(tile_tables, y_blocks, info, x1, p_prompt[0].reshape(n_p, PLE_DIM),
                        p_sample[0].reshape(n_s, PLE_DIM), norm_ple_g[0].reshape(1, D_MODEL),
                        ple_gate_w[0].astype(BF16), ple_w[0].astype(BF16), final_norm_g.reshape(1, D_MODEL), MOE_TM)

    return (y_p.reshape(bp, tp, D_MODEL), y_s.reshape(bs, ts, D_MODEL),
            conv_p[None], gdn_st_p[None], c_p[None], nn_p[None], m_p.reshape(1, bp, H_B),
            conv_s[None], gdn_st_s[None], c_s[None], nn_s[None], m_s.reshape(1, bs, H_B))
```
